```python
import jax, jax.numpy as jnp
from jax import lax
import numpy as np

D_MODEL = 1024
BATCH = 8
SEQ = 2048
DEPTH = 2

D_MIX = D_MODEL
D_POOL = D_MIX // 2
D_SGU = D_MIX - D_POOL
POOL_WINDOWS = (2, 4, 8, 16)
N_POOL_GROUPS = len(POOL_WINDOWS)
POOL_GROUP = D_POOL // N_POOL_GROUPS
N_SGU_HEADS = 4
SGU_HEAD = D_SGU // N_SGU_HEADS
CHUNK = 128
SPLITS = (D_POOL, 2 * D_POOL, 2 * D_POOL + D_SGU, 2 * D_POOL + 2 * D_SGU)
D_IN_PROJ = 2 * D_POOL + 3 * D_SGU
DEEPNORM_ALPHA = (2.0 * DEPTH) ** 0.25
DEEPNORM_BETA = (8.0 * DEPTH) ** -0.25
LN_EPS = 1e-5

kernel_name = "hybrid_pool_sgu_deepnorm_adaln"


def _layer_norm(x):
    xf = x.astype(jnp.float32)
    mu = jnp.mean(xf, axis=-1, keepdims=True)
    var = jnp.mean(jnp.square(xf - mu), axis=-1, keepdims=True)
    return ((xf - mu) * lax.rsqrt(var + LN_EPS)).astype(x.dtype)


def _pool_mixer(xa, w_pool, pool_scale):
    B, S, _ = xa.shape
    xg = xa.reshape(B, S, N_POOL_GROUPS, POOL_GROUP)
    cs = jnp.cumsum(xg.astype(jnp.float32), axis=1)
    cs = jnp.pad(cs, ((0, 0), (1, 0), (0, 0), (0, 0)))
    t = jnp.arange(S)
    pooled = []
    for g, w in enumerate(POOL_WINDOWS):
        lo = jnp.maximum(t + 1 - w, 0)
        cs_g = cs[:, :, g, :]
        win_sum = cs_g[:, 1:] - cs_g[:, lo]
        count = (t + 1 - lo).astype(jnp.float32)
        pooled.append(win_sum / count[None, :, None])
    pooled = jnp.stack(pooled, axis=2).astype(xa.dtype) - xg
    y = jnp.einsum('bsgc,gcd->bsgd', pooled, w_pool)
    return y.reshape(B, S, D_POOL) * pool_scale


def _sgu_mixer(u, v, ln_g, ln_b, w_s, b_s):
    B, S, _ = u.shape
    n_chunks = S // CHUNK
    v = v.reshape(B, n_chunks, CHUNK, N_SGU_HEADS, SGU_HEAD)
    v = _layer_norm(v) * ln_g + ln_b
    causal = jnp.tril(jnp.ones((CHUNK, CHUNK), dtype=bool))
    w = jnp.where(causal[None], w_s, 0)
    mixed = jnp.einsum('hts,bnshd->bnthd', w, v) + jnp.transpose(b_s)[:, :, None]
    return u * mixed.reshape(B, S, D_SGU)


def _hybrid_layer(x, c, w_ada, b_ada, w_in, w_pool, pool_scale,
                  sgu_ln_g, sgu_ln_b, w_sgu, b_sgu, w_out, ln_g, ln_b):
    mod = jax.nn.silu(c) @ w_ada + b_ada
    shift, scale, gate = jnp.split(mod, 3, axis=-1)
    h = _layer_norm(x) * (1 + scale[:, None]) + shift[:, None]
    proj = h @ w_in
    xa, ga, u, v, gb = jnp.split(proj, SPLITS, axis=-1)
    ya = _pool_mixer(xa, w_pool, pool_scale) * jax.nn.silu(ga)
    yb = _sgu_mixer(jax.nn.gelu(u, approximate=False), jax.nn.gelu(v, approximate=False),
                    sgu_ln_g, sgu_ln_b, w_sgu, b_sgu) * jax.nn.silu(gb)
    y = jnp.concatenate([ya, yb], axis=-1) @ w_out
    z = DEEPNORM_ALPHA * x + gate[:, None] * y
    return _layer_norm(z) * ln_g + ln_b


def _fwd_setup_inputs(seed: int = 0) -> dict:
    key = jax.random.key(seed)
    ks = jax.random.split(key, 16)
    f32 = jnp.float32
    x = jax.random.normal(ks[0], (BATCH, SEQ, D_MODEL), f32)
    c = jax.random.normal(ks[1], (BATCH, D_MODEL), f32)
    w_ada = jax.random.normal(ks[2], (DEPTH, D_MODEL, 3 * D_MODEL), f32) * (0.5 * D_MODEL ** -0.5)
    b_ada = jax.random.normal(ks[3], (DEPTH, 3 * D_MODEL), f32) * 0.02
    w_in = jax.random.normal(ks[4], (DEPTH, D_MODEL, D_IN_PROJ), f32) * D_MODEL ** -0.5
    w_pool = jax.random.normal(ks[5], (DEPTH, N_POOL_GROUPS, POOL_GROUP, POOL_GROUP), f32) * POOL_GROUP ** -0.5
    pool_scale = 1.0 + 0.1 * jax.random.normal(ks[6], (DEPTH, D_POOL), f32)
    sgu_ln_g = 1.0 + 0.1 * jax.random.normal(ks[7], (DEPTH, N_SGU_HEADS, SGU_HEAD), f32)
    sgu_ln_b = 0.02 * jax.random.normal(ks[8], (DEPTH, N_SGU_HEADS, SGU_HEAD), f32)
    w_sgu = jax.random.normal(ks[9], (DEPTH, N_SGU_HEADS, CHUNK, CHUNK), f32) * CHUNK ** -0.5
    b_sgu = 1.0 + 0.1 * jax.random.normal(ks[10], (DEPTH, N_SGU_HEADS, CHUNK), f32)
    w_out = jax.random.normal(ks[11], (DEPTH, D_MIX, D_MODEL), f32) * (D_MIX ** -0.5 * DEEPNORM_BETA)
    ln_g = 1.0 + 0.1 * jax.random.normal(ks[12], (DEPTH, D_MODEL), f32)
    ln_b = 0.02 * jax.random.normal(ks[13], (DEPTH, D_MODEL), f32)
    return {"x": x, "c": c, "w_ada": w_ada, "b_ada": b_ada, "w_in": w_in,
            "w_pool": w_pool, "pool_scale": pool_scale, "sgu_ln_g": sgu_ln_g,
            "sgu_ln_b": sgu_ln_b, "w_sgu": w_sgu, "b_sgu": b_sgu, "w_out": w_out,
            "ln_g": ln_g, "ln_b": ln_b}


def _fwd_reference(x, c, w_ada, b_ada, w_in, w_pool, pool_scale, sgu_ln_g, sgu_ln_b,
              w_sgu, b_sgu, w_out, ln_g, ln_b):
    for l in range(DEPTH):
        x = _hybrid_layer(x, c, w_ada[l], b_ada[l], w_in[l], w_pool[l], pool_scale[l],
                          sgu_ln_g[l], sgu_ln_b[l], w_sgu[l], b_sgu[l], w_out[l],
                          ln_g[l], ln_b[l])
    return x


import jax as _jax
import jax.numpy as _jnp

TWIN_FORMAT = 'train_step'
FWD_PARAMS = ['x', 'c', 'w_ada', 'b_ada', 'w_in', 'w_pool', 'pool_scale', 'sgu_ln_g', 'sgu_ln_b', 'w_sgu', 'b_sgu', 'w_out', 'ln_g', 'ln_b']
TWIN_WEIGHTS = ['w_ada', 'b_ada', 'w_in', 'w_pool', 'pool_scale', 'sgu_ln_g', 'sgu_ln_b', 'w_sgu', 'b_sgu', 'w_out', 'ln_g', 'ln_b']
TWIN_DIFF_INPUT = 'x'
TWIN_INPUTS = ['x', 'c', 'w_ada', 'b_ada', 'w_in', 'w_pool', 'pool_scale', 'sgu_ln_g', 'sgu_ln_b', 'w_sgu', 'b_sgu', 'w_out', 'ln_g', 'ln_b', 'loss_target', 'm_w_ada', 'm_b_ada', 'm_w_in', 'm_w_pool', 'm_pool_scale', 'm_sgu_ln_g', 'm_sgu_ln_b', 'm_w_sgu', 'm_b_sgu', 'm_w_out', 'm_ln_g', 'm_ln_b', 'v_w_ada', 'v_b_ada', 'v_w_in', 'v_w_pool', 'v_pool_scale', 'v_sgu_ln_g', 'v_sgu_ln_b', 'v_w_sgu', 'v_b_sgu', 'v_w_out', 'v_ln_g', 'v_ln_b']
TWIN_OUTPUTS = ['loss', 'grad_x', 'grad_w_ada', 'grad_b_ada', 'grad_w_in', 'grad_w_pool', 'grad_pool_scale', 'grad_sgu_ln_g', 'grad_sgu_ln_b', 'grad_w_sgu', 'grad_b_sgu', 'grad_w_out', 'grad_ln_g', 'grad_ln_b', 'delta_w_ada', 'delta_b_ada', 'delta_w_in', 'delta_w_pool', 'delta_pool_scale', 'delta_sgu_ln_g', 'delta_sgu_ln_b', 'delta_w_sgu', 'delta_b_sgu', 'delta_w_out', 'delta_ln_g', 'delta_ln_b', 'new_m_w_ada', 'new_m_b_ada', 'new_m_w_in', 'new_m_w_pool', 'new_m_pool_scale', 'new_m_sgu_ln_g', 'new_m_sgu_ln_b', 'new_m_w_sgu', 'new_m_b_sgu', 'new_m_w_out', 'new_m_ln_g', 'new_m_ln_b', 'new_v_w_ada', 'new_v_b_ada', 'new_v_w_in', 'new_v_w_pool', 'new_v_pool_scale', 'new_v_sgu_ln_g', 'new_v_sgu_ln_b', 'new_v_w_sgu', 'new_v_b_sgu', 'new_v_w_out', 'new_v_ln_g', 'new_v_ln_b']
TWIN_LEAF_KINDS = {'loss': 'loss', 'grad_x': 'grad_x', 'grad_w_ada': 'grad_w', 'grad_b_ada': 'grad_w', 'grad_w_in': 'grad_w', 'grad_w_pool': 'grad_w', 'grad_pool_scale': 'grad_w', 'grad_sgu_ln_g': 'grad_w', 'grad_sgu_ln_b': 'grad_w', 'grad_w_sgu': 'grad_w', 'grad_b_sgu': 'grad_w', 'grad_w_out': 'grad_w', 'grad_ln_g': 'grad_w', 'grad_ln_b': 'grad_w', 'delta_w_ada': 'delta_w', 'delta_b_ada': 'delta_w', 'delta_w_in': 'delta_w', 'delta_w_pool': 'delta_w', 'delta_pool_scale': 'delta_w', 'delta_sgu_ln_g': 'delta_w', 'delta_sgu_ln_b': 'delta_w', 'delta_w_sgu': 'delta_w', 'delta_b_sgu': 'delta_w', 'delta_w_out': 'delta_w', 'delta_ln_g': 'delta_w', 'delta_ln_b': 'delta_w', 'new_m_w_ada': 'new_m', 'new_m_b_ada': 'new_m', 'new_m_w_in': 'new_m', 'new_m_w_pool': 'new_m', 'new_m_pool_scale': 'new_m', 'new_m_sgu_ln_g': 'new_m', 'new_m_sgu_ln_b': 'new_m', 'new_m_w_sgu': 'new_m', 'new_m_b_sgu': 'new_m', 'new_m_w_out': 'new_m', 'new_m_ln_g': 'new_m', 'new_m_ln_b': 'new_m', 'new_v_w_ada': 'new_v', 'new_v_b_ada': 'new_v', 'new_v_w_in': 'new_v', 'new_v_w_pool': 'new_v', 'new_v_pool_scale': 'new_v', 'new_v_sgu_ln_g': 'new_v', 'new_v_sgu_ln_b': 'new_v', 'new_v_w_sgu': 'new_v', 'new_v_b_sgu': 'new_v', 'new_v_w_out': 'new_v', 'new_v_ln_g': 'new_v', 'new_v_ln_b': 'new_v'}


def _forward(args):
    return _fwd_reference(*[args[k] for k in FWD_PARAMS])


def _output_shape():
    out = _jax.eval_shape(lambda: _forward(_fwd_setup_inputs(0)))
    return out.shape, out.dtype

N_MICROBATCH = 1
ADAM_LR = 0.001
ADAM_B1 = 0.9
ADAM_B2 = 0.999
ADAM_EPS = 1e-08
ADAM_WD = 0.01
ADAM_STEP = 10
PER_EXAMPLE_BATCH_AXIS = {'x': 0, 'c': 0, 'loss_target': 0}
SHARED_INPUTS = []
_WEIGHT_DTYPES = {'w_ada': _jnp.float32, 'b_ada': _jnp.float32, 'w_in': _jnp.float32, 'w_pool': _jnp.float32, 'pool_scale': _jnp.float32, 'sgu_ln_g': _jnp.float32, 'sgu_ln_b': _jnp.float32, 'w_sgu': _jnp.float32, 'b_sgu': _jnp.float32, 'w_out': _jnp.float32, 'ln_g': _jnp.float32, 'ln_b': _jnp.float32}
MOMENT_SCALE = {'w_ada': 1.206592e-02, 'b_ada': 2.027636e-02, 'w_in': 7.920873e-03, 'w_pool': 8.474274e-03, 'pool_scale': 9.133966e-03, 'sgu_ln_g': 4.732930e-03, 'sgu_ln_b': 4.874592e-03, 'w_sgu': 4.692024e-03, 'b_sgu': 6.727247e-03, 'w_out': 1.665893e-02, 'ln_g': 1.178810e+01, 'ln_b': 3.533053e-01}


def _to_microbatches(a, axis):
    t = _jnp.moveaxis(a, axis, 0)
    t = t.reshape((N_MICROBATCH, t.shape[0] // N_MICROBATCH) + t.shape[1:])
    return _jnp.moveaxis(t, 1, axis + 1)


def setup_inputs(seed: int = 0) -> dict:
    inp = _fwd_setup_inputs(seed)
    key = _jax.random.fold_in(_jax.random.key(seed), 7919)
    shape, _ = _output_shape()
    out = dict(inp)
    out["loss_target"] = _jax.random.normal(_jax.random.fold_in(key, 0), shape, _jnp.float32)
    for i, name in enumerate(TWIN_WEIGHTS):
        w = inp[name].astype(_jnp.float32)
        if MOMENT_SCALE is None:
            s = _jnp.sqrt(_jnp.mean(_jnp.square(w)) + 1e-30)
        else:
            s = MOMENT_SCALE[name]
        km, kv = _jax.random.split(_jax.random.fold_in(key, i + 1))
        out[name] = w
        out["m_" + name] = s * _jax.random.normal(km, w.shape, _jnp.float32)
        out["v_" + name] = (s * s) * _jax.random.uniform(kv, w.shape, _jnp.float32, 0.5, 1.5)
    if N_MICROBATCH > 1:
        for name, axis in PER_EXAMPLE_BATCH_AXIS.items():
            out[name] = _to_microbatches(out[name], axis)
    return {'x': out['x'], 'c': out['c'], 'w_ada': out['w_ada'], 'b_ada': out['b_ada'], 'w_in': out['w_in'], 'w_pool': out['w_pool'], 'pool_scale': out['pool_scale'], 'sgu_ln_g': out['sgu_ln_g'], 'sgu_ln_b': out['sgu_ln_b'], 'w_sgu': out['w_sgu'], 'b_sgu': out['b_sgu'], 'w_out': out['w_out'], 'ln_g': out['ln_g'], 'ln_b': out['ln_b'], 'loss_target': out['loss_target'], 'm_w_ada': out['m_w_ada'], 'm_b_ada': out['m_b_ada'], 'm_w_in': out['m_w_in'], 'm_w_pool': out['m_w_pool'], 'm_pool_scale': out['m_pool_scale'], 'm_sgu_ln_g': out['m_sgu_ln_g'], 'm_sgu_ln_b': out['m_sgu_ln_b'], 'm_w_sgu': out['m_w_sgu'], 'm_b_sgu': out['m_b_sgu'], 'm_w_out': out['m_w_out'], 'm_ln_g': out['m_ln_g'], 'm_ln_b': out['m_ln_b'], 'v_w_ada': out['v_w_ada'], 'v_b_ada': out['v_b_ada'], 'v_w_in': out['v_w_in'], 'v_w_pool': out['v_w_pool'], 'v_pool_scale': out['v_pool_scale'], 'v_sgu_ln_g': out['v_sgu_ln_g'], 'v_sgu_ln_b': out['v_sgu_ln_b'], 'v_w_sgu': out['v_w_sgu'], 'v_b_sgu': out['v_b_sgu'], 'v_w_out': out['v_w_out'], 'v_ln_g': out['v_ln_g'], 'v_ln_b': out['v_ln_b']}


def _loss(weights, diff, rest, loss_target):
    with _jax.named_scope("forward"):
        args = {**rest, TWIN_DIFF_INPUT: diff, **{k: w.astype(_WEIGHT_DTYPES[k]) for k, w in weights.items()}}
        y = _forward(args)
    with _jax.named_scope("loss_head"):
        err = _jnp.square(y.astype(_jnp.float32) - loss_target)
        return 0.5 * _jnp.sum(_jnp.mean(err, axis=-1)) if err.ndim else 0.5 * err


def _adamw(w, g, m, v):
    m = ADAM_B1 * m + (1.0 - ADAM_B1) * g
    v = ADAM_B2 * v + (1.0 - ADAM_B2) * _jnp.square(g)
    m_hat = m / (1.0 - ADAM_B1 ** ADAM_STEP)
    v_hat = v / (1.0 - ADAM_B2 ** ADAM_STEP)
    delta = -ADAM_LR * (m_hat / (_jnp.sqrt(v_hat) + ADAM_EPS) + ADAM_WD * w)
    return delta, m, v


def reference(x, c, w_ada, b_ada, w_in, w_pool, pool_scale, sgu_ln_g, sgu_ln_b, w_sgu, b_sgu, w_out, ln_g, ln_b, loss_target, m_w_ada, m_b_ada, m_w_in, m_w_pool, m_pool_scale, m_sgu_ln_g, m_sgu_ln_b, m_w_sgu, m_b_sgu, m_w_out, m_ln_g, m_ln_b, v_w_ada, v_b_ada, v_w_in, v_w_pool, v_pool_scale, v_sgu_ln_g, v_sgu_ln_b, v_w_sgu, v_b_sgu, v_w_out, v_ln_g, v_ln_b):
    given = dict(x=x, c=c, w_ada=w_ada, b_ada=b_ada, w_in=w_in, w_pool=w_pool, pool_scale=pool_scale, sgu_ln_g=sgu_ln_g, sgu_ln_b=sgu_ln_b, w_sgu=w_sgu, b_sgu=b_sgu, w_out=w_out, ln_g=ln_g, ln_b=ln_b, loss_target=loss_target, m_w_ada=m_w_ada, m_b_ada=m_b_ada, m_w_in=m_w_in, m_w_pool=m_w_pool, m_pool_scale=m_pool_scale, m_sgu_ln_g=m_sgu_ln_g, m_sgu_ln_b=m_sgu_ln_b, m_w_sgu=m_w_sgu, m_b_sgu=m_b_sgu, m_w_out=m_w_out, m_ln_g=m_ln_g, m_ln_b=m_ln_b, v_w_ada=v_w_ada, v_b_ada=v_b_ada, v_w_in=v_w_in, v_w_pool=v_w_pool, v_pool_scale=v_pool_scale, v_sgu_ln_g=v_sgu_ln_g, v_sgu_ln_b=v_sgu_ln_b, v_w_sgu=v_w_sgu, v_b_sgu=v_b_sgu, v_w_out=v_w_out, v_ln_g=v_ln_g, v_ln_b=v_ln_b)
    weights = {n: given[n] for n in TWIN_WEIGHTS}
    shared = {n: given[n] for n in SHARED_INPUTS}
    per_example = {n: given[n] for n in ['x', 'c']}
    grad_fn = _jax.value_and_grad(_loss, argnums=(0, 1))

    def one_microbatch(ex, loss_target):
        ex = dict(ex)
        diff = ex.pop(TWIN_DIFF_INPUT)
        return grad_fn(weights, diff, {**shared, **ex}, loss_target)

    if N_MICROBATCH == 1:
        loss, (grad_w, grad_x) = one_microbatch(per_example, given["loss_target"])
    else:
        def body(carry, xs):
            loss_sum, grad_sum = carry
            l_k, (gw_k, gx_k) = one_microbatch(xs[0], xs[1])
            with _jax.named_scope("update"):
                return (loss_sum + l_k, _jax.tree.map(_jnp.add, grad_sum, gw_k)), gx_k

        init = (_jnp.zeros((), _jnp.float32), _jax.tree.map(_jnp.zeros_like, weights))
        (loss, grad_w), grad_x = _jax.lax.scan(body, init, (per_example, given["loss_target"]))
    with _jax.named_scope("update"):
        delta_w, new_m, new_v = {}, {}, {}
        for n in TWIN_WEIGHTS:
            delta_w[n], new_m[n], new_v[n] = _adamw(weights[n], grad_w[n], given["m_" + n], given["v_" + n])
    return (loss, grad_x, *[grad_w[n] for n in TWIN_WEIGHTS], *[delta_w[n] for n in TWIN_WEIGHTS],
            *[new_m[n] for n in TWIN_WEIGHTS], *[new_v[n] for n in TWIN_WEIGHTS])
```

```python
import functools

import jax
import jax.numpy as jnp
from jax import lax
from jax.experimental import pallas as pl
from jax.experimental.pallas import tpu as pltpu

F32 = jnp.float32
BF16 = jnp.bfloat16
MESH = pl.DeviceIdType.MESH

N_DEV = 8
N_CHIP = 4
DEPTH = 2
SEQ = 2048
D_MODEL = 1024
D_POOL = 512
D_PROJ = 2560
HEAD = 128
N_HEAD = 4
ROWS = 256
N_TILE = SEQ // ROWS
HALO = 16
W_IN_COLS = D_PROJ // N_CHIP
W_OUT_ROWS = D_MODEL // N_CHIP
W_ADA_COLS = 3 * D_MODEL // N_CHIP
DEEPNORM_ALPHA = (2.0 * DEPTH) ** 0.25
LN_EPS = 1e-5
INV_SQRT2 = 0.7071067811865476
INV_SQRT_2PI = 0.3989422804014327

ADAM_LR = 0.001
ADAM_B1 = 0.9
ADAM_B2 = 0.999
ADAM_EPS = 1e-08
ADAM_WD = 0.01
ADAM_STEP = 10

PK_W_POOL = 0
PK_W_SGU = 512
PK_POOL_SCALE = 1024
PK_SGU_LN_G = 1032
PK_SGU_LN_B = 1040
PK_B_SGU = 1048
PK_LN_G = 1056
PK_LN_B = 1064
PK_ROWS = 1088
PK_CHUNK = DEPTH * PK_ROWS // N_DEV

VMEM_LIMIT = 56 * 1024 * 1024

NN = (((1,), (0,)), ((), ()))
NT = (((1,), (1,)), ((), ()))
TN = (((0,), (0,)), ((), ()))


def _dot(a, b, dims=NN):
    return lax.dot_general(a, b, dims, preferred_element_type=F32)


def _dot_exact(a, b, dims=NN):
    return lax.dot_general(a, b, dims, preferred_element_type=F32, precision=lax.Precision.HIGHEST)


def _layer_norm(v):
    mu = jnp.mean(v, axis=-1, keepdims=True)
    d = v - mu
    var = jnp.mean(d * d, axis=-1, keepdims=True)
    rstd = lax.rsqrt(var + LN_EPS)
    return d * rstd, rstd


def _layer_norm_bwd(dvhat, vhat, rstd):
    m1 = jnp.mean(dvhat, axis=-1, keepdims=True)
    m2 = jnp.mean(dvhat * vhat, axis=-1, keepdims=True)
    return rstd * (dvhat - m1 - vhat * m2)


def _sigmoid(v):
    return 1.0 / (1.0 + jnp.exp(-v))


def _gelu_parts(v):
    phi = 0.5 * (1.0 + lax.erf(v * INV_SQRT2))
    pdf = INV_SQRT_2PI * jnp.exp(-0.5 * v * v)
    return phi, pdf


def _sum_rows(v):
    return jnp.sum(v, axis=0, keepdims=True)


def _window_sums(ext, toward_later):
    n = ext.shape[0]

    def shifted(v, k):
        return pltpu.roll(v, (n - k) if toward_later else k, 0)

    s2 = ext + shifted(ext, 1)
    r4 = s2[:, HEAD:]
    s4 = r4 + shifted(r4, 2)
    r8 = s4[:, HEAD:]
    s8 = r8 + shifted(r8, 4)
    r16 = s8[:, HEAD:]
    s16 = r16 + shifted(r16, 8)
    return jnp.concatenate([s2[:, :HEAD], s4[:, :HEAD], s8[:, :HEAD], s16], axis=1)


def _window_counts(row0):
    t1 = row0 + 1 + lax.broadcasted_iota(jnp.int32, (ROWS, D_POOL), 0)
    lane = lax.broadcasted_iota(jnp.int32, (ROWS, D_POOL), 1)
    width = jnp.where(lane < HEAD, 2, jnp.where(lane < 2 * HEAD, 4, jnp.where(lane < 3 * HEAD, 8, 16)))
    return jnp.minimum(t1, width).astype(F32)


def _causal_mask():
    r = lax.broadcasted_iota(jnp.int32, (HEAD, HEAD), 0)
    s = lax.broadcasted_iota(jnp.int32, (HEAD, HEAD), 1)
    return r >= s


def _chunks_to_lanes(v):
    return jnp.concatenate([v[n * HEAD:(n + 1) * HEAD] for n in range(ROWS // HEAD)], axis=1)


def _lanes_to_chunks(v):
    return jnp.concatenate([v[:, n * HEAD:(n + 1) * HEAD] for n in range(ROWS // HEAD)], axis=0)


def _mixer(proj, halo, row0, wpool_ref, pscale_ref, sgu_g_ref, sgu_b_ref, wsgu_ref, bsgu_t_ref):
    xa = proj[:, 0:512]
    ga = proj[:, 512:1024]
    u = proj[:, 1024:1536]
    v = proj[:, 1536:2048]
    gb = proj[:, 2048:2560]
    ext = jnp.concatenate([halo, xa], axis=0)
    win = _window_sums(ext, toward_later=False)[HALO:]
    cnt = _window_counts(row0)
    pooled = (win / cnt - xa).astype(BF16)
    pw = jnp.concatenate(
        [_dot(pooled[:, g * HEAD:(g + 1) * HEAD], wpool_ref[g].astype(BF16)) for g in range(N_HEAD)], axis=1)
    sig_a = _sigmoid(ga)
    ya = pw * pscale_ref[...] * (ga * sig_a)
    phi_u, pdf_u = _gelu_parts(u)
    phi_v, pdf_v = _gelu_parts(v)
    gu = u * phi_u
    gv = v * phi_v
    sig_b = _sigmoid(gb)
    silu_b = gb * sig_b
    mask = _causal_mask()
    vhat, rstd_v, vln_l, mixed = [], [], [], []
    for h in range(N_HEAD):
        vh, rh = _layer_norm(gv[:, h * HEAD:(h + 1) * HEAD])
        ln = (vh * sgu_g_ref[h:h + 1, :] + sgu_b_ref[h:h + 1, :]).astype(BF16)
        ln_l = _chunks_to_lanes(ln)
        wm = jnp.where(mask, wsgu_ref[h], 0.0).astype(BF16)
        mx = _lanes_to_chunks(_dot(wm, ln_l) + bsgu_t_ref[:, h:h + 1])
        vhat.append(vh)
        rstd_v.append(rh)
        vln_l.append(ln_l)
        mixed.append(mx)
    mixed = jnp.concatenate(mixed, axis=1)
    yb = gu * mixed * silu_b
    return dict(xa=xa, ga=ga, u=u, gb=gb, cnt=cnt, pooled=pooled, pw=pw, sig_a=sig_a, ya=ya, phi_u=phi_u, pdf_u=pdf_u,
                phi_v=phi_v, pdf_v=pdf_v, gu=gu, sig_b=sig_b, silu_b=silu_b, vhat=vhat, rstd_v=rstd_v, vln_l=vln_l,
                mixed=mixed, yb=yb, mask=mask)


def _full(shape):
    return pl.BlockSpec(shape, lambda i: (0,) * len(shape))


def _layer_weight_specs():
    return [
        _full((N_HEAD, HEAD, HEAD)),
        _full((1, D_POOL)),
        _full((N_HEAD, HEAD)),
        _full((N_HEAD, HEAD)),
        _full((N_HEAD, HEAD, HEAD)),
        _full((HEAD, N_HEAD)),
    ]


def _forward_layer(x, mod, w_in, w_out, small, ln_g, ln_b, target=None):
    last = target is not None

    def body(*refs):
        if last:
            (x_ref, mod_ref, win_ref, wout_ref, wpool_ref, pscale_ref, sgu_g_ref, sgu_b_ref, wsgu_ref, bsgu_t_ref,
             lng_ref, lnb_ref, tgt_ref, proj_ref, y_ref, out_ref, loss_ref, carry_ref) = refs
        else:
            (x_ref, mod_ref, win_ref, wout_ref, wpool_ref, pscale_ref, sgu_g_ref, sgu_b_ref, wsgu_ref, bsgu_t_ref,
             lng_ref, lnb_ref, proj_ref, y_ref, out_ref, carry_ref) = refs
        i = pl.program_id(0)

        @pl.when(i == 0)
        def _():
            carry_ref[...] = jnp.zeros_like(carry_ref)
            if last:
                loss_ref[...] = jnp.zeros_like(loss_ref)

        x = x_ref[...]
        shift, scale, gate = mod_ref[0:1, :], mod_ref[1:2, :], mod_ref[2:3, :]
        xn, _ = _layer_norm(x)
        h = xn * (1.0 + scale) + shift
        proj = _dot(h.astype(BF16), win_ref[...])
        proj_ref[...] = proj
        m = _mixer(proj, carry_ref[...], i * ROWS, wpool_ref, pscale_ref, sgu_g_ref, sgu_b_ref, wsgu_ref, bsgu_t_ref)
        carry_ref[...] = m["xa"][ROWS - HALO:]
        cat = jnp.concatenate([m["ya"], m["yb"]], axis=1).astype(BF16)
        y = _dot(cat, wout_ref[...])
        y_ref[...] = y
        zn, _ = _layer_norm(DEEPNORM_ALPHA * x + gate * y)
        out = zn * lng_ref[...] + lnb_ref[...]
        if last:
            err = out - tgt_ref[...]
            out_ref[...] = err * (1.0 / D_MODEL)
            loss_ref[...] += jnp.sum(err * err)
        else:
            out_ref[...] = out

    tile = pl.BlockSpec((ROWS, D_MODEL), lambda i: (i, 0))
    in_specs = [tile, _full((8, D_MODEL)), _full((D_MODEL, D_PROJ)), _full((D_MODEL, D_MODEL))]
    in_specs += _layer_weight_specs() + [_full((1, D_MODEL)), _full((1, D_MODEL))]
    out_shape = [jax.ShapeDtypeStruct((SEQ, D_PROJ), F32), jax.ShapeDtypeStruct((SEQ, D_MODEL), F32),
                 jax.ShapeDtypeStruct((SEQ, D_MODEL), F32)]
    out_specs = [pl.BlockSpec((ROWS, D_PROJ), lambda i: (i, 0)), tile, tile]
    args = [x, mod, w_in, w_out, *small, ln_g, ln_b]
    if last:
        in_specs.append(tile)
        args.append(target)
        out_shape.append(jax.ShapeDtypeStruct((8, 128), F32))
        out_specs.append(_full((8, 128)))
    return pl.pallas_call(
        body, name="fwd_last" if last else "fwd_first", grid=(N_TILE,), in_specs=in_specs, out_specs=out_specs,
        out_shape=out_shape, scratch_shapes=[pltpu.VMEM((HALO, D_POOL), F32)],
        compiler_params=pltpu.CompilerParams(dimension_semantics=("arbitrary",), vmem_limit_bytes=VMEM_LIMIT),
    )(*args)


def _backward_layer(name, dout, x, y, proj, mod, w_in, w_out, small, ln_g):
    def body(dout_ref, x_ref, y_ref, proj_ref, halo_ref, mod_ref, win_ref, wout_ref, wpool_ref, pscale_ref,
             sgu_g_ref, sgu_b_ref, wsgu_ref, bsgu_t_ref, lng_ref,
             dx_ref, h_ref, cat_ref, dy_ref, dproj_ref, pack_ref, dmod_ref, carry_ref):
        i = pl.program_id(0)
        tile = N_TILE - 1 - i

        @pl.when(i == 0)
        def _():
            carry_ref[...] = jnp.zeros_like(carry_ref)
            pack_ref[...] = jnp.zeros_like(pack_ref)
            dmod_ref[...] = jnp.zeros_like(dmod_ref)

        x = x_ref[...]
        y = y_ref[...]
        dout = dout_ref[...]
        shift, scale, gate = mod_ref[0:1, :], mod_ref[1:2, :], mod_ref[2:3, :]
        xn, rstd_x = _layer_norm(x)
        h = xn * (1.0 + scale) + shift
        h_ref[...] = h.astype(BF16)
        zn, rstd_z = _layer_norm(DEEPNORM_ALPHA * x + gate * y)
        g_ln_g = _sum_rows(dout * zn)
        g_ln_b = _sum_rows(dout)
        dz = _layer_norm_bwd(dout * lng_ref[...], zn, rstd_z)
        d_gate = _sum_rows(dz * y)
        dy = (gate * dz).astype(BF16)
        dy_ref[...] = dy

        halo = jnp.where(tile > 0, halo_ref[...], 0.0)
        m = _mixer(proj_ref[...], halo, tile * ROWS, wpool_ref, pscale_ref, sgu_g_ref, sgu_b_ref, wsgu_ref, bsgu_t_ref)
        cat_ref[...] = jnp.concatenate([m["ya"], m["yb"]], axis=1).astype(BF16)
        dcat = _dot(dy, wout_ref[...], NT)
        dya = dcat[:, :D_POOL]
        dyb = dcat[:, D_POOL:]

        ga, sig_a = m["ga"], m["sig_a"]
        dp = dya * (ga * sig_a)
        d_ga = dya * (m["pw"] * pscale_ref[...]) * (sig_a * (1.0 + ga * (1.0 - sig_a)))
        g_pscale = _sum_rows(dp * m["pw"])
        dpw = (dp * pscale_ref[...]).astype(BF16)
        dpooled = []
        for g in range(N_HEAD):
            cols = slice(g * HEAD, (g + 1) * HEAD)
            pack_ref[PK_W_POOL + g * HEAD:PK_W_POOL + (g + 1) * HEAD, :] += _dot(m["pooled"][:, cols], dpw[:, cols], TN)
            dpooled.append(_dot(dpw[:, cols], wpool_ref[g].astype(BF16), NT))
        dpooled = jnp.concatenate(dpooled, axis=1)
        q = dpooled / m["cnt"]
        ext = jnp.concatenate([q, carry_ref[...]], axis=0)
        d_xa = _window_sums(ext, toward_later=True)[:ROWS] - dpooled
        carry_ref[...] = q[:HALO]

        gu, mixed, silu_b, gb, sig_b = m["gu"], m["mixed"], m["silu_b"], m["gb"], m["sig_b"]
        d_mixed = dyb * gu * silu_b
        d_gu = dyb * mixed * silu_b
        d_gb = dyb * gu * mixed * (sig_b * (1.0 + gb * (1.0 - sig_b)))
        d_u = d_gu * (m["phi_u"] + m["u"] * m["pdf_u"])
        ones = jnp.ones((8, HEAD), F32)
        d_v = []
        for hd in range(N_HEAD):
            cols = slice(hd * HEAD, (hd + 1) * HEAD)
            dm = d_mixed[:, cols]
            dm_l = _chunks_to_lanes(dm.astype(BF16))
            g_w = _dot(dm_l, m["vln_l"][hd], NT)
            pack_ref[PK_W_SGU + hd * HEAD:PK_W_SGU + (hd + 1) * HEAD, :] += jnp.where(m["mask"], g_w, 0.0)
            dm_sum = dm[0:HEAD]
            for n in range(1, ROWS // HEAD):
                dm_sum = dm_sum + dm[n * HEAD:(n + 1) * HEAD]
            pack_ref[PK_B_SGU + hd:PK_B_SGU + hd + 1, :] += _dot_exact(ones, dm_sum, NT)[0:1]
            wm = jnp.where(m["mask"], wsgu_ref[hd], 0.0).astype(BF16)
            d_vln = _lanes_to_chunks(_dot(wm, dm_l, TN))
            vhat = m["vhat"][hd]
            pack_ref[PK_SGU_LN_G + hd:PK_SGU_LN_G + hd + 1, :] += _sum_rows(d_vln * vhat)
            pack_ref[PK_SGU_LN_B + hd:PK_SGU_LN_B + hd + 1, :] += _sum_rows(d_vln)
            d_v.append(_layer_norm_bwd(d_vln * sgu_g_ref[hd:hd + 1, :], vhat, m["rstd_v"][hd]))
        v = proj_ref[:, 1536:2048]
        d_v = jnp.concatenate(d_v, axis=1) * (m["phi_v"] + v * m["pdf_v"])

        dproj = jnp.concatenate([d_xa, d_ga, d_u, d_v, d_gb], axis=1).astype(BF16)
        dproj_ref[...] = dproj
        dh = _dot(dproj, win_ref[...], NT)
        d_scale = _sum_rows(dh * xn)
        d_shift = _sum_rows(dh)
        dx_ref[...] = DEEPNORM_ALPHA * dz + _layer_norm_bwd(dh * (1.0 + scale), xn, rstd_x)

        dmod_ref[0:1, :] += d_shift
        dmod_ref[1:2, :] += d_scale
        dmod_ref[2:3, :] += d_gate
        for g in range(N_HEAD):
            pack_ref[PK_POOL_SCALE + g:PK_POOL_SCALE + g + 1, :] += g_pscale[:, g * HEAD:(g + 1) * HEAD]
        for k in range(D_MODEL // HEAD):
            pack_ref[PK_LN_G + k:PK_LN_G + k + 1, :] += g_ln_g[:, k * HEAD:(k + 1) * HEAD]
            pack_ref[PK_LN_B + k:PK_LN_B + k + 1, :] += g_ln_b[:, k * HEAD:(k + 1) * HEAD]

    def rev(i):
        return (N_TILE - 1 - i, 0)

    tile = pl.BlockSpec((ROWS, D_MODEL), rev)
    halo = pl.BlockSpec((HALO, D_POOL), lambda i: (jnp.maximum((N_TILE - 1 - i) * (ROWS // HALO) - 1, 0), 0))
    in_specs = [tile, tile, tile, pl.BlockSpec((ROWS, D_PROJ), rev), halo, _full((8, D_MODEL)),
                _full((D_MODEL, D_PROJ)), _full((D_MODEL, D_MODEL))] + _layer_weight_specs() + [_full((1, D_MODEL))]
    out_shape = [jax.ShapeDtypeStruct((SEQ, D_MODEL), F32), jax.ShapeDtypeStruct((SEQ, D_MODEL), BF16),
                 jax.ShapeDtypeStruct((SEQ, D_MODEL), BF16), jax.ShapeDtypeStruct((SEQ, D_MODEL), BF16),
                 jax.ShapeDtypeStruct((SEQ, D_PROJ), BF16), jax.ShapeDtypeStruct((PK_ROWS, HEAD), F32),
                 jax.ShapeDtypeStruct((8, D_MODEL), F32)]
    out_specs = [tile, tile, tile, tile, pl.BlockSpec((ROWS, D_PROJ), rev), _full((PK_ROWS, HEAD)), _full((8, D_MODEL))]
    return pl.pallas_call(
        body, name=name, grid=(N_TILE,), in_specs=in_specs, out_specs=out_specs, out_shape=out_shape,
        scratch_shapes=[pltpu.VMEM((HALO, D_POOL), F32)],
        compiler_params=pltpu.CompilerParams(dimension_semantics=("arbitrary",), vmem_limit_bytes=VMEM_LIMIT),
    )(dout, x, y, proj, proj, mod, w_in, w_out, *small, ln_g)


def _weight_grad(name, a, b, n_block):
    m, n = a.shape[1], b.shape[1]

    def body(a_ref, b_ref, o_ref):
        o_ref[...] = _dot(a_ref[...], b_ref[...], TN)

    return pl.pallas_call(
        body, name=name, grid=(n // n_block,),
        in_specs=[pl.BlockSpec((SEQ, m), lambda i: (0, 0)), pl.BlockSpec((SEQ, n_block), lambda i: (0, i))],
        out_specs=pl.BlockSpec((m, n_block), lambda i: (0, i)),
        out_shape=jax.ShapeDtypeStruct((m, n), F32),
        compiler_params=pltpu.CompilerParams(dimension_semantics=("arbitrary",), vmem_limit_bytes=VMEM_LIMIT),
    )(a, b)


def _position():
    x, y, c = lax.axis_index("x"), lax.axis_index("y"), lax.axis_index("c")
    return x, y, c


def _other_chips(x, y):
    return [(1 - x, y), (x, 1 - y), (1 - x, 1 - y)]


def _other_devices(x, y, c):
    out = []
    for r in range(1, N_DEV):
        fx, fy, fc = (r >> 2) & 1, (r >> 1) & 1, r & 1
        out.append((x + fx - 2 * x * fx, y + fy - 2 * y * fy, c + fc - 2 * c * fc))
    return out


def _gather(c_vec, w_ada, b_ada, w_in, w_out):
    half_in = D_MODEL // 2
    half_out = W_OUT_ROWS // 2

    def body(c_ref, wada_ref, bada_ref, win_ref, wout_ref,
             win_all, wout_all, mod_ref, c_all,
             win_bf, wout_bf, mod_mine, mod_all, big_send, big_recv, fwd_send, fwd_recv, c_send, c_recv,
             mod_send, mod_recv, local_sem):
        x, y, c = _position()
        chip = 2 * x + y
        dev = 4 * x + 2 * y + c
        sibling = (x, y, 1 - c)
        chips = _other_chips(x, y)
        devices = _other_devices(x, y, c)

        def in_rows(core):
            return pl.ds(pl.multiple_of(core * half_in, half_in), half_in)

        def out_rows(ch, core):
            return pl.ds(pl.multiple_of(ch * W_OUT_ROWS + core * half_out, half_out), half_out)

        def in_cols(ch):
            return pl.ds(pl.multiple_of(ch * W_IN_COLS, 128), W_IN_COLS)

        win_bf[...] = win_ref[...].astype(BF16)
        wout_bf[...] = wout_ref[...].astype(BF16)

        sends = []
        for p, (px, py) in enumerate(chips):
            sends.append(pltpu.make_async_remote_copy(
                win_bf.at[:, in_rows(c), :], win_all.at[:, in_rows(c), in_cols(chip)],
                big_send.at[2 * p], big_recv.at[2 * p], device_id=(px, py, c), device_id_type=MESH))
            sends.append(pltpu.make_async_remote_copy(
                wout_bf.at[:, pl.ds(pl.multiple_of(c * half_out, half_out), half_out), :],
                wout_all.at[:, out_rows(chip, c), :],
                big_send.at[2 * p + 1], big_recv.at[2 * p + 1], device_id=(px, py, c), device_id_type=MESH))
        for cp in sends:
            cp.start()
        own_in = pltpu.make_async_copy(win_bf, win_all.at[:, :, in_cols(chip)], local_sem.at[0])
        own_out = pltpu.make_async_copy(
            wout_bf, wout_all.at[:, pl.ds(pl.multiple_of(chip * W_OUT_ROWS, W_OUT_ROWS), W_OUT_ROWS), :], local_sem.at[1])
        own_in.start()
        own_out.start()

        c_all[pl.ds(dev, 1), :] = c_ref[...]
        c_copies = [pltpu.make_async_remote_copy(c_ref, c_all.at[pl.ds(dev, 1), :], c_send.at[r], c_recv.at[r],
                                                 device_id=d, device_id_type=MESH) for r, d in enumerate(devices)]
        for cp in c_copies:
            cp.start()
        for cp in c_copies:
            cp.wait()

        cv = c_all[...]
        silu_c = (cv * _sigmoid(cv)).astype(BF16)
        for l in range(DEPTH):
            mod_mine[l] = _dot(silu_c, wada_ref[l].astype(BF16))
        mod_all[chip] = mod_mine[...]
        m_copies = [pltpu.make_async_remote_copy(mod_mine, mod_all.at[chip], mod_send.at[p], mod_recv.at[p],
                                                 device_id=(px, py, c), device_id_type=MESH)
                    for p, (px, py) in enumerate(chips)]
        for cp in m_copies:
            cp.start()
        for cp in m_copies:
            cp.wait()
        mod_ref[...] = jnp.zeros_like(mod_ref)
        for l in range(DEPTH):
            full = jnp.concatenate([mod_all[ch, l, pl.ds(dev, 1), :] for ch in range(N_CHIP)], axis=1) + bada_ref[l:l + 1, :]
            for k in range(3):
                mod_ref[l, k:k + 1, :] = full[:, k * D_MODEL:(k + 1) * D_MODEL]

        forwards = []
        for p, (px, py) in enumerate(chips):
            src_chip = 2 * px + py
            region_in = win_all.at[:, in_rows(c), in_cols(src_chip)]
            region_out = wout_all.at[:, out_rows(src_chip, c), :]
            pltpu.make_async_remote_copy(region_in, region_in, big_send.at[2 * p], big_recv.at[2 * p],
                                         device_id=(px, py, c), device_id_type=MESH).wait_recv()
            f_in = pltpu.make_async_remote_copy(region_in, region_in, fwd_send.at[2 * p], fwd_recv.at[2 * p],
                                                device_id=sibling, device_id_type=MESH)
            f_in.start()
            pltpu.make_async_remote_copy(region_out, region_out, big_send.at[2 * p + 1], big_recv.at[2 * p + 1],
                                         device_id=(px, py, c), device_id_type=MESH).wait_recv()
            f_out = pltpu.make_async_remote_copy(region_out, region_out, fwd_send.at[2 * p + 1], fwd_recv.at[2 * p + 1],
                                                 device_id=sibling, device_id_type=MESH)
            f_out.start()
            forwards += [f_in, f_out]
        for p, (px, py) in enumerate(chips):
            src_chip = 2 * px + py
            region_in = win_all.at[:, in_rows(1 - c), in_cols(src_chip)]
            region_out = wout_all.at[:, out_rows(src_chip, 1 - c), :]
            pltpu.make_async_remote_copy(region_in, region_in, fwd_send.at[2 * p], fwd_recv.at[2 * p],
                                         device_id=sibling, device_id_type=MESH).wait_recv()
            pltpu.make_async_remote_copy(region_out, region_out, fwd_send.at[2 * p + 1], fwd_recv.at[2 * p + 1],
                                         device_id=sibling, device_id_type=MESH).wait_recv()
        for cp in sends + forwards:
            cp.wait_send()
        own_in.wait()
        own_out.wait()

    vmem = pl.BlockSpec(memory_space=pltpu.VMEM)
    hbm = pl.BlockSpec(memory_space=pl.ANY)
    return pl.pallas_call(
        body, name="gather",
        in_specs=[vmem, vmem, vmem, vmem, vmem],
        out_specs=[hbm, hbm, vmem, vmem],
        out_shape=[jax.ShapeDtypeStruct((DEPTH, D_MODEL, D_PROJ), BF16), jax.ShapeDtypeStruct((DEPTH, D_MODEL, D_MODEL), BF16),
                   jax.ShapeDtypeStruct((DEPTH, 8, D_MODEL), F32), jax.ShapeDtypeStruct((N_DEV, D_MODEL), F32)],
        scratch_shapes=[
            pltpu.VMEM((DEPTH, D_MODEL, W_IN_COLS), BF16), pltpu.VMEM((DEPTH, W_OUT_ROWS, D_MODEL), BF16),
            pltpu.VMEM((DEPTH, N_DEV, W_ADA_COLS), F32), pltpu.VMEM((N_CHIP, DEPTH, N_DEV, W_ADA_COLS), F32),
            pltpu.SemaphoreType.DMA((6,)), pltpu.SemaphoreType.DMA((6,)),
            pltpu.SemaphoreType.DMA((6,)), pltpu.SemaphoreType.DMA((6,)),
            pltpu.SemaphoreType.DMA((7,)), pltpu.SemaphoreType.DMA((7,)),
            pltpu.SemaphoreType.DMA((3,)), pltpu.SemaphoreType.DMA((3,)),
            pltpu.SemaphoreType.DMA((2,)),
        ],
        compiler_params=pltpu.CompilerParams(vmem_limit_bytes=VMEM_LIMIT),
    )(c_vec, w_ada, b_ada, w_in, w_out)


def _reduce(gw_in, gw_out, pack, dmod):
    def body(gwin_ref, gwout_ref, pack_ref, dmod_ref,
             gin_out, gout_out, pack_out, dmod_all,
             a_in, b_in, a_out, b_out, s_in, s_out, r_in, r_out, f_in, f_out, p_recv, p_sum, dm_st,
             sib_send, sib_recv, ici_send, ici_recv, fin_send, fin_recv, pk_send, pk_recv, ag_send, ag_recv,
             dm_send, dm_recv, local_sem):
        x, y, c = _position()
        chip = 2 * x + y
        dev = 4 * x + 2 * y + c
        sibling = (x, y, 1 - c)
        devices = _other_devices(x, y, c)

        load_in = pltpu.make_async_copy(gwin_ref.at[c], a_in, local_sem.at[0])
        load_out = pltpu.make_async_copy(gwout_ref.at[c], a_out, local_sem.at[1])
        load_in.start()
        load_out.start()
        sib_in = pltpu.make_async_remote_copy(gwin_ref.at[1 - c], b_in, sib_send.at[0], sib_recv.at[0],
                                              device_id=sibling, device_id_type=MESH)
        sib_out = pltpu.make_async_remote_copy(gwout_ref.at[1 - c], b_out, sib_send.at[1], sib_recv.at[1],
                                               device_id=sibling, device_id_type=MESH)
        sib_in.start()
        sib_out.start()

        def chunk(ref, d):
            return ref.at[d // 4, pl.ds(pl.multiple_of((d % 4) * PK_CHUNK, 8), PK_CHUNK), :]

        p_recv[dev] = pack_ref[dev // 4, pl.ds(pl.multiple_of((dev % 4) * PK_CHUNK, 8), PK_CHUNK), :]
        pk = [pltpu.make_async_remote_copy(chunk(pack_ref, 4 * dx + 2 * dy + dc), p_recv.at[dev], pk_send.at[r], pk_recv.at[r],
                                           device_id=(dx, dy, dc), device_id_type=MESH)
              for r, (dx, dy, dc) in enumerate(devices)]
        for cp in pk:
            cp.start()
        for l in range(DEPTH):
            for k in range(3):
                for r in range(D_MODEL // HEAD):
                    dm_st[l, 8 * k + r] = dmod_ref[l, k:k + 1, r * HEAD:(r + 1) * HEAD]
        dmod_all[:, :, pl.ds(dev, 1), :] = dm_st[...]
        dm = [pltpu.make_async_remote_copy(dm_st, dmod_all.at[:, :, pl.ds(dev, 1), :], dm_send.at[r], dm_recv.at[r],
                                           device_id=d, device_id_type=MESH) for r, d in enumerate(devices)]
        for cp in dm:
            cp.start()
        for cp in pk:
            cp.wait()
        acc = p_recv[0]
        for d in range(1, N_DEV):
            acc = acc + p_recv[d]
        p_sum[...] = acc
        pack_out[dev // 4, pl.ds(pl.multiple_of((dev % 4) * PK_CHUNK, 8), PK_CHUNK), :] = acc
        ag = [pltpu.make_async_remote_copy(p_sum, chunk(pack_out, dev), ag_send.at[r], ag_recv.at[r],
                                           device_id=d, device_id_type=MESH) for r, d in enumerate(devices)]
        for cp in ag:
            cp.start()

        load_in.wait()
        load_out.wait()
        sib_in.wait()
        sib_out.wait()
        a_in[...] += b_in[...]
        a_out[...] += b_out[...]

        for ch in range(N_CHIP):
            @pl.when(ch != chip)
            def _(ch=ch):
                s_in[ch] = a_in[:, ch * W_IN_COLS:(ch + 1) * W_IN_COLS].astype(BF16)
                s_out[ch] = a_out[ch * W_OUT_ROWS:(ch + 1) * W_OUT_ROWS, :].astype(BF16)
                pltpu.make_async_remote_copy(s_in.at[ch], r_in.at[chip], ici_send.at[2 * ch], ici_recv.at[2 * chip],
                                             device_id=(ch // 2, ch % 2, c), device_id_type=MESH).start()
                pltpu.make_async_remote_copy(s_out.at[ch], r_out.at[chip], ici_send.at[2 * ch + 1], ici_recv.at[2 * chip + 1],
                                             device_id=(ch // 2, ch % 2, c), device_id_type=MESH).start()
        for ch in range(N_CHIP):
            @pl.when(ch != chip)
            def _(ch=ch):
                pltpu.make_async_remote_copy(s_in.at[ch], r_in.at[ch], ici_send.at[2 * ch], ici_recv.at[2 * ch],
                                             device_id=(ch // 2, ch % 2, c), device_id_type=MESH).wait()
                pltpu.make_async_remote_copy(s_out.at[ch], r_out.at[ch], ici_send.at[2 * ch + 1], ici_recv.at[2 * ch + 1],
                                             device_id=(ch // 2, ch % 2, c), device_id_type=MESH).wait()
        for me in range(N_CHIP):
            @pl.when(me == chip)
            def _(me=me):
                tot_in = None
                tot_out = None
                for ch in range(N_CHIP):
                    if ch == me:
                        part_in = a_in[:, me * W_IN_COLS:(me + 1) * W_IN_COLS]
                        part_out = a_out[me * W_OUT_ROWS:(me + 1) * W_OUT_ROWS, :]
                    else:
                        part_in = r_in[ch].astype(F32)
                        part_out = r_out[ch].astype(F32)
                    tot_in = part_in if tot_in is None else tot_in + part_in
                    tot_out = part_out if tot_out is None else tot_out + part_out
                f_in[...] = tot_in
                f_out[...] = tot_out

        keep_in = pltpu.make_async_copy(f_in, gin_out.at[c], local_sem.at[0])
        keep_out = pltpu.make_async_copy(f_out, gout_out.at[c], local_sem.at[1])
        keep_in.start()
        keep_out.start()
        give_in = pltpu.make_async_remote_copy(f_in, gin_out.at[c], fin_send.at[0], fin_recv.at[0],
                                               device_id=sibling, device_id_type=MESH)
        give_out = pltpu.make_async_remote_copy(f_out, gout_out.at[c], fin_send.at[1], fin_recv.at[1],
                                                device_id=sibling, device_id_type=MESH)
        give_in.start()
        give_out.start()
        for cp in dm + ag:
            cp.wait()
        keep_in.wait()
        keep_out.wait()
        give_in.wait_send()
        give_out.wait_send()
        pltpu.make_async_remote_copy(f_in, gin_out.at[1 - c], fin_send.at[0], fin_recv.at[0],
                                     device_id=sibling, device_id_type=MESH).wait_recv()
        pltpu.make_async_remote_copy(f_out, gout_out.at[1 - c], fin_send.at[1], fin_recv.at[1],
                                     device_id=sibling, device_id_type=MESH).wait_recv()

    vmem = pl.BlockSpec(memory_space=pltpu.VMEM)
    hbm = pl.BlockSpec(memory_space=pl.ANY)
    return pl.pallas_call(
        body, name="reduce",
        in_specs=[hbm, hbm, vmem, vmem],
        out_specs=[hbm, hbm, vmem, vmem],
        out_shape=[jax.ShapeDtypeStruct((DEPTH, D_MODEL, W_IN_COLS), F32), jax.ShapeDtypeStruct((DEPTH, W_OUT_ROWS, D_MODEL), F32),
                   jax.ShapeDtypeStruct((DEPTH, PK_ROWS, HEAD), F32), jax.ShapeDtypeStruct((DEPTH, 24, N_DEV, HEAD), F32)],
        scratch_shapes=[
            pltpu.VMEM((D_MODEL, D_PROJ), F32), pltpu.VMEM((D_MODEL, D_PROJ), F32),
            pltpu.VMEM((D_MODEL, D_MODEL), F32), pltpu.VMEM((D_MODEL, D_MODEL), F32),
            pltpu.VMEM((N_CHIP, D_MODEL, W_IN_COLS), BF16), pltpu.VMEM((N_CHIP, W_OUT_ROWS, D_MODEL), BF16),
            pltpu.VMEM((N_CHIP, D_MODEL, W_IN_COLS), BF16), pltpu.VMEM((N_CHIP, W_OUT_ROWS, D_MODEL), BF16),
            pltpu.VMEM((D_MODEL, W_IN_COLS), F32), pltpu.VMEM((W_OUT_ROWS, D_MODEL), F32),
            pltpu.VMEM((N_DEV, PK_CHUNK, HEAD), F32), pltpu.VMEM((PK_CHUNK, HEAD), F32),
            pltpu.VMEM((DEPTH, 24, 1, HEAD), F32),
            pltpu.SemaphoreType.DMA((2,)), pltpu.SemaphoreType.DMA((2,)),
            pltpu.SemaphoreType.DMA((8,)), pltpu.SemaphoreType.DMA((8,)),
            pltpu.SemaphoreType.DMA((2,)), pltpu.SemaphoreType.DMA((2,)),
            pltpu.SemaphoreType.DMA((7,)), pltpu.SemaphoreType.DMA((7,)),
            pltpu.SemaphoreType.DMA((7,)), pltpu.SemaphoreType.DMA((7,)),
            pltpu.SemaphoreType.DMA((7,)), pltpu.SemaphoreType.DMA((7,)),
            pltpu.SemaphoreType.DMA((2,)),
        ],
        compiler_params=pltpu.CompilerParams(vmem_limit_bytes=60 * 1024 * 1024),
    )(gw_in, gw_out, pack, dmod)


def _adamw(w, g, m, v):
    m = ADAM_B1 * m + (1.0 - ADAM_B1) * g
    v = ADAM_B2 * v + (1.0 - ADAM_B2) * (g * g)
    m_hat = m / (1.0 - ADAM_B1 ** ADAM_STEP)
    v_hat = v / (1.0 - ADAM_B2 ** ADAM_STEP)
    delta = -ADAM_LR * (m_hat / (jnp.sqrt(v_hat) + ADAM_EPS) + ADAM_WD * w)
    return delta, m, v


def _adam_sharded(name, w, g, m, v, rows):
    _, r, cols = w.shape

    def body(w_ref, g_ref, m_ref, v_ref, d_out, m_out, v_out):
        d_out[...], m_out[...], v_out[...] = _adamw(w_ref[...], g_ref[...], m_ref[...], v_ref[...])

    blk = pl.BlockSpec((1, rows, cols), lambda l, i: (l, i, 0))
    shape = jax.ShapeDtypeStruct(w.shape, F32)
    return pl.pallas_call(
        body, name=name, grid=(DEPTH, r // rows), in_specs=[blk] * 4, out_specs=[blk] * 3, out_shape=[shape] * 3,
        compiler_params=pltpu.CompilerParams(dimension_semantics=("arbitrary", "arbitrary"), vmem_limit_bytes=VMEM_LIMIT),
    )(w, g, m, v)


def _adam_w_ada(w, m, v, c_all, dmod_all):
    rows = 256

    def body(c_ref, dm_ref, w_ref, m_ref, v_ref, g_out, d_out, m_out, v_out):
        l = pl.program_id(0)
        chip = 2 * lax.axis_index("x") + lax.axis_index("y")
        cv = c_ref[...]
        silu_c = (cv * _sigmoid(cv)).astype(BF16).astype(F32)
        pieces = []
        for k in range(W_ADA_COLS // HEAD):
            dk = dm_ref[l, 6 * chip + k].astype(BF16).astype(F32)
            pieces.append(_dot_exact(silu_c, dk, TN))
        g = jnp.concatenate(pieces, axis=1)
        g_out[0] = g
        d_out[0], m_out[0], v_out[0] = _adamw(w_ref[0], g, m_ref[0], v_ref[0])

    blk = pl.BlockSpec((1, rows, W_ADA_COLS), lambda l, i: (l, i, 0))
    shape = jax.ShapeDtypeStruct(w.shape, F32)
    return pl.pallas_call(
        body, name="adam_w_ada", grid=(DEPTH, D_MODEL // rows),
        in_specs=[pl.BlockSpec((N_DEV, rows), lambda l, i: (0, i)),
                  pl.BlockSpec((DEPTH, 24, N_DEV, HEAD), lambda l, i: (0, 0, 0, 0)), blk, blk, blk],
        out_specs=[blk] * 4, out_shape=[shape] * 4,
        compiler_params=pltpu.CompilerParams(dimension_semantics=("arbitrary", "arbitrary"), vmem_limit_bytes=VMEM_LIMIT),
    )(c_all, dmod_all, w, m, v)


def _adam_small(pack, dmod_all, weights, ms, vs):
    n = len(weights)

    def body(*refs):
        pack_ref, dm_ref = refs[0], refs[1]
        w_refs, m_refs, v_refs = refs[2:2 + n], refs[2 + n:2 + 2 * n], refs[2 + 2 * n:2 + 3 * n]
        outs = refs[2 + 3 * n:]
        g_refs, d_refs, nm_refs, nv_refs = outs[0:n], outs[n:2 * n], outs[2 * n:3 * n], outs[3 * n:4 * n]

        def lanes(l, row0, count):
            return jnp.concatenate([pack_ref[l, row0 + k:row0 + k + 1, :] for k in range(count)], axis=1)

        def update(idx, at, g):
            g_refs[idx][at] = g
            d_refs[idx][at], nm_refs[idx][at], nv_refs[idx][at] = _adamw(w_refs[idx][at], g, m_refs[idx][at], v_refs[idx][at])

        for l in range(DEPTH):
            row = (slice(l, l + 1), slice(None))
            g_b = None
            for d in range(N_DEV):
                part = dm_ref[l, :, d, :]
                g_b = part if g_b is None else g_b + part
            update(0, row, jnp.concatenate([g_b[k:k + 1, :] for k in range(24)], axis=1))
            for g in range(N_HEAD):
                update(1, (l, g), pack_ref[l, PK_W_POOL + g * HEAD:PK_W_POOL + (g + 1) * HEAD, :])
                update(5, (l, g), pack_ref[l, PK_W_SGU + g * HEAD:PK_W_SGU + (g + 1) * HEAD, :])
            update(2, row, lanes(l, PK_POOL_SCALE, N_HEAD))
            update(3, (l,), pack_ref[l, PK_SGU_LN_G:PK_SGU_LN_G + N_HEAD, :])
            update(4, (l,), pack_ref[l, PK_SGU_LN_B:PK_SGU_LN_B + N_HEAD, :])
            update(6, (l,), pack_ref[l, PK_B_SGU:PK_B_SGU + N_HEAD, :])
            update(7, row, lanes(l, PK_LN_G, D_MODEL // HEAD))
            update(8, row, lanes(l, PK_LN_B, D_MODEL // HEAD))

    vmem = pl.BlockSpec(memory_space=pltpu.VMEM)
    shapes = [jax.ShapeDtypeStruct(w.shape, F32) for w in weights]
    return pl.pallas_call(
        body, name="adam_small", in_specs=[vmem] * (2 + 3 * n), out_specs=[vmem] * (4 * n), out_shape=shapes * 4,
        compiler_params=pltpu.CompilerParams(vmem_limit_bytes=VMEM_LIMIT),
    )(pack, dmod_all, *weights, *ms, *vs)


def kernel(x, c, w_ada, b_ada, w_in, w_pool, pool_scale, sgu_ln_g, sgu_ln_b, w_sgu, b_sgu, w_out, ln_g, ln_b, loss_target, m_w_ada, m_b_ada, m_w_in, m_w_pool, m_pool_scale, m_sgu_ln_g, m_sgu_ln_b, m_w_sgu, m_b_sgu, m_w_out, m_ln_g, m_ln_b, v_w_ada, v_b_ada, v_w_in, v_w_pool, v_pool_scale, v_sgu_ln_g, v_sgu_ln_b, v_w_sgu, v_b_sgu, v_w_out, v_ln_g, v_ln_b):
    w_in_all, w_out_all, mod, c_all = _gather(c, w_ada, b_ada, w_in, w_out)
    b_sgu_t = jnp.swapaxes(b_sgu, 1, 2)

    def small(l):
        return (w_pool[l], pool_scale[l:l + 1], sgu_ln_g[l], sgu_ln_b[l], w_sgu[l], b_sgu_t[l])

    x0 = x[0]
    proj0, y0, x1 = _forward_layer(x0, mod[0], w_in_all[0], w_out_all[0], small(0), ln_g[0:1], ln_b[0:1])
    proj1, y1, dout, sq = _forward_layer(x1, mod[1], w_in_all[1], w_out_all[1], small(1), ln_g[1:2], ln_b[1:2],
                                         target=loss_target[0])
    loss = lax.psum(sq[0, 0], ("x", "y", "c")) * (0.5 / D_MODEL)

    dx1, h1, cat1, dy1, dproj1, pack1, dmod1 = _backward_layer(
        "bwd_last", dout, x1, y1, proj1, mod[1], w_in_all[1], w_out_all[1], small(1), ln_g[1:2])
    gw_in1 = _weight_grad("wgrad_in_last", h1, dproj1, W_IN_COLS)
    gw_out1 = _weight_grad("wgrad_out_last", cat1, dy1, D_MODEL // 2)
    dx0, h0, cat0, dy0, dproj0, pack0, dmod0 = _backward_layer(
        "bwd_first", dx1, x0, y0, proj0, mod[0], w_in_all[0], w_out_all[0], small(0), ln_g[0:1])
    gw_in0 = _weight_grad("wgrad_in_first", h0, dproj0, W_IN_COLS)
    gw_out0 = _weight_grad("wgrad_out_first", cat0, dy0, D_MODEL // 2)

    g_w_in, g_w_out, pack, dmod_all = _reduce(
        jnp.stack([gw_in0, gw_in1]), jnp.stack([gw_out0, gw_out1]), jnp.stack([pack0, pack1]), jnp.stack([dmod0, dmod1]))

    g_w_ada, d_w_ada, nm_w_ada, nv_w_ada = _adam_w_ada(w_ada, m_w_ada, v_w_ada, c_all, dmod_all)
    d_w_in, nm_w_in, nv_w_in = _adam_sharded("adam_w_in", w_in, g_w_in, m_w_in, v_w_in, 256)
    d_w_out, nm_w_out, nv_w_out = _adam_sharded("adam_w_out", w_out, g_w_out, m_w_out, v_w_out, 256)
    small_w = (b_ada, w_pool, pool_scale, sgu_ln_g, sgu_ln_b, w_sgu, b_sgu, ln_g, ln_b)
    small_m = (m_b_ada, m_w_pool, m_pool_scale, m_sgu_ln_g, m_sgu_ln_b, m_w_sgu, m_b_sgu, m_ln_g, m_ln_b)
    small_v = (v_b_ada, v_w_pool, v_pool_scale, v_sgu_ln_g, v_sgu_ln_b, v_w_sgu, v_b_sgu, v_ln_g, v_ln_b)
    res = _adam_small(pack, dmod_all, small_w, small_m, small_v)
    n = len(small_w)
    g_s, d_s, nm_s, nv_s = res[0:n], res[n:2 * n], res[2 * n:3 * n], res[3 * n:4 * n]

    def ordered(ada, w_in_, w_out_, s):
        return (ada, s[0], w_in_, s[1], s[2], s[3], s[4], s[5], s[6], w_out_, s[7], s[8])

    return (loss, dx0[None], *ordered(g_w_ada, g_w_in, g_w_out, g_s), *ordered(d_w_ada, d_w_in, d_w_out, d_s),
            *ordered(nm_w_ada, nm_w_in, nm_w_out, nm_s), *ordered(nv_w_ada, nv_w_in, nv_w_out, nv_s))
```

```python
import jax
import jax.numpy as jnp
from jax import lax
from jax.experimental import pallas as pl
from jax.experimental.pallas import tpu as pltpu

F32 = jnp.float32
BF16 = jnp.bfloat16
MESH = pl.DeviceIdType.MESH

N_DEV = 8
N_CHIP = 4
DEPTH = 2
SEQ = 2048
D_MODEL = 1024
D_POOL = 512
D_PROJ = 2560
HEAD = 128
N_HEAD = 4
ROWS = 256
N_TILE = SEQ // ROWS
HALO = 16
W_IN_COLS = D_PROJ // N_CHIP
W_OUT_ROWS = D_MODEL // N_CHIP
W_ADA_COLS = 3 * D_MODEL // N_CHIP
HALF_IN = D_MODEL // 2
HALF_OUT = W_OUT_ROWS // 2
DEEPNORM_ALPHA = (2.0 * DEPTH) ** 0.25
LN_EPS = 1e-5
INV_SQRT2 = 0.7071067811865476
INV_SQRT_2PI = 0.3989422804014327

ADAM_LR = 0.001
ADAM_B1 = 0.9
ADAM_B2 = 0.999
ADAM_EPS = 1e-08
ADAM_WD = 0.01
ADAM_STEP = 10

PK_W_POOL = 0
PK_W_SGU = 512
PK_POOL_SCALE = 1024
PK_SGU_LN_G = 1032
PK_SGU_LN_B = 1040
PK_B_SGU = 1048
PK_LN_G = 1056
PK_LN_B = 1064
PK_LOSS = 1072
PK_ROWS = 1088
PK_CHUNK = DEPTH * PK_ROWS // N_DEV

VMEM_LIMIT = 56 * 1024 * 1024

NN = (((1,), (0,)), ((), ()))
NT = (((1,), (1,)), ((), ()))
TN = (((0,), (0,)), ((), ()))


def _dot(a, b, dims=NN):
    return lax.dot_general(a, b, dims, preferred_element_type=F32)


def _dot_exact(a, b, dims=NN):
    return lax.dot_general(a, b, dims, preferred_element_type=F32, precision=lax.Precision.HIGHEST)


def _layer_norm(v):
    mu = jnp.mean(v, axis=-1, keepdims=True)
    d = v - mu
    var = jnp.mean(d * d, axis=-1, keepdims=True)
    rstd = lax.rsqrt(var + LN_EPS)
    return d * rstd, rstd


def _layer_norm_bwd(dvhat, vhat, rstd):
    m1 = jnp.mean(dvhat, axis=-1, keepdims=True)
    m2 = jnp.mean(dvhat * vhat, axis=-1, keepdims=True)
    return rstd * (dvhat - m1 - vhat * m2)


def _sigmoid(v):
    return 1.0 / (1.0 + jnp.exp(-v))


def _gelu_parts(v):
    phi = 0.5 * (1.0 + lax.erf(v * INV_SQRT2))
    pdf = INV_SQRT_2PI * jnp.exp(-0.5 * v * v)
    return phi, pdf


def _sum_rows(v):
    return jnp.sum(v, axis=0, keepdims=True)


def _window_sums(ext, toward_later):
    n = ext.shape[0]

    def shifted(v, k):
        return pltpu.roll(v, (n - k) if toward_later else k, 0)

    s2 = ext + shifted(ext, 1)
    r4 = s2[:, HEAD:]
    s4 = r4 + shifted(r4, 2)
    r8 = s4[:, HEAD:]
    s8 = r8 + shifted(r8, 4)
    r16 = s8[:, HEAD:]
    s16 = r16 + shifted(r16, 8)
    return jnp.concatenate([s2[:, :HEAD], s4[:, :HEAD], s8[:, :HEAD], s16], axis=1)


def _window_counts(row0):
    t1 = row0 + 1 + lax.broadcasted_iota(jnp.int32, (ROWS, D_POOL), 0)
    lane = lax.broadcasted_iota(jnp.int32, (ROWS, D_POOL), 1)
    width = jnp.where(lane < HEAD, 2, jnp.where(lane < 2 * HEAD, 4, jnp.where(lane < 3 * HEAD, 8, 16)))
    return jnp.minimum(t1, width).astype(F32)


def _causal_mask():
    r = lax.broadcasted_iota(jnp.int32, (HEAD, HEAD), 0)
    s = lax.broadcasted_iota(jnp.int32, (HEAD, HEAD), 1)
    return r >= s


def _chunks_to_lanes(v):
    return jnp.concatenate([v[n * HEAD:(n + 1) * HEAD] for n in range(ROWS // HEAD)], axis=1)


def _lanes_to_chunks(v):
    return jnp.concatenate([v[:, n * HEAD:(n + 1) * HEAD] for n in range(ROWS // HEAD)], axis=0)


def _mixer(proj, halo, row0, wpool_ref, pscale, sgu_g_ref, sgu_b_ref, wsgu_ref, bsgu_t_ref):
    xa = proj[:, 0:512]
    ga = proj[:, 512:1024]
    u = proj[:, 1024:1536]
    v = proj[:, 1536:2048]
    gb = proj[:, 2048:2560]
    ext = jnp.concatenate([halo, xa], axis=0)
    win = _window_sums(ext, toward_later=False)[HALO:]
    cnt = _window_counts(row0)
    pooled = (win / cnt - xa).astype(BF16)
    pw = jnp.concatenate(
        [_dot(pooled[:, g * HEAD:(g + 1) * HEAD], wpool_ref[g].astype(BF16)) for g in range(N_HEAD)], axis=1)
    sig_a = _sigmoid(ga)
    ya = pw * pscale * (ga * sig_a)
    phi_u, pdf_u = _gelu_parts(u)
    phi_v, pdf_v = _gelu_parts(v)
    gu = u * phi_u
    gv = v * phi_v
    sig_b = _sigmoid(gb)
    silu_b = gb * sig_b
    mask = _causal_mask()
    vhat, rstd_v, vln_l, mixed = [], [], [], []
    for h in range(N_HEAD):
        vh, rh = _layer_norm(gv[:, h * HEAD:(h + 1) * HEAD])
        ln = (vh * sgu_g_ref[h:h + 1, :] + sgu_b_ref[h:h + 1, :]).astype(BF16)
        ln_l = _chunks_to_lanes(ln)
        wm = jnp.where(mask, wsgu_ref[h], 0.0).astype(BF16)
        mx = _lanes_to_chunks(_dot(wm, ln_l) + bsgu_t_ref[:, h:h + 1])
        vhat.append(vh)
        rstd_v.append(rh)
        vln_l.append(ln_l)
        mixed.append(mx)
    mixed = jnp.concatenate(mixed, axis=1)
    yb = gu * mixed * silu_b
    return dict(xa=xa, ga=ga, u=u, gb=gb, cnt=cnt, pooled=pooled, pw=pw, sig_a=sig_a, ya=ya, phi_u=phi_u, pdf_u=pdf_u,
                phi_v=phi_v, pdf_v=pdf_v, gu=gu, sig_b=sig_b, silu_b=silu_b, vhat=vhat, rstd_v=rstd_v, vln_l=vln_l,
                mixed=mixed, yb=yb, mask=mask)


def _const(shape, *index):
    lead = tuple(index) + (0,) * (len(shape) - len(index))
    return pl.BlockSpec(shape, lambda *_: lead)


def _layer_weight_specs(l):
    return [
        _const((None, N_HEAD, HEAD, HEAD), l),
        _const((DEPTH, D_POOL)),
        _const((None, N_HEAD, HEAD), l),
        _const((None, N_HEAD, HEAD), l),
        _const((None, N_HEAD, HEAD, HEAD), l),
        _const((None, HEAD, N_HEAD), l),
    ]


def _forward_layer(l, x, mod, w_in, w_out, small, ln_g, ln_b, target=None):
    last = target is not None

    def body(*refs):
        if last:
            (x_ref, mod_ref, win_ref, wout_ref, wpool_ref, pscale_ref, sgu_g_ref, sgu_b_ref, wsgu_ref, bsgu_t_ref,
             lng_ref, lnb_ref, tgt_ref, proj_ref, y_ref, out_ref, loss_ref, carry_ref) = refs
        else:
            (x_ref, mod_ref, win_ref, wout_ref, wpool_ref, pscale_ref, sgu_g_ref, sgu_b_ref, wsgu_ref, bsgu_t_ref,
             lng_ref, lnb_ref, proj_ref, y_ref, out_ref, carry_ref) = refs
        i = pl.program_id(0)

        @pl.when(i == 0)
        def _():
            carry_ref[...] = jnp.zeros_like(carry_ref)
            if last:
                loss_ref[...] = jnp.zeros_like(loss_ref)

        x = x_ref[...]
        shift, scale, gate = mod_ref[0:1, :], mod_ref[1:2, :], mod_ref[2:3, :]
        xn, _ = _layer_norm(x)
        h = xn * (1.0 + scale) + shift
        proj = _dot(h.astype(BF16), win_ref[...])
        proj_ref[...] = proj
        m = _mixer(proj, carry_ref[...], i * ROWS, wpool_ref, pscale_ref[l:l + 1, :], sgu_g_ref, sgu_b_ref, wsgu_ref,
                   bsgu_t_ref)
        carry_ref[...] = m["xa"][ROWS - HALO:]
        cat = jnp.concatenate([m["ya"], m["yb"]], axis=1).astype(BF16)
        y = _dot(cat, wout_ref[...])
        y_ref[...] = y
        zn, _ = _layer_norm(DEEPNORM_ALPHA * x + gate * y)
        out = zn * lng_ref[l:l + 1, :] + lnb_ref[l:l + 1, :]
        if last:
            err = out - tgt_ref[...]
            out_ref[...] = err * (1.0 / D_MODEL)
            loss_ref[...] += jnp.sum(err * err)
        else:
            out_ref[...] = out

    tile = pl.BlockSpec((ROWS, D_MODEL), lambda i: (i, 0))
    tile3 = pl.BlockSpec((None, ROWS, D_MODEL), lambda i: (0, i, 0))
    in_specs = [tile3 if x.ndim == 3 else tile, _const((None, 8, D_MODEL), l), _const((D_MODEL, D_PROJ)),
                _const((D_MODEL, D_MODEL))]
    in_specs += _layer_weight_specs(l) + [_const((DEPTH, D_MODEL)), _const((DEPTH, D_MODEL))]
    out_shape = [jax.ShapeDtypeStruct((SEQ, D_PROJ), F32), jax.ShapeDtypeStruct((SEQ, D_MODEL), F32),
                 jax.ShapeDtypeStruct((SEQ, D_MODEL), F32)]
    out_specs = [pl.BlockSpec((ROWS, D_PROJ), lambda i: (i, 0)), tile, tile]
    args = [x, mod, w_in, w_out, *small, ln_g, ln_b]
    if last:
        in_specs.append(tile3)
        args.append(target)
        out_shape.append(jax.ShapeDtypeStruct((8, HEAD), F32))
        out_specs.append(_const((8, HEAD)))
    return pl.pallas_call(
        body, name="fwd_last" if last else "fwd_first", grid=(N_TILE,), in_specs=in_specs, out_specs=out_specs,
        out_shape=out_shape, scratch_shapes=[pltpu.VMEM((HALO, D_POOL), F32)],
        compiler_params=pltpu.CompilerParams(dimension_semantics=("arbitrary",), vmem_limit_bytes=VMEM_LIMIT),
    )(*args)


def _backward_layer(l, dout, x, y, proj, mod, w_in, w_out, small, ln_g, sq=None, pack=None, dmod=None):
    creates = sq is not None

    def body(*refs):
        (dout_ref, x_ref, y_ref, proj_ref, halo_ref, mod_ref, win_ref, wout_ref, wpool_ref, pscale_ref,
         sgu_g_ref, sgu_b_ref, wsgu_ref, bsgu_t_ref, lng_ref) = refs[:15]
        dx_ref, h_ref, cat_ref, dy_ref, dproj_ref, pack_ref, dmod_ref, carry_ref = refs[-8:]
        i = pl.program_id(0)
        tile = N_TILE - 1 - i

        @pl.when(i == 0)
        def _():
            carry_ref[...] = jnp.zeros_like(carry_ref)
            pack_ref[...] = jnp.zeros_like(pack_ref)
            dmod_ref[...] = jnp.zeros_like(dmod_ref)
            if creates:
                pack_ref[PK_LOSS:PK_LOSS + 8, :] = refs[15][...]

        x = x_ref[...]
        y = y_ref[...]
        dout = dout_ref[...]
        pscale = pscale_ref[l:l + 1, :]
        shift, scale, gate = mod_ref[0:1, :], mod_ref[1:2, :], mod_ref[2:3, :]
        xn, rstd_x = _layer_norm(x)
        h = xn * (1.0 + scale) + shift
        h_ref[...] = h.astype(BF16)
        zn, rstd_z = _layer_norm(DEEPNORM_ALPHA * x + gate * y)
        g_ln_g = _sum_rows(dout * zn)
        g_ln_b = _sum_rows(dout)
        dz = _layer_norm_bwd(dout * lng_ref[l:l + 1, :], zn, rstd_z)
        d_gate = _sum_rows(dz * y)
        dy = (gate * dz).astype(BF16)
        dy_ref[...] = dy

        halo = jnp.where(tile > 0, halo_ref[...], 0.0)
        m = _mixer(proj_ref[...], halo, tile * ROWS, wpool_ref, pscale, sgu_g_ref, sgu_b_ref, wsgu_ref, bsgu_t_ref)
        cat_ref[...] = jnp.concatenate([m["ya"], m["yb"]], axis=1).astype(BF16)
        dcat = _dot(dy, wout_ref[...], NT)
        dya = dcat[:, :D_POOL]
        dyb = dcat[:, D_POOL:]

        ga, sig_a = m["ga"], m["sig_a"]
        dp = dya * (ga * sig_a)
        d_ga = dya * (m["pw"] * pscale) * (sig_a * (1.0 + ga * (1.0 - sig_a)))
        g_pscale = _sum_rows(dp * m["pw"])
        dpw = (dp * pscale).astype(BF16)
        dpooled = []
        for g in range(N_HEAD):
            cols = slice(g * HEAD, (g + 1) * HEAD)
            pack_ref[PK_W_POOL + g * HEAD:PK_W_POOL + (g + 1) * HEAD, :] += _dot(m["pooled"][:, cols], dpw[:, cols], TN)
            dpooled.append(_dot(dpw[:, cols], wpool_ref[g].astype(BF16), NT))
        dpooled = jnp.concatenate(dpooled, axis=1)
        q = dpooled / m["cnt"]
        ext = jnp.concatenate([q, carry_ref[...]], axis=0)
        d_xa = _window_sums(ext, toward_later=True)[:ROWS] - dpooled
        carry_ref[...] = q[:HALO]

        gu, mixed, silu_b, gb, sig_b = m["gu"], m["mixed"], m["silu_b"], m["gb"], m["sig_b"]
        d_mixed = dyb * gu * silu_b
        d_gu = dyb * mixed * silu_b
        d_gb = dyb * gu * mixed * (sig_b * (1.0 + gb * (1.0 - sig_b)))
        d_u = d_gu * (m["phi_u"] + m["u"] * m["pdf_u"])
        ones = jnp.ones((8, HEAD), F32)
        d_v = []
        for hd in range(N_HEAD):
            cols = slice(hd * HEAD, (hd + 1) * HEAD)
            dm = d_mixed[:, cols]
            dm_l = _chunks_to_lanes(dm.astype(BF16))
            g_w = _dot(dm_l, m["vln_l"][hd], NT)
            pack_ref[PK_W_SGU + hd * HEAD:PK_W_SGU + (hd + 1) * HEAD, :] += jnp.where(m["mask"], g_w, 0.0)
            dm_sum = dm[0:HEAD]
            for n in range(1, ROWS // HEAD):
                dm_sum = dm_sum + dm[n * HEAD:(n + 1) * HEAD]
            pack_ref[PK_B_SGU + hd:PK_B_SGU + hd + 1, :] += _dot_exact(ones, dm_sum, NT)[0:1]
            wm = jnp.where(m["mask"], wsgu_ref[hd], 0.0).astype(BF16)
            d_vln = _lanes_to_chunks(_dot(wm, dm_l, TN))
            vhat = m["vhat"][hd]
            pack_ref[PK_SGU_LN_G + hd:PK_SGU_LN_G + hd + 1, :] += _sum_rows(d_vln * vhat)
            pack_ref[PK_SGU_LN_B + hd:PK_SGU_LN_B + hd + 1, :] += _sum_rows(d_vln)
            d_v.append(_layer_norm_bwd(d_vln * sgu_g_ref[hd:hd + 1, :], vhat, m["rstd_v"][hd]))
        v = proj_ref[:, 1536:2048]
        d_v = jnp.concatenate(d_v, axis=1) * (m["phi_v"] + v * m["pdf_v"])

        dproj = jnp.concatenate([d_xa, d_ga, d_u, d_v, d_gb], axis=1).astype(BF16)
        dproj_ref[...] = dproj
        dh = _dot(dproj, win_ref[...], NT)
        d_scale = _sum_rows(dh * xn)
        d_shift = _sum_rows(dh)
        dx_ref[...] = DEEPNORM_ALPHA * dz + _layer_norm_bwd(dh * (1.0 + scale), xn, rstd_x)

        dmod_ref[0:1, :] += d_shift
        dmod_ref[1:2, :] += d_scale
        dmod_ref[2:3, :] += d_gate
        for g in range(N_HEAD):
            pack_ref[PK_POOL_SCALE + g:PK_POOL_SCALE + g + 1, :] += g_pscale[:, g * HEAD:(g + 1) * HEAD]
        for k in range(D_MODEL // HEAD):
            pack_ref[PK_LN_G + k:PK_LN_G + k + 1, :] += g_ln_g[:, k * HEAD:(k + 1) * HEAD]
            pack_ref[PK_LN_B + k:PK_LN_B + k + 1, :] += g_ln_b[:, k * HEAD:(k + 1) * HEAD]

    def rev(i):
        return (N_TILE - 1 - i, 0)

    tile = pl.BlockSpec((ROWS, D_MODEL), rev)
    tile3 = pl.BlockSpec((None, ROWS, D_MODEL), lambda i: (0, N_TILE - 1 - i, 0))
    halo = pl.BlockSpec((HALO, D_POOL), lambda i: (jnp.maximum((N_TILE - 1 - i) * (ROWS // HALO) - 1, 0), 0))
    in_specs = [tile, tile3 if x.ndim == 3 else tile, tile, pl.BlockSpec((ROWS, D_PROJ), rev), halo,
                _const((None, 8, D_MODEL), l), _const((D_MODEL, D_PROJ)), _const((D_MODEL, D_MODEL))]
    in_specs += _layer_weight_specs(l) + [_const((DEPTH, D_MODEL))]
    args = [dout, x, y, proj, proj, mod, w_in, w_out, *small, ln_g]
    aliases = {}
    hbm = pl.BlockSpec(memory_space=pl.ANY)
    if creates:
        in_specs.append(_const((8, HEAD)))
        args.append(sq)
    else:
        aliases = {len(args): 5, len(args) + 1: 6}
        in_specs += [hbm, hbm]
        args += [pack, dmod]
    out_shape = [jax.ShapeDtypeStruct((SEQ, D_MODEL), F32), jax.ShapeDtypeStruct((SEQ, D_MODEL), BF16),
                 jax.ShapeDtypeStruct((SEQ, D_MODEL), BF16), jax.ShapeDtypeStruct((SEQ, D_MODEL), BF16),
                 jax.ShapeDtypeStruct((SEQ, D_PROJ), BF16), jax.ShapeDtypeStruct((DEPTH, PK_ROWS, HEAD), F32),
                 jax.ShapeDtypeStruct((DEPTH, 8, D_MODEL), F32)]
    out_specs = [tile, tile, tile, tile, pl.BlockSpec((ROWS, D_PROJ), rev), _const((None, PK_ROWS, HEAD), l),
                 _const((None, 8, D_MODEL), l)]
    return pl.pallas_call(
        body, name="bwd_last" if creates else "bwd_first", grid=(N_TILE,), in_specs=in_specs, out_specs=out_specs,
        out_shape=out_shape, scratch_shapes=[pltpu.VMEM((HALO, D_POOL), F32)], input_output_aliases=aliases,
        compiler_params=pltpu.CompilerParams(dimension_semantics=("arbitrary",), vmem_limit_bytes=VMEM_LIMIT),
    )(*args)


def _weight_grad(name, a, b, n_block):
    m, n = a.shape[1], b.shape[1]

    def body(a_ref, b_ref, o_ref):
        o_ref[...] = _dot(a_ref[...], b_ref[...], TN)

    return pl.pallas_call(
        body, name=name, grid=(n // n_block,),
        in_specs=[pl.BlockSpec((SEQ, m), lambda i: (0, 0)), pl.BlockSpec((SEQ, n_block), lambda i: (0, i))],
        out_specs=pl.BlockSpec((m, n_block), lambda i: (0, i)),
        out_shape=jax.ShapeDtypeStruct((m, n), F32),
        compiler_params=pltpu.CompilerParams(dimension_semantics=("arbitrary",), vmem_limit_bytes=VMEM_LIMIT),
    )(a, b)


def _flip(v, f):
    return v + f - 2 * v * f


class _Place:
    def __init__(self):
        x, y, c = lax.axis_index("x"), lax.axis_index("y"), lax.axis_index("c")
        self.x, self.y, self.c = x, y, c
        self.chip = 2 * x + y
        self.dev = 4 * x + 2 * y + c
        self.sibling = (x, y, 1 - c)
        x1, y1 = _flip(x, 1 - c), _flip(y, c)
        x2, y2 = _flip(x, c), _flip(y, 1 - c)
        self.first = (x1, y1, c)
        self.second = (x2, y2, c)
        self.chip_first = 2 * x1 + y1
        self.chip_second = 2 * x2 + y2
        self.chip_far = 2 * (1 - x) + (1 - y)
        self.my_first_coord = jnp.where(c == 0, x, y)

    def first_coord(self, ch):
        return jnp.where(self.c == 0, ch // 2, ch % 2)

    def others(self):
        return [(_flip(self.x, (r >> 2) & 1), _flip(self.y, (r >> 1) & 1), _flip(self.c, r & 1)) for r in range(1, N_DEV)]

    def other_chips(self):
        return [(1 - self.x, self.y), (self.x, 1 - self.y), (1 - self.x, 1 - self.y)]


class _WeightGather:
    N_SEMS = 12

    def __init__(self, place, win, wout, send, recv):
        self.p, self.win, self.wout, self.send, self.recv = place, win, wout, send, recv
        self.started = []

    def _copies(self, k, ch, core, target):
        rows_in = pl.ds(pl.multiple_of(core * HALF_IN, HALF_IN), HALF_IN)
        cols_in = pl.ds(pl.multiple_of(ch * W_IN_COLS, 128), W_IN_COLS)
        rows_out = pl.ds(pl.multiple_of(ch * W_OUT_ROWS + core * HALF_OUT, HALF_OUT), HALF_OUT)
        r_in = self.win.at[rows_in, cols_in]
        r_out = self.wout.at[rows_out, :]
        return [pltpu.make_async_remote_copy(r_in, r_in, self.send.at[2 * k], self.recv.at[2 * k],
                                             device_id=target, device_id_type=MESH),
                pltpu.make_async_remote_copy(r_out, r_out, self.send.at[2 * k + 1], self.recv.at[2 * k + 1],
                                             device_id=target, device_id_type=MESH)]

    def _start(self, k, ch, core, target):
        for cp in self._copies(k, ch, core, target):
            cp.start()
            self.started.append(cp)

    def _landed(self, k):
        for cp in self._copies(k, 0, 0, self.p.sibling):
            cp.wait_recv()

    def start_first_round(self):
        self._start(0, self.p.chip, self.p.c, self.p.first)

    def start_second_round(self):
        p = self.p
        self._landed(0)
        self._start(1, p.chip, p.c, p.second)
        self._start(2, p.chip_first, p.c, p.second)
        self._start(3, p.chip_first, p.c, p.sibling)

    def pass_second_round(self):
        p = self.p
        self._landed(1)
        self._start(4, p.chip_second, p.c, p.sibling)
        self._landed(2)
        self._start(5, p.chip_far, p.c, p.sibling)

    def finish(self):
        for k in (3, 4, 5):
            self._landed(k)
        for cp in self.started:
            cp.wait_send()


def _prepare(c_vec, w_ada, b_ada, w_in, w_out):
    def body(c_ref, wada_ref, bada_ref, win_ref, wout_ref,
             win0, win1, wout0, wout1, mod_ref, c_all,
             win_bf, wout_bf, mod_mine, mod_all, g_send, g_recv, c_send, c_recv, mod_send, mod_recv, local_sem):
        p = _Place()
        win_bf[...] = win_ref[...].astype(BF16)
        wout_bf[...] = wout_ref[...].astype(BF16)
        cols = pl.ds(pl.multiple_of(p.chip * W_IN_COLS, 128), W_IN_COLS)
        rows = pl.ds(pl.multiple_of(p.chip * W_OUT_ROWS, W_OUT_ROWS), W_OUT_ROWS)
        own = [pltpu.make_async_copy(win_bf.at[0], win0.at[:, cols], local_sem.at[0]),
               pltpu.make_async_copy(wout_bf.at[0], wout0.at[rows, :], local_sem.at[1]),
               pltpu.make_async_copy(win_bf.at[1], win1.at[:, cols], local_sem.at[2]),
               pltpu.make_async_copy(wout_bf.at[1], wout1.at[rows, :], local_sem.at[3])]
        for cp in own:
            cp.start()

        c_all[pl.ds(p.dev, 1), :] = c_ref[...]
        c_copies = [pltpu.make_async_remote_copy(c_ref, c_all.at[pl.ds(p.dev, 1), :], c_send.at[r], c_recv.at[r],
                                                 device_id=d, device_id_type=MESH) for r, d in enumerate(p.others())]
        for cp in c_copies:
            cp.start()
        own[0].wait()
        own[1].wait()
        gather = _WeightGather(p, win0, wout0, g_send, g_recv)
        gather.start_first_round()
        for cp in c_copies:
            cp.wait()

        cv = c_all[...]
        silu_c = (cv * _sigmoid(cv)).astype(BF16)
        for l in range(DEPTH):
            mod_mine[l] = _dot(silu_c, wada_ref[l].astype(BF16))
        mod_all[p.chip] = mod_mine[...]
        m_copies = [pltpu.make_async_remote_copy(mod_mine, mod_all.at[p.chip], mod_send.at[k], mod_recv.at[k],
                                                 device_id=(px, py, p.c), device_id_type=MESH)
                    for k, (px, py) in enumerate(p.other_chips())]
        for cp in m_copies:
            cp.start()
        gather.start_second_round()
        for cp in m_copies:
            cp.wait()
        mod_ref[...] = jnp.zeros_like(mod_ref)
        for l in range(DEPTH):
            full = jnp.concatenate([mod_all[ch, l, pl.ds(p.dev, 1), :] for ch in range(N_CHIP)], axis=1) + bada_ref[l:l + 1, :]
            for k in range(3):
                mod_ref[l, k:k + 1, :] = full[:, k * D_MODEL:(k + 1) * D_MODEL]
        gather.pass_second_round()
        gather.finish()
        own[2].wait()
        own[3].wait()

    vmem = pl.BlockSpec(memory_space=pltpu.VMEM)
    hbm = pl.BlockSpec(memory_space=pl.ANY)
    w_in_shape = jax.ShapeDtypeStruct((D_MODEL, D_PROJ), BF16)
    w_out_shape = jax.ShapeDtypeStruct((D_MODEL, D_MODEL), BF16)
    return pl.pallas_call(
        body, name="prepare",
        in_specs=[vmem, vmem, vmem, vmem, vmem],
        out_specs=[hbm, hbm, hbm, hbm, vmem, vmem],
        out_shape=[w_in_shape, w_in_shape, w_out_shape, w_out_shape,
                   jax.ShapeDtypeStruct((DEPTH, 8, D_MODEL), F32), jax.ShapeDtypeStruct((N_DEV, D_MODEL), F32)],
        scratch_shapes=[
            pltpu.VMEM((DEPTH, D_MODEL, W_IN_COLS), BF16), pltpu.VMEM((DEPTH, W_OUT_ROWS, D_MODEL), BF16),
            pltpu.VMEM((DEPTH, N_DEV, W_ADA_COLS), F32), pltpu.VMEM((N_CHIP, DEPTH, N_DEV, W_ADA_COLS), F32),
            pltpu.SemaphoreType.DMA((_WeightGather.N_SEMS,)), pltpu.SemaphoreType.DMA((_WeightGather.N_SEMS,)),
            pltpu.SemaphoreType.DMA((7,)), pltpu.SemaphoreType.DMA((7,)),
            pltpu.SemaphoreType.DMA((3,)), pltpu.SemaphoreType.DMA((3,)),
            pltpu.SemaphoreType.DMA((4,)),
        ],
        compiler_params=pltpu.CompilerParams(vmem_limit_bytes=VMEM_LIMIT),
    )(c_vec, w_ada, b_ada, w_in, w_out)


def _gather_layer(w_in_l, w_out_l):
    def body(win_in, wout_in, win, wout, send, recv):
        gather = _WeightGather(_Place(), win, wout, send, recv)
        gather.start_first_round()
        gather.start_second_round()
        gather.pass_second_round()
        gather.finish()

    hbm = pl.BlockSpec(memory_space=pl.ANY)
    return pl.pallas_call(
        body, name="gather_last", in_specs=[hbm, hbm], out_specs=[hbm, hbm],
        out_shape=[jax.ShapeDtypeStruct(w_in_l.shape, BF16), jax.ShapeDtypeStruct(w_out_l.shape, BF16)],
        input_output_aliases={0: 0, 1: 1},
        scratch_shapes=[pltpu.SemaphoreType.DMA((_WeightGather.N_SEMS,)), pltpu.SemaphoreType.DMA((_WeightGather.N_SEMS,))],
    )(w_in_l, w_out_l)


class _GradReduce:
    SCRATCH = [
        pltpu.VMEM((HALF_IN, D_PROJ), F32), pltpu.VMEM((HALF_IN, D_PROJ), F32),
        pltpu.VMEM((N_CHIP, HALF_OUT, D_MODEL), F32), pltpu.VMEM((N_CHIP, HALF_OUT, D_MODEL), F32),
        pltpu.VMEM((N_CHIP, HALF_IN, W_IN_COLS), BF16), pltpu.VMEM((N_CHIP, HALF_OUT, D_MODEL), BF16),
        pltpu.VMEM((N_CHIP, HALF_IN, W_IN_COLS), BF16), pltpu.VMEM((N_CHIP, HALF_OUT, D_MODEL), BF16),
        pltpu.VMEM((HALF_IN, W_IN_COLS), BF16), pltpu.VMEM((HALF_OUT, D_MODEL), BF16),
        pltpu.VMEM((HALF_IN, W_IN_COLS), F32), pltpu.VMEM((HALF_OUT, D_MODEL), F32),
        pltpu.SemaphoreType.DMA((10,)),
        pltpu.SemaphoreType.DMA((2 * N_CHIP,)), pltpu.SemaphoreType.DMA((2 * N_CHIP,)),
        pltpu.SemaphoreType.DMA((2 * N_CHIP,)), pltpu.SemaphoreType.DMA((2,)),
    ]

    def __init__(self, place, gw_in, gw_out, fin_in, fin_out, scratch):
        self.p, self.gw_in, self.gw_out, self.fin_in, self.fin_out = place, gw_in, gw_out, fin_in, fin_out
        (self.a_in, self.b_in, self.a_out, self.b_out, self.st_in, self.st_out, self.r1_in, self.r1_out,
         self.r2_in, self.r2_out, self.f_in, self.f_out, self.sem, self.s1, self.r1, self.s2, self.r2) = scratch
        c = place.c
        self.my_rows = pl.ds(pl.multiple_of(c * HALF_IN, HALF_IN), HALF_IN)
        self.sib_rows = pl.ds(pl.multiple_of((1 - c) * HALF_IN, HALF_IN), HALF_IN)

    def _to_sibling(self):
        c, sib = self.p.c, self.p.sibling
        return [pltpu.make_async_remote_copy(self.gw_in.at[self.sib_rows, :], self.b_in, self.sem.at[2], self.sem.at[3],
                                             device_id=sib, device_id_type=MESH),
                pltpu.make_async_remote_copy(self.gw_out.at[:, 1 - c], self.b_out, self.sem.at[4], self.sem.at[5],
                                             device_id=sib, device_id_type=MESH)]

    def _loads(self):
        return [pltpu.make_async_copy(self.gw_in.at[self.my_rows, :], self.a_in, self.sem.at[0]),
                pltpu.make_async_copy(self.gw_out.at[:, self.p.c], self.a_out, self.sem.at[1])]

    def start(self):
        for cp in self._loads() + self._to_sibling():
            cp.start()

    def _round1(self, ch):
        return [pltpu.make_async_remote_copy(self.st_in.at[ch], self.r1_in.at[ch], self.s1.at[2 * ch], self.r1.at[2 * ch],
                                             device_id=self.p.first, device_id_type=MESH),
                pltpu.make_async_remote_copy(self.st_out.at[ch], self.r1_out.at[ch], self.s1.at[2 * ch + 1],
                                             self.r1.at[2 * ch + 1], device_id=self.p.first, device_id_type=MESH)]

    def _round2(self, ch):
        return [pltpu.make_async_remote_copy(self.st_in.at[ch], self.r2_in, self.s2.at[2 * ch], self.r2.at[0],
                                             device_id=self.p.second, device_id_type=MESH),
                pltpu.make_async_remote_copy(self.st_out.at[ch], self.r2_out, self.s2.at[2 * ch + 1], self.r2.at[1],
                                             device_id=self.p.second, device_id_type=MESH)]

    def _stage(self, ch):
        self.st_in[ch] = self.a_in[:, ch * W_IN_COLS:(ch + 1) * W_IN_COLS].astype(BF16)
        self.st_out[ch] = self.a_out[ch].astype(BF16)

    def chip_sum_and_first_round(self):
        p = self.p
        for cp in self._loads():
            cp.wait()
        for cp in self._to_sibling():
            cp.wait()
        self.a_in[...] += self.b_in[...]
        self.a_out[...] += self.b_out[...]
        for ch in range(N_CHIP):
            @pl.when(p.first_coord(ch) != p.my_first_coord)
            def _(ch=ch):
                self._stage(ch)
                for cp in self._round1(ch):
                    cp.start()

    def second_round(self):
        p = self.p
        for ch in range(N_CHIP):
            @pl.when(p.first_coord(ch) == p.my_first_coord)
            def _(ch=ch):
                for cp in self._round1(ch):
                    cp.wait_recv()
                cols = slice(ch * W_IN_COLS, (ch + 1) * W_IN_COLS)
                self.a_in[:, cols] += self.r1_in[ch].astype(F32)
                self.a_out[ch] += self.r1_out[ch].astype(F32)

                @pl.when(ch != p.chip)
                def _():
                    self._stage(ch)
                    for cp in self._round2(ch):
                        cp.start()

    def finish_start(self):
        p = self.p
        for cp in self._round2(0):
            cp.wait_recv()
        for ch in range(N_CHIP):
            @pl.when(ch == p.chip)
            def _(ch=ch):
                self.f_in[...] = self.a_in[:, ch * W_IN_COLS:(ch + 1) * W_IN_COLS] + self.r2_in[...].astype(F32)
                self.f_out[...] = self.a_out[ch] + self.r2_out[...].astype(F32)
        for cp in self._finals():
            cp.start()

    def _finals(self):
        c, sib = self.p.c, self.p.sibling
        return [pltpu.make_async_copy(self.f_in, self.fin_in.at[self.my_rows, :], self.sem.at[0]),
                pltpu.make_async_copy(self.f_out, self.fin_out.at[c], self.sem.at[1]),
                pltpu.make_async_remote_copy(self.f_in, self.fin_in.at[self.my_rows, :], self.sem.at[6], self.sem.at[7],
                                             device_id=sib, device_id_type=MESH),
                pltpu.make_async_remote_copy(self.f_out, self.fin_out.at[c], self.sem.at[8], self.sem.at[9],
                                             device_id=sib, device_id_type=MESH)]

    def finish(self):
        p = self.p
        for cp in self._finals():
            cp.wait()
        for ch in range(N_CHIP):
            @pl.when(p.first_coord(ch) != p.my_first_coord)
            def _(ch=ch):
                for cp in self._round1(ch):
                    cp.wait_send()

            @pl.when(jnp.logical_and(p.first_coord(ch) == p.my_first_coord, ch != p.chip))
            def _(ch=ch):
                for cp in self._round2(ch):
                    cp.wait_send()


FIN_IN = jax.ShapeDtypeStruct((D_MODEL, W_IN_COLS), F32)
FIN_OUT = jax.ShapeDtypeStruct((2, HALF_OUT, D_MODEL), F32)


def _reduce_layer(gw_in, gw_out):
    def body(gw_in_ref, gw_out_ref, fin_in, fin_out, *scratch):
        red = _GradReduce(_Place(), gw_in_ref, gw_out_ref, fin_in, fin_out, scratch)
        red.start()
        red.chip_sum_and_first_round()
        red.second_round()
        red.finish_start()
        red.finish()

    hbm = pl.BlockSpec(memory_space=pl.ANY)
    return pl.pallas_call(
        body, name="reduce_last", in_specs=[hbm, hbm], out_specs=[hbm, hbm], out_shape=[FIN_IN, FIN_OUT],
        scratch_shapes=_GradReduce.SCRATCH,
        compiler_params=pltpu.CompilerParams(vmem_limit_bytes=VMEM_LIMIT),
    )(gw_in, gw_out.reshape(N_CHIP, 2, HALF_OUT, D_MODEL))


def _reduce_final(gw_in, gw_out, pack, dmod):
    def body(gw_in_ref, gw_out_ref, pack_ref, dmod_ref, fin_in, fin_out, pack_out, dmod_all,
             p_recv, p_sum, dm_st, pk_send, pk_recv, ag_send, ag_recv, dm_send, dm_recv, *scratch):
        p = _Place()
        dev = p.dev
        devices = p.others()
        red = _GradReduce(p, gw_in_ref, gw_out_ref, fin_in, fin_out, scratch)
        red.start()

        def chunk(ref, d):
            return ref.at[d // 4, pl.ds(pl.multiple_of((d % 4) * PK_CHUNK, 8), PK_CHUNK), :]

        p_recv[dev] = pack_ref[dev // 4, pl.ds(pl.multiple_of((dev % 4) * PK_CHUNK, 8), PK_CHUNK), :]
        pk = [pltpu.make_async_remote_copy(chunk(pack_ref, 4 * dx + 2 * dy + dc), p_recv.at[dev], pk_send.at[r], pk_recv.at[r],
                                           device_id=(dx, dy, dc), device_id_type=MESH)
              for r, (dx, dy, dc) in enumerate(devices)]
        for cp in pk:
            cp.start()
        for l in range(DEPTH):
            for k in range(3):
                for r in range(D_MODEL // HEAD):
                    dm_st[l, 8 * k + r] = dmod_ref[l, k:k + 1, r * HEAD:(r + 1) * HEAD]
        dmod_all[:, :, pl.ds(dev, 1), :] = dm_st[...]
        dm = [pltpu.make_async_remote_copy(dm_st, dmod_all.at[:, :, pl.ds(dev, 1), :], dm_send.at[r], dm_recv.at[r],
                                           device_id=d, device_id_type=MESH) for r, d in enumerate(devices)]
        for cp in dm:
            cp.start()
        for cp in pk:
            cp.wait()
        acc = p_recv[0]
        for d in range(1, N_DEV):
            acc = acc + p_recv[d]
        p_sum[...] = acc
        pack_out[dev // 4, pl.ds(pl.multiple_of((dev % 4) * PK_CHUNK, 8), PK_CHUNK), :] = acc
        ag = [pltpu.make_async_remote_copy(p_sum, chunk(pack_out, dev), ag_send.at[r], ag_recv.at[r],
                                           device_id=d, device_id_type=MESH) for r, d in enumerate(devices)]
        for cp in ag:
            cp.start()

        red.chip_sum_and_first_round()
        red.second_round()
        red.finish_start()
        for cp in dm + ag:
            cp.wait()
        red.finish()

    vmem = pl.BlockSpec(memory_space=pltpu.VMEM)
    hbm = pl.BlockSpec(memory_space=pl.ANY)
    return pl.pallas_call(
        body, name="reduce_first",
        in_specs=[hbm, hbm, vmem, vmem],
        out_specs=[hbm, hbm, vmem, vmem],
        out_shape=[FIN_IN, FIN_OUT, jax.ShapeDtypeStruct((DEPTH, PK_ROWS, HEAD), F32),
                   jax.ShapeDtypeStruct((DEPTH, 24, N_DEV, HEAD), F32)],
        scratch_shapes=[
            pltpu.VMEM((N_DEV, PK_CHUNK, HEAD), F32), pltpu.VMEM((PK_CHUNK, HEAD), F32),
            pltpu.VMEM((DEPTH, 24, 1, HEAD), F32),
            pltpu.SemaphoreType.DMA((7,)), pltpu.SemaphoreType.DMA((7,)),
            pltpu.SemaphoreType.DMA((7,)), pltpu.SemaphoreType.DMA((7,)),
            pltpu.SemaphoreType.DMA((7,)), pltpu.SemaphoreType.DMA((7,)),
        ] + _GradReduce.SCRATCH,
        compiler_params=pltpu.CompilerParams(vmem_limit_bytes=VMEM_LIMIT),
    )(gw_in, gw_out.reshape(N_CHIP, 2, HALF_OUT, D_MODEL), pack, dmod)


def _adamw(w, g, m, v):
    m = ADAM_B1 * m + (1.0 - ADAM_B1) * g
    v = ADAM_B2 * v + (1.0 - ADAM_B2) * (g * g)
    m_hat = m / (1.0 - ADAM_B1 ** ADAM_STEP)
    v_hat = v / (1.0 - ADAM_B2 ** ADAM_STEP)
    delta = -ADAM_LR * (m_hat / (jnp.sqrt(v_hat) + ADAM_EPS) + ADAM_WD * w)
    return delta, m, v


def _adam_sharded(name, w, g_first, g_last, m, v, rows):
    _, r, cols = w.shape
    n = r // rows

    def body(w_ref, g0_ref, g1_ref, m_ref, v_ref, g_out, d_out, m_out, v_out):
        @pl.when(pl.program_id(0) == 0)
        def _():
            g_out[...] = g0_ref[...]

        @pl.when(pl.program_id(0) == 1)
        def _():
            g_out[...] = g1_ref[...]

        d_out[...], m_out[...], v_out[...] = _adamw(w_ref[...], g_out[...], m_ref[...], v_ref[...])

    blk = pl.BlockSpec((None, rows, cols), lambda l, i: (l, i, 0))
    g0 = pl.BlockSpec((rows, cols), lambda l, i: (jnp.where(l == 0, i, n - 1), 0))
    g1 = pl.BlockSpec((rows, cols), lambda l, i: (jnp.where(l == 1, i, 0), 0))
    shape = jax.ShapeDtypeStruct(w.shape, F32)
    return pl.pallas_call(
        body, name=name, grid=(DEPTH, n), in_specs=[blk, g0, g1, blk, blk], out_specs=[blk] * 4, out_shape=[shape] * 4,
        compiler_params=pltpu.CompilerParams(dimension_semantics=("arbitrary", "arbitrary"), vmem_limit_bytes=VMEM_LIMIT),
    )(w, g_first, g_last, m, v)


def _adam_w_ada(w, m, v, c_all, dmod_all):
    rows = 256

    def body(c_ref, dm_ref, w_ref, m_ref, v_ref, g_out, d_out, m_out, v_out):
        l = pl.program_id(0)
        chip = 2 * lax.axis_index("x") + lax.axis_index("y")
        cv = c_ref[...]
        silu_c = (cv * _sigmoid(cv)).astype(BF16).astype(F32)
        pieces = []
        for k in range(W_ADA_COLS // HEAD):
            dk = dm_ref[l, 6 * chip + k].astype(BF16).astype(F32)
            pieces.append(_dot_exact(silu_c, dk, TN))
        g = jnp.concatenate(pieces, axis=1)
        g_out[...] = g
        d_out[...], m_out[...], v_out[...] = _adamw(w_ref[...], g, m_ref[...], v_ref[...])

    blk = pl.BlockSpec((None, rows, W_ADA_COLS), lambda l, i: (l, i, 0))
    shape = jax.ShapeDtypeStruct(w.shape, F32)
    return pl.pallas_call(
        body, name="adam_w_ada", grid=(DEPTH, D_MODEL // rows),
        in_specs=[pl.BlockSpec((N_DEV, rows), lambda l, i: (0, i)),
                  pl.BlockSpec((DEPTH, 24, N_DEV, HEAD), lambda l, i: (0, 0, 0, 0)), blk, blk, blk],
        out_specs=[blk] * 4, out_shape=[shape] * 4,
        compiler_params=pltpu.CompilerParams(dimension_semantics=("arbitrary", "arbitrary"), vmem_limit_bytes=VMEM_LIMIT),
    )(c_all, dmod_all, w, m, v)


def _adam_small(pack, dmod_all, weights, ms, vs):
    n = len(weights)

    def body(*refs):
        pack_ref, dm_ref = refs[0], refs[1]
        w_refs, m_refs, v_refs = refs[2:2 + n], refs[2 + n:2 + 2 * n], refs[2 + 2 * n:2 + 3 * n]
        outs = refs[2 + 3 * n:]
        g_refs, d_refs, nm_refs, nv_refs = outs[0:n], outs[n:2 * n], outs[2 * n:3 * n], outs[3 * n:4 * n]
        outs[4 * n][...] = pack_ref[DEPTH - 1, PK_LOSS:PK_LOSS + 1, 0:1] * (0.5 / D_MODEL)

        def lanes(l, row0, count):
            return jnp.concatenate([pack_ref[l, row0 + k:row0 + k + 1, :] for k in range(count)], axis=1)

        def update(idx, at, g):
            g_refs[idx][at] = g
            d_refs[idx][at], nm_refs[idx][at], nv_refs[idx][at] = _adamw(w_refs[idx][at], g, m_refs[idx][at], v_refs[idx][at])

        for l in range(DEPTH):
            row = (slice(l, l + 1), slice(None))
            g_b = None
            for d in range(N_DEV):
                part = dm_ref[l, :, d, :]
                g_b = part if g_b is None else g_b + part
            update(0, row, jnp.concatenate([g_b[k:k + 1, :] for k in range(24)], axis=1))
            for g in range(N_HEAD):
                update(1, (l, g), pack_ref[l, PK_W_POOL + g * HEAD:PK_W_POOL + (g + 1) * HEAD, :])
                update(5, (l, g), pack_ref[l, PK_W_SGU + g * HEAD:PK_W_SGU + (g + 1) * HEAD, :])
            update(2, row, lanes(l, PK_POOL_SCALE, N_HEAD))
            update(3, (l,), pack_ref[l, PK_SGU_LN_G:PK_SGU_LN_G + N_HEAD, :])
            update(4, (l,), pack_ref[l, PK_SGU_LN_B:PK_SGU_LN_B + N_HEAD, :])
            update(6, (l,), pack_ref[l, PK_B_SGU:PK_B_SGU + N_HEAD, :])
            update(7, row, lanes(l, PK_LN_G, D_MODEL // HEAD))
            update(8, row, lanes(l, PK_LN_B, D_MODEL // HEAD))

    vmem = pl.BlockSpec(memory_space=pltpu.VMEM)
    shapes = [jax.ShapeDtypeStruct(w.shape, F32) for w in weights]
    return pl.pallas_call(
        body, name="adam_small", in_specs=[vmem] * (2 + 3 * n), out_specs=[vmem] * (4 * n + 1),
        out_shape=shapes * 4 + [jax.ShapeDtypeStruct((1, 1), F32)],
        compiler_params=pltpu.CompilerParams(vmem_limit_bytes=VMEM_LIMIT),
    )(pack, dmod_all, *weights, *ms, *vs)


def kernel(x, c, w_ada, b_ada, w_in, w_pool, pool_scale, sgu_ln_g, sgu_ln_b, w_sgu, b_sgu, w_out, ln_g, ln_b, loss_target, m_w_ada, m_b_ada, m_w_in, m_w_pool, m_pool_scale, m_sgu_ln_g, m_sgu_ln_b, m_w_sgu, m_b_sgu, m_w_out, m_ln_g, m_ln_b, v_w_ada, v_b_ada, v_w_in, v_w_pool, v_pool_scale, v_sgu_ln_g, v_sgu_ln_b, v_w_sgu, v_b_sgu, v_w_out, v_ln_g, v_ln_b):
    w_in0, w_in1, w_out0, w_out1, mod, c_all = _prepare(c, w_ada, b_ada, w_in, w_out)
    w_in1, w_out1 = _gather_layer(w_in1, w_out1)
    small = (w_pool, pool_scale, sgu_ln_g, sgu_ln_b, w_sgu, jnp.swapaxes(b_sgu, 1, 2))

    proj0, y0, x1 = _forward_layer(0, x, mod, w_in0, w_out0, small, ln_g, ln_b)
    proj1, y1, dout, sq = _forward_layer(1, x1, mod, w_in1, w_out1, small, ln_g, ln_b, target=loss_target)

    dx1, h1, cat1, dy1, dproj1, pack, dmod = _backward_layer(1, dout, x1, y1, proj1, mod, w_in1, w_out1, small, ln_g, sq=sq)
    g_in1, g_out1 = _reduce_layer(_weight_grad("wgrad_in_last", h1, dproj1, W_IN_COLS),
                                  _weight_grad("wgrad_out_last", cat1, dy1, D_MODEL // 2))
    dx0, h0, cat0, dy0, dproj0, pack, dmod = _backward_layer(0, dx1, x, y0, proj0, mod, w_in0, w_out0, small, ln_g,
                                                             pack=pack, dmod=dmod)
    g_in0, g_out0, pack, dmod_all = _reduce_final(_weight_grad("wgrad_in_first", h0, dproj0, W_IN_COLS),
                                                  _weight_grad("wgrad_out_first", cat0, dy0, D_MODEL // 2), pack, dmod)

    ada = _adam_w_ada(w_ada, m_w_ada, v_w_ada, c_all, dmod_all)
    win = _adam_sharded("adam_w_in", w_in, g_in0, g_in1, m_w_in, v_w_in, 256)
    wout = _adam_sharded("adam_w_out", w_out, g_out0.reshape(W_OUT_ROWS, D_MODEL), g_out1.reshape(W_OUT_ROWS, D_MODEL),
                         m_w_out, v_w_out, 256)
    small_w = (b_ada, w_pool, pool_scale, sgu_ln_g, sgu_ln_b, w_sgu, b_sgu, ln_g, ln_b)
    small_m = (m_b_ada, m_w_pool, m_pool_scale, m_sgu_ln_g, m_sgu_ln_b, m_w_sgu, m_b_sgu, m_ln_g, m_ln_b)
    small_v = (v_b_ada, v_w_pool, v_pool_scale, v_sgu_ln_g, v_sgu_ln_b, v_w_sgu, v_b_sgu, v_ln_g, v_ln_b)
    res = _adam_small(pack, dmod_all, small_w, small_m, small_v)
    n = len(small_w)
    loss = res[4 * n].reshape(())

    def ordered(k):
        s = res[k * n:(k + 1) * n]
        return (ada[k], s[0], win[k], s[1], s[2], s[3], s[4], s[5], s[6], wout[k], s[7], s[8])

    return (loss, dx0[None], *ordered(0), *ordered(1), *ordered(2), *ordered(3))
```

```python
import jax
import jax.numpy as jnp
from jax import lax
from jax.experimental import pallas as pl
from jax.experimental.pallas import tpu as pltpu

F32 = jnp.float32
BF16 = jnp.bfloat16
MESH = pl.DeviceIdType.MESH

N_DEV = 8
N_CHIP = 4
DEPTH = 2
SEQ = 2048
D_MODEL = 1024
D_POOL = 512
D_PROJ = 2560
HEAD = 128
N_HEAD = 4
ROWS = 256
N_TILE = SEQ // ROWS
HALO = 16
W_IN_COLS = D_PROJ // N_CHIP
W_OUT_ROWS = D_MODEL // N_CHIP
W_ADA_COLS = 3 * D_MODEL // N_CHIP
HALF_IN = D_MODEL // 2
HALF_OUT = W_OUT_ROWS // 2
DEEPNORM_ALPHA = (2.0 * DEPTH) ** 0.25
LN_EPS = 1e-5
INV_SQRT2 = 0.7071067811865476
INV_SQRT_2PI = 0.3989422804014327

ADAM_LR = 0.001
ADAM_B1 = 0.9
ADAM_B2 = 0.999
ADAM_EPS = 1e-08
ADAM_WD = 0.01
ADAM_STEP = 10

PK_W_POOL = 0
PK_W_SGU = 512
PK_POOL_SCALE = 1024
PK_SGU_LN_G = 1032
PK_SGU_LN_B = 1040
PK_B_SGU = 1048
PK_LN_G = 1056
PK_LN_B = 1064
PK_LOSS = 1072
PK_ROWS = 1088
PK_CHUNK = DEPTH * PK_ROWS // N_DEV

VMEM_LIMIT = 56 * 1024 * 1024

GATHER_SECOND_ROUND_STEP = 3
REDUCE_FIRST_ROUND_STEP = 2
REDUCE_SECOND_ROUND_STEP = 5

NN = (((1,), (0,)), ((), ()))
NT = (((1,), (1,)), ((), ()))
TN = (((0,), (0,)), ((), ()))


def _dot(a, b, dims=NN):
    return lax.dot_general(a, b, dims, preferred_element_type=F32)


def _dot_exact(a, b, dims=NN):
    return lax.dot_general(a, b, dims, preferred_element_type=F32, precision=lax.Precision.HIGHEST)


def _layer_norm(v):
    mu = jnp.mean(v, axis=-1, keepdims=True)
    d = v - mu
    var = jnp.mean(d * d, axis=-1, keepdims=True)
    rstd = lax.rsqrt(var + LN_EPS)
    return d * rstd, rstd


def _layer_norm_bwd(dvhat, vhat, rstd):
    m1 = jnp.mean(dvhat, axis=-1, keepdims=True)
    m2 = jnp.mean(dvhat * vhat, axis=-1, keepdims=True)
    return rstd * (dvhat - m1 - vhat * m2)


def _sigmoid(v):
    return 1.0 / (1.0 + jnp.exp(-v))


def _gelu_parts(v):
    phi = 0.5 * (1.0 + lax.erf(v * INV_SQRT2))
    pdf = INV_SQRT_2PI * jnp.exp(-0.5 * v * v)
    return phi, pdf


def _sum_rows(v):
    return jnp.sum(v, axis=0, keepdims=True)


def _window_sums(ext, toward_later):
    n = ext.shape[0]

    def shifted(v, k):
        return pltpu.roll(v, (n - k) if toward_later else k, 0)

    s2 = ext + shifted(ext, 1)
    r4 = s2[:, HEAD:]
    s4 = r4 + shifted(r4, 2)
    r8 = s4[:, HEAD:]
    s8 = r8 + shifted(r8, 4)
    r16 = s8[:, HEAD:]
    s16 = r16 + shifted(r16, 8)
    return jnp.concatenate([s2[:, :HEAD], s4[:, :HEAD], s8[:, :HEAD], s16], axis=1)


def _window_counts(row0):
    t1 = row0 + 1 + lax.broadcasted_iota(jnp.int32, (ROWS, D_POOL), 0)
    lane = lax.broadcasted_iota(jnp.int32, (ROWS, D_POOL), 1)
    width = jnp.where(lane < HEAD, 2, jnp.where(lane < 2 * HEAD, 4, jnp.where(lane < 3 * HEAD, 8, 16)))
    return jnp.minimum(t1, width).astype(F32)


def _causal_mask():
    r = lax.broadcasted_iota(jnp.int32, (HEAD, HEAD), 0)
    s = lax.broadcasted_iota(jnp.int32, (HEAD, HEAD), 1)
    return r >= s


def _chunks_to_lanes(v):
    return jnp.concatenate([v[n * HEAD:(n + 1) * HEAD] for n in range(ROWS // HEAD)], axis=1)


def _lanes_to_chunks(v):
    return jnp.concatenate([v[:, n * HEAD:(n + 1) * HEAD] for n in range(ROWS // HEAD)], axis=0)


def _mixer(proj, halo, row0, wpool_ref, pscale, sgu_g_ref, sgu_b_ref, wsgu_ref, bsgu_t_ref):
    xa = proj[:, 0:512]
    ga = proj[:, 512:1024]
    u = proj[:, 1024:1536]
    v = proj[:, 1536:2048]
    gb = proj[:, 2048:2560]
    ext = jnp.concatenate([halo, xa], axis=0)
    win = _window_sums(ext, toward_later=False)[HALO:]
    cnt = _window_counts(row0)
    pooled = (win / cnt - xa).astype(BF16)
    pw = jnp.concatenate(
        [_dot(pooled[:, g * HEAD:(g + 1) * HEAD], wpool_ref[g].astype(BF16)) for g in range(N_HEAD)], axis=1)
    sig_a = _sigmoid(ga)
    ya = pw * pscale * (ga * sig_a)
    phi_u, pdf_u = _gelu_parts(u)
    phi_v, pdf_v = _gelu_parts(v)
    gu = u * phi_u
    gv = v * phi_v
    sig_b = _sigmoid(gb)
    silu_b = gb * sig_b
    mask = _causal_mask()
    vhat, rstd_v, vln_l, mixed = [], [], [], []
    for h in range(N_HEAD):
        vh, rh = _layer_norm(gv[:, h * HEAD:(h + 1) * HEAD])
        ln = (vh * sgu_g_ref[h:h + 1, :] + sgu_b_ref[h:h + 1, :]).astype(BF16)
        ln_l = _chunks_to_lanes(ln)
        wm = jnp.where(mask, wsgu_ref[h], 0.0).astype(BF16)
        mx = _lanes_to_chunks(_dot(wm, ln_l) + bsgu_t_ref[:, h:h + 1])
        vhat.append(vh)
        rstd_v.append(rh)
        vln_l.append(ln_l)
        mixed.append(mx)
    mixed = jnp.concatenate(mixed, axis=1)
    yb = gu * mixed * silu_b
    return dict(xa=xa, ga=ga, u=u, gb=gb, cnt=cnt, pooled=pooled, pw=pw, sig_a=sig_a, ya=ya, phi_u=phi_u, pdf_u=pdf_u,
                phi_v=phi_v, pdf_v=pdf_v, gu=gu, sig_b=sig_b, silu_b=silu_b, vhat=vhat, rstd_v=rstd_v, vln_l=vln_l,
                mixed=mixed, yb=yb, mask=mask)


def _const(shape, *index):
    lead = tuple(index) + (0,) * (len(shape) - len(index))
    return pl.BlockSpec(shape, lambda *_: lead)


def _const_in(shape, *index):
    lead = tuple(index) + (0,) * (len(shape) - len(index))
    return pl.BlockSpec(shape, lambda *_: lead, pipeline_mode=pl.Buffered(1))


def _layer_weight_specs(l):
    return [
        _const_in((None, N_HEAD, HEAD, HEAD), l),
        _const_in((DEPTH, D_POOL)),
        _const_in((None, N_HEAD, HEAD), l),
        _const_in((None, N_HEAD, HEAD), l),
        _const_in((None, N_HEAD, HEAD, HEAD), l),
        _const_in((None, HEAD, N_HEAD), l),
    ]


def _forward_layer(l, x, mod, w_in, w_out, small, ln_g, ln_b, target=None, next_weights=None):
    last = target is not None
    gathers = next_weights is not None

    def body(*refs):
        if last:
            (x_ref, mod_ref, win_ref, wout_ref, wpool_ref, pscale_ref, sgu_g_ref, sgu_b_ref, wsgu_ref, bsgu_t_ref,
             lng_ref, lnb_ref, tgt_ref, proj_ref, y_ref, out_ref, loss_ref, carry_ref) = refs
        elif gathers:
            (x_ref, mod_ref, win_ref, wout_ref, wpool_ref, pscale_ref, sgu_g_ref, sgu_b_ref, wsgu_ref, bsgu_t_ref,
             lng_ref, lnb_ref, _, _, proj_ref, y_ref, out_ref, next_in, next_out, carry_ref, g_send, g_recv) = refs
            gather = _WeightGather(_Place(), next_in, next_out, g_send, g_recv)
        else:
            (x_ref, mod_ref, win_ref, wout_ref, wpool_ref, pscale_ref, sgu_g_ref, sgu_b_ref, wsgu_ref, bsgu_t_ref,
             lng_ref, lnb_ref, proj_ref, y_ref, out_ref, carry_ref) = refs
        i = pl.program_id(0)

        @pl.when(i == 0)
        def _():
            carry_ref[...] = jnp.zeros_like(carry_ref)
            if last:
                loss_ref[...] = jnp.zeros_like(loss_ref)
            if gathers:
                gather.start_first_round()

        if gathers:
            @pl.when(i == GATHER_SECOND_ROUND_STEP)
            def _():
                gather.start_second_round()

        x = x_ref[...]
        shift, scale, gate = mod_ref[0:1, :], mod_ref[1:2, :], mod_ref[2:3, :]
        xn, _ = _layer_norm(x)
        h = xn * (1.0 + scale) + shift
        proj = _dot(h.astype(BF16), win_ref[...])
        proj_ref[...] = proj
        m = _mixer(proj, carry_ref[...], i * ROWS, wpool_ref, pscale_ref[l:l + 1, :], sgu_g_ref, sgu_b_ref, wsgu_ref,
                   bsgu_t_ref)
        carry_ref[...] = m["xa"][ROWS - HALO:]
        cat = jnp.concatenate([m["ya"], m["yb"]], axis=1).astype(BF16)
        y = _dot(cat, wout_ref[...])
        y_ref[...] = y
        zn, _ = _layer_norm(DEEPNORM_ALPHA * x + gate * y)
        out = zn * lng_ref[l:l + 1, :] + lnb_ref[l:l + 1, :]
        if last:
            err = out - tgt_ref[...]
            out_ref[...] = err * (1.0 / D_MODEL)
            loss_ref[...] += jnp.sum(err * err)
        else:
            out_ref[...] = out

        if gathers:
            @pl.when(i == N_TILE - 1)
            def _():
                gather.pass_second_round()
                gather.finish()

    tile = pl.BlockSpec((ROWS, D_MODEL), lambda i: (i, 0))
    tile3 = pl.BlockSpec((None, ROWS, D_MODEL), lambda i: (0, i, 0))
    in_specs = [tile3 if x.ndim == 3 else tile, _const_in((None, 8, D_MODEL), l), _const_in((D_MODEL, D_PROJ)),
                _const_in((D_MODEL, D_MODEL))]
    in_specs += _layer_weight_specs(l) + [_const_in((DEPTH, D_MODEL)), _const_in((DEPTH, D_MODEL))]
    out_shape = [jax.ShapeDtypeStruct((SEQ, D_PROJ), F32), jax.ShapeDtypeStruct((SEQ, D_MODEL), F32),
                 jax.ShapeDtypeStruct((SEQ, D_MODEL), F32)]
    out_specs = [pl.BlockSpec((ROWS, D_PROJ), lambda i: (i, 0)), tile, tile]
    args = [x, mod, w_in, w_out, *small, ln_g, ln_b]
    scratch = [pltpu.VMEM((HALO, D_POOL), F32)]
    aliases = {}
    if last:
        in_specs.append(tile3)
        args.append(target)
        out_shape.append(jax.ShapeDtypeStruct((8, HEAD), F32))
        out_specs.append(_const((8, HEAD)))
    if gathers:
        hbm = pl.BlockSpec(memory_space=pl.ANY)
        aliases = {len(args): len(out_shape), len(args) + 1: len(out_shape) + 1}
        in_specs += [hbm, hbm]
        args += list(next_weights)
        out_shape += [jax.ShapeDtypeStruct(w.shape, BF16) for w in next_weights]
        out_specs += [hbm, hbm]
        scratch += [pltpu.SemaphoreType.DMA((_WeightGather.N_SEMS,)), pltpu.SemaphoreType.DMA((_WeightGather.N_SEMS,))]
    return pl.pallas_call(
        body, name="fwd_last" if last else "fwd_first", grid=(N_TILE,), in_specs=in_specs, out_specs=out_specs,
        out_shape=out_shape, scratch_shapes=scratch, input_output_aliases=aliases,
        compiler_params=pltpu.CompilerParams(dimension_semantics=("arbitrary",), vmem_limit_bytes=VMEM_LIMIT),
    )(*args)


def _backward_layer(l, dout, x, y, proj, mod, w_in, w_out, small, ln_g, sq=None, pack=None, dmod=None, later_grads=None):
    creates = sq is not None
    n_out = 7 if creates else 9

    def body(*refs):
        (dout_ref, x_ref, y_ref, proj_ref, halo_ref, mod_ref, win_ref, wout_ref, wpool_ref, pscale_ref,
         sgu_g_ref, sgu_b_ref, wsgu_ref, bsgu_t_ref, lng_ref) = refs[:15]
        n_in = 16 if creates else 19
        dx_ref, h_ref, cat_ref, dy_ref, dproj_ref, pack_ref, dmod_ref = refs[n_in:n_in + 7]
        carry_ref = refs[n_in + n_out]
        i = pl.program_id(0)
        tile = N_TILE - 1 - i
        if not creates:
            red = _GradReduce(_Place(), refs[17], refs[18], refs[n_in + 7], refs[n_in + 8], refs[n_in + n_out + 1:])

        @pl.when(i == 0)
        def _():
            carry_ref[...] = jnp.zeros_like(carry_ref)
            pack_ref[...] = jnp.zeros_like(pack_ref)
            dmod_ref[...] = jnp.zeros_like(dmod_ref)
            if creates:
                pack_ref[PK_LOSS:PK_LOSS + 8, :] = refs[15][...]
            else:
                red.start()

        if not creates:
            @pl.when(i == REDUCE_FIRST_ROUND_STEP)
            def _():
                red.chip_sum_and_first_round()

            @pl.when(i == REDUCE_SECOND_ROUND_STEP)
            def _():
                red.second_round()

        x = x_ref[...]
        y = y_ref[...]
        dout = dout_ref[...]
        pscale = pscale_ref[l:l + 1, :]
        shift, scale, gate = mod_ref[0:1, :], mod_ref[1:2, :], mod_ref[2:3, :]
        xn, rstd_x = _layer_norm(x)
        h = xn * (1.0 + scale) + shift
        h_ref[...] = h.astype(BF16)
        zn, rstd_z = _layer_norm(DEEPNORM_ALPHA * x + gate * y)
        g_ln_g = _sum_rows(dout * zn)
        g_ln_b = _sum_rows(dout)
        dz = _layer_norm_bwd(dout * lng_ref[l:l + 1, :], zn, rstd_z)
        d_gate = _sum_rows(dz * y)
        dy = (gate * dz).astype(BF16)
        dy_ref[...] = dy

        halo = jnp.where(tile > 0, halo_ref[...], 0.0)
        m = _mixer(proj_ref[...], halo, tile * ROWS, wpool_ref, pscale, sgu_g_ref, sgu_b_ref, wsgu_ref, bsgu_t_ref)
        cat_ref[...] = jnp.concatenate([m["ya"], m["yb"]], axis=1).astype(BF16)
        dcat = _dot(dy, wout_ref[...], NT)
        dya = dcat[:, :D_POOL]
        dyb = dcat[:, D_POOL:]

        ga, sig_a = m["ga"], m["sig_a"]
        dp = dya * (ga * sig_a)
        d_ga = dya * (m["pw"] * pscale) * (sig_a * (1.0 + ga * (1.0 - sig_a)))
        g_pscale = _sum_rows(dp * m["pw"])
        dpw = (dp * pscale).astype(BF16)
        dpooled = []
        for g in range(N_HEAD):
            cols = slice(g * HEAD, (g + 1) * HEAD)
            pack_ref[PK_W_POOL + g * HEAD:PK_W_POOL + (g + 1) * HEAD, :] += _dot(m["pooled"][:, cols], dpw[:, cols], TN)
            dpooled.append(_dot(dpw[:, cols], wpool_ref[g].astype(BF16), NT))
        dpooled = jnp.concatenate(dpooled, axis=1)
        q = dpooled / m["cnt"]
        ext = jnp.concatenate([q, carry_ref[...]], axis=0)
        d_xa = _window_sums(ext, toward_later=True)[:ROWS] - dpooled
        carry_ref[...] = q[:HALO]

        gu, mixed, silu_b, gb, sig_b = m["gu"], m["mixed"], m["silu_b"], m["gb"], m["sig_b"]
        d_mixed = dyb * gu * silu_b
        d_gu = dyb * mixed * silu_b
        d_gb = dyb * gu * mixed * (sig_b * (1.0 + gb * (1.0 - sig_b)))
        d_u = d_gu * (m["phi_u"] + m["u"] * m["pdf_u"])
        ones = jnp.ones((8, HEAD), F32)
        d_v = []
        for hd in range(N_HEAD):
            cols = slice(hd * HEAD, (hd + 1) * HEAD)
            dm = d_mixed[:, cols]
            dm_l = _chunks_to_lanes(dm.astype(BF16))
            g_w = _dot(dm_l, m["vln_l"][hd], NT)
            pack_ref[PK_W_SGU + hd * HEAD:PK_W_SGU + (hd + 1) * HEAD, :] += jnp.where(m["mask"], g_w, 0.0)
            dm_sum = dm[0:HEAD]
            for n in range(1, ROWS // HEAD):
                dm_sum = dm_sum + dm[n * HEAD:(n + 1) * HEAD]
            pack_ref[PK_B_SGU + hd:PK_B_SGU + hd + 1, :] += _dot_exact(ones, dm_sum, NT)[0:1]
            wm = jnp.where(m["mask"], wsgu_ref[hd], 0.0).astype(BF16)
            d_vln = _lanes_to_chunks(_dot(wm, dm_l, TN))
            vhat = m["vhat"][hd]
            pack_ref[PK_SGU_LN_G + hd:PK_SGU_LN_G + hd + 1, :] += _sum_rows(d_vln * vhat)
            pack_ref[PK_SGU_LN_B + hd:PK_SGU_LN_B + hd + 1, :] += _sum_rows(d_vln)
            d_v.append(_layer_norm_bwd(d_vln * sgu_g_ref[hd:hd + 1, :], vhat, m["rstd_v"][hd]))
        v = proj_ref[:, 1536:2048]
        d_v = jnp.concatenate(d_v, axis=1) * (m["phi_v"] + v * m["pdf_v"])

        dproj = jnp.concatenate([d_xa, d_ga, d_u, d_v, d_gb], axis=1).astype(BF16)
        dproj_ref[...] = dproj
        dh = _dot(dproj, win_ref[...], NT)
        d_scale = _sum_rows(dh * xn)
        d_shift = _sum_rows(dh)
        dx_ref[...] = DEEPNORM_ALPHA * dz + _layer_norm_bwd(dh * (1.0 + scale), xn, rstd_x)

        dmod_ref[0:1, :] += d_shift
        dmod_ref[1:2, :] += d_scale
        dmod_ref[2:3, :] += d_gate
        for g in range(N_HEAD):
            pack_ref[PK_POOL_SCALE + g:PK_POOL_SCALE + g + 1, :] += g_pscale[:, g * HEAD:(g + 1) * HEAD]
        for k in range(D_MODEL // HEAD):
            pack_ref[PK_LN_G + k:PK_LN_G + k + 1, :] += g_ln_g[:, k * HEAD:(k + 1) * HEAD]
            pack_ref[PK_LN_B + k:PK_LN_B + k + 1, :] += g_ln_b[:, k * HEAD:(k + 1) * HEAD]

        if not creates:
            @pl.when(i == N_TILE - 1)
            def _():
                red.finish_start()
                red.finish()

    def rev(i):
        return (N_TILE - 1 - i, 0)

    tile = pl.BlockSpec((ROWS, D_MODEL), rev)
    tile3 = pl.BlockSpec((None, ROWS, D_MODEL), lambda i: (0, N_TILE - 1 - i, 0))
    halo = pl.BlockSpec((HALO, D_POOL), lambda i: (jnp.maximum((N_TILE - 1 - i) * (ROWS // HALO) - 1, 0), 0))
    in_specs = [tile, tile3 if x.ndim == 3 else tile, tile, pl.BlockSpec((ROWS, D_PROJ), rev), halo,
                _const_in((None, 8, D_MODEL), l), _const_in((D_MODEL, D_PROJ)), _const_in((D_MODEL, D_MODEL))]
    in_specs += _layer_weight_specs(l) + [_const_in((DEPTH, D_MODEL))]
    args = [dout, x, y, proj, proj, mod, w_in, w_out, *small, ln_g]
    out_shape = [jax.ShapeDtypeStruct((SEQ, D_MODEL), F32), jax.ShapeDtypeStruct((SEQ, D_MODEL), BF16),
                 jax.ShapeDtypeStruct((SEQ, D_MODEL), BF16), jax.ShapeDtypeStruct((SEQ, D_MODEL), BF16),
                 jax.ShapeDtypeStruct((SEQ, D_PROJ), BF16), jax.ShapeDtypeStruct((DEPTH, PK_ROWS, HEAD), F32),
                 jax.ShapeDtypeStruct((DEPTH, 8, D_MODEL), F32)]
    out_specs = [tile, tile, tile, tile, pl.BlockSpec((ROWS, D_PROJ), rev), _const((None, PK_ROWS, HEAD), l),
                 _const((None, 8, D_MODEL), l)]
    scratch = [pltpu.VMEM((HALO, D_POOL), F32)]
    aliases = {}
    hbm = pl.BlockSpec(memory_space=pl.ANY)
    if creates:
        in_specs.append(_const_in((8, HEAD)))
        args.append(sq)
    else:
        aliases = {len(args): 5, len(args) + 1: 6}
        in_specs += [hbm, hbm, hbm, hbm]
        args += [pack, dmod, later_grads[0], later_grads[1].reshape(N_CHIP, 2, HALF_OUT, D_MODEL)]
        out_shape += [FIN_IN, FIN_OUT]
        out_specs += [hbm, hbm]
        scratch += _GradReduce.SCRATCH
    return pl.pallas_call(
        body, name="bwd_last" if creates else "bwd_first", grid=(N_TILE,), in_specs=in_specs, out_specs=out_specs,
        out_shape=out_shape, scratch_shapes=scratch, input_output_aliases=aliases,
        compiler_params=pltpu.CompilerParams(dimension_semantics=("arbitrary",), vmem_limit_bytes=VMEM_LIMIT),
    )(*args)


def _weight_grad(name, a, b, n_block):
    m, n = a.shape[1], b.shape[1]

    def body(a_ref, b_ref, o_ref):
        o_ref[...] = _dot(a_ref[...], b_ref[...], TN)

    return pl.pallas_call(
        body, name=name, grid=(n // n_block,),
        in_specs=[pl.BlockSpec((SEQ, m), lambda i: (0, 0)), pl.BlockSpec((SEQ, n_block), lambda i: (0, i))],
        out_specs=pl.BlockSpec((m, n_block), lambda i: (0, i)),
        out_shape=jax.ShapeDtypeStruct((m, n), F32),
        compiler_params=pltpu.CompilerParams(dimension_semantics=("arbitrary",), vmem_limit_bytes=VMEM_LIMIT),
    )(a, b)


def _flip(v, f):
    return v + f - 2 * v * f


class _Place:
    def __init__(self):
        x, y, c = lax.axis_index("x"), lax.axis_index("y"), lax.axis_index("c")
        self.x, self.y, self.c = x, y, c
        self.chip = 2 * x + y
        self.dev = 4 * x + 2 * y + c
        self.sibling = (x, y, 1 - c)
        x1, y1 = _flip(x, 1 - c), _flip(y, c)
        x2, y2 = _flip(x, c), _flip(y, 1 - c)
        self.first = (x1, y1, c)
        self.second = (x2, y2, c)
        self.chip_first = 2 * x1 + y1
        self.chip_second = 2 * x2 + y2
        self.chip_far = 2 * (1 - x) + (1 - y)
        self.my_first_coord = jnp.where(c == 0, x, y)

    def first_coord(self, ch):
        return jnp.where(self.c == 0, ch // 2, ch % 2)

    def others(self):
        return [(_flip(self.x, (r >> 2) & 1), _flip(self.y, (r >> 1) & 1), _flip(self.c, r & 1)) for r in range(1, N_DEV)]

    def other_chips(self):
        return [(1 - self.x, self.y), (self.x, 1 - self.y), (1 - self.x, 1 - self.y)]


class _WeightGather:
    N_SEMS = 12

    def __init__(self, place, win, wout, send, recv):
        self.p, self.win, self.wout, self.send, self.recv = place, win, wout, send, recv
        p = place
        self.plan = [(p.chip, p.first), (p.chip, p.second), (p.chip_first, p.second),
                     (p.chip_first, p.sibling), (p.chip_second, p.sibling), (p.chip_far, p.sibling)]

    def _copies(self, k):
        ch, target = self.plan[k]
        rows_in = pl.ds(pl.multiple_of(self.p.c * HALF_IN, HALF_IN), HALF_IN)
        cols_in = pl.ds(pl.multiple_of(ch * W_IN_COLS, 128), W_IN_COLS)
        rows_out = pl.ds(pl.multiple_of(ch * W_OUT_ROWS + self.p.c * HALF_OUT, HALF_OUT), HALF_OUT)
        r_in = self.win.at[rows_in, cols_in]
        r_out = self.wout.at[rows_out, :]
        return [pltpu.make_async_remote_copy(r_in, r_in, self.send.at[2 * k], self.recv.at[2 * k],
                                             device_id=target, device_id_type=MESH),
                pltpu.make_async_remote_copy(r_out, r_out, self.send.at[2 * k + 1], self.recv.at[2 * k + 1],
                                             device_id=target, device_id_type=MESH)]

    def _start(self, k):
        for cp in self._copies(k):
            cp.start()

    def _landed(self, k):
        for cp in self._copies(k):
            cp.wait_recv()

    def start_first_round(self):
        self._start(0)

    def start_second_round(self):
        self._landed(0)
        self._start(1)
        self._start(2)
        self._start(3)

    def pass_second_round(self):
        self._landed(1)
        self._start(4)
        self._landed(2)
        self._start(5)

    def finish(self):
        for k in (3, 4, 5):
            self._landed(k)
        for k in range(len(self.plan)):
            for cp in self._copies(k):
                cp.wait_send()


def _prepare(c_vec, w_ada, b_ada, w_in, w_out):
    def body(c_ref, wada_ref, bada_ref, win_ref, wout_ref,
             win0, win1, wout0, wout1, mod_ref, c_all,
             win_bf, wout_bf, mod_mine, mod_all, g_send, g_recv, c_send, c_recv, mod_send, mod_recv, local_sem):
        p = _Place()
        win_bf[...] = win_ref[...].astype(BF16)
        wout_bf[...] = wout_ref[...].astype(BF16)
        cols = pl.ds(pl.multiple_of(p.chip * W_IN_COLS, 128), W_IN_COLS)
        rows = pl.ds(pl.multiple_of(p.chip * W_OUT_ROWS, W_OUT_ROWS), W_OUT_ROWS)
        own = [pltpu.make_async_copy(win_bf.at[0], win0.at[:, cols], local_sem.at[0]),
               pltpu.make_async_copy(wout_bf.at[0], wout0.at[rows, :], local_sem.at[1]),
               pltpu.make_async_copy(win_bf.at[1], win1.at[:, cols], local_sem.at[2]),
               pltpu.make_async_copy(wout_bf.at[1], wout1.at[rows, :], local_sem.at[3])]
        for cp in own:
            cp.start()

        c_all[pl.ds(p.dev, 1), :] = c_ref[...]
        c_copies = [pltpu.make_async_remote_copy(c_ref, c_all.at[pl.ds(p.dev, 1), :], c_send.at[r], c_recv.at[r],
                                                 device_id=d, device_id_type=MESH) for r, d in enumerate(p.others())]
        for cp in c_copies:
            cp.start()
        own[0].wait()
        own[1].wait()
        gather = _WeightGather(p, win0, wout0, g_send, g_recv)
        gather.start_first_round()
        for cp in c_copies:
            cp.wait()

        cv = c_all[...]
        silu_c = (cv * _sigmoid(cv)).astype(BF16)
        for l in range(DEPTH):
            mod_mine[l] = _dot(silu_c, wada_ref[l].astype(BF16))
        mod_all[p.chip] = mod_mine[...]
        m_copies = [pltpu.make_async_remote_copy(mod_mine, mod_all.at[p.chip], mod_send.at[k], mod_recv.at[k],
                                                 device_id=(px, py, p.c), device_id_type=MESH)
                    for k, (px, py) in enumerate(p.other_chips())]
        for cp in m_copies:
            cp.start()
        gather.start_second_round()
        for cp in m_copies:
            cp.wait()
        mod_ref[...] = jnp.zeros_like(mod_ref)
        for l in range(DEPTH):
            full = jnp.concatenate([mod_all[ch, l, pl.ds(p.dev, 1), :] for ch in range(N_CHIP)], axis=1) + bada_ref[l:l + 1, :]
            for k in range(3):
                mod_ref[l, k:k + 1, :] = full[:, k * D_MODEL:(k + 1) * D_MODEL]
        gather.pass_second_round()
        gather.finish()
        own[2].wait()
        own[3].wait()

    vmem = pl.BlockSpec(memory_space=pltpu.VMEM)
    hbm = pl.BlockSpec(memory_space=pl.ANY)
    w_in_shape = jax.ShapeDtypeStruct((D_MODEL, D_PROJ), BF16)
    w_out_shape = jax.ShapeDtypeStruct((D_MODEL, D_MODEL), BF16)
    return pl.pallas_call(
        body, name="prepare",
        in_specs=[vmem, vmem, vmem, vmem, vmem],
        out_specs=[hbm, hbm, hbm, hbm, vmem, vmem],
        out_shape=[w_in_shape, w_in_shape, w_out_shape, w_out_shape,
                   jax.ShapeDtypeStruct((DEPTH, 8, D_MODEL), F32), jax.ShapeDtypeStruct((N_DEV, D_MODEL), F32)],
        scratch_shapes=[
            pltpu.VMEM((DEPTH, D_MODEL, W_IN_COLS), BF16), pltpu.VMEM((DEPTH, W_OUT_ROWS, D_MODEL), BF16),
            pltpu.VMEM((DEPTH, N_DEV, W_ADA_COLS), F32), pltpu.VMEM((N_CHIP, DEPTH, N_DEV, W_ADA_COLS), F32),
            pltpu.SemaphoreType.DMA((_WeightGather.N_SEMS,)), pltpu.SemaphoreType.DMA((_WeightGather.N_SEMS,)),
            pltpu.SemaphoreType.DMA((7,)), pltpu.SemaphoreType.DMA((7,)),
            pltpu.SemaphoreType.DMA((3,)), pltpu.SemaphoreType.DMA((3,)),
            pltpu.SemaphoreType.DMA((4,)),
        ],
        compiler_params=pltpu.CompilerParams(vmem_limit_bytes=VMEM_LIMIT),
    )(c_vec, w_ada, b_ada, w_in, w_out)


class _GradReduce:
    SCRATCH = [
        pltpu.VMEM((HALF_IN, D_PROJ), F32), pltpu.VMEM((HALF_IN, D_PROJ), F32),
        pltpu.VMEM((N_CHIP, HALF_OUT, D_MODEL), F32), pltpu.VMEM((N_CHIP, HALF_OUT, D_MODEL), F32),
        pltpu.VMEM((3, HALF_IN, W_IN_COLS), BF16), pltpu.VMEM((3, HALF_OUT, D_MODEL), BF16),
        pltpu.VMEM((2, HALF_IN, W_IN_COLS), BF16), pltpu.VMEM((2, HALF_OUT, D_MODEL), BF16),
        pltpu.VMEM((HALF_IN, W_IN_COLS), BF16), pltpu.VMEM((HALF_OUT, D_MODEL), BF16),
        pltpu.VMEM((HALF_IN, W_IN_COLS), F32), pltpu.VMEM((HALF_OUT, D_MODEL), F32),
        pltpu.SemaphoreType.DMA((10,)),
        pltpu.SemaphoreType.DMA((2 * N_CHIP,)), pltpu.SemaphoreType.DMA((2 * N_CHIP,)),
        pltpu.SemaphoreType.DMA((2 * N_CHIP,)), pltpu.SemaphoreType.DMA((2,)),
    ]

    def __init__(self, place, gw_in, gw_out, fin_in, fin_out, scratch):
        self.p, self.gw_in, self.gw_out, self.fin_in, self.fin_out = place, gw_in, gw_out, fin_in, fin_out
        (self.a_in, self.b_in, self.a_out, self.b_out, self.st_in, self.st_out, self.r1_in, self.r1_out,
         self.r2_in, self.r2_out, self.f_in, self.f_out, self.sem, self.s1, self.r1, self.s2, self.r2) = scratch
        c = place.c
        self.my_rows = pl.ds(pl.multiple_of(c * HALF_IN, HALF_IN), HALF_IN)
        self.sib_rows = pl.ds(pl.multiple_of((1 - c) * HALF_IN, HALF_IN), HALF_IN)

    def _to_sibling(self):
        c, sib = self.p.c, self.p.sibling
        return [pltpu.make_async_remote_copy(self.gw_in.at[self.sib_rows, :], self.b_in, self.sem.at[2], self.sem.at[3],
                                             device_id=sib, device_id_type=MESH),
                pltpu.make_async_remote_copy(self.gw_out.at[:, 1 - c], self.b_out, self.sem.at[4], self.sem.at[5],
                                             device_id=sib, device_id_type=MESH)]

    def _loads(self):
        return [pltpu.make_async_copy(self.gw_in.at[self.my_rows, :], self.a_in, self.sem.at[0]),
                pltpu.make_async_copy(self.gw_out.at[:, self.p.c], self.a_out, self.sem.at[1])]

    def start(self):
        for cp in self._loads() + self._to_sibling():
            cp.start()

    def _slot(self, ch):
        return jnp.where(self.p.c == 0, ch % 2, ch // 2)

    def _round1(self, ch):
        k = self._slot(ch)
        return [pltpu.make_async_remote_copy(self.st_in.at[k], self.r1_in.at[k], self.s1.at[2 * ch], self.r1.at[2 * ch],
                                             device_id=self.p.first, device_id_type=MESH),
                pltpu.make_async_remote_copy(self.st_out.at[k], self.r1_out.at[k], self.s1.at[2 * ch + 1],
                                             self.r1.at[2 * ch + 1], device_id=self.p.first, device_id_type=MESH)]

    def _round2(self, ch):
        return [pltpu.make_async_remote_copy(self.st_in.at[2], self.r2_in, self.s2.at[2 * ch], self.r2.at[0],
                                             device_id=self.p.second, device_id_type=MESH),
                pltpu.make_async_remote_copy(self.st_out.at[2], self.r2_out, self.s2.at[2 * ch + 1], self.r2.at[1],
                                             device_id=self.p.second, device_id_type=MESH)]

    def _stage(self, ch, k):
        self.st_in[k] = self.a_in[:, ch * W_IN_COLS:(ch + 1) * W_IN_COLS].astype(BF16)
        self.st_out[k] = self.a_out[ch].astype(BF16)

    def chip_sum_and_first_round(self):
        p = self.p
        for cp in self._loads():
            cp.wait()
        for cp in self._to_sibling():
            cp.wait()
        self.a_in[...] += self.b_in[...]
        self.a_out[...] += self.b_out[...]
        for ch in range(N_CHIP):
            @pl.when(p.first_coord(ch) != p.my_first_coord)
            def _(ch=ch):
                self._stage(ch, self._slot(ch))
                for cp in self._round1(ch):
                    cp.start()

    def second_round(self):
        p = self.p
        for ch in range(N_CHIP):
            @pl.when(p.first_coord(ch) == p.my_first_coord)
            def _(ch=ch):
                for cp in self._round1(ch):
                    cp.wait_recv()
                cols = slice(ch * W_IN_COLS, (ch + 1) * W_IN_COLS)
                self.a_in[:, cols] += self.r1_in[self._slot(ch)].astype(F32)
                self.a_out[ch] += self.r1_out[self._slot(ch)].astype(F32)

                @pl.when(ch != p.chip)
                def _():
                    self._stage(ch, 2)
                    for cp in self._round2(ch):
                        cp.start()

    def finish_start(self):
        p = self.p
        for cp in self._round2(0):
            cp.wait_recv()
        for ch in range(N_CHIP):
            @pl.when(ch == p.chip)
            def _(ch=ch):
                self.f_in[...] = self.a_in[:, ch * W_IN_COLS:(ch + 1) * W_IN_COLS] + self.r2_in[...].astype(F32)
                self.f_out[...] = self.a_out[ch] + self.r2_out[...].astype(F32)
        for cp in self._finals():
            cp.start()

    def _finals(self):
        c, sib = self.p.c, self.p.sibling
        return [pltpu.make_async_copy(self.f_in, self.fin_in.at[self.my_rows, :], self.sem.at[0]),
                pltpu.make_async_copy(self.f_out, self.fin_out.at[c], self.sem.at[1]),
                pltpu.make_async_remote_copy(self.f_in, self.fin_in.at[self.my_rows, :], self.sem.at[6], self.sem.at[7],
                                             device_id=sib, device_id_type=MESH),
                pltpu.make_async_remote_copy(self.f_out, self.fin_out.at[c], self.sem.at[8], self.sem.at[9],
                                             device_id=sib, device_id_type=MESH)]

    def finish(self):
        p = self.p
        for cp in self._finals():
            cp.wait()
        for ch in range(N_CHIP):
            @pl.when(p.first_coord(ch) != p.my_first_coord)
            def _(ch=ch):
                for cp in self._round1(ch):
                    cp.wait_send()

            @pl.when(jnp.logical_and(p.first_coord(ch) == p.my_first_coord, ch != p.chip))
            def _(ch=ch):
                for cp in self._round2(ch):
                    cp.wait_send()


FIN_IN = jax.ShapeDtypeStruct((D_MODEL, W_IN_COLS), F32)
FIN_OUT = jax.ShapeDtypeStruct((2, HALF_OUT, D_MODEL), F32)


def _reduce_final(gw_in, gw_out, pack, dmod):
    def body(gw_in_ref, gw_out_ref, pack_ref, dmod_ref, fin_in, fin_out, pack_out, dmod_all,
             p_recv, p_sum, dm_st, pk_send, pk_recv, ag_send, ag_recv, dm_send, dm_recv, *scratch):
        p = _Place()
        dev = p.dev
        devices = p.others()
        red = _GradReduce(p, gw_in_ref, gw_out_ref, fin_in, fin_out, scratch)
        red.start()

        def chunk(ref, d):
            return ref.at[d // 4, pl.ds(pl.multiple_of((d % 4) * PK_CHUNK, 8), PK_CHUNK), :]

        p_recv[dev] = pack_ref[dev // 4, pl.ds(pl.multiple_of((dev % 4) * PK_CHUNK, 8), PK_CHUNK), :]
        pk = [pltpu.make_async_remote_copy(chunk(pack_ref, 4 * dx + 2 * dy + dc), p_recv.at[dev], pk_send.at[r], pk_recv.at[r],
                                           device_id=(dx, dy, dc), device_id_type=MESH)
              for r, (dx, dy, dc) in enumerate(devices)]
        for cp in pk:
            cp.start()
        for l in range(DEPTH):
            for k in range(3):
                for r in range(D_MODEL // HEAD):
                    dm_st[l, 8 * k + r] = dmod_ref[l, k:k + 1, r * HEAD:(r + 1) * HEAD]
        dmod_all[:, :, pl.ds(dev, 1), :] = dm_st[...]
        dm = [pltpu.make_async_remote_copy(dm_st, dmod_all.at[:, :, pl.ds(dev, 1), :], dm_send.at[r], dm_recv.at[r],
                                           device_id=d, device_id_type=MESH) for r, d in enumerate(devices)]
        for cp in dm:
            cp.start()
        for cp in pk:
            cp.wait()
        acc = p_recv[0]
        for d in range(1, N_DEV):
            acc = acc + p_recv[d]
        p_sum[...] = acc
        pack_out[dev // 4, pl.ds(pl.multiple_of((dev % 4) * PK_CHUNK, 8), PK_CHUNK), :] = acc
        ag = [pltpu.make_async_remote_copy(p_sum, chunk(pack_out, dev), ag_send.at[r], ag_recv.at[r],
                                           device_id=d, device_id_type=MESH) for r, d in enumerate(devices)]
        for cp in ag:
            cp.start()

        red.chip_sum_and_first_round()
        red.second_round()
        red.finish_start()
        for cp in dm + ag:
            cp.wait()
        red.finish()

    vmem = pl.BlockSpec(memory_space=pltpu.VMEM)
    hbm = pl.BlockSpec(memory_space=pl.ANY)
    return pl.pallas_call(
        body, name="reduce_first",
        in_specs=[hbm, hbm, vmem, vmem],
        out_specs=[hbm, hbm, vmem, vmem],
        out_shape=[FIN_IN, FIN_OUT, jax.ShapeDtypeStruct((DEPTH, PK_ROWS, HEAD), F32),
                   jax.ShapeDtypeStruct((DEPTH, 24, N_DEV, HEAD), F32)],
        scratch_shapes=[
            pltpu.VMEM((N_DEV, PK_CHUNK, HEAD), F32), pltpu.VMEM((PK_CHUNK, HEAD), F32),
            pltpu.VMEM((DEPTH, 24, 1, HEAD), F32),
            pltpu.SemaphoreType.DMA((7,)), pltpu.SemaphoreType.DMA((7,)),
            pltpu.SemaphoreType.DMA((7,)), pltpu.SemaphoreType.DMA((7,)),
            pltpu.SemaphoreType.DMA((7,)), pltpu.SemaphoreType.DMA((7,)),
        ] + _GradReduce.SCRATCH,
        compiler_params=pltpu.CompilerParams(vmem_limit_bytes=VMEM_LIMIT),
    )(gw_in, gw_out.reshape(N_CHIP, 2, HALF_OUT, D_MODEL), pack, dmod)


def _adamw(w, g, m, v):
    m = ADAM_B1 * m + (1.0 - ADAM_B1) * g
    v = ADAM_B2 * v + (1.0 - ADAM_B2) * (g * g)
    m_hat = m / (1.0 - ADAM_B1 ** ADAM_STEP)
    v_hat = v / (1.0 - ADAM_B2 ** ADAM_STEP)
    delta = -ADAM_LR * (m_hat / (jnp.sqrt(v_hat) + ADAM_EPS) + ADAM_WD * w)
    return delta, m, v


def _adam_sharded(name, w, g_first, g_last, m, v, rows):
    _, r, cols = w.shape
    n = r // rows

    def body(w_ref, g0_ref, g1_ref, m_ref, v_ref, g_out, d_out, m_out, v_out):
        @pl.when(pl.program_id(0) == 0)
        def _():
            g_out[...] = g0_ref[...]

        @pl.when(pl.program_id(0) == 1)
        def _():
            g_out[...] = g1_ref[...]

        d_out[...], m_out[...], v_out[...] = _adamw(w_ref[...], g_out[...], m_ref[...], v_ref[...])

    blk = pl.BlockSpec((None, rows, cols), lambda l, i: (l, i, 0))
    g0 = pl.BlockSpec((rows, cols), lambda l, i: (jnp.where(l == 0, i, n - 1), 0))
    g1 = pl.BlockSpec((rows, cols), lambda l, i: (jnp.where(l == 1, i, 0), 0))
    shape = jax.ShapeDtypeStruct(w.shape, F32)
    return pl.pallas_call(
        body, name=name, grid=(DEPTH, n), in_specs=[blk, g0, g1, blk, blk], out_specs=[blk] * 4, out_shape=[shape] * 4,
        compiler_params=pltpu.CompilerParams(dimension_semantics=("arbitrary", "arbitrary"), vmem_limit_bytes=VMEM_LIMIT),
    )(w, g_first, g_last, m, v)


def _adam_w_ada(w, m, v, c_all, dmod_all):
    rows = 256

    def body(c_ref, dm_ref, w_ref, m_ref, v_ref, g_out, d_out, m_out, v_out):
        l = pl.program_id(0)
        chip = 2 * lax.axis_index("x") + lax.axis_index("y")
        cv = c_ref[...]
        silu_c = (cv * _sigmoid(cv)).astype(BF16).astype(F32)
        pieces = []
        for k in range(W_ADA_COLS // HEAD):
            dk = dm_ref[l, 6 * chip + k].astype(BF16).astype(F32)
            pieces.append(_dot_exact(silu_c, dk, TN))
        g = jnp.concatenate(pieces, axis=1)
        g_out[...] = g
        d_out[...], m_out[...], v_out[...] = _adamw(w_ref[...], g, m_ref[...], v_ref[...])

    blk = pl.BlockSpec((None, rows, W_ADA_COLS), lambda l, i: (l, i, 0))
    shape = jax.ShapeDtypeStruct(w.shape, F32)
    return pl.pallas_call(
        body, name="adam_w_ada", grid=(DEPTH, D_MODEL // rows),
        in_specs=[pl.BlockSpec((N_DEV, rows), lambda l, i: (0, i)),
                  pl.BlockSpec((DEPTH, 24, N_DEV, HEAD), lambda l, i: (0, 0, 0, 0)), blk, blk, blk],
        out_specs=[blk] * 4, out_shape=[shape] * 4,
        compiler_params=pltpu.CompilerParams(dimension_semantics=("arbitrary", "arbitrary"), vmem_limit_bytes=VMEM_LIMIT),
    )(c_all, dmod_all, w, m, v)


def _adam_small(pack, dmod_all, weights, ms, vs):
    n = len(weights)

    def body(*refs):
        pack_ref, dm_ref = refs[0], refs[1]
        w_refs, m_refs, v_refs = refs[2:2 + n], refs[2 + n:2 + 2 * n], refs[2 + 2 * n:2 + 3 * n]
        outs = refs[2 + 3 * n:]
        g_refs, d_refs, nm_refs, nv_refs = outs[0:n], outs[n:2 * n], outs[2 * n:3 * n], outs[3 * n:4 * n]
        outs[4 * n][...] = pack_ref[DEPTH - 1, PK_LOSS:PK_LOSS + 1, 0:1] * (0.5 / D_MODEL)

        def lanes(l, row0, count):
            return jnp.concatenate([pack_ref[l, row0 + k:row0 + k + 1, :] for k in range(count)], axis=1)

        def update(idx, at, g):
            g_refs[idx][at] = g
            d_refs[idx][at], nm_refs[idx][at], nv_refs[idx][at] = _adamw(w_refs[idx][at], g, m_refs[idx][at], v_refs[idx][at])

        for l in range(DEPTH):
            row = (slice(l, l + 1), slice(None))
            g_b = None
            for d in range(N_DEV):
                part = dm_ref[l, :, d, :]
                g_b = part if g_b is None else g_b + part
            update(0, row, jnp.concatenate([g_b[k:k + 1, :] for k in range(24)], axis=1))
            for g in range(N_HEAD):
                update(1, (l, g), pack_ref[l, PK_W_POOL + g * HEAD:PK_W_POOL + (g + 1) * HEAD, :])
                update(5, (l, g), pack_ref[l, PK_W_SGU + g * HEAD:PK_W_SGU + (g + 1) * HEAD, :])
            update(2, row, lanes(l, PK_POOL_SCALE, N_HEAD))
            update(3, (l,), pack_ref[l, PK_SGU_LN_G:PK_SGU_LN_G + N_HEAD, :])
            update(4, (l,), pack_ref[l, PK_SGU_LN_B:PK_SGU_LN_B + N_HEAD, :])
            update(6, (l,), pack_ref[l, PK_B_SGU:PK_B_SGU + N_HEAD, :])
            update(7, row, lanes(l, PK_LN_G, D_MODEL // HEAD))
            update(8, row, lanes(l, PK_LN_B, D_MODEL // HEAD))

    vmem = pl.BlockSpec(memory_space=pltpu.VMEM)
    shapes = [jax.ShapeDtypeStruct(w.shape, F32) for w in weights]
    return pl.pallas_call(
        body, name="adam_small", in_specs=[vmem] * (2 + 3 * n), out_specs=[vmem] * (4 * n + 1),
        out_shape=shapes * 4 + [jax.ShapeDtypeStruct((1, 1), F32)],
        compiler_params=pltpu.CompilerParams(vmem_limit_bytes=VMEM_LIMIT),
    )(pack, dmod_all, *weights, *ms, *vs)


def kernel(x, c, w_ada, b_ada, w_in, w_pool, pool_scale, sgu_ln_g, sgu_ln_b, w_sgu, b_sgu, w_out, ln_g, ln_b, loss_target, m_w_ada, m_b_ada, m_w_in, m_w_pool, m_pool_scale, m_sgu_ln_g, m_sgu_ln_b, m_w_sgu, m_b_sgu, m_w_out, m_ln_g, m_ln_b, v_w_ada, v_b_ada, v_w_in, v_w_pool, v_pool_scale, v_sgu_ln_g, v_sgu_ln_b, v_w_sgu, v_b_sgu, v_w_out, v_ln_g, v_ln_b):
    w_in0, w_in1, w_out0, w_out1, mod, c_all = _prepare(c, w_ada, b_ada, w_in, w_out)
    small = (w_pool, pool_scale, sgu_ln_g, sgu_ln_b, w_sgu, jnp.swapaxes(b_sgu, 1, 2))

    proj0, y0, x1, w_in1, w_out1 = _forward_layer(0, x, mod, w_in0, w_out0, small, ln_g, ln_b,
                                                  next_weights=(w_in1, w_out1))
    proj1, y1, dout, sq = _forward_layer(1, x1, mod, w_in1, w_out1, small, ln_g, ln_b, target=loss_target)

    dx1, h1, cat1, dy1, dproj1, pack, dmod = _backward_layer(1, dout, x1, y1, proj1, mod, w_in1, w_out1, small, ln_g, sq=sq)
    later = (_weight_grad("wgrad_in_last", h1, dproj1, W_IN_COLS), _weight_grad("wgrad_out_last", cat1, dy1, D_MODEL // 2))
    dx0, h0, cat0, dy0, dproj0, pack, dmod, g_in1, g_out1 = _backward_layer(
        0, dx1, x, y0, proj0, mod, w_in0, w_out0, small, ln_g, pack=pack, dmod=dmod, later_grads=later)
    g_in0, g_out0, pack, dmod_all = _reduce_final(_weight_grad("wgrad_in_first", h0, dproj0, W_IN_COLS),
                                                  _weight_grad("wgrad_out_first", cat0, dy0, D_MODEL // 2), pack, dmod)

    ada = _adam_w_ada(w_ada, m_w_ada, v_w_ada, c_all, dmod_all)
    win = _adam_sharded("adam_w_in", w_in, g_in0, g_in1, m_w_in, v_w_in, 256)
    wout = _adam_sharded("adam_w_out", w_out, g_out0.reshape(W_OUT_ROWS, D_MODEL), g_out1.reshape(W_OUT_ROWS, D_MODEL),
                         m_w_out, v_w_out, 256)
    small_w = (b_ada, w_pool, pool_scale, sgu_ln_g, sgu_ln_b, w_sgu, b_sgu, ln_g, ln_b)
    small_m = (m_b_ada, m_w_pool, m_pool_scale, m_sgu_ln_g, m_sgu_ln_b, m_w_sgu, m_b_sgu, m_ln_g, m_ln_b)
    small_v = (v_b_ada, v_w_pool, v_pool_scale, v_sgu_ln_g, v_sgu_ln_b, v_w_sgu, v_b_sgu, v_ln_g, v_ln_b)
    res = _adam_small(pack, dmod_all, small_w, small_m, small_v)
    n = len(small_w)
    loss = res[4 * n].reshape(())

    def ordered(k):
        s = res[k * n:(k + 1) * n]
        return (ada[k], s[0], win[k], s[1], s[2], s[3], s[4], s[5], s[6], wout[k], s[7], s[8])

    return (loss, dx0[None], *ordered(0), *ordered(1), *ordered(2), *ordered(3))
```

```python
import jax
import jax.numpy as jnp
from jax import lax
from jax.experimental import pallas as pl
from jax.experimental.pallas import tpu as pltpu

F32 = jnp.float32
BF16 = jnp.bfloat16
MESH = pl.DeviceIdType.MESH

N_DEV = 8
N_CHIP = 4
DEPTH = 2
SEQ = 2048
D_MODEL = 1024
D_POOL = 512
D_PROJ = 2560
HEAD = 128
N_HEAD = 4
ROWS = 256
N_TILE = SEQ // ROWS
HALO = 16
W_IN_COLS = D_PROJ // N_CHIP
W_OUT_ROWS = D_MODEL // N_CHIP
W_ADA_COLS = 3 * D_MODEL // N_CHIP
HALF_IN = D_MODEL // 2
HALF_OUT = W_OUT_ROWS // 2
DEEPNORM_ALPHA = (2.0 * DEPTH) ** 0.25
LN_EPS = 1e-5
INV_SQRT2 = 0.7071067811865476
INV_SQRT_2PI = 0.3989422804014327

ADAM_LR = 0.001
ADAM_B1 = 0.9
ADAM_B2 = 0.999
ADAM_EPS = 1e-08
ADAM_WD = 0.01
ADAM_STEP = 10

PK_W_POOL = 0
PK_W_SGU = 512
PK_POOL_SCALE = 1024
PK_SGU_LN_G = 1032
PK_SGU_LN_B = 1040
PK_B_SGU = 1048
PK_LN_G = 1056
PK_LN_B = 1064
PK_LOSS = 1072
PK_ROWS = 1088
PK_CHUNK = DEPTH * PK_ROWS // N_DEV

VMEM_LIMIT = 56 * 1024 * 1024

GATHER_SECOND_ROUND_STEP = 3

NN = (((1,), (0,)), ((), ()))
NT = (((1,), (1,)), ((), ()))
TN = (((0,), (0,)), ((), ()))


def _dot(a, b, dims=NN):
    return lax.dot_general(a, b, dims, preferred_element_type=F32)


def _dot_exact(a, b, dims=NN):
    return lax.dot_general(a, b, dims, preferred_element_type=F32, precision=lax.Precision.HIGHEST)


def _layer_norm(v):
    mu = jnp.mean(v, axis=-1, keepdims=True)
    d = v - mu
    var = jnp.mean(d * d, axis=-1, keepdims=True)
    rstd = lax.rsqrt(var + LN_EPS)
    return d * rstd, rstd


def _layer_norm_bwd(dvhat, vhat, rstd):
    m1 = jnp.mean(dvhat, axis=-1, keepdims=True)
    m2 = jnp.mean(dvhat * vhat, axis=-1, keepdims=True)
    return rstd * (dvhat - m1 - vhat * m2)


def _sigmoid(v):
    return 1.0 / (1.0 + jnp.exp(-v))


def _gelu_parts(v):
    phi = 0.5 * (1.0 + lax.erf(v * INV_SQRT2))
    pdf = INV_SQRT_2PI * jnp.exp(-0.5 * v * v)
    return phi, pdf


def _sum_rows(v):
    return jnp.sum(v, axis=0, keepdims=True)


def _window_sums(ext, toward_later):
    n = ext.shape[0]

    def shifted(v, k):
        return pltpu.roll(v, (n - k) if toward_later else k, 0)

    s2 = ext + shifted(ext, 1)
    r4 = s2[:, HEAD:]
    s4 = r4 + shifted(r4, 2)
    r8 = s4[:, HEAD:]
    s8 = r8 + shifted(r8, 4)
    r16 = s8[:, HEAD:]
    s16 = r16 + shifted(r16, 8)
    return jnp.concatenate([s2[:, :HEAD], s4[:, :HEAD], s8[:, :HEAD], s16], axis=1)


def _window_counts(row0):
    t1 = row0 + 1 + lax.broadcasted_iota(jnp.int32, (ROWS, D_POOL), 0)
    lane = lax.broadcasted_iota(jnp.int32, (ROWS, D_POOL), 1)
    width = jnp.where(lane < HEAD, 2, jnp.where(lane < 2 * HEAD, 4, jnp.where(lane < 3 * HEAD, 8, 16)))
    return jnp.minimum(t1, width).astype(F32)


def _causal_mask():
    r = lax.broadcasted_iota(jnp.int32, (HEAD, HEAD), 0)
    s = lax.broadcasted_iota(jnp.int32, (HEAD, HEAD), 1)
    return r >= s


def _chunks_to_lanes(v):
    return jnp.concatenate([v[n * HEAD:(n + 1) * HEAD] for n in range(ROWS // HEAD)], axis=1)


def _lanes_to_chunks(v):
    return jnp.concatenate([v[:, n * HEAD:(n + 1) * HEAD] for n in range(ROWS // HEAD)], axis=0)


def _mixer(proj, halo, row0, wpool_ref, pscale, sgu_g_ref, sgu_b_ref, wsgu_ref, bsgu_t_ref):
    xa = proj[:, 0:512]
    ga = proj[:, 512:1024]
    u = proj[:, 1024:1536]
    v = proj[:, 1536:2048]
    gb = proj[:, 2048:2560]
    ext = jnp.concatenate([halo, xa], axis=0)
    win = _window_sums(ext, toward_later=False)[HALO:]
    cnt = _window_counts(row0)
    pooled = (win / cnt - xa).astype(BF16)
    pw = jnp.concatenate(
        [_dot(pooled[:, g * HEAD:(g + 1) * HEAD], wpool_ref[g].astype(BF16)) for g in range(N_HEAD)], axis=1)
    sig_a = _sigmoid(ga)
    ya = pw * pscale * (ga * sig_a)
    phi_u, pdf_u = _gelu_parts(u)
    phi_v, pdf_v = _gelu_parts(v)
    gu = u * phi_u
    gv = v * phi_v
    sig_b = _sigmoid(gb)
    silu_b = gb * sig_b
    mask = _causal_mask()
    vhat, rstd_v, vln_l, mixed = [], [], [], []
    for h in range(N_HEAD):
        vh, rh = _layer_norm(gv[:, h * HEAD:(h + 1) * HEAD])
        ln = (vh * sgu_g_ref[h:h + 1, :] + sgu_b_ref[h:h + 1, :]).astype(BF16)
        ln_l = _chunks_to_lanes(ln)
        wm = jnp.where(mask, wsgu_ref[h], 0.0).astype(BF16)
        mx = _lanes_to_chunks(_dot(wm, ln_l) + bsgu_t_ref[:, h:h + 1])
        vhat.append(vh)
        rstd_v.append(rh)
        vln_l.append(ln_l)
        mixed.append(mx)
    mixed = jnp.concatenate(mixed, axis=1)
    yb = gu * mixed * silu_b
    return dict(xa=xa, ga=ga, u=u, gb=gb, cnt=cnt, pooled=pooled, pw=pw, sig_a=sig_a, ya=ya, phi_u=phi_u, pdf_u=pdf_u,
                phi_v=phi_v, pdf_v=pdf_v, gu=gu, sig_b=sig_b, silu_b=silu_b, vhat=vhat, rstd_v=rstd_v, vln_l=vln_l,
                mixed=mixed, yb=yb, mask=mask)


def _const(shape, *index):
    lead = tuple(index) + (0,) * (len(shape) - len(index))
    return pl.BlockSpec(shape, lambda *_: lead)


def _const_in(shape, *index):
    lead = tuple(index) + (0,) * (len(shape) - len(index))
    return pl.BlockSpec(shape, lambda *_: lead, pipeline_mode=pl.Buffered(1))


def _layer_weight_specs(l):
    return [
        _const_in((None, N_HEAD, HEAD, HEAD), l),
        _const_in((DEPTH, D_POOL)),
        _const_in((None, N_HEAD, HEAD), l),
        _const_in((None, N_HEAD, HEAD), l),
        _const_in((None, N_HEAD, HEAD, HEAD), l),
        _const_in((None, HEAD, N_HEAD), l),
    ]


def _forward_layer(l, x, mod, w_in, w_out, small, ln_g, ln_b, target=None, next_weights=None):
    last = target is not None
    gathers = next_weights is not None

    def body(*refs):
        if last:
            (x_ref, mod_ref, win_ref, wout_ref, wpool_ref, pscale_ref, sgu_g_ref, sgu_b_ref, wsgu_ref, bsgu_t_ref,
             lng_ref, lnb_ref, tgt_ref, proj_ref, y_ref, out_ref, loss_ref, carry_ref) = refs
        elif gathers:
            (x_ref, mod_ref, win_ref, wout_ref, wpool_ref, pscale_ref, sgu_g_ref, sgu_b_ref, wsgu_ref, bsgu_t_ref,
             lng_ref, lnb_ref, _, _, proj_ref, y_ref, out_ref, next_in, next_out, carry_ref, g_send, g_recv) = refs
            gather = _WeightGather(_Place(), next_in, next_out, g_send, g_recv)
        else:
            (x_ref, mod_ref, win_ref, wout_ref, wpool_ref, pscale_ref, sgu_g_ref, sgu_b_ref, wsgu_ref, bsgu_t_ref,
             lng_ref, lnb_ref, proj_ref, y_ref, out_ref, carry_ref) = refs
        i = pl.program_id(0)

        @pl.when(i == 0)
        def _():
            carry_ref[...] = jnp.zeros_like(carry_ref)
            if last:
                loss_ref[...] = jnp.zeros_like(loss_ref)
            if gathers:
                gather.start_first_round()

        if gathers:
            @pl.when(i == GATHER_SECOND_ROUND_STEP)
            def _():
                gather.start_second_round()

        x = x_ref[...]
        shift, scale, gate = mod_ref[0:1, :], mod_ref[1:2, :], mod_ref[2:3, :]
        xn, _ = _layer_norm(x)
        h = xn * (1.0 + scale) + shift
        proj = _dot(h.astype(BF16), win_ref[...])
        proj_ref[...] = proj
        m = _mixer(proj, carry_ref[...], i * ROWS, wpool_ref, pscale_ref[l:l + 1, :], sgu_g_ref, sgu_b_ref, wsgu_ref,
                   bsgu_t_ref)
        carry_ref[...] = m["xa"][ROWS - HALO:]
        cat = jnp.concatenate([m["ya"], m["yb"]], axis=1).astype(BF16)
        y = _dot(cat, wout_ref[...])
        y_ref[...] = y
        zn, _ = _layer_norm(DEEPNORM_ALPHA * x + gate * y)
        out = zn * lng_ref[l:l + 1, :] + lnb_ref[l:l + 1, :]
        if last:
            err = out - tgt_ref[...]
            out_ref[...] = err * (1.0 / D_MODEL)
            loss_ref[...] += jnp.sum(err * err)
        else:
            out_ref[...] = out

        if gathers:
            @pl.when(i == N_TILE - 1)
            def _():
                gather.pass_second_round()
                gather.finish()

    tile = pl.BlockSpec((ROWS, D_MODEL), lambda i: (i, 0))
    tile3 = pl.BlockSpec((None, ROWS, D_MODEL), lambda i: (0, i, 0))
    in_specs = [tile3 if x.ndim == 3 else tile, _const_in((None, 8, D_MODEL), l), _const_in((D_MODEL, D_PROJ)),
                _const_in((D_MODEL, D_MODEL))]
    in_specs += _layer_weight_specs(l) + [_const_in((DEPTH, D_MODEL)), _const_in((DEPTH, D_MODEL))]
    out_shape = [jax.ShapeDtypeStruct((SEQ, D_PROJ), F32), jax.ShapeDtypeStruct((SEQ, D_MODEL), F32),
                 jax.ShapeDtypeStruct((SEQ, D_MODEL), F32)]
    out_specs = [pl.BlockSpec((ROWS, D_PROJ), lambda i: (i, 0)), tile, tile]
    args = [x, mod, w_in, w_out, *small, ln_g, ln_b]
    scratch = [pltpu.VMEM((HALO, D_POOL), F32)]
    aliases = {}
    if last:
        in_specs.append(tile3)
        args.append(target)
        out_shape.append(jax.ShapeDtypeStruct((8, HEAD), F32))
        out_specs.append(_const((8, HEAD)))
    if gathers:
        hbm = pl.BlockSpec(memory_space=pl.ANY)
        aliases = {len(args): len(out_shape), len(args) + 1: len(out_shape) + 1}
        in_specs += [hbm, hbm]
        args += list(next_weights)
        out_shape += [jax.ShapeDtypeStruct(w.shape, BF16) for w in next_weights]
        out_specs += [hbm, hbm]
        scratch += [pltpu.SemaphoreType.DMA((_WeightGather.N_SEMS,)), pltpu.SemaphoreType.DMA((_WeightGather.N_SEMS,))]
    return pl.pallas_call(
        body, name="fwd_last" if last else "fwd_first", grid=(N_TILE,), in_specs=in_specs, out_specs=out_specs,
        out_shape=out_shape, scratch_shapes=scratch, input_output_aliases=aliases,
        compiler_params=pltpu.CompilerParams(dimension_semantics=("arbitrary",), vmem_limit_bytes=VMEM_LIMIT),
    )(*args)


def _backward_layer(l, dout, x, y, proj, mod, w_in, w_out, small, ln_g, sq=None, pack=None, dmod=None):
    creates = sq is not None

    def body(*refs):
        (dout_ref, x_ref, y_ref, proj_ref, halo_ref, mod_ref, win_ref, wout_ref, wpool_ref, pscale_ref,
         sgu_g_ref, sgu_b_ref, wsgu_ref, bsgu_t_ref, lng_ref) = refs[:15]
        dx_ref, h_ref, cat_ref, dy_ref, dproj_ref, pack_ref, dmod_ref, carry_ref = refs[-8:]
        i = pl.program_id(0)
        tile = N_TILE - 1 - i

        @pl.when(i == 0)
        def _():
            carry_ref[...] = jnp.zeros_like(carry_ref)
            pack_ref[...] = jnp.zeros_like(pack_ref)
            dmod_ref[...] = jnp.zeros_like(dmod_ref)
            if creates:
                pack_ref[PK_LOSS:PK_LOSS + 8, :] = refs[15][...]

        x = x_ref[...]
        y = y_ref[...]
        dout = dout_ref[...]
        pscale = pscale_ref[l:l + 1, :]
        shift, scale, gate = mod_ref[0:1, :], mod_ref[1:2, :], mod_ref[2:3, :]
        xn, rstd_x = _layer_norm(x)
        h = xn * (1.0 + scale) + shift
        h_ref[...] = h.astype(BF16)
        zn, rstd_z = _layer_norm(DEEPNORM_ALPHA * x + gate * y)
        g_ln_g = _sum_rows(dout * zn)
        g_ln_b = _sum_rows(dout)
        dz = _layer_norm_bwd(dout * lng_ref[l:l + 1, :], zn, rstd_z)
        d_gate = _sum_rows(dz * y)
        dy = (gate * dz).astype(BF16)
        dy_ref[...] = dy

        halo = jnp.where(tile > 0, halo_ref[...], 0.0)
        m = _mixer(proj_ref[...], halo, tile * ROWS, wpool_ref, pscale, sgu_g_ref, sgu_b_ref, wsgu_ref, bsgu_t_ref)
        cat_ref[...] = jnp.concatenate([m["ya"], m["yb"]], axis=1).astype(BF16)
        dcat = _dot(dy, wout_ref[...], NT)
        dya = dcat[:, :D_POOL]
        dyb = dcat[:, D_POOL:]

        ga, sig_a = m["ga"], m["sig_a"]
        dp = dya * (ga * sig_a)
        d_ga = dya * (m["pw"] * pscale) * (sig_a * (1.0 + ga * (1.0 - sig_a)))
        g_pscale = _sum_rows(dp * m["pw"])
        dpw = (dp * pscale).astype(BF16)
        dpooled = []
        for g in range(N_HEAD):
            cols = slice(g * HEAD, (g + 1) * HEAD)
            pack_ref[PK_W_POOL + g * HEAD:PK_W_POOL + (g + 1) * HEAD, :] += _dot(m["pooled"][:, cols], dpw[:, cols], TN)
            dpooled.append(_dot(dpw[:, cols], wpool_ref[g].astype(BF16), NT))
        dpooled = jnp.concatenate(dpooled, axis=1)
        q = dpooled / m["cnt"]
        ext = jnp.concatenate([q, carry_ref[...]], axis=0)
        d_xa = _window_sums(ext, toward_later=True)[:ROWS] - dpooled
        carry_ref[...] = q[:HALO]

        gu, mixed, silu_b, gb, sig_b = m["gu"], m["mixed"], m["silu_b"], m["gb"], m["sig_b"]
        d_mixed = dyb * gu * silu_b
        d_gu = dyb * mixed * silu_b
        d_gb = dyb * gu * mixed * (sig_b * (1.0 + gb * (1.0 - sig_b)))
        d_u = d_gu * (m["phi_u"] + m["u"] * m["pdf_u"])
        ones = jnp.ones((8, HEAD), F32)
        d_v = []
        for hd in range(N_HEAD):
            cols = slice(hd * HEAD, (hd + 1) * HEAD)
            dm = d_mixed[:, cols]
            dm_l = _chunks_to_lanes(dm.astype(BF16))
            g_w = _dot(dm_l, m["vln_l"][hd], NT)
            pack_ref[PK_W_SGU + hd * HEAD:PK_W_SGU + (hd + 1) * HEAD, :] += jnp.where(m["mask"], g_w, 0.0)
            dm_sum = dm[0:HEAD]
            for n in range(1, ROWS // HEAD):
                dm_sum = dm_sum + dm[n * HEAD:(n + 1) * HEAD]
            pack_ref[PK_B_SGU + hd:PK_B_SGU + hd + 1, :] += _dot_exact(ones, dm_sum, NT)[0:1]
            wm = jnp.where(m["mask"], wsgu_ref[hd], 0.0).astype(BF16)
            d_vln = _lanes_to_chunks(_dot(wm, dm_l, TN))
            vhat = m["vhat"][hd]
            pack_ref[PK_SGU_LN_G + hd:PK_SGU_LN_G + hd + 1, :] += _sum_rows(d_vln * vhat)
            pack_ref[PK_SGU_LN_B + hd:PK_SGU_LN_B + hd + 1, :] += _sum_rows(d_vln)
            d_v.append(_layer_norm_bwd(d_vln * sgu_g_ref[hd:hd + 1, :], vhat, m["rstd_v"][hd]))
        v = proj_ref[:, 1536:2048]
        d_v = jnp.concatenate(d_v, axis=1) * (m["phi_v"] + v * m["pdf_v"])

        dproj = jnp.concatenate([d_xa, d_ga, d_u, d_v, d_gb], axis=1).astype(BF16)
        dproj_ref[...] = dproj
        dh = _dot(dproj, win_ref[...], NT)
        d_scale = _sum_rows(dh * xn)
        d_shift = _sum_rows(dh)
        dx_ref[...] = DEEPNORM_ALPHA * dz + _layer_norm_bwd(dh * (1.0 + scale), xn, rstd_x)

        dmod_ref[0:1, :] += d_shift
        dmod_ref[1:2, :] += d_scale
        dmod_ref[2:3, :] += d_gate
        for g in range(N_HEAD):
            pack_ref[PK_POOL_SCALE + g:PK_POOL_SCALE + g + 1, :] += g_pscale[:, g * HEAD:(g + 1) * HEAD]
        for k in range(D_MODEL // HEAD):
            pack_ref[PK_LN_G + k:PK_LN_G + k + 1, :] += g_ln_g[:, k * HEAD:(k + 1) * HEAD]
            pack_ref[PK_LN_B + k:PK_LN_B + k + 1, :] += g_ln_b[:, k * HEAD:(k + 1) * HEAD]

    def rev(i):
        return (N_TILE - 1 - i, 0)

    tile = pl.BlockSpec((ROWS, D_MODEL), rev)
    tile3 = pl.BlockSpec((None, ROWS, D_MODEL), lambda i: (0, N_TILE - 1 - i, 0))
    halo = pl.BlockSpec((HALO, D_POOL), lambda i: (jnp.maximum((N_TILE - 1 - i) * (ROWS // HALO) - 1, 0), 0))
    in_specs = [tile, tile3 if x.ndim == 3 else tile, tile, pl.BlockSpec((ROWS, D_PROJ), rev), halo,
                _const_in((None, 8, D_MODEL), l), _const_in((D_MODEL, D_PROJ)), _const_in((D_MODEL, D_MODEL))]
    in_specs += _layer_weight_specs(l) + [_const_in((DEPTH, D_MODEL))]
    args = [dout, x, y, proj, proj, mod, w_in, w_out, *small, ln_g]
    out_shape = [jax.ShapeDtypeStruct((SEQ, D_MODEL), F32), jax.ShapeDtypeStruct((SEQ, D_MODEL), BF16),
                 jax.ShapeDtypeStruct((SEQ, D_MODEL), BF16), jax.ShapeDtypeStruct((SEQ, D_MODEL), BF16),
                 jax.ShapeDtypeStruct((SEQ, D_PROJ), BF16), jax.ShapeDtypeStruct((DEPTH, PK_ROWS, HEAD), F32),
                 jax.ShapeDtypeStruct((DEPTH, 8, D_MODEL), F32)]
    out_specs = [tile, tile, tile, tile, pl.BlockSpec((ROWS, D_PROJ), rev), _const((None, PK_ROWS, HEAD), l),
                 _const((None, 8, D_MODEL), l)]
    scratch = [pltpu.VMEM((HALO, D_POOL), F32)]
    aliases = {}
    hbm = pl.BlockSpec(memory_space=pl.ANY)
    if creates:
        in_specs.append(_const_in((8, HEAD)))
        args.append(sq)
    else:
        aliases = {len(args): 5, len(args) + 1: 6}
        in_specs += [hbm, hbm]
        args += [pack, dmod]
    return pl.pallas_call(
        body, name="bwd_last" if creates else "bwd_first", grid=(N_TILE,), in_specs=in_specs, out_specs=out_specs,
        out_shape=out_shape, scratch_shapes=scratch, input_output_aliases=aliases,
        compiler_params=pltpu.CompilerParams(dimension_semantics=("arbitrary",), vmem_limit_bytes=VMEM_LIMIT),
    )(*args)


def _flip(v, f):
    return v + f - 2 * v * f


class _Place:
    def __init__(self):
        x, y, c = lax.axis_index("x"), lax.axis_index("y"), lax.axis_index("c")
        self.x, self.y, self.c = x, y, c
        self.chip = 2 * x + y
        self.dev = 4 * x + 2 * y + c
        self.sibling = (x, y, 1 - c)
        x1, y1 = _flip(x, 1 - c), _flip(y, c)
        x2, y2 = _flip(x, c), _flip(y, 1 - c)
        self.first = (x1, y1, c)
        self.second = (x2, y2, c)
        self.chip_first = 2 * x1 + y1
        self.chip_second = 2 * x2 + y2
        self.chip_far = 2 * (1 - x) + (1 - y)
        self.my_first_coord = jnp.where(c == 0, x, y)

    def first_coord(self, ch):
        return jnp.where(self.c == 0, ch // 2, ch % 2)

    def others(self):
        return [(_flip(self.x, (r >> 2) & 1), _flip(self.y, (r >> 1) & 1), _flip(self.c, r & 1)) for r in range(1, N_DEV)]

    def other_chips(self):
        return [(1 - self.x, self.y), (self.x, 1 - self.y), (1 - self.x, 1 - self.y)]


class _WeightGather:
    N_SEMS = 12

    def __init__(self, place, win, wout, send, recv):
        self.p, self.win, self.wout, self.send, self.recv = place, win, wout, send, recv
        p = place
        self.plan = [(p.chip, p.first), (p.chip, p.second), (p.chip_first, p.second),
                     (p.chip_first, p.sibling), (p.chip_second, p.sibling), (p.chip_far, p.sibling)]

    def _copies(self, k):
        ch, target = self.plan[k]
        rows_in = pl.ds(pl.multiple_of(self.p.c * HALF_IN, HALF_IN), HALF_IN)
        cols_in = pl.ds(pl.multiple_of(ch * W_IN_COLS, 128), W_IN_COLS)
        rows_out = pl.ds(pl.multiple_of(ch * W_OUT_ROWS + self.p.c * HALF_OUT, HALF_OUT), HALF_OUT)
        r_in = self.win.at[rows_in, cols_in]
        r_out = self.wout.at[rows_out, :]
        return [pltpu.make_async_remote_copy(r_in, r_in, self.send.at[2 * k], self.recv.at[2 * k],
                                             device_id=target, device_id_type=MESH),
                pltpu.make_async_remote_copy(r_out, r_out, self.send.at[2 * k + 1], self.recv.at[2 * k + 1],
                                             device_id=target, device_id_type=MESH)]

    def _start(self, k):
        for cp in self._copies(k):
            cp.start()

    def _landed(self, k):
        for cp in self._copies(k):
            cp.wait_recv()

    def start_first_round(self):
        self._start(0)

    def start_second_round(self):
        self._landed(0)
        self._start(1)
        self._start(2)
        self._start(3)

    def pass_second_round(self):
        self._landed(1)
        self._start(4)
        self._landed(2)
        self._start(5)

    def finish(self):
        for k in (3, 4, 5):
            self._landed(k)
        for k in range(len(self.plan)):
            for cp in self._copies(k):
                cp.wait_send()


def _prepare(c_vec, w_ada, b_ada, w_in, w_out):
    def body(c_ref, wada_ref, bada_ref, win_ref, wout_ref,
             win0, win1, wout0, wout1, mod_ref, c_all,
             win_bf, wout_bf, mod_mine, mod_all, g_send, g_recv, c_send, c_recv, mod_send, mod_recv, local_sem):
        p = _Place()
        win_bf[...] = win_ref[...].astype(BF16)
        wout_bf[...] = wout_ref[...].astype(BF16)
        cols = pl.ds(pl.multiple_of(p.chip * W_IN_COLS, 128), W_IN_COLS)
        rows = pl.ds(pl.multiple_of(p.chip * W_OUT_ROWS, W_OUT_ROWS), W_OUT_ROWS)
        own = [pltpu.make_async_copy(win_bf.at[0], win0.at[:, cols], local_sem.at[0]),
               pltpu.make_async_copy(wout_bf.at[0], wout0.at[rows, :], local_sem.at[1]),
               pltpu.make_async_copy(win_bf.at[1], win1.at[:, cols], local_sem.at[2]),
               pltpu.make_async_copy(wout_bf.at[1], wout1.at[rows, :], local_sem.at[3])]
        for cp in own:
            cp.start()

        c_all[pl.ds(p.dev, 1), :] = c_ref[...]
        c_copies = [pltpu.make_async_remote_copy(c_ref, c_all.at[pl.ds(p.dev, 1), :], c_send.at[r], c_recv.at[r],
                                                 device_id=d, device_id_type=MESH) for r, d in enumerate(p.others())]
        for cp in c_copies:
            cp.start()
        own[0].wait()
        own[1].wait()
        gather = _WeightGather(p, win0, wout0, g_send, g_recv)
        gather.start_first_round()
        for cp in c_copies:
            cp.wait()

        cv = c_all[...]
        silu_c = (cv * _sigmoid(cv)).astype(BF16)
        for l in range(DEPTH):
            mod_mine[l] = _dot(silu_c, wada_ref[l].astype(BF16))
        mod_all[p.chip] = mod_mine[...]
        m_copies = [pltpu.make_async_remote_copy(mod_mine, mod_all.at[p.chip], mod_send.at[k], mod_recv.at[k],
                                                 device_id=(px, py, p.c), device_id_type=MESH)
                    for k, (px, py) in enumerate(p.other_chips())]
        for cp in m_copies:
            cp.start()
        gather.start_second_round()
        for cp in m_copies:
            cp.wait()
        mod_ref[...] = jnp.zeros_like(mod_ref)
        for l in range(DEPTH):
            full = jnp.concatenate([mod_all[ch, l, pl.ds(p.dev, 1), :] for ch in range(N_CHIP)], axis=1) + bada_ref[l:l + 1, :]
            for k in range(3):
                mod_ref[l, k:k + 1, :] = full[:, k * D_MODEL:(k + 1) * D_MODEL]
        gather.pass_second_round()
        gather.finish()
        own[2].wait()
        own[3].wait()

    vmem = pl.BlockSpec(memory_space=pltpu.VMEM)
    hbm = pl.BlockSpec(memory_space=pl.ANY)
    w_in_shape = jax.ShapeDtypeStruct((D_MODEL, D_PROJ), BF16)
    w_out_shape = jax.ShapeDtypeStruct((D_MODEL, D_MODEL), BF16)
    return pl.pallas_call(
        body, name="prepare",
        in_specs=[vmem, vmem, vmem, vmem, vmem],
        out_specs=[hbm, hbm, hbm, hbm, vmem, vmem],
        out_shape=[w_in_shape, w_in_shape, w_out_shape, w_out_shape,
                   jax.ShapeDtypeStruct((DEPTH, 8, D_MODEL), F32), jax.ShapeDtypeStruct((N_DEV, D_MODEL), F32)],
        scratch_shapes=[
            pltpu.VMEM((DEPTH, D_MODEL, W_IN_COLS), BF16), pltpu.VMEM((DEPTH, W_OUT_ROWS, D_MODEL), BF16),
            pltpu.VMEM((DEPTH, N_DEV, W_ADA_COLS), F32), pltpu.VMEM((N_CHIP, DEPTH, N_DEV, W_ADA_COLS), F32),
            pltpu.SemaphoreType.DMA((_WeightGather.N_SEMS,)), pltpu.SemaphoreType.DMA((_WeightGather.N_SEMS,)),
            pltpu.SemaphoreType.DMA((7,)), pltpu.SemaphoreType.DMA((7,)),
            pltpu.SemaphoreType.DMA((3,)), pltpu.SemaphoreType.DMA((3,)),
            pltpu.SemaphoreType.DMA((4,)),
        ],
        compiler_params=pltpu.CompilerParams(vmem_limit_bytes=VMEM_LIMIT),
    )(c_vec, w_ada, b_ada, w_in, w_out)


IN_STEPS = W_IN_COLS // HEAD
OUT_STEPS = 4
OUT_COLS = D_MODEL // OUT_STEPS
N_ITEMS = IN_STEPS + OUT_STEPS
DELAY_SUM, DELAY_SECOND, DELAY_FINAL = 1, 3, 5

FIN_IN = jax.ShapeDtypeStruct((D_MODEL, W_IN_COLS), F32)
FIN_OUT = jax.ShapeDtypeStruct((2, HALF_OUT, D_MODEL), F32)


def _wgrad_reduce(name, h, dproj, cat, dy, pack=None, dmod=None):
    shares = pack is not None

    def body(*refs):
        h_ref, dp_refs, cat_ref, dy_ref = refs[0], refs[1:5], refs[5], refs[6]
        n_in = 9 if shares else 7
        fin_in, fin_out = refs[n_in], refs[n_in + 1]
        scratch = refs[n_in + (4 if shares else 2):]
        (mine_in, send_in, sib_in, st_in, r1_in, r2_in, f_in,
         mine_out, send_out, sib_out, st_out, r1_out, r2_out, f_out,
         d2d_s, d2d_r, r1_s, r1_r, r2_s, r2_r, fin_l, fin_s, fin_r) = scratch[:23]
        p = _Place()
        c = p.c
        i = pl.program_id(0)
        my_rows = pl.ds(pl.multiple_of(c * HALF_IN, HALF_IN), HALF_IN)

        def bufs(j):
            if j < IN_STEPS:
                return [r.at[j] for r in (mine_in, send_in, sib_in, st_in, r1_in, r2_in, f_in)]
            return [r.at[j - IN_STEPS] for r in (mine_out, send_out, sib_out, st_out, r1_out, r2_out, f_out)]

        def piece(j, ref, ch):
            if j < IN_STEPS:
                return ref.at[:, ch * HEAD:(ch + 1) * HEAD]
            return ref.at[ch]

        def slot(ch):
            return jnp.where(c == 0, ch % 2, ch // 2)

        def to_sibling(j):
            _, send, sib, _, _, _, _ = bufs(j)
            return pltpu.make_async_remote_copy(send, sib, d2d_s.at[j], d2d_r.at[j], device_id=p.sibling, device_id_type=MESH)

        def first_round(j, ch):
            _, _, _, st, r1, _, _ = bufs(j)
            k = slot(ch)
            return pltpu.make_async_remote_copy(st.at[k], r1.at[k], r1_s.at[2 * j + k], r1_r.at[2 * j + k],
                                                device_id=p.first, device_id_type=MESH)

        def second_round(j):
            _, _, _, st, _, r2, _ = bufs(j)
            return pltpu.make_async_remote_copy(st.at[2], r2, r2_s.at[j], r2_r.at[j], device_id=p.second, device_id_type=MESH)

        def finals(j):
            f = bufs(j)[6]
            if j < IN_STEPS:
                dst = fin_in.at[my_rows, j * HEAD:(j + 1) * HEAD]
            else:
                dst = fin_out.at[c, :, (j - IN_STEPS) * OUT_COLS:(j - IN_STEPS + 1) * OUT_COLS]
            return [pltpu.make_async_copy(f, dst, fin_l.at[j]),
                    pltpu.make_async_remote_copy(f, dst, fin_s.at[j], fin_r.at[j], device_id=p.sibling, device_id_type=MESH)]

        def stage_sum(j):
            mine, _, sib, st, _, _, _ = bufs(j)
            to_sibling(j).wait_recv()
            mine[...] = mine[...] + sib[...]
            for ch in range(N_CHIP):
                @pl.when(p.first_coord(ch) != p.my_first_coord)
                def _(ch=ch):
                    st[slot(ch)] = piece(j, mine, ch)[...].astype(BF16)
                    first_round(j, ch).start()

        def stage_second(j):
            mine, _, _, st, r1, _, _ = bufs(j)
            for ch in range(N_CHIP):
                @pl.when(p.first_coord(ch) == p.my_first_coord)
                def _(ch=ch):
                    first_round(j, ch).wait_recv()
                    part = piece(j, mine, ch)
                    total = part[...] + r1[slot(ch)].astype(F32)
                    part[...] = total

                    @pl.when(ch != p.chip)
                    def _():
                        st[2] = total.astype(BF16)
                        second_round(j).start()

        def stage_final(j):
            mine, _, _, _, _, r2, f = bufs(j)
            second_round(j).wait_recv()
            for ch in range(N_CHIP):
                @pl.when(ch == p.chip)
                def _(ch=ch):
                    f[...] = piece(j, mine, ch)[...] + r2[...].astype(F32)
            for cp in finals(j):
                cp.start()

        def drain(j):
            to_sibling(j).wait_send()
            for ch in range(N_CHIP):
                @pl.when(p.first_coord(ch) != p.my_first_coord)
                def _(ch=ch):
                    first_round(j, ch).wait_send()

                @pl.when(jnp.logical_and(p.first_coord(ch) == p.my_first_coord, ch != p.chip))
                def _():
                    second_round(j).wait_send()
            for cp in finals(j):
                cp.wait()

        if shares:
            pack_ref, dmod_ref, pack_out, dmod_out = refs[7], refs[8], refs[n_in + 2], refs[n_in + 3]
            (p_recv, p_sum, p_all, dm_st, dmod_all, pk_send, pk_recv, ag_send, ag_recv, dm_send, dm_recv,
             own_sem) = scratch[23:]
            dev = p.dev
            devices = p.others()

            def chunk(ref, d):
                return ref.at[d // 4, pl.ds(pl.multiple_of((d % 4) * PK_CHUNK, 8), PK_CHUNK), :]

            def scatter(r):
                dx, dy_, dc = devices[r]
                return pltpu.make_async_remote_copy(chunk(pack_ref, 4 * dx + 2 * dy_ + dc), p_recv.at[dev], pk_send.at[r],
                                                    pk_recv.at[r], device_id=devices[r], device_id_type=MESH)

            def spread(r):
                return pltpu.make_async_remote_copy(p_sum, chunk(p_all, dev), ag_send.at[r], ag_recv.at[r],
                                                    device_id=devices[r], device_id_type=MESH)

            def own_chunk():
                return pltpu.make_async_copy(chunk(pack_ref, dev), p_recv.at[dev], own_sem.at[0])

            def results():
                return [pltpu.make_async_copy(p_all, pack_out, own_sem.at[0]),
                        pltpu.make_async_copy(dmod_all, dmod_out, own_sem.at[1])]

            def dmod_copy(r):
                return pltpu.make_async_remote_copy(dm_st, dmod_all.at[:, :, pl.ds(dev, 1), :], dm_send.at[r], dm_recv.at[r],
                                                    device_id=devices[r], device_id_type=MESH)

            @pl.when(i == 0)
            def _():
                own_chunk().start()
                for r in range(N_DEV - 1):
                    scatter(r).start()
                for l in range(DEPTH):
                    for k in range(3):
                        for r in range(D_MODEL // HEAD):
                            dm_st[l, 8 * k + r] = dmod_ref[l, k:k + 1, r * HEAD:(r + 1) * HEAD]
                dmod_all[:, :, pl.ds(dev, 1), :] = dm_st[...]
                for r in range(N_DEV - 1):
                    dmod_copy(r).start()

            @pl.when(i == 2)
            def _():
                own_chunk().wait()
                for r in range(N_DEV - 1):
                    scatter(r).wait()
                acc = p_recv[0]
                for d in range(1, N_DEV):
                    acc = acc + p_recv[d]
                p_sum[...] = acc
                chunk(p_all, dev)[...] = acc
                for r in range(N_DEV - 1):
                    spread(r).start()

        @pl.when(i < IN_STEPS)
        def _():
            rhs = jnp.concatenate([r[...] for r in dp_refs], axis=1)
            res = _dot(h_ref[...], rhs, TN)

            @pl.when(c == 0)
            def _():
                mine_in[i] = res[:HALF_IN]
                send_in[i] = res[HALF_IN:]

            @pl.when(c == 1)
            def _():
                mine_in[i] = res[HALF_IN:]
                send_in[i] = res[:HALF_IN]

        @pl.when(i >= IN_STEPS)
        def _():
            k = i - IN_STEPS
            res = _dot(cat_ref[...], dy_ref[...], TN)

            @pl.when(c == 0)
            def _():
                for ch in range(N_CHIP):
                    mine_out[k, ch] = res[ch * W_OUT_ROWS:ch * W_OUT_ROWS + HALF_OUT]
                    send_out[k, ch] = res[ch * W_OUT_ROWS + HALF_OUT:(ch + 1) * W_OUT_ROWS]

            @pl.when(c == 1)
            def _():
                for ch in range(N_CHIP):
                    mine_out[k, ch] = res[ch * W_OUT_ROWS + HALF_OUT:(ch + 1) * W_OUT_ROWS]
                    send_out[k, ch] = res[ch * W_OUT_ROWS:ch * W_OUT_ROWS + HALF_OUT]

        stages = ((0, lambda j: to_sibling(j).start()), (DELAY_SUM, stage_sum), (DELAY_SECOND, stage_second),
                  (DELAY_FINAL, stage_final))
        for step in range(N_ITEMS):
            @pl.when(i == step)
            def _(step=step):
                for delay, stage in stages:
                    if step - delay >= 0:
                        stage(step - delay)

        @pl.when(i == N_ITEMS - 1)
        def _():
            for step in range(N_ITEMS, N_ITEMS + DELAY_FINAL):
                for delay, stage in stages:
                    if 0 <= step - delay < N_ITEMS:
                        stage(step - delay)
            for j in range(N_ITEMS):
                drain(j)
            if shares:
                for r in range(N_DEV - 1):
                    spread(r).wait()
                    dmod_copy(r).wait()
                for cp in results():
                    cp.start()
                for cp in results():
                    cp.wait()

    hbm = pl.BlockSpec(memory_space=pl.ANY)

    def dproj_piece(ch):
        return pl.BlockSpec((SEQ, HEAD), lambda i: (0, ch * IN_STEPS + jnp.minimum(i, IN_STEPS - 1)))

    in_specs = [_const_in((SEQ, D_MODEL))] + [dproj_piece(ch) for ch in range(N_CHIP)]
    in_specs += [_const_in((SEQ, D_MODEL)), pl.BlockSpec((SEQ, OUT_COLS), lambda i: (0, jnp.maximum(i - IN_STEPS, 0)))]
    args = [h, dproj, dproj, dproj, dproj, cat, dy]
    out_shape = [FIN_IN, FIN_OUT]
    out_specs = [hbm, hbm]
    in_item = lambda *lead: pltpu.VMEM(lead + (HALF_IN, HEAD), BF16)
    out_item = lambda *lead: pltpu.VMEM(lead + (HALF_OUT, OUT_COLS), BF16)
    scratch = [
        pltpu.VMEM((IN_STEPS, HALF_IN, N_CHIP * HEAD), F32), pltpu.VMEM((IN_STEPS, HALF_IN, N_CHIP * HEAD), F32),
        pltpu.VMEM((IN_STEPS, HALF_IN, N_CHIP * HEAD), F32), in_item(IN_STEPS, 3), in_item(IN_STEPS, 2), in_item(IN_STEPS),
        pltpu.VMEM((IN_STEPS, HALF_IN, HEAD), F32),
        pltpu.VMEM((OUT_STEPS, N_CHIP, HALF_OUT, OUT_COLS), F32), pltpu.VMEM((OUT_STEPS, N_CHIP, HALF_OUT, OUT_COLS), F32),
        pltpu.VMEM((OUT_STEPS, N_CHIP, HALF_OUT, OUT_COLS), F32), out_item(OUT_STEPS, 3), out_item(OUT_STEPS, 2),
        out_item(OUT_STEPS), pltpu.VMEM((OUT_STEPS, HALF_OUT, OUT_COLS), F32),
        pltpu.SemaphoreType.DMA((N_ITEMS,)), pltpu.SemaphoreType.DMA((N_ITEMS,)),
        pltpu.SemaphoreType.DMA((2 * N_ITEMS,)), pltpu.SemaphoreType.DMA((2 * N_ITEMS,)),
        pltpu.SemaphoreType.DMA((N_ITEMS,)), pltpu.SemaphoreType.DMA((N_ITEMS,)),
        pltpu.SemaphoreType.DMA((N_ITEMS,)), pltpu.SemaphoreType.DMA((N_ITEMS,)), pltpu.SemaphoreType.DMA((N_ITEMS,)),
    ]
    if shares:
        in_specs += [hbm, _const_in((DEPTH, 8, D_MODEL))]
        args += [pack, dmod]
        out_shape += [jax.ShapeDtypeStruct((DEPTH, PK_ROWS, HEAD), F32), jax.ShapeDtypeStruct((DEPTH, 24, N_DEV, HEAD), F32)]
        out_specs += [hbm, hbm]
        scratch += [pltpu.VMEM((N_DEV, PK_CHUNK, HEAD), F32), pltpu.VMEM((PK_CHUNK, HEAD), F32),
                    pltpu.VMEM((DEPTH, PK_ROWS, HEAD), F32), pltpu.VMEM((DEPTH, 24, 1, HEAD), F32),
                    pltpu.VMEM((DEPTH, 24, N_DEV, HEAD), F32)]
        scratch += [pltpu.SemaphoreType.DMA((N_DEV - 1,))] * 6 + [pltpu.SemaphoreType.DMA((2,))]
    return pl.pallas_call(
        body, name=name, grid=(N_ITEMS,), in_specs=in_specs, out_specs=out_specs, out_shape=out_shape,
        scratch_shapes=scratch,
        compiler_params=pltpu.CompilerParams(dimension_semantics=("arbitrary",), vmem_limit_bytes=VMEM_LIMIT),
    )(*args)


def _adamw(w, g, m, v):
    m = ADAM_B1 * m + (1.0 - ADAM_B1) * g
    v = ADAM_B2 * v + (1.0 - ADAM_B2) * (g * g)
    m_hat = m / (1.0 - ADAM_B1 ** ADAM_STEP)
    v_hat = v / (1.0 - ADAM_B2 ** ADAM_STEP)
    delta = -ADAM_LR * (m_hat / (jnp.sqrt(v_hat) + ADAM_EPS) + ADAM_WD * w)
    return delta, m, v


def _adam_sharded(name, w, g_first, g_last, m, v, rows):
    _, r, cols = w.shape
    n = r // rows

    def body(w_ref, g0_ref, g1_ref, m_ref, v_ref, g_out, d_out, m_out, v_out):
        @pl.when(pl.program_id(0) == 0)
        def _():
            g_out[...] = g0_ref[...]

        @pl.when(pl.program_id(0) == 1)
        def _():
            g_out[...] = g1_ref[...]

        d_out[...], m_out[...], v_out[...] = _adamw(w_ref[...], g_out[...], m_ref[...], v_ref[...])

    blk = pl.BlockSpec((None, rows, cols), lambda l, i: (l, i, 0))
    g0 = pl.BlockSpec((rows, cols), lambda l, i: (jnp.where(l == 0, i, n - 1), 0))
    g1 = pl.BlockSpec((rows, cols), lambda l, i: (jnp.where(l == 1, i, 0), 0))
    shape = jax.ShapeDtypeStruct(w.shape, F32)
    return pl.pallas_call(
        body, name=name, grid=(DEPTH, n), in_specs=[blk, g0, g1, blk, blk], out_specs=[blk] * 4, out_shape=[shape] * 4,
        compiler_params=pltpu.CompilerParams(dimension_semantics=("arbitrary", "arbitrary"), vmem_limit_bytes=VMEM_LIMIT),
    )(w, g_first, g_last, m, v)


def _adam_w_ada(w, m, v, c_all, dmod_all):
    rows = 256

    def body(c_ref, dm_ref, w_ref, m_ref, v_ref, g_out, d_out, m_out, v_out):
        l = pl.program_id(0)
        chip = 2 * lax.axis_index("x") + lax.axis_index("y")
        cv = c_ref[...]
        silu_c = (cv * _sigmoid(cv)).astype(BF16).astype(F32)
        pieces = []
        for k in range(W_ADA_COLS // HEAD):
            dk = dm_ref[l, 6 * chip + k].astype(BF16).astype(F32)
            pieces.append(_dot_exact(silu_c, dk, TN))
        g = jnp.concatenate(pieces, axis=1)
        g_out[...] = g
        d_out[...], m_out[...], v_out[...] = _adamw(w_ref[...], g, m_ref[...], v_ref[...])

    blk = pl.BlockSpec((None, rows, W_ADA_COLS), lambda l, i: (l, i, 0))
    shape = jax.ShapeDtypeStruct(w.shape, F32)
    return pl.pallas_call(
        body, name="adam_w_ada", grid=(DEPTH, D_MODEL // rows),
        in_specs=[pl.BlockSpec((N_DEV, rows), lambda l, i: (0, i)),
                  pl.BlockSpec((DEPTH, 24, N_DEV, HEAD), lambda l, i: (0, 0, 0, 0)), blk, blk, blk],
        out_specs=[blk] * 4, out_shape=[shape] * 4,
        compiler_params=pltpu.CompilerParams(dimension_semantics=("arbitrary", "arbitrary"), vmem_limit_bytes=VMEM_LIMIT),
    )(c_all, dmod_all, w, m, v)


def _adam_small(pack, dmod_all, weights, ms, vs):
    n = len(weights)

    def body(*refs):
        pack_ref, dm_ref = refs[0], refs[1]
        w_refs, m_refs, v_refs = refs[2:2 + n], refs[2 + n:2 + 2 * n], refs[2 + 2 * n:2 + 3 * n]
        outs = refs[2 + 3 * n:]
        g_refs, d_refs, nm_refs, nv_refs = outs[0:n], outs[n:2 * n], outs[2 * n:3 * n], outs[3 * n:4 * n]
        outs[4 * n][...] = pack_ref[DEPTH - 1, PK_LOSS:PK_LOSS + 1, 0:1] * (0.5 / D_MODEL)

        def lanes(l, row0, count):
            return jnp.concatenate([pack_ref[l, row0 + k:row0 + k + 1, :] for k in range(count)], axis=1)

        def update(idx, at, g):
            g_refs[idx][at] = g
            d_refs[idx][at], nm_refs[idx][at], nv_refs[idx][at] = _adamw(w_refs[idx][at], g, m_refs[idx][at], v_refs[idx][at])

        for l in range(DEPTH):
            row = (slice(l, l + 1), slice(None))
            g_b = None
            for d in range(N_DEV):
                part = dm_ref[l, :, d, :]
                g_b = part if g_b is None else g_b + part
            update(0, row, jnp.concatenate([g_b[k:k + 1, :] for k in range(24)], axis=1))
            for g in range(N_HEAD):
                update(1, (l, g), pack_ref[l, PK_W_POOL + g * HEAD:PK_W_POOL + (g + 1) * HEAD, :])
                update(5, (l, g), pack_ref[l, PK_W_SGU + g * HEAD:PK_W_SGU + (g + 1) * HEAD, :])
            update(2, row, lanes(l, PK_POOL_SCALE, N_HEAD))
            update(3, (l,), pack_ref[l, PK_SGU_LN_G:PK_SGU_LN_G + N_HEAD, :])
            update(4, (l,), pack_ref[l, PK_SGU_LN_B:PK_SGU_LN_B + N_HEAD, :])
            update(6, (l,), pack_ref[l, PK_B_SGU:PK_B_SGU + N_HEAD, :])
            update(7, row, lanes(l, PK_LN_G, D_MODEL // HEAD))
            update(8, row, lanes(l, PK_LN_B, D_MODEL // HEAD))

    vmem = pl.BlockSpec(memory_space=pltpu.VMEM)
    shapes = [jax.ShapeDtypeStruct(w.shape, F32) for w in weights]
    return pl.pallas_call(
        body, name="adam_small", in_specs=[vmem] * (2 + 3 * n), out_specs=[vmem] * (4 * n + 1),
        out_shape=shapes * 4 + [jax.ShapeDtypeStruct((1, 1), F32)],
        compiler_params=pltpu.CompilerParams(vmem_limit_bytes=VMEM_LIMIT),
    )(pack, dmod_all, *weights, *ms, *vs)


def kernel(x, c, w_ada, b_ada, w_in, w_pool, pool_scale, sgu_ln_g, sgu_ln_b, w_sgu, b_sgu, w_out, ln_g, ln_b, loss_target, m_w_ada, m_b_ada, m_w_in, m_w_pool, m_pool_scale, m_sgu_ln_g, m_sgu_ln_b, m_w_sgu, m_b_sgu, m_w_out, m_ln_g, m_ln_b, v_w_ada, v_b_ada, v_w_in, v_w_pool, v_pool_scale, v_sgu_ln_g, v_sgu_ln_b, v_w_sgu, v_b_sgu, v_w_out, v_ln_g, v_ln_b):
    w_in0, w_in1, w_out0, w_out1, mod, c_all = _prepare(c, w_ada, b_ada, w_in, w_out)
    small = (w_pool, pool_scale, sgu_ln_g, sgu_ln_b, w_sgu, jnp.swapaxes(b_sgu, 1, 2))

    proj0, y0, x1, w_in1, w_out1 = _forward_layer(0, x, mod, w_in0, w_out0, small, ln_g, ln_b,
                                                  next_weights=(w_in1, w_out1))
    proj1, y1, dout, sq = _forward_layer(1, x1, mod, w_in1, w_out1, small, ln_g, ln_b, target=loss_target)

    dx1, h1, cat1, dy1, dproj1, pack, dmod = _backward_layer(1, dout, x1, y1, proj1, mod, w_in1, w_out1, small, ln_g, sq=sq)
    g_in1, g_out1 = _wgrad_reduce("wgrad_last", h1, dproj1, cat1, dy1)
    dx0, h0, cat0, dy0, dproj0, pack, dmod = _backward_layer(0, dx1, x, y0, proj0, mod, w_in0, w_out0, small, ln_g,
                                                             pack=pack, dmod=dmod)
    g_in0, g_out0, pack, dmod_all = _wgrad_reduce("wgrad_first", h0, dproj0, cat0, dy0, pack=pack, dmod=dmod)

    ada = _adam_w_ada(w_ada, m_w_ada, v_w_ada, c_all, dmod_all)
    win = _adam_sharded("adam_w_in", w_in, g_in0, g_in1, m_w_in, v_w_in, 256)
    wout = _adam_sharded("adam_w_out", w_out, g_out0.reshape(W_OUT_ROWS, D_MODEL), g_out1.reshape(W_OUT_ROWS, D_MODEL),
                         m_w_out, v_w_out, 256)
    small_w = (b_ada, w_pool, pool_scale, sgu_ln_g, sgu_ln_b, w_sgu, b_sgu, ln_g, ln_b)
    small_m = (m_b_ada, m_w_pool, m_pool_scale, m_sgu_ln_g, m_sgu_ln_b, m_w_sgu, m_b_sgu, m_ln_g, m_ln_b)
    small_v = (v_b_ada, v_w_pool, v_pool_scale, v_sgu_ln_g, v_sgu_ln_b, v_w_sgu, v_b_sgu, v_ln_g, v_ln_b)
    res = _adam_small(pack, dmod_all, small_w, small_m, small_v)
    n = len(small_w)
    loss = res[4 * n].reshape(())

    def ordered(k):
        s = res[k * n:(k + 1) * n]
        return (ada[k], s[0], win[k], s[1], s[2], s[3], s[4], s[5], s[6], wout[k], s[7], s[8])

    return (loss, dx0[None], *ordered(0), *ordered(1), *ordered(2), *ordered(3))
```

```python
import jax
import jax.numpy as jnp
from jax import lax
from jax.experimental import pallas as pl
from jax.experimental.pallas import tpu as pltpu

F32 = jnp.float32
BF16 = jnp.bfloat16
MESH = pl.DeviceIdType.MESH

N_DEV = 8
N_CHIP = 4
DEPTH = 2
SEQ = 2048
D_MODEL = 1024
D_POOL = 512
D_PROJ = 2560
HEAD = 128
N_HEAD = 4
ROWS = 256
N_TILE = SEQ // ROWS
HALO = 16
W_IN_COLS = D_PROJ // N_CHIP
W_OUT_ROWS = D_MODEL // N_CHIP
W_ADA_COLS = 3 * D_MODEL // N_CHIP
HALF_IN = D_MODEL // 2
HALF_OUT = W_OUT_ROWS // 2
DEEPNORM_ALPHA = (2.0 * DEPTH) ** 0.25
LN_EPS = 1e-5
INV_SQRT2 = 0.7071067811865476
INV_SQRT_2PI = 0.3989422804014327

ADAM_LR = 0.001
ADAM_B1 = 0.9
ADAM_B2 = 0.999
ADAM_EPS = 1e-08
ADAM_WD = 0.01
ADAM_STEP = 10
ADAM_ROWS = 512

PK_W_POOL = 0
PK_W_SGU = 512
PK_POOL_SCALE = 1024
PK_SGU_LN_G = 1032
PK_SGU_LN_B = 1040
PK_B_SGU = 1048
PK_LN_G = 1056
PK_LN_B = 1064
PK_LOSS = 1072
PK_ROWS = 1088
PK_HALF = PK_ROWS // 2
PK_PIECE = PK_HALF // N_CHIP

VMEM_LIMIT = 56 * 1024 * 1024

GATHER_SECOND_ROUND_STEP = 3

NN = (((1,), (0,)), ((), ()))
NT = (((1,), (1,)), ((), ()))
TN = (((0,), (0,)), ((), ()))


def _dot(a, b, dims=NN):
    return lax.dot_general(a, b, dims, preferred_element_type=F32)


def _dot_exact(a, b, dims=NN):
    return lax.dot_general(a, b, dims, preferred_element_type=F32, precision=lax.Precision.HIGHEST)


def _layer_norm(v):
    mu = jnp.mean(v, axis=-1, keepdims=True)
    d = v - mu
    var = jnp.mean(d * d, axis=-1, keepdims=True)
    rstd = lax.rsqrt(var + LN_EPS)
    return d * rstd, rstd


def _layer_norm_bwd(dvhat, vhat, rstd):
    m1 = jnp.mean(dvhat, axis=-1, keepdims=True)
    m2 = jnp.mean(dvhat * vhat, axis=-1, keepdims=True)
    return rstd * (dvhat - m1 - vhat * m2)


def _sigmoid(v):
    return 1.0 / (1.0 + jnp.exp(-v))


def _gelu_parts(v):
    phi = 0.5 * (1.0 + lax.erf(v * INV_SQRT2))
    pdf = INV_SQRT_2PI * jnp.exp(-0.5 * v * v)
    return phi, pdf


def _sum_rows(v):
    return jnp.sum(v, axis=0, keepdims=True)


def _window_sums(ext, toward_later):
    n = ext.shape[0]

    def shifted(v, k):
        return pltpu.roll(v, (n - k) if toward_later else k, 0)

    s2 = ext + shifted(ext, 1)
    r4 = s2[:, HEAD:]
    s4 = r4 + shifted(r4, 2)
    r8 = s4[:, HEAD:]
    s8 = r8 + shifted(r8, 4)
    r16 = s8[:, HEAD:]
    s16 = r16 + shifted(r16, 8)
    return jnp.concatenate([s2[:, :HEAD], s4[:, :HEAD], s8[:, :HEAD], s16], axis=1)


def _window_counts(row0):
    t1 = row0 + 1 + lax.broadcasted_iota(jnp.int32, (ROWS, D_POOL), 0)
    lane = lax.broadcasted_iota(jnp.int32, (ROWS, D_POOL), 1)
    width = jnp.where(lane < HEAD, 2, jnp.where(lane < 2 * HEAD, 4, jnp.where(lane < 3 * HEAD, 8, 16)))
    return jnp.minimum(t1, width).astype(F32)


def _causal_mask():
    r = lax.broadcasted_iota(jnp.int32, (HEAD, HEAD), 0)
    s = lax.broadcasted_iota(jnp.int32, (HEAD, HEAD), 1)
    return r >= s


def _chunks_to_lanes(v):
    return jnp.concatenate([v[n * HEAD:(n + 1) * HEAD] for n in range(ROWS // HEAD)], axis=1)


def _lanes_to_chunks(v):
    return jnp.concatenate([v[:, n * HEAD:(n + 1) * HEAD] for n in range(ROWS // HEAD)], axis=0)


def _mixer(proj, halo, row0, wpool_ref, pscale, sgu_g_ref, sgu_b_ref, wsgu_ref, bsgu_t_ref):
    xa = proj[:, 0:512]
    ga = proj[:, 512:1024]
    u = proj[:, 1024:1536]
    v = proj[:, 1536:2048]
    gb = proj[:, 2048:2560]
    ext = jnp.concatenate([halo, xa], axis=0)
    win = _window_sums(ext, toward_later=False)[HALO:]
    cnt = _window_counts(row0)
    pooled = (win / cnt - xa).astype(BF16)
    pw = jnp.concatenate(
        [_dot(pooled[:, g * HEAD:(g + 1) * HEAD], wpool_ref[g].astype(BF16)) for g in range(N_HEAD)], axis=1)
    sig_a = _sigmoid(ga)
    ya = pw * pscale * (ga * sig_a)
    phi_u, pdf_u = _gelu_parts(u)
    phi_v, pdf_v = _gelu_parts(v)
    gu = u * phi_u
    gv = v * phi_v
    sig_b = _sigmoid(gb)
    silu_b = gb * sig_b
    mask = _causal_mask()
    vhat, rstd_v, vln_l, mixed = [], [], [], []
    for h in range(N_HEAD):
        vh, rh = _layer_norm(gv[:, h * HEAD:(h + 1) * HEAD])
        ln = (vh * sgu_g_ref[h:h + 1, :] + sgu_b_ref[h:h + 1, :]).astype(BF16)
        ln_l = _chunks_to_lanes(ln)
        wm = jnp.where(mask, wsgu_ref[h], 0.0).astype(BF16)
        mx = _lanes_to_chunks(_dot(wm, ln_l) + bsgu_t_ref[:, h:h + 1])
        vhat.append(vh)
        rstd_v.append(rh)
        vln_l.append(ln_l)
        mixed.append(mx)
    mixed = jnp.concatenate(mixed, axis=1)
    yb = gu * mixed * silu_b
    return dict(xa=xa, ga=ga, u=u, gb=gb, cnt=cnt, pooled=pooled, pw=pw, sig_a=sig_a, ya=ya, phi_u=phi_u, pdf_u=pdf_u,
                phi_v=phi_v, pdf_v=pdf_v, gu=gu, sig_b=sig_b, silu_b=silu_b, vhat=vhat, rstd_v=rstd_v, vln_l=vln_l,
                mixed=mixed, yb=yb, mask=mask)


def _const(shape, *index):
    lead = tuple(index) + (0,) * (len(shape) - len(index))
    return pl.BlockSpec(shape, lambda *_: lead)


def _const_in(shape, *index):
    lead = tuple(index) + (0,) * (len(shape) - len(index))
    return pl.BlockSpec(shape, lambda *_: lead, pipeline_mode=pl.Buffered(1))


def _layer_weight_specs(l):
    return [
        _const_in((None, N_HEAD, HEAD, HEAD), l),
        _const_in((DEPTH, D_POOL)),
        _const_in((None, N_HEAD, HEAD), l),
        _const_in((None, N_HEAD, HEAD), l),
        _const_in((None, N_HEAD, HEAD, HEAD), l),
        _const_in((None, HEAD, N_HEAD), l),
    ]


def _forward_layer(l, x, mod, w_in, w_out, small, ln_g, ln_b, target=None, next_weights=None):
    last = target is not None
    gathers = next_weights is not None

    def body(*refs):
        if last:
            (x_ref, mod_ref, win_ref, wout_ref, wpool_ref, pscale_ref, sgu_g_ref, sgu_b_ref, wsgu_ref, bsgu_t_ref,
             lng_ref, lnb_ref, tgt_ref, proj_ref, y_ref, out_ref, loss_ref, carry_ref) = refs
        elif gathers:
            (x_ref, mod_ref, win_ref, wout_ref, wpool_ref, pscale_ref, sgu_g_ref, sgu_b_ref, wsgu_ref, bsgu_t_ref,
             lng_ref, lnb_ref, _, _, proj_ref, y_ref, out_ref, next_in, next_out, carry_ref, g_send, g_recv) = refs
            gather = _WeightGather(_Place(), next_in, next_out, g_send, g_recv)
        else:
            (x_ref, mod_ref, win_ref, wout_ref, wpool_ref, pscale_ref, sgu_g_ref, sgu_b_ref, wsgu_ref, bsgu_t_ref,
             lng_ref, lnb_ref, proj_ref, y_ref, out_ref, carry_ref) = refs
        i = pl.program_id(0)

        @pl.when(i == 0)
        def _():
            carry_ref[...] = jnp.zeros_like(carry_ref)
            if last:
                loss_ref[...] = jnp.zeros_like(loss_ref)
            if gathers:
                gather.start_first_round()

        if gathers:
            @pl.when(i == GATHER_SECOND_ROUND_STEP)
            def _():
                gather.start_second_round()

        x = x_ref[...]
        shift, scale, gate = mod_ref[0:1, :], mod_ref[1:2, :], mod_ref[2:3, :]
        xn, _ = _layer_norm(x)
        h = xn * (1.0 + scale) + shift
        proj = _dot(h.astype(BF16), win_ref[...])
        proj_ref[...] = proj
        m = _mixer(proj, carry_ref[...], i * ROWS, wpool_ref, pscale_ref[l:l + 1, :], sgu_g_ref, sgu_b_ref, wsgu_ref,
                   bsgu_t_ref)
        carry_ref[...] = m["xa"][ROWS - HALO:]
        cat = jnp.concatenate([m["ya"], m["yb"]], axis=1).astype(BF16)
        y = _dot(cat, wout_ref[...])
        y_ref[...] = y
        zn, _ = _layer_norm(DEEPNORM_ALPHA * x + gate * y)
        out = zn * lng_ref[l:l + 1, :] + lnb_ref[l:l + 1, :]
        if last:
            err = out - tgt_ref[...]
            out_ref[...] = err * (1.0 / D_MODEL)
            loss_ref[...] += jnp.sum(err * err)
        else:
            out_ref[...] = out

        if gathers:
            @pl.when(i == N_TILE - 1)
            def _():
                gather.pass_second_round()
                gather.finish()

    tile = pl.BlockSpec((ROWS, D_MODEL), lambda i: (i, 0))
    tile3 = pl.BlockSpec((None, ROWS, D_MODEL), lambda i: (0, i, 0))
    in_specs = [tile3 if x.ndim == 3 else tile, _const_in((None, 8, D_MODEL), l), _const_in((D_MODEL, D_PROJ)),
                _const_in((D_MODEL, D_MODEL))]
    in_specs += _layer_weight_specs(l) + [_const_in((DEPTH, D_MODEL)), _const_in((DEPTH, D_MODEL))]
    out_shape = [jax.ShapeDtypeStruct((SEQ, D_PROJ), F32), jax.ShapeDtypeStruct((SEQ, D_MODEL), F32),
                 jax.ShapeDtypeStruct((SEQ, D_MODEL), F32)]
    out_specs = [pl.BlockSpec((ROWS, D_PROJ), lambda i: (i, 0)), tile, tile]
    args = [x, mod, w_in, w_out, *small, ln_g, ln_b]
    scratch = [pltpu.VMEM((HALO, D_POOL), F32)]
    aliases = {}
    if last:
        in_specs.append(tile3)
        args.append(target)
        out_shape.append(jax.ShapeDtypeStruct((8, HEAD), F32))
        out_specs.append(_const((8, HEAD)))
    if gathers:
        hbm = pl.BlockSpec(memory_space=pl.ANY)
        aliases = {len(args): len(out_shape), len(args) + 1: len(out_shape) + 1}
        in_specs += [hbm, hbm]
        args += list(next_weights)
        out_shape += [jax.ShapeDtypeStruct(w.shape, BF16) for w in next_weights]
        out_specs += [hbm, hbm]
        scratch += [pltpu.SemaphoreType.DMA((_WeightGather.N_SEMS,)), pltpu.SemaphoreType.DMA((_WeightGather.N_SEMS,))]
    return pl.pallas_call(
        body, name="fwd_last" if last else "fwd_first", grid=(N_TILE,), in_specs=in_specs, out_specs=out_specs,
        out_shape=out_shape, scratch_shapes=scratch, input_output_aliases=aliases,
        compiler_params=pltpu.CompilerParams(dimension_semantics=("arbitrary",), vmem_limit_bytes=VMEM_LIMIT),
    )(*args)


def _backward_layer(l, dout, x, y, proj, mod, w_in, w_out, small, ln_g, sq=None):
    has_loss = sq is not None

    def body(*refs):
        (dout_ref, x_ref, y_ref, proj_ref, halo_ref, mod_ref, win_ref, wout_ref, wpool_ref, pscale_ref,
         sgu_g_ref, sgu_b_ref, wsgu_ref, bsgu_t_ref, lng_ref) = refs[:15]
        dx_ref, h_ref, cat_ref, dy_ref, dproj_ref, pack_ref, dmod_ref, carry_ref = refs[-8:]
        i = pl.program_id(0)
        tile = N_TILE - 1 - i

        @pl.when(i == 0)
        def _():
            carry_ref[...] = jnp.zeros_like(carry_ref)
            pack_ref[...] = jnp.zeros_like(pack_ref)
            dmod_ref[...] = jnp.zeros_like(dmod_ref)
            if has_loss:
                pack_ref[PK_LOSS:PK_LOSS + 8, :] = refs[15][...]

        x = x_ref[...]
        y = y_ref[...]
        dout = dout_ref[...]
        pscale = pscale_ref[l:l + 1, :]
        shift, scale, gate = mod_ref[0:1, :], mod_ref[1:2, :], mod_ref[2:3, :]
        xn, rstd_x = _layer_norm(x)
        h = xn * (1.0 + scale) + shift
        h_ref[...] = h.astype(BF16)
        zn, rstd_z = _layer_norm(DEEPNORM_ALPHA * x + gate * y)
        g_ln_g = _sum_rows(dout * zn)
        g_ln_b = _sum_rows(dout)
        dz = _layer_norm_bwd(dout * lng_ref[l:l + 1, :], zn, rstd_z)
        d_gate = _sum_rows(dz * y)
        dy = (gate * dz).astype(BF16)
        dy_ref[...] = dy

        halo = jnp.where(tile > 0, halo_ref[...], 0.0)
        m = _mixer(proj_ref[...], halo, tile * ROWS, wpool_ref, pscale, sgu_g_ref, sgu_b_ref, wsgu_ref, bsgu_t_ref)
        cat_ref[...] = jnp.concatenate([m["ya"], m["yb"]], axis=1).astype(BF16)
        dcat = _dot(dy, wout_ref[...], NT)
        dya = dcat[:, :D_POOL]
        dyb = dcat[:, D_POOL:]

        ga, sig_a = m["ga"], m["sig_a"]
        dp = dya * (ga * sig_a)
        d_ga = dya * (m["pw"] * pscale) * (sig_a * (1.0 + ga * (1.0 - sig_a)))
        g_pscale = _sum_rows(dp * m["pw"])
        dpw = (dp * pscale).astype(BF16)
        dpooled = []
        for g in range(N_HEAD):
            cols = slice(g * HEAD, (g + 1) * HEAD)
            pack_ref[PK_W_POOL + g * HEAD:PK_W_POOL + (g + 1) * HEAD, :] += _dot(m["pooled"][:, cols], dpw[:, cols], TN)
            dpooled.append(_dot(dpw[:, cols], wpool_ref[g].astype(BF16), NT))
        dpooled = jnp.concatenate(dpooled, axis=1)
        q = dpooled / m["cnt"]
        ext = jnp.concatenate([q, carry_ref[...]], axis=0)
        d_xa = _window_sums(ext, toward_later=True)[:ROWS] - dpooled
        carry_ref[...] = q[:HALO]

        gu, mixed, silu_b, gb, sig_b = m["gu"], m["mixed"], m["silu_b"], m["gb"], m["sig_b"]
        d_mixed = dyb * gu * silu_b
        d_gu = dyb * mixed * silu_b
        d_gb = dyb * gu * mixed * (sig_b * (1.0 + gb * (1.0 - sig_b)))
        d_u = d_gu * (m["phi_u"] + m["u"] * m["pdf_u"])
        ones = jnp.ones((8, HEAD), F32)
        d_v = []
        for hd in range(N_HEAD):
            cols = slice(hd * HEAD, (hd + 1) * HEAD)
            dm = d_mixed[:, cols]
            dm_l = _chunks_to_lanes(dm.astype(BF16))
            g_w = _dot(dm_l, m["vln_l"][hd], NT)
            pack_ref[PK_W_SGU + hd * HEAD:PK_W_SGU + (hd + 1) * HEAD, :] += jnp.where(m["mask"], g_w, 0.0)
            dm_sum = dm[0:HEAD]
            for n in range(1, ROWS // HEAD):
                dm_sum = dm_sum + dm[n * HEAD:(n + 1) * HEAD]
            pack_ref[PK_B_SGU + hd:PK_B_SGU + hd + 1, :] += _dot_exact(ones, dm_sum, NT)[0:1]
            wm = jnp.where(m["mask"], wsgu_ref[hd], 0.0).astype(BF16)
            d_vln = _lanes_to_chunks(_dot(wm, dm_l, TN))
            vhat = m["vhat"][hd]
            pack_ref[PK_SGU_LN_G + hd:PK_SGU_LN_G + hd + 1, :] += _sum_rows(d_vln * vhat)
            pack_ref[PK_SGU_LN_B + hd:PK_SGU_LN_B + hd + 1, :] += _sum_rows(d_vln)
            d_v.append(_layer_norm_bwd(d_vln * sgu_g_ref[hd:hd + 1, :], vhat, m["rstd_v"][hd]))
        v = proj_ref[:, 1536:2048]
        d_v = jnp.concatenate(d_v, axis=1) * (m["phi_v"] + v * m["pdf_v"])

        dproj = jnp.concatenate([d_xa, d_ga, d_u, d_v, d_gb], axis=1).astype(BF16)
        dproj_ref[...] = dproj
        dh = _dot(dproj, win_ref[...], NT)
        d_scale = _sum_rows(dh * xn)
        d_shift = _sum_rows(dh)
        dx_ref[...] = DEEPNORM_ALPHA * dz + _layer_norm_bwd(dh * (1.0 + scale), xn, rstd_x)

        dmod_ref[0:1, :] += d_shift
        dmod_ref[1:2, :] += d_scale
        dmod_ref[2:3, :] += d_gate
        for g in range(N_HEAD):
            pack_ref[PK_POOL_SCALE + g:PK_POOL_SCALE + g + 1, :] += g_pscale[:, g * HEAD:(g + 1) * HEAD]
        for k in range(D_MODEL // HEAD):
            pack_ref[PK_LN_G + k:PK_LN_G + k + 1, :] += g_ln_g[:, k * HEAD:(k + 1) * HEAD]
            pack_ref[PK_LN_B + k:PK_LN_B + k + 1, :] += g_ln_b[:, k * HEAD:(k + 1) * HEAD]

    def rev(i):
        return (N_TILE - 1 - i, 0)

    tile = pl.BlockSpec((ROWS, D_MODEL), rev)
    tile3 = pl.BlockSpec((None, ROWS, D_MODEL), lambda i: (0, N_TILE - 1 - i, 0))
    halo = pl.BlockSpec((HALO, D_POOL), lambda i: (jnp.maximum((N_TILE - 1 - i) * (ROWS // HALO) - 1, 0), 0))
    in_specs = [tile, tile3 if x.ndim == 3 else tile, tile, pl.BlockSpec((ROWS, D_PROJ), rev), halo,
                _const_in((None, 8, D_MODEL), l), _const_in((D_MODEL, D_PROJ)), _const_in((D_MODEL, D_MODEL))]
    in_specs += _layer_weight_specs(l) + [_const_in((DEPTH, D_MODEL))]
    args = [dout, x, y, proj, proj, mod, w_in, w_out, *small, ln_g]
    out_shape = [jax.ShapeDtypeStruct((SEQ, D_MODEL), F32), jax.ShapeDtypeStruct((SEQ, D_MODEL), BF16),
                 jax.ShapeDtypeStruct((SEQ, D_MODEL), BF16), jax.ShapeDtypeStruct((SEQ, D_MODEL), BF16),
                 jax.ShapeDtypeStruct((SEQ, D_PROJ), BF16), jax.ShapeDtypeStruct((PK_ROWS, HEAD), F32),
                 jax.ShapeDtypeStruct((8, D_MODEL), F32)]
    out_specs = [tile, tile, tile, tile, pl.BlockSpec((ROWS, D_PROJ), rev), _const((PK_ROWS, HEAD)), _const((8, D_MODEL))]
    if has_loss:
        in_specs.append(_const_in((8, HEAD)))
        args.append(sq)
    return pl.pallas_call(
        body, name="bwd_last" if has_loss else "bwd_first", grid=(N_TILE,), in_specs=in_specs, out_specs=out_specs,
        out_shape=out_shape, scratch_shapes=[pltpu.VMEM((HALO, D_POOL), F32)],
        compiler_params=pltpu.CompilerParams(dimension_semantics=("arbitrary",), vmem_limit_bytes=VMEM_LIMIT),
    )(*args)


def _flip(v, f):
    return v + f - 2 * v * f


class _Place:
    def __init__(self):
        x, y, c = lax.axis_index("x"), lax.axis_index("y"), lax.axis_index("c")
        self.x, self.y, self.c = x, y, c
        self.chip = 2 * x + y
        self.dev = 4 * x + 2 * y + c
        self.sibling = (x, y, 1 - c)
        x1, y1 = _flip(x, 1 - c), _flip(y, c)
        x2, y2 = _flip(x, c), _flip(y, 1 - c)
        self.first = (x1, y1, c)
        self.second = (x2, y2, c)
        self.chip_first = 2 * x1 + y1
        self.chip_second = 2 * x2 + y2
        self.chip_far = 2 * (1 - x) + (1 - y)
        self.my_first_coord = jnp.where(c == 0, x, y)

    def first_coord(self, ch):
        return jnp.where(self.c == 0, ch // 2, ch % 2)

    def others(self):
        return [(_flip(self.x, (r >> 2) & 1), _flip(self.y, (r >> 1) & 1), _flip(self.c, r & 1)) for r in range(1, N_DEV)]

    def other_chips(self):
        return [(1 - self.x, self.y), (self.x, 1 - self.y), (1 - self.x, 1 - self.y)]


class _WeightGather:
    N_SEMS = 12

    def __init__(self, place, win, wout, send, recv):
        self.p, self.win, self.wout, self.send, self.recv = place, win, wout, send, recv
        p = place
        self.plan = [(p.chip, p.first), (p.chip, p.second), (p.chip_first, p.second),
                     (p.chip_first, p.sibling), (p.chip_second, p.sibling), (p.chip_far, p.sibling)]

    def _copies(self, k):
        ch, target = self.plan[k]
        rows_in = pl.ds(pl.multiple_of(self.p.c * HALF_IN, HALF_IN), HALF_IN)
        cols_in = pl.ds(pl.multiple_of(ch * W_IN_COLS, 128), W_IN_COLS)
        rows_out = pl.ds(pl.multiple_of(ch * W_OUT_ROWS + self.p.c * HALF_OUT, HALF_OUT), HALF_OUT)
        r_in = self.win.at[rows_in, cols_in]
        r_out = self.wout.at[rows_out, :]
        return [pltpu.make_async_remote_copy(r_in, r_in, self.send.at[2 * k], self.recv.at[2 * k],
                                             device_id=target, device_id_type=MESH),
                pltpu.make_async_remote_copy(r_out, r_out, self.send.at[2 * k + 1], self.recv.at[2 * k + 1],
                                             device_id=target, device_id_type=MESH)]

    def _start(self, k):
        for cp in self._copies(k):
            cp.start()

    def _landed(self, k):
        for cp in self._copies(k):
            cp.wait_recv()

    def start_first_round(self):
        self._start(0)

    def start_second_round(self):
        self._landed(0)
        self._start(1)
        self._start(2)
        self._start(3)

    def pass_second_round(self):
        self._landed(1)
        self._start(4)
        self._landed(2)
        self._start(5)

    def finish(self):
        for k in (3, 4, 5):
            self._landed(k)
        for k in range(len(self.plan)):
            for cp in self._copies(k):
                cp.wait_send()


def _prepare(c_vec, w_ada, b_ada, w_in, w_out):
    def body(c_ref, wada_ref, bada_ref, win_ref, wout_ref,
             win0, win1, wout0, wout1, mod_ref, c_all,
             win_bf, wout_bf, mod_mine, mod_all, g_send, g_recv, c_send, c_recv, mod_send, mod_recv, local_sem):
        p = _Place()
        win_bf[...] = win_ref[...].astype(BF16)
        wout_bf[...] = wout_ref[...].astype(BF16)
        cols = pl.ds(pl.multiple_of(p.chip * W_IN_COLS, 128), W_IN_COLS)
        rows = pl.ds(pl.multiple_of(p.chip * W_OUT_ROWS, W_OUT_ROWS), W_OUT_ROWS)
        own = [pltpu.make_async_copy(win_bf.at[0], win0.at[:, cols], local_sem.at[0]),
               pltpu.make_async_copy(wout_bf.at[0], wout0.at[rows, :], local_sem.at[1]),
               pltpu.make_async_copy(win_bf.at[1], win1.at[:, cols], local_sem.at[2]),
               pltpu.make_async_copy(wout_bf.at[1], wout1.at[rows, :], local_sem.at[3])]
        for cp in own:
            cp.start()

        c_all[pl.ds(p.dev, 1), :] = c_ref[...]
        c_copies = [pltpu.make_async_remote_copy(c_ref, c_all.at[pl.ds(p.dev, 1), :], c_send.at[r], c_recv.at[r],
                                                 device_id=d, device_id_type=MESH) for r, d in enumerate(p.others())]
        for cp in c_copies:
            cp.start()
        own[0].wait()
        own[1].wait()
        gather = _WeightGather(p, win0, wout0, g_send, g_recv)
        gather.start_first_round()
        for cp in c_copies:
            cp.wait()

        cv = c_all[...]
        silu_c = (cv * _sigmoid(cv)).astype(BF16)
        for l in range(DEPTH):
            mod_mine[l] = _dot(silu_c, wada_ref[l].astype(BF16))
        mod_all[p.chip] = mod_mine[...]
        m_copies = [pltpu.make_async_remote_copy(mod_mine, mod_all.at[p.chip], mod_send.at[k], mod_recv.at[k],
                                                 device_id=(px, py, p.c), device_id_type=MESH)
                    for k, (px, py) in enumerate(p.other_chips())]
        for cp in m_copies:
            cp.start()
        gather.start_second_round()
        for cp in m_copies:
            cp.wait()
        mod_ref[...] = jnp.zeros_like(mod_ref)
        for l in range(DEPTH):
            full = jnp.concatenate([mod_all[ch, l, pl.ds(p.dev, 1), :] for ch in range(N_CHIP)], axis=1) + bada_ref[l:l + 1, :]
            for k in range(3):
                mod_ref[l, k:k + 1, :] = full[:, k * D_MODEL:(k + 1) * D_MODEL]
        gather.pass_second_round()
        gather.finish()
        own[2].wait()
        own[3].wait()

    vmem = pl.BlockSpec(memory_space=pltpu.VMEM)
    hbm = pl.BlockSpec(memory_space=pl.ANY)
    w_in_shape = jax.ShapeDtypeStruct((D_MODEL, D_PROJ), BF16)
    w_out_shape = jax.ShapeDtypeStruct((D_MODEL, D_MODEL), BF16)
    return pl.pallas_call(
        body, name="prepare",
        in_specs=[vmem, vmem, vmem, vmem, vmem],
        out_specs=[hbm, hbm, hbm, hbm, vmem, vmem],
        out_shape=[w_in_shape, w_in_shape, w_out_shape, w_out_shape,
                   jax.ShapeDtypeStruct((DEPTH, 8, D_MODEL), F32), jax.ShapeDtypeStruct((N_DEV, D_MODEL), F32)],
        scratch_shapes=[
            pltpu.VMEM((DEPTH, D_MODEL, W_IN_COLS), BF16), pltpu.VMEM((DEPTH, W_OUT_ROWS, D_MODEL), BF16),
            pltpu.VMEM((DEPTH, N_DEV, W_ADA_COLS), F32), pltpu.VMEM((N_CHIP, DEPTH, N_DEV, W_ADA_COLS), F32),
            pltpu.SemaphoreType.DMA((_WeightGather.N_SEMS,)), pltpu.SemaphoreType.DMA((_WeightGather.N_SEMS,)),
            pltpu.SemaphoreType.DMA((7,)), pltpu.SemaphoreType.DMA((7,)),
            pltpu.SemaphoreType.DMA((3,)), pltpu.SemaphoreType.DMA((3,)),
            pltpu.SemaphoreType.DMA((4,)),
        ],
        compiler_params=pltpu.CompilerParams(vmem_limit_bytes=VMEM_LIMIT),
    )(c_vec, w_ada, b_ada, w_in, w_out)


IN_STEPS = W_IN_COLS // HEAD
OUT_STEPS = 4
OUT_COLS = D_MODEL // OUT_STEPS
OUT_FIRST = 2
ITEMS = ([("out", k) for k in range(OUT_FIRST)] + [("in", k) for k in range(IN_STEPS)]
         + [("out", k) for k in range(OUT_FIRST, OUT_STEPS)])
N_ITEMS = len(ITEMS)
DELAY_SUM, DELAY_SECOND, DELAY_FINAL = 1, 3, 5
SMALL_SCATTER_STEP, SMALL_GATHER_STEP, SMALL_PASS_STEP = 1, 3, 5

FIN_IN = jax.ShapeDtypeStruct((D_MODEL, W_IN_COLS), F32)
FIN_OUT = jax.ShapeDtypeStruct((2, HALF_OUT, D_MODEL), F32)


def _wgrad_reduce(name, h, dproj, cat, dy, pack, dmod):
    def body(*refs):
        h_ref, dp_refs, cat_ref, dy_ref, pack_ref, dmod_ref = refs[0], refs[1:5], refs[5], refs[6], refs[7], refs[8]
        fin_in, fin_out, pack_out, dmod_out = refs[9:13]
        scratch = refs[13:]
        (mine_in, send_in, sib_in, st_in, r1_in, r2_in, f_in,
         mine_out, send_out, sib_out, st_out, r1_out, r2_out, f_out,
         d2d_s, d2d_r, r1_s, r1_r, r2_s, r2_r, fin_l, fin_s, fin_r) = scratch[:23]
        p = _Place()
        c = p.c
        i = pl.program_id(0)
        my_rows = pl.ds(pl.multiple_of(c * HALF_IN, HALF_IN), HALF_IN)

        def bufs(j):
            kind, k = ITEMS[j]
            if kind == "in":
                return [r.at[k] for r in (mine_in, send_in, sib_in, st_in, r1_in, r2_in, f_in)]
            return [r.at[k] for r in (mine_out, send_out, sib_out, st_out, r1_out, r2_out, f_out)]

        def piece(j, ref, ch):
            if ITEMS[j][0] == "in":
                return ref.at[:, ch * HEAD:(ch + 1) * HEAD]
            return ref.at[ch]

        def slot(ch):
            return jnp.where(c == 0, ch % 2, ch // 2)

        def to_sibling(j):
            _, send, sib, _, _, _, _ = bufs(j)
            return pltpu.make_async_remote_copy(send, sib, d2d_s.at[j], d2d_r.at[j], device_id=p.sibling, device_id_type=MESH)

        def first_round(j, ch):
            _, _, _, st, r1, _, _ = bufs(j)
            k = slot(ch)
            return pltpu.make_async_remote_copy(st.at[k], r1.at[k], r1_s.at[2 * j + k], r1_r.at[2 * j + k],
                                                device_id=p.first, device_id_type=MESH)

        def second_round(j):
            _, _, _, st, _, r2, _ = bufs(j)
            return pltpu.make_async_remote_copy(st.at[2], r2, r2_s.at[j], r2_r.at[j], device_id=p.second, device_id_type=MESH)

        def finals(j):
            f = bufs(j)[6]
            kind, k = ITEMS[j]
            if kind == "in":
                dst = fin_in.at[my_rows, k * HEAD:(k + 1) * HEAD]
            else:
                dst = fin_out.at[c, :, k * OUT_COLS:(k + 1) * OUT_COLS]
            return [pltpu.make_async_copy(f, dst, fin_l.at[j]),
                    pltpu.make_async_remote_copy(f, dst, fin_s.at[j], fin_r.at[j], device_id=p.sibling, device_id_type=MESH)]

        def stage_sum(j):
            mine, _, sib, st, _, _, _ = bufs(j)
            to_sibling(j).wait_recv()
            mine[...] = mine[...] + sib[...]
            for ch in range(N_CHIP):
                @pl.when(p.first_coord(ch) != p.my_first_coord)
                def _(ch=ch):
                    st[slot(ch)] = piece(j, mine, ch)[...].astype(BF16)
                    first_round(j, ch).start()

        def stage_second(j):
            mine, _, _, st, r1, _, _ = bufs(j)
            for ch in range(N_CHIP):
                @pl.when(p.first_coord(ch) == p.my_first_coord)
                def _(ch=ch):
                    first_round(j, ch).wait_recv()
                    part = piece(j, mine, ch)
                    total = part[...] + r1[slot(ch)].astype(F32)
                    part[...] = total

                    @pl.when(ch != p.chip)
                    def _():
                        st[2] = total.astype(BF16)
                        second_round(j).start()

        def stage_final(j):
            mine, _, _, _, _, r2, f = bufs(j)
            second_round(j).wait_recv()
            for ch in range(N_CHIP):
                @pl.when(ch == p.chip)
                def _(ch=ch):
                    f[...] = piece(j, mine, ch)[...] + r2[...].astype(F32)
            for cp in finals(j):
                cp.start()

        def drain(j):
            to_sibling(j).wait_send()
            for ch in range(N_CHIP):
                @pl.when(p.first_coord(ch) != p.my_first_coord)
                def _(ch=ch):
                    first_round(j, ch).wait_send()

                @pl.when(jnp.logical_and(p.first_coord(ch) == p.my_first_coord, ch != p.chip))
                def _():
                    second_round(j).wait_send()
            for cp in finals(j):
                cp.wait()

        (pk_mine, pk_sib, pk_rs, pk_fin, pk_all, dm_st, dm_all, pk_sem, rs_s, rs_r, ag_s, ag_r, dm_s, dm_r) = scratch[23:]
        dev = p.dev
        devices = p.others()

        def half(core):
            return pl.ds(pl.multiple_of(core * PK_HALF, 8), PK_HALF)

        def finished(core, ch):
            return pl.ds(pl.multiple_of(core * PK_HALF + ch * PK_PIECE, 8), PK_PIECE)

        def pk_load():
            return pltpu.make_async_copy(pack_ref.at[half(c)], pk_mine, pk_sem.at[0])

        def pk_give():
            return pltpu.make_async_remote_copy(pack_ref.at[half(1 - c)], pk_sib, pk_sem.at[1], pk_sem.at[2],
                                                device_id=p.sibling, device_id_type=MESH)

        def pk_scatter(ch):
            return pltpu.make_async_remote_copy(pk_mine.at[ch * PK_PIECE:(ch + 1) * PK_PIECE], pk_rs.at[p.chip], rs_s.at[ch],
                                                rs_r.at[p.chip], device_id=(ch // 2, ch % 2, c), device_id_type=MESH)

        def pk_spread(ch):
            return pltpu.make_async_remote_copy(pk_fin, pk_all.at[finished(c, p.chip)], ag_s.at[ch], ag_r.at[p.chip],
                                                device_id=(ch // 2, ch % 2, c), device_id_type=MESH)

        def pk_pass():
            return pltpu.make_async_remote_copy(pk_all.at[half(c)], pk_all.at[half(c)], pk_sem.at[3], pk_sem.at[4],
                                                device_id=p.sibling, device_id_type=MESH)

        def dm_copy(r):
            return pltpu.make_async_remote_copy(dm_st, dm_all.at[:, pl.ds(dev, 1), :], dm_s.at[r], dm_r.at[r],
                                                device_id=devices[r], device_id_type=MESH)

        def results():
            return [pltpu.make_async_copy(pk_all, pack_out, pk_sem.at[0]), pltpu.make_async_copy(dm_all, dmod_out, pk_sem.at[5])]

        @pl.when(i == 0)
        def _():
            pk_load().start()
            pk_give().start()
            for k in range(3):
                for r in range(D_MODEL // HEAD):
                    dm_st[8 * k + r] = dmod_ref[k:k + 1, r * HEAD:(r + 1) * HEAD]
            dm_all[:, pl.ds(dev, 1), :] = dm_st[...]
            for r in range(N_DEV - 1):
                dm_copy(r).start()

        @pl.when(i == SMALL_SCATTER_STEP)
        def _():
            pk_load().wait()
            pk_give().wait()
            pk_mine[...] = pk_mine[...] + pk_sib[...]
            for ch in range(N_CHIP):
                @pl.when(ch != p.chip)
                def _(ch=ch):
                    pk_scatter(ch).start()

        @pl.when(i == SMALL_GATHER_STEP)
        def _():
            for ch in range(N_CHIP):
                @pl.when(ch != p.chip)
                def _(ch=ch):
                    pltpu.make_async_remote_copy(pk_fin, pk_rs.at[ch], rs_s.at[ch], rs_r.at[ch],
                                                 device_id=p.sibling, device_id_type=MESH).wait_recv()
            for me in range(N_CHIP):
                @pl.when(me == p.chip)
                def _(me=me):
                    total = None
                    for ch in range(N_CHIP):
                        part = pk_mine[me * PK_PIECE:(me + 1) * PK_PIECE] if ch == me else pk_rs[ch]
                        total = part if total is None else total + part
                    pk_fin[...] = total
                    pk_all[finished(c, me)] = total
            for ch in range(N_CHIP):
                @pl.when(ch != p.chip)
                def _(ch=ch):
                    pk_spread(ch).start()

        @pl.when(i == SMALL_PASS_STEP)
        def _():
            for ch in range(N_CHIP):
                @pl.when(ch != p.chip)
                def _(ch=ch):
                    pltpu.make_async_remote_copy(pk_fin, pk_all.at[finished(c, ch)], ag_s.at[ch], ag_r.at[ch],
                                                 device_id=p.sibling, device_id_type=MESH).wait_recv()
            pk_pass().start()

        def small_finish():
            pk_pass().wait()
            for ch in range(N_CHIP):
                @pl.when(ch != p.chip)
                def _(ch=ch):
                    pk_scatter(ch).wait_send()
                    pk_spread(ch).wait_send()
            for r in range(N_DEV - 1):
                dm_copy(r).wait()
            for cp in results():
                cp.start()
            for cp in results():
                cp.wait()

        in_step = jnp.logical_and(i >= OUT_FIRST, i < OUT_FIRST + IN_STEPS)

        @pl.when(in_step)
        def _():
            k = i - OUT_FIRST
            rhs = jnp.concatenate([r[...] for r in dp_refs], axis=1)
            res = _dot(h_ref[...], rhs, TN)

            @pl.when(c == 0)
            def _():
                mine_in[k] = res[:HALF_IN]
                send_in[k] = res[HALF_IN:]

            @pl.when(c == 1)
            def _():
                mine_in[k] = res[HALF_IN:]
                send_in[k] = res[:HALF_IN]

        @pl.when(jnp.logical_not(in_step))
        def _():
            k = jnp.where(i < OUT_FIRST, i, i - IN_STEPS)
            res = _dot(cat_ref[...], dy_ref[...], TN)

            @pl.when(c == 0)
            def _():
                for ch in range(N_CHIP):
                    mine_out[k, ch] = res[ch * W_OUT_ROWS:ch * W_OUT_ROWS + HALF_OUT]
                    send_out[k, ch] = res[ch * W_OUT_ROWS + HALF_OUT:(ch + 1) * W_OUT_ROWS]

            @pl.when(c == 1)
            def _():
                for ch in range(N_CHIP):
                    mine_out[k, ch] = res[ch * W_OUT_ROWS + HALF_OUT:(ch + 1) * W_OUT_ROWS]
                    send_out[k, ch] = res[ch * W_OUT_ROWS:ch * W_OUT_ROWS + HALF_OUT]

        stages = ((0, lambda j: to_sibling(j).start()), (DELAY_SUM, stage_sum), (DELAY_SECOND, stage_second),
                  (DELAY_FINAL, stage_final))
        for step in range(N_ITEMS):
            @pl.when(i == step)
            def _(step=step):
                for delay, stage in stages:
                    if step - delay >= 0:
                        stage(step - delay)

        @pl.when(i == N_ITEMS - 1)
        def _():
            for step in range(N_ITEMS, N_ITEMS + DELAY_FINAL):
                for delay, stage in stages:
                    if 0 <= step - delay < N_ITEMS:
                        stage(step - delay)
            small_finish()
            for j in range(N_ITEMS):
                drain(j)

    hbm = pl.BlockSpec(memory_space=pl.ANY)

    def dproj_piece(ch):
        return pl.BlockSpec((SEQ, HEAD), lambda i: (0, ch * IN_STEPS + jnp.clip(i - OUT_FIRST, 0, IN_STEPS - 1)))

    def dy_quarter(i):
        return (0, jnp.where(i < OUT_FIRST, i, jnp.maximum(i - IN_STEPS, OUT_FIRST)))

    in_specs = [_const_in((SEQ, D_MODEL))] + [dproj_piece(ch) for ch in range(N_CHIP)]
    in_specs += [_const_in((SEQ, D_MODEL)), pl.BlockSpec((SEQ, OUT_COLS), dy_quarter), hbm, _const_in((8, D_MODEL))]
    args = [h, dproj, dproj, dproj, dproj, cat, dy, pack, dmod]
    out_shape = [FIN_IN, FIN_OUT, jax.ShapeDtypeStruct((PK_ROWS, HEAD), F32), jax.ShapeDtypeStruct((24, N_DEV, HEAD), F32)]
    out_specs = [hbm, hbm, hbm, hbm]
    in_item = lambda *lead: pltpu.VMEM(lead + (HALF_IN, HEAD), BF16)
    out_item = lambda *lead: pltpu.VMEM(lead + (HALF_OUT, OUT_COLS), BF16)
    scratch = [
        pltpu.VMEM((IN_STEPS, HALF_IN, N_CHIP * HEAD), F32), pltpu.VMEM((IN_STEPS, HALF_IN, N_CHIP * HEAD), F32),
        pltpu.VMEM((IN_STEPS, HALF_IN, N_CHIP * HEAD), F32), in_item(IN_STEPS, 3), in_item(IN_STEPS, 2), in_item(IN_STEPS),
        pltpu.VMEM((IN_STEPS, HALF_IN, HEAD), F32),
        pltpu.VMEM((OUT_STEPS, N_CHIP, HALF_OUT, OUT_COLS), F32), pltpu.VMEM((OUT_STEPS, N_CHIP, HALF_OUT, OUT_COLS), F32),
        pltpu.VMEM((OUT_STEPS, N_CHIP, HALF_OUT, OUT_COLS), F32), out_item(OUT_STEPS, 3), out_item(OUT_STEPS, 2),
        out_item(OUT_STEPS), pltpu.VMEM((OUT_STEPS, HALF_OUT, OUT_COLS), F32),
        pltpu.SemaphoreType.DMA((N_ITEMS,)), pltpu.SemaphoreType.DMA((N_ITEMS,)),
        pltpu.SemaphoreType.DMA((2 * N_ITEMS,)), pltpu.SemaphoreType.DMA((2 * N_ITEMS,)),
        pltpu.SemaphoreType.DMA((N_ITEMS,)), pltpu.SemaphoreType.DMA((N_ITEMS,)),
        pltpu.SemaphoreType.DMA((N_ITEMS,)), pltpu.SemaphoreType.DMA((N_ITEMS,)), pltpu.SemaphoreType.DMA((N_ITEMS,)),
    ]
    scratch += [
        pltpu.VMEM((PK_HALF, HEAD), F32), pltpu.VMEM((PK_HALF, HEAD), F32), pltpu.VMEM((N_CHIP, PK_PIECE, HEAD), F32),
        pltpu.VMEM((PK_PIECE, HEAD), F32), pltpu.VMEM((PK_ROWS, HEAD), F32),
        pltpu.VMEM((24, 1, HEAD), F32), pltpu.VMEM((24, N_DEV, HEAD), F32),
        pltpu.SemaphoreType.DMA((6,)),
        pltpu.SemaphoreType.DMA((N_CHIP,)), pltpu.SemaphoreType.DMA((N_CHIP,)),
        pltpu.SemaphoreType.DMA((N_CHIP,)), pltpu.SemaphoreType.DMA((N_CHIP,)),
        pltpu.SemaphoreType.DMA((N_DEV - 1,)), pltpu.SemaphoreType.DMA((N_DEV - 1,)),
    ]
    return pl.pallas_call(
        body, name=name, grid=(N_ITEMS,), in_specs=in_specs, out_specs=out_specs, out_shape=out_shape,
        scratch_shapes=scratch,
        compiler_params=pltpu.CompilerParams(dimension_semantics=("arbitrary",), vmem_limit_bytes=VMEM_LIMIT),
    )(*args)


def _adamw(w, g, m, v):
    m = ADAM_B1 * m + (1.0 - ADAM_B1) * g
    v = ADAM_B2 * v + (1.0 - ADAM_B2) * (g * g)
    m_hat = m / (1.0 - ADAM_B1 ** ADAM_STEP)
    v_hat = v / (1.0 - ADAM_B2 ** ADAM_STEP)
    delta = -ADAM_LR * (m_hat / (jnp.sqrt(v_hat) + ADAM_EPS) + ADAM_WD * w)
    return delta, m, v


def _adam_sharded(name, w, g_first, g_last, m, v, rows):
    _, r, cols = w.shape
    n = r // rows

    def body(w_ref, g0_ref, g1_ref, m_ref, v_ref, g_out, d_out, m_out, v_out):
        @pl.when(pl.program_id(0) == 0)
        def _():
            g_out[...] = g0_ref[...]

        @pl.when(pl.program_id(0) == 1)
        def _():
            g_out[...] = g1_ref[...]

        d_out[...], m_out[...], v_out[...] = _adamw(w_ref[...], g_out[...], m_ref[...], v_ref[...])

    blk = pl.BlockSpec((None, rows, cols), lambda l, i: (l, i, 0))
    g0 = pl.BlockSpec((rows, cols), lambda l, i: (jnp.where(l == 0, i, n - 1), 0))
    g1 = pl.BlockSpec((rows, cols), lambda l, i: (jnp.where(l == 1, i, 0), 0))
    shape = jax.ShapeDtypeStruct(w.shape, F32)
    return pl.pallas_call(
        body, name=name, grid=(DEPTH, n), in_specs=[blk, g0, g1, blk, blk], out_specs=[blk] * 4, out_shape=[shape] * 4,
        compiler_params=pltpu.CompilerParams(dimension_semantics=("arbitrary", "arbitrary"), vmem_limit_bytes=VMEM_LIMIT),
    )(w, g_first, g_last, m, v)


def _adam_w_ada(w, m, v, c_all, dmods):
    rows = ADAM_ROWS

    def body(c_ref, dm0_ref, dm1_ref, w_ref, m_ref, v_ref, g_out, d_out, m_out, v_out):
        l = pl.program_id(0)
        chip = 2 * lax.axis_index("x") + lax.axis_index("y")
        cv = c_ref[...]
        silu_c = (cv * _sigmoid(cv)).astype(BF16).astype(F32)
        pieces = []
        for k in range(W_ADA_COLS // HEAD):
            row = (W_ADA_COLS // HEAD) * chip + k
            dk = jnp.where(l == 0, dm0_ref[row], dm1_ref[row]).astype(BF16).astype(F32)
            pieces.append(_dot_exact(silu_c, dk, TN))
        g = jnp.concatenate(pieces, axis=1)
        g_out[...] = g
        d_out[...], m_out[...], v_out[...] = _adamw(w_ref[...], g, m_ref[...], v_ref[...])

    blk = pl.BlockSpec((None, rows, W_ADA_COLS), lambda l, i: (l, i, 0))
    shape = jax.ShapeDtypeStruct(w.shape, F32)
    return pl.pallas_call(
        body, name="adam_w_ada", grid=(DEPTH, D_MODEL // rows),
        in_specs=[pl.BlockSpec((N_DEV, rows), lambda l, i: (0, i)), _const_in((24, N_DEV, HEAD)),
                  _const_in((24, N_DEV, HEAD)), blk, blk, blk],
        out_specs=[blk] * 4, out_shape=[shape] * 4,
        compiler_params=pltpu.CompilerParams(dimension_semantics=("arbitrary", "arbitrary"), vmem_limit_bytes=VMEM_LIMIT),
    )(c_all, *dmods, w, m, v)


def _adam_small(packs, dmods, weights, ms, vs):
    n = len(weights)

    def body(*refs):
        pack_refs, dm_refs = refs[0:DEPTH], refs[DEPTH:2 * DEPTH]
        b = 2 * DEPTH
        w_refs, m_refs, v_refs = refs[b:b + n], refs[b + n:b + 2 * n], refs[b + 2 * n:b + 3 * n]
        outs = refs[b + 3 * n:]
        g_refs, d_refs, nm_refs, nv_refs = outs[0:n], outs[n:2 * n], outs[2 * n:3 * n], outs[3 * n:4 * n]
        outs[4 * n][...] = pack_refs[DEPTH - 1][PK_LOSS:PK_LOSS + 1, 0:1] * (0.5 / D_MODEL)

        def lanes(l, row0, count):
            return jnp.concatenate([pack_refs[l][row0 + k:row0 + k + 1, :] for k in range(count)], axis=1)

        def update(idx, at, g):
            g_refs[idx][at] = g
            d_refs[idx][at], nm_refs[idx][at], nv_refs[idx][at] = _adamw(w_refs[idx][at], g, m_refs[idx][at], v_refs[idx][at])

        for l in range(DEPTH):
            row = (slice(l, l + 1), slice(None))
            g_b = None
            for d in range(N_DEV):
                part = dm_refs[l][:, d, :]
                g_b = part if g_b is None else g_b + part
            update(0, row, jnp.concatenate([g_b[k:k + 1, :] for k in range(24)], axis=1))
            for g in range(N_HEAD):
                update(1, (l, g), pack_refs[l][PK_W_POOL + g * HEAD:PK_W_POOL + (g + 1) * HEAD, :])
                update(5, (l, g), pack_refs[l][PK_W_SGU + g * HEAD:PK_W_SGU + (g + 1) * HEAD, :])
            update(2, row, lanes(l, PK_POOL_SCALE, N_HEAD))
            update(3, (l,), pack_refs[l][PK_SGU_LN_G:PK_SGU_LN_G + N_HEAD, :])
            update(4, (l,), pack_refs[l][PK_SGU_LN_B:PK_SGU_LN_B + N_HEAD, :])
            update(6, (l,), pack_refs[l][PK_B_SGU:PK_B_SGU + N_HEAD, :])
            update(7, row, lanes(l, PK_LN_G, D_MODEL // HEAD))
            update(8, row, lanes(l, PK_LN_B, D_MODEL // HEAD))

    vmem = pl.BlockSpec(memory_space=pltpu.VMEM)
    shapes = [jax.ShapeDtypeStruct(w.shape, F32) for w in weights]
    return pl.pallas_call(
        body, name="adam_small", in_specs=[vmem] * (2 * DEPTH + 3 * n), out_specs=[vmem] * (4 * n + 1),
        out_shape=shapes * 4 + [jax.ShapeDtypeStruct((1, 1), F32)],
        compiler_params=pltpu.CompilerParams(vmem_limit_bytes=VMEM_LIMIT),
    )(*packs, *dmods, *weights, *ms, *vs)


def kernel(x, c, w_ada, b_ada, w_in, w_pool, pool_scale, sgu_ln_g, sgu_ln_b, w_sgu, b_sgu, w_out, ln_g, ln_b, loss_target, m_w_ada, m_b_ada, m_w_in, m_w_pool, m_pool_scale, m_sgu_ln_g, m_sgu_ln_b, m_w_sgu, m_b_sgu, m_w_out, m_ln_g, m_ln_b, v_w_ada, v_b_ada, v_w_in, v_w_pool, v_pool_scale, v_sgu_ln_g, v_sgu_ln_b, v_w_sgu, v_b_sgu, v_w_out, v_ln_g, v_ln_b):
    w_in0, w_in1, w_out0, w_out1, mod, c_all = _prepare(c, w_ada, b_ada, w_in, w_out)
    small = (w_pool, pool_scale, sgu_ln_g, sgu_ln_b, w_sgu, jnp.swapaxes(b_sgu, 1, 2))

    proj0, y0, x1, w_in1, w_out1 = _forward_layer(0, x, mod, w_in0, w_out0, small, ln_g, ln_b,
                                                  next_weights=(w_in1, w_out1))
    proj1, y1, dout, sq = _forward_layer(1, x1, mod, w_in1, w_out1, small, ln_g, ln_b, target=loss_target)

    dx1, h1, cat1, dy1, dproj1, pack1, dmod1 = _backward_layer(1, dout, x1, y1, proj1, mod, w_in1, w_out1, small, ln_g, sq=sq)
    g_in1, g_out1, pack1, dmods1 = _wgrad_reduce("wgrad_last", h1, dproj1, cat1, dy1, pack1, dmod1)
    dx0, h0, cat0, dy0, dproj0, pack0, dmod0 = _backward_layer(0, dx1, x, y0, proj0, mod, w_in0, w_out0, small, ln_g)
    g_in0, g_out0, pack0, dmods0 = _wgrad_reduce("wgrad_first", h0, dproj0, cat0, dy0, pack0, dmod0)

    ada = _adam_w_ada(w_ada, m_w_ada, v_w_ada, c_all, (dmods0, dmods1))
    win = _adam_sharded("adam_w_in", w_in, g_in0, g_in1, m_w_in, v_w_in, ADAM_ROWS)
    wout = _adam_sharded("adam_w_out", w_out, g_out0.reshape(W_OUT_ROWS, D_MODEL), g_out1.reshape(W_OUT_ROWS, D_MODEL),
                         m_w_out, v_w_out, W_OUT_ROWS)
    small_w = (b_ada, w_pool, pool_scale, sgu_ln_g, sgu_ln_b, w_sgu, b_sgu, ln_g, ln_b)
    small_m = (m_b_ada, m_w_pool, m_pool_scale, m_sgu_ln_g, m_sgu_ln_b, m_w_sgu, m_b_sgu, m_ln_g, m_ln_b)
    small_v = (v_b_ada, v_w_pool, v_pool_scale, v_sgu_ln_g, v_sgu_ln_b, v_w_sgu, v_b_sgu, v_ln_g, v_ln_b)
    res = _adam_small((pack0, pack1), (dmods0, dmods1), small_w, small_m, small_v)
    n = len(small_w)
    loss = res[4 * n].reshape(())

    def ordered(k):
        s = res[k * n:(k + 1) * n]
        return (ada[k], s[0], win[k], s[1], s[2], s[3], s[4], s[5], s[6], wout[k], s[7], s[8])

    return (loss, dx0[None], *ordered(0), *ordered(1), *ordered(2), *ordered(3))
```

```python
import jax
import jax.numpy as jnp
from jax import lax
from jax.experimental import pallas as pl
from jax.experimental.pallas import tpu as pltpu

F32 = jnp.float32
BF16 = jnp.bfloat16
MESH = pl.DeviceIdType.MESH

N_DEV = 8
N_CHIP = 4
DEPTH = 2
SEQ = 2048
D_MODEL = 1024
D_POOL = 512
D_PROJ = 2560
HEAD = 128
N_HEAD = 4
ROWS = 256
N_TILE = SEQ // ROWS
HALO = 16
W_IN_COLS = D_PROJ // N_CHIP
W_OUT_ROWS = D_MODEL // N_CHIP
W_ADA_COLS = 3 * D_MODEL // N_CHIP
HALF_IN = D_MODEL // 2
HALF_OUT = W_OUT_ROWS // 2
DEEPNORM_ALPHA = (2.0 * DEPTH) ** 0.25
LN_EPS = 1e-5
INV_SQRT2 = 0.7071067811865476
INV_SQRT_2PI = 0.3989422804014327

ADAM_LR = 0.001
ADAM_B1 = 0.9
ADAM_B2 = 0.999
ADAM_EPS = 1e-08
ADAM_WD = 0.01
ADAM_STEP = 10
ADAM_ROWS = 512

PK_W_POOL = 0
PK_W_SGU = 512
PK_POOL_SCALE = 1024
PK_SGU_LN_G = 1032
PK_SGU_LN_B = 1040
PK_B_SGU = 1048
PK_LN_G = 1056
PK_LN_B = 1064
PK_LOSS = 1072
PK_ROWS = 1088
PK_HALF = PK_ROWS // 2
PK_PIECE = PK_HALF // N_CHIP

VMEM_LIMIT = 56 * 1024 * 1024

GATHER_SECOND_ROUND_STEP = 3

NN = (((1,), (0,)), ((), ()))
NT = (((1,), (1,)), ((), ()))
TN = (((0,), (0,)), ((), ()))


def _dot(a, b, dims=NN):
    return lax.dot_general(a, b, dims, preferred_element_type=F32)


def _dot_exact(a, b, dims=NN):
    return lax.dot_general(a, b, dims, preferred_element_type=F32, precision=lax.Precision.HIGHEST)


def _layer_norm(v):
    mu = jnp.mean(v, axis=-1, keepdims=True)
    d = v - mu
    var = jnp.mean(d * d, axis=-1, keepdims=True)
    rstd = lax.rsqrt(var + LN_EPS)
    return d * rstd, rstd


def _layer_norm_bwd(dvhat, vhat, rstd):
    m1 = jnp.mean(dvhat, axis=-1, keepdims=True)
    m2 = jnp.mean(dvhat * vhat, axis=-1, keepdims=True)
    return rstd * (dvhat - m1 - vhat * m2)


def _sigmoid(v):
    return 1.0 / (1.0 + jnp.exp(-v))


def _gelu_parts(v):
    phi = 0.5 * (1.0 + lax.erf(v * INV_SQRT2))
    pdf = INV_SQRT_2PI * jnp.exp(-0.5 * v * v)
    return phi, pdf


def _sum_rows(v):
    return jnp.sum(v, axis=0, keepdims=True)


def _window_sums(ext, toward_later):
    n = ext.shape[0]

    def shifted(v, k):
        return pltpu.roll(v, (n - k) if toward_later else k, 0)

    s2 = ext + shifted(ext, 1)
    r4 = s2[:, HEAD:]
    s4 = r4 + shifted(r4, 2)
    r8 = s4[:, HEAD:]
    s8 = r8 + shifted(r8, 4)
    r16 = s8[:, HEAD:]
    s16 = r16 + shifted(r16, 8)
    return jnp.concatenate([s2[:, :HEAD], s4[:, :HEAD], s8[:, :HEAD], s16], axis=1)


def _window_counts(row0):
    t1 = row0 + 1 + lax.broadcasted_iota(jnp.int32, (ROWS, D_POOL), 0)
    lane = lax.broadcasted_iota(jnp.int32, (ROWS, D_POOL), 1)
    width = jnp.where(lane < HEAD, 2, jnp.where(lane < 2 * HEAD, 4, jnp.where(lane < 3 * HEAD, 8, 16)))
    return jnp.minimum(t1, width).astype(F32)


def _causal_mask():
    r = lax.broadcasted_iota(jnp.int32, (HEAD, HEAD), 0)
    s = lax.broadcasted_iota(jnp.int32, (HEAD, HEAD), 1)
    return r >= s


def _chunks_to_lanes(v):
    return jnp.concatenate([v[n * HEAD:(n + 1) * HEAD] for n in range(ROWS // HEAD)], axis=1)


def _lanes_to_chunks(v):
    return jnp.concatenate([v[:, n * HEAD:(n + 1) * HEAD] for n in range(ROWS // HEAD)], axis=0)


def _mixer(proj, halo, row0, wpool_ref, pscale, sgu_g_ref, sgu_b_ref, wsgu_ref, bsgu_t_ref):
    xa = proj[:, 0:512]
    ga = proj[:, 512:1024]
    u = proj[:, 1024:1536]
    v = proj[:, 1536:2048]
    gb = proj[:, 2048:2560]
    ext = jnp.concatenate([halo, xa], axis=0)
    win = _window_sums(ext, toward_later=False)[HALO:]
    cnt = _window_counts(row0)
    pooled = (win / cnt - xa).astype(BF16)
    pw = jnp.concatenate(
        [_dot(pooled[:, g * HEAD:(g + 1) * HEAD], wpool_ref[g].astype(BF16)) for g in range(N_HEAD)], axis=1)
    sig_a = _sigmoid(ga)
    ya = pw * pscale * (ga * sig_a)
    phi_u, pdf_u = _gelu_parts(u)
    phi_v, pdf_v = _gelu_parts(v)
    gu = u * phi_u
    gv = v * phi_v
    sig_b = _sigmoid(gb)
    silu_b = gb * sig_b
    mask = _causal_mask()
    vhat, rstd_v, vln_l, mixed = [], [], [], []
    for h in range(N_HEAD):
        vh, rh = _layer_norm(gv[:, h * HEAD:(h + 1) * HEAD])
        ln = (vh * sgu_g_ref[h:h + 1, :] + sgu_b_ref[h:h + 1, :]).astype(BF16)
        ln_l = _chunks_to_lanes(ln)
        wm = jnp.where(mask, wsgu_ref[h], 0.0).astype(BF16)
        mx = _lanes_to_chunks(_dot(wm, ln_l) + bsgu_t_ref[:, h:h + 1])
        vhat.append(vh)
        rstd_v.append(rh)
        vln_l.append(ln_l)
        mixed.append(mx)
    mixed = jnp.concatenate(mixed, axis=1)
    yb = gu * mixed * silu_b
    return dict(xa=xa, ga=ga, u=u, gb=gb, cnt=cnt, pooled=pooled, pw=pw, sig_a=sig_a, ya=ya, phi_u=phi_u, pdf_u=pdf_u,
                phi_v=phi_v, pdf_v=pdf_v, gu=gu, sig_b=sig_b, silu_b=silu_b, vhat=vhat, rstd_v=rstd_v, vln_l=vln_l,
                mixed=mixed, yb=yb, mask=mask)


def _const(shape, *index):
    lead = tuple(index) + (0,) * (len(shape) - len(index))
    return pl.BlockSpec(shape, lambda *_: lead)


def _const_in(shape, *index):
    lead = tuple(index) + (0,) * (len(shape) - len(index))
    return pl.BlockSpec(shape, lambda *_: lead, pipeline_mode=pl.Buffered(1))


def _layer_weight_specs(l):
    return [
        _const_in((None, N_HEAD, HEAD, HEAD), l),
        _const_in((DEPTH, D_POOL)),
        _const_in((None, N_HEAD, HEAD), l),
        _const_in((None, N_HEAD, HEAD), l),
        _const_in((None, N_HEAD, HEAD, HEAD), l),
        _const_in((None, HEAD, N_HEAD), l),
    ]


def _forward_layer(l, x, mod, w_in, w_out, small, ln_g, ln_b, target=None, next_weights=None):
    last = target is not None
    gathers = next_weights is not None

    def body(*refs):
        if last:
            (x_ref, mod_ref, win_ref, wout_ref, wpool_ref, pscale_ref, sgu_g_ref, sgu_b_ref, wsgu_ref, bsgu_t_ref,
             lng_ref, lnb_ref, tgt_ref, proj_ref, y_ref, out_ref, loss_ref, carry_ref) = refs
        elif gathers:
            (x_ref, mod_ref, win_ref, wout_ref, wpool_ref, pscale_ref, sgu_g_ref, sgu_b_ref, wsgu_ref, bsgu_t_ref,
             lng_ref, lnb_ref, _, _, proj_ref, y_ref, out_ref, next_in, next_out, carry_ref, g_send, g_recv) = refs
            gather = _WeightGather(_Place(), next_in, next_out, g_send, g_recv)
        else:
            (x_ref, mod_ref, win_ref, wout_ref, wpool_ref, pscale_ref, sgu_g_ref, sgu_b_ref, wsgu_ref, bsgu_t_ref,
             lng_ref, lnb_ref, proj_ref, y_ref, out_ref, carry_ref) = refs
        i = pl.program_id(0)

        @pl.when(i == 0)
        def _():
            carry_ref[...] = jnp.zeros_like(carry_ref)
            if last:
                loss_ref[...] = jnp.zeros_like(loss_ref)
            if gathers:
                gather.start_first_round()

        if gathers:
            @pl.when(i == GATHER_SECOND_ROUND_STEP)
            def _():
                gather.start_second_round()

        x = x_ref[...]
        shift, scale, gate = mod_ref[0:1, :], mod_ref[1:2, :], mod_ref[2:3, :]
        xn, _ = _layer_norm(x)
        h = xn * (1.0 + scale) + shift
        proj = _dot(h.astype(BF16), win_ref[...])
        proj_ref[...] = proj
        m = _mixer(proj, carry_ref[...], i * ROWS, wpool_ref, pscale_ref[l:l + 1, :], sgu_g_ref, sgu_b_ref, wsgu_ref,
                   bsgu_t_ref)
        carry_ref[...] = m["xa"][ROWS - HALO:]
        cat = jnp.concatenate([m["ya"], m["yb"]], axis=1).astype(BF16)
        y = _dot(cat, wout_ref[...])
        y_ref[...] = y
        zn, _ = _layer_norm(DEEPNORM_ALPHA * x + gate * y)
        out = zn * lng_ref[l:l + 1, :] + lnb_ref[l:l + 1, :]
        if last:
            err = out - tgt_ref[...]
            out_ref[...] = err * (1.0 / D_MODEL)
            loss_ref[...] += jnp.sum(err * err)
        else:
            out_ref[...] = out

        if gathers:
            @pl.when(i == N_TILE - 1)
            def _():
                gather.pass_second_round()
                gather.finish()

    tile = pl.BlockSpec((ROWS, D_MODEL), lambda i: (i, 0))
    tile3 = pl.BlockSpec((None, ROWS, D_MODEL), lambda i: (0, i, 0))
    in_specs = [tile3 if x.ndim == 3 else tile, _const_in((None, 8, D_MODEL), l), _const_in((D_MODEL, D_PROJ)),
                _const_in((D_MODEL, D_MODEL))]
    in_specs += _layer_weight_specs(l) + [_const_in((DEPTH, D_MODEL)), _const_in((DEPTH, D_MODEL))]
    out_shape = [jax.ShapeDtypeStruct((SEQ, D_PROJ), F32), jax.ShapeDtypeStruct((SEQ, D_MODEL), F32),
                 jax.ShapeDtypeStruct((SEQ, D_MODEL), F32)]
    out_specs = [pl.BlockSpec((ROWS, D_PROJ), lambda i: (i, 0)), tile, tile]
    args = [x, mod, w_in, w_out, *small, ln_g, ln_b]
    scratch = [pltpu.VMEM((HALO, D_POOL), F32)]
    aliases = {}
    if last:
        in_specs.append(tile3)
        args.append(target)
        out_shape.append(jax.ShapeDtypeStruct((8, HEAD), F32))
        out_specs.append(_const((8, HEAD)))
    if gathers:
        hbm = pl.BlockSpec(memory_space=pl.ANY)
        aliases = {len(args): len(out_shape), len(args) + 1: len(out_shape) + 1}
        in_specs += [hbm, hbm]
        args += list(next_weights)
        out_shape += [jax.ShapeDtypeStruct(w.shape, BF16) for w in next_weights]
        out_specs += [hbm, hbm]
        scratch += [pltpu.SemaphoreType.DMA((_WeightGather.N_SEMS,)), pltpu.SemaphoreType.DMA((_WeightGather.N_SEMS,))]
    return pl.pallas_call(
        body, name="fwd_last" if last else "fwd_first", grid=(N_TILE,), in_specs=in_specs, out_specs=out_specs,
        out_shape=out_shape, scratch_shapes=scratch, input_output_aliases=aliases,
        compiler_params=pltpu.CompilerParams(dimension_semantics=("arbitrary",), vmem_limit_bytes=VMEM_LIMIT),
    )(*args)


def _backward_layer(l, dout, x, y, proj, mod, w_in, w_out, small, ln_g, sq=None, shared=None):
    has_loss = sq is not None

    def body(*refs):
        (dout_ref, x_ref, y_ref, proj_ref, halo_ref, mod_ref, win_ref, wout_ref, wpool_ref, pscale_ref,
         sgu_g_ref, sgu_b_ref, wsgu_ref, bsgu_t_ref, lng_ref) = refs[:15]
        n_in = 16 if has_loss else 15 + 6
        dx_ref, h_ref, cat_ref, dy_ref, dproj_ref, pack_ref, dmod_ref, carry_ref = refs[n_in:n_in + 8]
        i = pl.program_id(0)
        tile = N_TILE - 1 - i

        @pl.when(i == 0)
        def _():
            carry_ref[...] = jnp.zeros_like(carry_ref)
            pack_ref[...] = jnp.zeros_like(pack_ref)
            dmod_ref[...] = jnp.zeros_like(dmod_ref)
            if has_loss:
                pack_ref[PK_LOSS:PK_LOSS + 8, :] = refs[15][...]

        x = x_ref[...]
        y = y_ref[...]
        dout = dout_ref[...]
        pscale = pscale_ref[l:l + 1, :]
        shift, scale, gate = mod_ref[0:1, :], mod_ref[1:2, :], mod_ref[2:3, :]
        xn, rstd_x = _layer_norm(x)
        h = xn * (1.0 + scale) + shift
        h_ref[...] = h.astype(BF16)
        zn, rstd_z = _layer_norm(DEEPNORM_ALPHA * x + gate * y)
        g_ln_g = _sum_rows(dout * zn)
        g_ln_b = _sum_rows(dout)
        dz = _layer_norm_bwd(dout * lng_ref[l:l + 1, :], zn, rstd_z)
        d_gate = _sum_rows(dz * y)
        dy = (gate * dz).astype(BF16)
        dy_ref[...] = dy

        halo = jnp.where(tile > 0, halo_ref[...], 0.0)
        m = _mixer(proj_ref[...], halo, tile * ROWS, wpool_ref, pscale, sgu_g_ref, sgu_b_ref, wsgu_ref, bsgu_t_ref)
        cat_ref[...] = jnp.concatenate([m["ya"], m["yb"]], axis=1).astype(BF16)
        dcat = _dot(dy, wout_ref[...], NT)
        dya = dcat[:, :D_POOL]
        dyb = dcat[:, D_POOL:]

        ga, sig_a = m["ga"], m["sig_a"]
        dp = dya * (ga * sig_a)
        d_ga = dya * (m["pw"] * pscale) * (sig_a * (1.0 + ga * (1.0 - sig_a)))
        g_pscale = _sum_rows(dp * m["pw"])
        dpw = (dp * pscale).astype(BF16)
        dpooled = []
        for g in range(N_HEAD):
            cols = slice(g * HEAD, (g + 1) * HEAD)
            pack_ref[PK_W_POOL + g * HEAD:PK_W_POOL + (g + 1) * HEAD, :] += _dot(m["pooled"][:, cols], dpw[:, cols], TN)
            dpooled.append(_dot(dpw[:, cols], wpool_ref[g].astype(BF16), NT))
        dpooled = jnp.concatenate(dpooled, axis=1)
        q = dpooled / m["cnt"]
        ext = jnp.concatenate([q, carry_ref[...]], axis=0)
        d_xa = _window_sums(ext, toward_later=True)[:ROWS] - dpooled
        carry_ref[...] = q[:HALO]

        gu, mixed, silu_b, gb, sig_b = m["gu"], m["mixed"], m["silu_b"], m["gb"], m["sig_b"]
        d_mixed = dyb * gu * silu_b
        d_gu = dyb * mixed * silu_b
        d_gb = dyb * gu * mixed * (sig_b * (1.0 + gb * (1.0 - sig_b)))
        d_u = d_gu * (m["phi_u"] + m["u"] * m["pdf_u"])
        ones = jnp.ones((8, HEAD), F32)
        d_v = []
        for hd in range(N_HEAD):
            cols = slice(hd * HEAD, (hd + 1) * HEAD)
            dm = d_mixed[:, cols]
            dm_l = _chunks_to_lanes(dm.astype(BF16))
            g_w = _dot(dm_l, m["vln_l"][hd], NT)
            pack_ref[PK_W_SGU + hd * HEAD:PK_W_SGU + (hd + 1) * HEAD, :] += jnp.where(m["mask"], g_w, 0.0)
            dm_sum = dm[0:HEAD]
            for n in range(1, ROWS // HEAD):
                dm_sum = dm_sum + dm[n * HEAD:(n + 1) * HEAD]
            pack_ref[PK_B_SGU + hd:PK_B_SGU + hd + 1, :] += _dot_exact(ones, dm_sum, NT)[0:1]
            wm = jnp.where(m["mask"], wsgu_ref[hd], 0.0).astype(BF16)
            d_vln = _lanes_to_chunks(_dot(wm, dm_l, TN))
            vhat = m["vhat"][hd]
            pack_ref[PK_SGU_LN_G + hd:PK_SGU_LN_G + hd + 1, :] += _sum_rows(d_vln * vhat)
            pack_ref[PK_SGU_LN_B + hd:PK_SGU_LN_B + hd + 1, :] += _sum_rows(d_vln)
            d_v.append(_layer_norm_bwd(d_vln * sgu_g_ref[hd:hd + 1, :], vhat, m["rstd_v"][hd]))
        v = proj_ref[:, 1536:2048]
        d_v = jnp.concatenate(d_v, axis=1) * (m["phi_v"] + v * m["pdf_v"])

        dproj = jnp.concatenate([d_xa, d_ga, d_u, d_v, d_gb], axis=1).astype(BF16)
        dproj_ref[...] = dproj
        dh = _dot(dproj, win_ref[...], NT)
        d_scale = _sum_rows(dh * xn)
        d_shift = _sum_rows(dh)
        dx_ref[...] = DEEPNORM_ALPHA * dz + _layer_norm_bwd(dh * (1.0 + scale), xn, rstd_x)

        dmod_ref[0:1, :] += d_shift
        dmod_ref[1:2, :] += d_scale
        dmod_ref[2:3, :] += d_gate
        for g in range(N_HEAD):
            pack_ref[PK_POOL_SCALE + g:PK_POOL_SCALE + g + 1, :] += g_pscale[:, g * HEAD:(g + 1) * HEAD]
        for k in range(D_MODEL // HEAD):
            pack_ref[PK_LN_G + k:PK_LN_G + k + 1, :] += g_ln_g[:, k * HEAD:(k + 1) * HEAD]
            pack_ref[PK_LN_B + k:PK_LN_B + k + 1, :] += g_ln_b[:, k * HEAD:(k + 1) * HEAD]

    def rev(i):
        return (N_TILE - 1 - i, 0)

    tile = pl.BlockSpec((ROWS, D_MODEL), rev)
    tile3 = pl.BlockSpec((None, ROWS, D_MODEL), lambda i: (0, N_TILE - 1 - i, 0))
    halo = pl.BlockSpec((HALO, D_POOL), lambda i: (jnp.maximum((N_TILE - 1 - i) * (ROWS // HALO) - 1, 0), 0))
    in_specs = [tile, tile3 if x.ndim == 3 else tile, tile, pl.BlockSpec((ROWS, D_PROJ), rev), halo,
                _const_in((None, 8, D_MODEL), l), _const_in((D_MODEL, D_PROJ)), _const_in((D_MODEL, D_MODEL))]
    in_specs += _layer_weight_specs(l) + [_const_in((DEPTH, D_MODEL))]
    args = [dout, x, y, proj, proj, mod, w_in, w_out, *small, ln_g]
    stacked = lambda cols: pl.BlockSpec((None, ROWS, cols), lambda i: (l, N_TILE - 1 - i, 0))
    out_shape = [jax.ShapeDtypeStruct((SEQ, D_MODEL), F32), jax.ShapeDtypeStruct((DEPTH, SEQ, D_MODEL), BF16),
                 jax.ShapeDtypeStruct((DEPTH, SEQ, D_MODEL), BF16), jax.ShapeDtypeStruct((DEPTH, SEQ, D_MODEL), BF16),
                 jax.ShapeDtypeStruct((DEPTH, SEQ, D_PROJ), BF16), jax.ShapeDtypeStruct((DEPTH, PK_ROWS, HEAD), F32),
                 jax.ShapeDtypeStruct((DEPTH, 8, D_MODEL), F32)]
    out_specs = [tile, stacked(D_MODEL), stacked(D_MODEL), stacked(D_MODEL), stacked(D_PROJ),
                 _const((None, PK_ROWS, HEAD), l), _const((None, 8, D_MODEL), l)]
    aliases = {}
    if has_loss:
        in_specs.append(_const_in((8, HEAD)))
        args.append(sq)
    else:
        aliases = {len(args) + k: 1 + k for k in range(len(shared))}
        in_specs += [pl.BlockSpec(memory_space=pl.ANY)] * len(shared)
        args += list(shared)
    return pl.pallas_call(
        body, name="bwd_last" if has_loss else "bwd_first", grid=(N_TILE,), in_specs=in_specs, out_specs=out_specs,
        out_shape=out_shape, scratch_shapes=[pltpu.VMEM((HALO, D_POOL), F32)], input_output_aliases=aliases,
        compiler_params=pltpu.CompilerParams(dimension_semantics=("arbitrary",), vmem_limit_bytes=VMEM_LIMIT),
    )(*args)


def _flip(v, f):
    return v + f - 2 * v * f


class _Place:
    def __init__(self):
        x, y, c = lax.axis_index("x"), lax.axis_index("y"), lax.axis_index("c")
        self.x, self.y, self.c = x, y, c
        self.chip = 2 * x + y
        self.dev = 4 * x + 2 * y + c
        self.sibling = (x, y, 1 - c)
        x1, y1 = _flip(x, 1 - c), _flip(y, c)
        x2, y2 = _flip(x, c), _flip(y, 1 - c)
        self.first = (x1, y1, c)
        self.second = (x2, y2, c)
        self.chip_first = 2 * x1 + y1
        self.chip_second = 2 * x2 + y2
        self.chip_far = 2 * (1 - x) + (1 - y)
        self.my_first_coord = jnp.where(c == 0, x, y)

    def first_coord(self, ch):
        return jnp.where(self.c == 0, ch // 2, ch % 2)

    def others(self):
        return [(_flip(self.x, (r >> 2) & 1), _flip(self.y, (r >> 1) & 1), _flip(self.c, r & 1)) for r in range(1, N_DEV)]

    def other_chips(self):
        return [(1 - self.x, self.y), (self.x, 1 - self.y), (1 - self.x, 1 - self.y)]


class _WeightGather:
    N_SEMS = 12

    def __init__(self, place, win, wout, send, recv):
        self.p, self.win, self.wout, self.send, self.recv = place, win, wout, send, recv
        p = place
        self.plan = [(p.chip, p.first), (p.chip, p.second), (p.chip_first, p.second),
                     (p.chip_first, p.sibling), (p.chip_second, p.sibling), (p.chip_far, p.sibling)]

    def _copies(self, k):
        ch, target = self.plan[k]
        rows_in = pl.ds(pl.multiple_of(self.p.c * HALF_IN, HALF_IN), HALF_IN)
        cols_in = pl.ds(pl.multiple_of(ch * W_IN_COLS, 128), W_IN_COLS)
        rows_out = pl.ds(pl.multiple_of(ch * W_OUT_ROWS + self.p.c * HALF_OUT, HALF_OUT), HALF_OUT)
        r_in = self.win.at[rows_in, cols_in]
        r_out = self.wout.at[rows_out, :]
        return [pltpu.make_async_remote_copy(r_in, r_in, self.send.at[2 * k], self.recv.at[2 * k],
                                             device_id=target, device_id_type=MESH),
                pltpu.make_async_remote_copy(r_out, r_out, self.send.at[2 * k + 1], self.recv.at[2 * k + 1],
                                             device_id=target, device_id_type=MESH)]

    def _start(self, k):
        for cp in self._copies(k):
            cp.start()

    def _landed(self, k):
        for cp in self._copies(k):
            cp.wait_recv()

    def start_first_round(self):
        self._start(0)

    def start_second_round(self):
        self._landed(0)
        self._start(1)
        self._start(2)
        self._start(3)

    def pass_second_round(self):
        self._landed(1)
        self._start(4)
        self._landed(2)
        self._start(5)

    def finish(self):
        for k in (3, 4, 5):
            self._landed(k)
        for k in range(len(self.plan)):
            for cp in self._copies(k):
                cp.wait_send()


def _prepare(c_vec, w_ada, b_ada, w_in, w_out):
    def body(c_ref, wada_ref, bada_ref, win_ref, wout_ref,
             win0, win1, wout0, wout1, mod_ref, c_all,
             win_bf, wout_bf, mod_mine, mod_all, g_send, g_recv, c_send, c_recv, mod_send, mod_recv, local_sem):
        p = _Place()
        win_bf[...] = win_ref[...].astype(BF16)
        wout_bf[...] = wout_ref[...].astype(BF16)
        cols = pl.ds(pl.multiple_of(p.chip * W_IN_COLS, 128), W_IN_COLS)
        rows = pl.ds(pl.multiple_of(p.chip * W_OUT_ROWS, W_OUT_ROWS), W_OUT_ROWS)
        own = [pltpu.make_async_copy(win_bf.at[0], win0.at[:, cols], local_sem.at[0]),
               pltpu.make_async_copy(wout_bf.at[0], wout0.at[rows, :], local_sem.at[1]),
               pltpu.make_async_copy(win_bf.at[1], win1.at[:, cols], local_sem.at[2]),
               pltpu.make_async_copy(wout_bf.at[1], wout1.at[rows, :], local_sem.at[3])]
        for cp in own:
            cp.start()

        c_all[pl.ds(p.dev, 1), :] = c_ref[...]
        c_copies = [pltpu.make_async_remote_copy(c_ref, c_all.at[pl.ds(p.dev, 1), :], c_send.at[r], c_recv.at[r],
                                                 device_id=d, device_id_type=MESH) for r, d in enumerate(p.others())]
        for cp in c_copies:
            cp.start()
        own[0].wait()
        own[1].wait()
        gather = _WeightGather(p, win0, wout0, g_send, g_recv)
        gather.start_first_round()
        for cp in c_copies:
            cp.wait()

        cv = c_all[...]
        silu_c = (cv * _sigmoid(cv)).astype(BF16)
        for l in range(DEPTH):
            mod_mine[l] = _dot(silu_c, wada_ref[l].astype(BF16))
        mod_all[p.chip] = mod_mine[...]
        m_copies = [pltpu.make_async_remote_copy(mod_mine, mod_all.at[p.chip], mod_send.at[k], mod_recv.at[k],
                                                 device_id=(px, py, p.c), device_id_type=MESH)
                    for k, (px, py) in enumerate(p.other_chips())]
        for cp in m_copies:
            cp.start()
        gather.start_second_round()
        for cp in m_copies:
            cp.wait()
        mod_ref[...] = jnp.zeros_like(mod_ref)
        for l in range(DEPTH):
            full = jnp.concatenate([mod_all[ch, l, pl.ds(p.dev, 1), :] for ch in range(N_CHIP)], axis=1) + bada_ref[l:l + 1, :]
            for k in range(3):
                mod_ref[l, k:k + 1, :] = full[:, k * D_MODEL:(k + 1) * D_MODEL]
        gather.pass_second_round()
        gather.finish()
        own[2].wait()
        own[3].wait()

    vmem = pl.BlockSpec(memory_space=pltpu.VMEM)
    hbm = pl.BlockSpec(memory_space=pl.ANY)
    w_in_shape = jax.ShapeDtypeStruct((D_MODEL, D_PROJ), BF16)
    w_out_shape = jax.ShapeDtypeStruct((D_MODEL, D_MODEL), BF16)
    return pl.pallas_call(
        body, name="prepare",
        in_specs=[vmem, vmem, vmem, vmem, vmem],
        out_specs=[hbm, hbm, hbm, hbm, vmem, vmem],
        out_shape=[w_in_shape, w_in_shape, w_out_shape, w_out_shape,
                   jax.ShapeDtypeStruct((DEPTH, 8, D_MODEL), F32), jax.ShapeDtypeStruct((N_DEV, D_MODEL), F32)],
        scratch_shapes=[
            pltpu.VMEM((DEPTH, D_MODEL, W_IN_COLS), BF16), pltpu.VMEM((DEPTH, W_OUT_ROWS, D_MODEL), BF16),
            pltpu.VMEM((DEPTH, N_DEV, W_ADA_COLS), F32), pltpu.VMEM((N_CHIP, DEPTH, N_DEV, W_ADA_COLS), F32),
            pltpu.SemaphoreType.DMA((_WeightGather.N_SEMS,)), pltpu.SemaphoreType.DMA((_WeightGather.N_SEMS,)),
            pltpu.SemaphoreType.DMA((7,)), pltpu.SemaphoreType.DMA((7,)),
            pltpu.SemaphoreType.DMA((3,)), pltpu.SemaphoreType.DMA((3,)),
            pltpu.SemaphoreType.DMA((4,)),
        ],
        compiler_params=pltpu.CompilerParams(vmem_limit_bytes=VMEM_LIMIT),
    )(c_vec, w_ada, b_ada, w_in, w_out)


IN_STEPS = W_IN_COLS // HEAD
OUT_STEPS = 4
OUT_COLS = D_MODEL // OUT_STEPS
OUT_FIRST = 2
ITEMS = ([("out", k) for k in range(OUT_FIRST)] + [("in", k) for k in range(IN_STEPS)]
         + [("out", k) for k in range(OUT_FIRST, OUT_STEPS)])
N_ITEMS = len(ITEMS)
N_STEPS = DEPTH * N_ITEMS
DELAY_SUM, DELAY_SECOND, DELAY_FINAL = 1, 3, 5
SMALL_SCATTER_STEP, SMALL_GATHER_STEP, SMALL_PASS_STEP, SMALL_FINISH_STEP = 1, 3, 5, 7


def _wgrad_reduce(h, dproj, cat, dy, pack, dmod):
    def body(*refs):
        h_ref, dp_refs, cat_ref, dy_ref, pack_ref, dmod_ref = refs[0], refs[1:5], refs[5], refs[6], refs[7], refs[8]
        fin_in, fin_out, pack_out, dmod_out = refs[9:13]
        scratch = refs[13:]
        (mine_in, send_in, sib_in, st_in, r1_in, r2_in, f_in,
         mine_out, send_out, sib_out, st_out, r1_out, r2_out, f_out,
         d2d_s, d2d_r, r1_s, r1_r, r2_s, r2_r, fin_l, fin_s, fin_r) = scratch[:23]
        p = _Place()
        c = p.c
        i = pl.program_id(0)
        my_rows = pl.ds(pl.multiple_of(c * HALF_IN, HALF_IN), HALF_IN)

        def layer_of(j):
            return DEPTH - 1 - j // N_ITEMS

        def bufs(j):
            kind, k = ITEMS[j % N_ITEMS]
            if kind == "in":
                return [r.at[k] for r in (mine_in, send_in, sib_in, st_in, r1_in, r2_in, f_in)]
            return [r.at[k] for r in (mine_out, send_out, sib_out, st_out, r1_out, r2_out, f_out)]

        def piece(j, ref, ch):
            if ITEMS[j % N_ITEMS][0] == "in":
                return ref.at[:, ch * HEAD:(ch + 1) * HEAD]
            return ref.at[ch]

        def slot(ch):
            return jnp.where(c == 0, ch % 2, ch // 2)

        def to_sibling(j):
            _, send, sib, _, _, _, _ = bufs(j)
            return pltpu.make_async_remote_copy(send, sib, d2d_s.at[j], d2d_r.at[j], device_id=p.sibling, device_id_type=MESH)

        def first_round(j, ch):
            _, _, _, st, r1, _, _ = bufs(j)
            k = slot(ch)
            return pltpu.make_async_remote_copy(st.at[k], r1.at[k], r1_s.at[2 * j + k], r1_r.at[2 * j + k],
                                                device_id=p.first, device_id_type=MESH)

        def second_round(j):
            _, _, _, st, _, r2, _ = bufs(j)
            return pltpu.make_async_remote_copy(st.at[2], r2, r2_s.at[j], r2_r.at[j], device_id=p.second, device_id_type=MESH)

        def finals(j):
            f = bufs(j)[6]
            kind, k = ITEMS[j % N_ITEMS]
            if kind == "in":
                dst = fin_in.at[layer_of(j), my_rows, k * HEAD:(k + 1) * HEAD]
            else:
                dst = fin_out.at[layer_of(j), c, :, k * OUT_COLS:(k + 1) * OUT_COLS]
            return [pltpu.make_async_copy(f, dst, fin_l.at[j]),
                    pltpu.make_async_remote_copy(f, dst, fin_s.at[j], fin_r.at[j], device_id=p.sibling, device_id_type=MESH)]

        def stage_sum(j):
            mine, _, sib, st, _, _, _ = bufs(j)
            to_sibling(j).wait_recv()
            mine[...] = mine[...] + sib[...]
            for ch in range(N_CHIP):
                @pl.when(p.first_coord(ch) != p.my_first_coord)
                def _(ch=ch):
                    st[slot(ch)] = piece(j, mine, ch)[...].astype(BF16)
                    first_round(j, ch).start()

        def stage_second(j):
            mine, _, _, st, r1, _, _ = bufs(j)
            for ch in range(N_CHIP):
                @pl.when(p.first_coord(ch) == p.my_first_coord)
                def _(ch=ch):
                    first_round(j, ch).wait_recv()
                    part = piece(j, mine, ch)
                    total = part[...] + r1[slot(ch)].astype(F32)
                    part[...] = total

                    @pl.when(ch != p.chip)
                    def _():
                        st[2] = total.astype(BF16)
                        second_round(j).start()

        def stage_final(j):
            mine, _, _, _, _, r2, f = bufs(j)
            second_round(j).wait_recv()
            for ch in range(N_CHIP):
                @pl.when(ch == p.chip)
                def _(ch=ch):
                    f[...] = piece(j, mine, ch)[...] + r2[...].astype(F32)
            for cp in finals(j):
                cp.start()

        def drain(j):
            to_sibling(j).wait_send()
            for ch in range(N_CHIP):
                @pl.when(p.first_coord(ch) != p.my_first_coord)
                def _(ch=ch):
                    first_round(j, ch).wait_send()

                @pl.when(jnp.logical_and(p.first_coord(ch) == p.my_first_coord, ch != p.chip))
                def _():
                    second_round(j).wait_send()
            for cp in finals(j):
                cp.wait()

        dev = p.dev
        devices = p.others()

        def half(core):
            return pl.ds(pl.multiple_of(core * PK_HALF, 8), PK_HALF)

        def finished(core, ch):
            return pl.ds(pl.multiple_of(core * PK_HALF + ch * PK_PIECE, 8), PK_PIECE)

        def small_exchange(l, first_step, bufs_l):
            (pk_mine, pk_sib, pk_rs, pk_fin, pk_all, dm_st, dm_all, pk_sem, rs_s, rs_r, ag_s, ag_r, dm_s, dm_r) = bufs_l

            def pk_load():
                return pltpu.make_async_copy(pack_ref.at[l, half(c)], pk_mine, pk_sem.at[0])

            def pk_give():
                return pltpu.make_async_remote_copy(pack_ref.at[l, half(1 - c)], pk_sib, pk_sem.at[1], pk_sem.at[2],
                                                    device_id=p.sibling, device_id_type=MESH)

            def pk_scatter(ch):
                return pltpu.make_async_remote_copy(pk_mine.at[ch * PK_PIECE:(ch + 1) * PK_PIECE], pk_rs.at[p.chip],
                                                    rs_s.at[ch], rs_r.at[p.chip], device_id=(ch // 2, ch % 2, c),
                                                    device_id_type=MESH)

            def pk_spread(ch):
                return pltpu.make_async_remote_copy(pk_fin, pk_all.at[finished(c, p.chip)], ag_s.at[ch], ag_r.at[p.chip],
                                                    device_id=(ch // 2, ch % 2, c), device_id_type=MESH)

            def pk_pass():
                return pltpu.make_async_remote_copy(pk_all.at[half(c)], pk_all.at[half(c)], pk_sem.at[3], pk_sem.at[4],
                                                    device_id=p.sibling, device_id_type=MESH)

            def dm_copy(r):
                return pltpu.make_async_remote_copy(dm_st, dm_all.at[:, pl.ds(dev, 1), :], dm_s.at[r], dm_r.at[r],
                                                    device_id=devices[r], device_id_type=MESH)

            def results():
                return [pltpu.make_async_copy(pk_all, pack_out.at[l], pk_sem.at[0]),
                        pltpu.make_async_copy(dm_all, dmod_out.at[l], pk_sem.at[5])]

            @pl.when(i == first_step)
            def _():
                pk_load().start()
                pk_give().start()
                for k in range(3):
                    for r in range(D_MODEL // HEAD):
                        dm_st[8 * k + r] = dmod_ref[l, k:k + 1, r * HEAD:(r + 1) * HEAD]
                dm_all[:, pl.ds(dev, 1), :] = dm_st[...]
                for r in range(N_DEV - 1):
                    dm_copy(r).start()

            @pl.when(i == first_step + SMALL_SCATTER_STEP)
            def _():
                pk_load().wait()
                pk_give().wait()
                pk_mine[...] = pk_mine[...] + pk_sib[...]
                for ch in range(N_CHIP):
                    @pl.when(ch != p.chip)
                    def _(ch=ch):
                        pk_scatter(ch).start()

            @pl.when(i == first_step + SMALL_GATHER_STEP)
            def _():
                for ch in range(N_CHIP):
                    @pl.when(ch != p.chip)
                    def _(ch=ch):
                        pltpu.make_async_remote_copy(pk_fin, pk_rs.at[ch], rs_s.at[ch], rs_r.at[ch],
                                                     device_id=p.sibling, device_id_type=MESH).wait_recv()
                for me in range(N_CHIP):
                    @pl.when(me == p.chip)
                    def _(me=me):
                        total = None
                        for ch in range(N_CHIP):
                            part = pk_mine[me * PK_PIECE:(me + 1) * PK_PIECE] if ch == me else pk_rs[ch]
                            total = part if total is None else total + part
                        pk_fin[...] = total
                        pk_all[finished(c, me)] = total
                for ch in range(N_CHIP):
                    @pl.when(ch != p.chip)
                    def _(ch=ch):
                        pk_spread(ch).start()

            @pl.when(i == first_step + SMALL_PASS_STEP)
            def _():
                for ch in range(N_CHIP):
                    @pl.when(ch != p.chip)
                    def _(ch=ch):
                        pltpu.make_async_remote_copy(pk_fin, pk_all.at[finished(c, ch)], ag_s.at[ch], ag_r.at[ch],
                                                     device_id=p.sibling, device_id_type=MESH).wait_recv()
                pk_pass().start()

            @pl.when(i == first_step + SMALL_FINISH_STEP)
            def _():
                pk_pass().wait()
                for ch in range(N_CHIP):
                    @pl.when(ch != p.chip)
                    def _(ch=ch):
                        pk_scatter(ch).wait_send()
                        pk_spread(ch).wait_send()
                for r in range(N_DEV - 1):
                    dm_copy(r).wait()
                for cp in results():
                    cp.start()
                for cp in results():
                    cp.wait()

        n_small = 14
        for l in range(DEPTH):
            small_exchange(l, (DEPTH - 1 - l) * N_ITEMS, scratch[23 + n_small * l:23 + n_small * (l + 1)])

        for step in range(N_ITEMS, N_STEPS):
            @pl.when(i == step)
            def _(step=step):
                drain(step - N_ITEMS)

        ii = jnp.where(i < N_ITEMS, i, i - N_ITEMS)
        in_step = jnp.logical_and(ii >= OUT_FIRST, ii < OUT_FIRST + IN_STEPS)

        @pl.when(in_step)
        def _():
            k = ii - OUT_FIRST
            rhs = jnp.concatenate([r[...] for r in dp_refs], axis=1)
            res = _dot(h_ref[...], rhs, TN)

            @pl.when(c == 0)
            def _():
                mine_in[k] = res[:HALF_IN]
                send_in[k] = res[HALF_IN:]

            @pl.when(c == 1)
            def _():
                mine_in[k] = res[HALF_IN:]
                send_in[k] = res[:HALF_IN]

        @pl.when(jnp.logical_not(in_step))
        def _():
            k = jnp.where(ii < OUT_FIRST, ii, ii - IN_STEPS)
            res = _dot(cat_ref[...], dy_ref[...], TN)

            @pl.when(c == 0)
            def _():
                for ch in range(N_CHIP):
                    mine_out[k, ch] = res[ch * W_OUT_ROWS:ch * W_OUT_ROWS + HALF_OUT]
                    send_out[k, ch] = res[ch * W_OUT_ROWS + HALF_OUT:(ch + 1) * W_OUT_ROWS]

            @pl.when(c == 1)
            def _():
                for ch in range(N_CHIP):
                    mine_out[k, ch] = res[ch * W_OUT_ROWS + HALF_OUT:(ch + 1) * W_OUT_ROWS]
                    send_out[k, ch] = res[ch * W_OUT_ROWS:ch * W_OUT_ROWS + HALF_OUT]

        stages = ((0, lambda j: to_sibling(j).start()), (DELAY_SUM, stage_sum), (DELAY_SECOND, stage_second),
                  (DELAY_FINAL, stage_final))
        for step in range(N_STEPS):
            @pl.when(i == step)
            def _(step=step):
                for delay, stage in stages:
                    if step - delay >= 0:
                        stage(step - delay)

        @pl.when(i == N_STEPS - 1)
        def _():
            for step in range(N_STEPS, N_STEPS + DELAY_FINAL):
                for delay, stage in stages:
                    if 0 <= step - delay < N_STEPS:
                        stage(step - delay)
            for j in range(N_STEPS - N_ITEMS, N_STEPS):
                drain(j)

    hbm = pl.BlockSpec(memory_space=pl.ANY)

    def layer(i):
        return jnp.where(i < N_ITEMS, DEPTH - 1, 0)

    def item(i):
        return jnp.where(i < N_ITEMS, i, i - N_ITEMS)

    def whole(i):
        return (layer(i), 0, 0)

    def dproj_piece(ch):
        return pl.BlockSpec((None, SEQ, HEAD),
                            lambda i: (layer(i), 0, ch * IN_STEPS + jnp.clip(item(i) - OUT_FIRST, 0, IN_STEPS - 1)))

    def dy_quarter(i):
        return (layer(i), 0, jnp.where(item(i) < OUT_FIRST, item(i), jnp.maximum(item(i) - IN_STEPS, OUT_FIRST)))

    operand = pl.BlockSpec((None, SEQ, D_MODEL), whole, pipeline_mode=pl.Buffered(1))
    in_specs = [operand] + [dproj_piece(ch) for ch in range(N_CHIP)]
    in_specs += [operand, pl.BlockSpec((None, SEQ, OUT_COLS), dy_quarter), hbm, _const_in((DEPTH, 8, D_MODEL))]
    args = [h, dproj, dproj, dproj, dproj, cat, dy, pack, dmod]
    out_shape = [jax.ShapeDtypeStruct((DEPTH, D_MODEL, W_IN_COLS), F32), jax.ShapeDtypeStruct((DEPTH, 2, HALF_OUT, D_MODEL), F32),
                 jax.ShapeDtypeStruct((DEPTH, PK_ROWS, HEAD), F32), jax.ShapeDtypeStruct((DEPTH, 24, N_DEV, HEAD), F32)]
    out_specs = [hbm, hbm, hbm, hbm]
    in_item = lambda *lead: pltpu.VMEM(lead + (HALF_IN, HEAD), BF16)
    out_item = lambda *lead: pltpu.VMEM(lead + (HALF_OUT, OUT_COLS), BF16)
    scratch = [
        pltpu.VMEM((IN_STEPS, HALF_IN, N_CHIP * HEAD), F32), pltpu.VMEM((IN_STEPS, HALF_IN, N_CHIP * HEAD), F32),
        pltpu.VMEM((IN_STEPS, HALF_IN, N_CHIP * HEAD), F32), in_item(IN_STEPS, 3), in_item(IN_STEPS, 2), in_item(IN_STEPS),
        pltpu.VMEM((IN_STEPS, HALF_IN, HEAD), F32),
        pltpu.VMEM((OUT_STEPS, N_CHIP, HALF_OUT, OUT_COLS), F32), pltpu.VMEM((OUT_STEPS, N_CHIP, HALF_OUT, OUT_COLS), F32),
        pltpu.VMEM((OUT_STEPS, N_CHIP, HALF_OUT, OUT_COLS), F32), out_item(OUT_STEPS, 3), out_item(OUT_STEPS, 2),
        out_item(OUT_STEPS), pltpu.VMEM((OUT_STEPS, HALF_OUT, OUT_COLS), F32),
        pltpu.SemaphoreType.DMA((N_STEPS,)), pltpu.SemaphoreType.DMA((N_STEPS,)),
        pltpu.SemaphoreType.DMA((2 * N_STEPS,)), pltpu.SemaphoreType.DMA((2 * N_STEPS,)),
        pltpu.SemaphoreType.DMA((N_STEPS,)), pltpu.SemaphoreType.DMA((N_STEPS,)),
        pltpu.SemaphoreType.DMA((N_STEPS,)), pltpu.SemaphoreType.DMA((N_STEPS,)), pltpu.SemaphoreType.DMA((N_STEPS,)),
    ]
    for _ in range(DEPTH):
        scratch += [
            pltpu.VMEM((PK_HALF, HEAD), F32), pltpu.VMEM((PK_HALF, HEAD), F32), pltpu.VMEM((N_CHIP, PK_PIECE, HEAD), F32),
            pltpu.VMEM((PK_PIECE, HEAD), F32), pltpu.VMEM((PK_ROWS, HEAD), F32),
            pltpu.VMEM((24, 1, HEAD), F32), pltpu.VMEM((24, N_DEV, HEAD), F32),
            pltpu.SemaphoreType.DMA((6,)),
            pltpu.SemaphoreType.DMA((N_CHIP,)), pltpu.SemaphoreType.DMA((N_CHIP,)),
            pltpu.SemaphoreType.DMA((N_CHIP,)), pltpu.SemaphoreType.DMA((N_CHIP,)),
            pltpu.SemaphoreType.DMA((N_DEV - 1,)), pltpu.SemaphoreType.DMA((N_DEV - 1,)),
        ]
    return pl.pallas_call(
        body, name="wgrad", grid=(N_STEPS,), in_specs=in_specs, out_specs=out_specs, out_shape=out_shape,
        scratch_shapes=scratch,
        compiler_params=pltpu.CompilerParams(dimension_semantics=("arbitrary",), vmem_limit_bytes=VMEM_LIMIT),
    )(*args)


def _adamw(w, g, m, v):
    m = ADAM_B1 * m + (1.0 - ADAM_B1) * g
    v = ADAM_B2 * v + (1.0 - ADAM_B2) * (g * g)
    m_hat = m / (1.0 - ADAM_B1 ** ADAM_STEP)
    v_hat = v / (1.0 - ADAM_B2 ** ADAM_STEP)
    delta = -ADAM_LR * (m_hat / (jnp.sqrt(v_hat) + ADAM_EPS) + ADAM_WD * w)
    return delta, m, v


def _adam_sharded(name, w, g, m, v, rows):
    _, r, cols = w.shape

    def body(w_ref, g_ref, m_ref, v_ref, d_out, m_out, v_out):
        d_out[...], m_out[...], v_out[...] = _adamw(w_ref[...], g_ref[...], m_ref[...], v_ref[...])

    blk = pl.BlockSpec((None, rows, cols), lambda l, i: (l, i, 0))
    shape = jax.ShapeDtypeStruct(w.shape, F32)
    return pl.pallas_call(
        body, name=name, grid=(DEPTH, r // rows), in_specs=[blk] * 4, out_specs=[blk] * 3, out_shape=[shape] * 3,
        compiler_params=pltpu.CompilerParams(dimension_semantics=("arbitrary", "arbitrary"), vmem_limit_bytes=VMEM_LIMIT),
    )(w, g, m, v)


def _adam_w_ada(w, m, v, c_all, dmods):
    rows = ADAM_ROWS

    def body(c_ref, dm_ref, w_ref, m_ref, v_ref, g_out, d_out, m_out, v_out):
        l = pl.program_id(0)
        chip = 2 * lax.axis_index("x") + lax.axis_index("y")
        cv = c_ref[...]
        silu_c = (cv * _sigmoid(cv)).astype(BF16).astype(F32)
        pieces = []
        for k in range(W_ADA_COLS // HEAD):
            dk = dm_ref[l, (W_ADA_COLS // HEAD) * chip + k].astype(BF16).astype(F32)
            pieces.append(_dot_exact(silu_c, dk, TN))
        g = jnp.concatenate(pieces, axis=1)
        g_out[...] = g
        d_out[...], m_out[...], v_out[...] = _adamw(w_ref[...], g, m_ref[...], v_ref[...])

    blk = pl.BlockSpec((None, rows, W_ADA_COLS), lambda l, i: (l, i, 0))
    shape = jax.ShapeDtypeStruct(w.shape, F32)
    return pl.pallas_call(
        body, name="adam_w_ada", grid=(DEPTH, D_MODEL // rows),
        in_specs=[pl.BlockSpec((N_DEV, rows), lambda l, i: (0, i)), _const_in((DEPTH, 24, N_DEV, HEAD)), blk, blk, blk],
        out_specs=[blk] * 4, out_shape=[shape] * 4,
        compiler_params=pltpu.CompilerParams(dimension_semantics=("arbitrary", "arbitrary"), vmem_limit_bytes=VMEM_LIMIT),
    )(c_all, dmods, w, m, v)


def _adam_small(packs, dmods, weights, ms, vs):
    n = len(weights)

    def body(*refs):
        pack_refs, dm_refs = refs[0], refs[1]
        b = 2
        w_refs, m_refs, v_refs = refs[b:b + n], refs[b + n:b + 2 * n], refs[b + 2 * n:b + 3 * n]
        outs = refs[b + 3 * n:]
        g_refs, d_refs, nm_refs, nv_refs = outs[0:n], outs[n:2 * n], outs[2 * n:3 * n], outs[3 * n:4 * n]
        outs[4 * n][...] = pack_refs.at[DEPTH - 1][PK_LOSS:PK_LOSS + 1, 0:1] * (0.5 / D_MODEL)

        def lanes(l, row0, count):
            return jnp.concatenate([pack_refs.at[l][row0 + k:row0 + k + 1, :] for k in range(count)], axis=1)

        def update(idx, at, g):
            g_refs[idx][at] = g
            d_refs[idx][at], nm_refs[idx][at], nv_refs[idx][at] = _adamw(w_refs[idx][at], g, m_refs[idx][at], v_refs[idx][at])

        for l in range(DEPTH):
            row = (slice(l, l + 1), slice(None))
            g_b = None
            for d in range(N_DEV):
                part = dm_refs.at[l][:, d, :]
                g_b = part if g_b is None else g_b + part
            update(0, row, jnp.concatenate([g_b[k:k + 1, :] for k in range(24)], axis=1))
            for g in range(N_HEAD):
                update(1, (l, g), pack_refs.at[l][PK_W_POOL + g * HEAD:PK_W_POOL + (g + 1) * HEAD, :])
                update(5, (l, g), pack_refs.at[l][PK_W_SGU + g * HEAD:PK_W_SGU + (g + 1) * HEAD, :])
            update(2, row, lanes(l, PK_POOL_SCALE, N_HEAD))
            update(3, (l,), pack_refs.at[l][PK_SGU_LN_G:PK_SGU_LN_G + N_HEAD, :])
            update(4, (l,), pack_refs.at[l][PK_SGU_LN_B:PK_SGU_LN_B + N_HEAD, :])
            update(6, (l,), pack_refs.at[l][PK_B_SGU:PK_B_SGU + N_HEAD, :])
            update(7, row, lanes(l, PK_LN_G, D_MODEL // HEAD))
            update(8, row, lanes(l, PK_LN_B, D_MODEL // HEAD))

    vmem = pl.BlockSpec(memory_space=pltpu.VMEM)
    shapes = [jax.ShapeDtypeStruct(w.shape, F32) for w in weights]
    return pl.pallas_call(
        body, name="adam_small", in_specs=[vmem] * (2 + 3 * n), out_specs=[vmem] * (4 * n + 1),
        out_shape=shapes * 4 + [jax.ShapeDtypeStruct((1, 1), F32)],
        compiler_params=pltpu.CompilerParams(vmem_limit_bytes=VMEM_LIMIT),
    )(packs, dmods, *weights, *ms, *vs)


def kernel(x, c, w_ada, b_ada, w_in, w_pool, pool_scale, sgu_ln_g, sgu_ln_b, w_sgu, b_sgu, w_out, ln_g, ln_b, loss_target, m_w_ada, m_b_ada, m_w_in, m_w_pool, m_pool_scale, m_sgu_ln_g, m_sgu_ln_b, m_w_sgu, m_b_sgu, m_w_out, m_ln_g, m_ln_b, v_w_ada, v_b_ada, v_w_in, v_w_pool, v_pool_scale, v_sgu_ln_g, v_sgu_ln_b, v_w_sgu, v_b_sgu, v_w_out, v_ln_g, v_ln_b):
    w_in0, w_in1, w_out0, w_out1, mod, c_all = _prepare(c, w_ada, b_ada, w_in, w_out)
    small = (w_pool, pool_scale, sgu_ln_g, sgu_ln_b, w_sgu, jnp.swapaxes(b_sgu, 1, 2))

    proj0, y0, x1, w_in1, w_out1 = _forward_layer(0, x, mod, w_in0, w_out0, small, ln_g, ln_b,
                                                  next_weights=(w_in1, w_out1))
    proj1, y1, dout, sq = _forward_layer(1, x1, mod, w_in1, w_out1, small, ln_g, ln_b, target=loss_target)

    dx1, *shared = _backward_layer(1, dout, x1, y1, proj1, mod, w_in1, w_out1, small, ln_g, sq=sq)
    dx0, h, cat, dy, dproj, pack, dmod = _backward_layer(0, dx1, x, y0, proj0, mod, w_in0, w_out0, small, ln_g, shared=shared)
    g_in, g_out, pack, dmods = _wgrad_reduce(h, dproj, cat, dy, pack, dmod)

    ada = _adam_w_ada(w_ada, m_w_ada, v_w_ada, c_all, dmods)
    g_out = g_out.reshape(DEPTH, W_OUT_ROWS, D_MODEL)
    win = (g_in, *_adam_sharded("adam_w_in", w_in, g_in, m_w_in, v_w_in, ADAM_ROWS))
    wout = (g_out, *_adam_sharded("adam_w_out", w_out, g_out, m_w_out, v_w_out, W_OUT_ROWS))
    small_w = (b_ada, w_pool, pool_scale, sgu_ln_g, sgu_ln_b, w_sgu, b_sgu, ln_g, ln_b)
    small_m = (m_b_ada, m_w_pool, m_pool_scale, m_sgu_ln_g, m_sgu_ln_b, m_w_sgu, m_b_sgu, m_ln_g, m_ln_b)
    small_v = (v_b_ada, v_w_pool, v_pool_scale, v_sgu_ln_g, v_sgu_ln_b, v_w_sgu, v_b_sgu, v_ln_g, v_ln_b)
    res = _adam_small(pack, dmods, small_w, small_m, small_v)
    n = len(small_w)
    loss = res[4 * n].reshape(())

    def ordered(k):
        s = res[k * n:(k + 1) * n]
        return (ada[k], s[0], win[k], s[1], s[2], s[3], s[4], s[5], s[6], wout[k], s[7], s[8])

    return (loss, dx0[None], *ordered(0), *ordered(1), *ordered(2), *ordered(3))
```

```python
import jax
import jax.numpy as jnp
from jax import lax
from jax.experimental import pallas as pl
from jax.experimental.pallas import tpu as pltpu

F32 = jnp.float32
BF16 = jnp.bfloat16
MESH = pl.DeviceIdType.MESH

N_DEV = 8
N_CHIP = 4
DEPTH = 2
SEQ = 2048
D_MODEL = 1024
D_POOL = 512
D_PROJ = 2560
HEAD = 128
N_HEAD = 4
ROWS = 256
N_TILE = SEQ // ROWS
HALO = 16
W_IN_COLS = D_PROJ // N_CHIP
W_OUT_ROWS = D_MODEL // N_CHIP
W_ADA_COLS = 3 * D_MODEL // N_CHIP
HALF_IN = D_MODEL // 2
HALF_OUT = W_OUT_ROWS // 2
DEEPNORM_ALPHA = (2.0 * DEPTH) ** 0.25
LN_EPS = 1e-5
INV_SQRT2 = 0.7071067811865476
INV_SQRT_2PI = 0.3989422804014327

ADAM_LR = 0.001
ADAM_B1 = 0.9
ADAM_B2 = 0.999
ADAM_EPS = 1e-08
ADAM_WD = 0.01
ADAM_STEP = 10
ADAM_ROWS = 512

PK_W_POOL = 0
PK_W_SGU = 512
PK_POOL_SCALE = 1024
PK_SGU_LN_G = 1032
PK_SGU_LN_B = 1040
PK_B_SGU = 1048
PK_LN_G = 1056
PK_LN_B = 1064
PK_LOSS = 1072
PK_ROWS = 1088
PK_HALF = PK_ROWS // 2
PK_PIECE = PK_HALF // N_CHIP

VMEM_LIMIT = 56 * 1024 * 1024

GATHER_SECOND_ROUND_STEP = 1
GATHER_PASS_STEP = 3

NN = (((1,), (0,)), ((), ()))
NT = (((1,), (1,)), ((), ()))
TN = (((0,), (0,)), ((), ()))


def _dot(a, b, dims=NN):
    return lax.dot_general(a, b, dims, preferred_element_type=F32)


def _dot_exact(a, b, dims=NN):
    return lax.dot_general(a, b, dims, preferred_element_type=F32, precision=lax.Precision.HIGHEST)


def _layer_norm(v):
    mu = jnp.mean(v, axis=-1, keepdims=True)
    d = v - mu
    var = jnp.mean(d * d, axis=-1, keepdims=True)
    rstd = lax.rsqrt(var + LN_EPS)
    return d * rstd, rstd


def _layer_norm_bwd(dvhat, vhat, rstd):
    m1 = jnp.mean(dvhat, axis=-1, keepdims=True)
    m2 = jnp.mean(dvhat * vhat, axis=-1, keepdims=True)
    return rstd * (dvhat - m1 - vhat * m2)


def _sigmoid(v):
    return 1.0 / (1.0 + jnp.exp(-v))


def _gelu_parts(v):
    phi = 0.5 * (1.0 + lax.erf(v * INV_SQRT2))
    pdf = INV_SQRT_2PI * jnp.exp(-0.5 * v * v)
    return phi, pdf


def _sum_rows(v):
    return jnp.sum(v, axis=0, keepdims=True)


def _window_sums(ext, toward_later):
    n = ext.shape[0]

    def shifted(v, k):
        return pltpu.roll(v, (n - k) if toward_later else k, 0)

    s2 = ext + shifted(ext, 1)
    r4 = s2[:, HEAD:]
    s4 = r4 + shifted(r4, 2)
    r8 = s4[:, HEAD:]
    s8 = r8 + shifted(r8, 4)
    r16 = s8[:, HEAD:]
    s16 = r16 + shifted(r16, 8)
    return jnp.concatenate([s2[:, :HEAD], s4[:, :HEAD], s8[:, :HEAD], s16], axis=1)


def _window_counts(row0):
    t1 = row0 + 1 + lax.broadcasted_iota(jnp.int32, (ROWS, D_POOL), 0)
    lane = lax.broadcasted_iota(jnp.int32, (ROWS, D_POOL), 1)
    width = jnp.where(lane < HEAD, 2, jnp.where(lane < 2 * HEAD, 4, jnp.where(lane < 3 * HEAD, 8, 16)))
    return jnp.minimum(t1, width).astype(F32)


def _causal_mask():
    r = lax.broadcasted_iota(jnp.int32, (HEAD, HEAD), 0)
    s = lax.broadcasted_iota(jnp.int32, (HEAD, HEAD), 1)
    return r >= s


def _chunks_to_lanes(v):
    return jnp.concatenate([v[n * HEAD:(n + 1) * HEAD] for n in range(ROWS // HEAD)], axis=1)


def _lanes_to_chunks(v):
    return jnp.concatenate([v[:, n * HEAD:(n + 1) * HEAD] for n in range(ROWS // HEAD)], axis=0)


def _mixer(proj, halo, row0, wpool_ref, pscale, sgu_g_ref, sgu_b_ref, wsgu_ref, bsgu_t_ref):
    xa = proj[:, 0:512]
    ga = proj[:, 512:1024]
    u = proj[:, 1024:1536]
    v = proj[:, 1536:2048]
    gb = proj[:, 2048:2560]
    ext = jnp.concatenate([halo, xa], axis=0)
    win = _window_sums(ext, toward_later=False)[HALO:]
    cnt = _window_counts(row0)
    pooled = (win / cnt - xa).astype(BF16)
    pw = jnp.concatenate(
        [_dot(pooled[:, g * HEAD:(g + 1) * HEAD], wpool_ref[g].astype(BF16)) for g in range(N_HEAD)], axis=1)
    sig_a = _sigmoid(ga)
    ya = pw * pscale * (ga * sig_a)
    phi_u, pdf_u = _gelu_parts(u)
    phi_v, pdf_v = _gelu_parts(v)
    gu = u * phi_u
    gv = v * phi_v
    sig_b = _sigmoid(gb)
    silu_b = gb * sig_b
    mask = _causal_mask()
    vhat, rstd_v, vln_l, mixed = [], [], [], []
    for h in range(N_HEAD):
        vh, rh = _layer_norm(gv[:, h * HEAD:(h + 1) * HEAD])
        ln = (vh * sgu_g_ref[h:h + 1, :] + sgu_b_ref[h:h + 1, :]).astype(BF16)
        ln_l = _chunks_to_lanes(ln)
        wm = jnp.where(mask, wsgu_ref[h], 0.0).astype(BF16)
        mx = _lanes_to_chunks(_dot(wm, ln_l) + bsgu_t_ref[:, h:h + 1])
        vhat.append(vh)
        rstd_v.append(rh)
        vln_l.append(ln_l)
        mixed.append(mx)
    mixed = jnp.concatenate(mixed, axis=1)
    yb = gu * mixed * silu_b
    return dict(xa=xa, ga=ga, u=u, gb=gb, cnt=cnt, pooled=pooled, pw=pw, sig_a=sig_a, ya=ya, phi_u=phi_u, pdf_u=pdf_u,
                phi_v=phi_v, pdf_v=pdf_v, gu=gu, sig_b=sig_b, silu_b=silu_b, vhat=vhat, rstd_v=rstd_v, vln_l=vln_l,
                mixed=mixed, yb=yb, mask=mask)


def _const(shape, *index):
    lead = tuple(index) + (0,) * (len(shape) - len(index))
    return pl.BlockSpec(shape, lambda *_: lead)


def _const_in(shape, *index):
    lead = tuple(index) + (0,) * (len(shape) - len(index))
    return pl.BlockSpec(shape, lambda *_: lead, pipeline_mode=pl.Buffered(1))


def _layer_weight_specs(l):
    return [
        _const_in((None, N_HEAD, HEAD, HEAD), l),
        _const_in((DEPTH, D_POOL)),
        _const_in((None, N_HEAD, HEAD), l),
        _const_in((None, N_HEAD, HEAD), l),
        _const_in((None, N_HEAD, HEAD, HEAD), l),
        _const_in((None, HEAD, N_HEAD), l),
    ]


def _forward_layer(l, x, mod, w_in, w_out, small, ln_g, ln_b, target=None, next_weights=None):
    last = target is not None
    gathers = next_weights is not None

    def body(*refs):
        if last:
            (x_ref, mod_ref, win_ref, wout_ref, wpool_ref, pscale_ref, sgu_g_ref, sgu_b_ref, wsgu_ref, bsgu_t_ref,
             lng_ref, lnb_ref, tgt_ref, proj_ref, y_ref, out_ref, loss_ref, carry_ref) = refs
        elif gathers:
            (x_ref, mod_ref, win_ref, wout_ref, wpool_ref, pscale_ref, sgu_g_ref, sgu_b_ref, wsgu_ref, bsgu_t_ref,
             lng_ref, lnb_ref, _, _, proj_ref, y_ref, out_ref, next_in, next_out, carry_ref, g_send, g_recv) = refs
            gather = _WeightGather(_Place(), next_in, next_out, g_send, g_recv)
        else:
            (x_ref, mod_ref, win_ref, wout_ref, wpool_ref, pscale_ref, sgu_g_ref, sgu_b_ref, wsgu_ref, bsgu_t_ref,
             lng_ref, lnb_ref, proj_ref, y_ref, out_ref, carry_ref) = refs
        i = pl.program_id(0)

        @pl.when(i == 0)
        def _():
            carry_ref[...] = jnp.zeros_like(carry_ref)
            if last:
                loss_ref[...] = jnp.zeros_like(loss_ref)
            if gathers:
                gather.start_first_round()

        if gathers:
            for q in range(_WeightGather.CHUNKS):
                @pl.when(i == GATHER_SECOND_ROUND_STEP + q)
                def _(q=q):
                    gather.start_second_round(q)

                @pl.when(i == GATHER_PASS_STEP + q)
                def _(q=q):
                    gather.pass_second_round(q)

        x = x_ref[...]
        shift, scale, gate = mod_ref[0:1, :], mod_ref[1:2, :], mod_ref[2:3, :]
        xn, _ = _layer_norm(x)
        h = xn * (1.0 + scale) + shift
        proj = _dot(h.astype(BF16), win_ref[...])
        proj_ref[...] = proj
        m = _mixer(proj, carry_ref[...], i * ROWS, wpool_ref, pscale_ref[l:l + 1, :], sgu_g_ref, sgu_b_ref, wsgu_ref,
                   bsgu_t_ref)
        carry_ref[...] = m["xa"][ROWS - HALO:]
        cat = jnp.concatenate([m["ya"], m["yb"]], axis=1).astype(BF16)
        y = _dot(cat, wout_ref[...])
        y_ref[...] = y
        zn, _ = _layer_norm(DEEPNORM_ALPHA * x + gate * y)
        out = zn * lng_ref[l:l + 1, :] + lnb_ref[l:l + 1, :]
        if last:
            err = out - tgt_ref[...]
            out_ref[...] = err * (1.0 / D_MODEL)
            loss_ref[...] += jnp.sum(err * err)
        else:
            out_ref[...] = out

        if gathers:
            @pl.when(i == N_TILE - 1)
            def _():
                gather.finish()

    tile = pl.BlockSpec((ROWS, D_MODEL), lambda i: (i, 0))
    tile3 = pl.BlockSpec((None, ROWS, D_MODEL), lambda i: (0, i, 0))
    in_specs = [tile3 if x.ndim == 3 else tile, _const_in((None, 8, D_MODEL), l), _const_in((D_MODEL, D_PROJ)),
                _const_in((D_MODEL, D_MODEL))]
    in_specs += _layer_weight_specs(l) + [_const_in((DEPTH, D_MODEL)), _const_in((DEPTH, D_MODEL))]
    out_shape = [jax.ShapeDtypeStruct((SEQ, D_PROJ), F32), jax.ShapeDtypeStruct((SEQ, D_MODEL), F32),
                 jax.ShapeDtypeStruct((SEQ, D_MODEL), F32)]
    out_specs = [pl.BlockSpec((ROWS, D_PROJ), lambda i: (i, 0)), tile, tile]
    args = [x, mod, w_in, w_out, *small, ln_g, ln_b]
    scratch = [pltpu.VMEM((HALO, D_POOL), F32)]
    aliases = {}
    if last:
        in_specs.append(tile3)
        args.append(target)
        out_shape.append(jax.ShapeDtypeStruct((8, HEAD), F32))
        out_specs.append(_const((8, HEAD)))
    if gathers:
        hbm = pl.BlockSpec(memory_space=pl.ANY)
        aliases = {len(args): len(out_shape), len(args) + 1: len(out_shape) + 1}
        in_specs += [hbm, hbm]
        args += list(next_weights)
        out_shape += [jax.ShapeDtypeStruct(w.shape, BF16) for w in next_weights]
        out_specs += [hbm, hbm]
        scratch += [pltpu.SemaphoreType.DMA((_WeightGather.N_SEMS,)), pltpu.SemaphoreType.DMA((_WeightGather.N_SEMS,))]
    return pl.pallas_call(
        body, name="fwd_last" if last else "fwd_first", grid=(N_TILE,), in_specs=in_specs, out_specs=out_specs,
        out_shape=out_shape, scratch_shapes=scratch, input_output_aliases=aliases,
        compiler_params=pltpu.CompilerParams(dimension_semantics=("arbitrary",), vmem_limit_bytes=VMEM_LIMIT),
    )(*args)


def _backward_layer(l, dout, x, y, proj, mod, w_in, w_out, small, ln_g, sq=None, shared=None):
    has_loss = sq is not None

    def body(*refs):
        (dout_ref, x_ref, y_ref, proj_ref, halo_ref, mod_ref, win_ref, wout_ref, wpool_ref, pscale_ref,
         sgu_g_ref, sgu_b_ref, wsgu_ref, bsgu_t_ref, lng_ref) = refs[:15]
        n_in = 16 if has_loss else 15 + 6
        dx_ref, h_ref, cat_ref, dy_ref, dproj_ref, pack_ref, dmod_ref, carry_ref = refs[n_in:n_in + 8]
        i = pl.program_id(0)
        tile = N_TILE - 1 - i

        @pl.when(i == 0)
        def _():
            carry_ref[...] = jnp.zeros_like(carry_ref)
            pack_ref[...] = jnp.zeros_like(pack_ref)
            dmod_ref[...] = jnp.zeros_like(dmod_ref)
            if has_loss:
                pack_ref[PK_LOSS:PK_LOSS + 8, :] = refs[15][...]

        x = x_ref[...]
        y = y_ref[...]
        dout = dout_ref[...]
        pscale = pscale_ref[l:l + 1, :]
        shift, scale, gate = mod_ref[0:1, :], mod_ref[1:2, :], mod_ref[2:3, :]
        xn, rstd_x = _layer_norm(x)
        h = xn * (1.0 + scale) + shift
        h_ref[...] = h.astype(BF16)
        zn, rstd_z = _layer_norm(DEEPNORM_ALPHA * x + gate * y)
        g_ln_g = _sum_rows(dout * zn)
        g_ln_b = _sum_rows(dout)
        dz = _layer_norm_bwd(dout * lng_ref[l:l + 1, :], zn, rstd_z)
        d_gate = _sum_rows(dz * y)
        dy = (gate * dz).astype(BF16)
        dy_ref[...] = dy

        halo = jnp.where(tile > 0, halo_ref[...], 0.0)
        m = _mixer(proj_ref[...], halo, tile * ROWS, wpool_ref, pscale, sgu_g_ref, sgu_b_ref, wsgu_ref, bsgu_t_ref)
        cat_ref[...] = jnp.concatenate([m["ya"], m["yb"]], axis=1).astype(BF16)
        dcat = _dot(dy, wout_ref[...], NT)
        dya = dcat[:, :D_POOL]
        dyb = dcat[:, D_POOL:]

        ga, sig_a = m["ga"], m["sig_a"]
        dp = dya * (ga * sig_a)
        d_ga = dya * (m["pw"] * pscale) * (sig_a * (1.0 + ga * (1.0 - sig_a)))
        g_pscale = _sum_rows(dp * m["pw"])
        dpw = (dp * pscale).astype(BF16)
        dpooled = []
        for g in range(N_HEAD):
            cols = slice(g * HEAD, (g + 1) * HEAD)
            pack_ref[PK_W_POOL + g * HEAD:PK_W_POOL + (g + 1) * HEAD, :] += _dot(m["pooled"][:, cols], dpw[:, cols], TN)
            dpooled.append(_dot(dpw[:, cols], wpool_ref[g].astype(BF16), NT))
        dpooled = jnp.concatenate(dpooled, axis=1)
        q = dpooled / m["cnt"]
        ext = jnp.concatenate([q, carry_ref[...]], axis=0)
        d_xa = _window_sums(ext, toward_later=True)[:ROWS] - dpooled
        carry_ref[...] = q[:HALO]

        gu, mixed, silu_b, gb, sig_b = m["gu"], m["mixed"], m["silu_b"], m["gb"], m["sig_b"]
        d_mixed = dyb * gu * silu_b
        d_gu = dyb * mixed * silu_b
        d_gb = dyb * gu * mixed * (sig_b * (1.0 + gb * (1.0 - sig_b)))
        d_u = d_gu * (m["phi_u"] + m["u"] * m["pdf_u"])
        ones = jnp.ones((8, HEAD), F32)
        d_v = []
        for hd in range(N_HEAD):
            cols = slice(hd * HEAD, (hd + 1) * HEAD)
            dm = d_mixed[:, cols]
            dm_l = _chunks_to_lanes(dm.astype(BF16))
            g_w = _dot(dm_l, m["vln_l"][hd], NT)
            pack_ref[PK_W_SGU + hd * HEAD:PK_W_SGU + (hd + 1) * HEAD, :] += jnp.where(m["mask"], g_w, 0.0)
            dm_sum = dm[0:HEAD]
            for n in range(1, ROWS // HEAD):
                dm_sum = dm_sum + dm[n * HEAD:(n + 1) * HEAD]
            pack_ref[PK_B_SGU + hd:PK_B_SGU + hd + 1, :] += _dot_exact(ones, dm_sum, NT)[0:1]
            wm = jnp.where(m["mask"], wsgu_ref[hd], 0.0).astype(BF16)
            d_vln = _lanes_to_chunks(_dot(wm, dm_l, TN))
            vhat = m["vhat"][hd]
            pack_ref[PK_SGU_LN_G + hd:PK_SGU_LN_G + hd + 1, :] += _sum_rows(d_vln * vhat)
            pack_ref[PK_SGU_LN_B + hd:PK_SGU_LN_B + hd + 1, :] += _sum_rows(d_vln)
            d_v.append(_layer_norm_bwd(d_vln * sgu_g_ref[hd:hd + 1, :], vhat, m["rstd_v"][hd]))
        v = proj_ref[:, 1536:2048]
        d_v = jnp.concatenate(d_v, axis=1) * (m["phi_v"] + v * m["pdf_v"])

        dproj = jnp.concatenate([d_xa, d_ga, d_u, d_v, d_gb], axis=1).astype(BF16)
        dproj_ref[...] = dproj
        dh = _dot(dproj, win_ref[...], NT)
        d_scale = _sum_rows(dh * xn)
        d_shift = _sum_rows(dh)
        dx_ref[...] = DEEPNORM_ALPHA * dz + _layer_norm_bwd(dh * (1.0 + scale), xn, rstd_x)

        dmod_ref[0:1, :] += d_shift
        dmod_ref[1:2, :] += d_scale
        dmod_ref[2:3, :] += d_gate
        for g in range(N_HEAD):
            pack_ref[PK_POOL_SCALE + g:PK_POOL_SCALE + g + 1, :] += g_pscale[:, g * HEAD:(g + 1) * HEAD]
        for k in range(D_MODEL // HEAD):
            pack_ref[PK_LN_G + k:PK_LN_G + k + 1, :] += g_ln_g[:, k * HEAD:(k + 1) * HEAD]
            pack_ref[PK_LN_B + k:PK_LN_B + k + 1, :] += g_ln_b[:, k * HEAD:(k + 1) * HEAD]

    def rev(i):
        return (N_TILE - 1 - i, 0)

    tile = pl.BlockSpec((ROWS, D_MODEL), rev)
    tile3 = pl.BlockSpec((None, ROWS, D_MODEL), lambda i: (0, N_TILE - 1 - i, 0))
    halo = pl.BlockSpec((HALO, D_POOL), lambda i: (jnp.maximum((N_TILE - 1 - i) * (ROWS // HALO) - 1, 0), 0))
    in_specs = [tile, tile3 if x.ndim == 3 else tile, tile, pl.BlockSpec((ROWS, D_PROJ), rev), halo,
                _const_in((None, 8, D_MODEL), l), _const_in((D_MODEL, D_PROJ)), _const_in((D_MODEL, D_MODEL))]
    in_specs += _layer_weight_specs(l) + [_const_in((DEPTH, D_MODEL))]
    args = [dout, x, y, proj, proj, mod, w_in, w_out, *small, ln_g]
    stacked = lambda cols: pl.BlockSpec((None, ROWS, cols), lambda i: (l, N_TILE - 1 - i, 0))
    out_shape = [jax.ShapeDtypeStruct((SEQ, D_MODEL), F32), jax.ShapeDtypeStruct((DEPTH, SEQ, D_MODEL), BF16),
                 jax.ShapeDtypeStruct((DEPTH, SEQ, D_MODEL), BF16), jax.ShapeDtypeStruct((DEPTH, SEQ, D_MODEL), BF16),
                 jax.ShapeDtypeStruct((DEPTH, SEQ, D_PROJ), BF16), jax.ShapeDtypeStruct((DEPTH, PK_ROWS, HEAD), F32),
                 jax.ShapeDtypeStruct((DEPTH, 8, D_MODEL), F32)]
    out_specs = [tile, stacked(D_MODEL), stacked(D_MODEL), stacked(D_MODEL), stacked(D_PROJ),
                 _const((None, PK_ROWS, HEAD), l), _const((None, 8, D_MODEL), l)]
    aliases = {}
    if has_loss:
        in_specs.append(_const_in((8, HEAD)))
        args.append(sq)
    else:
        aliases = {len(args) + k: 1 + k for k in range(len(shared))}
        in_specs += [pl.BlockSpec(memory_space=pl.ANY)] * len(shared)
        args += list(shared)
    return pl.pallas_call(
        body, name="bwd_last" if has_loss else "bwd_first", grid=(N_TILE,), in_specs=in_specs, out_specs=out_specs,
        out_shape=out_shape, scratch_shapes=[pltpu.VMEM((HALO, D_POOL), F32)], input_output_aliases=aliases,
        compiler_params=pltpu.CompilerParams(dimension_semantics=("arbitrary",), vmem_limit_bytes=VMEM_LIMIT),
    )(*args)


def _flip(v, f):
    return v + f - 2 * v * f


class _Place:
    def __init__(self):
        x, y, c = lax.axis_index("x"), lax.axis_index("y"), lax.axis_index("c")
        self.x, self.y, self.c = x, y, c
        self.chip = 2 * x + y
        self.dev = 4 * x + 2 * y + c
        self.sibling = (x, y, 1 - c)
        x1, y1 = _flip(x, 1 - c), _flip(y, c)
        x2, y2 = _flip(x, c), _flip(y, 1 - c)
        self.first = (x1, y1, c)
        self.second = (x2, y2, c)
        self.chip_first = 2 * x1 + y1
        self.chip_second = 2 * x2 + y2
        self.chip_far = 2 * (1 - x) + (1 - y)
        self.my_first_coord = jnp.where(c == 0, x, y)

    def first_coord(self, ch):
        return jnp.where(self.c == 0, ch // 2, ch % 2)

    def others(self):
        return [(_flip(self.x, (r >> 2) & 1), _flip(self.y, (r >> 1) & 1), _flip(self.c, r & 1)) for r in range(1, N_DEV)]

    def other_chips(self):
        return [(1 - self.x, self.y), (self.x, 1 - self.y), (1 - self.x, 1 - self.y)]


class _WeightGather:
    CHUNKS = 4
    N_SEMS = 12 * CHUNKS

    def __init__(self, place, win, wout, send, recv):
        self.p, self.win, self.wout, self.send, self.recv = place, win, wout, send, recv
        p = place
        self.plan = [(p.chip, p.first), (p.chip, p.second), (p.chip_first, p.second),
                     (p.chip_first, p.sibling), (p.chip_second, p.sibling), (p.chip_far, p.sibling)]

    def _copies(self, k, q):
        ch, target = self.plan[k]
        n_in, n_out = HALF_IN // self.CHUNKS, HALF_OUT // self.CHUNKS
        rows_in = pl.ds(pl.multiple_of(self.p.c * HALF_IN + q * n_in, n_in), n_in)
        cols_in = pl.ds(pl.multiple_of(ch * W_IN_COLS, 128), W_IN_COLS)
        rows_out = pl.ds(pl.multiple_of(ch * W_OUT_ROWS + self.p.c * HALF_OUT + q * n_out, n_out), n_out)
        r_in = self.win.at[rows_in, cols_in]
        r_out = self.wout.at[rows_out, :]
        s = 2 * (6 * q + k)
        return [pltpu.make_async_remote_copy(r_in, r_in, self.send.at[s], self.recv.at[s],
                                             device_id=target, device_id_type=MESH),
                pltpu.make_async_remote_copy(r_out, r_out, self.send.at[s + 1], self.recv.at[s + 1],
                                             device_id=target, device_id_type=MESH)]

    def _start(self, k, q):
        for cp in self._copies(k, q):
            cp.start()

    def _landed(self, k, q):
        for cp in self._copies(k, q):
            cp.wait_recv()

    def start_first_round(self):
        for q in range(self.CHUNKS):
            self._start(0, q)

    def start_second_round(self, q):
        self._landed(0, q)
        self._start(1, q)
        self._start(2, q)
        self._start(3, q)

    def pass_second_round(self, q):
        self._landed(1, q)
        self._start(4, q)
        self._landed(2, q)
        self._start(5, q)

    def finish(self):
        for q in range(self.CHUNKS):
            for k in (3, 4, 5):
                self._landed(k, q)
        for q in range(self.CHUNKS):
            for k in range(len(self.plan)):
                for cp in self._copies(k, q):
                    cp.wait_send()


def _prepare(c_vec, w_ada, b_ada, w_in, w_out):
    def body(c_ref, wada_ref, bada_ref, win_ref, wout_ref,
             win0, win1, wout0, wout1, mod_ref, c_all,
             win_bf, wout_bf, mod_mine, mod_all, g_send, g_recv, c_send, c_recv, mod_send, mod_recv, local_sem):
        p = _Place()
        win_bf[...] = win_ref[...].astype(BF16)
        wout_bf[...] = wout_ref[...].astype(BF16)
        cols = pl.ds(pl.multiple_of(p.chip * W_IN_COLS, 128), W_IN_COLS)
        rows = pl.ds(pl.multiple_of(p.chip * W_OUT_ROWS, W_OUT_ROWS), W_OUT_ROWS)
        own = [pltpu.make_async_copy(win_bf.at[0], win0.at[:, cols], local_sem.at[0]),
               pltpu.make_async_copy(wout_bf.at[0], wout0.at[rows, :], local_sem.at[1]),
               pltpu.make_async_copy(win_bf.at[1], win1.at[:, cols], local_sem.at[2]),
               pltpu.make_async_copy(wout_bf.at[1], wout1.at[rows, :], local_sem.at[3])]
        for cp in own:
            cp.start()

        c_all[pl.ds(p.dev, 1), :] = c_ref[...]
        c_copies = [pltpu.make_async_remote_copy(c_ref, c_all.at[pl.ds(p.dev, 1), :], c_send.at[r], c_recv.at[r],
                                                 device_id=d, device_id_type=MESH) for r, d in enumerate(p.others())]
        for cp in c_copies:
            cp.start()
        own[0].wait()
        own[1].wait()
        gather = _WeightGather(p, win0, wout0, g_send, g_recv)
        gather.start_first_round()
        for cp in c_copies:
            cp.wait()

        cv = c_all[...]
        silu_c = (cv * _sigmoid(cv)).astype(BF16)
        for l in range(DEPTH):
            mod_mine[l] = _dot(silu_c, wada_ref[l].astype(BF16))
        mod_all[p.chip] = mod_mine[...]
        m_copies = [pltpu.make_async_remote_copy(mod_mine, mod_all.at[p.chip], mod_send.at[k], mod_recv.at[k],
                                                 device_id=(px, py, p.c), device_id_type=MESH)
                    for k, (px, py) in enumerate(p.other_chips())]
        for cp in m_copies:
            cp.start()
        for q in range(gather.CHUNKS):
            gather.start_second_round(q)
        for cp in m_copies:
            cp.wait()
        mod_ref[...] = jnp.zeros_like(mod_ref)
        for l in range(DEPTH):
            full = jnp.concatenate([mod_all[ch, l, pl.ds(p.dev, 1), :] for ch in range(N_CHIP)], axis=1) + bada_ref[l:l + 1, :]
            for k in range(3):
                mod_ref[l, k:k + 1, :] = full[:, k * D_MODEL:(k + 1) * D_MODEL]
        for q in range(gather.CHUNKS):
            gather.pass_second_round(q)
        gather.finish()
        own[2].wait()
        own[3].wait()

    vmem = pl.BlockSpec(memory_space=pltpu.VMEM)
    hbm = pl.BlockSpec(memory_space=pl.ANY)
    w_in_shape = jax.ShapeDtypeStruct((D_MODEL, D_PROJ), BF16)
    w_out_shape = jax.ShapeDtypeStruct((D_MODEL, D_MODEL), BF16)
    return pl.pallas_call(
        body, name="prepare",
        in_specs=[vmem, vmem, vmem, vmem, vmem],
        out_specs=[hbm, hbm, hbm, hbm, vmem, vmem],
        out_shape=[w_in_shape, w_in_shape, w_out_shape, w_out_shape,
                   jax.ShapeDtypeStruct((DEPTH, 8, D_MODEL), F32), jax.ShapeDtypeStruct((N_DEV, D_MODEL), F32)],
        scratch_shapes=[
            pltpu.VMEM((DEPTH, D_MODEL, W_IN_COLS), BF16), pltpu.VMEM((DEPTH, W_OUT_ROWS, D_MODEL), BF16),
            pltpu.VMEM((DEPTH, N_DEV, W_ADA_COLS), F32), pltpu.VMEM((N_CHIP, DEPTH, N_DEV, W_ADA_COLS), F32),
            pltpu.SemaphoreType.DMA((_WeightGather.N_SEMS,)), pltpu.SemaphoreType.DMA((_WeightGather.N_SEMS,)),
            pltpu.SemaphoreType.DMA((7,)), pltpu.SemaphoreType.DMA((7,)),
            pltpu.SemaphoreType.DMA((3,)), pltpu.SemaphoreType.DMA((3,)),
            pltpu.SemaphoreType.DMA((4,)),
        ],
        compiler_params=pltpu.CompilerParams(vmem_limit_bytes=VMEM_LIMIT),
    )(c_vec, w_ada, b_ada, w_in, w_out)


IN_STEPS = W_IN_COLS // HEAD
OUT_STEPS = 4
OUT_COLS = D_MODEL // OUT_STEPS
OUT_FIRST = 2
ITEMS = ([("out", k) for k in range(OUT_FIRST)] + [("in", k) for k in range(IN_STEPS)]
         + [("out", k) for k in range(OUT_FIRST, OUT_STEPS)])
N_ITEMS = len(ITEMS)
N_STEPS = DEPTH * N_ITEMS
DELAY_SUM, DELAY_SECOND, DELAY_FINAL = 1, 3, 5
SMALL_SCATTER_STEP, SMALL_GATHER_STEP, SMALL_PASS_STEP, SMALL_FINISH_STEP = 1, 3, 5, 7


def _wgrad_reduce(h, dproj, cat, dy, pack, dmod):
    def body(*refs):
        h_ref, dp_refs, cat_ref, dy_ref, pack_ref, dmod_ref = refs[0], refs[1:5], refs[5], refs[6], refs[7], refs[8]
        fin_in, fin_out, pack_out, dmod_out = refs[9:13]
        scratch = refs[13:]
        (mine_in, send_in, sib_in, st_in, r1_in, r2_in, f_in,
         mine_out, send_out, sib_out, st_out, r1_out, r2_out, f_out,
         d2d_s, d2d_r, r1_s, r1_r, r2_s, r2_r, fin_l, fin_s, fin_r) = scratch[:23]
        p = _Place()
        c = p.c
        i = pl.program_id(0)
        my_rows = pl.ds(pl.multiple_of(c * HALF_IN, HALF_IN), HALF_IN)

        def layer_of(j):
            return DEPTH - 1 - j // N_ITEMS

        def bufs(j):
            kind, k = ITEMS[j % N_ITEMS]
            if kind == "in":
                return [r.at[k] for r in (mine_in, send_in, sib_in, st_in, r1_in, r2_in, f_in)]
            return [r.at[k] for r in (mine_out, send_out, sib_out, st_out, r1_out, r2_out, f_out)]

        def piece(j, ref, ch):
            if ITEMS[j % N_ITEMS][0] == "in":
                return ref.at[:, ch * HEAD:(ch + 1) * HEAD]
            return ref.at[ch]

        def slot(ch):
            return jnp.where(c == 0, ch % 2, ch // 2)

        def to_sibling(j):
            _, send, sib, _, _, _, _ = bufs(j)
            return pltpu.make_async_remote_copy(send, sib, d2d_s.at[j], d2d_r.at[j], device_id=p.sibling, device_id_type=MESH)

        def first_round(j, ch):
            _, _, _, st, r1, _, _ = bufs(j)
            k = slot(ch)
            return pltpu.make_async_remote_copy(st.at[k], r1.at[k], r1_s.at[2 * j + k], r1_r.at[2 * j + k],
                                                device_id=p.first, device_id_type=MESH)

        def second_round(j):
            _, _, _, st, _, r2, _ = bufs(j)
            return pltpu.make_async_remote_copy(st.at[2], r2, r2_s.at[j], r2_r.at[j], device_id=p.second, device_id_type=MESH)

        def finals(j):
            f = bufs(j)[6]
            kind, k = ITEMS[j % N_ITEMS]
            if kind == "in":
                dst = fin_in.at[layer_of(j), my_rows, k * HEAD:(k + 1) * HEAD]
            else:
                dst = fin_out.at[layer_of(j), c, :, k * OUT_COLS:(k + 1) * OUT_COLS]
            return [pltpu.make_async_copy(f, dst, fin_l.at[j]),
                    pltpu.make_async_remote_copy(f, dst, fin_s.at[j], fin_r.at[j], device_id=p.sibling, device_id_type=MESH)]

        def stage_sum(j):
            mine, _, sib, st, _, _, _ = bufs(j)
            to_sibling(j).wait_recv()
            mine[...] = mine[...] + sib[...]
            for ch in range(N_CHIP):
                @pl.when(p.first_coord(ch) != p.my_first_coord)
                def _(ch=ch):
                    st[slot(ch)] = piece(j, mine, ch)[...].astype(BF16)
                    first_round(j, ch).start()

        def stage_second(j):
            mine, _, _, st, r1, _, _ = bufs(j)
            for ch in range(N_CHIP):
                @pl.when(p.first_coord(ch) == p.my_first_coord)
                def _(ch=ch):
                    first_round(j, ch).wait_recv()
                    part = piece(j, mine, ch)
                    total = part[...] + r1[slot(ch)].astype(F32)
                    part[...] = total

                    @pl.when(ch != p.chip)
                    def _():
                        st[2] = total.astype(BF16)
                        second_round(j).start()

        def stage_final(j):
            mine, _, _, _, _, r2, f = bufs(j)
            second_round(j).wait_recv()
            for ch in range(N_CHIP):
                @pl.when(ch == p.chip)
                def _(ch=ch):
                    f[...] = piece(j, mine, ch)[...] + r2[...].astype(F32)
            for cp in finals(j):
                cp.start()

        def drain(j):
            to_sibling(j).wait_send()
            for ch in range(N_CHIP):
                @pl.when(p.first_coord(ch) != p.my_first_coord)
                def _(ch=ch):
                    first_round(j, ch).wait_send()

                @pl.when(jnp.logical_and(p.first_coord(ch) == p.my_first_coord, ch != p.chip))
                def _():
                    second_round(j).wait_send()
            for cp in finals(j):
                cp.wait()

        dev = p.dev
        devices = p.others()

        def half(core):
            return pl.ds(pl.multiple_of(core * PK_HALF, 8), PK_HALF)

        def finished(core, ch):
            return pl.ds(pl.multiple_of(core * PK_HALF + ch * PK_PIECE, 8), PK_PIECE)

        def small_exchange(l, first_step, bufs_l):
            (pk_mine, pk_sib, pk_rs, pk_fin, pk_all, dm_st, dm_all, pk_sem, rs_s, rs_r, ag_s, ag_r, dm_s, dm_r) = bufs_l

            def pk_load():
                return pltpu.make_async_copy(pack_ref.at[l, half(c)], pk_mine, pk_sem.at[0])

            def pk_give():
                return pltpu.make_async_remote_copy(pack_ref.at[l, half(1 - c)], pk_sib, pk_sem.at[1], pk_sem.at[2],
                                                    device_id=p.sibling, device_id_type=MESH)

            def pk_scatter(ch):
                return pltpu.make_async_remote_copy(pk_mine.at[ch * PK_PIECE:(ch + 1) * PK_PIECE], pk_rs.at[p.chip],
                                                    rs_s.at[ch], rs_r.at[p.chip], device_id=(ch // 2, ch % 2, c),
                                                    device_id_type=MESH)

            def pk_spread(ch):
                return pltpu.make_async_remote_copy(pk_fin, pk_all.at[finished(c, p.chip)], ag_s.at[ch], ag_r.at[p.chip],
                                                    device_id=(ch // 2, ch % 2, c), device_id_type=MESH)

            def pk_pass():
                return pltpu.make_async_remote_copy(pk_all.at[half(c)], pk_all.at[half(c)], pk_sem.at[3], pk_sem.at[4],
                                                    device_id=p.sibling, device_id_type=MESH)

            def dm_copy(r):
                return pltpu.make_async_remote_copy(dm_st, dm_all.at[:, pl.ds(dev, 1), :], dm_s.at[r], dm_r.at[r],
                                                    device_id=devices[r], device_id_type=MESH)

            def results():
                return [pltpu.make_async_copy(pk_all, pack_out.at[l], pk_sem.at[0]),
                        pltpu.make_async_copy(dm_all, dmod_out.at[l], pk_sem.at[5])]

            @pl.when(i == first_step)
            def _():
                pk_load().start()
                pk_give().start()
                for k in range(3):
                    for r in range(D_MODEL // HEAD):
                        dm_st[8 * k + r] = dmod_ref[l, k:k + 1, r * HEAD:(r + 1) * HEAD]
                dm_all[:, pl.ds(dev, 1), :] = dm_st[...]
                for r in range(N_DEV - 1):
                    dm_copy(r).start()

            @pl.when(i == first_step + SMALL_SCATTER_STEP)
            def _():
                pk_load().wait()
                pk_give().wait()
                pk_mine[...] = pk_mine[...] + pk_sib[...]
                for ch in range(N_CHIP):
                    @pl.when(ch != p.chip)
                    def _(ch=ch):
                        pk_scatter(ch).start()

            @pl.when(i == first_step + SMALL_GATHER_STEP)
            def _():
                for ch in range(N_CHIP):
                    @pl.when(ch != p.chip)
                    def _(ch=ch):
                        pltpu.make_async_remote_copy(pk_fin, pk_rs.at[ch], rs_s.at[ch], rs_r.at[ch],
                                                     device_id=p.sibling, device_id_type=MESH).wait_recv()
                for me in range(N_CHIP):
                    @pl.when(me == p.chip)
                    def _(me=me):
                        total = None
                        for ch in range(N_CHIP):
                            part = pk_mine[me * PK_PIECE:(me + 1) * PK_PIECE] if ch == me else pk_rs[ch]
                            total = part if total is None else total + part
                        pk_fin[...] = total
                        pk_all[finished(c, me)] = total
                for ch in range(N_CHIP):
                    @pl.when(ch != p.chip)
                    def _(ch=ch):
                        pk_spread(ch).start()

            @pl.when(i == first_step + SMALL_PASS_STEP)
            def _():
                for ch in range(N_CHIP):
                    @pl.when(ch != p.chip)
                    def _(ch=ch):
                        pltpu.make_async_remote_copy(pk_fin, pk_all.at[finished(c, ch)], ag_s.at[ch], ag_r.at[ch],
                                                     device_id=p.sibling, device_id_type=MESH).wait_recv()
                pk_pass().start()

            @pl.when(i == first_step + SMALL_FINISH_STEP)
            def _():
                pk_pass().wait()
                for ch in range(N_CHIP):
                    @pl.when(ch != p.chip)
                    def _(ch=ch):
                        pk_scatter(ch).wait_send()
                        pk_spread(ch).wait_send()
                for r in range(N_DEV - 1):
                    dm_copy(r).wait()
                for cp in results():
                    cp.start()
                for cp in results():
                    cp.wait()

        n_small = 14
        for l in range(DEPTH):
            small_exchange(l, (DEPTH - 1 - l) * N_ITEMS, scratch[23 + n_small * l:23 + n_small * (l + 1)])

        for step in range(N_ITEMS, N_STEPS):
            @pl.when(i == step)
            def _(step=step):
                drain(step - N_ITEMS)

        ii = jnp.where(i < N_ITEMS, i, i - N_ITEMS)
        in_step = jnp.logical_and(ii >= OUT_FIRST, ii < OUT_FIRST + IN_STEPS)

        @pl.when(in_step)
        def _():
            k = ii - OUT_FIRST
            rhs = jnp.concatenate([r[...] for r in dp_refs], axis=1)
            res = _dot(h_ref[...], rhs, TN)

            @pl.when(c == 0)
            def _():
                mine_in[k] = res[:HALF_IN]
                send_in[k] = res[HALF_IN:]

            @pl.when(c == 1)
            def _():
                mine_in[k] = res[HALF_IN:]
                send_in[k] = res[:HALF_IN]

        @pl.when(jnp.logical_not(in_step))
        def _():
            k = jnp.where(ii < OUT_FIRST, ii, ii - IN_STEPS)
            res = _dot(cat_ref[...], dy_ref[...], TN)

            @pl.when(c == 0)
            def _():
                for ch in range(N_CHIP):
                    mine_out[k, ch] = res[ch * W_OUT_ROWS:ch * W_OUT_ROWS + HALF_OUT]
                    send_out[k, ch] = res[ch * W_OUT_ROWS + HALF_OUT:(ch + 1) * W_OUT_ROWS]

            @pl.when(c == 1)
            def _():
                for ch in range(N_CHIP):
                    mine_out[k, ch] = res[ch * W_OUT_ROWS + HALF_OUT:(ch + 1) * W_OUT_ROWS]
                    send_out[k, ch] = res[ch * W_OUT_ROWS:ch * W_OUT_ROWS + HALF_OUT]

        stages = ((0, lambda j: to_sibling(j).start()), (DELAY_SUM, stage_sum), (DELAY_SECOND, stage_second),
                  (DELAY_FINAL, stage_final))
        for step in range(N_STEPS):
            @pl.when(i == step)
            def _(step=step):
                for delay, stage in stages:
                    if step - delay >= 0:
                        stage(step - delay)

        @pl.when(i == N_STEPS - 1)
        def _():
            for step in range(N_STEPS, N_STEPS + DELAY_FINAL):
                for delay, stage in stages:
                    if 0 <= step - delay < N_STEPS:
                        stage(step - delay)
            for j in range(N_STEPS - N_ITEMS, N_STEPS):
                drain(j)

    hbm = pl.BlockSpec(memory_space=pl.ANY)

    def layer(i):
        return jnp.where(i < N_ITEMS, DEPTH - 1, 0)

    def item(i):
        return jnp.where(i < N_ITEMS, i, i - N_ITEMS)

    def whole(i):
        return (layer(i), 0, 0)

    def dproj_piece(ch):
        return pl.BlockSpec((None, SEQ, HEAD),
                            lambda i: (layer(i), 0, ch * IN_STEPS + jnp.clip(item(i) - OUT_FIRST, 0, IN_STEPS - 1)))

    def dy_quarter(i):
        return (layer(i), 0, jnp.where(item(i) < OUT_FIRST, item(i), jnp.maximum(item(i) - IN_STEPS, OUT_FIRST)))

    operand = pl.BlockSpec((None, SEQ, D_MODEL), whole, pipeline_mode=pl.Buffered(1))
    in_specs = [operand] + [dproj_piece(ch) for ch in range(N_CHIP)]
    in_specs += [operand, pl.BlockSpec((None, SEQ, OUT_COLS), dy_quarter), hbm, _const_in((DEPTH, 8, D_MODEL))]
    args = [h, dproj, dproj, dproj, dproj, cat, dy, pack, dmod]
    out_shape = [jax.ShapeDtypeStruct((DEPTH, D_MODEL, W_IN_COLS), F32), jax.ShapeDtypeStruct((DEPTH, 2, HALF_OUT, D_MODEL), F32),
                 jax.ShapeDtypeStruct((DEPTH, PK_ROWS, HEAD), F32), jax.ShapeDtypeStruct((DEPTH, 24, N_DEV, HEAD), F32)]
    out_specs = [hbm, hbm, hbm, hbm]
    in_item = lambda *lead: pltpu.VMEM(lead + (HALF_IN, HEAD), BF16)
    out_item = lambda *lead: pltpu.VMEM(lead + (HALF_OUT, OUT_COLS), BF16)
    scratch = [
        pltpu.VMEM((IN_STEPS, HALF_IN, N_CHIP * HEAD), F32), pltpu.VMEM((IN_STEPS, HALF_IN, N_CHIP * HEAD), F32),
        pltpu.VMEM((IN_STEPS, HALF_IN, N_CHIP * HEAD), F32), in_item(IN_STEPS, 3), in_item(IN_STEPS, 2), in_item(IN_STEPS),
        pltpu.VMEM((IN_STEPS, HALF_IN, HEAD), F32),
        pltpu.VMEM((OUT_STEPS, N_CHIP, HALF_OUT, OUT_COLS), F32), pltpu.VMEM((OUT_STEPS, N_CHIP, HALF_OUT, OUT_COLS), F32),
        pltpu.VMEM((OUT_STEPS, N_CHIP, HALF_OUT, OUT_COLS), F32), out_item(OUT_STEPS, 3), out_item(OUT_STEPS, 2),
        out_item(OUT_STEPS), pltpu.VMEM((OUT_STEPS, HALF_OUT, OUT_COLS), F32),
        pltpu.SemaphoreType.DMA((N_STEPS,)), pltpu.SemaphoreType.DMA((N_STEPS,)),
        pltpu.SemaphoreType.DMA((2 * N_STEPS,)), pltpu.SemaphoreType.DMA((2 * N_STEPS,)),
        pltpu.SemaphoreType.DMA((N_STEPS,)), pltpu.SemaphoreType.DMA((N_STEPS,)),
        pltpu.SemaphoreType.DMA((N_STEPS,)), pltpu.SemaphoreType.DMA((N_STEPS,)), pltpu.SemaphoreType.DMA((N_STEPS,)),
    ]
    for _ in range(DEPTH):
        scratch += [
            pltpu.VMEM((PK_HALF, HEAD), F32), pltpu.VMEM((PK_HALF, HEAD), F32), pltpu.VMEM((N_CHIP, PK_PIECE, HEAD), F32),
            pltpu.VMEM((PK_PIECE, HEAD), F32), pltpu.VMEM((PK_ROWS, HEAD), F32),
            pltpu.VMEM((24, 1, HEAD), F32), pltpu.VMEM((24, N_DEV, HEAD), F32),
            pltpu.SemaphoreType.DMA((6,)),
            pltpu.SemaphoreType.DMA((N_CHIP,)), pltpu.SemaphoreType.DMA((N_CHIP,)),
            pltpu.SemaphoreType.DMA((N_CHIP,)), pltpu.SemaphoreType.DMA((N_CHIP,)),
            pltpu.SemaphoreType.DMA((N_DEV - 1,)), pltpu.SemaphoreType.DMA((N_DEV - 1,)),
        ]
    return pl.pallas_call(
        body, name="wgrad", grid=(N_STEPS,), in_specs=in_specs, out_specs=out_specs, out_shape=out_shape,
        scratch_shapes=scratch,
        compiler_params=pltpu.CompilerParams(dimension_semantics=("arbitrary",), vmem_limit_bytes=VMEM_LIMIT),
    )(*args)


def _adamw(w, g, m, v):
    m = ADAM_B1 * m + (1.0 - ADAM_B1) * g
    v = ADAM_B2 * v + (1.0 - ADAM_B2) * (g * g)
    m_hat = m / (1.0 - ADAM_B1 ** ADAM_STEP)
    v_hat = v / (1.0 - ADAM_B2 ** ADAM_STEP)
    delta = -ADAM_LR * (m_hat / (jnp.sqrt(v_hat) + ADAM_EPS) + ADAM_WD * w)
    return delta, m, v


def _adam_sharded(name, w, g, m, v, rows):
    _, r, cols = w.shape

    def body(w_ref, g_ref, m_ref, v_ref, d_out, m_out, v_out):
        d_out[...], m_out[...], v_out[...] = _adamw(w_ref[...], g_ref[...], m_ref[...], v_ref[...])

    blk = pl.BlockSpec((None, rows, cols), lambda l, i: (l, i, 0))
    shape = jax.ShapeDtypeStruct(w.shape, F32)
    return pl.pallas_call(
        body, name=name, grid=(DEPTH, r // rows), in_specs=[blk] * 4, out_specs=[blk] * 3, out_shape=[shape] * 3,
        compiler_params=pltpu.CompilerParams(dimension_semantics=("arbitrary", "arbitrary"), vmem_limit_bytes=VMEM_LIMIT),
    )(w, g, m, v)


def _adam_w_ada(w, m, v, c_all, dmods):
    rows = ADAM_ROWS

    def body(c_ref, dm_ref, w_ref, m_ref, v_ref, g_out, d_out, m_out, v_out):
        l = pl.program_id(0)
        chip = 2 * lax.axis_index("x") + lax.axis_index("y")
        cv = c_ref[...]
        silu_c = (cv * _sigmoid(cv)).astype(BF16).astype(F32)
        pieces = []
        for k in range(W_ADA_COLS // HEAD):
            dk = dm_ref[l, (W_ADA_COLS // HEAD) * chip + k].astype(BF16).astype(F32)
            pieces.append(_dot_exact(silu_c, dk, TN))
        g = jnp.concatenate(pieces, axis=1)
        g_out[...] = g
        d_out[...], m_out[...], v_out[...] = _adamw(w_ref[...], g, m_ref[...], v_ref[...])

    blk = pl.BlockSpec((None, rows, W_ADA_COLS), lambda l, i: (l, i, 0))
    shape = jax.ShapeDtypeStruct(w.shape, F32)
    return pl.pallas_call(
        body, name="adam_w_ada", grid=(DEPTH, D_MODEL // rows),
        in_specs=[pl.BlockSpec((N_DEV, rows), lambda l, i: (0, i)), _const_in((DEPTH, 24, N_DEV, HEAD)), blk, blk, blk],
        out_specs=[blk] * 4, out_shape=[shape] * 4,
        compiler_params=pltpu.CompilerParams(dimension_semantics=("arbitrary", "arbitrary"), vmem_limit_bytes=VMEM_LIMIT),
    )(c_all, dmods, w, m, v)


def _adam_small(packs, dmods, weights, ms, vs):
    n = len(weights)

    def body(*refs):
        pack_refs, dm_refs = refs[0], refs[1]
        b = 2
        w_refs, m_refs, v_refs = refs[b:b + n], refs[b + n:b + 2 * n], refs[b + 2 * n:b + 3 * n]
        outs = refs[b + 3 * n:]
        g_refs, d_refs, nm_refs, nv_refs = outs[0:n], outs[n:2 * n], outs[2 * n:3 * n], outs[3 * n:4 * n]
        outs[4 * n][...] = pack_refs.at[DEPTH - 1][PK_LOSS:PK_LOSS + 1, 0:1] * (0.5 / D_MODEL)

        def lanes(l, row0, count):
            return jnp.concatenate([pack_refs.at[l][row0 + k:row0 + k + 1, :] for k in range(count)], axis=1)

        def update(idx, at, g):
            g_refs[idx][at] = g
            d_refs[idx][at], nm_refs[idx][at], nv_refs[idx][at] = _adamw(w_refs[idx][at], g, m_refs[idx][at], v_refs[idx][at])

        for l in range(DEPTH):
            row = (slice(l, l + 1), slice(None))
            g_b = None
            for d in range(N_DEV):
                part = dm_refs.at[l][:, d, :]
                g_b = part if g_b is None else g_b + part
            update(0, row, jnp.concatenate([g_b[k:k + 1, :] for k in range(24)], axis=1))
            for g in range(N_HEAD):
                update(1, (l, g), pack_refs.at[l][PK_W_POOL + g * HEAD:PK_W_POOL + (g + 1) * HEAD, :])
                update(5, (l, g), pack_refs.at[l][PK_W_SGU + g * HEAD:PK_W_SGU + (g + 1) * HEAD, :])
            update(2, row, lanes(l, PK_POOL_SCALE, N_HEAD))
            update(3, (l,), pack_refs.at[l][PK_SGU_LN_G:PK_SGU_LN_G + N_HEAD, :])
            update(4, (l,), pack_refs.at[l][PK_SGU_LN_B:PK_SGU_LN_B + N_HEAD, :])
            update(6, (l,), pack_refs.at[l][PK_B_SGU:PK_B_SGU + N_HEAD, :])
            update(7, row, lanes(l, PK_LN_G, D_MODEL // HEAD))
            update(8, row, lanes(l, PK_LN_B, D_MODEL // HEAD))

    vmem = pl.BlockSpec(memory_space=pltpu.VMEM)
    shapes = [jax.ShapeDtypeStruct(w.shape, F32) for w in weights]
    return pl.pallas_call(
        body, name="adam_small", in_specs=[vmem] * (2 + 3 * n), out_specs=[vmem] * (4 * n + 1),
        out_shape=shapes * 4 + [jax.ShapeDtypeStruct((1, 1), F32)],
        compiler_params=pltpu.CompilerParams(vmem_limit_bytes=VMEM_LIMIT),
    )(packs, dmods, *weights, *ms, *vs)


def kernel(x, c, w_ada, b_ada, w_in, w_pool, pool_scale, sgu_ln_g, sgu_ln_b, w_sgu, b_sgu, w_out, ln_g, ln_b, loss_target, m_w_ada, m_b_ada, m_w_in, m_w_pool, m_pool_scale, m_sgu_ln_g, m_sgu_ln_b, m_w_sgu, m_b_sgu, m_w_out, m_ln_g, m_ln_b, v_w_ada, v_b_ada, v_w_in, v_w_pool, v_pool_scale, v_sgu_ln_g, v_sgu_ln_b, v_w_sgu, v_b_sgu, v_w_out, v_ln_g, v_ln_b):
    w_in0, w_in1, w_out0, w_out1, mod, c_all = _prepare(c, w_ada, b_ada, w_in, w_out)
    small = (w_pool, pool_scale, sgu_ln_g, sgu_ln_b, w_sgu, jnp.swapaxes(b_sgu, 1, 2))

    proj0, y0, x1, w_in1, w_out1 = _forward_layer(0, x, mod, w_in0, w_out0, small, ln_g, ln_b,
                                                  next_weights=(w_in1, w_out1))
    proj1, y1, dout, sq = _forward_layer(1, x1, mod, w_in1, w_out1, small, ln_g, ln_b, target=loss_target)

    dx1, *shared = _backward_layer(1, dout, x1, y1, proj1, mod, w_in1, w_out1, small, ln_g, sq=sq)
    dx0, h, cat, dy, dproj, pack, dmod = _backward_layer(0, dx1, x, y0, proj0, mod, w_in0, w_out0, small, ln_g, shared=shared)
    g_in, g_out, pack, dmods = _wgrad_reduce(h, dproj, cat, dy, pack, dmod)

    ada = _adam_w_ada(w_ada, m_w_ada, v_w_ada, c_all, dmods)
    g_out = g_out.reshape(DEPTH, W_OUT_ROWS, D_MODEL)
    win = (g_in, *_adam_sharded("adam_w_in", w_in, g_in, m_w_in, v_w_in, ADAM_ROWS))
    wout = (g_out, *_adam_sharded("adam_w_out", w_out, g_out, m_w_out, v_w_out, W_OUT_ROWS))
    small_w = (b_ada, w_pool, pool_scale, sgu_ln_g, sgu_ln_b, w_sgu, b_sgu, ln_g, ln_b)
    small_m = (m_b_ada, m_w_pool, m_pool_scale, m_sgu_ln_g, m_sgu_ln_b, m_w_sgu, m_b_sgu, m_ln_g, m_ln_b)
    small_v = (v_b_ada, v_w_pool, v_pool_scale, v_sgu_ln_g, v_sgu_ln_b, v_w_sgu, v_b_sgu, v_ln_g, v_ln_b)
    res = _adam_small(pack, dmods, small_w, small_m, small_v)
    n = len(small_w)
    loss = res[4 * n].reshape(())

    def ordered(k):
        s = res[k * n:(k + 1) * n]
        return (ada[k], s[0], win[k], s[1], s[2], s[3], s[4], s[5], s[6], wout[k], s[7], s[8])

    return (loss, dx0[None], *ordered(0), *ordered(1), *ordered(2), *ordered(3))
```

```python
import jax
import jax.numpy as jnp
from jax import lax
from jax.experimental import pallas as pl
from jax.experimental.pallas import tpu as pltpu

F32 = jnp.float32
BF16 = jnp.bfloat16
MESH = pl.DeviceIdType.MESH

N_DEV = 8
N_CHIP = 4
DEPTH = 2
SEQ = 2048
D_MODEL = 1024
D_POOL = 512
D_PROJ = 2560
HEAD = 128
N_HEAD = 4
ROWS = 256
N_TILE = SEQ // ROWS
HALO = 16
W_IN_COLS = D_PROJ // N_CHIP
W_OUT_ROWS = D_MODEL // N_CHIP
W_ADA_COLS = 3 * D_MODEL // N_CHIP
HALF_IN = D_MODEL // 2
HALF_OUT = W_OUT_ROWS // 2
DEEPNORM_ALPHA = (2.0 * DEPTH) ** 0.25
LN_EPS = 1e-5
INV_SQRT2 = 0.7071067811865476
INV_SQRT_2PI = 0.3989422804014327

ADAM_LR = 0.001
ADAM_B1 = 0.9
ADAM_B2 = 0.999
ADAM_EPS = 1e-08
ADAM_WD = 0.01
ADAM_STEP = 10
ADAM_PARTS = 2

PK_W_POOL = 0
PK_W_SGU = 512
PK_POOL_SCALE = 1024
PK_SGU_LN_G = 1032
PK_SGU_LN_B = 1040
PK_B_SGU = 1048
PK_LN_G = 1056
PK_LN_B = 1064
PK_LOSS = 1072
PK_ROWS = 1088
PK_HALF = PK_ROWS // 2
PK_PIECE = PK_HALF // N_CHIP

VMEM_LIMIT = 56 * 1024 * 1024

GATHER_SECOND_ROUND_STEP = 1
GATHER_PASS_STEP = 3

NN = (((1,), (0,)), ((), ()))
NT = (((1,), (1,)), ((), ()))
TN = (((0,), (0,)), ((), ()))


def _dot(a, b, dims=NN):
    return lax.dot_general(a, b, dims, preferred_element_type=F32)


def _dot_exact(a, b, dims=NN):
    return lax.dot_general(a, b, dims, preferred_element_type=F32, precision=lax.Precision.HIGHEST)


def _layer_norm(v):
    mu = jnp.mean(v, axis=-1, keepdims=True)
    d = v - mu
    var = jnp.mean(d * d, axis=-1, keepdims=True)
    rstd = lax.rsqrt(var + LN_EPS)
    return d * rstd, rstd


def _layer_norm_bwd(dvhat, vhat, rstd):
    m1 = jnp.mean(dvhat, axis=-1, keepdims=True)
    m2 = jnp.mean(dvhat * vhat, axis=-1, keepdims=True)
    return rstd * (dvhat - m1 - vhat * m2)


def _sigmoid(v):
    return 1.0 / (1.0 + jnp.exp(-v))


def _gelu_parts(v):
    phi = 0.5 * (1.0 + lax.erf(v * INV_SQRT2))
    pdf = INV_SQRT_2PI * jnp.exp(-0.5 * v * v)
    return phi, pdf


def _sum_rows(v):
    return jnp.sum(v, axis=0, keepdims=True)


def _window_sums(ext, toward_later):
    n = ext.shape[0]

    def shifted(v, k):
        return pltpu.roll(v, (n - k) if toward_later else k, 0)

    s2 = ext + shifted(ext, 1)
    r4 = s2[:, HEAD:]
    s4 = r4 + shifted(r4, 2)
    r8 = s4[:, HEAD:]
    s8 = r8 + shifted(r8, 4)
    r16 = s8[:, HEAD:]
    s16 = r16 + shifted(r16, 8)
    return jnp.concatenate([s2[:, :HEAD], s4[:, :HEAD], s8[:, :HEAD], s16], axis=1)


def _window_counts(row0):
    t1 = row0 + 1 + lax.broadcasted_iota(jnp.int32, (ROWS, D_POOL), 0)
    lane = lax.broadcasted_iota(jnp.int32, (ROWS, D_POOL), 1)
    width = jnp.where(lane < HEAD, 2, jnp.where(lane < 2 * HEAD, 4, jnp.where(lane < 3 * HEAD, 8, 16)))
    return jnp.minimum(t1, width).astype(F32)


def _causal_mask():
    r = lax.broadcasted_iota(jnp.int32, (HEAD, HEAD), 0)
    s = lax.broadcasted_iota(jnp.int32, (HEAD, HEAD), 1)
    return r >= s


def _chunks_to_lanes(v):
    return jnp.concatenate([v[n * HEAD:(n + 1) * HEAD] for n in range(ROWS // HEAD)], axis=1)


def _lanes_to_chunks(v):
    return jnp.concatenate([v[:, n * HEAD:(n + 1) * HEAD] for n in range(ROWS // HEAD)], axis=0)


def _mixer(proj, halo, row0, wpool_ref, pscale, sgu_g_ref, sgu_b_ref, wsgu_ref, bsgu_t_ref):
    xa = proj[:, 0:512]
    ga = proj[:, 512:1024]
    u = proj[:, 1024:1536]
    v = proj[:, 1536:2048]
    gb = proj[:, 2048:2560]
    ext = jnp.concatenate([halo, xa], axis=0)
    win = _window_sums(ext, toward_later=False)[HALO:]
    cnt = _window_counts(row0)
    pooled = (win / cnt - xa).astype(BF16)
    pw = jnp.concatenate(
        [_dot(pooled[:, g * HEAD:(g + 1) * HEAD], wpool_ref[g].astype(BF16)) for g in range(N_HEAD)], axis=1)
    sig_a = _sigmoid(ga)
    ya = pw * pscale * (ga * sig_a)
    phi_u, pdf_u = _gelu_parts(u)
    phi_v, pdf_v = _gelu_parts(v)
    gu = u * phi_u
    gv = v * phi_v
    sig_b = _sigmoid(gb)
    silu_b = gb * sig_b
    mask = _causal_mask()
    vhat, rstd_v, vln_l, mixed = [], [], [], []
    for h in range(N_HEAD):
        vh, rh = _layer_norm(gv[:, h * HEAD:(h + 1) * HEAD])
        ln = (vh * sgu_g_ref[h:h + 1, :] + sgu_b_ref[h:h + 1, :]).astype(BF16)
        ln_l = _chunks_to_lanes(ln)
        wm = jnp.where(mask, wsgu_ref[h], 0.0).astype(BF16)
        mx = _lanes_to_chunks(_dot(wm, ln_l) + bsgu_t_ref[:, h:h + 1])
        vhat.append(vh)
        rstd_v.append(rh)
        vln_l.append(ln_l)
        mixed.append(mx)
    mixed = jnp.concatenate(mixed, axis=1)
    yb = gu * mixed * silu_b
    return dict(xa=xa, ga=ga, u=u, gb=gb, cnt=cnt, pooled=pooled, pw=pw, sig_a=sig_a, ya=ya, phi_u=phi_u, pdf_u=pdf_u,
                phi_v=phi_v, pdf_v=pdf_v, gu=gu, sig_b=sig_b, silu_b=silu_b, vhat=vhat, rstd_v=rstd_v, vln_l=vln_l,
                mixed=mixed, yb=yb, mask=mask)


def _const(shape, *index):
    lead = tuple(index) + (0,) * (len(shape) - len(index))
    return pl.BlockSpec(shape, lambda *_: lead)


def _const_in(shape, *index):
    lead = tuple(index) + (0,) * (len(shape) - len(index))
    return pl.BlockSpec(shape, lambda *_: lead, pipeline_mode=pl.Buffered(1))


def _layer_weight_specs(l):
    return [
        _const_in((None, N_HEAD, HEAD, HEAD), l),
        _const_in((DEPTH, D_POOL)),
        _const_in((None, N_HEAD, HEAD), l),
        _const_in((None, N_HEAD, HEAD), l),
        _const_in((None, N_HEAD, HEAD, HEAD), l),
        _const_in((None, HEAD, N_HEAD), l),
    ]


def _forward_layer(l, x, mod, w_in, w_out, small, ln_g, ln_b, target=None, next_weights=None):
    last = target is not None
    gathers = next_weights is not None

    def body(*refs):
        if last:
            (x_ref, mod_ref, win_ref, wout_ref, wpool_ref, pscale_ref, sgu_g_ref, sgu_b_ref, wsgu_ref, bsgu_t_ref,
             lng_ref, lnb_ref, tgt_ref, proj_ref, y_ref, out_ref, loss_ref, carry_ref) = refs
        elif gathers:
            (x_ref, mod_ref, win_ref, wout_ref, wpool_ref, pscale_ref, sgu_g_ref, sgu_b_ref, wsgu_ref, bsgu_t_ref,
             lng_ref, lnb_ref, _, _, proj_ref, y_ref, out_ref, next_in, next_out, carry_ref, g_send, g_recv) = refs
            gather = _WeightGather(_Place(), next_in, next_out, g_send, g_recv)
        else:
            (x_ref, mod_ref, win_ref, wout_ref, wpool_ref, pscale_ref, sgu_g_ref, sgu_b_ref, wsgu_ref, bsgu_t_ref,
             lng_ref, lnb_ref, proj_ref, y_ref, out_ref, carry_ref) = refs
        i = pl.program_id(0)

        @pl.when(i == 0)
        def _():
            carry_ref[...] = jnp.zeros_like(carry_ref)
            if last:
                loss_ref[...] = jnp.zeros_like(loss_ref)
            if gathers:
                gather.start_first_round()

        if gathers:
            for q in range(_WeightGather.CHUNKS):
                @pl.when(i == GATHER_SECOND_ROUND_STEP + q)
                def _(q=q):
                    gather.start_second_round(q)

                @pl.when(i == GATHER_PASS_STEP + q)
                def _(q=q):
                    gather.pass_second_round(q)

        x = x_ref[...]
        shift, scale, gate = mod_ref[0:1, :], mod_ref[1:2, :], mod_ref[2:3, :]
        xn, _ = _layer_norm(x)
        h = xn * (1.0 + scale) + shift
        proj = _dot(h.astype(BF16), win_ref[...])
        proj_ref[...] = proj
        m = _mixer(proj, carry_ref[...], i * ROWS, wpool_ref, pscale_ref[l:l + 1, :], sgu_g_ref, sgu_b_ref, wsgu_ref,
                   bsgu_t_ref)
        carry_ref[...] = m["xa"][ROWS - HALO:]
        cat = jnp.concatenate([m["ya"], m["yb"]], axis=1).astype(BF16)
        y = _dot(cat, wout_ref[...])
        y_ref[...] = y
        zn, _ = _layer_norm(DEEPNORM_ALPHA * x + gate * y)
        out = zn * lng_ref[l:l + 1, :] + lnb_ref[l:l + 1, :]
        if last:
            err = out - tgt_ref[...]
            out_ref[...] = err * (1.0 / D_MODEL)
            loss_ref[...] += jnp.sum(err * err)
        else:
            out_ref[...] = out

        if gathers:
            @pl.when(i == N_TILE - 1)
            def _():
                gather.finish()

    tile = pl.BlockSpec((ROWS, D_MODEL), lambda i: (i, 0))
    tile3 = pl.BlockSpec((None, ROWS, D_MODEL), lambda i: (0, i, 0))
    in_specs = [tile3 if x.ndim == 3 else tile, _const_in((None, 8, D_MODEL), l), _const_in((D_MODEL, D_PROJ)),
                _const_in((D_MODEL, D_MODEL))]
    in_specs += _layer_weight_specs(l) + [_const_in((DEPTH, D_MODEL)), _const_in((DEPTH, D_MODEL))]
    out_shape = [jax.ShapeDtypeStruct((SEQ, D_PROJ), F32), jax.ShapeDtypeStruct((SEQ, D_MODEL), F32),
                 jax.ShapeDtypeStruct((SEQ, D_MODEL), F32)]
    out_specs = [pl.BlockSpec((ROWS, D_PROJ), lambda i: (i, 0)), tile, tile]
    args = [x, mod, w_in, w_out, *small, ln_g, ln_b]
    scratch = [pltpu.VMEM((HALO, D_POOL), F32)]
    aliases = {}
    if last:
        in_specs.append(tile3)
        args.append(target)
        out_shape.append(jax.ShapeDtypeStruct((8, HEAD), F32))
        out_specs.append(_const((8, HEAD)))
    if gathers:
        hbm = pl.BlockSpec(memory_space=pl.ANY)
        aliases = {len(args): len(out_shape), len(args) + 1: len(out_shape) + 1}
        in_specs += [hbm, hbm]
        args += list(next_weights)
        out_shape += [jax.ShapeDtypeStruct(w.shape, BF16) for w in next_weights]
        out_specs += [hbm, hbm]
        scratch += [pltpu.SemaphoreType.DMA((_WeightGather.N_SEMS,)), pltpu.SemaphoreType.DMA((_WeightGather.N_SEMS,))]
    return pl.pallas_call(
        body, name="fwd_last" if last else "fwd_first", grid=(N_TILE,), in_specs=in_specs, out_specs=out_specs,
        out_shape=out_shape, scratch_shapes=scratch, input_output_aliases=aliases,
        compiler_params=pltpu.CompilerParams(dimension_semantics=("arbitrary",), vmem_limit_bytes=VMEM_LIMIT),
    )(*args)


def _backward_layer(l, dout, x, y, proj, mod, w_in, w_out, small, ln_g, sq=None, shared=None):
    has_loss = sq is not None

    def body(*refs):
        (dout_ref, x_ref, y_ref, proj_ref, halo_ref, mod_ref, win_ref, wout_ref, wpool_ref, pscale_ref,
         sgu_g_ref, sgu_b_ref, wsgu_ref, bsgu_t_ref, lng_ref) = refs[:15]
        n_in = 16 if has_loss else 15 + 6
        dx_ref, h_ref, cat_ref, dy_ref, dproj_ref, pack_ref, dmod_ref, carry_ref = refs[n_in:n_in + 8]
        i = pl.program_id(0)
        tile = N_TILE - 1 - i

        @pl.when(i == 0)
        def _():
            carry_ref[...] = jnp.zeros_like(carry_ref)
            pack_ref[...] = jnp.zeros_like(pack_ref)
            dmod_ref[...] = jnp.zeros_like(dmod_ref)
            if has_loss:
                pack_ref[PK_LOSS:PK_LOSS + 8, :] = refs[15][...]

        x = x_ref[...]
        y = y_ref[...]
        dout = dout_ref[...]
        pscale = pscale_ref[l:l + 1, :]
        shift, scale, gate = mod_ref[0:1, :], mod_ref[1:2, :], mod_ref[2:3, :]
        xn, rstd_x = _layer_norm(x)
        h = xn * (1.0 + scale) + shift
        h_ref[...] = h.astype(BF16)
        zn, rstd_z = _layer_norm(DEEPNORM_ALPHA * x + gate * y)
        g_ln_g = _sum_rows(dout * zn)
        g_ln_b = _sum_rows(dout)
        dz = _layer_norm_bwd(dout * lng_ref[l:l + 1, :], zn, rstd_z)
        d_gate = _sum_rows(dz * y)
        dy = (gate * dz).astype(BF16)
        dy_ref[...] = dy

        halo = jnp.where(tile > 0, halo_ref[...], 0.0)
        m = _mixer(proj_ref[...], halo, tile * ROWS, wpool_ref, pscale, sgu_g_ref, sgu_b_ref, wsgu_ref, bsgu_t_ref)
        cat_ref[...] = jnp.concatenate([m["ya"], m["yb"]], axis=1).astype(BF16)
        dcat = _dot(dy, wout_ref[...], NT)
        dya = dcat[:, :D_POOL]
        dyb = dcat[:, D_POOL:]

        ga, sig_a = m["ga"], m["sig_a"]
        dp = dya * (ga * sig_a)
        d_ga = dya * (m["pw"] * pscale) * (sig_a * (1.0 + ga * (1.0 - sig_a)))
        g_pscale = _sum_rows(dp * m["pw"])
        dpw = (dp * pscale).astype(BF16)
        dpooled = []
        for g in range(N_HEAD):
            cols = slice(g * HEAD, (g + 1) * HEAD)
            pack_ref[PK_W_POOL + g * HEAD:PK_W_POOL + (g + 1) * HEAD, :] += _dot(m["pooled"][:, cols], dpw[:, cols], TN)
            dpooled.append(_dot(dpw[:, cols], wpool_ref[g].astype(BF16), NT))
        dpooled = jnp.concatenate(dpooled, axis=1)
        q = dpooled / m["cnt"]
        ext = jnp.concatenate([q, carry_ref[...]], axis=0)
        d_xa = _window_sums(ext, toward_later=True)[:ROWS] - dpooled
        carry_ref[...] = q[:HALO]

        gu, mixed, silu_b, gb, sig_b = m["gu"], m["mixed"], m["silu_b"], m["gb"], m["sig_b"]
        d_mixed = dyb * gu * silu_b
        d_gu = dyb * mixed * silu_b
        d_gb = dyb * gu * mixed * (sig_b * (1.0 + gb * (1.0 - sig_b)))
        d_u = d_gu * (m["phi_u"] + m["u"] * m["pdf_u"])
        ones = jnp.ones((8, HEAD), F32)
        d_v = []
        for hd in range(N_HEAD):
            cols = slice(hd * HEAD, (hd + 1) * HEAD)
            dm = d_mixed[:, cols]
            dm_l = _chunks_to_lanes(dm.astype(BF16))
            g_w = _dot(dm_l, m["vln_l"][hd], NT)
            pack_ref[PK_W_SGU + hd * HEAD:PK_W_SGU + (hd + 1) * HEAD, :] += jnp.where(m["mask"], g_w, 0.0)
            dm_sum = dm[0:HEAD]
            for n in range(1, ROWS // HEAD):
                dm_sum = dm_sum + dm[n * HEAD:(n + 1) * HEAD]
            pack_ref[PK_B_SGU + hd:PK_B_SGU + hd + 1, :] += _dot_exact(ones, dm_sum, NT)[0:1]
            wm = jnp.where(m["mask"], wsgu_ref[hd], 0.0).astype(BF16)
            d_vln = _lanes_to_chunks(_dot(wm, dm_l, TN))
            vhat = m["vhat"][hd]
            pack_ref[PK_SGU_LN_G + hd:PK_SGU_LN_G + hd + 1, :] += _sum_rows(d_vln * vhat)
            pack_ref[PK_SGU_LN_B + hd:PK_SGU_LN_B + hd + 1, :] += _sum_rows(d_vln)
            d_v.append(_layer_norm_bwd(d_vln * sgu_g_ref[hd:hd + 1, :], vhat, m["rstd_v"][hd]))
        v = proj_ref[:, 1536:2048]
        d_v = jnp.concatenate(d_v, axis=1) * (m["phi_v"] + v * m["pdf_v"])

        dproj = jnp.concatenate([d_xa, d_ga, d_u, d_v, d_gb], axis=1).astype(BF16)
        dproj_ref[...] = dproj
        dh = _dot(dproj, win_ref[...], NT)
        d_scale = _sum_rows(dh * xn)
        d_shift = _sum_rows(dh)
        dx_ref[...] = DEEPNORM_ALPHA * dz + _layer_norm_bwd(dh * (1.0 + scale), xn, rstd_x)

        dmod_ref[0:1, :] += d_shift
        dmod_ref[1:2, :] += d_scale
        dmod_ref[2:3, :] += d_gate
        for g in range(N_HEAD):
            pack_ref[PK_POOL_SCALE + g:PK_POOL_SCALE + g + 1, :] += g_pscale[:, g * HEAD:(g + 1) * HEAD]
        for k in range(D_MODEL // HEAD):
            pack_ref[PK_LN_G + k:PK_LN_G + k + 1, :] += g_ln_g[:, k * HEAD:(k + 1) * HEAD]
            pack_ref[PK_LN_B + k:PK_LN_B + k + 1, :] += g_ln_b[:, k * HEAD:(k + 1) * HEAD]

    def rev(i):
        return (N_TILE - 1 - i, 0)

    tile = pl.BlockSpec((ROWS, D_MODEL), rev)
    tile3 = pl.BlockSpec((None, ROWS, D_MODEL), lambda i: (0, N_TILE - 1 - i, 0))
    halo = pl.BlockSpec((HALO, D_POOL), lambda i: (jnp.maximum((N_TILE - 1 - i) * (ROWS // HALO) - 1, 0), 0))
    in_specs = [tile, tile3 if x.ndim == 3 else tile, tile, pl.BlockSpec((ROWS, D_PROJ), rev), halo,
                _const_in((None, 8, D_MODEL), l), _const_in((D_MODEL, D_PROJ)), _const_in((D_MODEL, D_MODEL))]
    in_specs += _layer_weight_specs(l) + [_const_in((DEPTH, D_MODEL))]
    args = [dout, x, y, proj, proj, mod, w_in, w_out, *small, ln_g]
    stacked = lambda cols: pl.BlockSpec((None, ROWS, cols), lambda i: (l, N_TILE - 1 - i, 0))
    out_shape = [jax.ShapeDtypeStruct((SEQ, D_MODEL), F32), jax.ShapeDtypeStruct((DEPTH, SEQ, D_MODEL), BF16),
                 jax.ShapeDtypeStruct((DEPTH, SEQ, D_MODEL), BF16), jax.ShapeDtypeStruct((DEPTH, SEQ, D_MODEL), BF16),
                 jax.ShapeDtypeStruct((DEPTH, SEQ, D_PROJ), BF16), jax.ShapeDtypeStruct((DEPTH, PK_ROWS, HEAD), F32),
                 jax.ShapeDtypeStruct((DEPTH, 8, D_MODEL), F32)]
    out_specs = [tile, stacked(D_MODEL), stacked(D_MODEL), stacked(D_MODEL), stacked(D_PROJ),
                 _const((None, PK_ROWS, HEAD), l), _const((None, 8, D_MODEL), l)]
    aliases = {}
    if has_loss:
        in_specs.append(_const_in((8, HEAD)))
        args.append(sq)
    else:
        aliases = {len(args) + k: 1 + k for k in range(len(shared))}
        in_specs += [pl.BlockSpec(memory_space=pl.ANY)] * len(shared)
        args += list(shared)
    return pl.pallas_call(
        body, name="bwd_last" if has_loss else "bwd_first", grid=(N_TILE,), in_specs=in_specs, out_specs=out_specs,
        out_shape=out_shape, scratch_shapes=[pltpu.VMEM((HALO, D_POOL), F32)], input_output_aliases=aliases,
        compiler_params=pltpu.CompilerParams(dimension_semantics=("arbitrary",), vmem_limit_bytes=VMEM_LIMIT),
    )(*args)


def _flip(v, f):
    return v + f - 2 * v * f


class _Place:
    def __init__(self):
        x, y, c = lax.axis_index("x"), lax.axis_index("y"), lax.axis_index("c")
        self.x, self.y, self.c = x, y, c
        self.chip = 2 * x + y
        self.dev = 4 * x + 2 * y + c
        self.sibling = (x, y, 1 - c)
        x1, y1 = _flip(x, 1 - c), _flip(y, c)
        x2, y2 = _flip(x, c), _flip(y, 1 - c)
        self.first = (x1, y1, c)
        self.second = (x2, y2, c)
        self.chip_first = 2 * x1 + y1
        self.chip_second = 2 * x2 + y2
        self.chip_far = 2 * (1 - x) + (1 - y)
        self.my_first_coord = jnp.where(c == 0, x, y)

    def first_coord(self, ch):
        return jnp.where(self.c == 0, ch // 2, ch % 2)

    def others(self):
        return [(_flip(self.x, (r >> 2) & 1), _flip(self.y, (r >> 1) & 1), _flip(self.c, r & 1)) for r in range(1, N_DEV)]

    def other_chips(self):
        return [(1 - self.x, self.y), (self.x, 1 - self.y), (1 - self.x, 1 - self.y)]


class _WeightGather:
    CHUNKS = 4
    N_SEMS = 12 * CHUNKS

    def __init__(self, place, win, wout, send, recv):
        self.p, self.win, self.wout, self.send, self.recv = place, win, wout, send, recv
        p = place
        self.plan = [(p.chip, p.first), (p.chip, p.second), (p.chip_first, p.second),
                     (p.chip_first, p.sibling), (p.chip_second, p.sibling), (p.chip_far, p.sibling)]

    def _copies(self, k, q):
        ch, target = self.plan[k]
        n_in, n_out = HALF_IN // self.CHUNKS, HALF_OUT // self.CHUNKS
        rows_in = pl.ds(pl.multiple_of(self.p.c * HALF_IN + q * n_in, n_in), n_in)
        cols_in = pl.ds(pl.multiple_of(ch * W_IN_COLS, 128), W_IN_COLS)
        rows_out = pl.ds(pl.multiple_of(ch * W_OUT_ROWS + self.p.c * HALF_OUT + q * n_out, n_out), n_out)
        r_in = self.win.at[rows_in, cols_in]
        r_out = self.wout.at[rows_out, :]
        s = 2 * (6 * q + k)
        return [pltpu.make_async_remote_copy(r_in, r_in, self.send.at[s], self.recv.at[s],
                                             device_id=target, device_id_type=MESH),
                pltpu.make_async_remote_copy(r_out, r_out, self.send.at[s + 1], self.recv.at[s + 1],
                                             device_id=target, device_id_type=MESH)]

    def _start(self, k, q):
        for cp in self._copies(k, q):
            cp.start()

    def _landed(self, k, q):
        for cp in self._copies(k, q):
            cp.wait_recv()

    def start_first_round(self):
        for q in range(self.CHUNKS):
            self._start(0, q)

    def start_second_round(self, q):
        self._landed(0, q)
        self._start(1, q)
        self._start(2, q)
        self._start(3, q)

    def pass_second_round(self, q):
        self._landed(1, q)
        self._start(4, q)
        self._landed(2, q)
        self._start(5, q)

    def finish(self):
        for q in range(self.CHUNKS):
            for k in (3, 4, 5):
                self._landed(k, q)
        for q in range(self.CHUNKS):
            for k in range(len(self.plan)):
                for cp in self._copies(k, q):
                    cp.wait_send()


def _prepare(c_vec, w_ada, b_ada, w_in, w_out):
    def body(c_ref, wada_hbm, bada_ref, win_hbm, wout_hbm,
             win0, win1, wout0, wout1, mod_ref, c_all,
             wada_ref, win_ref, wout_ref, win_bf, wout_bf, mod_mine, mod_all,
             g_send, g_recv, c_send, c_recv, mod_send, mod_recv, local_sem):
        p = _Place()
        loads = [pltpu.make_async_copy(win_hbm.at[0], win_ref.at[0], local_sem.at[4]),
                 pltpu.make_async_copy(wout_hbm.at[0], wout_ref.at[0], local_sem.at[5]),
                 pltpu.make_async_copy(win_hbm.at[1], win_ref.at[1], local_sem.at[6]),
                 pltpu.make_async_copy(wout_hbm.at[1], wout_ref.at[1], local_sem.at[7]),
                 pltpu.make_async_copy(wada_hbm, wada_ref, local_sem.at[8])]
        for cp in loads:
            cp.start()

        c_all[pl.ds(p.dev, 1), :] = c_ref[...]
        c_copies = [pltpu.make_async_remote_copy(c_ref, c_all.at[pl.ds(p.dev, 1), :], c_send.at[r], c_recv.at[r],
                                                 device_id=d, device_id_type=MESH) for r, d in enumerate(p.others())]
        for cp in c_copies:
            cp.start()

        cols = pl.ds(pl.multiple_of(p.chip * W_IN_COLS, 128), W_IN_COLS)
        rows = pl.ds(pl.multiple_of(p.chip * W_OUT_ROWS, W_OUT_ROWS), W_OUT_ROWS)
        own = [pltpu.make_async_copy(win_bf.at[0], win0.at[:, cols], local_sem.at[0]),
               pltpu.make_async_copy(wout_bf.at[0], wout0.at[rows, :], local_sem.at[1]),
               pltpu.make_async_copy(win_bf.at[1], win1.at[:, cols], local_sem.at[2]),
               pltpu.make_async_copy(wout_bf.at[1], wout1.at[rows, :], local_sem.at[3])]
        for l in range(DEPTH):
            loads[2 * l].wait()
            win_bf[l] = win_ref[l].astype(BF16)
            own[2 * l].start()
            loads[2 * l + 1].wait()
            wout_bf[l] = wout_ref[l].astype(BF16)
            own[2 * l + 1].start()
            if l == 0:
                own[0].wait()
                own[1].wait()
                gather = _WeightGather(p, win0, wout0, g_send, g_recv)
                gather.start_first_round()
        for cp in c_copies:
            cp.wait()
        loads[4].wait()

        cv = c_all[...]
        silu_c = (cv * _sigmoid(cv)).astype(BF16)
        for l in range(DEPTH):
            mod_mine[l] = _dot(silu_c, wada_ref[l].astype(BF16))
        mod_all[p.chip] = mod_mine[...]
        m_copies = [pltpu.make_async_remote_copy(mod_mine, mod_all.at[p.chip], mod_send.at[k], mod_recv.at[k],
                                                 device_id=(px, py, p.c), device_id_type=MESH)
                    for k, (px, py) in enumerate(p.other_chips())]
        for cp in m_copies:
            cp.start()
        for q in range(gather.CHUNKS):
            gather.start_second_round(q)
        for cp in m_copies:
            cp.wait()
        mod_ref[...] = jnp.zeros_like(mod_ref)
        for l in range(DEPTH):
            full = jnp.concatenate([mod_all[ch, l, pl.ds(p.dev, 1), :] for ch in range(N_CHIP)], axis=1) + bada_ref[l:l + 1, :]
            for k in range(3):
                mod_ref[l, k:k + 1, :] = full[:, k * D_MODEL:(k + 1) * D_MODEL]
        for q in range(gather.CHUNKS):
            gather.pass_second_round(q)
        gather.finish()
        own[2].wait()
        own[3].wait()

    vmem = pl.BlockSpec(memory_space=pltpu.VMEM)
    hbm = pl.BlockSpec(memory_space=pl.ANY)
    w_in_shape = jax.ShapeDtypeStruct((D_MODEL, D_PROJ), BF16)
    w_out_shape = jax.ShapeDtypeStruct((D_MODEL, D_MODEL), BF16)
    return pl.pallas_call(
        body, name="prepare",
        in_specs=[vmem, hbm, vmem, hbm, hbm],
        out_specs=[hbm, hbm, hbm, hbm, vmem, vmem],
        out_shape=[w_in_shape, w_in_shape, w_out_shape, w_out_shape,
                   jax.ShapeDtypeStruct((DEPTH, 8, D_MODEL), F32), jax.ShapeDtypeStruct((N_DEV, D_MODEL), F32)],
        scratch_shapes=[
            pltpu.VMEM(w_ada.shape, F32), pltpu.VMEM(w_in.shape, F32), pltpu.VMEM(w_out.shape, F32),
            pltpu.VMEM((DEPTH, D_MODEL, W_IN_COLS), BF16), pltpu.VMEM((DEPTH, W_OUT_ROWS, D_MODEL), BF16),
            pltpu.VMEM((DEPTH, N_DEV, W_ADA_COLS), F32), pltpu.VMEM((N_CHIP, DEPTH, N_DEV, W_ADA_COLS), F32),
            pltpu.SemaphoreType.DMA((_WeightGather.N_SEMS,)), pltpu.SemaphoreType.DMA((_WeightGather.N_SEMS,)),
            pltpu.SemaphoreType.DMA((7,)), pltpu.SemaphoreType.DMA((7,)),
            pltpu.SemaphoreType.DMA((3,)), pltpu.SemaphoreType.DMA((3,)),
            pltpu.SemaphoreType.DMA((9,)),
        ],
        compiler_params=pltpu.CompilerParams(vmem_limit_bytes=VMEM_LIMIT),
    )(c_vec, w_ada, b_ada, w_in, w_out)


IN_STEPS = W_IN_COLS // HEAD
OUT_STEPS = 4
OUT_COLS = D_MODEL // OUT_STEPS
OUT_FIRST = 2
ITEMS = ([("out", k) for k in range(OUT_FIRST)] + [("in", k) for k in range(IN_STEPS)]
         + [("out", k) for k in range(OUT_FIRST, OUT_STEPS)])
N_ITEMS = len(ITEMS)
N_STEPS = DEPTH * N_ITEMS
DELAY_SUM, DELAY_SECOND, DELAY_FINAL = 1, 3, 5
SMALL_SCATTER_STEP, SMALL_GATHER_STEP, SMALL_PASS_STEP, SMALL_FINISH_STEP = 1, 3, 5, 7


def _wgrad_reduce(h, dproj, cat, dy, pack, dmod):
    def body(*refs):
        h_ref, dp_refs, cat_ref, dy_ref, pack_ref, dmod_ref = refs[0], refs[1:5], refs[5], refs[6], refs[7], refs[8]
        fin_in, fin_out, pack_out, dmod_out = refs[9:13]
        scratch = refs[13:]
        (mine_in, send_in, sib_in, st_in, r1_in, r2_in, f_in,
         mine_out, send_out, sib_out, st_out, r1_out, r2_out, f_out,
         d2d_s, d2d_r, r1_s, r1_r, r2_s, r2_r, fin_l, fin_s, fin_r) = scratch[:23]
        p = _Place()
        c = p.c
        i = pl.program_id(0)
        my_rows = pl.ds(pl.multiple_of(c * HALF_IN, HALF_IN), HALF_IN)

        def layer_of(j):
            return DEPTH - 1 - j // N_ITEMS

        def bufs(j):
            kind, k = ITEMS[j % N_ITEMS]
            if kind == "in":
                return [r.at[k] for r in (mine_in, send_in, sib_in, st_in, r1_in, r2_in, f_in)]
            return [r.at[k] for r in (mine_out, send_out, sib_out, st_out, r1_out, r2_out, f_out)]

        def piece(j, ref, ch):
            if ITEMS[j % N_ITEMS][0] == "in":
                return ref.at[:, ch * HEAD:(ch + 1) * HEAD]
            return ref.at[ch]

        def slot(ch):
            return jnp.where(c == 0, ch % 2, ch // 2)

        def to_sibling(j):
            _, send, sib, _, _, _, _ = bufs(j)
            return pltpu.make_async_remote_copy(send, sib, d2d_s.at[j], d2d_r.at[j], device_id=p.sibling, device_id_type=MESH)

        def first_round(j, ch):
            _, _, _, st, r1, _, _ = bufs(j)
            k = slot(ch)
            return pltpu.make_async_remote_copy(st.at[k], r1.at[k], r1_s.at[2 * j + k], r1_r.at[2 * j + k],
                                                device_id=p.first, device_id_type=MESH)

        def second_round(j):
            _, _, _, st, _, r2, _ = bufs(j)
            return pltpu.make_async_remote_copy(st.at[2], r2, r2_s.at[j], r2_r.at[j], device_id=p.second, device_id_type=MESH)

        def finals(j):
            f = bufs(j)[6]
            kind, k = ITEMS[j % N_ITEMS]
            if kind == "in":
                dst = fin_in.at[layer_of(j), my_rows, k * HEAD:(k + 1) * HEAD]
            else:
                dst = fin_out.at[layer_of(j), c, :, k * OUT_COLS:(k + 1) * OUT_COLS]
            return [pltpu.make_async_copy(f, dst, fin_l.at[j]),
                    pltpu.make_async_remote_copy(f, dst, fin_s.at[j], fin_r.at[j], device_id=p.sibling, device_id_type=MESH)]

        def stage_sum(j):
            mine, _, sib, st, _, _, _ = bufs(j)
            to_sibling(j).wait_recv()
            mine[...] = mine[...] + sib[...]
            for ch in range(N_CHIP):
                @pl.when(p.first_coord(ch) != p.my_first_coord)
                def _(ch=ch):
                    st[slot(ch)] = piece(j, mine, ch)[...].astype(BF16)
                    first_round(j, ch).start()

        def stage_second(j):
            mine, _, _, st, r1, _, _ = bufs(j)
            for ch in range(N_CHIP):
                @pl.when(p.first_coord(ch) == p.my_first_coord)
                def _(ch=ch):
                    first_round(j, ch).wait_recv()
                    part = piece(j, mine, ch)
                    total = part[...] + r1[slot(ch)].astype(F32)
                    part[...] = total

                    @pl.when(ch != p.chip)
                    def _():
                        st[2] = total.astype(BF16)
                        second_round(j).start()

        def stage_final(j):
            mine, _, _, _, _, r2, f = bufs(j)
            second_round(j).wait_recv()
            for ch in range(N_CHIP):
                @pl.when(ch == p.chip)
                def _(ch=ch):
                    f[...] = piece(j, mine, ch)[...] + r2[...].astype(F32)
            for cp in finals(j):
                cp.start()

        def drain(j):
            to_sibling(j).wait_send()
            for ch in range(N_CHIP):
                @pl.when(p.first_coord(ch) != p.my_first_coord)
                def _(ch=ch):
                    first_round(j, ch).wait_send()

                @pl.when(jnp.logical_and(p.first_coord(ch) == p.my_first_coord, ch != p.chip))
                def _():
                    second_round(j).wait_send()
            for cp in finals(j):
                cp.wait()

        dev = p.dev
        devices = p.others()

        def half(core):
            return pl.ds(pl.multiple_of(core * PK_HALF, 8), PK_HALF)

        def finished(core, ch):
            return pl.ds(pl.multiple_of(core * PK_HALF + ch * PK_PIECE, 8), PK_PIECE)

        def small_exchange(l, first_step, bufs_l):
            (pk_mine, pk_sib, pk_rs, pk_fin, pk_all, dm_st, dm_all, pk_sem, rs_s, rs_r, ag_s, ag_r, dm_s, dm_r) = bufs_l

            def pk_load():
                return pltpu.make_async_copy(pack_ref.at[l, half(c)], pk_mine, pk_sem.at[0])

            def pk_give():
                return pltpu.make_async_remote_copy(pack_ref.at[l, half(1 - c)], pk_sib, pk_sem.at[1], pk_sem.at[2],
                                                    device_id=p.sibling, device_id_type=MESH)

            def pk_scatter(ch):
                return pltpu.make_async_remote_copy(pk_mine.at[ch * PK_PIECE:(ch + 1) * PK_PIECE], pk_rs.at[p.chip],
                                                    rs_s.at[ch], rs_r.at[p.chip], device_id=(ch // 2, ch % 2, c),
                                                    device_id_type=MESH)

            def pk_spread(ch):
                return pltpu.make_async_remote_copy(pk_fin, pk_all.at[finished(c, p.chip)], ag_s.at[ch], ag_r.at[p.chip],
                                                    device_id=(ch // 2, ch % 2, c), device_id_type=MESH)

            def pk_pass():
                return pltpu.make_async_remote_copy(pk_all.at[half(c)], pk_all.at[half(c)], pk_sem.at[3], pk_sem.at[4],
                                                    device_id=p.sibling, device_id_type=MESH)

            def dm_copy(r):
                return pltpu.make_async_remote_copy(dm_st, dm_all.at[:, pl.ds(dev, 1), :], dm_s.at[r], dm_r.at[r],
                                                    device_id=devices[r], device_id_type=MESH)

            def results():
                return [pltpu.make_async_copy(pk_all, pack_out.at[l], pk_sem.at[0]),
                        pltpu.make_async_copy(dm_all, dmod_out.at[l], pk_sem.at[5])]

            @pl.when(i == first_step)
            def _():
                pk_load().start()
                pk_give().start()
                for k in range(3):
                    for r in range(D_MODEL // HEAD):
                        dm_st[8 * k + r] = dmod_ref[l, k:k + 1, r * HEAD:(r + 1) * HEAD]
                dm_all[:, pl.ds(dev, 1), :] = dm_st[...]
                for r in range(N_DEV - 1):
                    dm_copy(r).start()

            @pl.when(i == first_step + SMALL_SCATTER_STEP)
            def _():
                pk_load().wait()
                pk_give().wait()
                pk_mine[...] = pk_mine[...] + pk_sib[...]
                for ch in range(N_CHIP):
                    @pl.when(ch != p.chip)
                    def _(ch=ch):
                        pk_scatter(ch).start()

            @pl.when(i == first_step + SMALL_GATHER_STEP)
            def _():
                for ch in range(N_CHIP):
                    @pl.when(ch != p.chip)
                    def _(ch=ch):
                        pltpu.make_async_remote_copy(pk_fin, pk_rs.at[ch], rs_s.at[ch], rs_r.at[ch],
                                                     device_id=p.sibling, device_id_type=MESH).wait_recv()
                for me in range(N_CHIP):
                    @pl.when(me == p.chip)
                    def _(me=me):
                        total = None
                        for ch in range(N_CHIP):
                            part = pk_mine[me * PK_PIECE:(me + 1) * PK_PIECE] if ch == me else pk_rs[ch]
                            total = part if total is None else total + part
                        pk_fin[...] = total
                        pk_all[finished(c, me)] = total
                for ch in range(N_CHIP):
                    @pl.when(ch != p.chip)
                    def _(ch=ch):
                        pk_spread(ch).start()

            @pl.when(i == first_step + SMALL_PASS_STEP)
            def _():
                for ch in range(N_CHIP):
                    @pl.when(ch != p.chip)
                    def _(ch=ch):
                        pltpu.make_async_remote_copy(pk_fin, pk_all.at[finished(c, ch)], ag_s.at[ch], ag_r.at[ch],
                                                     device_id=p.sibling, device_id_type=MESH).wait_recv()
                pk_pass().start()

            @pl.when(i == first_step + SMALL_FINISH_STEP)
            def _():
                pk_pass().wait()
                for ch in range(N_CHIP):
                    @pl.when(ch != p.chip)
                    def _(ch=ch):
                        pk_scatter(ch).wait_send()
                        pk_spread(ch).wait_send()
                for r in range(N_DEV - 1):
                    dm_copy(r).wait()
                for cp in results():
                    cp.start()
                for cp in results():
                    cp.wait()

        n_small = 14
        for l in range(DEPTH):
            small_exchange(l, (DEPTH - 1 - l) * N_ITEMS, scratch[23 + n_small * l:23 + n_small * (l + 1)])

        for step in range(N_ITEMS, N_STEPS):
            @pl.when(i == step)
            def _(step=step):
                drain(step - N_ITEMS)

        ii = jnp.where(i < N_ITEMS, i, i - N_ITEMS)
        in_step = jnp.logical_and(ii >= OUT_FIRST, ii < OUT_FIRST + IN_STEPS)

        @pl.when(in_step)
        def _():
            k = ii - OUT_FIRST
            rhs = jnp.concatenate([r[...] for r in dp_refs], axis=1)
            res = _dot(h_ref[...], rhs, TN)

            @pl.when(c == 0)
            def _():
                mine_in[k] = res[:HALF_IN]
                send_in[k] = res[HALF_IN:]

            @pl.when(c == 1)
            def _():
                mine_in[k] = res[HALF_IN:]
                send_in[k] = res[:HALF_IN]

        @pl.when(jnp.logical_not(in_step))
        def _():
            k = jnp.where(ii < OUT_FIRST, ii, ii - IN_STEPS)
            res = _dot(cat_ref[...], dy_ref[...], TN)

            @pl.when(c == 0)
            def _():
                for ch in range(N_CHIP):
                    mine_out[k, ch] = res[ch * W_OUT_ROWS:ch * W_OUT_ROWS + HALF_OUT]
                    send_out[k, ch] = res[ch * W_OUT_ROWS + HALF_OUT:(ch + 1) * W_OUT_ROWS]

            @pl.when(c == 1)
            def _():
                for ch in range(N_CHIP):
                    mine_out[k, ch] = res[ch * W_OUT_ROWS + HALF_OUT:(ch + 1) * W_OUT_ROWS]
                    send_out[k, ch] = res[ch * W_OUT_ROWS:ch * W_OUT_ROWS + HALF_OUT]

        stages = ((0, lambda j: to_sibling(j).start()), (DELAY_SUM, stage_sum), (DELAY_SECOND, stage_second),
                  (DELAY_FINAL, stage_final))
        for step in range(N_STEPS):
            @pl.when(i == step)
            def _(step=step):
                for delay, stage in stages:
                    if step - delay >= 0:
                        stage(step - delay)

        @pl.when(i == N_STEPS - 1)
        def _():
            for step in range(N_STEPS, N_STEPS + DELAY_FINAL):
                for delay, stage in stages:
                    if 0 <= step - delay < N_STEPS:
                        stage(step - delay)
            for j in range(N_STEPS - N_ITEMS, N_STEPS):
                drain(j)

    hbm = pl.BlockSpec(memory_space=pl.ANY)

    def layer(i):
        return jnp.where(i < N_ITEMS, DEPTH - 1, 0)

    def item(i):
        return jnp.where(i < N_ITEMS, i, i - N_ITEMS)

    def whole(i):
        return (layer(i), 0, 0)

    def dproj_piece(ch):
        return pl.BlockSpec((None, SEQ, HEAD),
                            lambda i: (layer(i), 0, ch * IN_STEPS + jnp.clip(item(i) - OUT_FIRST, 0, IN_STEPS - 1)))

    def dy_quarter(i):
        return (layer(i), 0, jnp.where(item(i) < OUT_FIRST, item(i), jnp.maximum(item(i) - IN_STEPS, OUT_FIRST)))

    operand = pl.BlockSpec((None, SEQ, D_MODEL), whole, pipeline_mode=pl.Buffered(1))
    in_specs = [operand] + [dproj_piece(ch) for ch in range(N_CHIP)]
    in_specs += [operand, pl.BlockSpec((None, SEQ, OUT_COLS), dy_quarter), hbm, _const_in((DEPTH, 8, D_MODEL))]
    args = [h, dproj, dproj, dproj, dproj, cat, dy, pack, dmod]
    out_shape = [jax.ShapeDtypeStruct((DEPTH, D_MODEL, W_IN_COLS), F32), jax.ShapeDtypeStruct((DEPTH, 2, HALF_OUT, D_MODEL), F32),
                 jax.ShapeDtypeStruct((DEPTH, PK_ROWS, HEAD), F32), jax.ShapeDtypeStruct((DEPTH, 24, N_DEV, HEAD), F32)]
    out_specs = [hbm, hbm, hbm, hbm]
    in_item = lambda *lead: pltpu.VMEM(lead + (HALF_IN, HEAD), BF16)
    out_item = lambda *lead: pltpu.VMEM(lead + (HALF_OUT, OUT_COLS), BF16)
    scratch = [
        pltpu.VMEM((IN_STEPS, HALF_IN, N_CHIP * HEAD), F32), pltpu.VMEM((IN_STEPS, HALF_IN, N_CHIP * HEAD), F32),
        pltpu.VMEM((IN_STEPS, HALF_IN, N_CHIP * HEAD), F32), in_item(IN_STEPS, 3), in_item(IN_STEPS, 2), in_item(IN_STEPS),
        pltpu.VMEM((IN_STEPS, HALF_IN, HEAD), F32),
        pltpu.VMEM((OUT_STEPS, N_CHIP, HALF_OUT, OUT_COLS), F32), pltpu.VMEM((OUT_STEPS, N_CHIP, HALF_OUT, OUT_COLS), F32),
        pltpu.VMEM((OUT_STEPS, N_CHIP, HALF_OUT, OUT_COLS), F32), out_item(OUT_STEPS, 3), out_item(OUT_STEPS, 2),
        out_item(OUT_STEPS), pltpu.VMEM((OUT_STEPS, HALF_OUT, OUT_COLS), F32),
        pltpu.SemaphoreType.DMA((N_STEPS,)), pltpu.SemaphoreType.DMA((N_STEPS,)),
        pltpu.SemaphoreType.DMA((2 * N_STEPS,)), pltpu.SemaphoreType.DMA((2 * N_STEPS,)),
        pltpu.SemaphoreType.DMA((N_STEPS,)), pltpu.SemaphoreType.DMA((N_STEPS,)),
        pltpu.SemaphoreType.DMA((N_STEPS,)), pltpu.SemaphoreType.DMA((N_STEPS,)), pltpu.SemaphoreType.DMA((N_STEPS,)),
    ]
    for _ in range(DEPTH):
        scratch += [
            pltpu.VMEM((PK_HALF, HEAD), F32), pltpu.VMEM((PK_HALF, HEAD), F32), pltpu.VMEM((N_CHIP, PK_PIECE, HEAD), F32),
            pltpu.VMEM((PK_PIECE, HEAD), F32), pltpu.VMEM((PK_ROWS, HEAD), F32),
            pltpu.VMEM((24, 1, HEAD), F32), pltpu.VMEM((24, N_DEV, HEAD), F32),
            pltpu.SemaphoreType.DMA((6,)),
            pltpu.SemaphoreType.DMA((N_CHIP,)), pltpu.SemaphoreType.DMA((N_CHIP,)),
            pltpu.SemaphoreType.DMA((N_CHIP,)), pltpu.SemaphoreType.DMA((N_CHIP,)),
            pltpu.SemaphoreType.DMA((N_DEV - 1,)), pltpu.SemaphoreType.DMA((N_DEV - 1,)),
        ]
    return pl.pallas_call(
        body, name="wgrad", grid=(N_STEPS,), in_specs=in_specs, out_specs=out_specs, out_shape=out_shape,
        scratch_shapes=scratch,
        compiler_params=pltpu.CompilerParams(dimension_semantics=("arbitrary",), vmem_limit_bytes=VMEM_LIMIT),
    )(*args)


def _adamw(w, g, m, v):
    m = ADAM_B1 * m + (1.0 - ADAM_B1) * g
    v = ADAM_B2 * v + (1.0 - ADAM_B2) * (g * g)
    m_hat = m / (1.0 - ADAM_B1 ** ADAM_STEP)
    v_hat = v / (1.0 - ADAM_B2 ** ADAM_STEP)
    delta = -ADAM_LR * (m_hat / (jnp.sqrt(v_hat) + ADAM_EPS) + ADAM_WD * w)
    return delta, m, v


def _adam_sharded(c_all, dmods, ada, w_in_set, w_out_set):
    rows = D_MODEL // ADAM_PARTS

    def body(c_ref, dm_ref, wa_ref, ma_ref, va_ref, wi_ref, gi_ref, mi_ref, vi_ref, wo_ref, go_ref, mo_ref, vo_ref,
             ga_out, da_out, ma_out, va_out, di_out, mi_out, vi_out, do_out, mo_out, vo_out):
        l = pl.program_id(0)
        chip = 2 * lax.axis_index("x") + lax.axis_index("y")
        cv = c_ref[...]
        silu_c = (cv * _sigmoid(cv)).astype(BF16).astype(F32)
        pieces = []
        for k in range(W_ADA_COLS // HEAD):
            dk = dm_ref[l, (W_ADA_COLS // HEAD) * chip + k].astype(BF16).astype(F32)
            pieces.append(_dot_exact(silu_c, dk, TN))
        g = jnp.concatenate(pieces, axis=1)
        ga_out[...] = g
        da_out[...], ma_out[...], va_out[...] = _adamw(wa_ref[...], g, ma_ref[...], va_ref[...])
        di_out[...], mi_out[...], vi_out[...] = _adamw(wi_ref[...], gi_ref[...], mi_ref[...], vi_ref[...])
        do_out[...], mo_out[...], vo_out[...] = _adamw(wo_ref[...], go_ref[...], mo_ref[...], vo_ref[...])

    def blk(r, cols):
        return pl.BlockSpec((None, r, cols), lambda l, i: (l, i, 0))

    b_ada, b_in, b_out = blk(rows, W_ADA_COLS), blk(rows, W_IN_COLS), blk(W_OUT_ROWS // ADAM_PARTS, D_MODEL)
    shapes = [jax.ShapeDtypeStruct(a[0].shape, F32) for a in (ada, w_in_set, w_out_set)]
    return pl.pallas_call(
        body, name="adam_sharded", grid=(DEPTH, ADAM_PARTS),
        in_specs=[pl.BlockSpec((N_DEV, rows), lambda l, i: (0, i)), _const_in((DEPTH, 24, N_DEV, HEAD))]
        + [b_ada] * 3 + [b_in] * 4 + [b_out] * 4,
        out_specs=[b_ada] * 4 + [b_in] * 3 + [b_out] * 3,
        out_shape=[shapes[0]] * 4 + [shapes[1]] * 3 + [shapes[2]] * 3,
        compiler_params=pltpu.CompilerParams(dimension_semantics=("arbitrary", "arbitrary"), vmem_limit_bytes=VMEM_LIMIT),
    )(c_all, dmods, *ada, *w_in_set, *w_out_set)


def _adam_small(packs, dmods, weights, ms, vs):
    n = len(weights)

    def body(*refs):
        pack_refs, dm_refs = refs[0], refs[1]
        b = 2
        w_refs, m_refs, v_refs = refs[b:b + n], refs[b + n:b + 2 * n], refs[b + 2 * n:b + 3 * n]
        outs = refs[b + 3 * n:]
        g_refs, d_refs, nm_refs, nv_refs = outs[0:n], outs[n:2 * n], outs[2 * n:3 * n], outs[3 * n:4 * n]
        outs[4 * n][...] = pack_refs.at[DEPTH - 1][PK_LOSS:PK_LOSS + 1, 0:1] * (0.5 / D_MODEL)

        def lanes(l, row0, count):
            return jnp.concatenate([pack_refs.at[l][row0 + k:row0 + k + 1, :] for k in range(count)], axis=1)

        def update(idx, at, g):
            g_refs[idx][at] = g
            d_refs[idx][at], nm_refs[idx][at], nv_refs[idx][at] = _adamw(w_refs[idx][at], g, m_refs[idx][at], v_refs[idx][at])

        for l in range(DEPTH):
            row = (slice(l, l + 1), slice(None))
            g_b = None
            for d in range(N_DEV):
                part = dm_refs.at[l][:, d, :]
                g_b = part if g_b is None else g_b + part
            update(0, row, jnp.concatenate([g_b[k:k + 1, :] for k in range(24)], axis=1))
            for g in range(N_HEAD):
                update(1, (l, g), pack_refs.at[l][PK_W_POOL + g * HEAD:PK_W_POOL + (g + 1) * HEAD, :])
                update(5, (l, g), pack_refs.at[l][PK_W_SGU + g * HEAD:PK_W_SGU + (g + 1) * HEAD, :])
            update(2, row, lanes(l, PK_POOL_SCALE, N_HEAD))
            update(3, (l,), pack_refs.at[l][PK_SGU_LN_G:PK_SGU_LN_G + N_HEAD, :])
            update(4, (l,), pack_refs.at[l][PK_SGU_LN_B:PK_SGU_LN_B + N_HEAD, :])
            update(6, (l,), pack_refs.at[l][PK_B_SGU:PK_B_SGU + N_HEAD, :])
            update(7, row, lanes(l, PK_LN_G, D_MODEL // HEAD))
            update(8, row, lanes(l, PK_LN_B, D_MODEL // HEAD))

    vmem = pl.BlockSpec(memory_space=pltpu.VMEM)
    shapes = [jax.ShapeDtypeStruct(w.shape, F32) for w in weights]
    return pl.pallas_call(
        body, name="adam_small", in_specs=[vmem] * (2 + 3 * n), out_specs=[vmem] * (4 * n + 1),
        out_shape=shapes * 4 + [jax.ShapeDtypeStruct((1, 1), F32)],
        compiler_params=pltpu.CompilerParams(vmem_limit_bytes=VMEM_LIMIT),
    )(packs, dmods, *weights, *ms, *vs)


def kernel(x, c, w_ada, b_ada, w_in, w_pool, pool_scale, sgu_ln_g, sgu_ln_b, w_sgu, b_sgu, w_out, ln_g, ln_b, loss_target, m_w_ada, m_b_ada, m_w_in, m_w_pool, m_pool_scale, m_sgu_ln_g, m_sgu_ln_b, m_w_sgu, m_b_sgu, m_w_out, m_ln_g, m_ln_b, v_w_ada, v_b_ada, v_w_in, v_w_pool, v_pool_scale, v_sgu_ln_g, v_sgu_ln_b, v_w_sgu, v_b_sgu, v_w_out, v_ln_g, v_ln_b):
    w_in0, w_in1, w_out0, w_out1, mod, c_all = _prepare(c, w_ada, b_ada, w_in, w_out)
    small = (w_pool, pool_scale, sgu_ln_g, sgu_ln_b, w_sgu, jnp.swapaxes(b_sgu, 1, 2))

    proj0, y0, x1, w_in1, w_out1 = _forward_layer(0, x, mod, w_in0, w_out0, small, ln_g, ln_b,
                                                  next_weights=(w_in1, w_out1))
    proj1, y1, dout, sq = _forward_layer(1, x1, mod, w_in1, w_out1, small, ln_g, ln_b, target=loss_target)

    dx1, *shared = _backward_layer(1, dout, x1, y1, proj1, mod, w_in1, w_out1, small, ln_g, sq=sq)
    dx0, h, cat, dy, dproj, pack, dmod = _backward_layer(0, dx1, x, y0, proj0, mod, w_in0, w_out0, small, ln_g, shared=shared)
    g_in, g_out, pack, dmods = _wgrad_reduce(h, dproj, cat, dy, pack, dmod)

    g_out = g_out.reshape(DEPTH, W_OUT_ROWS, D_MODEL)
    big = _adam_sharded(c_all, dmods, (w_ada, m_w_ada, v_w_ada), (w_in, g_in, m_w_in, v_w_in), (w_out, g_out, m_w_out, v_w_out))
    ada, win, wout = big[0:4], (g_in, *big[4:7]), (g_out, *big[7:10])
    small_w = (b_ada, w_pool, pool_scale, sgu_ln_g, sgu_ln_b, w_sgu, b_sgu, ln_g, ln_b)
    small_m = (m_b_ada, m_w_pool, m_pool_scale, m_sgu_ln_g, m_sgu_ln_b, m_w_sgu, m_b_sgu, m_ln_g, m_ln_b)
    small_v = (v_b_ada, v_w_pool, v_pool_scale, v_sgu_ln_g, v_sgu_ln_b, v_w_sgu, v_b_sgu, v_ln_g, v_ln_b)
    res = _adam_small(pack, dmods, small_w, small_m, small_v)
    n = len(small_w)
    loss = res[4 * n].reshape(())

    def ordered(k):
        s = res[k * n:(k + 1) * n]
        return (ada[k], s[0], win[k], s[1], s[2], s[3], s[4], s[5], s[6], wout[k], s[7], s[8])

    return (loss, dx0[None], *ordered(0), *ordered(1), *ordered(2), *ordered(3))
```

```python
import jax
import jax.numpy as jnp
from jax import lax
from jax.experimental import pallas as pl
from jax.experimental.pallas import tpu as pltpu

F32 = jnp.float32
BF16 = jnp.bfloat16
MESH = pl.DeviceIdType.MESH

N_DEV = 8
N_CHIP = 4
DEPTH = 2
SEQ = 2048
D_MODEL = 1024
D_POOL = 512
D_PROJ = 2560
HEAD = 128
N_HEAD = 4
ROWS = 256
N_TILE = SEQ // ROWS
HALO = 16
W_IN_COLS = D_PROJ // N_CHIP
W_OUT_ROWS = D_MODEL // N_CHIP
W_ADA_COLS = 3 * D_MODEL // N_CHIP
HALF_IN = D_MODEL // 2
HALF_OUT = W_OUT_ROWS // 2
DEEPNORM_ALPHA = (2.0 * DEPTH) ** 0.25
LN_EPS = 1e-5
INV_SQRT2 = 0.7071067811865476
INV_SQRT_2PI = 0.3989422804014327

ADAM_LR = 0.001
ADAM_B1 = 0.9
ADAM_B2 = 0.999
ADAM_EPS = 1e-08
ADAM_WD = 0.01
ADAM_STEP = 10
ADAM_PARTS = 2

PK_W_POOL = 0
PK_W_SGU = 512
PK_POOL_SCALE = 1024
PK_SGU_LN_G = 1032
PK_SGU_LN_B = 1040
PK_B_SGU = 1048
PK_LN_G = 1056
PK_LN_B = 1064
PK_LOSS = 1072
PK_ROWS = 1088
PK_HALF = PK_ROWS // 2
PK_PIECE = PK_HALF // N_CHIP

VMEM_LIMIT = 56 * 1024 * 1024

GATHER_SECOND_ROUND_STEP = 1
GATHER_PASS_STEP = 3

NN = (((1,), (0,)), ((), ()))
NT = (((1,), (1,)), ((), ()))
TN = (((0,), (0,)), ((), ()))


def _dot(a, b, dims=NN):
    return lax.dot_general(a, b, dims, preferred_element_type=F32)


def _dot_exact(a, b, dims=NN):
    return lax.dot_general(a, b, dims, preferred_element_type=F32, precision=lax.Precision.HIGHEST)


def _layer_norm(v):
    mu = jnp.mean(v, axis=-1, keepdims=True)
    d = v - mu
    var = jnp.mean(d * d, axis=-1, keepdims=True)
    rstd = lax.rsqrt(var + LN_EPS)
    return d * rstd, rstd


def _layer_norm_bwd(dvhat, vhat, rstd):
    m1 = jnp.mean(dvhat, axis=-1, keepdims=True)
    m2 = jnp.mean(dvhat * vhat, axis=-1, keepdims=True)
    return rstd * (dvhat - m1 - vhat * m2)


def _sigmoid(v):
    return 1.0 / (1.0 + jnp.exp(-v))


def _gelu_parts(v):
    phi = 0.5 * (1.0 + lax.erf(v * INV_SQRT2))
    pdf = INV_SQRT_2PI * jnp.exp(-0.5 * v * v)
    return phi, pdf


def _sum_rows(v):
    return jnp.sum(v, axis=0, keepdims=True)


def _window_sums(ext, toward_later):
    n = ext.shape[0]

    def shifted(v, k):
        return pltpu.roll(v, (n - k) if toward_later else k, 0)

    s2 = ext + shifted(ext, 1)
    r4 = s2[:, HEAD:]
    s4 = r4 + shifted(r4, 2)
    r8 = s4[:, HEAD:]
    s8 = r8 + shifted(r8, 4)
    r16 = s8[:, HEAD:]
    s16 = r16 + shifted(r16, 8)
    return jnp.concatenate([s2[:, :HEAD], s4[:, :HEAD], s8[:, :HEAD], s16], axis=1)


def _window_counts(row0):
    t1 = row0 + 1 + lax.broadcasted_iota(jnp.int32, (ROWS, D_POOL), 0)
    lane = lax.broadcasted_iota(jnp.int32, (ROWS, D_POOL), 1)
    width = jnp.where(lane < HEAD, 2, jnp.where(lane < 2 * HEAD, 4, jnp.where(lane < 3 * HEAD, 8, 16)))
    return jnp.minimum(t1, width).astype(F32)


def _causal_mask():
    r = lax.broadcasted_iota(jnp.int32, (HEAD, HEAD), 0)
    s = lax.broadcasted_iota(jnp.int32, (HEAD, HEAD), 1)
    return r >= s


def _chunks_to_lanes(v):
    return jnp.concatenate([v[n * HEAD:(n + 1) * HEAD] for n in range(ROWS // HEAD)], axis=1)


def _lanes_to_chunks(v):
    return jnp.concatenate([v[:, n * HEAD:(n + 1) * HEAD] for n in range(ROWS // HEAD)], axis=0)


def _pack_stats(rstd_x, rstd_z, rstd_v):
    lane = lax.broadcasted_iota(jnp.int32, (ROWS, HEAD), 1)
    packed = rstd_x
    for k, r in enumerate([rstd_z] + list(rstd_v)):
        packed = jnp.where(lane < 16 * (k + 1), packed, r)
    return packed


def _unpack_stats(stats):
    cols = [stats[:, 16 * k:16 * k + 1] for k in range(2 + N_HEAD)]
    return cols[0], cols[1], cols[2:]


def _mixer(proj, halo, row0, wpool_ref, pscale, sgu_g_ref, sgu_b_ref, wsgu_ref, bsgu_t_ref, saved=None):
    xa = proj[:, 0:512]
    ga = proj[:, 512:1024]
    u = proj[:, 1024:1536]
    v = proj[:, 1536:2048]
    gb = proj[:, 2048:2560]
    ext = jnp.concatenate([halo, xa], axis=0)
    win = _window_sums(ext, toward_later=False)[HALO:]
    cnt = _window_counts(row0)
    pooled = (win / cnt - xa).astype(BF16)
    pw = jnp.concatenate(
        [_dot(pooled[:, g * HEAD:(g + 1) * HEAD], wpool_ref[g].astype(BF16)) for g in range(N_HEAD)], axis=1)
    sig_a = _sigmoid(ga) if saved is None else saved["sig_a"]
    ya = pw * pscale * (ga * sig_a)
    phi_u, pdf_u = _gelu_parts(u)
    phi_v, pdf_v = _gelu_parts(v)
    gu = u * phi_u
    gv = v * phi_v
    sig_b = _sigmoid(gb) if saved is None else saved["sig_b"]
    silu_b = gb * sig_b
    mask = _causal_mask()
    vhat, rstd_v, vln_l, mixed = [], [], [], []
    for h in range(N_HEAD):
        if saved is None:
            vh, rh = _layer_norm(gv[:, h * HEAD:(h + 1) * HEAD])
        else:
            vh, rh = saved["vhat"][h], saved["rstd_v"][h]
        ln = (vh * sgu_g_ref[h:h + 1, :] + sgu_b_ref[h:h + 1, :]).astype(BF16)
        ln_l = _chunks_to_lanes(ln)
        wm = jnp.where(mask, wsgu_ref[h], 0.0).astype(BF16)
        mx = _lanes_to_chunks(_dot(wm, ln_l) + bsgu_t_ref[:, h:h + 1])
        vhat.append(vh)
        rstd_v.append(rh)
        vln_l.append(ln_l)
        mixed.append(mx)
    mixed = jnp.concatenate(mixed, axis=1)
    yb = gu * mixed * silu_b
    return dict(xa=xa, ga=ga, u=u, gb=gb, cnt=cnt, pooled=pooled, pw=pw, sig_a=sig_a, ya=ya, phi_u=phi_u, pdf_u=pdf_u,
                phi_v=phi_v, pdf_v=pdf_v, gu=gu, sig_b=sig_b, silu_b=silu_b, vhat=vhat, rstd_v=rstd_v, vln_l=vln_l,
                mixed=mixed, yb=yb, mask=mask)


def _const(shape, *index):
    lead = tuple(index) + (0,) * (len(shape) - len(index))
    return pl.BlockSpec(shape, lambda *_: lead)


def _const_in(shape, *index):
    lead = tuple(index) + (0,) * (len(shape) - len(index))
    return pl.BlockSpec(shape, lambda *_: lead, pipeline_mode=pl.Buffered(1))


def _layer_weight_specs(l):
    return [
        _const_in((None, N_HEAD, HEAD, HEAD), l),
        _const_in((DEPTH, D_POOL)),
        _const_in((None, N_HEAD, HEAD), l),
        _const_in((None, N_HEAD, HEAD), l),
        _const_in((None, N_HEAD, HEAD, HEAD), l),
        _const_in((None, HEAD, N_HEAD), l),
    ]


N_SAVED = 7


def _forward_layer(l, x, mod, w_in, w_out, small, ln_g, ln_b, target=None, next_weights=None):
    last = target is not None
    gathers = next_weights is not None

    def body(*refs):
        (x_ref, mod_ref, win_ref, wout_ref, wpool_ref, pscale_ref, sgu_g_ref, sgu_b_ref, wsgu_ref, bsgu_t_ref,
         lng_ref, lnb_ref) = refs[:12]
        n_in = 12 + (1 if last else 0) + (2 if gathers else 0)
        proj_ref, y_ref, xn_ref, zn_ref, stats_ref, sig_ref, vhat_ref = refs[n_in:n_in + N_SAVED]
        rest = refs[n_in + N_SAVED:]
        if last:
            tgt_ref = refs[12]
            dout_ref, loss_ref, carry_ref = rest
        elif gathers:
            next_in, next_out, carry_ref, g_send, g_recv = rest
            gather = _WeightGather(_Place(), next_in, next_out, g_send, g_recv)
        else:
            (carry_ref,) = rest
        i = pl.program_id(0)

        @pl.when(i == 0)
        def _():
            carry_ref[...] = jnp.zeros_like(carry_ref)
            if last:
                loss_ref[...] = jnp.zeros_like(loss_ref)
            if gathers:
                gather.start_first_round()

        if gathers:
            for q in range(_WeightGather.CHUNKS):
                @pl.when(i == GATHER_SECOND_ROUND_STEP + q)
                def _(q=q):
                    gather.start_second_round(q)

                @pl.when(i == GATHER_PASS_STEP + q)
                def _(q=q):
                    gather.pass_second_round(q)

        x = x_ref[...]
        if l > 0:
            x = x * lng_ref[l - 1:l, :] + lnb_ref[l - 1:l, :]
        shift, scale, gate = mod_ref[0:1, :], mod_ref[1:2, :], mod_ref[2:3, :]
        xn, rstd_x = _layer_norm(x)
        xn_ref[...] = xn
        h = xn * (1.0 + scale) + shift
        proj = _dot(h.astype(BF16), win_ref[...])
        proj_ref[...] = proj
        m = _mixer(proj, carry_ref[...], i * ROWS, wpool_ref, pscale_ref[l:l + 1, :], sgu_g_ref, sgu_b_ref, wsgu_ref,
                   bsgu_t_ref)
        carry_ref[...] = m["xa"][ROWS - HALO:]
        sig_ref[...] = jnp.concatenate([m["sig_a"], m["sig_b"]], axis=1)
        vhat_ref[...] = jnp.concatenate(m["vhat"], axis=1)
        cat = jnp.concatenate([m["ya"], m["yb"]], axis=1).astype(BF16)
        y = _dot(cat, wout_ref[...])
        y_ref[...] = y
        zn, rstd_z = _layer_norm(DEEPNORM_ALPHA * x + gate * y)
        zn_ref[...] = zn
        stats_ref[...] = _pack_stats(rstd_x, rstd_z, m["rstd_v"])
        if last:
            err = zn * lng_ref[l:l + 1, :] + lnb_ref[l:l + 1, :] - tgt_ref[...]
            dout_ref[...] = err * (1.0 / D_MODEL)
            loss_ref[...] += jnp.sum(err * err)

        if gathers:
            @pl.when(i == N_TILE - 1)
            def _():
                gather.finish()

    tile = pl.BlockSpec((ROWS, D_MODEL), lambda i: (i, 0))
    tile3 = pl.BlockSpec((None, ROWS, D_MODEL), lambda i: (0, i, 0))
    in_specs = [tile3 if x.ndim == 3 else tile, _const_in((None, 8, D_MODEL), l), _const_in((D_MODEL, D_PROJ)),
                _const_in((D_MODEL, D_MODEL))]
    in_specs += _layer_weight_specs(l) + [_const_in((DEPTH, D_MODEL)), _const_in((DEPTH, D_MODEL))]
    rows_of = lambda cols: pl.BlockSpec((ROWS, cols), lambda i: (i, 0))
    saved_cols = (D_PROJ, D_MODEL, D_MODEL, D_MODEL, HEAD, D_MODEL, D_POOL)
    out_shape = [jax.ShapeDtypeStruct((SEQ, cols), F32) for cols in saved_cols]
    out_specs = [rows_of(cols) for cols in saved_cols]
    args = [x, mod, w_in, w_out, *small, ln_g, ln_b]
    scratch = [pltpu.VMEM((HALO, D_POOL), F32)]
    aliases = {}
    if last:
        in_specs.append(tile3)
        args.append(target)
        out_shape += [jax.ShapeDtypeStruct((SEQ, D_MODEL), F32), jax.ShapeDtypeStruct((8, HEAD), F32)]
        out_specs += [tile, _const((8, HEAD))]
    if gathers:
        hbm = pl.BlockSpec(memory_space=pl.ANY)
        aliases = {len(args): len(out_shape), len(args) + 1: len(out_shape) + 1}
        in_specs += [hbm, hbm]
        args += list(next_weights)
        out_shape += [jax.ShapeDtypeStruct(w.shape, BF16) for w in next_weights]
        out_specs += [hbm, hbm]
        scratch += [pltpu.SemaphoreType.DMA((_WeightGather.N_SEMS,)), pltpu.SemaphoreType.DMA((_WeightGather.N_SEMS,))]
    return pl.pallas_call(
        body, name="fwd_last" if last else "fwd_first", grid=(N_TILE,), in_specs=in_specs, out_specs=out_specs,
        out_shape=out_shape, scratch_shapes=scratch, input_output_aliases=aliases,
        compiler_params=pltpu.CompilerParams(dimension_semantics=("arbitrary",), vmem_limit_bytes=VMEM_LIMIT),
    )(*args)


def _backward_layer(l, dout, saved, mod, w_in, w_out, small, ln_g, sq=None, shared=None):
    has_loss = sq is not None

    def body(*refs):
        (dout_ref, proj_ref, y_ref, xn_ref, zn_ref, stats_ref, sig_ref, vhat_ref, halo_ref, mod_ref, win_ref, wout_ref,
         wpool_ref, pscale_ref, sgu_g_ref, sgu_b_ref, wsgu_ref, bsgu_t_ref, lng_ref) = refs[:19]
        n_in = 20 if has_loss else 19 + 6
        dx_ref, h_ref, cat_ref, dy_ref, dproj_ref, pack_ref, dmod_ref, carry_ref = refs[n_in:n_in + 8]
        i = pl.program_id(0)
        tile = N_TILE - 1 - i

        @pl.when(i == 0)
        def _():
            carry_ref[...] = jnp.zeros_like(carry_ref)
            pack_ref[...] = jnp.zeros_like(pack_ref)
            dmod_ref[...] = jnp.zeros_like(dmod_ref)
            if has_loss:
                pack_ref[PK_LOSS:PK_LOSS + 8, :] = refs[19][...]

        xn = xn_ref[...]
        zn = zn_ref[...]
        y = y_ref[...]
        dout = dout_ref[...]
        rstd_x, rstd_z, rstd_v = _unpack_stats(stats_ref[...])
        kept = dict(sig_a=sig_ref[:, :D_POOL], sig_b=sig_ref[:, D_POOL:], rstd_v=rstd_v,
                    vhat=[vhat_ref[:, hd * HEAD:(hd + 1) * HEAD] for hd in range(N_HEAD)])
        pscale = pscale_ref[l:l + 1, :]
        shift, scale, gate = mod_ref[0:1, :], mod_ref[1:2, :], mod_ref[2:3, :]
        h = xn * (1.0 + scale) + shift
        h_ref[...] = h.astype(BF16)
        g_ln_g = _sum_rows(dout * zn)
        g_ln_b = _sum_rows(dout)
        dz = _layer_norm_bwd(dout * lng_ref[l:l + 1, :], zn, rstd_z)
        d_gate = _sum_rows(dz * y)
        dy = (gate * dz).astype(BF16)
        dy_ref[...] = dy

        halo = jnp.where(tile > 0, halo_ref[...], 0.0)
        m = _mixer(proj_ref[...], halo, tile * ROWS, wpool_ref, pscale, sgu_g_ref, sgu_b_ref, wsgu_ref, bsgu_t_ref,
                   saved=kept)
        cat_ref[...] = jnp.concatenate([m["ya"], m["yb"]], axis=1).astype(BF16)
        dcat = _dot(dy, wout_ref[...], NT)
        dya = dcat[:, :D_POOL]
        dyb = dcat[:, D_POOL:]

        ga, sig_a = m["ga"], m["sig_a"]
        dp = dya * (ga * sig_a)
        d_ga = dya * (m["pw"] * pscale) * (sig_a * (1.0 + ga * (1.0 - sig_a)))
        g_pscale = _sum_rows(dp * m["pw"])
        dpw = (dp * pscale).astype(BF16)
        dpooled = []
        for g in range(N_HEAD):
            cols = slice(g * HEAD, (g + 1) * HEAD)
            pack_ref[PK_W_POOL + g * HEAD:PK_W_POOL + (g + 1) * HEAD, :] += _dot(m["pooled"][:, cols], dpw[:, cols], TN)
            dpooled.append(_dot(dpw[:, cols], wpool_ref[g].astype(BF16), NT))
        dpooled = jnp.concatenate(dpooled, axis=1)
        q = dpooled / m["cnt"]
        ext = jnp.concatenate([q, carry_ref[...]], axis=0)
        d_xa = _window_sums(ext, toward_later=True)[:ROWS] - dpooled
        carry_ref[...] = q[:HALO]

        gu, mixed, silu_b, gb, sig_b = m["gu"], m["mixed"], m["silu_b"], m["gb"], m["sig_b"]
        d_mixed = dyb * gu * silu_b
        d_gu = dyb * mixed * silu_b
        d_gb = dyb * gu * mixed * (sig_b * (1.0 + gb * (1.0 - sig_b)))
        d_u = d_gu * (m["phi_u"] + m["u"] * m["pdf_u"])
        ones = jnp.ones((8, HEAD), F32)
        d_v = []
        for hd in range(N_HEAD):
            cols = slice(hd * HEAD, (hd + 1) * HEAD)
            dm = d_mixed[:, cols]
            dm_l = _chunks_to_lanes(dm.astype(BF16))
            g_w = _dot(dm_l, m["vln_l"][hd], NT)
            pack_ref[PK_W_SGU + hd * HEAD:PK_W_SGU + (hd + 1) * HEAD, :] += jnp.where(m["mask"], g_w, 0.0)
            dm_sum = dm[0:HEAD]
            for n in range(1, ROWS // HEAD):
                dm_sum = dm_sum + dm[n * HEAD:(n + 1) * HEAD]
            pack_ref[PK_B_SGU + hd:PK_B_SGU + hd + 1, :] += _dot_exact(ones, dm_sum, NT)[0:1]
            wm = jnp.where(m["mask"], wsgu_ref[hd], 0.0).astype(BF16)
            d_vln = _lanes_to_chunks(_dot(wm, dm_l, TN))
            vhat = m["vhat"][hd]
            pack_ref[PK_SGU_LN_G + hd:PK_SGU_LN_G + hd + 1, :] += _sum_rows(d_vln * vhat)
            pack_ref[PK_SGU_LN_B + hd:PK_SGU_LN_B + hd + 1, :] += _sum_rows(d_vln)
            d_v.append(_layer_norm_bwd(d_vln * sgu_g_ref[hd:hd + 1, :], vhat, m["rstd_v"][hd]))
        v = proj_ref[:, 1536:2048]
        d_v = jnp.concatenate(d_v, axis=1) * (m["phi_v"] + v * m["pdf_v"])

        dproj = jnp.concatenate([d_xa, d_ga, d_u, d_v, d_gb], axis=1).astype(BF16)
        dproj_ref[...] = dproj
        dh = _dot(dproj, win_ref[...], NT)
        d_scale = _sum_rows(dh * xn)
        d_shift = _sum_rows(dh)
        dx_ref[...] = DEEPNORM_ALPHA * dz + _layer_norm_bwd(dh * (1.0 + scale), xn, rstd_x)

        dmod_ref[0:1, :] += d_shift
        dmod_ref[1:2, :] += d_scale
        dmod_ref[2:3, :] += d_gate
        for g in range(N_HEAD):
            pack_ref[PK_POOL_SCALE + g:PK_POOL_SCALE + g + 1, :] += g_pscale[:, g * HEAD:(g + 1) * HEAD]
        for k in range(D_MODEL // HEAD):
            pack_ref[PK_LN_G + k:PK_LN_G + k + 1, :] += g_ln_g[:, k * HEAD:(k + 1) * HEAD]
            pack_ref[PK_LN_B + k:PK_LN_B + k + 1, :] += g_ln_b[:, k * HEAD:(k + 1) * HEAD]

    def rev(i):
        return (N_TILE - 1 - i, 0)

    tile = pl.BlockSpec((ROWS, D_MODEL), rev)
    halo = pl.BlockSpec((HALO, D_POOL), lambda i: (jnp.maximum((N_TILE - 1 - i) * (ROWS // HALO) - 1, 0), 0))
    in_specs = [tile] + [pl.BlockSpec((ROWS, a.shape[1]), rev) for a in saved] + [halo]
    in_specs += [_const_in((None, 8, D_MODEL), l), _const_in((D_MODEL, D_PROJ)), _const_in((D_MODEL, D_MODEL))]
    in_specs += _layer_weight_specs(l) + [_const_in((DEPTH, D_MODEL))]
    args = [dout, *saved, saved[0], mod, w_in, w_out, *small, ln_g]
    stacked = lambda cols: pl.BlockSpec((None, ROWS, cols), lambda i: (l, N_TILE - 1 - i, 0))
    out_shape = [jax.ShapeDtypeStruct((SEQ, D_MODEL), F32), jax.ShapeDtypeStruct((DEPTH, SEQ, D_MODEL), BF16),
                 jax.ShapeDtypeStruct((DEPTH, SEQ, D_MODEL), BF16), jax.ShapeDtypeStruct((DEPTH, SEQ, D_MODEL), BF16),
                 jax.ShapeDtypeStruct((DEPTH, SEQ, D_PROJ), BF16), jax.ShapeDtypeStruct((DEPTH, PK_ROWS, HEAD), F32),
                 jax.ShapeDtypeStruct((DEPTH, 8, D_MODEL), F32)]
    out_specs = [tile, stacked(D_MODEL), stacked(D_MODEL), stacked(D_MODEL), stacked(D_PROJ),
                 _const((None, PK_ROWS, HEAD), l), _const((None, 8, D_MODEL), l)]
    aliases = {}
    if has_loss:
        in_specs.append(_const_in((8, HEAD)))
        args.append(sq)
    else:
        aliases = {len(args) + k: 1 + k for k in range(len(shared))}
        in_specs += [pl.BlockSpec(memory_space=pl.ANY)] * len(shared)
        args += list(shared)
    return pl.pallas_call(
        body, name="bwd_last" if has_loss else "bwd_first", grid=(N_TILE,), in_specs=in_specs, out_specs=out_specs,
        out_shape=out_shape, scratch_shapes=[pltpu.VMEM((HALO, D_POOL), F32)], input_output_aliases=aliases,
        compiler_params=pltpu.CompilerParams(dimension_semantics=("arbitrary",), vmem_limit_bytes=VMEM_LIMIT),
    )(*args)


def _flip(v, f):
    return v + f - 2 * v * f


class _Place:
    def __init__(self):
        x, y, c = lax.axis_index("x"), lax.axis_index("y"), lax.axis_index("c")
        self.x, self.y, self.c = x, y, c
        self.chip = 2 * x + y
        self.dev = 4 * x + 2 * y + c
        self.sibling = (x, y, 1 - c)
        x1, y1 = _flip(x, 1 - c), _flip(y, c)
        x2, y2 = _flip(x, c), _flip(y, 1 - c)
        self.first = (x1, y1, c)
        self.second = (x2, y2, c)
        self.chip_first = 2 * x1 + y1
        self.chip_second = 2 * x2 + y2
        self.chip_far = 2 * (1 - x) + (1 - y)
        self.my_first_coord = jnp.where(c == 0, x, y)

    def first_coord(self, ch):
        return jnp.where(self.c == 0, ch // 2, ch % 2)

    def others(self):
        return [(_flip(self.x, (r >> 2) & 1), _flip(self.y, (r >> 1) & 1), _flip(self.c, r & 1)) for r in range(1, N_DEV)]

    def other_chips(self):
        return [(1 - self.x, self.y), (self.x, 1 - self.y), (1 - self.x, 1 - self.y)]


class _WeightGather:
    CHUNKS = 4
    N_SEMS = 12 * CHUNKS

    def __init__(self, place, win, wout, send, recv):
        self.p, self.win, self.wout, self.send, self.recv = place, win, wout, send, recv
        p = place
        self.plan = [(p.chip, p.first), (p.chip, p.second), (p.chip_first, p.second),
                     (p.chip_first, p.sibling), (p.chip_second, p.sibling), (p.chip_far, p.sibling)]

    def _copies(self, k, q):
        ch, target = self.plan[k]
        n_in, n_out = HALF_IN // self.CHUNKS, HALF_OUT // self.CHUNKS
        rows_in = pl.ds(pl.multiple_of(self.p.c * HALF_IN + q * n_in, n_in), n_in)
        cols_in = pl.ds(pl.multiple_of(ch * W_IN_COLS, 128), W_IN_COLS)
        rows_out = pl.ds(pl.multiple_of(ch * W_OUT_ROWS + self.p.c * HALF_OUT + q * n_out, n_out), n_out)
        r_in = self.win.at[rows_in, cols_in]
        r_out = self.wout.at[rows_out, :]
        s = 2 * (6 * q + k)
        return [pltpu.make_async_remote_copy(r_in, r_in, self.send.at[s], self.recv.at[s],
                                             device_id=target, device_id_type=MESH),
                pltpu.make_async_remote_copy(r_out, r_out, self.send.at[s + 1], self.recv.at[s + 1],
                                             device_id=target, device_id_type=MESH)]

    def _start(self, k, q):
        for cp in self._copies(k, q):
            cp.start()

    def _landed(self, k, q):
        for cp in self._copies(k, q):
            cp.wait_recv()

    def start_first_round(self):
        for q in range(self.CHUNKS):
            self._start(0, q)

    def start_second_round(self, q):
        self._landed(0, q)
        self._start(1, q)
        self._start(2, q)
        self._start(3, q)

    def pass_second_round(self, q):
        self._landed(1, q)
        self._start(4, q)
        self._landed(2, q)
        self._start(5, q)

    def finish(self):
        for q in range(self.CHUNKS):
            for k in (3, 4, 5):
                self._landed(k, q)
        for q in range(self.CHUNKS):
            for k in range(len(self.plan)):
                for cp in self._copies(k, q):
                    cp.wait_send()


def _prepare(c_vec, w_ada, b_ada, w_in, w_out):
    def body(c_ref, wada_hbm, bada_ref, win_hbm, wout_hbm,
             win0, win1, wout0, wout1, mod_ref, c_all,
             wada_ref, win_ref, wout_ref, win_bf, wout_bf, mod_mine, mod_all,
             g_send, g_recv, c_send, c_recv, mod_send, mod_recv, local_sem):
        p = _Place()
        loads = [pltpu.make_async_copy(win_hbm.at[0], win_ref.at[0], local_sem.at[4]),
                 pltpu.make_async_copy(wout_hbm.at[0], wout_ref.at[0], local_sem.at[5]),
                 pltpu.make_async_copy(win_hbm.at[1], win_ref.at[1], local_sem.at[6]),
                 pltpu.make_async_copy(wout_hbm.at[1], wout_ref.at[1], local_sem.at[7]),
                 pltpu.make_async_copy(wada_hbm, wada_ref, local_sem.at[8])]
        for cp in loads:
            cp.start()

        c_all[pl.ds(p.dev, 1), :] = c_ref[...]
        c_copies = [pltpu.make_async_remote_copy(c_ref, c_all.at[pl.ds(p.dev, 1), :], c_send.at[r], c_recv.at[r],
                                                 device_id=d, device_id_type=MESH) for r, d in enumerate(p.others())]
        for cp in c_copies:
            cp.start()

        cols = pl.ds(pl.multiple_of(p.chip * W_IN_COLS, 128), W_IN_COLS)
        rows = pl.ds(pl.multiple_of(p.chip * W_OUT_ROWS, W_OUT_ROWS), W_OUT_ROWS)
        own = [pltpu.make_async_copy(win_bf.at[0], win0.at[:, cols], local_sem.at[0]),
               pltpu.make_async_copy(wout_bf.at[0], wout0.at[rows, :], local_sem.at[1]),
               pltpu.make_async_copy(win_bf.at[1], win1.at[:, cols], local_sem.at[2]),
               pltpu.make_async_copy(wout_bf.at[1], wout1.at[rows, :], local_sem.at[3])]
        for l in range(DEPTH):
            loads[2 * l].wait()
            win_bf[l] = win_ref[l].astype(BF16)
            own[2 * l].start()
            loads[2 * l + 1].wait()
            wout_bf[l] = wout_ref[l].astype(BF16)
            own[2 * l + 1].start()
            if l == 0:
                own[0].wait()
                own[1].wait()
                gather = _WeightGather(p, win0, wout0, g_send, g_recv)
                gather.start_first_round()
        for cp in c_copies:
            cp.wait()
        loads[4].wait()

        cv = c_all[...]
        silu_c = (cv * _sigmoid(cv)).astype(BF16)
        for l in range(DEPTH):
            mod_mine[l] = _dot(silu_c, wada_ref[l].astype(BF16))
        mod_all[p.chip] = mod_mine[...]
        m_copies = [pltpu.make_async_remote_copy(mod_mine, mod_all.at[p.chip], mod_send.at[k], mod_recv.at[k],
                                                 device_id=(px, py, p.c), device_id_type=MESH)
                    for k, (px, py) in enumerate(p.other_chips())]
        for cp in m_copies:
            cp.start()
        for q in range(gather.CHUNKS):
            gather.start_second_round(q)
        for cp in m_copies:
            cp.wait()
        mod_ref[...] = jnp.zeros_like(mod_ref)
        for l in range(DEPTH):
            full = jnp.concatenate([mod_all[ch, l, pl.ds(p.dev, 1), :] for ch in range(N_CHIP)], axis=1) + bada_ref[l:l + 1, :]
            for k in range(3):
                mod_ref[l, k:k + 1, :] = full[:, k * D_MODEL:(k + 1) * D_MODEL]
        for q in range(gather.CHUNKS):
            gather.pass_second_round(q)
        gather.finish()
        own[2].wait()
        own[3].wait()

    vmem = pl.BlockSpec(memory_space=pltpu.VMEM)
    hbm = pl.BlockSpec(memory_space=pl.ANY)
    w_in_shape = jax.ShapeDtypeStruct((D_MODEL, D_PROJ), BF16)
    w_out_shape = jax.ShapeDtypeStruct((D_MODEL, D_MODEL), BF16)
    return pl.pallas_call(
        body, name="prepare",
        in_specs=[vmem, hbm, vmem, hbm, hbm],
        out_specs=[hbm, hbm, hbm, hbm, vmem, vmem],
        out_shape=[w_in_shape, w_in_shape, w_out_shape, w_out_shape,
                   jax.ShapeDtypeStruct((DEPTH, 8, D_MODEL), F32), jax.ShapeDtypeStruct((N_DEV, D_MODEL), F32)],
        scratch_shapes=[
            pltpu.VMEM(w_ada.shape, F32), pltpu.VMEM(w_in.shape, F32), pltpu.VMEM(w_out.shape, F32),
            pltpu.VMEM((DEPTH, D_MODEL, W_IN_COLS), BF16), pltpu.VMEM((DEPTH, W_OUT_ROWS, D_MODEL), BF16),
            pltpu.VMEM((DEPTH, N_DEV, W_ADA_COLS), F32), pltpu.VMEM((N_CHIP, DEPTH, N_DEV, W_ADA_COLS), F32),
            pltpu.SemaphoreType.DMA((_WeightGather.N_SEMS,)), pltpu.SemaphoreType.DMA((_WeightGather.N_SEMS,)),
            pltpu.SemaphoreType.DMA((7,)), pltpu.SemaphoreType.DMA((7,)),
            pltpu.SemaphoreType.DMA((3,)), pltpu.SemaphoreType.DMA((3,)),
            pltpu.SemaphoreType.DMA((9,)),
        ],
        compiler_params=pltpu.CompilerParams(vmem_limit_bytes=VMEM_LIMIT),
    )(c_vec, w_ada, b_ada, w_in, w_out)


IN_STEPS = W_IN_COLS // HEAD
OUT_STEPS = 4
OUT_COLS = D_MODEL // OUT_STEPS
OUT_FIRST = 2
ITEMS = ([("out", k) for k in range(OUT_FIRST)] + [("in", k) for k in range(IN_STEPS)]
         + [("out", k) for k in range(OUT_FIRST, OUT_STEPS)])
N_ITEMS = len(ITEMS)
N_STEPS = DEPTH * N_ITEMS
DELAY_SUM, DELAY_SECOND, DELAY_FINAL = 1, 3, 5
SMALL_SCATTER_STEP, SMALL_GATHER_STEP, SMALL_PASS_STEP, SMALL_FINISH_STEP = 1, 3, 5, 7


def _wgrad_reduce(h, dproj, cat, dy, pack, dmod):
    def body(*refs):
        h_ref, dp_refs, cat_ref, dy_ref, pack_ref, dmod_ref = refs[0], refs[1:5], refs[5], refs[6], refs[7], refs[8]
        fin_in, fin_out, pack_out, dmod_out = refs[9:13]
        scratch = refs[13:]
        (mine_in, send_in, sib_in, st_in, r1_in, r2_in, f_in,
         mine_out, send_out, sib_out, st_out, r1_out, r2_out, f_out,
         d2d_s, d2d_r, r1_s, r1_r, r2_s, r2_r, fin_l, fin_s, fin_r) = scratch[:23]
        p = _Place()
        c = p.c
        i = pl.program_id(0)
        my_rows = pl.ds(pl.multiple_of(c * HALF_IN, HALF_IN), HALF_IN)

        def layer_of(j):
            return DEPTH - 1 - j // N_ITEMS

        def bufs(j):
            kind, k = ITEMS[j % N_ITEMS]
            if kind == "in":
                return [r.at[k] for r in (mine_in, send_in, sib_in, st_in, r1_in, r2_in, f_in)]
            return [r.at[k] for r in (mine_out, send_out, sib_out, st_out, r1_out, r2_out, f_out)]

        def piece(j, ref, ch):
            if ITEMS[j % N_ITEMS][0] == "in":
                return ref.at[:, ch * HEAD:(ch + 1) * HEAD]
            return ref.at[ch]

        def slot(ch):
            return jnp.where(c == 0, ch % 2, ch // 2)

        def to_sibling(j):
            _, send, sib, _, _, _, _ = bufs(j)
            return pltpu.make_async_remote_copy(send, sib, d2d_s.at[j], d2d_r.at[j], device_id=p.sibling, device_id_type=MESH)

        def first_round(j, ch):
            _, _, _, st, r1, _, _ = bufs(j)
            k = slot(ch)
            return pltpu.make_async_remote_copy(st.at[k], r1.at[k], r1_s.at[2 * j + k], r1_r.at[2 * j + k],
                                                device_id=p.first, device_id_type=MESH)

        def second_round(j):
            _, _, _, st, _, r2, _ = bufs(j)
            return pltpu.make_async_remote_copy(st.at[2], r2, r2_s.at[j], r2_r.at[j], device_id=p.second, device_id_type=MESH)

        def finals(j):
            f = bufs(j)[6]
            kind, k = ITEMS[j % N_ITEMS]
            if kind == "in":
                dst = fin_in.at[layer_of(j), my_rows, k * HEAD:(k + 1) * HEAD]
            else:
                dst = fin_out.at[layer_of(j), c, :, k * OUT_COLS:(k + 1) * OUT_COLS]
            return [pltpu.make_async_copy(f, dst, fin_l.at[j]),
                    pltpu.make_async_remote_copy(f, dst, fin_s.at[j], fin_r.at[j], device_id=p.sibling, device_id_type=MESH)]

        def stage_sum(j):
            mine, _, sib, st, _, _, _ = bufs(j)
            to_sibling(j).wait_recv()
            mine[...] = mine[...] + sib[...]
            for ch in range(N_CHIP):
                @pl.when(p.first_coord(ch) != p.my_first_coord)
                def _(ch=ch):
                    st[slot(ch)] = piece(j, mine, ch)[...].astype(BF16)
                    first_round(j, ch).start()

        def stage_second(j):
            mine, _, _, st, r1, _, _ = bufs(j)
            for ch in range(N_CHIP):
                @pl.when(p.first_coord(ch) == p.my_first_coord)
                def _(ch=ch):
                    first_round(j, ch).wait_recv()
                    part = piece(j, mine, ch)
                    total = part[...] + r1[slot(ch)].astype(F32)
                    part[...] = total

                    @pl.when(ch != p.chip)
                    def _():
                        st[2] = total.astype(BF16)
                        second_round(j).start()

        def stage_final(j):
            mine, _, _, _, _, r2, f = bufs(j)
            second_round(j).wait_recv()
            for ch in range(N_CHIP):
                @pl.when(ch == p.chip)
                def _(ch=ch):
                    f[...] = piece(j, mine, ch)[...] + r2[...].astype(F32)
            for cp in finals(j):
                cp.start()

        def drain(j):
            to_sibling(j).wait_send()
            for ch in range(N_CHIP):
                @pl.when(p.first_coord(ch) != p.my_first_coord)
                def _(ch=ch):
                    first_round(j, ch).wait_send()

                @pl.when(jnp.logical_and(p.first_coord(ch) == p.my_first_coord, ch != p.chip))
                def _():
                    second_round(j).wait_send()
            for cp in finals(j):
                cp.wait()

        dev = p.dev
        devices = p.others()

        def half(core):
            return pl.ds(pl.multiple_of(core * PK_HALF, 8), PK_HALF)

        def finished(core, ch):
            return pl.ds(pl.multiple_of(core * PK_HALF + ch * PK_PIECE, 8), PK_PIECE)

        def small_exchange(l, first_step, bufs_l):
            (pk_mine, pk_sib, pk_rs, pk_fin, pk_all, dm_st, dm_all, pk_sem, rs_s, rs_r, ag_s, ag_r, dm_s, dm_r) = bufs_l

            def pk_load():
                return pltpu.make_async_copy(pack_ref.at[l, half(c)], pk_mine, pk_sem.at[0])

            def pk_give():
                return pltpu.make_async_remote_copy(pack_ref.at[l, half(1 - c)], pk_sib, pk_sem.at[1], pk_sem.at[2],
                                                    device_id=p.sibling, device_id_type=MESH)

            def pk_scatter(ch):
                return pltpu.make_async_remote_copy(pk_mine.at[ch * PK_PIECE:(ch + 1) * PK_PIECE], pk_rs.at[p.chip],
                                                    rs_s.at[ch], rs_r.at[p.chip], device_id=(ch // 2, ch % 2, c),
                                                    device_id_type=MESH)

            def pk_spread(ch):
                return pltpu.make_async_remote_copy(pk_fin, pk_all.at[finished(c, p.chip)], ag_s.at[ch], ag_r.at[p.chip],
                                                    device_id=(ch // 2, ch % 2, c), device_id_type=MESH)

            def pk_pass():
                return pltpu.make_async_remote_copy(pk_all.at[half(c)], pk_all.at[half(c)], pk_sem.at[3], pk_sem.at[4],
                                                    device_id=p.sibling, device_id_type=MESH)

            def dm_copy(r):
                return pltpu.make_async_remote_copy(dm_st, dm_all.at[:, pl.ds(dev, 1), :], dm_s.at[r], dm_r.at[r],
                                                    device_id=devices[r], device_id_type=MESH)

            def results():
                return [pltpu.make_async_copy(pk_all, pack_out.at[l], pk_sem.at[0]),
                        pltpu.make_async_copy(dm_all, dmod_out.at[l], pk_sem.at[5])]

            @pl.when(i == first_step)
            def _():
                pk_load().start()
                pk_give().start()
                for k in range(3):
                    for r in range(D_MODEL // HEAD):
                        dm_st[8 * k + r] = dmod_ref[l, k:k + 1, r * HEAD:(r + 1) * HEAD]
                dm_all[:, pl.ds(dev, 1), :] = dm_st[...]
                for r in range(N_DEV - 1):
                    dm_copy(r).start()

            @pl.when(i == first_step + SMALL_SCATTER_STEP)
            def _():
                pk_load().wait()
                pk_give().wait()
                pk_mine[...] = pk_mine[...] + pk_sib[...]
                for ch in range(N_CHIP):
                    @pl.when(ch != p.chip)
                    def _(ch=ch):
                        pk_scatter(ch).start()

            @pl.when(i == first_step + SMALL_GATHER_STEP)
            def _():
                for ch in range(N_CHIP):
                    @pl.when(ch != p.chip)
                    def _(ch=ch):
                        pltpu.make_async_remote_copy(pk_fin, pk_rs.at[ch], rs_s.at[ch], rs_r.at[ch],
                                                     device_id=p.sibling, device_id_type=MESH).wait_recv()
                for me in range(N_CHIP):
                    @pl.when(me == p.chip)
                    def _(me=me):
                        total = None
                        for ch in range(N_CHIP):
                            part = pk_mine[me * PK_PIECE:(me + 1) * PK_PIECE] if ch == me else pk_rs[ch]
                            total = part if total is None else total + part
                        pk_fin[...] = total
                        pk_all[finished(c, me)] = total
                for ch in range(N_CHIP):
                    @pl.when(ch != p.chip)
                    def _(ch=ch):
                        pk_spread(ch).start()

            @pl.when(i == first_step + SMALL_PASS_STEP)
            def _():
                for ch in range(N_CHIP):
                    @pl.when(ch != p.chip)
                    def _(ch=ch):
                        pltpu.make_async_remote_copy(pk_fin, pk_all.at[finished(c, ch)], ag_s.at[ch], ag_r.at[ch],
                                                     device_id=p.sibling, device_id_type=MESH).wait_recv()
                pk_pass().start()

            @pl.when(i == first_step + SMALL_FINISH_STEP)
            def _():
                pk_pass().wait()
                for ch in range(N_CHIP):
                    @pl.when(ch != p.chip)
                    def _(ch=ch):
                        pk_scatter(ch).wait_send()
                        pk_spread(ch).wait_send()
                for r in range(N_DEV - 1):
                    dm_copy(r).wait()
                for cp in results():
                    cp.start()
                for cp in results():
                    cp.wait()

        n_small = 14
        for l in range(DEPTH):
            small_exchange(l, (DEPTH - 1 - l) * N_ITEMS, scratch[23 + n_small * l:23 + n_small * (l + 1)])

        for step in range(N_ITEMS, N_STEPS):
            @pl.when(i == step)
            def _(step=step):
                drain(step - N_ITEMS)

        ii = jnp.where(i < N_ITEMS, i, i - N_ITEMS)
        in_step = jnp.logical_and(ii >= OUT_FIRST, ii < OUT_FIRST + IN_STEPS)

        @pl.when(in_step)
        def _():
            k = ii - OUT_FIRST
            rhs = jnp.concatenate([r[...] for r in dp_refs], axis=1)
            res = _dot(h_ref[...], rhs, TN)

            @pl.when(c == 0)
            def _():
                mine_in[k] = res[:HALF_IN]
                send_in[k] = res[HALF_IN:]

            @pl.when(c == 1)
            def _():
                mine_in[k] = res[HALF_IN:]
                send_in[k] = res[:HALF_IN]

        @pl.when(jnp.logical_not(in_step))
        def _():
            k = jnp.where(ii < OUT_FIRST, ii, ii - IN_STEPS)
            res = _dot(cat_ref[...], dy_ref[...], TN)

            @pl.when(c == 0)
            def _():
                for ch in range(N_CHIP):
                    mine_out[k, ch] = res[ch * W_OUT_ROWS:ch * W_OUT_ROWS + HALF_OUT]
                    send_out[k, ch] = res[ch * W_OUT_ROWS + HALF_OUT:(ch + 1) * W_OUT_ROWS]

            @pl.when(c == 1)
            def _():
                for ch in range(N_CHIP):
                    mine_out[k, ch] = res[ch * W_OUT_ROWS + HALF_OUT:(ch + 1) * W_OUT_ROWS]
                    send_out[k, ch] = res[ch * W_OUT_ROWS:ch * W_OUT_ROWS + HALF_OUT]

        stages = ((0, lambda j: to_sibling(j).start()), (DELAY_SUM, stage_sum), (DELAY_SECOND, stage_second),
                  (DELAY_FINAL, stage_final))
        for step in range(N_STEPS):
            @pl.when(i == step)
            def _(step=step):
                for delay, stage in stages:
                    if step - delay >= 0:
                        stage(step - delay)

        @pl.when(i == N_STEPS - 1)
        def _():
            for step in range(N_STEPS, N_STEPS + DELAY_FINAL):
                for delay, stage in stages:
                    if 0 <= step - delay < N_STEPS:
                        stage(step - delay)
            for j in range(N_STEPS - N_ITEMS, N_STEPS):
                drain(j)

    hbm = pl.BlockSpec(memory_space=pl.ANY)

    def layer(i):
        return jnp.where(i < N_ITEMS, DEPTH - 1, 0)

    def item(i):
        return jnp.where(i < N_ITEMS, i, i - N_ITEMS)

    def whole(i):
        return (layer(i), 0, 0)

    def dproj_piece(ch):
        return pl.BlockSpec((None, SEQ, HEAD),
                            lambda i: (layer(i), 0, ch * IN_STEPS + jnp.clip(item(i) - OUT_FIRST, 0, IN_STEPS - 1)))

    def dy_quarter(i):
        return (layer(i), 0, jnp.where(item(i) < OUT_FIRST, item(i), jnp.maximum(item(i) - IN_STEPS, OUT_FIRST)))

    operand = pl.BlockSpec((None, SEQ, D_MODEL), whole, pipeline_mode=pl.Buffered(1))
    in_specs = [operand] + [dproj_piece(ch) for ch in range(N_CHIP)]
    in_specs += [operand, pl.BlockSpec((None, SEQ, OUT_COLS), dy_quarter), hbm, _const_in((DEPTH, 8, D_MODEL))]
    args = [h, dproj, dproj, dproj, dproj, cat, dy, pack, dmod]
    out_shape = [jax.ShapeDtypeStruct((DEPTH, D_MODEL, W_IN_COLS), F32), jax.ShapeDtypeStruct((DEPTH, 2, HALF_OUT, D_MODEL), F32),
                 jax.ShapeDtypeStruct((DEPTH, PK_ROWS, HEAD), F32), jax.ShapeDtypeStruct((DEPTH, 24, N_DEV, HEAD), F32)]
    out_specs = [hbm, hbm, hbm, hbm]
    in_item = lambda *lead: pltpu.VMEM(lead + (HALF_IN, HEAD), BF16)
    out_item = lambda *lead: pltpu.VMEM(lead + (HALF_OUT, OUT_COLS), BF16)
    scratch = [
        pltpu.VMEM((IN_STEPS, HALF_IN, N_CHIP * HEAD), F32), pltpu.VMEM((IN_STEPS, HALF_IN, N_CHIP * HEAD), F32),
        pltpu.VMEM((IN_STEPS, HALF_IN, N_CHIP * HEAD), F32), in_item(IN_STEPS, 3), in_item(IN_STEPS, 2), in_item(IN_STEPS),
        pltpu.VMEM((IN_STEPS, HALF_IN, HEAD), F32),
        pltpu.VMEM((OUT_STEPS, N_CHIP, HALF_OUT, OUT_COLS), F32), pltpu.VMEM((OUT_STEPS, N_CHIP, HALF_OUT, OUT_COLS), F32),
        pltpu.VMEM((OUT_STEPS, N_CHIP, HALF_OUT, OUT_COLS), F32), out_item(OUT_STEPS, 3), out_item(OUT_STEPS, 2),
        out_item(OUT_STEPS), pltpu.VMEM((OUT_STEPS, HALF_OUT, OUT_COLS), F32),
        pltpu.SemaphoreType.DMA((N_STEPS,)), pltpu.SemaphoreType.DMA((N_STEPS,)),
        pltpu.SemaphoreType.DMA((2 * N_STEPS,)), pltpu.SemaphoreType.DMA((2 * N_STEPS,)),
        pltpu.SemaphoreType.DMA((N_STEPS,)), pltpu.SemaphoreType.DMA((N_STEPS,)),
        pltpu.SemaphoreType.DMA((N_STEPS,)), pltpu.SemaphoreType.DMA((N_STEPS,)), pltpu.SemaphoreType.DMA((N_STEPS,)),
    ]
    for _ in range(DEPTH):
        scratch += [
            pltpu.VMEM((PK_HALF, HEAD), F32), pltpu.VMEM((PK_HALF, HEAD), F32), pltpu.VMEM((N_CHIP, PK_PIECE, HEAD), F32),
            pltpu.VMEM((PK_PIECE, HEAD), F32), pltpu.VMEM((PK_ROWS, HEAD), F32),
            pltpu.VMEM((24, 1, HEAD), F32), pltpu.VMEM((24, N_DEV, HEAD), F32),
            pltpu.SemaphoreType.DMA((6,)),
            pltpu.SemaphoreType.DMA((N_CHIP,)), pltpu.SemaphoreType.DMA((N_CHIP,)),
            pltpu.SemaphoreType.DMA((N_CHIP,)), pltpu.SemaphoreType.DMA((N_CHIP,)),
            pltpu.SemaphoreType.DMA((N_DEV - 1,)), pltpu.SemaphoreType.DMA((N_DEV - 1,)),
        ]
    return pl.pallas_call(
        body, name="wgrad", grid=(N_STEPS,), in_specs=in_specs, out_specs=out_specs, out_shape=out_shape,
        scratch_shapes=scratch,
        compiler_params=pltpu.CompilerParams(dimension_semantics=("arbitrary",), vmem_limit_bytes=VMEM_LIMIT),
    )(*args)


def _adamw(w, g, m, v):
    m = ADAM_B1 * m + (1.0 - ADAM_B1) * g
    v = ADAM_B2 * v + (1.0 - ADAM_B2) * (g * g)
    m_hat = m / (1.0 - ADAM_B1 ** ADAM_STEP)
    v_hat = v / (1.0 - ADAM_B2 ** ADAM_STEP)
    delta = -ADAM_LR * (m_hat / (jnp.sqrt(v_hat) + ADAM_EPS) + ADAM_WD * w)
    return delta, m, v


def _adam_sharded(c_all, dmods, ada, w_in_set, w_out_set):
    rows = D_MODEL // ADAM_PARTS

    def body(c_ref, dm_ref, wa_ref, ma_ref, va_ref, wi_ref, gi_ref, mi_ref, vi_ref, wo_ref, go_ref, mo_ref, vo_ref,
             ga_out, da_out, ma_out, va_out, di_out, mi_out, vi_out, do_out, mo_out, vo_out):
        l = pl.program_id(0)
        chip = 2 * lax.axis_index("x") + lax.axis_index("y")
        cv = c_ref[...]
        silu_c = (cv * _sigmoid(cv)).astype(BF16).astype(F32)
        pieces = []
        for k in range(W_ADA_COLS // HEAD):
            dk = dm_ref[l, (W_ADA_COLS // HEAD) * chip + k].astype(BF16).astype(F32)
            pieces.append(_dot_exact(silu_c, dk, TN))
        g = jnp.concatenate(pieces, axis=1)
        ga_out[...] = g
        da_out[...], ma_out[...], va_out[...] = _adamw(wa_ref[...], g, ma_ref[...], va_ref[...])
        di_out[...], mi_out[...], vi_out[...] = _adamw(wi_ref[...], gi_ref[...], mi_ref[...], vi_ref[...])
        do_out[...], mo_out[...], vo_out[...] = _adamw(wo_ref[...], go_ref[...], mo_ref[...], vo_ref[...])

    def blk(r, cols):
        return pl.BlockSpec((None, r, cols), lambda l, i: (l, i, 0))

    b_ada, b_in, b_out = blk(rows, W_ADA_COLS), blk(rows, W_IN_COLS), blk(W_OUT_ROWS // ADAM_PARTS, D_MODEL)
    shapes = [jax.ShapeDtypeStruct(a[0].shape, F32) for a in (ada, w_in_set, w_out_set)]
    return pl.pallas_call(
        body, name="adam_sharded", grid=(DEPTH, ADAM_PARTS),
        in_specs=[pl.BlockSpec((N_DEV, rows), lambda l, i: (0, i)), _const_in((DEPTH, 24, N_DEV, HEAD))]
        + [b_ada] * 3 + [b_in] * 4 + [b_out] * 4,
        out_specs=[b_ada] * 4 + [b_in] * 3 + [b_out] * 3,
        out_shape=[shapes[0]] * 4 + [shapes[1]] * 3 + [shapes[2]] * 3,
        compiler_params=pltpu.CompilerParams(dimension_semantics=("arbitrary", "arbitrary"), vmem_limit_bytes=VMEM_LIMIT),
    )(c_all, dmods, *ada, *w_in_set, *w_out_set)


def _adam_small(packs, dmods, weights, ms, vs):
    n = len(weights)

    def body(*refs):
        pack_refs, dm_refs = refs[0], refs[1]
        b = 2
        w_refs, m_refs, v_refs = refs[b:b + n], refs[b + n:b + 2 * n], refs[b + 2 * n:b + 3 * n]
        outs = refs[b + 3 * n:]
        g_refs, d_refs, nm_refs, nv_refs = outs[0:n], outs[n:2 * n], outs[2 * n:3 * n], outs[3 * n:4 * n]
        outs[4 * n][...] = pack_refs.at[DEPTH - 1][PK_LOSS:PK_LOSS + 1, 0:1] * (0.5 / D_MODEL)

        def lanes(l, row0, count):
            return jnp.concatenate([pack_refs.at[l][row0 + k:row0 + k + 1, :] for k in range(count)], axis=1)

        def update(idx, at, g):
            g_refs[idx][at] = g
            d_refs[idx][at], nm_refs[idx][at], nv_refs[idx][at] = _adamw(w_refs[idx][at], g, m_refs[idx][at], v_refs[idx][at])

        for l in range(DEPTH):
            row = (slice(l, l + 1), slice(None))
            g_b = None
            for d in range(N_DEV):
                part = dm_refs.at[l][:, d, :]
                g_b = part if g_b is None else g_b + part
            update(0, row, jnp.concatenate([g_b[k:k + 1, :] for k in range(24)], axis=1))
            for g in range(N_HEAD):
                update(1, (l, g), pack_refs.at[l][PK_W_POOL + g * HEAD:PK_W_POOL + (g + 1) * HEAD, :])
                update(5, (l, g), pack_refs.at[l][PK_W_SGU + g * HEAD:PK_W_SGU + (g + 1) * HEAD, :])
            update(2, row, lanes(l, PK_POOL_SCALE, N_HEAD))
            update(3, (l,), pack_refs.at[l][PK_SGU_LN_G:PK_SGU_LN_G + N_HEAD, :])
            update(4, (l,), pack_refs.at[l][PK_SGU_LN_B:PK_SGU_LN_B + N_HEAD, :])
            update(6, (l,), pack_refs.at[l][PK_B_SGU:PK_B_SGU + N_HEAD, :])
            update(7, row, lanes(l, PK_LN_G, D_MODEL // HEAD))
            update(8, row, lanes(l, PK_LN_B, D_MODEL // HEAD))

    vmem = pl.BlockSpec(memory_space=pltpu.VMEM)
    shapes = [jax.ShapeDtypeStruct(w.shape, F32) for w in weights]
    return pl.pallas_call(
        body, name="adam_small", in_specs=[vmem] * (2 + 3 * n), out_specs=[vmem] * (4 * n + 1),
        out_shape=shapes * 4 + [jax.ShapeDtypeStruct((1, 1), F32)],
        compiler_params=pltpu.CompilerParams(vmem_limit_bytes=VMEM_LIMIT),
    )(packs, dmods, *weights, *ms, *vs)


def kernel(x, c, w_ada, b_ada, w_in, w_pool, pool_scale, sgu_ln_g, sgu_ln_b, w_sgu, b_sgu, w_out, ln_g, ln_b, loss_target, m_w_ada, m_b_ada, m_w_in, m_w_pool, m_pool_scale, m_sgu_ln_g, m_sgu_ln_b, m_w_sgu, m_b_sgu, m_w_out, m_ln_g, m_ln_b, v_w_ada, v_b_ada, v_w_in, v_w_pool, v_pool_scale, v_sgu_ln_g, v_sgu_ln_b, v_w_sgu, v_b_sgu, v_w_out, v_ln_g, v_ln_b):
    w_in0, w_in1, w_out0, w_out1, mod, c_all = _prepare(c, w_ada, b_ada, w_in, w_out)
    small = (w_pool, pool_scale, sgu_ln_g, sgu_ln_b, w_sgu, jnp.swapaxes(b_sgu, 1, 2))

    *saved0, w_in1, w_out1 = _forward_layer(0, x, mod, w_in0, w_out0, small, ln_g, ln_b, next_weights=(w_in1, w_out1))
    zn0 = saved0[3]
    *saved1, dout, sq = _forward_layer(1, zn0, mod, w_in1, w_out1, small, ln_g, ln_b, target=loss_target)

    dx1, *shared = _backward_layer(1, dout, saved1, mod, w_in1, w_out1, small, ln_g, sq=sq)
    dx0, h, cat, dy, dproj, pack, dmod = _backward_layer(0, dx1, saved0, mod, w_in0, w_out0, small, ln_g, shared=shared)
    g_in, g_out, pack, dmods = _wgrad_reduce(h, dproj, cat, dy, pack, dmod)

    g_out = g_out.reshape(DEPTH, W_OUT_ROWS, D_MODEL)
    big = _adam_sharded(c_all, dmods, (w_ada, m_w_ada, v_w_ada), (w_in, g_in, m_w_in, v_w_in), (w_out, g_out, m_w_out, v_w_out))
    ada, win, wout = big[0:4], (g_in, *big[4:7]), (g_out, *big[7:10])
    small_w = (b_ada, w_pool, pool_scale, sgu_ln_g, sgu_ln_b, w_sgu, b_sgu, ln_g, ln_b)
    small_m = (m_b_ada, m_w_pool, m_pool_scale, m_sgu_ln_g, m_sgu_ln_b, m_w_sgu, m_b_sgu, m_ln_g, m_ln_b)
    small_v = (v_b_ada, v_w_pool, v_pool_scale, v_sgu_ln_g, v_sgu_ln_b, v_w_sgu, v_b_sgu, v_ln_g, v_ln_b)
    res = _adam_small(pack, dmods, small_w, small_m, small_v)
    n = len(small_w)
    loss = res[4 * n].reshape(())

    def ordered(k):
        s = res[k * n:(k + 1) * n]
        return (ada[k], s[0], win[k], s[1], s[2], s[3], s[4], s[5], s[6], wout[k], s[7], s[8])

    return (loss, dx0[None], *ordered(0), *ordered(1), *ordered(2), *ordered(3))
```

```python
import jax
import jax.numpy as jnp
from jax import lax
from jax.experimental import pallas as pl
from jax.experimental.pallas import tpu as pltpu

F32 = jnp.float32
BF16 = jnp.bfloat16
MESH = pl.DeviceIdType.MESH

N_DEV = 8
N_CHIP = 4
DEPTH = 2
SEQ = 2048
D_MODEL = 1024
D_POOL = 512
D_PROJ = 2560
HEAD = 128
N_HEAD = 4
ROWS = 256
N_TILE = SEQ // ROWS
HALO = 16
W_IN_COLS = D_PROJ // N_CHIP
W_OUT_ROWS = D_MODEL // N_CHIP
W_ADA_COLS = 3 * D_MODEL // N_CHIP
HALF_IN = D_MODEL // 2
HALF_OUT = W_OUT_ROWS // 2
DEEPNORM_ALPHA = (2.0 * DEPTH) ** 0.25
LN_EPS = 1e-5
INV_SQRT2 = 0.7071067811865476
INV_SQRT_2PI = 0.3989422804014327

ADAM_LR = 0.001
ADAM_B1 = 0.9
ADAM_B2 = 0.999
ADAM_EPS = 1e-08
ADAM_WD = 0.01
ADAM_STEP = 10
ADAM_PARTS = 2

PK_W_POOL = 0
PK_W_SGU = 512
PK_POOL_SCALE = 1024
PK_SGU_LN_G = 1032
PK_SGU_LN_B = 1040
PK_B_SGU = 1048
PK_LN_G = 1056
PK_LN_B = 1064
PK_LOSS = 1072
PK_ROWS = 1088
PK_HALF = PK_ROWS // 2
PK_PIECE = PK_HALF // N_CHIP

VMEM_LIMIT = 56 * 1024 * 1024

GATHER_SECOND_ROUND_STEP = 1
GATHER_PASS_STEP = 3

NN = (((1,), (0,)), ((), ()))
NT = (((1,), (1,)), ((), ()))
TN = (((0,), (0,)), ((), ()))


def _dot(a, b, dims=NN):
    return lax.dot_general(a, b, dims, preferred_element_type=F32)


def _dot_exact(a, b, dims=NN):
    return lax.dot_general(a, b, dims, preferred_element_type=F32, precision=lax.Precision.HIGHEST)


def _layer_norm(v):
    mu = jnp.mean(v, axis=-1, keepdims=True)
    d = v - mu
    var = jnp.mean(d * d, axis=-1, keepdims=True)
    rstd = lax.rsqrt(var + LN_EPS)
    return d * rstd, rstd


def _layer_norm_bwd(dvhat, vhat, rstd):
    m1 = jnp.mean(dvhat, axis=-1, keepdims=True)
    m2 = jnp.mean(dvhat * vhat, axis=-1, keepdims=True)
    return rstd * (dvhat - m1 - vhat * m2)


def _sigmoid(v):
    return 1.0 / (1.0 + jnp.exp(-v))


def _gelu_parts(v):
    phi = 0.5 * (1.0 + lax.erf(v * INV_SQRT2))
    pdf = INV_SQRT_2PI * jnp.exp(-0.5 * v * v)
    return phi, pdf


def _sum_rows(v):
    return jnp.sum(v, axis=0, keepdims=True)


def _window_sums(ext, toward_later):
    n = ext.shape[0]

    def shifted(v, k):
        return pltpu.roll(v, (n - k) if toward_later else k, 0)

    s2 = ext + shifted(ext, 1)
    r4 = s2[:, HEAD:]
    s4 = r4 + shifted(r4, 2)
    r8 = s4[:, HEAD:]
    s8 = r8 + shifted(r8, 4)
    r16 = s8[:, HEAD:]
    s16 = r16 + shifted(r16, 8)
    return jnp.concatenate([s2[:, :HEAD], s4[:, :HEAD], s8[:, :HEAD], s16], axis=1)


def _window_counts(row0):
    t1 = row0 + 1 + lax.broadcasted_iota(jnp.int32, (ROWS, D_POOL), 0)
    lane = lax.broadcasted_iota(jnp.int32, (ROWS, D_POOL), 1)
    width = jnp.where(lane < HEAD, 2, jnp.where(lane < 2 * HEAD, 4, jnp.where(lane < 3 * HEAD, 8, 16)))
    return jnp.minimum(t1, width).astype(F32)


def _causal_mask():
    r = lax.broadcasted_iota(jnp.int32, (HEAD, HEAD), 0)
    s = lax.broadcasted_iota(jnp.int32, (HEAD, HEAD), 1)
    return r >= s


def _chunks_to_lanes(v):
    return jnp.concatenate([v[n * HEAD:(n + 1) * HEAD] for n in range(ROWS // HEAD)], axis=1)


def _lanes_to_chunks(v):
    return jnp.concatenate([v[:, n * HEAD:(n + 1) * HEAD] for n in range(ROWS // HEAD)], axis=0)


def _pack_stats(rstd_x, rstd_z, rstd_v):
    lane = lax.broadcasted_iota(jnp.int32, (ROWS, HEAD), 1)
    packed = rstd_x
    for k, r in enumerate([rstd_z] + list(rstd_v)):
        packed = jnp.where(lane < 16 * (k + 1), packed, r)
    return packed


def _unpack_stats(stats):
    cols = [stats[:, 16 * k:16 * k + 1] for k in range(2 + N_HEAD)]
    return cols[0], cols[1], cols[2:]


def _mixer(proj, halo, row0, wpool_ref, pscale, sgu_g_ref, sgu_b_ref, wsgu_ref, bsgu_t_ref, saved=None):
    xa = proj[:, 0:512]
    ga = proj[:, 512:1024]
    u = proj[:, 1024:1536]
    v = proj[:, 1536:2048]
    gb = proj[:, 2048:2560]
    ext = jnp.concatenate([halo, xa], axis=0)
    win = _window_sums(ext, toward_later=False)[HALO:]
    cnt = _window_counts(row0)
    pooled = (win / cnt - xa).astype(BF16)
    pw = jnp.concatenate(
        [_dot(pooled[:, g * HEAD:(g + 1) * HEAD], wpool_ref[g].astype(BF16)) for g in range(N_HEAD)], axis=1)
    sig_a = _sigmoid(ga) if saved is None else saved["sig_a"]
    ya = pw * pscale * (ga * sig_a)
    phi_u, pdf_u = _gelu_parts(u)
    phi_v, pdf_v = _gelu_parts(v)
    gu = u * phi_u
    gv = v * phi_v
    sig_b = _sigmoid(gb) if saved is None else saved["sig_b"]
    silu_b = gb * sig_b
    mask = _causal_mask()
    vhat, rstd_v, vln_l, mixed = [], [], [], []
    for h in range(N_HEAD):
        if saved is None:
            vh, rh = _layer_norm(gv[:, h * HEAD:(h + 1) * HEAD])
        else:
            vh, rh = saved["vhat"][h], saved["rstd_v"][h]
        ln = (vh * sgu_g_ref[h:h + 1, :] + sgu_b_ref[h:h + 1, :]).astype(BF16)
        ln_l = _chunks_to_lanes(ln)
        wm = jnp.where(mask, wsgu_ref[h], 0.0).astype(BF16)
        mx = _lanes_to_chunks(_dot(wm, ln_l) + bsgu_t_ref[:, h:h + 1])
        vhat.append(vh)
        rstd_v.append(rh)
        vln_l.append(ln_l)
        mixed.append(mx)
    mixed = jnp.concatenate(mixed, axis=1)
    yb = gu * mixed * silu_b
    return dict(xa=xa, ga=ga, u=u, gb=gb, cnt=cnt, pooled=pooled, pw=pw, sig_a=sig_a, ya=ya, phi_u=phi_u, pdf_u=pdf_u,
                phi_v=phi_v, pdf_v=pdf_v, gu=gu, sig_b=sig_b, silu_b=silu_b, vhat=vhat, rstd_v=rstd_v, vln_l=vln_l,
                mixed=mixed, yb=yb, mask=mask)


def _const(shape, *index):
    lead = tuple(index) + (0,) * (len(shape) - len(index))
    return pl.BlockSpec(shape, lambda *_: lead)


def _const_in(shape, *index):
    lead = tuple(index) + (0,) * (len(shape) - len(index))
    return pl.BlockSpec(shape, lambda *_: lead, pipeline_mode=pl.Buffered(1))


def _layer_weight_specs(l):
    return [
        _const_in((None, N_HEAD, HEAD, HEAD), l),
        _const_in((DEPTH, D_POOL)),
        _const_in((None, N_HEAD, HEAD), l),
        _const_in((None, N_HEAD, HEAD), l),
        _const_in((None, N_HEAD, HEAD, HEAD), l),
        _const_in((None, HEAD, N_HEAD), l),
    ]


def _forward_tile(l, i, x_ref, mod_ref, win_ref, wout_ref, small_refs, lng_ref, lnb_ref, carry_ref, saved_refs):
    wpool_ref, pscale_ref, sgu_g_ref, sgu_b_ref, wsgu_ref, bsgu_t_ref = small_refs
    proj_ref, y_ref, xn_ref, zn_ref, stats_ref, sig_ref, vhat_ref = saved_refs
    x = x_ref[...]
    if l > 0:
        x = x * lng_ref[l - 1:l, :] + lnb_ref[l - 1:l, :]
    shift, scale, gate = mod_ref[0:1, :], mod_ref[1:2, :], mod_ref[2:3, :]
    xn, rstd_x = _layer_norm(x)
    xn_ref[...] = xn
    h = xn * (1.0 + scale) + shift
    proj = _dot(h.astype(BF16), win_ref[...])
    proj_ref[...] = proj
    m = _mixer(proj, carry_ref[...], i * ROWS, wpool_ref, pscale_ref[l:l + 1, :], sgu_g_ref, sgu_b_ref, wsgu_ref, bsgu_t_ref)
    carry_ref[...] = m["xa"][ROWS - HALO:]
    sig_ref[...] = jnp.concatenate([m["sig_a"], m["sig_b"]], axis=1)
    vhat_ref[...] = jnp.concatenate(m["vhat"], axis=1)
    cat = jnp.concatenate([m["ya"], m["yb"]], axis=1).astype(BF16)
    y = _dot(cat, wout_ref[...])
    y_ref[...] = y
    zn, rstd_z = _layer_norm(DEEPNORM_ALPHA * x + gate * y)
    zn_ref[...] = zn
    stats_ref[...] = _pack_stats(rstd_x, rstd_z, m["rstd_v"])
    return zn


SAVED_COLS = (D_PROJ, D_MODEL, D_MODEL, D_MODEL, HEAD, D_MODEL, D_POOL)


def _saved_outputs():
    return ([jax.ShapeDtypeStruct((SEQ, cols), F32) for cols in SAVED_COLS],
            [pl.BlockSpec((ROWS, cols), lambda i: (i, 0)) for cols in SAVED_COLS])


def _forward_last(zn_prev, mod, w_in, w_out, small, ln_g, ln_b, target):
    l = DEPTH - 1
    n_saved = len(SAVED_COLS)

    def body(*refs):
        x_ref, mod_ref, win_ref, wout_ref = refs[:4]
        small_refs, lng_ref, lnb_ref, tgt_ref = refs[4:10], refs[10], refs[11], refs[12]
        saved_refs = refs[13:13 + n_saved]
        dout_ref, loss_ref, carry_ref = refs[13 + n_saved:]
        i = pl.program_id(0)

        @pl.when(i == 0)
        def _():
            carry_ref[...] = jnp.zeros_like(carry_ref)
            loss_ref[...] = jnp.zeros_like(loss_ref)

        zn = _forward_tile(l, i, x_ref, mod_ref, win_ref, wout_ref, small_refs, lng_ref, lnb_ref, carry_ref, saved_refs)
        err = zn * lng_ref[l:l + 1, :] + lnb_ref[l:l + 1, :] - tgt_ref[...]
        dout_ref[...] = err * (1.0 / D_MODEL)
        loss_ref[...] += jnp.sum(err * err)

    tile = pl.BlockSpec((ROWS, D_MODEL), lambda i: (i, 0))
    tile3 = pl.BlockSpec((None, ROWS, D_MODEL), lambda i: (0, i, 0))
    in_specs = [tile, _const_in((None, 8, D_MODEL), l), _const_in((D_MODEL, D_PROJ)), _const_in((D_MODEL, D_MODEL))]
    in_specs += _layer_weight_specs(l) + [_const_in((DEPTH, D_MODEL)), _const_in((DEPTH, D_MODEL)), tile3]
    out_shape, out_specs = _saved_outputs()
    out_shape += [jax.ShapeDtypeStruct((SEQ, D_MODEL), F32), jax.ShapeDtypeStruct((8, HEAD), F32)]
    out_specs += [tile, _const((8, HEAD))]
    return pl.pallas_call(
        body, name="fwd_last", grid=(N_TILE,), in_specs=in_specs, out_specs=out_specs, out_shape=out_shape,
        scratch_shapes=[pltpu.VMEM((HALO, D_POOL), F32)],
        compiler_params=pltpu.CompilerParams(dimension_semantics=("arbitrary",), vmem_limit_bytes=VMEM_LIMIT),
    )(zn_prev, mod, w_in, w_out, *small, ln_g, ln_b, target)


def _backward_layer(l, dout, saved, mod, w_in, w_out, small, ln_g, sq=None, shared=None):
    has_loss = sq is not None

    def body(*refs):
        (dout_ref, proj_ref, y_ref, xn_ref, zn_ref, stats_ref, sig_ref, vhat_ref, halo_ref, mod_ref, win_ref, wout_ref,
         wpool_ref, pscale_ref, sgu_g_ref, sgu_b_ref, wsgu_ref, bsgu_t_ref, lng_ref) = refs[:19]
        n_in = 20 if has_loss else 19 + 6
        dx_ref, h_ref, cat_ref, dy_ref, dproj_ref, pack_ref, dmod_ref, carry_ref = refs[n_in:n_in + 8]
        i = pl.program_id(0)
        tile = N_TILE - 1 - i

        @pl.when(i == 0)
        def _():
            carry_ref[...] = jnp.zeros_like(carry_ref)
            pack_ref[...] = jnp.zeros_like(pack_ref)
            dmod_ref[...] = jnp.zeros_like(dmod_ref)
            if has_loss:
                pack_ref[PK_LOSS:PK_LOSS + 8, :] = refs[19][...]

        xn = xn_ref[...]
        zn = zn_ref[...]
        y = y_ref[...]
        dout = dout_ref[...]
        rstd_x, rstd_z, rstd_v = _unpack_stats(stats_ref[...])
        kept = dict(sig_a=sig_ref[:, :D_POOL], sig_b=sig_ref[:, D_POOL:], rstd_v=rstd_v,
                    vhat=[vhat_ref[:, hd * HEAD:(hd + 1) * HEAD] for hd in range(N_HEAD)])
        pscale = pscale_ref[l:l + 1, :]
        shift, scale, gate = mod_ref[0:1, :], mod_ref[1:2, :], mod_ref[2:3, :]
        h = xn * (1.0 + scale) + shift
        h_ref[...] = h.astype(BF16)
        g_ln_g = _sum_rows(dout * zn)
        g_ln_b = _sum_rows(dout)
        dz = _layer_norm_bwd(dout * lng_ref[l:l + 1, :], zn, rstd_z)
        d_gate = _sum_rows(dz * y)
        dy = (gate * dz).astype(BF16)
        dy_ref[...] = dy

        halo = jnp.where(tile > 0, halo_ref[...], 0.0)
        m = _mixer(proj_ref[...], halo, tile * ROWS, wpool_ref, pscale, sgu_g_ref, sgu_b_ref, wsgu_ref, bsgu_t_ref,
                   saved=kept)
        cat_ref[...] = jnp.concatenate([m["ya"], m["yb"]], axis=1).astype(BF16)
        dcat = _dot(dy, wout_ref[...], NT)
        dya = dcat[:, :D_POOL]
        dyb = dcat[:, D_POOL:]

        ga, sig_a = m["ga"], m["sig_a"]
        dp = dya * (ga * sig_a)
        d_ga = dya * (m["pw"] * pscale) * (sig_a * (1.0 + ga * (1.0 - sig_a)))
        g_pscale = _sum_rows(dp * m["pw"])
        dpw = (dp * pscale).astype(BF16)
        dpooled = []
        for g in range(N_HEAD):
            cols = slice(g * HEAD, (g + 1) * HEAD)
            pack_ref[PK_W_POOL + g * HEAD:PK_W_POOL + (g + 1) * HEAD, :] += _dot(m["pooled"][:, cols], dpw[:, cols], TN)
            dpooled.append(_dot(dpw[:, cols], wpool_ref[g].astype(BF16), NT))
        dpooled = jnp.concatenate(dpooled, axis=1)
        q = dpooled / m["cnt"]
        ext = jnp.concatenate([q, carry_ref[...]], axis=0)
        d_xa = _window_sums(ext, toward_later=True)[:ROWS] - dpooled
        carry_ref[...] = q[:HALO]

        gu, mixed, silu_b, gb, sig_b = m["gu"], m["mixed"], m["silu_b"], m["gb"], m["sig_b"]
        d_mixed = dyb * gu * silu_b
        d_gu = dyb * mixed * silu_b
        d_gb = dyb * gu * mixed * (sig_b * (1.0 + gb * (1.0 - sig_b)))
        d_u = d_gu * (m["phi_u"] + m["u"] * m["pdf_u"])
        ones = jnp.ones((8, HEAD), F32)
        d_v = []
        for hd in range(N_HEAD):
            cols = slice(hd * HEAD, (hd + 1) * HEAD)
            dm = d_mixed[:, cols]
            dm_l = _chunks_to_lanes(dm.astype(BF16))
            g_w = _dot(dm_l, m["vln_l"][hd], NT)
            pack_ref[PK_W_SGU + hd * HEAD:PK_W_SGU + (hd + 1) * HEAD, :] += jnp.where(m["mask"], g_w, 0.0)
            dm_sum = dm[0:HEAD]
            for n in range(1, ROWS // HEAD):
                dm_sum = dm_sum + dm[n * HEAD:(n + 1) * HEAD]
            pack_ref[PK_B_SGU + hd:PK_B_SGU + hd + 1, :] += _dot_exact(ones, dm_sum, NT)[0:1]
            wm = jnp.where(m["mask"], wsgu_ref[hd], 0.0).astype(BF16)
            d_vln = _lanes_to_chunks(_dot(wm, dm_l, TN))
            vhat = m["vhat"][hd]
            pack_ref[PK_SGU_LN_G + hd:PK_SGU_LN_G + hd + 1, :] += _sum_rows(d_vln * vhat)
            pack_ref[PK_SGU_LN_B + hd:PK_SGU_LN_B + hd + 1, :] += _sum_rows(d_vln)
            d_v.append(_layer_norm_bwd(d_vln * sgu_g_ref[hd:hd + 1, :], vhat, m["rstd_v"][hd]))
        v = proj_ref[:, 1536:2048]
        d_v = jnp.concatenate(d_v, axis=1) * (m["phi_v"] + v * m["pdf_v"])

        dproj = jnp.concatenate([d_xa, d_ga, d_u, d_v, d_gb], axis=1).astype(BF16)
        dproj_ref[...] = dproj
        dh = _dot(dproj, win_ref[...], NT)
        d_scale = _sum_rows(dh * xn)
        d_shift = _sum_rows(dh)
        dx_ref[...] = DEEPNORM_ALPHA * dz + _layer_norm_bwd(dh * (1.0 + scale), xn, rstd_x)

        dmod_ref[0:1, :] += d_shift
        dmod_ref[1:2, :] += d_scale
        dmod_ref[2:3, :] += d_gate
        for g in range(N_HEAD):
            pack_ref[PK_POOL_SCALE + g:PK_POOL_SCALE + g + 1, :] += g_pscale[:, g * HEAD:(g + 1) * HEAD]
        for k in range(D_MODEL // HEAD):
            pack_ref[PK_LN_G + k:PK_LN_G + k + 1, :] += g_ln_g[:, k * HEAD:(k + 1) * HEAD]
            pack_ref[PK_LN_B + k:PK_LN_B + k + 1, :] += g_ln_b[:, k * HEAD:(k + 1) * HEAD]

    def rev(i):
        return (N_TILE - 1 - i, 0)

    tile = pl.BlockSpec((ROWS, D_MODEL), rev)
    halo = pl.BlockSpec((HALO, D_POOL), lambda i: (jnp.maximum((N_TILE - 1 - i) * (ROWS // HALO) - 1, 0), 0))
    in_specs = [tile] + [pl.BlockSpec((ROWS, a.shape[1]), rev) for a in saved] + [halo]
    in_specs += [_const_in((None, 8, D_MODEL), l), _const_in((D_MODEL, D_PROJ)), _const_in((D_MODEL, D_MODEL))]
    in_specs += _layer_weight_specs(l) + [_const_in((DEPTH, D_MODEL))]
    args = [dout, *saved, saved[0], mod, w_in, w_out, *small, ln_g]
    stacked = lambda cols: pl.BlockSpec((None, ROWS, cols), lambda i: (l, N_TILE - 1 - i, 0))
    out_shape = [jax.ShapeDtypeStruct((SEQ, D_MODEL), F32), jax.ShapeDtypeStruct((DEPTH, SEQ, D_MODEL), BF16),
                 jax.ShapeDtypeStruct((DEPTH, SEQ, D_MODEL), BF16), jax.ShapeDtypeStruct((DEPTH, SEQ, D_MODEL), BF16),
                 jax.ShapeDtypeStruct((DEPTH, SEQ, D_PROJ), BF16), jax.ShapeDtypeStruct((DEPTH, PK_ROWS, HEAD), F32),
                 jax.ShapeDtypeStruct((DEPTH, 8, D_MODEL), F32)]
    out_specs = [tile, stacked(D_MODEL), stacked(D_MODEL), stacked(D_MODEL), stacked(D_PROJ),
                 _const((None, PK_ROWS, HEAD), l), _const((None, 8, D_MODEL), l)]
    aliases = {}
    if has_loss:
        in_specs.append(_const_in((8, HEAD)))
        args.append(sq)
    else:
        aliases = {len(args) + k: 1 + k for k in range(len(shared))}
        in_specs += [pl.BlockSpec(memory_space=pl.ANY)] * len(shared)
        args += list(shared)
    return pl.pallas_call(
        body, name="bwd_last" if has_loss else "bwd_first", grid=(N_TILE,), in_specs=in_specs, out_specs=out_specs,
        out_shape=out_shape, scratch_shapes=[pltpu.VMEM((HALO, D_POOL), F32)], input_output_aliases=aliases,
        compiler_params=pltpu.CompilerParams(dimension_semantics=("arbitrary",), vmem_limit_bytes=VMEM_LIMIT),
    )(*args)


def _flip(v, f):
    return v + f - 2 * v * f


class _Place:
    def __init__(self):
        x, y, c = lax.axis_index("x"), lax.axis_index("y"), lax.axis_index("c")
        self.x, self.y, self.c = x, y, c
        self.chip = 2 * x + y
        self.dev = 4 * x + 2 * y + c
        self.sibling = (x, y, 1 - c)
        x1, y1 = _flip(x, 1 - c), _flip(y, c)
        x2, y2 = _flip(x, c), _flip(y, 1 - c)
        self.first = (x1, y1, c)
        self.second = (x2, y2, c)
        self.chip_first = 2 * x1 + y1
        self.chip_second = 2 * x2 + y2
        self.chip_far = 2 * (1 - x) + (1 - y)
        self.my_first_coord = jnp.where(c == 0, x, y)

    def first_coord(self, ch):
        return jnp.where(self.c == 0, ch // 2, ch % 2)

    def others(self):
        return [(_flip(self.x, (r >> 2) & 1), _flip(self.y, (r >> 1) & 1), _flip(self.c, r & 1)) for r in range(1, N_DEV)]

    def other_chips(self):
        return [(1 - self.x, self.y), (self.x, 1 - self.y), (1 - self.x, 1 - self.y)]


class _WeightGather:
    CHUNKS = 4
    N_SEMS = 12 * CHUNKS

    def __init__(self, place, win, wout, send, recv):
        self.p, self.win, self.wout, self.send, self.recv = place, win, wout, send, recv
        p = place
        self.plan = [(p.chip, p.first), (p.chip, p.second), (p.chip_first, p.second),
                     (p.chip_first, p.sibling), (p.chip_second, p.sibling), (p.chip_far, p.sibling)]

    def _copies(self, k, q):
        ch, target = self.plan[k]
        n_in, n_out = HALF_IN // self.CHUNKS, HALF_OUT // self.CHUNKS
        rows_in = pl.ds(pl.multiple_of(self.p.c * HALF_IN + q * n_in, n_in), n_in)
        cols_in = pl.ds(pl.multiple_of(ch * W_IN_COLS, 128), W_IN_COLS)
        rows_out = pl.ds(pl.multiple_of(ch * W_OUT_ROWS + self.p.c * HALF_OUT + q * n_out, n_out), n_out)
        r_in = self.win.at[rows_in, cols_in]
        r_out = self.wout.at[rows_out, :]
        s = 2 * (6 * q + k)
        return [pltpu.make_async_remote_copy(r_in, r_in, self.send.at[s], self.recv.at[s],
                                             device_id=target, device_id_type=MESH),
                pltpu.make_async_remote_copy(r_out, r_out, self.send.at[s + 1], self.recv.at[s + 1],
                                             device_id=target, device_id_type=MESH)]

    def _start(self, k, q):
        for cp in self._copies(k, q):
            cp.start()

    def _landed(self, k, q):
        for cp in self._copies(k, q):
            cp.wait_recv()

    def start_first_round(self):
        for q in range(self.CHUNKS):
            self._start(0, q)

    def start_second_round(self, q):
        self._landed(0, q)
        self._start(1, q)
        self._start(2, q)
        self._start(3, q)

    def pass_second_round(self, q):
        self._landed(1, q)
        self._start(4, q)
        self._landed(2, q)
        self._start(5, q)

    def finish(self):
        for q in range(self.CHUNKS):
            for k in (3, 4, 5):
                self._landed(k, q)
        for q in range(self.CHUNKS):
            for k in range(len(self.plan)):
                for cp in self._copies(k, q):
                    cp.wait_send()


def _forward_first(x, c_vec, w_ada, b_ada, w_in, w_out, small, ln_g, ln_b):
    n_saved = len(SAVED_COLS)

    def body(*refs):
        x_ref, c_ref, wada_hbm, bada_ref, win_hbm, wout_hbm = refs[:6]
        small_refs, lng_ref, lnb_ref = refs[6:12], refs[12], refs[13]
        saved_refs = refs[14:14 + n_saved]
        win0, wout0, win1, wout1, mod_out, c_out = refs[14 + n_saved:20 + n_saved]
        (carry_ref, wada_ref, win_ref, wout_ref, win_bf, wout_bf, mod_mine, mod_all, c_all, mod_ref, win_v, wout_v,
         g0_send, g0_recv, g1_send, g1_recv, c_send, c_recv, mod_send, mod_recv, local_sem) = refs[20 + n_saved:]
        i = pl.program_id(0)
        p = _Place()
        gather0 = _WeightGather(p, win0, wout0, g0_send, g0_recv)
        gather1 = _WeightGather(p, win1, wout1, g1_send, g1_recv)

        @pl.when(i == 0)
        def _():
            carry_ref[...] = jnp.zeros_like(carry_ref)
            loads = [pltpu.make_async_copy(win_hbm.at[0], win_ref.at[0], local_sem.at[4]),
                     pltpu.make_async_copy(wout_hbm.at[0], wout_ref.at[0], local_sem.at[5]),
                     pltpu.make_async_copy(win_hbm.at[1], win_ref.at[1], local_sem.at[6]),
                     pltpu.make_async_copy(wout_hbm.at[1], wout_ref.at[1], local_sem.at[7]),
                     pltpu.make_async_copy(wada_hbm, wada_ref, local_sem.at[8])]
            for cp in loads:
                cp.start()

            c_all[pl.ds(p.dev, 1), :] = c_ref[...]
            mine = c_all.at[pl.ds(p.dev, 1), :]
            c_copies = [pltpu.make_async_remote_copy(mine, mine, c_send.at[r], c_recv.at[r], device_id=d, device_id_type=MESH)
                        for r, d in enumerate(p.others())]
            for cp in c_copies:
                cp.start()

            cols = pl.ds(pl.multiple_of(p.chip * W_IN_COLS, 128), W_IN_COLS)
            rows = pl.ds(pl.multiple_of(p.chip * W_OUT_ROWS, W_OUT_ROWS), W_OUT_ROWS)
            own = [pltpu.make_async_copy(win_bf.at[0], win0.at[:, cols], local_sem.at[0]),
                   pltpu.make_async_copy(wout_bf.at[0], wout0.at[rows, :], local_sem.at[1]),
                   pltpu.make_async_copy(win_bf.at[1], win1.at[:, cols], local_sem.at[2]),
                   pltpu.make_async_copy(wout_bf.at[1], wout1.at[rows, :], local_sem.at[3])]
            for l in range(DEPTH):
                loads[2 * l].wait()
                win_bf[l] = win_ref[l].astype(BF16)
                own[2 * l].start()
                loads[2 * l + 1].wait()
                wout_bf[l] = wout_ref[l].astype(BF16)
                own[2 * l + 1].start()
                if l == 0:
                    own[0].wait()
                    own[1].wait()
                    gather0.start_first_round()
            for cp in c_copies:
                cp.wait()
            loads[4].wait()

            cv = c_all[...]
            c_out[...] = cv
            silu_c = (cv * _sigmoid(cv)).astype(BF16)
            for l in range(DEPTH):
                mod_mine[l] = _dot(silu_c, wada_ref[l].astype(BF16))
            mod_all[p.chip] = mod_mine[...]
            m_copies = [pltpu.make_async_remote_copy(mod_mine, mod_all.at[p.chip], mod_send.at[k], mod_recv.at[k],
                                                     device_id=(px, py, p.c), device_id_type=MESH)
                        for k, (px, py) in enumerate(p.other_chips())]
            for cp in m_copies:
                cp.start()
            for q in range(gather0.CHUNKS):
                gather0.start_second_round(q)
            own[2].wait()
            own[3].wait()
            gather1.start_first_round()
            for cp in m_copies:
                cp.wait()
            mod_ref[...] = jnp.zeros_like(mod_ref)
            for l in range(DEPTH):
                full = jnp.concatenate([mod_all[ch, l, pl.ds(p.dev, 1), :] for ch in range(N_CHIP)], axis=1) + bada_ref[l:l + 1, :]
                for k in range(3):
                    mod_ref[l, k:k + 1, :] = full[:, k * D_MODEL:(k + 1) * D_MODEL]
            mod_out[...] = mod_ref[...]
            for q in range(gather0.CHUNKS):
                gather0.pass_second_round(q)
            gather0.finish()
            fetch = [pltpu.make_async_copy(win0, win_v, local_sem.at[9]), pltpu.make_async_copy(wout0, wout_v, local_sem.at[10])]
            for cp in fetch:
                cp.start()
            for cp in fetch:
                cp.wait()

        for q in range(_WeightGather.CHUNKS):
            @pl.when(i == GATHER_SECOND_ROUND_STEP + q)
            def _(q=q):
                gather1.start_second_round(q)

            @pl.when(i == GATHER_PASS_STEP + q)
            def _(q=q):
                gather1.pass_second_round(q)

        _forward_tile(0, i, x_ref, mod_ref.at[0], win_v, wout_v, small_refs, lng_ref, lnb_ref, carry_ref, saved_refs)

        @pl.when(i == N_TILE - 1)
        def _():
            gather1.finish()

    hbm = pl.BlockSpec(memory_space=pl.ANY)
    tile3 = pl.BlockSpec((None, ROWS, D_MODEL), lambda i: (0, i, 0))
    in_specs = [tile3, _const_in((1, D_MODEL)), hbm, _const_in((DEPTH, 3 * D_MODEL)), hbm, hbm]
    in_specs += _layer_weight_specs(0) + [_const_in((DEPTH, D_MODEL)), _const_in((DEPTH, D_MODEL))]
    out_shape, out_specs = _saved_outputs()
    w_in_shape = jax.ShapeDtypeStruct((D_MODEL, D_PROJ), BF16)
    w_out_shape = jax.ShapeDtypeStruct((D_MODEL, D_MODEL), BF16)
    out_shape += [w_in_shape, w_out_shape, w_in_shape, w_out_shape,
                  jax.ShapeDtypeStruct((DEPTH, 8, D_MODEL), F32), jax.ShapeDtypeStruct((N_DEV, D_MODEL), F32)]
    out_specs += [hbm, hbm, hbm, hbm, _const((DEPTH, 8, D_MODEL)), _const((N_DEV, D_MODEL))]
    gather_sems = [pltpu.SemaphoreType.DMA((_WeightGather.N_SEMS,))] * 4
    scratch = [
        pltpu.VMEM((HALO, D_POOL), F32),
        pltpu.VMEM(w_ada.shape, F32), pltpu.VMEM(w_in.shape, F32), pltpu.VMEM(w_out.shape, F32),
        pltpu.VMEM((DEPTH, D_MODEL, W_IN_COLS), BF16), pltpu.VMEM((DEPTH, W_OUT_ROWS, D_MODEL), BF16),
        pltpu.VMEM((DEPTH, N_DEV, W_ADA_COLS), F32), pltpu.VMEM((N_CHIP, DEPTH, N_DEV, W_ADA_COLS), F32),
        pltpu.VMEM((N_DEV, D_MODEL), F32), pltpu.VMEM((DEPTH, 8, D_MODEL), F32),
        pltpu.VMEM((D_MODEL, D_PROJ), BF16), pltpu.VMEM((D_MODEL, D_MODEL), BF16),
    ] + gather_sems + [
        pltpu.SemaphoreType.DMA((7,)), pltpu.SemaphoreType.DMA((7,)),
        pltpu.SemaphoreType.DMA((3,)), pltpu.SemaphoreType.DMA((3,)),
        pltpu.SemaphoreType.DMA((11,)),
    ]
    return pl.pallas_call(
        body, name="fwd_first", grid=(N_TILE,), in_specs=in_specs, out_specs=out_specs, out_shape=out_shape,
        scratch_shapes=scratch,
        compiler_params=pltpu.CompilerParams(dimension_semantics=("arbitrary",), vmem_limit_bytes=VMEM_LIMIT),
    )(x, c_vec, w_ada, b_ada, w_in, w_out, *small, ln_g, ln_b)


IN_STEPS = W_IN_COLS // HEAD
OUT_STEPS = 4
OUT_COLS = D_MODEL // OUT_STEPS
OUT_FIRST = 2
ITEMS = ([("out", k) for k in range(OUT_FIRST)] + [("in", k) for k in range(IN_STEPS)]
         + [("out", k) for k in range(OUT_FIRST, OUT_STEPS)])
N_ITEMS = len(ITEMS)
N_STEPS = DEPTH * N_ITEMS
DELAY_SUM, DELAY_SECOND, DELAY_FINAL = 1, 3, 5
SMALL_SCATTER_STEP, SMALL_GATHER_STEP, SMALL_PASS_STEP, SMALL_FINISH_STEP = 1, 3, 5, 7


def _wgrad_reduce(h, dproj, cat, dy, pack, dmod):
    def body(*refs):
        h_ref, dp_refs, cat_ref, dy_ref, pack_ref, dmod_ref = refs[0], refs[1:5], refs[5], refs[6], refs[7], refs[8]
        fin_in, fin_out, pack_out, dmod_out = refs[9:13]
        scratch = refs[13:]
        (mine_in, send_in, sib_in, st_in, r1_in, r2_in, f_in,
         mine_out, send_out, sib_out, st_out, r1_out, r2_out, f_out,
         d2d_s, d2d_r, r1_s, r1_r, r2_s, r2_r, fin_l, fin_s, fin_r) = scratch[:23]
        p = _Place()
        c = p.c
        i = pl.program_id(0)
        my_rows = pl.ds(pl.multiple_of(c * HALF_IN, HALF_IN), HALF_IN)

        def layer_of(j):
            return DEPTH - 1 - j // N_ITEMS

        def bufs(j):
            kind, k = ITEMS[j % N_ITEMS]
            if kind == "in":
                return [r.at[k] for r in (mine_in, send_in, sib_in, st_in, r1_in, r2_in, f_in)]
            return [r.at[k] for r in (mine_out, send_out, sib_out, st_out, r1_out, r2_out, f_out)]

        def piece(j, ref, ch):
            if ITEMS[j % N_ITEMS][0] == "in":
                return ref.at[:, ch * HEAD:(ch + 1) * HEAD]
            return ref.at[ch]

        def slot(ch):
            return jnp.where(c == 0, ch % 2, ch // 2)

        def to_sibling(j):
            _, send, sib, _, _, _, _ = bufs(j)
            return pltpu.make_async_remote_copy(send, sib, d2d_s.at[j], d2d_r.at[j], device_id=p.sibling, device_id_type=MESH)

        def first_round(j, ch):
            _, _, _, st, r1, _, _ = bufs(j)
            k = slot(ch)
            return pltpu.make_async_remote_copy(st.at[k], r1.at[k], r1_s.at[2 * j + k], r1_r.at[2 * j + k],
                                                device_id=p.first, device_id_type=MESH)

        def second_round(j):
            _, _, _, st, _, r2, _ = bufs(j)
            return pltpu.make_async_remote_copy(st.at[2], r2, r2_s.at[j], r2_r.at[j], device_id=p.second, device_id_type=MESH)

        def finals(j):
            f = bufs(j)[6]
            kind, k = ITEMS[j % N_ITEMS]
            if kind == "in":
                dst = fin_in.at[layer_of(j), my_rows, k * HEAD:(k + 1) * HEAD]
            else:
                dst = fin_out.at[layer_of(j), c, :, k * OUT_COLS:(k + 1) * OUT_COLS]
            return [pltpu.make_async_copy(f, dst, fin_l.at[j]),
                    pltpu.make_async_remote_copy(f, dst, fin_s.at[j], fin_r.at[j], device_id=p.sibling, device_id_type=MESH)]

        def stage_sum(j):
            mine, _, sib, st, _, _, _ = bufs(j)
            to_sibling(j).wait_recv()
            mine[...] = mine[...] + sib[...]
            for ch in range(N_CHIP):
                @pl.when(p.first_coord(ch) != p.my_first_coord)
                def _(ch=ch):
                    st[slot(ch)] = piece(j, mine, ch)[...].astype(BF16)
                    first_round(j, ch).start()

        def stage_second(j):
            mine, _, _, st, r1, _, _ = bufs(j)
            for ch in range(N_CHIP):
                @pl.when(p.first_coord(ch) == p.my_first_coord)
                def _(ch=ch):
                    first_round(j, ch).wait_recv()
                    part = piece(j, mine, ch)
                    total = part[...] + r1[slot(ch)].astype(F32)
                    part[...] = total

                    @pl.when(ch != p.chip)
                    def _():
                        st[2] = total.astype(BF16)
                        second_round(j).start()

        def stage_final(j):
            mine, _, _, _, _, r2, f = bufs(j)
            second_round(j).wait_recv()
            for ch in range(N_CHIP):
                @pl.when(ch == p.chip)
                def _(ch=ch):
                    f[...] = piece(j, mine, ch)[...] + r2[...].astype(F32)
            for cp in finals(j):
                cp.start()

        def drain(j):
            to_sibling(j).wait_send()
            for ch in range(N_CHIP):
                @pl.when(p.first_coord(ch) != p.my_first_coord)
                def _(ch=ch):
                    first_round(j, ch).wait_send()

                @pl.when(jnp.logical_and(p.first_coord(ch) == p.my_first_coord, ch != p.chip))
                def _():
                    second_round(j).wait_send()
            for cp in finals(j):
                cp.wait()

        dev = p.dev
        devices = p.others()

        def half(core):
            return pl.ds(pl.multiple_of(core * PK_HALF, 8), PK_HALF)

        def finished(core, ch):
            return pl.ds(pl.multiple_of(core * PK_HALF + ch * PK_PIECE, 8), PK_PIECE)

        def small_exchange(l, first_step, bufs_l):
            (pk_mine, pk_sib, pk_rs, pk_fin, pk_all, dm_st, dm_all, pk_sem, rs_s, rs_r, ag_s, ag_r, dm_s, dm_r) = bufs_l

            def pk_load():
                return pltpu.make_async_copy(pack_ref.at[l, half(c)], pk_mine, pk_sem.at[0])

            def pk_give():
                return pltpu.make_async_remote_copy(pack_ref.at[l, half(1 - c)], pk_sib, pk_sem.at[1], pk_sem.at[2],
                                                    device_id=p.sibling, device_id_type=MESH)

            def pk_scatter(ch):
                return pltpu.make_async_remote_copy(pk_mine.at[ch * PK_PIECE:(ch + 1) * PK_PIECE], pk_rs.at[p.chip],
                                                    rs_s.at[ch], rs_r.at[p.chip], device_id=(ch // 2, ch % 2, c),
                                                    device_id_type=MESH)

            def pk_spread(ch):
                return pltpu.make_async_remote_copy(pk_fin, pk_all.at[finished(c, p.chip)], ag_s.at[ch], ag_r.at[p.chip],
                                                    device_id=(ch // 2, ch % 2, c), device_id_type=MESH)

            def pk_pass():
                return pltpu.make_async_remote_copy(pk_all.at[half(c)], pk_all.at[half(c)], pk_sem.at[3], pk_sem.at[4],
                                                    device_id=p.sibling, device_id_type=MESH)

            def dm_copy(r):
                return pltpu.make_async_remote_copy(dm_st, dm_all.at[:, pl.ds(dev, 1), :], dm_s.at[r], dm_r.at[r],
                                                    device_id=devices[r], device_id_type=MESH)

            def results():
                return [pltpu.make_async_copy(pk_all, pack_out.at[l], pk_sem.at[0]),
                        pltpu.make_async_copy(dm_all, dmod_out.at[l], pk_sem.at[5])]

            @pl.when(i == first_step)
            def _():
                pk_load().start()
                pk_give().start()
                for k in range(3):
                    for r in range(D_MODEL // HEAD):
                        dm_st[8 * k + r] = dmod_ref[l, k:k + 1, r * HEAD:(r + 1) * HEAD]
                dm_all[:, pl.ds(dev, 1), :] = dm_st[...]
                for r in range(N_DEV - 1):
                    dm_copy(r).start()

            @pl.when(i == first_step + SMALL_SCATTER_STEP)
            def _():
                pk_load().wait()
                pk_give().wait()
                pk_mine[...] = pk_mine[...] + pk_sib[...]
                for ch in range(N_CHIP):
                    @pl.when(ch != p.chip)
                    def _(ch=ch):
                        pk_scatter(ch).start()

            @pl.when(i == first_step + SMALL_GATHER_STEP)
            def _():
                for ch in range(N_CHIP):
                    @pl.when(ch != p.chip)
                    def _(ch=ch):
                        pltpu.make_async_remote_copy(pk_fin, pk_rs.at[ch], rs_s.at[ch], rs_r.at[ch],
                                                     device_id=p.sibling, device_id_type=MESH).wait_recv()
                for me in range(N_CHIP):
                    @pl.when(me == p.chip)
                    def _(me=me):
                        total = None
                        for ch in range(N_CHIP):
                            part = pk_mine[me * PK_PIECE:(me + 1) * PK_PIECE] if ch == me else pk_rs[ch]
                            total = part if total is None else total + part
                        pk_fin[...] = total
                        pk_all[finished(c, me)] = total
                for ch in range(N_CHIP):
                    @pl.when(ch != p.chip)
                    def _(ch=ch):
                        pk_spread(ch).start()

            @pl.when(i == first_step + SMALL_PASS_STEP)
            def _():
                for ch in range(N_CHIP):
                    @pl.when(ch != p.chip)
                    def _(ch=ch):
                        pltpu.make_async_remote_copy(pk_fin, pk_all.at[finished(c, ch)], ag_s.at[ch], ag_r.at[ch],
                                                     device_id=p.sibling, device_id_type=MESH).wait_recv()
                pk_pass().start()

            @pl.when(i == first_step + SMALL_FINISH_STEP)
            def _():
                pk_pass().wait()
                for ch in range(N_CHIP):
                    @pl.when(ch != p.chip)
                    def _(ch=ch):
                        pk_scatter(ch).wait_send()
                        pk_spread(ch).wait_send()
                for r in range(N_DEV - 1):
                    dm_copy(r).wait()
                for cp in results():
                    cp.start()
                for cp in results():
                    cp.wait()

        n_small = 14
        for l in range(DEPTH):
            small_exchange(l, (DEPTH - 1 - l) * N_ITEMS, scratch[23 + n_small * l:23 + n_small * (l + 1)])

        for step in range(N_ITEMS, N_STEPS):
            @pl.when(i == step)
            def _(step=step):
                drain(step - N_ITEMS)

        ii = jnp.where(i < N_ITEMS, i, i - N_ITEMS)
        in_step = jnp.logical_and(ii >= OUT_FIRST, ii < OUT_FIRST + IN_STEPS)

        @pl.when(in_step)
        def _():
            k = ii - OUT_FIRST
            rhs = jnp.concatenate([r[...] for r in dp_refs], axis=1)
            res = _dot(h_ref[...], rhs, TN)

            @pl.when(c == 0)
            def _():
                mine_in[k] = res[:HALF_IN]
                send_in[k] = res[HALF_IN:]

            @pl.when(c == 1)
            def _():
                mine_in[k] = res[HALF_IN:]
                send_in[k] = res[:HALF_IN]

        @pl.when(jnp.logical_not(in_step))
        def _():
            k = jnp.where(ii < OUT_FIRST, ii, ii - IN_STEPS)
            res = _dot(cat_ref[...], dy_ref[...], TN)

            @pl.when(c == 0)
            def _():
                for ch in range(N_CHIP):
                    mine_out[k, ch] = res[ch * W_OUT_ROWS:ch * W_OUT_ROWS + HALF_OUT]
                    send_out[k, ch] = res[ch * W_OUT_ROWS + HALF_OUT:(ch + 1) * W_OUT_ROWS]

            @pl.when(c == 1)
            def _():
                for ch in range(N_CHIP):
                    mine_out[k, ch] = res[ch * W_OUT_ROWS + HALF_OUT:(ch + 1) * W_OUT_ROWS]
                    send_out[k, ch] = res[ch * W_OUT_ROWS:ch * W_OUT_ROWS + HALF_OUT]

        stages = ((0, lambda j: to_sibling(j).start()), (DELAY_SUM, stage_sum), (DELAY_SECOND, stage_second),
                  (DELAY_FINAL, stage_final))
        for step in range(N_STEPS):
            @pl.when(i == step)
            def _(step=step):
                for delay, stage in stages:
                    if step - delay >= 0:
                        stage(step - delay)

        @pl.when(i == N_STEPS - 1)
        def _():
            for step in range(N_STEPS, N_STEPS + DELAY_FINAL):
                for delay, stage in stages:
                    if 0 <= step - delay < N_STEPS:
                        stage(step - delay)
            for j in range(N_STEPS - N_ITEMS, N_STEPS):
                drain(j)

    hbm = pl.BlockSpec(memory_space=pl.ANY)

    def layer(i):
        return jnp.where(i < N_ITEMS, DEPTH - 1, 0)

    def item(i):
        return jnp.where(i < N_ITEMS, i, i - N_ITEMS)

    def whole(i):
        return (layer(i), 0, 0)

    def dproj_piece(ch):
        return pl.BlockSpec((None, SEQ, HEAD),
                            lambda i: (layer(i), 0, ch * IN_STEPS + jnp.clip(item(i) - OUT_FIRST, 0, IN_STEPS - 1)))

    def dy_quarter(i):
        return (layer(i), 0, jnp.where(item(i) < OUT_FIRST, item(i), jnp.maximum(item(i) - IN_STEPS, OUT_FIRST)))

    operand = pl.BlockSpec((None, SEQ, D_MODEL), whole, pipeline_mode=pl.Buffered(1))
    in_specs = [operand] + [dproj_piece(ch) for ch in range(N_CHIP)]
    in_specs += [operand, pl.BlockSpec((None, SEQ, OUT_COLS), dy_quarter), hbm, _const_in((DEPTH, 8, D_MODEL))]
    args = [h, dproj, dproj, dproj, dproj, cat, dy, pack, dmod]
    out_shape = [jax.ShapeDtypeStruct((DEPTH, D_MODEL, W_IN_COLS), F32), jax.ShapeDtypeStruct((DEPTH, 2, HALF_OUT, D_MODEL), F32),
                 jax.ShapeDtypeStruct((DEPTH, PK_ROWS, HEAD), F32), jax.ShapeDtypeStruct((DEPTH, 24, N_DEV, HEAD), F32)]
    out_specs = [hbm, hbm, hbm, hbm]
    in_item = lambda *lead: pltpu.VMEM(lead + (HALF_IN, HEAD), BF16)
    out_item = lambda *lead: pltpu.VMEM(lead + (HALF_OUT, OUT_COLS), BF16)
    scratch = [
        pltpu.VMEM((IN_STEPS, HALF_IN, N_CHIP * HEAD), F32), pltpu.VMEM((IN_STEPS, HALF_IN, N_CHIP * HEAD), F32),
        pltpu.VMEM((IN_STEPS, HALF_IN, N_CHIP * HEAD), F32), in_item(IN_STEPS, 3), in_item(IN_STEPS, 2), in_item(IN_STEPS),
        pltpu.VMEM((IN_STEPS, HALF_IN, HEAD), F32),
        pltpu.VMEM((OUT_STEPS, N_CHIP, HALF_OUT, OUT_COLS), F32), pltpu.VMEM((OUT_STEPS, N_CHIP, HALF_OUT, OUT_COLS), F32),
        pltpu.VMEM((OUT_STEPS, N_CHIP, HALF_OUT, OUT_COLS), F32), out_item(OUT_STEPS, 3), out_item(OUT_STEPS, 2),
        out_item(OUT_STEPS), pltpu.VMEM((OUT_STEPS, HALF_OUT, OUT_COLS), F32),
        pltpu.SemaphoreType.DMA((N_STEPS,)), pltpu.SemaphoreType.DMA((N_STEPS,)),
        pltpu.SemaphoreType.DMA((2 * N_STEPS,)), pltpu.SemaphoreType.DMA((2 * N_STEPS,)),
        pltpu.SemaphoreType.DMA((N_STEPS,)), pltpu.SemaphoreType.DMA((N_STEPS,)),
        pltpu.SemaphoreType.DMA((N_STEPS,)), pltpu.SemaphoreType.DMA((N_STEPS,)), pltpu.SemaphoreType.DMA((N_STEPS,)),
    ]
    for _ in range(DEPTH):
        scratch += [
            pltpu.VMEM((PK_HALF, HEAD), F32), pltpu.VMEM((PK_HALF, HEAD), F32), pltpu.VMEM((N_CHIP, PK_PIECE, HEAD), F32),
            pltpu.VMEM((PK_PIECE, HEAD), F32), pltpu.VMEM((PK_ROWS, HEAD), F32),
            pltpu.VMEM((24, 1, HEAD), F32), pltpu.VMEM((24, N_DEV, HEAD), F32),
            pltpu.SemaphoreType.DMA((6,)),
            pltpu.SemaphoreType.DMA((N_CHIP,)), pltpu.SemaphoreType.DMA((N_CHIP,)),
            pltpu.SemaphoreType.DMA((N_CHIP,)), pltpu.SemaphoreType.DMA((N_CHIP,)),
            pltpu.SemaphoreType.DMA((N_DEV - 1,)), pltpu.SemaphoreType.DMA((N_DEV - 1,)),
        ]
    return pl.pallas_call(
        body, name="wgrad", grid=(N_STEPS,), in_specs=in_specs, out_specs=out_specs, out_shape=out_shape,
        scratch_shapes=scratch,
        compiler_params=pltpu.CompilerParams(dimension_semantics=("arbitrary",), vmem_limit_bytes=VMEM_LIMIT),
    )(*args)


def _adamw(w, g, m, v):
    m = ADAM_B1 * m + (1.0 - ADAM_B1) * g
    v = ADAM_B2 * v + (1.0 - ADAM_B2) * (g * g)
    m_hat = m / (1.0 - ADAM_B1 ** ADAM_STEP)
    v_hat = v / (1.0 - ADAM_B2 ** ADAM_STEP)
    delta = -ADAM_LR * (m_hat / (jnp.sqrt(v_hat) + ADAM_EPS) + ADAM_WD * w)
    return delta, m, v


def _adam_sharded(c_all, dmods, ada, w_in_set, w_out_set):
    rows = D_MODEL // ADAM_PARTS

    def body(c_ref, dm_ref, wa_ref, ma_ref, va_ref, wi_ref, gi_ref, mi_ref, vi_ref, wo_ref, go_ref, mo_ref, vo_ref,
             ga_out, da_out, ma_out, va_out, di_out, mi_out, vi_out, do_out, mo_out, vo_out):
        l = pl.program_id(0)
        chip = 2 * lax.axis_index("x") + lax.axis_index("y")
        cv = c_ref[...]
        silu_c = (cv * _sigmoid(cv)).astype(BF16).astype(F32)
        pieces = []
        for k in range(W_ADA_COLS // HEAD):
            dk = dm_ref[l, (W_ADA_COLS // HEAD) * chip + k].astype(BF16).astype(F32)
            pieces.append(_dot_exact(silu_c, dk, TN))
        g = jnp.concatenate(pieces, axis=1)
        ga_out[...] = g
        da_out[...], ma_out[...], va_out[...] = _adamw(wa_ref[...], g, ma_ref[...], va_ref[...])
        di_out[...], mi_out[...], vi_out[...] = _adamw(wi_ref[...], gi_ref[...], mi_ref[...], vi_ref[...])
        do_out[...], mo_out[...], vo_out[...] = _adamw(wo_ref[...], go_ref[...], mo_ref[...], vo_ref[...])

    def blk(r, cols):
        return pl.BlockSpec((None, r, cols), lambda l, i: (l, i, 0))

    b_ada, b_in, b_out = blk(rows, W_ADA_COLS), blk(rows, W_IN_COLS), blk(W_OUT_ROWS // ADAM_PARTS, D_MODEL)
    shapes = [jax.ShapeDtypeStruct(a[0].shape, F32) for a in (ada, w_in_set, w_out_set)]
    return pl.pallas_call(
        body, name="adam_sharded", grid=(DEPTH, ADAM_PARTS),
        in_specs=[pl.BlockSpec((N_DEV, rows), lambda l, i: (0, i)), _const_in((DEPTH, 24, N_DEV, HEAD))]
        + [b_ada] * 3 + [b_in] * 4 + [b_out] * 4,
        out_specs=[b_ada] * 4 + [b_in] * 3 + [b_out] * 3,
        out_shape=[shapes[0]] * 4 + [shapes[1]] * 3 + [shapes[2]] * 3,
        compiler_params=pltpu.CompilerParams(dimension_semantics=("arbitrary", "arbitrary"), vmem_limit_bytes=VMEM_LIMIT),
    )(c_all, dmods, *ada, *w_in_set, *w_out_set)


def _adam_small(packs, dmods, weights, ms, vs):
    n = len(weights)

    def body(*refs):
        pack_refs, dm_refs = refs[0], refs[1]
        b = 2
        w_refs, m_refs, v_refs = refs[b:b + n], refs[b + n:b + 2 * n], refs[b + 2 * n:b + 3 * n]
        outs = refs[b + 3 * n:]
        g_refs, d_refs, nm_refs, nv_refs = outs[0:n], outs[n:2 * n], outs[2 * n:3 * n], outs[3 * n:4 * n]
        outs[4 * n][...] = pack_refs.at[DEPTH - 1][PK_LOSS:PK_LOSS + 1, 0:1] * (0.5 / D_MODEL)

        def lanes(l, row0, count):
            return jnp.concatenate([pack_refs.at[l][row0 + k:row0 + k + 1, :] for k in range(count)], axis=1)

        def update(idx, at, g):
            g_refs[idx][at] = g
            d_refs[idx][at], nm_refs[idx][at], nv_refs[idx][at] = _adamw(w_refs[idx][at], g, m_refs[idx][at], v_refs[idx][at])

        for l in range(DEPTH):
            row = (slice(l, l + 1), slice(None))
            g_b = None
            for d in range(N_DEV):
                part = dm_refs.at[l][:, d, :]
                g_b = part if g_b is None else g_b + part
            update(0, row, jnp.concatenate([g_b[k:k + 1, :] for k in range(24)], axis=1))
            for g in range(N_HEAD):
                update(1, (l, g), pack_refs.at[l][PK_W_POOL + g * HEAD:PK_W_POOL + (g + 1) * HEAD, :])
                update(5, (l, g), pack_refs.at[l][PK_W_SGU + g * HEAD:PK_W_SGU + (g + 1) * HEAD, :])
            update(2, row, lanes(l, PK_POOL_SCALE, N_HEAD))
            update(3, (l,), pack_refs.at[l][PK_SGU_LN_G:PK_SGU_LN_G + N_HEAD, :])
            update(4, (l,), pack_refs.at[l][PK_SGU_LN_B:PK_SGU_LN_B + N_HEAD, :])
            update(6, (l,), pack_refs.at[l][PK_B_SGU:PK_B_SGU + N_HEAD, :])
            update(7, row, lanes(l, PK_LN_G, D_MODEL // HEAD))
            update(8, row, lanes(l, PK_LN_B, D_MODEL // HEAD))

    vmem = pl.BlockSpec(memory_space=pltpu.VMEM)
    shapes = [jax.ShapeDtypeStruct(w.shape, F32) for w in weights]
    return pl.pallas_call(
        body, name="adam_small", in_specs=[vmem] * (2 + 3 * n), out_specs=[vmem] * (4 * n + 1),
        out_shape=shapes * 4 + [jax.ShapeDtypeStruct((1, 1), F32)],
        compiler_params=pltpu.CompilerParams(vmem_limit_bytes=VMEM_LIMIT),
    )(packs, dmods, *weights, *ms, *vs)


def kernel(x, c, w_ada, b_ada, w_in, w_pool, pool_scale, sgu_ln_g, sgu_ln_b, w_sgu, b_sgu, w_out, ln_g, ln_b, loss_target, m_w_ada, m_b_ada, m_w_in, m_w_pool, m_pool_scale, m_sgu_ln_g, m_sgu_ln_b, m_w_sgu, m_b_sgu, m_w_out, m_ln_g, m_ln_b, v_w_ada, v_b_ada, v_w_in, v_w_pool, v_pool_scale, v_sgu_ln_g, v_sgu_ln_b, v_w_sgu, v_b_sgu, v_w_out, v_ln_g, v_ln_b):
    small = (w_pool, pool_scale, sgu_ln_g, sgu_ln_b, w_sgu, jnp.swapaxes(b_sgu, 1, 2))
    *saved0, w_in0, w_out0, w_in1, w_out1, mod, c_all = _forward_first(x, c, w_ada, b_ada, w_in, w_out, small, ln_g, ln_b)
    *saved1, dout, sq = _forward_last(saved0[3], mod, w_in1, w_out1, small, ln_g, ln_b, loss_target)

    dx1, *shared = _backward_layer(1, dout, saved1, mod, w_in1, w_out1, small, ln_g, sq=sq)
    dx0, h, cat, dy, dproj, pack, dmod = _backward_layer(0, dx1, saved0, mod, w_in0, w_out0, small, ln_g, shared=shared)
    g_in, g_out, pack, dmods = _wgrad_reduce(h, dproj, cat, dy, pack, dmod)

    g_out = g_out.reshape(DEPTH, W_OUT_ROWS, D_MODEL)
    big = _adam_sharded(c_all, dmods, (w_ada, m_w_ada, v_w_ada), (w_in, g_in, m_w_in, v_w_in), (w_out, g_out, m_w_out, v_w_out))
    ada, win, wout = big[0:4], (g_in, *big[4:7]), (g_out, *big[7:10])
    small_w = (b_ada, w_pool, pool_scale, sgu_ln_g, sgu_ln_b, w_sgu, b_sgu, ln_g, ln_b)
    small_m = (m_b_ada, m_w_pool, m_pool_scale, m_sgu_ln_g, m_sgu_ln_b, m_w_sgu, m_b_sgu, m_ln_g, m_ln_b)
    small_v = (v_b_ada, v_w_pool, v_pool_scale, v_sgu_ln_g, v_sgu_ln_b, v_w_sgu, v_b_sgu, v_ln_g, v_ln_b)
    res = _adam_small(pack, dmods, small_w, small_m, small_v)
    n = len(small_w)
    loss = res[4 * n].reshape(())

    def ordered(k):
        s = res[k * n:(k + 1) * n]
        return (ada[k], s[0], win[k], s[1], s[2], s[3], s[4], s[5], s[6], wout[k], s[7], s[8])

    return (loss, dx0[None], *ordered(0), *ordered(1), *ordered(2), *ordered(3))
```

```python
import jax
import jax.numpy as jnp
from jax import lax
from jax.experimental import pallas as pl
from jax.experimental.pallas import tpu as pltpu

F32 = jnp.float32
BF16 = jnp.bfloat16
MESH = pl.DeviceIdType.MESH

N_DEV = 8
N_CHIP = 4
DEPTH = 2
SEQ = 2048
D_MODEL = 1024
D_POOL = 512
D_PROJ = 2560
HEAD = 128
N_HEAD = 4
ROWS = 256
N_TILE = SEQ // ROWS
HALO = 16
W_IN_COLS = D_PROJ // N_CHIP
W_OUT_ROWS = D_MODEL // N_CHIP
W_ADA_COLS = 3 * D_MODEL // N_CHIP
HALF_IN = D_MODEL // 2
HALF_OUT = W_OUT_ROWS // 2
DEEPNORM_ALPHA = (2.0 * DEPTH) ** 0.25
LN_EPS = 1e-5
INV_SQRT2 = 0.7071067811865476
INV_SQRT_2PI = 0.3989422804014327

ADAM_LR = 0.001
ADAM_B1 = 0.9
ADAM_B2 = 0.999
ADAM_EPS = 1e-08
ADAM_WD = 0.01
ADAM_STEP = 10
ADAM_PARTS = 2

PK_W_POOL = 0
PK_W_SGU = 512
PK_POOL_SCALE = 1024
PK_SGU_LN_G = 1032
PK_SGU_LN_B = 1040
PK_B_SGU = 1048
PK_LN_G = 1056
PK_LN_B = 1064
PK_LOSS = 1072
PK_ROWS = 1088
PK_HALF = PK_ROWS // 2
PK_PIECE = PK_HALF // N_CHIP

VMEM_LIMIT = 56 * 1024 * 1024

GATHER_SECOND_ROUND_STEP = 1
GATHER_PASS_STEP = 3

NN = (((1,), (0,)), ((), ()))
NT = (((1,), (1,)), ((), ()))
TN = (((0,), (0,)), ((), ()))


def _dot(a, b, dims=NN):
    return lax.dot_general(a, b, dims, preferred_element_type=F32)


def _dot_exact(a, b, dims=NN):
    return lax.dot_general(a, b, dims, preferred_element_type=F32, precision=lax.Precision.HIGHEST)


def _layer_norm(v):
    mu = jnp.mean(v, axis=-1, keepdims=True)
    d = v - mu
    var = jnp.mean(d * d, axis=-1, keepdims=True)
    rstd = lax.rsqrt(var + LN_EPS)
    return d * rstd, rstd


def _layer_norm_bwd(dvhat, vhat, rstd):
    m1 = jnp.mean(dvhat, axis=-1, keepdims=True)
    m2 = jnp.mean(dvhat * vhat, axis=-1, keepdims=True)
    return rstd * (dvhat - m1 - vhat * m2)


def _sigmoid(v):
    return 1.0 / (1.0 + jnp.exp(-v))


def _gelu_parts(v):
    phi = 0.5 * (1.0 + lax.erf(v * INV_SQRT2))
    pdf = INV_SQRT_2PI * jnp.exp(-0.5 * v * v)
    return phi, pdf


def _sum_rows(v):
    return jnp.sum(v, axis=0, keepdims=True)


def _window_sums(ext, toward_later):
    n = ext.shape[0]

    def shifted(v, k):
        return pltpu.roll(v, (n - k) if toward_later else k, 0)

    s2 = ext + shifted(ext, 1)
    r4 = s2[:, HEAD:]
    s4 = r4 + shifted(r4, 2)
    r8 = s4[:, HEAD:]
    s8 = r8 + shifted(r8, 4)
    r16 = s8[:, HEAD:]
    s16 = r16 + shifted(r16, 8)
    return jnp.concatenate([s2[:, :HEAD], s4[:, :HEAD], s8[:, :HEAD], s16], axis=1)


def _window_counts(row0):
    t1 = row0 + 1 + lax.broadcasted_iota(jnp.int32, (ROWS, D_POOL), 0)
    lane = lax.broadcasted_iota(jnp.int32, (ROWS, D_POOL), 1)
    width = jnp.where(lane < HEAD, 2, jnp.where(lane < 2 * HEAD, 4, jnp.where(lane < 3 * HEAD, 8, 16)))
    return jnp.minimum(t1, width).astype(F32)


def _causal_mask():
    r = lax.broadcasted_iota(jnp.int32, (HEAD, HEAD), 0)
    s = lax.broadcasted_iota(jnp.int32, (HEAD, HEAD), 1)
    return r >= s


def _chunks_to_lanes(v):
    return jnp.concatenate([v[n * HEAD:(n + 1) * HEAD] for n in range(ROWS // HEAD)], axis=1)


def _lanes_to_chunks(v):
    return jnp.concatenate([v[:, n * HEAD:(n + 1) * HEAD] for n in range(ROWS // HEAD)], axis=0)


def _pack_stats(rstd_x, rstd_z, rstd_v):
    lane = lax.broadcasted_iota(jnp.int32, (ROWS, HEAD), 1)
    packed = rstd_x
    for k, r in enumerate([rstd_z] + list(rstd_v)):
        packed = jnp.where(lane < 16 * (k + 1), packed, r)
    return packed


def _unpack_stats(stats):
    cols = [stats[:, 16 * k:16 * k + 1] for k in range(2 + N_HEAD)]
    return cols[0], cols[1], cols[2:]


def _mixer(proj, halo, row0, wpool_ref, pscale, sgu_g_ref, sgu_b_ref, wsgu_ref, bsgu_ref, saved=None):
    xa = proj[:, 0:512]
    ga = proj[:, 512:1024]
    u = proj[:, 1024:1536]
    v = proj[:, 1536:2048]
    gb = proj[:, 2048:2560]
    ext = jnp.concatenate([halo, xa], axis=0)
    win = _window_sums(ext, toward_later=False)[HALO:]
    cnt = _window_counts(row0)
    pooled = (win / cnt - xa).astype(BF16)
    pw = jnp.concatenate(
        [_dot(pooled[:, g * HEAD:(g + 1) * HEAD], wpool_ref[g].astype(BF16)) for g in range(N_HEAD)], axis=1)
    sig_a = _sigmoid(ga) if saved is None else saved["sig_a"]
    ya = pw * pscale * (ga * sig_a)
    phi_u, pdf_u = _gelu_parts(u)
    phi_v, pdf_v = _gelu_parts(v)
    gu = u * phi_u
    gv = v * phi_v
    sig_b = _sigmoid(gb) if saved is None else saved["sig_b"]
    silu_b = gb * sig_b
    mask = _causal_mask()
    diag = lax.broadcasted_iota(jnp.int32, (HEAD, HEAD), 0) == lax.broadcasted_iota(jnp.int32, (HEAD, HEAD), 1)
    vhat, rstd_v, vln_l, mixed = [], [], [], []
    for h in range(N_HEAD):
        if saved is None:
            vh, rh = _layer_norm(gv[:, h * HEAD:(h + 1) * HEAD])
        else:
            vh, rh = saved["vhat"][h], saved["rstd_v"][h]
        ln = (vh * sgu_g_ref[h:h + 1, :] + sgu_b_ref[h:h + 1, :]).astype(BF16)
        ln_l = _chunks_to_lanes(ln)
        wm = jnp.where(mask, wsgu_ref[h], 0.0).astype(BF16)
        bias = jnp.sum(jnp.where(diag, jnp.broadcast_to(bsgu_ref[h:h + 1, :], (HEAD, HEAD)), 0.0), axis=1, keepdims=True)
        mx = _lanes_to_chunks(_dot(wm, ln_l) + bias)
        vhat.append(vh)
        rstd_v.append(rh)
        vln_l.append(ln_l)
        mixed.append(mx)
    mixed = jnp.concatenate(mixed, axis=1)
    yb = gu * mixed * silu_b
    return dict(xa=xa, ga=ga, u=u, gb=gb, cnt=cnt, pooled=pooled, pw=pw, sig_a=sig_a, ya=ya, phi_u=phi_u, pdf_u=pdf_u,
                phi_v=phi_v, pdf_v=pdf_v, gu=gu, sig_b=sig_b, silu_b=silu_b, vhat=vhat, rstd_v=rstd_v, vln_l=vln_l,
                mixed=mixed, yb=yb, mask=mask)


def _const(shape, *index):
    lead = tuple(index) + (0,) * (len(shape) - len(index))
    return pl.BlockSpec(shape, lambda *_: lead)


def _const_in(shape, *index):
    lead = tuple(index) + (0,) * (len(shape) - len(index))
    return pl.BlockSpec(shape, lambda *_: lead, pipeline_mode=pl.Buffered(1))


def _layer_weight_specs(l):
    return [
        _const_in((None, N_HEAD, HEAD, HEAD), l),
        _const_in((DEPTH, D_POOL)),
        _const_in((None, N_HEAD, HEAD), l),
        _const_in((None, N_HEAD, HEAD), l),
        _const_in((None, N_HEAD, HEAD, HEAD), l),
        _const_in((None, N_HEAD, HEAD), l),
    ]


def _forward_tile(l, i, x_ref, mod_ref, win_ref, wout_ref, small_refs, lng_ref, lnb_ref, carry_ref, saved_refs):
    wpool_ref, pscale_ref, sgu_g_ref, sgu_b_ref, wsgu_ref, bsgu_ref = small_refs
    proj_ref, y_ref, xn_ref, zn_ref, stats_ref, sig_ref, vhat_ref = saved_refs
    x = x_ref[...]
    if l > 0:
        x = x * lng_ref[l - 1:l, :] + lnb_ref[l - 1:l, :]
    shift, scale, gate = mod_ref[0:1, :], mod_ref[1:2, :], mod_ref[2:3, :]
    xn, rstd_x = _layer_norm(x)
    xn_ref[...] = xn.astype(xn_ref.dtype)
    h = xn * (1.0 + scale) + shift
    proj = _dot(h.astype(BF16), win_ref[...])
    proj_ref[...] = proj
    m = _mixer(proj, carry_ref[...], i * ROWS, wpool_ref, pscale_ref[l:l + 1, :], sgu_g_ref, sgu_b_ref, wsgu_ref, bsgu_ref)
    carry_ref[...] = m["xa"][ROWS - HALO:]
    sig_ref[...] = jnp.concatenate([m["sig_a"], m["sig_b"]], axis=1).astype(sig_ref.dtype)
    vhat_ref[...] = jnp.concatenate(m["vhat"], axis=1).astype(vhat_ref.dtype)
    cat = jnp.concatenate([m["ya"], m["yb"]], axis=1).astype(BF16)
    y = _dot(cat, wout_ref[...])
    y_ref[...] = y.astype(y_ref.dtype)
    zn, rstd_z = _layer_norm(DEEPNORM_ALPHA * x + gate * y)
    zn_ref[...] = zn
    stats_ref[...] = _pack_stats(rstd_x, rstd_z, m["rstd_v"])
    return zn


SAVED_COLS = (D_PROJ, D_MODEL, D_MODEL, D_MODEL, HEAD, D_MODEL, D_POOL)
SAVED_TYPES = (F32, BF16, BF16, F32, F32, BF16, BF16)


def _saved_outputs():
    return ([jax.ShapeDtypeStruct((SEQ, cols), t) for cols, t in zip(SAVED_COLS, SAVED_TYPES)],
            [pl.BlockSpec((ROWS, cols), lambda i: (i, 0)) for cols in SAVED_COLS])


def _forward_last(zn_prev, mod, w_in, w_out, small, ln_g, ln_b, target):
    l = DEPTH - 1
    n_saved = len(SAVED_COLS)

    def body(*refs):
        x_ref, mod_ref, win_ref, wout_ref = refs[:4]
        small_refs, lng_ref, lnb_ref, tgt_ref = refs[4:10], refs[10], refs[11], refs[12]
        saved_refs = refs[13:13 + n_saved]
        dout_ref, loss_ref, carry_ref = refs[13 + n_saved:]
        i = pl.program_id(0)

        @pl.when(i == 0)
        def _():
            carry_ref[...] = jnp.zeros_like(carry_ref)
            loss_ref[...] = jnp.zeros_like(loss_ref)

        zn = _forward_tile(l, i, x_ref, mod_ref, win_ref, wout_ref, small_refs, lng_ref, lnb_ref, carry_ref, saved_refs)
        err = zn * lng_ref[l:l + 1, :] + lnb_ref[l:l + 1, :] - tgt_ref[...]
        dout_ref[...] = err * (1.0 / D_MODEL)
        loss_ref[...] += jnp.sum(err * err)

    tile = pl.BlockSpec((ROWS, D_MODEL), lambda i: (i, 0))
    tile3 = pl.BlockSpec((None, ROWS, D_MODEL), lambda i: (0, i, 0))
    in_specs = [tile, _const_in((None, 8, D_MODEL), l), _const_in((D_MODEL, D_PROJ)), _const_in((D_MODEL, D_MODEL))]
    in_specs += _layer_weight_specs(l) + [_const_in((DEPTH, D_MODEL)), _const_in((DEPTH, D_MODEL)), tile3]
    out_shape, out_specs = _saved_outputs()
    out_shape += [jax.ShapeDtypeStruct((SEQ, D_MODEL), F32), jax.ShapeDtypeStruct((8, HEAD), F32)]
    out_specs += [tile, _const((8, HEAD))]
    return pl.pallas_call(
        body, name="fwd_last", grid=(N_TILE,), in_specs=in_specs, out_specs=out_specs, out_shape=out_shape,
        scratch_shapes=[pltpu.VMEM((HALO, D_POOL), F32)],
        compiler_params=pltpu.CompilerParams(dimension_semantics=("arbitrary",), vmem_limit_bytes=VMEM_LIMIT),
    )(zn_prev, mod, w_in, w_out, *small, ln_g, ln_b, target)


def _backward_layer(l, dout, saved, mod, w_in, w_out, small, ln_g, sq=None, shared=None):
    has_loss = sq is not None

    def body(*refs):
        (dout_ref, proj_ref, y_ref, xn_ref, zn_ref, stats_ref, sig_ref, vhat_ref, halo_ref, mod_ref, win_ref, wout_ref,
         wpool_ref, pscale_ref, sgu_g_ref, sgu_b_ref, wsgu_ref, bsgu_ref, lng_ref) = refs[:19]
        n_in = 20 if has_loss else 19 + 6
        dx_ref, h_ref, cat_ref, dy_ref, dproj_ref, pack_ref, dmod_ref, carry_ref = refs[n_in:n_in + 8]
        i = pl.program_id(0)
        tile = N_TILE - 1 - i

        @pl.when(i == 0)
        def _():
            carry_ref[...] = jnp.zeros_like(carry_ref)
            pack_ref[...] = jnp.zeros_like(pack_ref)
            dmod_ref[...] = jnp.zeros_like(dmod_ref)
            if has_loss:
                pack_ref[PK_LOSS:PK_LOSS + 8, :] = refs[19][...]

        xn = xn_ref[...].astype(F32)
        zn = zn_ref[...]
        y = y_ref[...].astype(F32)
        dout = dout_ref[...]
        rstd_x, rstd_z, rstd_v = _unpack_stats(stats_ref[...])
        kept = dict(sig_a=sig_ref[:, :D_POOL].astype(F32), sig_b=sig_ref[:, D_POOL:].astype(F32), rstd_v=rstd_v,
                    vhat=[vhat_ref[:, hd * HEAD:(hd + 1) * HEAD].astype(F32) for hd in range(N_HEAD)])
        pscale = pscale_ref[l:l + 1, :]
        shift, scale, gate = mod_ref[0:1, :], mod_ref[1:2, :], mod_ref[2:3, :]
        h = xn * (1.0 + scale) + shift
        h_ref[...] = h.astype(BF16)
        g_ln_g = _sum_rows(dout * zn)
        g_ln_b = _sum_rows(dout)
        dz = _layer_norm_bwd(dout * lng_ref[l:l + 1, :], zn, rstd_z)
        d_gate = _sum_rows(dz * y)
        dy = (gate * dz).astype(BF16)
        dy_ref[...] = dy

        halo = jnp.where(tile > 0, halo_ref[...], 0.0)
        m = _mixer(proj_ref[...], halo, tile * ROWS, wpool_ref, pscale, sgu_g_ref, sgu_b_ref, wsgu_ref, bsgu_ref,
                   saved=kept)
        cat_ref[...] = jnp.concatenate([m["ya"], m["yb"]], axis=1).astype(BF16)
        dcat = _dot(dy, wout_ref[...], NT)
        dya = dcat[:, :D_POOL]
        dyb = dcat[:, D_POOL:]

        ga, sig_a = m["ga"], m["sig_a"]
        dp = dya * (ga * sig_a)
        d_ga = dya * (m["pw"] * pscale) * (sig_a * (1.0 + ga * (1.0 - sig_a)))
        g_pscale = _sum_rows(dp * m["pw"])
        dpw = (dp * pscale).astype(BF16)
        dpooled = []
        for g in range(N_HEAD):
            cols = slice(g * HEAD, (g + 1) * HEAD)
            pack_ref[PK_W_POOL + g * HEAD:PK_W_POOL + (g + 1) * HEAD, :] += _dot(m["pooled"][:, cols], dpw[:, cols], TN)
            dpooled.append(_dot(dpw[:, cols], wpool_ref[g].astype(BF16), NT))
        dpooled = jnp.concatenate(dpooled, axis=1)
        q = dpooled / m["cnt"]
        ext = jnp.concatenate([q, carry_ref[...]], axis=0)
        d_xa = _window_sums(ext, toward_later=True)[:ROWS] - dpooled
        carry_ref[...] = q[:HALO]

        gu, mixed, silu_b, gb, sig_b = m["gu"], m["mixed"], m["silu_b"], m["gb"], m["sig_b"]
        d_mixed = dyb * gu * silu_b
        d_gu = dyb * mixed * silu_b
        d_gb = dyb * gu * mixed * (sig_b * (1.0 + gb * (1.0 - sig_b)))
        d_u = d_gu * (m["phi_u"] + m["u"] * m["pdf_u"])
        ones = jnp.ones((8, HEAD), F32)
        d_v = []
        for hd in range(N_HEAD):
            cols = slice(hd * HEAD, (hd + 1) * HEAD)
            dm = d_mixed[:, cols]
            dm_l = _chunks_to_lanes(dm.astype(BF16))
            g_w = _dot(dm_l, m["vln_l"][hd], NT)
            pack_ref[PK_W_SGU + hd * HEAD:PK_W_SGU + (hd + 1) * HEAD, :] += jnp.where(m["mask"], g_w, 0.0)
            dm_sum = dm[0:HEAD]
            for n in range(1, ROWS // HEAD):
                dm_sum = dm_sum + dm[n * HEAD:(n + 1) * HEAD]
            pack_ref[PK_B_SGU + hd:PK_B_SGU + hd + 1, :] += _dot_exact(ones, dm_sum, NT)[0:1]
            wm = jnp.where(m["mask"], wsgu_ref[hd], 0.0).astype(BF16)
            d_vln = _lanes_to_chunks(_dot(wm, dm_l, TN))
            vhat = m["vhat"][hd]
            pack_ref[PK_SGU_LN_G + hd:PK_SGU_LN_G + hd + 1, :] += _sum_rows(d_vln * vhat)
            pack_ref[PK_SGU_LN_B + hd:PK_SGU_LN_B + hd + 1, :] += _sum_rows(d_vln)
            d_v.append(_layer_norm_bwd(d_vln * sgu_g_ref[hd:hd + 1, :], vhat, m["rstd_v"][hd]))
        v = proj_ref[:, 1536:2048]
        d_v = jnp.concatenate(d_v, axis=1) * (m["phi_v"] + v * m["pdf_v"])

        dproj = jnp.concatenate([d_xa, d_ga, d_u, d_v, d_gb], axis=1).astype(BF16)
        dproj_ref[...] = dproj
        dh = _dot(dproj, win_ref[...], NT)
        d_scale = _sum_rows(dh * xn)
        d_shift = _sum_rows(dh)
        dx_ref[...] = DEEPNORM_ALPHA * dz + _layer_norm_bwd(dh * (1.0 + scale), xn, rstd_x)

        dmod_ref[0:1, :] += d_shift
        dmod_ref[1:2, :] += d_scale
        dmod_ref[2:3, :] += d_gate
        for g in range(N_HEAD):
            pack_ref[PK_POOL_SCALE + g:PK_POOL_SCALE + g + 1, :] += g_pscale[:, g * HEAD:(g + 1) * HEAD]
        for k in range(D_MODEL // HEAD):
            pack_ref[PK_LN_G + k:PK_LN_G + k + 1, :] += g_ln_g[:, k * HEAD:(k + 1) * HEAD]
            pack_ref[PK_LN_B + k:PK_LN_B + k + 1, :] += g_ln_b[:, k * HEAD:(k + 1) * HEAD]

    def rev(i):
        return (N_TILE - 1 - i, 0)

    tile = pl.BlockSpec((ROWS, D_MODEL), rev)
    halo = pl.BlockSpec((HALO, D_POOL), lambda i: (jnp.maximum((N_TILE - 1 - i) * (ROWS // HALO) - 1, 0), 0))
    in_specs = [tile] + [pl.BlockSpec((ROWS, a.shape[1]), rev) for a in saved] + [halo]
    in_specs += [_const_in((None, 8, D_MODEL), l), _const_in((D_MODEL, D_PROJ)), _const_in((D_MODEL, D_MODEL))]
    in_specs += _layer_weight_specs(l) + [_const_in((DEPTH, D_MODEL))]
    args = [dout, *saved, saved[0], mod, w_in, w_out, *small, ln_g]
    stacked = lambda cols: pl.BlockSpec((None, ROWS, cols), lambda i: (l, N_TILE - 1 - i, 0))
    out_shape = [jax.ShapeDtypeStruct((SEQ, D_MODEL), F32), jax.ShapeDtypeStruct((DEPTH, SEQ, D_MODEL), BF16),
                 jax.ShapeDtypeStruct((DEPTH, SEQ, D_MODEL), BF16), jax.ShapeDtypeStruct((DEPTH, SEQ, D_MODEL), BF16),
                 jax.ShapeDtypeStruct((DEPTH, SEQ, D_PROJ), BF16), jax.ShapeDtypeStruct((DEPTH, PK_ROWS, HEAD), F32),
                 jax.ShapeDtypeStruct((DEPTH, 8, D_MODEL), F32)]
    out_specs = [tile, stacked(D_MODEL), stacked(D_MODEL), stacked(D_MODEL), stacked(D_PROJ),
                 _const((None, PK_ROWS, HEAD), l), _const((None, 8, D_MODEL), l)]
    aliases = {}
    if has_loss:
        in_specs.append(_const_in((8, HEAD)))
        args.append(sq)
    else:
        aliases = {len(args) + k: 1 + k for k in range(len(shared))}
        in_specs += [pl.BlockSpec(memory_space=pl.ANY)] * len(shared)
        args += list(shared)
    return pl.pallas_call(
        body, name="bwd_last" if has_loss else "bwd_first", grid=(N_TILE,), in_specs=in_specs, out_specs=out_specs,
        out_shape=out_shape, scratch_shapes=[pltpu.VMEM((HALO, D_POOL), F32)], input_output_aliases=aliases,
        compiler_params=pltpu.CompilerParams(dimension_semantics=("arbitrary",), vmem_limit_bytes=VMEM_LIMIT),
    )(*args)


def _flip(v, f):
    return v + f - 2 * v * f


class _Place:
    def __init__(self):
        x, y, c = lax.axis_index("x"), lax.axis_index("y"), lax.axis_index("c")
        self.x, self.y, self.c = x, y, c
        self.chip = 2 * x + y
        self.dev = 4 * x + 2 * y + c
        self.sibling = (x, y, 1 - c)
        x1, y1 = _flip(x, 1 - c), _flip(y, c)
        x2, y2 = _flip(x, c), _flip(y, 1 - c)
        self.first = (x1, y1, c)
        self.second = (x2, y2, c)
        self.chip_first = 2 * x1 + y1
        self.chip_second = 2 * x2 + y2
        self.chip_far = 2 * (1 - x) + (1 - y)
        self.my_first_coord = jnp.where(c == 0, x, y)

    def first_coord(self, ch):
        return jnp.where(self.c == 0, ch // 2, ch % 2)

    def others(self):
        return [(_flip(self.x, (r >> 2) & 1), _flip(self.y, (r >> 1) & 1), _flip(self.c, r & 1)) for r in range(1, N_DEV)]

    def other_chips(self):
        return [(1 - self.x, self.y), (self.x, 1 - self.y), (1 - self.x, 1 - self.y)]


class _WeightGather:
    CHUNKS = 4
    N_SEMS = 12 * CHUNKS

    def __init__(self, place, win, wout, send, recv):
        self.p, self.win, self.wout, self.send, self.recv = place, win, wout, send, recv
        p = place
        self.plan = [(p.chip, p.first), (p.chip, p.second), (p.chip_first, p.second),
                     (p.chip_first, p.sibling), (p.chip_second, p.sibling), (p.chip_far, p.sibling)]

    def _copies(self, k, q):
        ch, target = self.plan[k]
        n_in, n_out = HALF_IN // self.CHUNKS, HALF_OUT // self.CHUNKS
        rows_in = pl.ds(pl.multiple_of(self.p.c * HALF_IN + q * n_in, n_in), n_in)
        cols_in = pl.ds(pl.multiple_of(ch * W_IN_COLS, 128), W_IN_COLS)
        rows_out = pl.ds(pl.multiple_of(ch * W_OUT_ROWS + self.p.c * HALF_OUT + q * n_out, n_out), n_out)
        r_in = self.win.at[rows_in, cols_in]
        r_out = self.wout.at[rows_out, :]
        s = 2 * (6 * q + k)
        return [pltpu.make_async_remote_copy(r_in, r_in, self.send.at[s], self.recv.at[s],
                                             device_id=target, device_id_type=MESH),
                pltpu.make_async_remote_copy(r_out, r_out, self.send.at[s + 1], self.recv.at[s + 1],
                                             device_id=target, device_id_type=MESH)]

    def _start(self, k, q):
        for cp in self._copies(k, q):
            cp.start()

    def _landed(self, k, q):
        for cp in self._copies(k, q):
            cp.wait_recv()

    def start_first_round(self):
        for q in range(self.CHUNKS):
            self._start(0, q)

    def start_second_round(self, q):
        self._landed(0, q)
        self._start(1, q)
        self._start(2, q)
        self._start(3, q)

    def pass_second_round(self, q):
        self._landed(1, q)
        self._start(4, q)
        self._landed(2, q)
        self._start(5, q)

    def finish(self):
        for q in range(self.CHUNKS):
            for k in (3, 4, 5):
                self._landed(k, q)
        for q in range(self.CHUNKS):
            for k in range(len(self.plan)):
                for cp in self._copies(k, q):
                    cp.wait_send()


def _forward_first(x, c_vec, w_ada, b_ada, w_in, w_out, small, ln_g, ln_b):
    n_saved = len(SAVED_COLS)

    def body(*refs):
        x_ref, c_ref, wada_hbm, bada_ref, win_hbm, wout_hbm = refs[:6]
        small_refs, lng_ref, lnb_ref = refs[6:12], refs[12], refs[13]
        saved_refs = refs[14:14 + n_saved]
        win0, wout0, win1, wout1, mod_out, c_out = refs[14 + n_saved:20 + n_saved]
        (carry_ref, wada_ref, win_ref, wout_ref, win_bf, wout_bf, mod_mine, mod_all, c_all, mod_ref, win_v, wout_v,
         g0_send, g0_recv, g1_send, g1_recv, c_send, c_recv, mod_send, mod_recv, local_sem) = refs[20 + n_saved:]
        i = pl.program_id(0)
        p = _Place()
        gather0 = _WeightGather(p, win0, wout0, g0_send, g0_recv)
        gather1 = _WeightGather(p, win1, wout1, g1_send, g1_recv)

        @pl.when(i == 0)
        def _():
            carry_ref[...] = jnp.zeros_like(carry_ref)
            loads = [pltpu.make_async_copy(win_hbm.at[0], win_ref.at[0], local_sem.at[4]),
                     pltpu.make_async_copy(wout_hbm.at[0], wout_ref.at[0], local_sem.at[5]),
                     pltpu.make_async_copy(win_hbm.at[1], win_ref.at[1], local_sem.at[6]),
                     pltpu.make_async_copy(wout_hbm.at[1], wout_ref.at[1], local_sem.at[7]),
                     pltpu.make_async_copy(wada_hbm, wada_ref, local_sem.at[8])]
            for cp in loads:
                cp.start()

            c_all[pl.ds(p.dev, 1), :] = c_ref[...]
            mine = c_all.at[pl.ds(p.dev, 1), :]
            c_copies = [pltpu.make_async_remote_copy(mine, mine, c_send.at[r], c_recv.at[r], device_id=d, device_id_type=MESH)
                        for r, d in enumerate(p.others())]
            for cp in c_copies:
                cp.start()

            cols = pl.ds(pl.multiple_of(p.chip * W_IN_COLS, 128), W_IN_COLS)
            rows = pl.ds(pl.multiple_of(p.chip * W_OUT_ROWS, W_OUT_ROWS), W_OUT_ROWS)
            own = [pltpu.make_async_copy(win_bf.at[0], win0.at[:, cols], local_sem.at[0]),
                   pltpu.make_async_copy(wout_bf.at[0], wout0.at[rows, :], local_sem.at[1]),
                   pltpu.make_async_copy(win_bf.at[1], win1.at[:, cols], local_sem.at[2]),
                   pltpu.make_async_copy(wout_bf.at[1], wout1.at[rows, :], local_sem.at[3])]
            for l in range(DEPTH):
                loads[2 * l].wait()
                win_bf[l] = win_ref[l].astype(BF16)
                own[2 * l].start()
                loads[2 * l + 1].wait()
                wout_bf[l] = wout_ref[l].astype(BF16)
                own[2 * l + 1].start()
                if l == 0:
                    own[0].wait()
                    own[1].wait()
                    gather0.start_first_round()
            for cp in c_copies:
                cp.wait()
            loads[4].wait()

            cv = c_all[...]
            c_out[...] = cv
            silu_c = (cv * _sigmoid(cv)).astype(BF16)
            for l in range(DEPTH):
                mod_mine[l] = _dot(silu_c, wada_ref[l].astype(BF16))
            mod_all[p.chip] = mod_mine[...]
            m_copies = [pltpu.make_async_remote_copy(mod_mine, mod_all.at[p.chip], mod_send.at[k], mod_recv.at[k],
                                                     device_id=(px, py, p.c), device_id_type=MESH)
                        for k, (px, py) in enumerate(p.other_chips())]
            for cp in m_copies:
                cp.start()
            for q in range(gather0.CHUNKS):
                gather0.start_second_round(q)
            own[2].wait()
            own[3].wait()
            gather1.start_first_round()
            for cp in m_copies:
                cp.wait()
            mod_ref[...] = jnp.zeros_like(mod_ref)
            for l in range(DEPTH):
                full = jnp.concatenate([mod_all[ch, l, pl.ds(p.dev, 1), :] for ch in range(N_CHIP)], axis=1) + bada_ref[l:l + 1, :]
                for k in range(3):
                    mod_ref[l, k:k + 1, :] = full[:, k * D_MODEL:(k + 1) * D_MODEL]
            mod_out[...] = mod_ref[...]
            for q in range(gather0.CHUNKS):
                gather0.pass_second_round(q)
            gather0.finish()
            fetch = [pltpu.make_async_copy(win0, win_v, local_sem.at[9]), pltpu.make_async_copy(wout0, wout_v, local_sem.at[10])]
            for cp in fetch:
                cp.start()
            for cp in fetch:
                cp.wait()

        for q in range(_WeightGather.CHUNKS):
            @pl.when(i == GATHER_SECOND_ROUND_STEP + q)
            def _(q=q):
                gather1.start_second_round(q)

            @pl.when(i == GATHER_PASS_STEP + q)
            def _(q=q):
                gather1.pass_second_round(q)

        _forward_tile(0, i, x_ref, mod_ref.at[0], win_v, wout_v, small_refs, lng_ref, lnb_ref, carry_ref, saved_refs)

        @pl.when(i == N_TILE - 1)
        def _():
            gather1.finish()

    hbm = pl.BlockSpec(memory_space=pl.ANY)
    tile3 = pl.BlockSpec((None, ROWS, D_MODEL), lambda i: (0, i, 0))
    in_specs = [tile3, _const_in((1, D_MODEL)), hbm, _const_in((DEPTH, 3 * D_MODEL)), hbm, hbm]
    in_specs += _layer_weight_specs(0) + [_const_in((DEPTH, D_MODEL)), _const_in((DEPTH, D_MODEL))]
    out_shape, out_specs = _saved_outputs()
    w_in_shape = jax.ShapeDtypeStruct((D_MODEL, D_PROJ), BF16)
    w_out_shape = jax.ShapeDtypeStruct((D_MODEL, D_MODEL), BF16)
    out_shape += [w_in_shape, w_out_shape, w_in_shape, w_out_shape,
                  jax.ShapeDtypeStruct((DEPTH, 8, D_MODEL), F32), jax.ShapeDtypeStruct((N_DEV, D_MODEL), F32)]
    out_specs += [hbm, hbm, hbm, hbm, _const((DEPTH, 8, D_MODEL)), _const((N_DEV, D_MODEL))]
    gather_sems = [pltpu.SemaphoreType.DMA((_WeightGather.N_SEMS,))] * 4
    scratch = [
        pltpu.VMEM((HALO, D_POOL), F32),
        pltpu.VMEM(w_ada.shape, F32), pltpu.VMEM(w_in.shape, F32), pltpu.VMEM(w_out.shape, F32),
        pltpu.VMEM((DEPTH, D_MODEL, W_IN_COLS), BF16), pltpu.VMEM((DEPTH, W_OUT_ROWS, D_MODEL), BF16),
        pltpu.VMEM((DEPTH, N_DEV, W_ADA_COLS), F32), pltpu.VMEM((N_CHIP, DEPTH, N_DEV, W_ADA_COLS), F32),
        pltpu.VMEM((N_DEV, D_MODEL), F32), pltpu.VMEM((DEPTH, 8, D_MODEL), F32),
        pltpu.VMEM((D_MODEL, D_PROJ), BF16), pltpu.VMEM((D_MODEL, D_MODEL), BF16),
    ] + gather_sems + [
        pltpu.SemaphoreType.DMA((7,)), pltpu.SemaphoreType.DMA((7,)),
        pltpu.SemaphoreType.DMA((3,)), pltpu.SemaphoreType.DMA((3,)),
        pltpu.SemaphoreType.DMA((11,)),
    ]
    return pl.pallas_call(
        body, name="fwd_first", grid=(N_TILE,), in_specs=in_specs, out_specs=out_specs, out_shape=out_shape,
        scratch_shapes=scratch,
        compiler_params=pltpu.CompilerParams(dimension_semantics=("arbitrary",), vmem_limit_bytes=VMEM_LIMIT),
    )(x, c_vec, w_ada, b_ada, w_in, w_out, *small, ln_g, ln_b)


IN_STEPS = W_IN_COLS // HEAD
OUT_STEPS = 4
OUT_COLS = D_MODEL // OUT_STEPS
OUT_FIRST = 2
ITEMS = ([("out", k) for k in range(OUT_FIRST)] + [("in", k) for k in range(IN_STEPS)]
         + [("out", k) for k in range(OUT_FIRST, OUT_STEPS)])
N_ITEMS = len(ITEMS)
N_STEPS = DEPTH * N_ITEMS
DELAY_SUM, DELAY_SECOND, DELAY_FINAL = 1, 3, 5
SMALL_SCATTER_STEP, SMALL_GATHER_STEP, SMALL_PASS_STEP, SMALL_FINISH_STEP = 1, 3, 5, 7


def _wgrad_reduce(h, dproj, cat, dy, pack, dmod):
    def body(*refs):
        h_ref, dp_refs, cat_ref, dy_ref, pack_ref, dmod_ref = refs[0], refs[1:5], refs[5], refs[6], refs[7], refs[8]
        fin_in, fin_out, pack_out, dmod_out = refs[9:13]
        scratch = refs[13:]
        (mine_in, send_in, sib_in, st_in, r1_in, r2_in, f_in,
         mine_out, send_out, sib_out, st_out, r1_out, r2_out, f_out,
         d2d_s, d2d_r, r1_s, r1_r, r2_s, r2_r, fin_l, fin_s, fin_r) = scratch[:23]
        p = _Place()
        c = p.c
        i = pl.program_id(0)
        my_rows = pl.ds(pl.multiple_of(c * HALF_IN, HALF_IN), HALF_IN)

        def layer_of(j):
            return DEPTH - 1 - j // N_ITEMS

        def bufs(j):
            kind, k = ITEMS[j % N_ITEMS]
            if kind == "in":
                return [r.at[k] for r in (mine_in, send_in, sib_in, st_in, r1_in, r2_in, f_in)]
            return [r.at[k] for r in (mine_out, send_out, sib_out, st_out, r1_out, r2_out, f_out)]

        def piece(j, ref, ch):
            if ITEMS[j % N_ITEMS][0] == "in":
                return ref.at[:, ch * HEAD:(ch + 1) * HEAD]
            return ref.at[ch]

        def slot(ch):
            return jnp.where(c == 0, ch % 2, ch // 2)

        def to_sibling(j):
            _, send, sib, _, _, _, _ = bufs(j)
            return pltpu.make_async_remote_copy(send, sib, d2d_s.at[j], d2d_r.at[j], device_id=p.sibling, device_id_type=MESH)

        def first_round(j, ch):
            _, _, _, st, r1, _, _ = bufs(j)
            k = slot(ch)
            return pltpu.make_async_remote_copy(st.at[k], r1.at[k], r1_s.at[2 * j + k], r1_r.at[2 * j + k],
                                                device_id=p.first, device_id_type=MESH)

        def second_round(j):
            _, _, _, st, _, r2, _ = bufs(j)
            return pltpu.make_async_remote_copy(st.at[2], r2, r2_s.at[j], r2_r.at[j], device_id=p.second, device_id_type=MESH)

        def finals(j):
            f = bufs(j)[6]
            kind, k = ITEMS[j % N_ITEMS]
            if kind == "in":
                dst = fin_in.at[layer_of(j), my_rows, k * HEAD:(k + 1) * HEAD]
            else:
                dst = fin_out.at[layer_of(j), c, :, k * OUT_COLS:(k + 1) * OUT_COLS]
            return [pltpu.make_async_copy(f, dst, fin_l.at[j]),
                    pltpu.make_async_remote_copy(f, dst, fin_s.at[j], fin_r.at[j], device_id=p.sibling, device_id_type=MESH)]

        def stage_sum(j):
            mine, _, sib, st, _, _, _ = bufs(j)
            to_sibling(j).wait_recv()
            mine[...] = mine[...] + sib[...]
            for ch in range(N_CHIP):
                @pl.when(p.first_coord(ch) != p.my_first_coord)
                def _(ch=ch):
                    st[slot(ch)] = piece(j, mine, ch)[...].astype(BF16)
                    first_round(j, ch).start()

        def stage_second(j):
            mine, _, _, st, r1, _, _ = bufs(j)
            for ch in range(N_CHIP):
                @pl.when(p.first_coord(ch) == p.my_first_coord)
                def _(ch=ch):
                    first_round(j, ch).wait_recv()
                    part = piece(j, mine, ch)
                    total = part[...] + r1[slot(ch)].astype(F32)
                    part[...] = total

                    @pl.when(ch != p.chip)
                    def _():
                        st[2] = total.astype(BF16)
                        second_round(j).start()

        def stage_final(j):
            mine, _, _, _, _, r2, f = bufs(j)
            second_round(j).wait_recv()
            for ch in range(N_CHIP):
                @pl.when(ch == p.chip)
                def _(ch=ch):
                    f[...] = piece(j, mine, ch)[...] + r2[...].astype(F32)
            for cp in finals(j):
                cp.start()

        def drain(j):
            to_sibling(j).wait_send()
            for ch in range(N_CHIP):
                @pl.when(p.first_coord(ch) != p.my_first_coord)
                def _(ch=ch):
                    first_round(j, ch).wait_send()

                @pl.when(jnp.logical_and(p.first_coord(ch) == p.my_first_coord, ch != p.chip))
                def _():
                    second_round(j).wait_send()
            for cp in finals(j):
                cp.wait()

        dev = p.dev
        devices = p.others()

        def half(core):
            return pl.ds(pl.multiple_of(core * PK_HALF, 8), PK_HALF)

        def finished(core, ch):
            return pl.ds(pl.multiple_of(core * PK_HALF + ch * PK_PIECE, 8), PK_PIECE)

        def small_exchange(l, first_step, bufs_l):
            (pk_mine, pk_sib, pk_rs, pk_fin, pk_all, dm_st, dm_all, pk_sem, rs_s, rs_r, ag_s, ag_r, dm_s, dm_r) = bufs_l

            def pk_load():
                return pltpu.make_async_copy(pack_ref.at[l, half(c)], pk_mine, pk_sem.at[0])

            def pk_give():
                return pltpu.make_async_remote_copy(pack_ref.at[l, half(1 - c)], pk_sib, pk_sem.at[1], pk_sem.at[2],
                                                    device_id=p.sibling, device_id_type=MESH)

            def pk_scatter(ch):
                return pltpu.make_async_remote_copy(pk_mine.at[ch * PK_PIECE:(ch + 1) * PK_PIECE], pk_rs.at[p.chip],
                                                    rs_s.at[ch], rs_r.at[p.chip], device_id=(ch // 2, ch % 2, c),
                                                    device_id_type=MESH)

            def pk_spread(ch):
                return pltpu.make_async_remote_copy(pk_fin, pk_all.at[finished(c, p.chip)], ag_s.at[ch], ag_r.at[p.chip],
                                                    device_id=(ch // 2, ch % 2, c), device_id_type=MESH)

            def pk_pass():
                return pltpu.make_async_remote_copy(pk_all.at[half(c)], pk_all.at[half(c)], pk_sem.at[3], pk_sem.at[4],
                                                    device_id=p.sibling, device_id_type=MESH)

            def dm_copy(r):
                return pltpu.make_async_remote_copy(dm_st, dm_all.at[:, pl.ds(dev, 1), :], dm_s.at[r], dm_r.at[r],
                                                    device_id=devices[r], device_id_type=MESH)

            def results():
                return [pltpu.make_async_copy(pk_all, pack_out.at[l], pk_sem.at[0]),
                        pltpu.make_async_copy(dm_all, dmod_out.at[l], pk_sem.at[5])]

            @pl.when(i == first_step)
            def _():
                pk_load().start()
                pk_give().start()
                for k in range(3):
                    for r in range(D_MODEL // HEAD):
                        dm_st[8 * k + r] = dmod_ref[l, k:k + 1, r * HEAD:(r + 1) * HEAD]
                dm_all[:, pl.ds(dev, 1), :] = dm_st[...]
                for r in range(N_DEV - 1):
                    dm_copy(r).start()

            @pl.when(i == first_step + SMALL_SCATTER_STEP)
            def _():
                pk_load().wait()
                pk_give().wait()
                pk_mine[...] = pk_mine[...] + pk_sib[...]
                for ch in range(N_CHIP):
                    @pl.when(ch != p.chip)
                    def _(ch=ch):
                        pk_scatter(ch).start()

            @pl.when(i == first_step + SMALL_GATHER_STEP)
            def _():
                for ch in range(N_CHIP):
                    @pl.when(ch != p.chip)
                    def _(ch=ch):
                        pltpu.make_async_remote_copy(pk_fin, pk_rs.at[ch], rs_s.at[ch], rs_r.at[ch],
                                                     device_id=p.sibling, device_id_type=MESH).wait_recv()
                for me in range(N_CHIP):
                    @pl.when(me == p.chip)
                    def _(me=me):
                        total = None
                        for ch in range(N_CHIP):
                            part = pk_mine[me * PK_PIECE:(me + 1) * PK_PIECE] if ch == me else pk_rs[ch]
                            total = part if total is None else total + part
                        pk_fin[...] = total
                        pk_all[finished(c, me)] = total
                for ch in range(N_CHIP):
                    @pl.when(ch != p.chip)
                    def _(ch=ch):
                        pk_spread(ch).start()

            @pl.when(i == first_step + SMALL_PASS_STEP)
            def _():
                for ch in range(N_CHIP):
                    @pl.when(ch != p.chip)
                    def _(ch=ch):
                        pltpu.make_async_remote_copy(pk_fin, pk_all.at[finished(c, ch)], ag_s.at[ch], ag_r.at[ch],
                                                     device_id=p.sibling, device_id_type=MESH).wait_recv()
                pk_pass().start()

            @pl.when(i == first_step + SMALL_FINISH_STEP)
            def _():
                pk_pass().wait()
                for ch in range(N_CHIP):
                    @pl.when(ch != p.chip)
                    def _(ch=ch):
                        pk_scatter(ch).wait_send()
                        pk_spread(ch).wait_send()
                for r in range(N_DEV - 1):
                    dm_copy(r).wait()
                for cp in results():
                    cp.start()
                for cp in results():
                    cp.wait()

        n_small = 14
        for l in range(DEPTH):
            small_exchange(l, (DEPTH - 1 - l) * N_ITEMS, scratch[23 + n_small * l:23 + n_small * (l + 1)])

        for step in range(N_ITEMS, N_STEPS):
            @pl.when(i == step)
            def _(step=step):
                drain(step - N_ITEMS)

        ii = jnp.where(i < N_ITEMS, i, i - N_ITEMS)
        in_step = jnp.logical_and(ii >= OUT_FIRST, ii < OUT_FIRST + IN_STEPS)

        @pl.when(in_step)
        def _():
            k = ii - OUT_FIRST
            rhs = jnp.concatenate([r[...] for r in dp_refs], axis=1)
            res = _dot(h_ref[...], rhs, TN)

            @pl.when(c == 0)
            def _():
                mine_in[k] = res[:HALF_IN]
                send_in[k] = res[HALF_IN:]

            @pl.when(c == 1)
            def _():
                mine_in[k] = res[HALF_IN:]
                send_in[k] = res[:HALF_IN]

        @pl.when(jnp.logical_not(in_step))
        def _():
            k = jnp.where(ii < OUT_FIRST, ii, ii - IN_STEPS)
            res = _dot(cat_ref[...], dy_ref[...], TN)

            @pl.when(c == 0)
            def _():
                for ch in range(N_CHIP):
                    mine_out[k, ch] = res[ch * W_OUT_ROWS:ch * W_OUT_ROWS + HALF_OUT]
                    send_out[k, ch] = res[ch * W_OUT_ROWS + HALF_OUT:(ch + 1) * W_OUT_ROWS]

            @pl.when(c == 1)
            def _():
                for ch in range(N_CHIP):
                    mine_out[k, ch] = res[ch * W_OUT_ROWS + HALF_OUT:(ch + 1) * W_OUT_ROWS]
                    send_out[k, ch] = res[ch * W_OUT_ROWS:ch * W_OUT_ROWS + HALF_OUT]

        stages = ((0, lambda j: to_sibling(j).start()), (DELAY_SUM, stage_sum), (DELAY_SECOND, stage_second),
                  (DELAY_FINAL, stage_final))
        for step in range(N_STEPS):
            @pl.when(i == step)
            def _(step=step):
                for delay, stage in stages:
                    if step - delay >= 0:
                        stage(step - delay)

        @pl.when(i == N_STEPS - 1)
        def _():
            for step in range(N_STEPS, N_STEPS + DELAY_FINAL):
                for delay, stage in stages:
                    if 0 <= step - delay < N_STEPS:
                        stage(step - delay)
            for j in range(N_STEPS - N_ITEMS, N_STEPS):
                drain(j)

    hbm = pl.BlockSpec(memory_space=pl.ANY)

    def layer(i):
        return jnp.where(i < N_ITEMS, DEPTH - 1, 0)

    def item(i):
        return jnp.where(i < N_ITEMS, i, i - N_ITEMS)

    def whole(i):
        return (layer(i), 0, 0)

    def dproj_piece(ch):
        return pl.BlockSpec((None, SEQ, HEAD),
                            lambda i: (layer(i), 0, ch * IN_STEPS + jnp.clip(item(i) - OUT_FIRST, 0, IN_STEPS - 1)))

    def dy_quarter(i):
        return (layer(i), 0, jnp.where(item(i) < OUT_FIRST, item(i), jnp.maximum(item(i) - IN_STEPS, OUT_FIRST)))

    operand = pl.BlockSpec((None, SEQ, D_MODEL), whole, pipeline_mode=pl.Buffered(1))
    in_specs = [operand] + [dproj_piece(ch) for ch in range(N_CHIP)]
    in_specs += [operand, pl.BlockSpec((None, SEQ, OUT_COLS), dy_quarter), hbm, _const_in((DEPTH, 8, D_MODEL))]
    args = [h, dproj, dproj, dproj, dproj, cat, dy, pack, dmod]
    out_shape = [jax.ShapeDtypeStruct((DEPTH, D_MODEL, W_IN_COLS), F32), jax.ShapeDtypeStruct((DEPTH, 2, HALF_OUT, D_MODEL), F32),
                 jax.ShapeDtypeStruct((DEPTH, PK_ROWS, HEAD), F32), jax.ShapeDtypeStruct((DEPTH, 24, N_DEV, HEAD), F32)]
    out_specs = [hbm, hbm, hbm, hbm]
    in_item = lambda *lead: pltpu.VMEM(lead + (HALF_IN, HEAD), BF16)
    out_item = lambda *lead: pltpu.VMEM(lead + (HALF_OUT, OUT_COLS), BF16)
    scratch = [
        pltpu.VMEM((IN_STEPS, HALF_IN, N_CHIP * HEAD), F32), pltpu.VMEM((IN_STEPS, HALF_IN, N_CHIP * HEAD), F32),
        pltpu.VMEM((IN_STEPS, HALF_IN, N_CHIP * HEAD), F32), in_item(IN_STEPS, 3), in_item(IN_STEPS, 2), in_item(IN_STEPS),
        pltpu.VMEM((IN_STEPS, HALF_IN, HEAD), F32),
        pltpu.VMEM((OUT_STEPS, N_CHIP, HALF_OUT, OUT_COLS), F32), pltpu.VMEM((OUT_STEPS, N_CHIP, HALF_OUT, OUT_COLS), F32),
        pltpu.VMEM((OUT_STEPS, N_CHIP, HALF_OUT, OUT_COLS), F32), out_item(OUT_STEPS, 3), out_item(OUT_STEPS, 2),
        out_item(OUT_STEPS), pltpu.VMEM((OUT_STEPS, HALF_OUT, OUT_COLS), F32),
        pltpu.SemaphoreType.DMA((N_STEPS,)), pltpu.SemaphoreType.DMA((N_STEPS,)),
        pltpu.SemaphoreType.DMA((2 * N_STEPS,)), pltpu.SemaphoreType.DMA((2 * N_STEPS,)),
        pltpu.SemaphoreType.DMA((N_STEPS,)), pltpu.SemaphoreType.DMA((N_STEPS,)),
        pltpu.SemaphoreType.DMA((N_STEPS,)), pltpu.SemaphoreType.DMA((N_STEPS,)), pltpu.SemaphoreType.DMA((N_STEPS,)),
    ]
    for _ in range(DEPTH):
        scratch += [
            pltpu.VMEM((PK_HALF, HEAD), F32), pltpu.VMEM((PK_HALF, HEAD), F32), pltpu.VMEM((N_CHIP, PK_PIECE, HEAD), F32),
            pltpu.VMEM((PK_PIECE, HEAD), F32), pltpu.VMEM((PK_ROWS, HEAD), F32),
            pltpu.VMEM((24, 1, HEAD), F32), pltpu.VMEM((24, N_DEV, HEAD), F32),
            pltpu.SemaphoreType.DMA((6,)),
            pltpu.SemaphoreType.DMA((N_CHIP,)), pltpu.SemaphoreType.DMA((N_CHIP,)),
            pltpu.SemaphoreType.DMA((N_CHIP,)), pltpu.SemaphoreType.DMA((N_CHIP,)),
            pltpu.SemaphoreType.DMA((N_DEV - 1,)), pltpu.SemaphoreType.DMA((N_DEV - 1,)),
        ]
    return pl.pallas_call(
        body, name="wgrad", grid=(N_STEPS,), in_specs=in_specs, out_specs=out_specs, out_shape=out_shape,
        scratch_shapes=scratch,
        compiler_params=pltpu.CompilerParams(dimension_semantics=("arbitrary",), vmem_limit_bytes=VMEM_LIMIT),
    )(*args)


def _adamw(w, g, m, v):
    m = ADAM_B1 * m + (1.0 - ADAM_B1) * g
    v = ADAM_B2 * v + (1.0 - ADAM_B2) * (g * g)
    m_hat = m / (1.0 - ADAM_B1 ** ADAM_STEP)
    v_hat = v / (1.0 - ADAM_B2 ** ADAM_STEP)
    delta = -ADAM_LR * (m_hat / (jnp.sqrt(v_hat) + ADAM_EPS) + ADAM_WD * w)
    return delta, m, v


def _adam_sharded(c_all, dmods, ada, w_in_set, w_out_set):
    rows = D_MODEL // ADAM_PARTS

    def body(c_ref, dm_ref, wa_ref, ma_ref, va_ref, wi_ref, gi_ref, mi_ref, vi_ref, wo_ref, go_ref, mo_ref, vo_ref,
             ga_out, da_out, ma_out, va_out, di_out, mi_out, vi_out, do_out, mo_out, vo_out):
        l = pl.program_id(0)
        chip = 2 * lax.axis_index("x") + lax.axis_index("y")
        cv = c_ref[...]
        silu_c = (cv * _sigmoid(cv)).astype(BF16).astype(F32)
        pieces = []
        for k in range(W_ADA_COLS // HEAD):
            dk = dm_ref[l, (W_ADA_COLS // HEAD) * chip + k].astype(BF16).astype(F32)
            pieces.append(_dot_exact(silu_c, dk, TN))
        g = jnp.concatenate(pieces, axis=1)
        ga_out[...] = g
        da_out[...], ma_out[...], va_out[...] = _adamw(wa_ref[...], g, ma_ref[...], va_ref[...])
        di_out[...], mi_out[...], vi_out[...] = _adamw(wi_ref[...], gi_ref[...], mi_ref[...], vi_ref[...])
        do_out[...], mo_out[...], vo_out[...] = _adamw(wo_ref[...], go_ref[...], mo_ref[...], vo_ref[...])

    def blk(r, cols):
        return pl.BlockSpec((None, r, cols), lambda l, i: (l, i, 0))

    b_ada, b_in, b_out = blk(rows, W_ADA_COLS), blk(rows, W_IN_COLS), blk(W_OUT_ROWS // ADAM_PARTS, D_MODEL)
    shapes = [jax.ShapeDtypeStruct(a[0].shape, F32) for a in (ada, w_in_set, w_out_set)]
    return pl.pallas_call(
        body, name="adam_sharded", grid=(DEPTH, ADAM_PARTS),
        in_specs=[pl.BlockSpec((N_DEV, rows), lambda l, i: (0, i)), _const_in((DEPTH, 24, N_DEV, HEAD))]
        + [b_ada] * 3 + [b_in] * 4 + [b_out] * 4,
        out_specs=[b_ada] * 4 + [b_in] * 3 + [b_out] * 3,
        out_shape=[shapes[0]] * 4 + [shapes[1]] * 3 + [shapes[2]] * 3,
        compiler_params=pltpu.CompilerParams(dimension_semantics=("arbitrary", "arbitrary"), vmem_limit_bytes=VMEM_LIMIT),
    )(c_all, dmods, *ada, *w_in_set, *w_out_set)


def _adam_small(packs, dmods, weights, ms, vs):
    n = len(weights)

    def body(*refs):
        pack_refs, dm_refs = refs[0], refs[1]
        b = 2
        w_refs, m_refs, v_refs = refs[b:b + n], refs[b + n:b + 2 * n], refs[b + 2 * n:b + 3 * n]
        outs = refs[b + 3 * n:]
        g_refs, d_refs, nm_refs, nv_refs = outs[0:n], outs[n:2 * n], outs[2 * n:3 * n], outs[3 * n:4 * n]
        outs[4 * n][...] = pack_refs.at[DEPTH - 1][PK_LOSS:PK_LOSS + 1, 0:1] * (0.5 / D_MODEL)

        def lanes(l, row0, count):
            return jnp.concatenate([pack_refs.at[l][row0 + k:row0 + k + 1, :] for k in range(count)], axis=1)

        def update(idx, at, g):
            g_refs[idx][at] = g
            d_refs[idx][at], nm_refs[idx][at], nv_refs[idx][at] = _adamw(w_refs[idx][at], g, m_refs[idx][at], v_refs[idx][at])

        for l in range(DEPTH):
            row = (slice(l, l + 1), slice(None))
            g_b = None
            for d in range(N_DEV):
                part = dm_refs.at[l][:, d, :]
                g_b = part if g_b is None else g_b + part
            update(0, row, jnp.concatenate([g_b[k:k + 1, :] for k in range(24)], axis=1))
            for g in range(N_HEAD):
                update(1, (l, g), pack_refs.at[l][PK_W_POOL + g * HEAD:PK_W_POOL + (g + 1) * HEAD, :])
                update(5, (l, g), pack_refs.at[l][PK_W_SGU + g * HEAD:PK_W_SGU + (g + 1) * HEAD, :])
            update(2, row, lanes(l, PK_POOL_SCALE, N_HEAD))
            update(3, (l,), pack_refs.at[l][PK_SGU_LN_G:PK_SGU_LN_G + N_HEAD, :])
            update(4, (l,), pack_refs.at[l][PK_SGU_LN_B:PK_SGU_LN_B + N_HEAD, :])
            update(6, (l,), pack_refs.at[l][PK_B_SGU:PK_B_SGU + N_HEAD, :])
            update(7, row, lanes(l, PK_LN_G, D_MODEL // HEAD))
            update(8, row, lanes(l, PK_LN_B, D_MODEL // HEAD))

    vmem = pl.BlockSpec(memory_space=pltpu.VMEM)
    shapes = [jax.ShapeDtypeStruct(w.shape, F32) for w in weights]
    return pl.pallas_call(
        body, name="adam_small", in_specs=[vmem] * (2 + 3 * n), out_specs=[vmem] * (4 * n + 1),
        out_shape=shapes * 4 + [jax.ShapeDtypeStruct((1, 1), F32)],
        compiler_params=pltpu.CompilerParams(vmem_limit_bytes=VMEM_LIMIT),
    )(packs, dmods, *weights, *ms, *vs)


def kernel(x, c, w_ada, b_ada, w_in, w_pool, pool_scale, sgu_ln_g, sgu_ln_b, w_sgu, b_sgu, w_out, ln_g, ln_b, loss_target, m_w_ada, m_b_ada, m_w_in, m_w_pool, m_pool_scale, m_sgu_ln_g, m_sgu_ln_b, m_w_sgu, m_b_sgu, m_w_out, m_ln_g, m_ln_b, v_w_ada, v_b_ada, v_w_in, v_w_pool, v_pool_scale, v_sgu_ln_g, v_sgu_ln_b, v_w_sgu, v_b_sgu, v_w_out, v_ln_g, v_ln_b):
    small = (w_pool, pool_scale, sgu_ln_g, sgu_ln_b, w_sgu, b_sgu)
    *saved0, w_in0, w_out0, w_in1, w_out1, mod, c_all = _forward_first(x, c, w_ada, b_ada, w_in, w_out, small, ln_g, ln_b)
    *saved1, dout, sq = _forward_last(saved0[3], mod, w_in1, w_out1, small, ln_g, ln_b, loss_target)

    dx1, *shared = _backward_layer(1, dout, saved1, mod, w_in1, w_out1, small, ln_g, sq=sq)
    dx0, h, cat, dy, dproj, pack, dmod = _backward_layer(0, dx1, saved0, mod, w_in0, w_out0, small, ln_g, shared=shared)
    g_in, g_out, pack, dmods = _wgrad_reduce(h, dproj, cat, dy, pack, dmod)

    g_out = g_out.reshape(DEPTH, W_OUT_ROWS, D_MODEL)
    big = _adam_sharded(c_all, dmods, (w_ada, m_w_ada, v_w_ada), (w_in, g_in, m_w_in, v_w_in), (w_out, g_out, m_w_out, v_w_out))
    ada, win, wout = big[0:4], (g_in, *big[4:7]), (g_out, *big[7:10])
    small_w = (b_ada, w_pool, pool_scale, sgu_ln_g, sgu_ln_b, w_sgu, b_sgu, ln_g, ln_b)
    small_m = (m_b_ada, m_w_pool, m_pool_scale, m_sgu_ln_g, m_sgu_ln_b, m_w_sgu, m_b_sgu, m_ln_g, m_ln_b)
    small_v = (v_b_ada, v_w_pool, v_pool_scale, v_sgu_ln_g, v_sgu_ln_b, v_w_sgu, v_b_sgu, v_ln_g, v_ln_b)
    res = _adam_small(pack, dmods, small_w, small_m, small_v)
    n = len(small_w)
    loss = res[4 * n].reshape(())

    def ordered(k):
        s = res[k * n:(k + 1) * n]
        return (ada[k], s[0], win[k], s[1], s[2], s[3], s[4], s[5], s[6], wout[k], s[7], s[8])

    return (loss, dx0[None], *ordered(0), *ordered(1), *ordered(2), *ordered(3))
```

```python
import jax
import jax.numpy as jnp
from jax import lax
from jax.experimental import pallas as pl
from jax.experimental.pallas import tpu as pltpu

F32 = jnp.float32
BF16 = jnp.bfloat16
MESH = pl.DeviceIdType.MESH

N_DEV = 8
N_CHIP = 4
DEPTH = 2
SEQ = 2048
D_MODEL = 1024
D_POOL = 512
D_PROJ = 2560
HEAD = 128
N_HEAD = 4
ROWS = 256
N_TILE = SEQ // ROWS
HALO = 16
W_IN_COLS = D_PROJ // N_CHIP
W_OUT_ROWS = D_MODEL // N_CHIP
W_ADA_COLS = 3 * D_MODEL // N_CHIP
HALF_IN = D_MODEL // 2
HALF_OUT = W_OUT_ROWS // 2
DEEPNORM_ALPHA = (2.0 * DEPTH) ** 0.25
LN_EPS = 1e-5
INV_SQRT2 = 0.7071067811865476
INV_SQRT_2PI = 0.3989422804014327

ADAM_LR = 0.001
ADAM_B1 = 0.9
ADAM_B2 = 0.999
ADAM_EPS = 1e-08
ADAM_WD = 0.01
ADAM_STEP = 10
ADAM_PARTS = 2

PK_W_POOL = 0
PK_W_SGU = 512
PK_POOL_SCALE = 1024
PK_SGU_LN_G = 1032
PK_SGU_LN_B = 1040
PK_B_SGU = 1048
PK_LN_G = 1056
PK_LN_B = 1064
PK_LOSS = 1072
PK_ROWS = 1088
PK_HALF = PK_ROWS // 2
PK_PIECE = PK_HALF // N_CHIP

VMEM_LIMIT = 56 * 1024 * 1024

GATHER_SECOND_ROUND_STEP = 0
GATHER_PASS_STEP = 2

NN = (((1,), (0,)), ((), ()))
NT = (((1,), (1,)), ((), ()))
TN = (((0,), (0,)), ((), ()))


def _dot(a, b, dims=NN):
    return lax.dot_general(a, b, dims, preferred_element_type=F32)


def _dot_exact(a, b, dims=NN):
    return lax.dot_general(a, b, dims, preferred_element_type=F32, precision=lax.Precision.HIGHEST)


def _layer_norm(v):
    mu = jnp.mean(v, axis=-1, keepdims=True)
    d = v - mu
    var = jnp.mean(d * d, axis=-1, keepdims=True)
    rstd = lax.rsqrt(var + LN_EPS)
    return d * rstd, rstd


def _layer_norm_bwd(dvhat, vhat, rstd):
    m1 = jnp.mean(dvhat, axis=-1, keepdims=True)
    m2 = jnp.mean(dvhat * vhat, axis=-1, keepdims=True)
    return rstd * (dvhat - m1 - vhat * m2)


def _sigmoid(v):
    return 1.0 / (1.0 + jnp.exp(-v))


def _gelu_parts(v):
    phi = 0.5 * (1.0 + lax.erf(v * INV_SQRT2))
    pdf = INV_SQRT_2PI * jnp.exp(-0.5 * v * v)
    return phi, pdf


def _sum_rows(v):
    return jnp.sum(v, axis=0, keepdims=True)


def _window_sums(ext, toward_later):
    n = ext.shape[0]

    def shifted(v, k):
        return pltpu.roll(v, (n - k) if toward_later else k, 0)

    s2 = ext + shifted(ext, 1)
    r4 = s2[:, HEAD:]
    s4 = r4 + shifted(r4, 2)
    r8 = s4[:, HEAD:]
    s8 = r8 + shifted(r8, 4)
    r16 = s8[:, HEAD:]
    s16 = r16 + shifted(r16, 8)
    return jnp.concatenate([s2[:, :HEAD], s4[:, :HEAD], s8[:, :HEAD], s16], axis=1)


def _window_counts(row0):
    t1 = row0 + 1 + lax.broadcasted_iota(jnp.int32, (ROWS, D_POOL), 0)
    lane = lax.broadcasted_iota(jnp.int32, (ROWS, D_POOL), 1)
    width = jnp.where(lane < HEAD, 2, jnp.where(lane < 2 * HEAD, 4, jnp.where(lane < 3 * HEAD, 8, 16)))
    return jnp.minimum(t1, width).astype(F32)


def _causal_mask():
    r = lax.broadcasted_iota(jnp.int32, (HEAD, HEAD), 0)
    s = lax.broadcasted_iota(jnp.int32, (HEAD, HEAD), 1)
    return r >= s


def _chunks_to_lanes(v):
    return jnp.concatenate([v[n * HEAD:(n + 1) * HEAD] for n in range(ROWS // HEAD)], axis=1)


def _lanes_to_chunks(v):
    return jnp.concatenate([v[:, n * HEAD:(n + 1) * HEAD] for n in range(ROWS // HEAD)], axis=0)


def _pack_stats(rstd_x, rstd_z, rstd_v):
    lane = lax.broadcasted_iota(jnp.int32, (ROWS, HEAD), 1)
    packed = rstd_x
    for k, r in enumerate([rstd_z] + list(rstd_v)):
        packed = jnp.where(lane < 16 * (k + 1), packed, r)
    return packed


def _unpack_stats(stats):
    cols = [stats[:, 16 * k:16 * k + 1] for k in range(2 + N_HEAD)]
    return cols[0], cols[1], cols[2:]


def _mixer(proj, halo, row0, wpool_ref, pscale, sgu_g_ref, sgu_b_ref, wsgu_ref, bsgu_ref, saved=None):
    xa = proj[:, 0:512]
    ga = proj[:, 512:1024]
    u = proj[:, 1024:1536]
    v = proj[:, 1536:2048]
    gb = proj[:, 2048:2560]
    ext = jnp.concatenate([halo, xa], axis=0)
    win = _window_sums(ext, toward_later=False)[HALO:]
    cnt = _window_counts(row0)
    pooled = (win / cnt - xa).astype(BF16)
    pw = jnp.concatenate(
        [_dot(pooled[:, g * HEAD:(g + 1) * HEAD], wpool_ref[g].astype(BF16)) for g in range(N_HEAD)], axis=1)
    sig_a = _sigmoid(ga) if saved is None else saved["sig_a"]
    ya = pw * pscale * (ga * sig_a)
    phi_u, pdf_u = _gelu_parts(u)
    phi_v, pdf_v = _gelu_parts(v)
    gu = u * phi_u
    gv = v * phi_v
    sig_b = _sigmoid(gb) if saved is None else saved["sig_b"]
    silu_b = gb * sig_b
    mask = _causal_mask()
    diag = lax.broadcasted_iota(jnp.int32, (HEAD, HEAD), 0) == lax.broadcasted_iota(jnp.int32, (HEAD, HEAD), 1)
    vhat, rstd_v, vln_l, mixed = [], [], [], []
    for h in range(N_HEAD):
        if saved is None:
            vh, rh = _layer_norm(gv[:, h * HEAD:(h + 1) * HEAD])
        else:
            vh, rh = saved["vhat"][h], saved["rstd_v"][h]
        ln = (vh * sgu_g_ref[h:h + 1, :] + sgu_b_ref[h:h + 1, :]).astype(BF16)
        ln_l = _chunks_to_lanes(ln)
        wm = jnp.where(mask, wsgu_ref[h], 0.0).astype(BF16)
        bias = jnp.sum(jnp.where(diag, jnp.broadcast_to(bsgu_ref[h:h + 1, :], (HEAD, HEAD)), 0.0), axis=1, keepdims=True)
        mx = _lanes_to_chunks(_dot(wm, ln_l) + bias)
        vhat.append(vh)
        rstd_v.append(rh)
        vln_l.append(ln_l)
        mixed.append(mx)
    mixed = jnp.concatenate(mixed, axis=1)
    yb = gu * mixed * silu_b
    return dict(xa=xa, ga=ga, u=u, gb=gb, cnt=cnt, pooled=pooled, pw=pw, sig_a=sig_a, ya=ya, phi_u=phi_u, pdf_u=pdf_u,
                phi_v=phi_v, pdf_v=pdf_v, gu=gu, sig_b=sig_b, silu_b=silu_b, vhat=vhat, rstd_v=rstd_v, vln_l=vln_l,
                mixed=mixed, yb=yb, mask=mask)


def _const(shape, *index):
    lead = tuple(index) + (0,) * (len(shape) - len(index))
    return pl.BlockSpec(shape, lambda *_: lead)


def _const_in(shape, *index):
    lead = tuple(index) + (0,) * (len(shape) - len(index))
    return pl.BlockSpec(shape, lambda *_: lead, pipeline_mode=pl.Buffered(1))


def _layer_weight_specs(l):
    return [
        _const_in((None, N_HEAD, HEAD, HEAD), l),
        _const_in((DEPTH, D_POOL)),
        _const_in((None, N_HEAD, HEAD), l),
        _const_in((None, N_HEAD, HEAD), l),
        _const_in((None, N_HEAD, HEAD, HEAD), l),
        _const_in((None, N_HEAD, HEAD), l),
    ]


def _forward_tile(l, i, x_ref, mod_ref, win_ref, wout_ref, small_refs, lng_ref, lnb_ref, carry_ref, saved_refs):
    wpool_ref, pscale_ref, sgu_g_ref, sgu_b_ref, wsgu_ref, bsgu_ref = small_refs
    proj_ref, y_ref, xn_ref, zn_ref, stats_ref, sig_ref, vhat_ref = saved_refs
    x = x_ref[...]
    if l > 0:
        x = x * lng_ref[l - 1:l, :] + lnb_ref[l - 1:l, :]
    shift, scale, gate = mod_ref[0:1, :], mod_ref[1:2, :], mod_ref[2:3, :]
    xn, rstd_x = _layer_norm(x)
    xn_ref[...] = xn.astype(xn_ref.dtype)
    h = xn * (1.0 + scale) + shift
    proj = _dot(h.astype(BF16), win_ref[...])
    proj_ref[...] = proj
    m = _mixer(proj, carry_ref[...], i * ROWS, wpool_ref, pscale_ref[l:l + 1, :], sgu_g_ref, sgu_b_ref, wsgu_ref, bsgu_ref)
    carry_ref[...] = m["xa"][ROWS - HALO:]
    sig_ref[...] = jnp.concatenate([m["sig_a"], m["sig_b"]], axis=1).astype(sig_ref.dtype)
    vhat_ref[...] = jnp.concatenate(m["vhat"], axis=1).astype(vhat_ref.dtype)
    cat = jnp.concatenate([m["ya"], m["yb"]], axis=1).astype(BF16)
    y = _dot(cat, wout_ref[...])
    y_ref[...] = y.astype(y_ref.dtype)
    zn, rstd_z = _layer_norm(DEEPNORM_ALPHA * x + gate * y)
    zn_ref[...] = zn
    stats_ref[...] = _pack_stats(rstd_x, rstd_z, m["rstd_v"])
    return zn


SAVED_COLS = (D_PROJ, D_MODEL, D_MODEL, D_MODEL, HEAD, D_MODEL, D_POOL)
SAVED_TYPES = (F32, BF16, BF16, F32, F32, BF16, BF16)


def _saved_outputs():
    return ([jax.ShapeDtypeStruct((SEQ, cols), t) for cols, t in zip(SAVED_COLS, SAVED_TYPES)],
            [pl.BlockSpec((ROWS, cols), lambda i: (i, 0)) for cols in SAVED_COLS])


def _forward_last(zn_prev, mod, w_in, w_out, small, ln_g, ln_b, target):
    l = DEPTH - 1
    n_saved = len(SAVED_COLS)

    def body(*refs):
        x_ref, mod_ref, win_ref, wout_ref = refs[:4]
        small_refs, lng_ref, lnb_ref, tgt_ref = refs[4:10], refs[10], refs[11], refs[12]
        saved_refs = refs[13:13 + n_saved]
        dout_ref, loss_ref, carry_ref = refs[13 + n_saved:]
        i = pl.program_id(0)

        @pl.when(i == 0)
        def _():
            carry_ref[...] = jnp.zeros_like(carry_ref)
            loss_ref[...] = jnp.zeros_like(loss_ref)

        zn = _forward_tile(l, i, x_ref, mod_ref, win_ref, wout_ref, small_refs, lng_ref, lnb_ref, carry_ref, saved_refs)
        err = zn * lng_ref[l:l + 1, :] + lnb_ref[l:l + 1, :] - tgt_ref[...]
        dout_ref[...] = err * (1.0 / D_MODEL)
        loss_ref[...] += jnp.sum(err * err)

    tile = pl.BlockSpec((ROWS, D_MODEL), lambda i: (i, 0))
    tile3 = pl.BlockSpec((None, ROWS, D_MODEL), lambda i: (0, i, 0))
    in_specs = [tile, _const_in((None, 8, D_MODEL), l), _const_in((D_MODEL, D_PROJ)), _const_in((D_MODEL, D_MODEL))]
    in_specs += _layer_weight_specs(l) + [_const_in((DEPTH, D_MODEL)), _const_in((DEPTH, D_MODEL)), tile3]
    out_shape, out_specs = _saved_outputs()
    out_shape += [jax.ShapeDtypeStruct((SEQ, D_MODEL), F32), jax.ShapeDtypeStruct((8, HEAD), F32)]
    out_specs += [tile, _const((8, HEAD))]
    return pl.pallas_call(
        body, name="fwd_last", grid=(N_TILE,), in_specs=in_specs, out_specs=out_specs, out_shape=out_shape,
        scratch_shapes=[pltpu.VMEM((HALO, D_POOL), F32)],
        compiler_params=pltpu.CompilerParams(dimension_semantics=("arbitrary",), vmem_limit_bytes=VMEM_LIMIT),
    )(zn_prev, mod, w_in, w_out, *small, ln_g, ln_b, target)


def _backward_layer(l, dout, saved, mod, w_in, w_out, small, ln_g, sq=None, shared=None):
    has_loss = sq is not None

    def body(*refs):
        (dout_ref, proj_ref, y_ref, xn_ref, zn_ref, stats_ref, sig_ref, vhat_ref, halo_ref, mod_ref, win_ref, wout_ref,
         wpool_ref, pscale_ref, sgu_g_ref, sgu_b_ref, wsgu_ref, bsgu_ref, lng_ref) = refs[:19]
        n_in = 20 if has_loss else 19 + 6
        dx_ref, h_ref, cat_ref, dy_ref, dproj_ref, pack_ref, dmod_ref, carry_ref = refs[n_in:n_in + 8]
        i = pl.program_id(0)
        tile = N_TILE - 1 - i

        @pl.when(i == 0)
        def _():
            carry_ref[...] = jnp.zeros_like(carry_ref)
            pack_ref[...] = jnp.zeros_like(pack_ref)
            dmod_ref[...] = jnp.zeros_like(dmod_ref)
            if has_loss:
                pack_ref[PK_LOSS:PK_LOSS + 8, :] = refs[19][...]

        xn = xn_ref[...].astype(F32)
        zn = zn_ref[...]
        y = y_ref[...].astype(F32)
        dout = dout_ref[...]
        rstd_x, rstd_z, rstd_v = _unpack_stats(stats_ref[...])
        kept = dict(sig_a=sig_ref[:, :D_POOL].astype(F32), sig_b=sig_ref[:, D_POOL:].astype(F32), rstd_v=rstd_v,
                    vhat=[vhat_ref[:, hd * HEAD:(hd + 1) * HEAD].astype(F32) for hd in range(N_HEAD)])
        pscale = pscale_ref[l:l + 1, :]
        shift, scale, gate = mod_ref[0:1, :], mod_ref[1:2, :], mod_ref[2:3, :]
        h = xn * (1.0 + scale) + shift
        h_ref[...] = h.astype(BF16)
        g_ln_g = _sum_rows(dout * zn)
        g_ln_b = _sum_rows(dout)
        dz = _layer_norm_bwd(dout * lng_ref[l:l + 1, :], zn, rstd_z)
        d_gate = _sum_rows(dz * y)
        dy = (gate * dz).astype(BF16)
        dy_ref[...] = dy

        halo = jnp.where(tile > 0, halo_ref[...], 0.0)
        m = _mixer(proj_ref[...], halo, tile * ROWS, wpool_ref, pscale, sgu_g_ref, sgu_b_ref, wsgu_ref, bsgu_ref,
                   saved=kept)
        cat_ref[...] = jnp.concatenate([m["ya"], m["yb"]], axis=1).astype(BF16)
        dcat = _dot(dy, wout_ref[...], NT)
        dya = dcat[:, :D_POOL]
        dyb = dcat[:, D_POOL:]

        ga, sig_a = m["ga"], m["sig_a"]
        dp = dya * (ga * sig_a)
        d_ga = dya * (m["pw"] * pscale) * (sig_a * (1.0 + ga * (1.0 - sig_a)))
        g_pscale = _sum_rows(dp * m["pw"])
        dpw = (dp * pscale).astype(BF16)
        dpooled = []
        for g in range(N_HEAD):
            cols = slice(g * HEAD, (g + 1) * HEAD)
            pack_ref[PK_W_POOL + g * HEAD:PK_W_POOL + (g + 1) * HEAD, :] += _dot(m["pooled"][:, cols], dpw[:, cols], TN)
            dpooled.append(_dot(dpw[:, cols], wpool_ref[g].astype(BF16), NT))
        dpooled = jnp.concatenate(dpooled, axis=1)
        q = dpooled / m["cnt"]
        ext = jnp.concatenate([q, carry_ref[...]], axis=0)
        d_xa = _window_sums(ext, toward_later=True)[:ROWS] - dpooled
        carry_ref[...] = q[:HALO]

        gu, mixed, silu_b, gb, sig_b = m["gu"], m["mixed"], m["silu_b"], m["gb"], m["sig_b"]
        d_mixed = dyb * gu * silu_b
        d_gu = dyb * mixed * silu_b
        d_gb = dyb * gu * mixed * (sig_b * (1.0 + gb * (1.0 - sig_b)))
        d_u = d_gu * (m["phi_u"] + m["u"] * m["pdf_u"])
        ones = jnp.ones((8, HEAD), F32)
        d_v = []
        for hd in range(N_HEAD):
            cols = slice(hd * HEAD, (hd + 1) * HEAD)
            dm = d_mixed[:, cols]
            dm_l = _chunks_to_lanes(dm.astype(BF16))
            g_w = _dot(dm_l, m["vln_l"][hd], NT)
            pack_ref[PK_W_SGU + hd * HEAD:PK_W_SGU + (hd + 1) * HEAD, :] += jnp.where(m["mask"], g_w, 0.0)
            dm_sum = dm[0:HEAD]
            for n in range(1, ROWS // HEAD):
                dm_sum = dm_sum + dm[n * HEAD:(n + 1) * HEAD]
            pack_ref[PK_B_SGU + hd:PK_B_SGU + hd + 1, :] += _dot_exact(ones, dm_sum, NT)[0:1]
            wm = jnp.where(m["mask"], wsgu_ref[hd], 0.0).astype(BF16)
            d_vln = _lanes_to_chunks(_dot(wm, dm_l, TN))
            vhat = m["vhat"][hd]
            pack_ref[PK_SGU_LN_G + hd:PK_SGU_LN_G + hd + 1, :] += _sum_rows(d_vln * vhat)
            pack_ref[PK_SGU_LN_B + hd:PK_SGU_LN_B + hd + 1, :] += _sum_rows(d_vln)
            d_v.append(_layer_norm_bwd(d_vln * sgu_g_ref[hd:hd + 1, :], vhat, m["rstd_v"][hd]))
        v = proj_ref[:, 1536:2048]
        d_v = jnp.concatenate(d_v, axis=1) * (m["phi_v"] + v * m["pdf_v"])

        dproj = jnp.concatenate([d_xa, d_ga, d_u, d_v, d_gb], axis=1).astype(BF16)
        dproj_ref[...] = dproj
        dh = _dot(dproj, win_ref[...], NT)
        d_scale = _sum_rows(dh * xn)
        d_shift = _sum_rows(dh)
        dx_ref[...] = DEEPNORM_ALPHA * dz + _layer_norm_bwd(dh * (1.0 + scale), xn, rstd_x)

        dmod_ref[0:1, :] += d_shift
        dmod_ref[1:2, :] += d_scale
        dmod_ref[2:3, :] += d_gate
        for g in range(N_HEAD):
            pack_ref[PK_POOL_SCALE + g:PK_POOL_SCALE + g + 1, :] += g_pscale[:, g * HEAD:(g + 1) * HEAD]
        for k in range(D_MODEL // HEAD):
            pack_ref[PK_LN_G + k:PK_LN_G + k + 1, :] += g_ln_g[:, k * HEAD:(k + 1) * HEAD]
            pack_ref[PK_LN_B + k:PK_LN_B + k + 1, :] += g_ln_b[:, k * HEAD:(k + 1) * HEAD]

    def rev(i):
        return (N_TILE - 1 - i, 0)

    tile = pl.BlockSpec((ROWS, D_MODEL), rev)
    halo = pl.BlockSpec((HALO, D_POOL), lambda i: (jnp.maximum((N_TILE - 1 - i) * (ROWS // HALO) - 1, 0), 0))
    in_specs = [tile] + [pl.BlockSpec((ROWS, a.shape[1]), rev) for a in saved] + [halo]
    in_specs += [_const_in((None, 8, D_MODEL), l), _const_in((D_MODEL, D_PROJ)), _const_in((D_MODEL, D_MODEL))]
    in_specs += _layer_weight_specs(l) + [_const_in((DEPTH, D_MODEL))]
    args = [dout, *saved, saved[0], mod, w_in, w_out, *small, ln_g]
    stacked = lambda cols: pl.BlockSpec((None, ROWS, cols), lambda i: (l, N_TILE - 1 - i, 0))
    out_shape = [jax.ShapeDtypeStruct((SEQ, D_MODEL), F32), jax.ShapeDtypeStruct((DEPTH, SEQ, D_MODEL), BF16),
                 jax.ShapeDtypeStruct((DEPTH, SEQ, D_MODEL), BF16), jax.ShapeDtypeStruct((DEPTH, SEQ, D_MODEL), BF16),
                 jax.ShapeDtypeStruct((DEPTH, SEQ, D_PROJ), BF16), jax.ShapeDtypeStruct((DEPTH, PK_ROWS, HEAD), F32),
                 jax.ShapeDtypeStruct((DEPTH, 8, D_MODEL), F32)]
    out_specs = [tile, stacked(D_MODEL), stacked(D_MODEL), stacked(D_MODEL), stacked(D_PROJ),
                 _const((None, PK_ROWS, HEAD), l), _const((None, 8, D_MODEL), l)]
    aliases = {}
    if has_loss:
        in_specs.append(_const_in((8, HEAD)))
        args.append(sq)
    else:
        aliases = {len(args) + k: 1 + k for k in range(len(shared))}
        in_specs += [pl.BlockSpec(memory_space=pl.ANY)] * len(shared)
        args += list(shared)
    return pl.pallas_call(
        body, name="bwd_last" if has_loss else "bwd_first", grid=(N_TILE,), in_specs=in_specs, out_specs=out_specs,
        out_shape=out_shape, scratch_shapes=[pltpu.VMEM((HALO, D_POOL), F32)], input_output_aliases=aliases,
        compiler_params=pltpu.CompilerParams(dimension_semantics=("arbitrary",), vmem_limit_bytes=VMEM_LIMIT),
    )(*args)


def _flip(v, f):
    return v + f - 2 * v * f


class _Place:
    def __init__(self):
        x, y, c = lax.axis_index("x"), lax.axis_index("y"), lax.axis_index("c")
        self.x, self.y, self.c = x, y, c
        self.chip = 2 * x + y
        self.dev = 4 * x + 2 * y + c
        self.sibling = (x, y, 1 - c)
        x1, y1 = _flip(x, 1 - c), _flip(y, c)
        x2, y2 = _flip(x, c), _flip(y, 1 - c)
        self.first = (x1, y1, c)
        self.second = (x2, y2, c)
        self.chip_first = 2 * x1 + y1
        self.chip_second = 2 * x2 + y2
        self.chip_far = 2 * (1 - x) + (1 - y)
        self.my_first_coord = jnp.where(c == 0, x, y)

    def first_coord(self, ch):
        return jnp.where(self.c == 0, ch // 2, ch % 2)

    def others(self):
        return [(_flip(self.x, (r >> 2) & 1), _flip(self.y, (r >> 1) & 1), _flip(self.c, r & 1)) for r in range(1, N_DEV)]

    def other_chips(self):
        return [(1 - self.x, self.y), (self.x, 1 - self.y), (1 - self.x, 1 - self.y)]


class _WeightGather:
    CHUNKS = 4
    N_SEMS = 12 * CHUNKS

    def __init__(self, place, win, wout, send, recv):
        self.p, self.win, self.wout, self.send, self.recv = place, win, wout, send, recv
        p = place
        self.plan = [(p.chip, p.first), (p.chip, p.second), (p.chip_first, p.second),
                     (p.chip_first, p.sibling), (p.chip_second, p.sibling), (p.chip_far, p.sibling)]

    def _copies(self, k, q):
        ch, target = self.plan[k]
        n_in, n_out = HALF_IN // self.CHUNKS, HALF_OUT // self.CHUNKS
        rows_in = pl.ds(pl.multiple_of(self.p.c * HALF_IN + q * n_in, n_in), n_in)
        cols_in = pl.ds(pl.multiple_of(ch * W_IN_COLS, 128), W_IN_COLS)
        rows_out = pl.ds(pl.multiple_of(ch * W_OUT_ROWS + self.p.c * HALF_OUT + q * n_out, n_out), n_out)
        r_in = self.win.at[rows_in, cols_in]
        r_out = self.wout.at[rows_out, :]
        s = 2 * (6 * q + k)
        return [pltpu.make_async_remote_copy(r_in, r_in, self.send.at[s], self.recv.at[s],
                                             device_id=target, device_id_type=MESH),
                pltpu.make_async_remote_copy(r_out, r_out, self.send.at[s + 1], self.recv.at[s + 1],
                                             device_id=target, device_id_type=MESH)]

    def _start(self, k, q):
        for cp in self._copies(k, q):
            cp.start()

    def _landed(self, k, q):
        for cp in self._copies(k, q):
            cp.wait_recv()

    def start_first_round(self):
        for q in range(self.CHUNKS):
            self._start(0, q)

    def start_second_round(self, q):
        self._landed(0, q)
        self._start(1, q)
        self._start(2, q)
        self._start(3, q)

    def pass_second_round(self, q):
        self._landed(1, q)
        self._start(4, q)
        self._landed(2, q)
        self._start(5, q)

    def finish(self):
        for q in range(self.CHUNKS):
            for k in (3, 4, 5):
                self._landed(k, q)
        for q in range(self.CHUNKS):
            for k in range(len(self.plan)):
                for cp in self._copies(k, q):
                    cp.wait_send()


def _forward_first(x, c_vec, w_ada, b_ada, w_in, w_out, small, ln_g, ln_b):
    n_saved = len(SAVED_COLS)

    def body(*refs):
        x_ref, c_ref, wada_hbm, bada_ref, win_hbm, wout_hbm = refs[:6]
        small_refs, lng_ref, lnb_ref = refs[6:12], refs[12], refs[13]
        saved_refs = refs[14:14 + n_saved]
        win0, wout0, win1, wout1, mod_out, c_out = refs[14 + n_saved:20 + n_saved]
        (carry_ref, wada_ref, win_ref, wout_ref, win_bf, wout_bf, mod_mine, mod_all, c_all, mod_ref, win_v, wout_v,
         g0_send, g0_recv, g1_send, g1_recv, c_send, c_recv, mod_send, mod_recv, local_sem) = refs[20 + n_saved:]
        i = pl.program_id(0)
        p = _Place()
        gather0 = _WeightGather(p, win0, wout0, g0_send, g0_recv)
        gather1 = _WeightGather(p, win1, wout1, g1_send, g1_recv)

        @pl.when(i == 0)
        def _():
            carry_ref[...] = jnp.zeros_like(carry_ref)
            loads = [pltpu.make_async_copy(win_hbm.at[0], win_ref.at[0], local_sem.at[4]),
                     pltpu.make_async_copy(wout_hbm.at[0], wout_ref.at[0], local_sem.at[5]),
                     pltpu.make_async_copy(win_hbm.at[1], win_ref.at[1], local_sem.at[6]),
                     pltpu.make_async_copy(wout_hbm.at[1], wout_ref.at[1], local_sem.at[7]),
                     pltpu.make_async_copy(wada_hbm, wada_ref, local_sem.at[8])]
            for cp in loads:
                cp.start()

            c_all[pl.ds(p.dev, 1), :] = c_ref[...]
            mine = c_all.at[pl.ds(p.dev, 1), :]
            c_copies = [pltpu.make_async_remote_copy(mine, mine, c_send.at[r], c_recv.at[r], device_id=d, device_id_type=MESH)
                        for r, d in enumerate(p.others())]
            for cp in c_copies:
                cp.start()

            cols = pl.ds(pl.multiple_of(p.chip * W_IN_COLS, 128), W_IN_COLS)
            rows = pl.ds(pl.multiple_of(p.chip * W_OUT_ROWS, W_OUT_ROWS), W_OUT_ROWS)
            own = [pltpu.make_async_copy(win_bf.at[0], win0.at[:, cols], local_sem.at[0]),
                   pltpu.make_async_copy(wout_bf.at[0], wout0.at[rows, :], local_sem.at[1]),
                   pltpu.make_async_copy(win_bf.at[1], win1.at[:, cols], local_sem.at[2]),
                   pltpu.make_async_copy(wout_bf.at[1], wout1.at[rows, :], local_sem.at[3])]
            for l in range(DEPTH):
                loads[2 * l].wait()
                win_bf[l] = win_ref[l].astype(BF16)
                own[2 * l].start()
                loads[2 * l + 1].wait()
                wout_bf[l] = wout_ref[l].astype(BF16)
                own[2 * l + 1].start()
                if l == 0:
                    own[0].wait()
                    own[1].wait()
                    gather0.start_first_round()
            for cp in c_copies:
                cp.wait()
            loads[4].wait()

            cv = c_all[...]
            c_out[...] = cv
            silu_c = (cv * _sigmoid(cv)).astype(BF16)
            for l in range(DEPTH):
                mod_mine[l] = _dot(silu_c, wada_ref[l].astype(BF16))
            mod_all[p.chip] = mod_mine[...]
            m_copies = [pltpu.make_async_remote_copy(mod_mine, mod_all.at[p.chip], mod_send.at[k], mod_recv.at[k],
                                                     device_id=(px, py, p.c), device_id_type=MESH)
                        for k, (px, py) in enumerate(p.other_chips())]
            for cp in m_copies:
                cp.start()
            for q in range(gather0.CHUNKS):
                gather0.start_second_round(q)
            own[2].wait()
            own[3].wait()
            gather1.start_first_round()
            for cp in m_copies:
                cp.wait()
            mod_ref[...] = jnp.zeros_like(mod_ref)
            for l in range(DEPTH):
                full = jnp.concatenate([mod_all[ch, l, pl.ds(p.dev, 1), :] for ch in range(N_CHIP)], axis=1) + bada_ref[l:l + 1, :]
                for k in range(3):
                    mod_ref[l, k:k + 1, :] = full[:, k * D_MODEL:(k + 1) * D_MODEL]
            mod_out[...] = mod_ref[...]
            for q in range(gather0.CHUNKS):
                gather0.pass_second_round(q)
            gather0.finish()
            fetch = [pltpu.make_async_copy(win0, win_v, local_sem.at[9]), pltpu.make_async_copy(wout0, wout_v, local_sem.at[10])]
            for cp in fetch:
                cp.start()
            for cp in fetch:
                cp.wait()

        for q in range(_WeightGather.CHUNKS):
            @pl.when(i == GATHER_SECOND_ROUND_STEP + q)
            def _(q=q):
                gather1.start_second_round(q)

            @pl.when(i == GATHER_PASS_STEP + q)
            def _(q=q):
                gather1.pass_second_round(q)

        _forward_tile(0, i, x_ref, mod_ref.at[0], win_v, wout_v, small_refs, lng_ref, lnb_ref, carry_ref, saved_refs)

        @pl.when(i == N_TILE - 1)
        def _():
            gather1.finish()

    hbm = pl.BlockSpec(memory_space=pl.ANY)
    tile3 = pl.BlockSpec((None, ROWS, D_MODEL), lambda i: (0, i, 0))
    in_specs = [tile3, _const_in((1, D_MODEL)), hbm, _const_in((DEPTH, 3 * D_MODEL)), hbm, hbm]
    in_specs += _layer_weight_specs(0) + [_const_in((DEPTH, D_MODEL)), _const_in((DEPTH, D_MODEL))]
    out_shape, out_specs = _saved_outputs()
    w_in_shape = jax.ShapeDtypeStruct((D_MODEL, D_PROJ), BF16)
    w_out_shape = jax.ShapeDtypeStruct((D_MODEL, D_MODEL), BF16)
    out_shape += [w_in_shape, w_out_shape, w_in_shape, w_out_shape,
                  jax.ShapeDtypeStruct((DEPTH, 8, D_MODEL), F32), jax.ShapeDtypeStruct((N_DEV, D_MODEL), F32)]
    out_specs += [hbm, hbm, hbm, hbm, _const((DEPTH, 8, D_MODEL)), _const((N_DEV, D_MODEL))]
    gather_sems = [pltpu.SemaphoreType.DMA((_WeightGather.N_SEMS,))] * 4
    scratch = [
        pltpu.VMEM((HALO, D_POOL), F32),
        pltpu.VMEM(w_ada.shape, F32), pltpu.VMEM(w_in.shape, F32), pltpu.VMEM(w_out.shape, F32),
        pltpu.VMEM((DEPTH, D_MODEL, W_IN_COLS), BF16), pltpu.VMEM((DEPTH, W_OUT_ROWS, D_MODEL), BF16),
        pltpu.VMEM((DEPTH, N_DEV, W_ADA_COLS), F32), pltpu.VMEM((N_CHIP, DEPTH, N_DEV, W_ADA_COLS), F32),
        pltpu.VMEM((N_DEV, D_MODEL), F32), pltpu.VMEM((DEPTH, 8, D_MODEL), F32),
        pltpu.VMEM((D_MODEL, D_PROJ), BF16), pltpu.VMEM((D_MODEL, D_MODEL), BF16),
    ] + gather_sems + [
        pltpu.SemaphoreType.DMA((7,)), pltpu.SemaphoreType.DMA((7,)),
        pltpu.SemaphoreType.DMA((3,)), pltpu.SemaphoreType.DMA((3,)),
        pltpu.SemaphoreType.DMA((11,)),
    ]
    return pl.pallas_call(
        body, name="fwd_first", grid=(N_TILE,), in_specs=in_specs, out_specs=out_specs, out_shape=out_shape,
        scratch_shapes=scratch,
        compiler_params=pltpu.CompilerParams(dimension_semantics=("arbitrary",), vmem_limit_bytes=VMEM_LIMIT),
    )(x, c_vec, w_ada, b_ada, w_in, w_out, *small, ln_g, ln_b)


IN_STEPS = W_IN_COLS // HEAD
OUT_STEPS = 4
OUT_COLS = D_MODEL // OUT_STEPS
OUT_FIRST = 2
ITEMS = ([("out", k) for k in range(OUT_FIRST)] + [("in", k) for k in range(IN_STEPS)]
         + [("out", k) for k in range(OUT_FIRST, OUT_STEPS)])
N_ITEMS = len(ITEMS)
N_STEPS = DEPTH * N_ITEMS
DELAY_SUM, DELAY_SECOND, DELAY_FINAL = 1, 3, 5
SMALL_SCATTER_STEP, SMALL_GATHER_STEP, SMALL_PASS_STEP, SMALL_FINISH_STEP = 1, 3, 5, 7


def _wgrad_reduce(h, dproj, cat, dy, pack, dmod):
    def body(*refs):
        h_ref, dp_refs, cat_ref, dy_ref, pack_ref, dmod_ref = refs[0], refs[1:5], refs[5], refs[6], refs[7], refs[8]
        fin_in, fin_out, pack_out, dmod_out = refs[9:13]
        scratch = refs[13:]
        (mine_in, send_in, sib_in, st_in, r1_in, r2_in, f_in,
         mine_out, send_out, sib_out, st_out, r1_out, r2_out, f_out,
         d2d_s, d2d_r, r1_s, r1_r, r2_s, r2_r, fin_l, fin_s, fin_r) = scratch[:23]
        p = _Place()
        c = p.c
        i = pl.program_id(0)
        my_rows = pl.ds(pl.multiple_of(c * HALF_IN, HALF_IN), HALF_IN)

        def layer_of(j):
            return DEPTH - 1 - j // N_ITEMS

        def bufs(j):
            kind, k = ITEMS[j % N_ITEMS]
            if kind == "in":
                return [r.at[k] for r in (mine_in, send_in, sib_in, st_in, r1_in, r2_in, f_in)]
            return [r.at[k] for r in (mine_out, send_out, sib_out, st_out, r1_out, r2_out, f_out)]

        def piece(j, ref, ch):
            if ITEMS[j % N_ITEMS][0] == "in":
                return ref.at[:, ch * HEAD:(ch + 1) * HEAD]
            return ref.at[ch]

        def slot(ch):
            return jnp.where(c == 0, ch % 2, ch // 2)

        def to_sibling(j):
            _, send, sib, _, _, _, _ = bufs(j)
            return pltpu.make_async_remote_copy(send, sib, d2d_s.at[j], d2d_r.at[j], device_id=p.sibling, device_id_type=MESH)

        def first_round(j, ch):
            _, _, _, st, r1, _, _ = bufs(j)
            k = slot(ch)
            return pltpu.make_async_remote_copy(st.at[k], r1.at[k], r1_s.at[2 * j + k], r1_r.at[2 * j + k],
                                                device_id=p.first, device_id_type=MESH)

        def second_round(j):
            _, _, _, st, _, r2, _ = bufs(j)
            return pltpu.make_async_remote_copy(st.at[2], r2, r2_s.at[j], r2_r.at[j], device_id=p.second, device_id_type=MESH)

        def finals(j):
            f = bufs(j)[6]
            kind, k = ITEMS[j % N_ITEMS]
            if kind == "in":
                dst = fin_in.at[layer_of(j), my_rows, k * HEAD:(k + 1) * HEAD]
            else:
                dst = fin_out.at[layer_of(j), c, :, k * OUT_COLS:(k + 1) * OUT_COLS]
            return [pltpu.make_async_copy(f, dst, fin_l.at[j]),
                    pltpu.make_async_remote_copy(f, dst, fin_s.at[j], fin_r.at[j], device_id=p.sibling, device_id_type=MESH)]

        def stage_sum(j):
            mine, _, sib, st, _, _, _ = bufs(j)
            to_sibling(j).wait_recv()
            mine[...] = mine[...] + sib[...]
            for ch in range(N_CHIP):
                @pl.when(p.first_coord(ch) != p.my_first_coord)
                def _(ch=ch):
                    st[slot(ch)] = piece(j, mine, ch)[...].astype(BF16)
                    first_round(j, ch).start()

        def stage_second(j):
            mine, _, _, st, r1, _, _ = bufs(j)
            for ch in range(N_CHIP):
                @pl.when(p.first_coord(ch) == p.my_first_coord)
                def _(ch=ch):
                    first_round(j, ch).wait_recv()
                    part = piece(j, mine, ch)
                    total = part[...] + r1[slot(ch)].astype(F32)
                    part[...] = total

                    @pl.when(ch != p.chip)
                    def _():
                        st[2] = total.astype(BF16)
                        second_round(j).start()

        def stage_final(j):
            mine, _, _, _, _, r2, f = bufs(j)
            second_round(j).wait_recv()
            for ch in range(N_CHIP):
                @pl.when(ch == p.chip)
                def _(ch=ch):
                    f[...] = piece(j, mine, ch)[...] + r2[...].astype(F32)
            for cp in finals(j):
                cp.start()

        def drain(j):
            to_sibling(j).wait_send()
            for ch in range(N_CHIP):
                @pl.when(p.first_coord(ch) != p.my_first_coord)
                def _(ch=ch):
                    first_round(j, ch).wait_send()

                @pl.when(jnp.logical_and(p.first_coord(ch) == p.my_first_coord, ch != p.chip))
                def _():
                    second_round(j).wait_send()
            for cp in finals(j):
                cp.wait()

        dev = p.dev
        devices = p.others()

        def half(core):
            return pl.ds(pl.multiple_of(core * PK_HALF, 8), PK_HALF)

        def finished(core, ch):
            return pl.ds(pl.multiple_of(core * PK_HALF + ch * PK_PIECE, 8), PK_PIECE)

        def small_exchange(l, first_step, bufs_l):
            (pk_mine, pk_sib, pk_rs, pk_fin, pk_all, dm_st, dm_all, pk_sem, rs_s, rs_r, ag_s, ag_r, dm_s, dm_r) = bufs_l

            def pk_load():
                return pltpu.make_async_copy(pack_ref.at[l, half(c)], pk_mine, pk_sem.at[0])

            def pk_give():
                return pltpu.make_async_remote_copy(pack_ref.at[l, half(1 - c)], pk_sib, pk_sem.at[1], pk_sem.at[2],
                                                    device_id=p.sibling, device_id_type=MESH)

            def pk_scatter(ch):
                return pltpu.make_async_remote_copy(pk_mine.at[ch * PK_PIECE:(ch + 1) * PK_PIECE], pk_rs.at[p.chip],
                                                    rs_s.at[ch], rs_r.at[p.chip], device_id=(ch // 2, ch % 2, c),
                                                    device_id_type=MESH)

            def pk_spread(ch):
                return pltpu.make_async_remote_copy(pk_fin, pk_all.at[finished(c, p.chip)], ag_s.at[ch], ag_r.at[p.chip],
                                                    device_id=(ch // 2, ch % 2, c), device_id_type=MESH)

            def pk_pass():
                return pltpu.make_async_remote_copy(pk_all.at[half(c)], pk_all.at[half(c)], pk_sem.at[3], pk_sem.at[4],
                                                    device_id=p.sibling, device_id_type=MESH)

            def dm_copy(r):
                return pltpu.make_async_remote_copy(dm_st, dm_all.at[:, pl.ds(dev, 1), :], dm_s.at[r], dm_r.at[r],
                                                    device_id=devices[r], device_id_type=MESH)

            def results():
                return [pltpu.make_async_copy(pk_all, pack_out.at[l], pk_sem.at[0]),
                        pltpu.make_async_copy(dm_all, dmod_out.at[l], pk_sem.at[5])]

            @pl.when(i == first_step)
            def _():
                pk_load().start()
                pk_give().start()
                for k in range(3):
                    for r in range(D_MODEL // HEAD):
                        dm_st[8 * k + r] = dmod_ref[l, k:k + 1, r * HEAD:(r + 1) * HEAD]
                dm_all[:, pl.ds(dev, 1), :] = dm_st[...]
                for r in range(N_DEV - 1):
                    dm_copy(r).start()

            @pl.when(i == first_step + SMALL_SCATTER_STEP)
            def _():
                pk_load().wait()
                pk_give().wait()
                pk_mine[...] = pk_mine[...] + pk_sib[...]
                for ch in range(N_CHIP):
                    @pl.when(ch != p.chip)
                    def _(ch=ch):
                        pk_scatter(ch).start()

            @pl.when(i == first_step + SMALL_GATHER_STEP)
            def _():
                for ch in range(N_CHIP):
                    @pl.when(ch != p.chip)
                    def _(ch=ch):
                        pltpu.make_async_remote_copy(pk_fin, pk_rs.at[ch], rs_s.at[ch], rs_r.at[ch],
                                                     device_id=p.sibling, device_id_type=MESH).wait_recv()
                for me in range(N_CHIP):
                    @pl.when(me == p.chip)
                    def _(me=me):
                        total = None
                        for ch in range(N_CHIP):
                            part = pk_mine[me * PK_PIECE:(me + 1) * PK_PIECE] if ch == me else pk_rs[ch]
                            total = part if total is None else total + part
                        pk_fin[...] = total
                        pk_all[finished(c, me)] = total
                for ch in range(N_CHIP):
                    @pl.when(ch != p.chip)
                    def _(ch=ch):
                        pk_spread(ch).start()

            @pl.when(i == first_step + SMALL_PASS_STEP)
            def _():
                for ch in range(N_CHIP):
                    @pl.when(ch != p.chip)
                    def _(ch=ch):
                        pltpu.make_async_remote_copy(pk_fin, pk_all.at[finished(c, ch)], ag_s.at[ch], ag_r.at[ch],
                                                     device_id=p.sibling, device_id_type=MESH).wait_recv()
                pk_pass().start()

            @pl.when(i == first_step + SMALL_FINISH_STEP)
            def _():
                pk_pass().wait()
                for ch in range(N_CHIP):
                    @pl.when(ch != p.chip)
                    def _(ch=ch):
                        pk_scatter(ch).wait_send()
                        pk_spread(ch).wait_send()
                for r in range(N_DEV - 1):
                    dm_copy(r).wait()
                for cp in results():
                    cp.start()
                for cp in results():
                    cp.wait()

        n_small = 14
        for l in range(DEPTH):
            small_exchange(l, (DEPTH - 1 - l) * N_ITEMS, scratch[23 + n_small * l:23 + n_small * (l + 1)])

        for step in range(N_ITEMS, N_STEPS):
            @pl.when(i == step)
            def _(step=step):
                drain(step - N_ITEMS)

        ii = jnp.where(i < N_ITEMS, i, i - N_ITEMS)
        in_step = jnp.logical_and(ii >= OUT_FIRST, ii < OUT_FIRST + IN_STEPS)

        @pl.when(in_step)
        def _():
            k = ii - OUT_FIRST
            rhs = jnp.concatenate([r[...] for r in dp_refs], axis=1)
            res = _dot(h_ref[...], rhs, TN)

            @pl.when(c == 0)
            def _():
                mine_in[k] = res[:HALF_IN]
                send_in[k] = res[HALF_IN:]

            @pl.when(c == 1)
            def _():
                mine_in[k] = res[HALF_IN:]
                send_in[k] = res[:HALF_IN]

        @pl.when(jnp.logical_not(in_step))
        def _():
            k = jnp.where(ii < OUT_FIRST, ii, ii - IN_STEPS)
            res = _dot(cat_ref[...], dy_ref[...], TN)

            @pl.when(c == 0)
            def _():
                for ch in range(N_CHIP):
                    mine_out[k, ch] = res[ch * W_OUT_ROWS:ch * W_OUT_ROWS + HALF_OUT]
                    send_out[k, ch] = res[ch * W_OUT_ROWS + HALF_OUT:(ch + 1) * W_OUT_ROWS]

            @pl.when(c == 1)
            def _():
                for ch in range(N_CHIP):
                    mine_out[k, ch] = res[ch * W_OUT_ROWS + HALF_OUT:(ch + 1) * W_OUT_ROWS]
                    send_out[k, ch] = res[ch * W_OUT_ROWS:ch * W_OUT_ROWS + HALF_OUT]

        stages = ((0, lambda j: to_sibling(j).start()), (DELAY_SUM, stage_sum), (DELAY_SECOND, stage_second),
                  (DELAY_FINAL, stage_final))
        for step in range(N_STEPS):
            @pl.when(i == step)
            def _(step=step):
                for delay, stage in stages:
                    if step - delay >= 0:
                        stage(step - delay)

        @pl.when(i == N_STEPS - 1)
        def _():
            for step in range(N_STEPS, N_STEPS + DELAY_FINAL):
                for delay, stage in stages:
                    if 0 <= step - delay < N_STEPS:
                        stage(step - delay)
            for j in range(N_STEPS - N_ITEMS, N_STEPS):
                drain(j)

    hbm = pl.BlockSpec(memory_space=pl.ANY)

    def layer(i):
        return jnp.where(i < N_ITEMS, DEPTH - 1, 0)

    def item(i):
        return jnp.where(i < N_ITEMS, i, i - N_ITEMS)

    def whole(i):
        return (layer(i), 0, 0)

    def dproj_piece(ch):
        return pl.BlockSpec((None, SEQ, HEAD),
                            lambda i: (layer(i), 0, ch * IN_STEPS + jnp.clip(item(i) - OUT_FIRST, 0, IN_STEPS - 1)))

    def dy_quarter(i):
        return (layer(i), 0, jnp.where(item(i) < OUT_FIRST, item(i), jnp.maximum(item(i) - IN_STEPS, OUT_FIRST)))

    operand = pl.BlockSpec((None, SEQ, D_MODEL), whole, pipeline_mode=pl.Buffered(1))
    in_specs = [operand] + [dproj_piece(ch) for ch in range(N_CHIP)]
    in_specs += [operand, pl.BlockSpec((None, SEQ, OUT_COLS), dy_quarter), hbm, _const_in((DEPTH, 8, D_MODEL))]
    args = [h, dproj, dproj, dproj, dproj, cat, dy, pack, dmod]
    out_shape = [jax.ShapeDtypeStruct((DEPTH, D_MODEL, W_IN_COLS), F32), jax.ShapeDtypeStruct((DEPTH, 2, HALF_OUT, D_MODEL), F32),
                 jax.ShapeDtypeStruct((DEPTH, PK_ROWS, HEAD), F32), jax.ShapeDtypeStruct((DEPTH, 24, N_DEV, HEAD), F32)]
    out_specs = [hbm, hbm, hbm, hbm]
    in_item = lambda *lead: pltpu.VMEM(lead + (HALF_IN, HEAD), BF16)
    out_item = lambda *lead: pltpu.VMEM(lead + (HALF_OUT, OUT_COLS), BF16)
    scratch = [
        pltpu.VMEM((IN_STEPS, HALF_IN, N_CHIP * HEAD), F32), pltpu.VMEM((IN_STEPS, HALF_IN, N_CHIP * HEAD), F32),
        pltpu.VMEM((IN_STEPS, HALF_IN, N_CHIP * HEAD), F32), in_item(IN_STEPS, 3), in_item(IN_STEPS, 2), in_item(IN_STEPS),
        pltpu.VMEM((IN_STEPS, HALF_IN, HEAD), F32),
        pltpu.VMEM((OUT_STEPS, N_CHIP, HALF_OUT, OUT_COLS), F32), pltpu.VMEM((OUT_STEPS, N_CHIP, HALF_OUT, OUT_COLS), F32),
        pltpu.VMEM((OUT_STEPS, N_CHIP, HALF_OUT, OUT_COLS), F32), out_item(OUT_STEPS, 3), out_item(OUT_STEPS, 2),
        out_item(OUT_STEPS), pltpu.VMEM((OUT_STEPS, HALF_OUT, OUT_COLS), F32),
        pltpu.SemaphoreType.DMA((N_STEPS,)), pltpu.SemaphoreType.DMA((N_STEPS,)),
        pltpu.SemaphoreType.DMA((2 * N_STEPS,)), pltpu.SemaphoreType.DMA((2 * N_STEPS,)),
        pltpu.SemaphoreType.DMA((N_STEPS,)), pltpu.SemaphoreType.DMA((N_STEPS,)),
        pltpu.SemaphoreType.DMA((N_STEPS,)), pltpu.SemaphoreType.DMA((N_STEPS,)), pltpu.SemaphoreType.DMA((N_STEPS,)),
    ]
    for _ in range(DEPTH):
        scratch += [
            pltpu.VMEM((PK_HALF, HEAD), F32), pltpu.VMEM((PK_HALF, HEAD), F32), pltpu.VMEM((N_CHIP, PK_PIECE, HEAD), F32),
            pltpu.VMEM((PK_PIECE, HEAD), F32), pltpu.VMEM((PK_ROWS, HEAD), F32),
            pltpu.VMEM((24, 1, HEAD), F32), pltpu.VMEM((24, N_DEV, HEAD), F32),
            pltpu.SemaphoreType.DMA((6,)),
            pltpu.SemaphoreType.DMA((N_CHIP,)), pltpu.SemaphoreType.DMA((N_CHIP,)),
            pltpu.SemaphoreType.DMA((N_CHIP,)), pltpu.SemaphoreType.DMA((N_CHIP,)),
            pltpu.SemaphoreType.DMA((N_DEV - 1,)), pltpu.SemaphoreType.DMA((N_DEV - 1,)),
        ]
    return pl.pallas_call(
        body, name="wgrad", grid=(N_STEPS,), in_specs=in_specs, out_specs=out_specs, out_shape=out_shape,
        scratch_shapes=scratch,
        compiler_params=pltpu.CompilerParams(dimension_semantics=("arbitrary",), vmem_limit_bytes=VMEM_LIMIT),
    )(*args)


def _adamw(w, g, m, v):
    m = ADAM_B1 * m + (1.0 - ADAM_B1) * g
    v = ADAM_B2 * v + (1.0 - ADAM_B2) * (g * g)
    m_hat = m / (1.0 - ADAM_B1 ** ADAM_STEP)
    v_hat = v / (1.0 - ADAM_B2 ** ADAM_STEP)
    delta = -ADAM_LR * (m_hat / (jnp.sqrt(v_hat) + ADAM_EPS) + ADAM_WD * w)
    return delta, m, v


def _adam_sharded(c_all, dmods, ada, w_in_set, w_out_set):
    rows = D_MODEL // ADAM_PARTS

    def body(c_ref, dm_ref, wa_ref, ma_ref, va_ref, wi_ref, gi_ref, mi_ref, vi_ref, wo_ref, go_ref, mo_ref, vo_ref,
             ga_out, da_out, ma_out, va_out, di_out, mi_out, vi_out, do_out, mo_out, vo_out):
        l = pl.program_id(0)
        chip = 2 * lax.axis_index("x") + lax.axis_index("y")
        cv = c_ref[...]
        silu_c = (cv * _sigmoid(cv)).astype(BF16).astype(F32)
        pieces = []
        for k in range(W_ADA_COLS // HEAD):
            dk = dm_ref[l, (W_ADA_COLS // HEAD) * chip + k].astype(BF16).astype(F32)
            pieces.append(_dot_exact(silu_c, dk, TN))
        g = jnp.concatenate(pieces, axis=1)
        ga_out[...] = g
        da_out[...], ma_out[...], va_out[...] = _adamw(wa_ref[...], g, ma_ref[...], va_ref[...])
        di_out[...], mi_out[...], vi_out[...] = _adamw(wi_ref[...], gi_ref[...], mi_ref[...], vi_ref[...])
        do_out[...], mo_out[...], vo_out[...] = _adamw(wo_ref[...], go_ref[...], mo_ref[...], vo_ref[...])

    def blk(r, cols):
        return pl.BlockSpec((None, r, cols), lambda l, i: (l, i, 0))

    b_ada, b_in, b_out = blk(rows, W_ADA_COLS), blk(rows, W_IN_COLS), blk(W_OUT_ROWS // ADAM_PARTS, D_MODEL)
    shapes = [jax.ShapeDtypeStruct(a[0].shape, F32) for a in (ada, w_in_set, w_out_set)]
    return pl.pallas_call(
        body, name="adam_sharded", grid=(DEPTH, ADAM_PARTS),
        in_specs=[pl.BlockSpec((N_DEV, rows), lambda l, i: (0, i)), _const_in((DEPTH, 24, N_DEV, HEAD))]
        + [b_ada] * 3 + [b_in] * 4 + [b_out] * 4,
        out_specs=[b_ada] * 4 + [b_in] * 3 + [b_out] * 3,
        out_shape=[shapes[0]] * 4 + [shapes[1]] * 3 + [shapes[2]] * 3,
        compiler_params=pltpu.CompilerParams(dimension_semantics=("arbitrary", "arbitrary"), vmem_limit_bytes=VMEM_LIMIT),
    )(c_all, dmods, *ada, *w_in_set, *w_out_set)


def _adam_small(packs, dmods, weights, ms, vs):
    n = len(weights)

    def body(*refs):
        pack_refs, dm_refs = refs[0], refs[1]
        b = 2
        w_refs, m_refs, v_refs = refs[b:b + n], refs[b + n:b + 2 * n], refs[b + 2 * n:b + 3 * n]
        outs = refs[b + 3 * n:]
        g_refs, d_refs, nm_refs, nv_refs = outs[0:n], outs[n:2 * n], outs[2 * n:3 * n], outs[3 * n:4 * n]
        outs[4 * n][...] = pack_refs.at[DEPTH - 1][PK_LOSS:PK_LOSS + 1, 0:1] * (0.5 / D_MODEL)

        def lanes(l, row0, count):
            return jnp.concatenate([pack_refs.at[l][row0 + k:row0 + k + 1, :] for k in range(count)], axis=1)

        def update(idx, at, g):
            g_refs[idx][at] = g
            d_refs[idx][at], nm_refs[idx][at], nv_refs[idx][at] = _adamw(w_refs[idx][at], g, m_refs[idx][at], v_refs[idx][at])

        for l in range(DEPTH):
            row = (slice(l, l + 1), slice(None))
            g_b = None
            for d in range(N_DEV):
                part = dm_refs.at[l][:, d, :]
                g_b = part if g_b is None else g_b + part
            update(0, row, jnp.concatenate([g_b[k:k + 1, :] for k in range(24)], axis=1))
            for g in range(N_HEAD):
                update(1, (l, g), pack_refs.at[l][PK_W_POOL + g * HEAD:PK_W_POOL + (g + 1) * HEAD, :])
                update(5, (l, g), pack_refs.at[l][PK_W_SGU + g * HEAD:PK_W_SGU + (g + 1) * HEAD, :])
            update(2, row, lanes(l, PK_POOL_SCALE, N_HEAD))
            update(3, (l,), pack_refs.at[l][PK_SGU_LN_G:PK_SGU_LN_G + N_HEAD, :])
            update(4, (l,), pack_refs.at[l][PK_SGU_LN_B:PK_SGU_LN_B + N_HEAD, :])
            update(6, (l,), pack_refs.at[l][PK_B_SGU:PK_B_SGU + N_HEAD, :])
            update(7, row, lanes(l, PK_LN_G, D_MODEL // HEAD))
            update(8, row, lanes(l, PK_LN_B, D_MODEL // HEAD))

    vmem = pl.BlockSpec(memory_space=pltpu.VMEM)
    shapes = [jax.ShapeDtypeStruct(w.shape, F32) for w in weights]
    return pl.pallas_call(
        body, name="adam_small", in_specs=[vmem] * (2 + 3 * n), out_specs=[vmem] * (4 * n + 1),
        out_shape=shapes * 4 + [jax.ShapeDtypeStruct((1, 1), F32)],
        compiler_params=pltpu.CompilerParams(vmem_limit_bytes=VMEM_LIMIT),
    )(packs, dmods, *weights, *ms, *vs)


def kernel(x, c, w_ada, b_ada, w_in, w_pool, pool_scale, sgu_ln_g, sgu_ln_b, w_sgu, b_sgu, w_out, ln_g, ln_b, loss_target, m_w_ada, m_b_ada, m_w_in, m_w_pool, m_pool_scale, m_sgu_ln_g, m_sgu_ln_b, m_w_sgu, m_b_sgu, m_w_out, m_ln_g, m_ln_b, v_w_ada, v_b_ada, v_w_in, v_w_pool, v_pool_scale, v_sgu_ln_g, v_sgu_ln_b, v_w_sgu, v_b_sgu, v_w_out, v_ln_g, v_ln_b):
    small = (w_pool, pool_scale, sgu_ln_g, sgu_ln_b, w_sgu, b_sgu)
    *saved0, w_in0, w_out0, w_in1, w_out1, mod, c_all = _forward_first(x, c, w_ada, b_ada, w_in, w_out, small, ln_g, ln_b)
    *saved1, dout, sq = _forward_last(saved0[3], mod, w_in1, w_out1, small, ln_g, ln_b, loss_target)

    dx1, *shared = _backward_layer(1, dout, saved1, mod, w_in1, w_out1, small, ln_g, sq=sq)
    dx0, h, cat, dy, dproj, pack, dmod = _backward_layer(0, dx1, saved0, mod, w_in0, w_out0, small, ln_g, shared=shared)
    g_in, g_out, pack, dmods = _wgrad_reduce(h, dproj, cat, dy, pack, dmod)

    g_out = g_out.reshape(DEPTH, W_OUT_ROWS, D_MODEL)
    big = _adam_sharded(c_all, dmods, (w_ada, m_w_ada, v_w_ada), (w_in, g_in, m_w_in, v_w_in), (w_out, g_out, m_w_out, v_w_out))
    ada, win, wout = big[0:4], (g_in, *big[4:7]), (g_out, *big[7:10])
    small_w = (b_ada, w_pool, pool_scale, sgu_ln_g, sgu_ln_b, w_sgu, b_sgu, ln_g, ln_b)
    small_m = (m_b_ada, m_w_pool, m_pool_scale, m_sgu_ln_g, m_sgu_ln_b, m_w_sgu, m_b_sgu, m_ln_g, m_ln_b)
    small_v = (v_b_ada, v_w_pool, v_pool_scale, v_sgu_ln_g, v_sgu_ln_b, v_w_sgu, v_b_sgu, v_ln_g, v_ln_b)
    res = _adam_small(pack, dmods, small_w, small_m, small_v)
    n = len(small_w)
    loss = res[4 * n].reshape(())

    def ordered(k):
        s = res[k * n:(k + 1) * n]
        return (ada[k], s[0], win[k], s[1], s[2], s[3], s[4], s[5], s[6], wout[k], s[7], s[8])

    return (loss, dx0[None], *ordered(0), *ordered(1), *ordered(2), *ordered(3))
```

```python
import jax
import jax.numpy as jnp
from jax import lax
from jax.experimental import pallas as pl
from jax.experimental.pallas import tpu as pltpu

F32 = jnp.float32
BF16 = jnp.bfloat16
MESH = pl.DeviceIdType.MESH

N_DEV = 8
N_CHIP = 4
DEPTH = 2
SEQ = 2048
D_MODEL = 1024
D_POOL = 512
D_PROJ = 2560
HEAD = 128
N_HEAD = 4
ROWS = 256
N_TILE = SEQ // ROWS
HALO = 16
W_IN_COLS = D_PROJ // N_CHIP
W_OUT_ROWS = D_MODEL // N_CHIP
W_ADA_COLS = 3 * D_MODEL // N_CHIP
HALF_IN = D_MODEL // 2
HALF_OUT = W_OUT_ROWS // 2
DEEPNORM_ALPHA = (2.0 * DEPTH) ** 0.25
LN_EPS = 1e-5
INV_SQRT2 = 0.7071067811865476
INV_SQRT_2PI = 0.3989422804014327

ADAM_LR = 0.001
ADAM_B1 = 0.9
ADAM_B2 = 0.999
ADAM_EPS = 1e-08
ADAM_WD = 0.01
ADAM_STEP = 10
ADAM_PARTS = 2

PK_W_POOL = 0
PK_W_SGU = 512
PK_POOL_SCALE = 1024
PK_SGU_LN_G = 1032
PK_SGU_LN_B = 1040
PK_B_SGU = 1048
PK_LN_G = 1056
PK_LN_B = 1064
PK_ROWS = 1152
PK_HALF = PK_ROWS // 2
PK_PIECE = PK_HALF // N_CHIP
DM_LOSS = 3 * D_MODEL // HEAD
DM_ROWS = DM_LOSS + 1

VMEM_LIMIT = 56 * 1024 * 1024

GATHER_SECOND_ROUND_STEP = 0
GATHER_PASS_STEP = 2

NN = (((1,), (0,)), ((), ()))
NT = (((1,), (1,)), ((), ()))
TN = (((0,), (0,)), ((), ()))


def _dot(a, b, dims=NN):
    return lax.dot_general(a, b, dims, preferred_element_type=F32)


def _dot_exact(a, b, dims=NN):
    return lax.dot_general(a, b, dims, preferred_element_type=F32, precision=lax.Precision.HIGHEST)


def _layer_norm(v):
    mu = jnp.mean(v, axis=-1, keepdims=True)
    d = v - mu
    var = jnp.mean(d * d, axis=-1, keepdims=True)
    rstd = lax.rsqrt(var + LN_EPS)
    return d * rstd, rstd


def _layer_norm_bwd(dvhat, vhat, rstd):
    m1 = jnp.mean(dvhat, axis=-1, keepdims=True)
    m2 = jnp.mean(dvhat * vhat, axis=-1, keepdims=True)
    return rstd * (dvhat - m1 - vhat * m2)


def _sigmoid(v):
    return 1.0 / (1.0 + jnp.exp(-v))


def _gelu_parts(v):
    phi = 0.5 * (1.0 + lax.erf(v * INV_SQRT2))
    pdf = INV_SQRT_2PI * jnp.exp(-0.5 * v * v)
    return phi, pdf


def _sum_rows(v):
    return jnp.sum(v, axis=0, keepdims=True)


def _window_sums(ext, toward_later):
    n = ext.shape[0]

    def shifted(v, k):
        return pltpu.roll(v, (n - k) if toward_later else k, 0)

    s2 = ext + shifted(ext, 1)
    r4 = s2[:, HEAD:]
    s4 = r4 + shifted(r4, 2)
    r8 = s4[:, HEAD:]
    s8 = r8 + shifted(r8, 4)
    r16 = s8[:, HEAD:]
    s16 = r16 + shifted(r16, 8)
    return jnp.concatenate([s2[:, :HEAD], s4[:, :HEAD], s8[:, :HEAD], s16], axis=1)


def _window_counts(row0):
    t1 = row0 + 1 + lax.broadcasted_iota(jnp.int32, (ROWS, D_POOL), 0)
    lane = lax.broadcasted_iota(jnp.int32, (ROWS, D_POOL), 1)
    width = jnp.where(lane < HEAD, 2, jnp.where(lane < 2 * HEAD, 4, jnp.where(lane < 3 * HEAD, 8, 16)))
    return jnp.minimum(t1, width).astype(F32)


def _causal_mask():
    r = lax.broadcasted_iota(jnp.int32, (HEAD, HEAD), 0)
    s = lax.broadcasted_iota(jnp.int32, (HEAD, HEAD), 1)
    return r >= s


def _chunks_to_lanes(v):
    return jnp.concatenate([v[n * HEAD:(n + 1) * HEAD] for n in range(ROWS // HEAD)], axis=1)


def _lanes_to_chunks(v):
    return jnp.concatenate([v[:, n * HEAD:(n + 1) * HEAD] for n in range(ROWS // HEAD)], axis=0)


def _pack_stats(rstd_x, rstd_z, rstd_v):
    lane = lax.broadcasted_iota(jnp.int32, (ROWS, HEAD), 1)
    packed = rstd_x
    for k, r in enumerate([rstd_z] + list(rstd_v)):
        packed = jnp.where(lane < 16 * (k + 1), packed, r)
    return packed


def _unpack_stats(stats):
    cols = [stats[:, 16 * k:16 * k + 1] for k in range(2 + N_HEAD)]
    return cols[0], cols[1], cols[2:]


def _mixer(proj, halo, row0, wpool_ref, pscale, sgu_g_ref, sgu_b_ref, wsgu_ref, bsgu_ref, saved=None):
    xa = proj[:, 0:512]
    ga = proj[:, 512:1024]
    u = proj[:, 1024:1536]
    v = proj[:, 1536:2048]
    gb = proj[:, 2048:2560]
    ext = jnp.concatenate([halo, xa], axis=0)
    win = _window_sums(ext, toward_later=False)[HALO:]
    cnt = _window_counts(row0)
    pooled = (win / cnt - xa).astype(BF16)
    pw = jnp.concatenate(
        [_dot(pooled[:, g * HEAD:(g + 1) * HEAD], wpool_ref[g].astype(BF16)) for g in range(N_HEAD)], axis=1)
    sig_a = _sigmoid(ga) if saved is None else saved["sig_a"]
    ya = pw * pscale * (ga * sig_a)
    phi_u, pdf_u = _gelu_parts(u)
    phi_v, pdf_v = _gelu_parts(v)
    gu = u * phi_u
    gv = v * phi_v
    sig_b = _sigmoid(gb) if saved is None else saved["sig_b"]
    silu_b = gb * sig_b
    mask = _causal_mask()
    diag = lax.broadcasted_iota(jnp.int32, (HEAD, HEAD), 0) == lax.broadcasted_iota(jnp.int32, (HEAD, HEAD), 1)
    vhat, rstd_v, vln_l, mixed = [], [], [], []
    for h in range(N_HEAD):
        if saved is None:
            vh, rh = _layer_norm(gv[:, h * HEAD:(h + 1) * HEAD])
        else:
            vh, rh = saved["vhat"][h], saved["rstd_v"][h]
        ln = (vh * sgu_g_ref[h:h + 1, :] + sgu_b_ref[h:h + 1, :]).astype(BF16)
        ln_l = _chunks_to_lanes(ln)
        wm = jnp.where(mask, wsgu_ref[h], 0.0).astype(BF16)
        bias = jnp.sum(jnp.where(diag, jnp.broadcast_to(bsgu_ref[h:h + 1, :], (HEAD, HEAD)), 0.0), axis=1, keepdims=True)
        mx = _lanes_to_chunks(_dot(wm, ln_l) + bias)
        vhat.append(vh)
        rstd_v.append(rh)
        vln_l.append(ln_l)
        mixed.append(mx)
    mixed = jnp.concatenate(mixed, axis=1)
    yb = gu * mixed * silu_b
    return dict(xa=xa, ga=ga, u=u, gb=gb, cnt=cnt, pooled=pooled, pw=pw, sig_a=sig_a, ya=ya, phi_u=phi_u, pdf_u=pdf_u,
                phi_v=phi_v, pdf_v=pdf_v, gu=gu, sig_b=sig_b, silu_b=silu_b, vhat=vhat, rstd_v=rstd_v, vln_l=vln_l,
                mixed=mixed, yb=yb, mask=mask)


def _const(shape, *index):
    lead = tuple(index) + (0,) * (len(shape) - len(index))
    return pl.BlockSpec(shape, lambda *_: lead)


def _const_in(shape, *index):
    lead = tuple(index) + (0,) * (len(shape) - len(index))
    return pl.BlockSpec(shape, lambda *_: lead, pipeline_mode=pl.Buffered(1))


def _layer_weight_specs(l):
    return [
        _const_in((None, N_HEAD, HEAD, HEAD), l),
        _const_in((DEPTH, D_POOL)),
        _const_in((None, N_HEAD, HEAD), l),
        _const_in((None, N_HEAD, HEAD), l),
        _const_in((None, N_HEAD, HEAD, HEAD), l),
        _const_in((None, N_HEAD, HEAD), l),
    ]


def _forward_tile(l, i, x_ref, mod_ref, win_ref, wout_ref, small_refs, lng_ref, lnb_ref, carry_ref, saved_refs):
    wpool_ref, pscale_ref, sgu_g_ref, sgu_b_ref, wsgu_ref, bsgu_ref = small_refs
    proj_ref, y_ref, xn_ref, zn_ref, stats_ref, sig_ref, vhat_ref = saved_refs
    x = x_ref[...]
    if l > 0:
        x = x * lng_ref[l - 1:l, :] + lnb_ref[l - 1:l, :]
    shift, scale, gate = mod_ref[0:1, :], mod_ref[1:2, :], mod_ref[2:3, :]
    xn, rstd_x = _layer_norm(x)
    xn_ref[...] = xn.astype(xn_ref.dtype)
    h = xn * (1.0 + scale) + shift
    proj = _dot(h.astype(BF16), win_ref[...])
    proj_ref[...] = proj
    m = _mixer(proj, carry_ref[...], i * ROWS, wpool_ref, pscale_ref[l:l + 1, :], sgu_g_ref, sgu_b_ref, wsgu_ref, bsgu_ref)
    carry_ref[...] = m["xa"][ROWS - HALO:]
    sig_ref[...] = jnp.concatenate([m["sig_a"], m["sig_b"]], axis=1).astype(sig_ref.dtype)
    vhat_ref[...] = jnp.concatenate(m["vhat"], axis=1).astype(vhat_ref.dtype)
    cat = jnp.concatenate([m["ya"], m["yb"]], axis=1).astype(BF16)
    y = _dot(cat, wout_ref[...])
    y_ref[...] = y.astype(y_ref.dtype)
    zn, rstd_z = _layer_norm(DEEPNORM_ALPHA * x + gate * y)
    zn_ref[...] = zn
    stats_ref[...] = _pack_stats(rstd_x, rstd_z, m["rstd_v"])
    return zn


SAVED_COLS = (D_PROJ, D_MODEL, D_MODEL, D_MODEL, HEAD, D_MODEL, D_POOL)
SAVED_TYPES = (F32, BF16, BF16, F32, F32, BF16, BF16)


def _saved_outputs():
    return ([jax.ShapeDtypeStruct((SEQ, cols), t) for cols, t in zip(SAVED_COLS, SAVED_TYPES)],
            [pl.BlockSpec((ROWS, cols), lambda i: (i, 0)) for cols in SAVED_COLS])


def _forward_last(zn_prev, mod, w_in, w_out, small, ln_g, ln_b, target):
    l = DEPTH - 1
    n_saved = len(SAVED_COLS)

    def body(*refs):
        x_ref, mod_ref, win_ref, wout_ref = refs[:4]
        small_refs, lng_ref, lnb_ref, tgt_ref = refs[4:10], refs[10], refs[11], refs[12]
        saved_refs = refs[13:13 + n_saved]
        dout_ref, loss_ref, carry_ref = refs[13 + n_saved:]
        i = pl.program_id(0)

        @pl.when(i == 0)
        def _():
            carry_ref[...] = jnp.zeros_like(carry_ref)
            loss_ref[...] = jnp.zeros_like(loss_ref)

        zn = _forward_tile(l, i, x_ref, mod_ref, win_ref, wout_ref, small_refs, lng_ref, lnb_ref, carry_ref, saved_refs)
        err = zn * lng_ref[l:l + 1, :] + lnb_ref[l:l + 1, :] - tgt_ref[...]
        dout_ref[...] = err * (1.0 / D_MODEL)
        loss_ref[...] += jnp.sum(err * err)

    tile = pl.BlockSpec((ROWS, D_MODEL), lambda i: (i, 0))
    tile3 = pl.BlockSpec((None, ROWS, D_MODEL), lambda i: (0, i, 0))
    in_specs = [tile, _const_in((None, 8, D_MODEL), l), _const_in((D_MODEL, D_PROJ)), _const_in((D_MODEL, D_MODEL))]
    in_specs += _layer_weight_specs(l) + [_const_in((DEPTH, D_MODEL)), _const_in((DEPTH, D_MODEL)), tile3]
    out_shape, out_specs = _saved_outputs()
    out_shape += [jax.ShapeDtypeStruct((SEQ, D_MODEL), F32), jax.ShapeDtypeStruct((8, HEAD), F32)]
    out_specs += [tile, _const((8, HEAD))]
    return pl.pallas_call(
        body, name="fwd_last", grid=(N_TILE,), in_specs=in_specs, out_specs=out_specs, out_shape=out_shape,
        scratch_shapes=[pltpu.VMEM((HALO, D_POOL), F32)],
        compiler_params=pltpu.CompilerParams(dimension_semantics=("arbitrary",), vmem_limit_bytes=VMEM_LIMIT),
    )(zn_prev, mod, w_in, w_out, *small, ln_g, ln_b, target)


def _backward_layer(l, dout, saved, mod, w_in, w_out, small, ln_g, sq=None, shared=None):
    has_loss = sq is not None

    def body(*refs):
        (dout_ref, proj_ref, y_ref, xn_ref, zn_ref, stats_ref, sig_ref, vhat_ref, halo_ref, mod_ref, win_ref, wout_ref,
         wpool_ref, pscale_ref, sgu_g_ref, sgu_b_ref, wsgu_ref, bsgu_ref, lng_ref) = refs[:19]
        n_in = 20 if has_loss else 19 + 6
        dx_ref, h_ref, cat_ref, dy_ref, dproj_ref, pack_ref, dmod_ref, carry_ref = refs[n_in:n_in + 8]
        i = pl.program_id(0)
        tile = N_TILE - 1 - i

        @pl.when(i == 0)
        def _():
            carry_ref[...] = jnp.zeros_like(carry_ref)
            pack_ref[...] = jnp.zeros_like(pack_ref)
            dmod_ref[...] = jnp.zeros_like(dmod_ref)
            if has_loss:
                dmod_ref[3:4, 0:HEAD] = refs[19][0:1, :]

        xn = xn_ref[...].astype(F32)
        zn = zn_ref[...]
        y = y_ref[...].astype(F32)
        dout = dout_ref[...]
        rstd_x, rstd_z, rstd_v = _unpack_stats(stats_ref[...])
        kept = dict(sig_a=sig_ref[:, :D_POOL].astype(F32), sig_b=sig_ref[:, D_POOL:].astype(F32), rstd_v=rstd_v,
                    vhat=[vhat_ref[:, hd * HEAD:(hd + 1) * HEAD].astype(F32) for hd in range(N_HEAD)])
        pscale = pscale_ref[l:l + 1, :]
        shift, scale, gate = mod_ref[0:1, :], mod_ref[1:2, :], mod_ref[2:3, :]
        h = xn * (1.0 + scale) + shift
        h_ref[...] = h.astype(BF16)
        g_ln_g = _sum_rows(dout * zn)
        g_ln_b = _sum_rows(dout)
        dz = _layer_norm_bwd(dout * lng_ref[l:l + 1, :], zn, rstd_z)
        d_gate = _sum_rows(dz * y)
        dy = (gate * dz).astype(BF16)
        dy_ref[...] = dy

        halo = jnp.where(tile > 0, halo_ref[...], 0.0)
        m = _mixer(proj_ref[...], halo, tile * ROWS, wpool_ref, pscale, sgu_g_ref, sgu_b_ref, wsgu_ref, bsgu_ref,
                   saved=kept)
        cat_ref[...] = jnp.concatenate([m["ya"], m["yb"]], axis=1).astype(BF16)
        dcat = _dot(dy, wout_ref[...], NT)
        dya = dcat[:, :D_POOL]
        dyb = dcat[:, D_POOL:]

        ga, sig_a = m["ga"], m["sig_a"]
        dp = dya * (ga * sig_a)
        d_ga = dya * (m["pw"] * pscale) * (sig_a * (1.0 + ga * (1.0 - sig_a)))
        g_pscale = _sum_rows(dp * m["pw"])
        dpw = (dp * pscale).astype(BF16)
        dpooled = []
        for g in range(N_HEAD):
            cols = slice(g * HEAD, (g + 1) * HEAD)
            pack_ref[PK_W_POOL + g * HEAD:PK_W_POOL + (g + 1) * HEAD, :] += _dot(m["pooled"][:, cols], dpw[:, cols], TN)
            dpooled.append(_dot(dpw[:, cols], wpool_ref[g].astype(BF16), NT))
        dpooled = jnp.concatenate(dpooled, axis=1)
        q = dpooled / m["cnt"]
        ext = jnp.concatenate([q, carry_ref[...]], axis=0)
        d_xa = _window_sums(ext, toward_later=True)[:ROWS] - dpooled
        carry_ref[...] = q[:HALO]

        gu, mixed, silu_b, gb, sig_b = m["gu"], m["mixed"], m["silu_b"], m["gb"], m["sig_b"]
        d_mixed = dyb * gu * silu_b
        d_gu = dyb * mixed * silu_b
        d_gb = dyb * gu * mixed * (sig_b * (1.0 + gb * (1.0 - sig_b)))
        d_u = d_gu * (m["phi_u"] + m["u"] * m["pdf_u"])
        ones = jnp.ones((8, HEAD), F32)
        d_v = []
        for hd in range(N_HEAD):
            cols = slice(hd * HEAD, (hd + 1) * HEAD)
            dm = d_mixed[:, cols]
            dm_l = _chunks_to_lanes(dm.astype(BF16))
            g_w = _dot(dm_l, m["vln_l"][hd], NT)
            pack_ref[PK_W_SGU + hd * HEAD:PK_W_SGU + (hd + 1) * HEAD, :] += jnp.where(m["mask"], g_w, 0.0)
            dm_sum = dm[0:HEAD]
            for n in range(1, ROWS // HEAD):
                dm_sum = dm_sum + dm[n * HEAD:(n + 1) * HEAD]
            pack_ref[PK_B_SGU + hd:PK_B_SGU + hd + 1, :] += _dot_exact(ones, dm_sum, NT)[0:1]
            wm = jnp.where(m["mask"], wsgu_ref[hd], 0.0).astype(BF16)
            d_vln = _lanes_to_chunks(_dot(wm, dm_l, TN))
            vhat = m["vhat"][hd]
            pack_ref[PK_SGU_LN_G + hd:PK_SGU_LN_G + hd + 1, :] += _sum_rows(d_vln * vhat)
            pack_ref[PK_SGU_LN_B + hd:PK_SGU_LN_B + hd + 1, :] += _sum_rows(d_vln)
            d_v.append(_layer_norm_bwd(d_vln * sgu_g_ref[hd:hd + 1, :], vhat, m["rstd_v"][hd]))
        v = proj_ref[:, 1536:2048]
        d_v = jnp.concatenate(d_v, axis=1) * (m["phi_v"] + v * m["pdf_v"])

        dproj = jnp.concatenate([d_xa, d_ga, d_u, d_v, d_gb], axis=1).astype(BF16)
        dproj_ref[...] = dproj
        dh = _dot(dproj, win_ref[...], NT)
        d_scale = _sum_rows(dh * xn)
        d_shift = _sum_rows(dh)
        dx_ref[...] = DEEPNORM_ALPHA * dz + _layer_norm_bwd(dh * (1.0 + scale), xn, rstd_x)

        dmod_ref[0:1, :] += d_shift
        dmod_ref[1:2, :] += d_scale
        dmod_ref[2:3, :] += d_gate
        for g in range(N_HEAD):
            pack_ref[PK_POOL_SCALE + g:PK_POOL_SCALE + g + 1, :] += g_pscale[:, g * HEAD:(g + 1) * HEAD]
        for k in range(D_MODEL // HEAD):
            pack_ref[PK_LN_G + k:PK_LN_G + k + 1, :] += g_ln_g[:, k * HEAD:(k + 1) * HEAD]
            pack_ref[PK_LN_B + k:PK_LN_B + k + 1, :] += g_ln_b[:, k * HEAD:(k + 1) * HEAD]

    def rev(i):
        return (N_TILE - 1 - i, 0)

    tile = pl.BlockSpec((ROWS, D_MODEL), rev)
    halo = pl.BlockSpec((HALO, D_POOL), lambda i: (jnp.maximum((N_TILE - 1 - i) * (ROWS // HALO) - 1, 0), 0))
    in_specs = [tile] + [pl.BlockSpec((ROWS, a.shape[1]), rev) for a in saved] + [halo]
    in_specs += [_const_in((None, 8, D_MODEL), l), _const_in((D_MODEL, D_PROJ)), _const_in((D_MODEL, D_MODEL))]
    in_specs += _layer_weight_specs(l) + [_const_in((DEPTH, D_MODEL))]
    args = [dout, *saved, saved[0], mod, w_in, w_out, *small, ln_g]
    stacked = lambda cols: pl.BlockSpec((None, ROWS, cols), lambda i: (l, N_TILE - 1 - i, 0))
    out_shape = [jax.ShapeDtypeStruct((SEQ, D_MODEL), F32), jax.ShapeDtypeStruct((DEPTH, SEQ, D_MODEL), BF16),
                 jax.ShapeDtypeStruct((DEPTH, SEQ, D_MODEL), BF16), jax.ShapeDtypeStruct((DEPTH, SEQ, D_MODEL), BF16),
                 jax.ShapeDtypeStruct((DEPTH, SEQ, D_PROJ), BF16), jax.ShapeDtypeStruct((DEPTH, PK_ROWS, HEAD), F32),
                 jax.ShapeDtypeStruct((DEPTH, 8, D_MODEL), F32)]
    out_specs = [tile, stacked(D_MODEL), stacked(D_MODEL), stacked(D_MODEL), stacked(D_PROJ),
                 _const((None, PK_ROWS, HEAD), l), _const((None, 8, D_MODEL), l)]
    aliases = {}
    if has_loss:
        in_specs.append(_const_in((8, HEAD)))
        args.append(sq)
    else:
        aliases = {len(args) + k: 1 + k for k in range(len(shared))}
        in_specs += [pl.BlockSpec(memory_space=pl.ANY)] * len(shared)
        args += list(shared)
    return pl.pallas_call(
        body, name="bwd_last" if has_loss else "bwd_first", grid=(N_TILE,), in_specs=in_specs, out_specs=out_specs,
        out_shape=out_shape, scratch_shapes=[pltpu.VMEM((HALO, D_POOL), F32)], input_output_aliases=aliases,
        compiler_params=pltpu.CompilerParams(dimension_semantics=("arbitrary",), vmem_limit_bytes=VMEM_LIMIT),
    )(*args)


def _flip(v, f):
    return v + f - 2 * v * f


class _Place:
    def __init__(self):
        x, y, c = lax.axis_index("x"), lax.axis_index("y"), lax.axis_index("c")
        self.x, self.y, self.c = x, y, c
        self.chip = 2 * x + y
        self.dev = 4 * x + 2 * y + c
        self.sibling = (x, y, 1 - c)
        x1, y1 = _flip(x, 1 - c), _flip(y, c)
        x2, y2 = _flip(x, c), _flip(y, 1 - c)
        self.first = (x1, y1, c)
        self.second = (x2, y2, c)
        self.chip_first = 2 * x1 + y1
        self.chip_second = 2 * x2 + y2
        self.chip_far = 2 * (1 - x) + (1 - y)
        self.my_first_coord = jnp.where(c == 0, x, y)

    def first_coord(self, ch):
        return jnp.where(self.c == 0, ch // 2, ch % 2)

    def others(self):
        return [(_flip(self.x, (r >> 2) & 1), _flip(self.y, (r >> 1) & 1), _flip(self.c, r & 1)) for r in range(1, N_DEV)]

    def other_chips(self):
        return [(1 - self.x, self.y), (self.x, 1 - self.y), (1 - self.x, 1 - self.y)]


class _WeightGather:
    CHUNKS = 4
    N_SEMS = 12 * CHUNKS

    def __init__(self, place, win, wout, send, recv):
        self.p, self.win, self.wout, self.send, self.recv = place, win, wout, send, recv
        p = place
        self.plan = [(p.chip, p.first), (p.chip, p.second), (p.chip_first, p.second),
                     (p.chip_first, p.sibling), (p.chip_second, p.sibling), (p.chip_far, p.sibling)]

    def _copies(self, k, q):
        ch, target = self.plan[k]
        n_in, n_out = HALF_IN // self.CHUNKS, HALF_OUT // self.CHUNKS
        rows_in = pl.ds(pl.multiple_of(self.p.c * HALF_IN + q * n_in, n_in), n_in)
        cols_in = pl.ds(pl.multiple_of(ch * W_IN_COLS, 128), W_IN_COLS)
        rows_out = pl.ds(pl.multiple_of(ch * W_OUT_ROWS + self.p.c * HALF_OUT + q * n_out, n_out), n_out)
        r_in = self.win.at[rows_in, cols_in]
        r_out = self.wout.at[rows_out, :]
        s = 2 * (6 * q + k)
        return [pltpu.make_async_remote_copy(r_in, r_in, self.send.at[s], self.recv.at[s],
                                             device_id=target, device_id_type=MESH),
                pltpu.make_async_remote_copy(r_out, r_out, self.send.at[s + 1], self.recv.at[s + 1],
                                             device_id=target, device_id_type=MESH)]

    def _start(self, k, q):
        for cp in self._copies(k, q):
            cp.start()

    def _landed(self, k, q):
        for cp in self._copies(k, q):
            cp.wait_recv()

    def start_first_round(self):
        for q in range(self.CHUNKS):
            self._start(0, q)

    def start_second_round(self, q):
        self._landed(0, q)
        self._start(1, q)
        self._start(2, q)
        self._start(3, q)

    def pass_second_round(self, q):
        self._landed(1, q)
        self._start(4, q)
        self._landed(2, q)
        self._start(5, q)

    def finish(self):
        for q in range(self.CHUNKS):
            for k in (3, 4, 5):
                self._landed(k, q)
        for q in range(self.CHUNKS):
            for k in range(len(self.plan)):
                for cp in self._copies(k, q):
                    cp.wait_send()


def _forward_first(x, c_vec, w_ada, b_ada, w_in, w_out, small, ln_g, ln_b):
    n_saved = len(SAVED_COLS)

    def body(*refs):
        x_ref, c_ref, wada_hbm, bada_ref, win_hbm, wout_hbm = refs[:6]
        small_refs, lng_ref, lnb_ref = refs[6:12], refs[12], refs[13]
        saved_refs = refs[14:14 + n_saved]
        win0, wout0, win1, wout1, mod_out, c_out = refs[14 + n_saved:20 + n_saved]
        (carry_ref, wada_ref, win_ref, wout_ref, win_bf, wout_bf, mod_mine, mod_all, c_all, mod_ref, win_v, wout_v,
         g0_send, g0_recv, g1_send, g1_recv, c_send, c_recv, mod_send, mod_recv, local_sem) = refs[20 + n_saved:]
        i = pl.program_id(0)
        p = _Place()
        gather0 = _WeightGather(p, win0, wout0, g0_send, g0_recv)
        gather1 = _WeightGather(p, win1, wout1, g1_send, g1_recv)

        @pl.when(i == 0)
        def _():
            carry_ref[...] = jnp.zeros_like(carry_ref)
            loads = [pltpu.make_async_copy(win_hbm.at[0], win_ref.at[0], local_sem.at[4]),
                     pltpu.make_async_copy(wout_hbm.at[0], wout_ref.at[0], local_sem.at[5]),
                     pltpu.make_async_copy(win_hbm.at[1], win_ref.at[1], local_sem.at[6]),
                     pltpu.make_async_copy(wout_hbm.at[1], wout_ref.at[1], local_sem.at[7]),
                     pltpu.make_async_copy(wada_hbm, wada_ref, local_sem.at[8])]
            for cp in loads:
                cp.start()

            c_all[pl.ds(p.dev, 1), :] = c_ref[...]
            mine = c_all.at[pl.ds(p.dev, 1), :]
            c_copies = [pltpu.make_async_remote_copy(mine, mine, c_send.at[r], c_recv.at[r], device_id=d, device_id_type=MESH)
                        for r, d in enumerate(p.others())]
            for cp in c_copies:
                cp.start()

            cols = pl.ds(pl.multiple_of(p.chip * W_IN_COLS, 128), W_IN_COLS)
            rows = pl.ds(pl.multiple_of(p.chip * W_OUT_ROWS, W_OUT_ROWS), W_OUT_ROWS)
            own = [pltpu.make_async_copy(win_bf.at[0], win0.at[:, cols], local_sem.at[0]),
                   pltpu.make_async_copy(wout_bf.at[0], wout0.at[rows, :], local_sem.at[1]),
                   pltpu.make_async_copy(win_bf.at[1], win1.at[:, cols], local_sem.at[2]),
                   pltpu.make_async_copy(wout_bf.at[1], wout1.at[rows, :], local_sem.at[3])]
            for l in range(DEPTH):
                loads[2 * l].wait()
                win_bf[l] = win_ref[l].astype(BF16)
                own[2 * l].start()
                loads[2 * l + 1].wait()
                wout_bf[l] = wout_ref[l].astype(BF16)
                own[2 * l + 1].start()
                if l == 0:
                    own[0].wait()
                    own[1].wait()
                    gather0.start_first_round()
            for cp in c_copies:
                cp.wait()
            loads[4].wait()

            cv = c_all[...]
            c_out[...] = cv
            silu_c = (cv * _sigmoid(cv)).astype(BF16)
            for l in range(DEPTH):
                mod_mine[l] = _dot(silu_c, wada_ref[l].astype(BF16))
            mod_all[p.chip] = mod_mine[...]
            m_copies = [pltpu.make_async_remote_copy(mod_mine, mod_all.at[p.chip], mod_send.at[k], mod_recv.at[k],
                                                     device_id=(px, py, p.c), device_id_type=MESH)
                        for k, (px, py) in enumerate(p.other_chips())]
            for cp in m_copies:
                cp.start()
            for q in range(gather0.CHUNKS):
                gather0.start_second_round(q)
            own[2].wait()
            own[3].wait()
            gather1.start_first_round()
            for cp in m_copies:
                cp.wait()
            mod_ref[...] = jnp.zeros_like(mod_ref)
            for l in range(DEPTH):
                full = jnp.concatenate([mod_all[ch, l, pl.ds(p.dev, 1), :] for ch in range(N_CHIP)], axis=1) + bada_ref[l:l + 1, :]
                for k in range(3):
                    mod_ref[l, k:k + 1, :] = full[:, k * D_MODEL:(k + 1) * D_MODEL]
            mod_out[...] = mod_ref[...]
            for q in range(gather0.CHUNKS):
                gather0.pass_second_round(q)
            gather0.finish()
            fetch = [pltpu.make_async_copy(win0, win_v, local_sem.at[9]), pltpu.make_async_copy(wout0, wout_v, local_sem.at[10])]
            for cp in fetch:
                cp.start()
            for cp in fetch:
                cp.wait()

        for q in range(_WeightGather.CHUNKS):
            @pl.when(i == GATHER_SECOND_ROUND_STEP + q)
            def _(q=q):
                gather1.start_second_round(q)

            @pl.when(i == GATHER_PASS_STEP + q)
            def _(q=q):
                gather1.pass_second_round(q)

        _forward_tile(0, i, x_ref, mod_ref.at[0], win_v, wout_v, small_refs, lng_ref, lnb_ref, carry_ref, saved_refs)

        @pl.when(i == N_TILE - 1)
        def _():
            gather1.finish()

    hbm = pl.BlockSpec(memory_space=pl.ANY)
    tile3 = pl.BlockSpec((None, ROWS, D_MODEL), lambda i: (0, i, 0))
    in_specs = [tile3, _const_in((1, D_MODEL)), hbm, _const_in((DEPTH, 3 * D_MODEL)), hbm, hbm]
    in_specs += _layer_weight_specs(0) + [_const_in((DEPTH, D_MODEL)), _const_in((DEPTH, D_MODEL))]
    out_shape, out_specs = _saved_outputs()
    w_in_shape = jax.ShapeDtypeStruct((D_MODEL, D_PROJ), BF16)
    w_out_shape = jax.ShapeDtypeStruct((D_MODEL, D_MODEL), BF16)
    out_shape += [w_in_shape, w_out_shape, w_in_shape, w_out_shape,
                  jax.ShapeDtypeStruct((DEPTH, 8, D_MODEL), F32), jax.ShapeDtypeStruct((N_DEV, D_MODEL), F32)]
    out_specs += [hbm, hbm, hbm, hbm, _const((DEPTH, 8, D_MODEL)), _const((N_DEV, D_MODEL))]
    gather_sems = [pltpu.SemaphoreType.DMA((_WeightGather.N_SEMS,))] * 4
    scratch = [
        pltpu.VMEM((HALO, D_POOL), F32),
        pltpu.VMEM(w_ada.shape, F32), pltpu.VMEM(w_in.shape, F32), pltpu.VMEM(w_out.shape, F32),
        pltpu.VMEM((DEPTH, D_MODEL, W_IN_COLS), BF16), pltpu.VMEM((DEPTH, W_OUT_ROWS, D_MODEL), BF16),
        pltpu.VMEM((DEPTH, N_DEV, W_ADA_COLS), F32), pltpu.VMEM((N_CHIP, DEPTH, N_DEV, W_ADA_COLS), F32),
        pltpu.VMEM((N_DEV, D_MODEL), F32), pltpu.VMEM((DEPTH, 8, D_MODEL), F32),
        pltpu.VMEM((D_MODEL, D_PROJ), BF16), pltpu.VMEM((D_MODEL, D_MODEL), BF16),
    ] + gather_sems + [
        pltpu.SemaphoreType.DMA((7,)), pltpu.SemaphoreType.DMA((7,)),
        pltpu.SemaphoreType.DMA((3,)), pltpu.SemaphoreType.DMA((3,)),
        pltpu.SemaphoreType.DMA((11,)),
    ]
    return pl.pallas_call(
        body, name="fwd_first", grid=(N_TILE,), in_specs=in_specs, out_specs=out_specs, out_shape=out_shape,
        scratch_shapes=scratch,
        compiler_params=pltpu.CompilerParams(dimension_semantics=("arbitrary",), vmem_limit_bytes=VMEM_LIMIT),
    )(x, c_vec, w_ada, b_ada, w_in, w_out, *small, ln_g, ln_b)


IN_STEPS = W_IN_COLS // HEAD
OUT_STEPS = 4
OUT_COLS = D_MODEL // OUT_STEPS
OUT_FIRST = 2
ITEMS = ([("out", k) for k in range(OUT_FIRST)] + [("in", k) for k in range(IN_STEPS)]
         + [("out", k) for k in range(OUT_FIRST, OUT_STEPS)])
N_ITEMS = len(ITEMS)
N_STEPS = DEPTH * N_ITEMS
DELAY_SUM, DELAY_SECOND, DELAY_FINAL = 1, 3, 5
SMALL_SCATTER_STEP, SMALL_GATHER_STEP, SMALL_PASS_STEP, SMALL_FINISH_STEP = 1, 3, 5, 7


def _wgrad_reduce(h, dproj, cat, dy, pack, dmod):
    def body(*refs):
        h_ref, dp_refs, cat_ref, dy_ref, pack_ref, dmod_ref = refs[0], refs[1:5], refs[5], refs[6], refs[7], refs[8]
        fin_in, fin_out, pack_out, dmod_out = refs[9:13]
        scratch = refs[13:]
        (mine_in, send_in, sib_in, st_in, r1_in, r2_in, f_in,
         mine_out, send_out, sib_out, st_out, r1_out, r2_out, f_out,
         d2d_s, d2d_r, r1_s, r1_r, r2_s, r2_r, fin_l, fin_s, fin_r) = scratch[:23]
        p = _Place()
        c = p.c
        i = pl.program_id(0)
        my_rows = pl.ds(pl.multiple_of(c * HALF_IN, HALF_IN), HALF_IN)

        def layer_of(j):
            return DEPTH - 1 - j // N_ITEMS

        def bufs(j):
            kind, k = ITEMS[j % N_ITEMS]
            if kind == "in":
                return [r.at[k] for r in (mine_in, send_in, sib_in, st_in, r1_in, r2_in, f_in)]
            return [r.at[k] for r in (mine_out, send_out, sib_out, st_out, r1_out, r2_out, f_out)]

        def piece(j, ref, ch):
            if ITEMS[j % N_ITEMS][0] == "in":
                return ref.at[:, ch * HEAD:(ch + 1) * HEAD]
            return ref.at[ch]

        def slot(ch):
            return jnp.where(c == 0, ch % 2, ch // 2)

        def to_sibling(j):
            _, send, sib, _, _, _, _ = bufs(j)
            return pltpu.make_async_remote_copy(send, sib, d2d_s.at[j], d2d_r.at[j], device_id=p.sibling, device_id_type=MESH)

        def first_round(j, ch):
            _, _, _, st, r1, _, _ = bufs(j)
            k = slot(ch)
            return pltpu.make_async_remote_copy(st.at[k], r1.at[k], r1_s.at[2 * j + k], r1_r.at[2 * j + k],
                                                device_id=p.first, device_id_type=MESH)

        def second_round(j):
            _, _, _, st, _, r2, _ = bufs(j)
            return pltpu.make_async_remote_copy(st.at[2], r2, r2_s.at[j], r2_r.at[j], device_id=p.second, device_id_type=MESH)

        def finals(j):
            f = bufs(j)[6]
            kind, k = ITEMS[j % N_ITEMS]
            if kind == "in":
                dst = fin_in.at[layer_of(j), my_rows, k * HEAD:(k + 1) * HEAD]
            else:
                dst = fin_out.at[layer_of(j), c, :, k * OUT_COLS:(k + 1) * OUT_COLS]
            return [pltpu.make_async_copy(f, dst, fin_l.at[j]),
                    pltpu.make_async_remote_copy(f, dst, fin_s.at[j], fin_r.at[j], device_id=p.sibling, device_id_type=MESH)]

        def stage_sum(j):
            mine, _, sib, st, _, _, _ = bufs(j)
            to_sibling(j).wait_recv()
            mine[...] = mine[...] + sib[...]
            for ch in range(N_CHIP):
                @pl.when(p.first_coord(ch) != p.my_first_coord)
                def _(ch=ch):
                    st[slot(ch)] = piece(j, mine, ch)[...].astype(BF16)
                    first_round(j, ch).start()

        def stage_second(j):
            mine, _, _, st, r1, _, _ = bufs(j)
            for ch in range(N_CHIP):
                @pl.when(p.first_coord(ch) == p.my_first_coord)
                def _(ch=ch):
                    first_round(j, ch).wait_recv()
                    part = piece(j, mine, ch)
                    total = part[...] + r1[slot(ch)].astype(F32)
                    part[...] = total

                    @pl.when(ch != p.chip)
                    def _():
                        st[2] = total.astype(BF16)
                        second_round(j).start()

        def stage_final(j):
            mine, _, _, _, _, r2, f = bufs(j)
            second_round(j).wait_recv()
            for ch in range(N_CHIP):
                @pl.when(ch == p.chip)
                def _(ch=ch):
                    f[...] = piece(j, mine, ch)[...] + r2[...].astype(F32)
            for cp in finals(j):
                cp.start()

        def drain(j):
            to_sibling(j).wait_send()
            for ch in range(N_CHIP):
                @pl.when(p.first_coord(ch) != p.my_first_coord)
                def _(ch=ch):
                    first_round(j, ch).wait_send()

                @pl.when(jnp.logical_and(p.first_coord(ch) == p.my_first_coord, ch != p.chip))
                def _():
                    second_round(j).wait_send()
            for cp in finals(j):
                cp.wait()

        dev = p.dev
        devices = p.others()

        def half(core):
            return pl.ds(pl.multiple_of(core * PK_HALF, 16), PK_HALF)

        def finished(core, ch):
            return pl.ds(pl.multiple_of(core * PK_HALF + ch * PK_PIECE, 16), PK_PIECE)

        def small_exchange(l, first_step, bufs_l):
            (pk_mine, pk_sib, pk_st, pk_rs, pk_fin, pk_all, dm_st, dm_all, pk_sem, rs_s, rs_r, ag_s, ag_r, dm_s, dm_r) = bufs_l

            def pk_load():
                return pltpu.make_async_copy(pack_ref.at[l, half(c)], pk_mine, pk_sem.at[0])

            def pk_give():
                return pltpu.make_async_remote_copy(pack_ref.at[l, half(1 - c)], pk_sib, pk_sem.at[1], pk_sem.at[2],
                                                    device_id=p.sibling, device_id_type=MESH)

            def pk_scatter(ch):
                return pltpu.make_async_remote_copy(pk_st.at[ch * PK_PIECE:(ch + 1) * PK_PIECE], pk_rs.at[p.chip],
                                                    rs_s.at[ch], rs_r.at[p.chip], device_id=(ch // 2, ch % 2, c),
                                                    device_id_type=MESH)

            def pk_spread(ch):
                return pltpu.make_async_remote_copy(pk_fin, pk_all.at[finished(c, p.chip)], ag_s.at[ch], ag_r.at[p.chip],
                                                    device_id=(ch // 2, ch % 2, c), device_id_type=MESH)

            def pk_pass():
                return pltpu.make_async_remote_copy(pk_all.at[half(c)], pk_all.at[half(c)], pk_sem.at[3], pk_sem.at[4],
                                                    device_id=p.sibling, device_id_type=MESH)

            def dm_copy(r):
                return pltpu.make_async_remote_copy(dm_st, dm_all.at[:, pl.ds(dev, 1), :], dm_s.at[r], dm_r.at[r],
                                                    device_id=devices[r], device_id_type=MESH)

            def results():
                return [pltpu.make_async_copy(pk_all, pack_out.at[l], pk_sem.at[0]),
                        pltpu.make_async_copy(dm_all, dmod_out.at[l], pk_sem.at[5])]

            @pl.when(i == first_step)
            def _():
                pk_load().start()
                pk_give().start()
                for k in range(3):
                    for r in range(D_MODEL // HEAD):
                        dm_st[8 * k + r] = dmod_ref[l, k:k + 1, r * HEAD:(r + 1) * HEAD]
                dm_st[DM_LOSS] = dmod_ref[l, 3:4, 0:HEAD]
                dm_all[:, pl.ds(dev, 1), :] = dm_st[...]
                for r in range(N_DEV - 1):
                    dm_copy(r).start()

            @pl.when(i == first_step + SMALL_SCATTER_STEP)
            def _():
                pk_load().wait()
                pk_give().wait()
                total = pk_mine[...] + pk_sib[...]
                pk_mine[...] = total
                pk_st[...] = total.astype(BF16)
                for ch in range(N_CHIP):
                    @pl.when(ch != p.chip)
                    def _(ch=ch):
                        pk_scatter(ch).start()

            @pl.when(i == first_step + SMALL_GATHER_STEP)
            def _():
                for ch in range(N_CHIP):
                    @pl.when(ch != p.chip)
                    def _(ch=ch):
                        pltpu.make_async_remote_copy(pk_fin, pk_rs.at[ch], rs_s.at[ch], rs_r.at[ch],
                                                     device_id=p.sibling, device_id_type=MESH).wait_recv()
                for me in range(N_CHIP):
                    @pl.when(me == p.chip)
                    def _(me=me):
                        total = None
                        for ch in range(N_CHIP):
                            part = pk_mine[me * PK_PIECE:(me + 1) * PK_PIECE] if ch == me else pk_rs[ch].astype(F32)
                            total = part if total is None else total + part
                        pk_fin[...] = total.astype(BF16)
                        pk_all[finished(c, me)] = total.astype(BF16)
                for ch in range(N_CHIP):
                    @pl.when(ch != p.chip)
                    def _(ch=ch):
                        pk_spread(ch).start()

            @pl.when(i == first_step + SMALL_PASS_STEP)
            def _():
                for ch in range(N_CHIP):
                    @pl.when(ch != p.chip)
                    def _(ch=ch):
                        pltpu.make_async_remote_copy(pk_fin, pk_all.at[finished(c, ch)], ag_s.at[ch], ag_r.at[ch],
                                                     device_id=p.sibling, device_id_type=MESH).wait_recv()
                pk_pass().start()

            @pl.when(i == first_step + SMALL_FINISH_STEP)
            def _():
                pk_pass().wait()
                for ch in range(N_CHIP):
                    @pl.when(ch != p.chip)
                    def _(ch=ch):
                        pk_scatter(ch).wait_send()
                        pk_spread(ch).wait_send()
                for r in range(N_DEV - 1):
                    dm_copy(r).wait()
                for cp in results():
                    cp.start()
                for cp in results():
                    cp.wait()

        n_small = 15
        for l in range(DEPTH):
            small_exchange(l, (DEPTH - 1 - l) * N_ITEMS, scratch[23 + n_small * l:23 + n_small * (l + 1)])

        for step in range(N_ITEMS, N_STEPS):
            @pl.when(i == step)
            def _(step=step):
                drain(step - N_ITEMS)

        ii = jnp.where(i < N_ITEMS, i, i - N_ITEMS)
        in_step = jnp.logical_and(ii >= OUT_FIRST, ii < OUT_FIRST + IN_STEPS)

        @pl.when(in_step)
        def _():
            k = ii - OUT_FIRST
            rhs = jnp.concatenate([r[...] for r in dp_refs], axis=1)
            res = _dot(h_ref[...], rhs, TN)

            @pl.when(c == 0)
            def _():
                mine_in[k] = res[:HALF_IN]
                send_in[k] = res[HALF_IN:]

            @pl.when(c == 1)
            def _():
                mine_in[k] = res[HALF_IN:]
                send_in[k] = res[:HALF_IN]

        @pl.when(jnp.logical_not(in_step))
        def _():
            k = jnp.where(ii < OUT_FIRST, ii, ii - IN_STEPS)
            res = _dot(cat_ref[...], dy_ref[...], TN)

            @pl.when(c == 0)
            def _():
                for ch in range(N_CHIP):
                    mine_out[k, ch] = res[ch * W_OUT_ROWS:ch * W_OUT_ROWS + HALF_OUT]
                    send_out[k, ch] = res[ch * W_OUT_ROWS + HALF_OUT:(ch + 1) * W_OUT_ROWS]

            @pl.when(c == 1)
            def _():
                for ch in range(N_CHIP):
                    mine_out[k, ch] = res[ch * W_OUT_ROWS + HALF_OUT:(ch + 1) * W_OUT_ROWS]
                    send_out[k, ch] = res[ch * W_OUT_ROWS:ch * W_OUT_ROWS + HALF_OUT]

        stages = ((0, lambda j: to_sibling(j).start()), (DELAY_SUM, stage_sum), (DELAY_SECOND, stage_second),
                  (DELAY_FINAL, stage_final))
        for step in range(N_STEPS):
            @pl.when(i == step)
            def _(step=step):
                for delay, stage in stages:
                    if step - delay >= 0:
                        stage(step - delay)

        @pl.when(i == N_STEPS - 1)
        def _():
            for step in range(N_STEPS, N_STEPS + DELAY_FINAL):
                for delay, stage in stages:
                    if 0 <= step - delay < N_STEPS:
                        stage(step - delay)
            for j in range(N_STEPS - N_ITEMS, N_STEPS):
                drain(j)

    hbm = pl.BlockSpec(memory_space=pl.ANY)

    def layer(i):
        return jnp.where(i < N_ITEMS, DEPTH - 1, 0)

    def item(i):
        return jnp.where(i < N_ITEMS, i, i - N_ITEMS)

    def whole(i):
        return (layer(i), 0, 0)

    def dproj_piece(ch):
        return pl.BlockSpec((None, SEQ, HEAD),
                            lambda i: (layer(i), 0, ch * IN_STEPS + jnp.clip(item(i) - OUT_FIRST, 0, IN_STEPS - 1)))

    def dy_quarter(i):
        return (layer(i), 0, jnp.where(item(i) < OUT_FIRST, item(i), jnp.maximum(item(i) - IN_STEPS, OUT_FIRST)))

    operand = pl.BlockSpec((None, SEQ, D_MODEL), whole, pipeline_mode=pl.Buffered(1))
    in_specs = [operand] + [dproj_piece(ch) for ch in range(N_CHIP)]
    in_specs += [operand, pl.BlockSpec((None, SEQ, OUT_COLS), dy_quarter), hbm, _const_in((DEPTH, 8, D_MODEL))]
    args = [h, dproj, dproj, dproj, dproj, cat, dy, pack, dmod]
    out_shape = [jax.ShapeDtypeStruct((DEPTH, D_MODEL, W_IN_COLS), F32), jax.ShapeDtypeStruct((DEPTH, 2, HALF_OUT, D_MODEL), F32),
                 jax.ShapeDtypeStruct((DEPTH, PK_ROWS, HEAD), BF16), jax.ShapeDtypeStruct((DEPTH, DM_ROWS, N_DEV, HEAD), F32)]
    out_specs = [hbm, hbm, hbm, hbm]
    in_item = lambda *lead: pltpu.VMEM(lead + (HALF_IN, HEAD), BF16)
    out_item = lambda *lead: pltpu.VMEM(lead + (HALF_OUT, OUT_COLS), BF16)
    scratch = [
        pltpu.VMEM((IN_STEPS, HALF_IN, N_CHIP * HEAD), F32), pltpu.VMEM((IN_STEPS, HALF_IN, N_CHIP * HEAD), F32),
        pltpu.VMEM((IN_STEPS, HALF_IN, N_CHIP * HEAD), F32), in_item(IN_STEPS, 3), in_item(IN_STEPS, 2), in_item(IN_STEPS),
        pltpu.VMEM((IN_STEPS, HALF_IN, HEAD), F32),
        pltpu.VMEM((OUT_STEPS, N_CHIP, HALF_OUT, OUT_COLS), F32), pltpu.VMEM((OUT_STEPS, N_CHIP, HALF_OUT, OUT_COLS), F32),
        pltpu.VMEM((OUT_STEPS, N_CHIP, HALF_OUT, OUT_COLS), F32), out_item(OUT_STEPS, 3), out_item(OUT_STEPS, 2),
        out_item(OUT_STEPS), pltpu.VMEM((OUT_STEPS, HALF_OUT, OUT_COLS), F32),
        pltpu.SemaphoreType.DMA((N_STEPS,)), pltpu.SemaphoreType.DMA((N_STEPS,)),
        pltpu.SemaphoreType.DMA((2 * N_STEPS,)), pltpu.SemaphoreType.DMA((2 * N_STEPS,)),
        pltpu.SemaphoreType.DMA((N_STEPS,)), pltpu.SemaphoreType.DMA((N_STEPS,)),
        pltpu.SemaphoreType.DMA((N_STEPS,)), pltpu.SemaphoreType.DMA((N_STEPS,)), pltpu.SemaphoreType.DMA((N_STEPS,)),
    ]
    for _ in range(DEPTH):
        scratch += [
            pltpu.VMEM((PK_HALF, HEAD), F32), pltpu.VMEM((PK_HALF, HEAD), F32), pltpu.VMEM((PK_HALF, HEAD), BF16),
            pltpu.VMEM((N_CHIP, PK_PIECE, HEAD), BF16), pltpu.VMEM((PK_PIECE, HEAD), BF16), pltpu.VMEM((PK_ROWS, HEAD), BF16),
            pltpu.VMEM((DM_ROWS, 1, HEAD), F32), pltpu.VMEM((DM_ROWS, N_DEV, HEAD), F32),
            pltpu.SemaphoreType.DMA((6,)),
            pltpu.SemaphoreType.DMA((N_CHIP,)), pltpu.SemaphoreType.DMA((N_CHIP,)),
            pltpu.SemaphoreType.DMA((N_CHIP,)), pltpu.SemaphoreType.DMA((N_CHIP,)),
            pltpu.SemaphoreType.DMA((N_DEV - 1,)), pltpu.SemaphoreType.DMA((N_DEV - 1,)),
        ]
    return pl.pallas_call(
        body, name="wgrad", grid=(N_STEPS,), in_specs=in_specs, out_specs=out_specs, out_shape=out_shape,
        scratch_shapes=scratch,
        compiler_params=pltpu.CompilerParams(dimension_semantics=("arbitrary",), vmem_limit_bytes=VMEM_LIMIT),
    )(*args)


def _adamw(w, g, m, v):
    m = ADAM_B1 * m + (1.0 - ADAM_B1) * g
    v = ADAM_B2 * v + (1.0 - ADAM_B2) * (g * g)
    m_hat = m / (1.0 - ADAM_B1 ** ADAM_STEP)
    v_hat = v / (1.0 - ADAM_B2 ** ADAM_STEP)
    delta = -ADAM_LR * (m_hat / (jnp.sqrt(v_hat) + ADAM_EPS) + ADAM_WD * w)
    return delta, m, v


def _adam_sharded(c_all, dmods, ada, w_in_set, w_out_set):
    rows = D_MODEL // ADAM_PARTS

    def body(c_ref, dm_ref, wa_ref, ma_ref, va_ref, wi_ref, gi_ref, mi_ref, vi_ref, wo_ref, go_ref, mo_ref, vo_ref,
             ga_out, da_out, ma_out, va_out, di_out, mi_out, vi_out, do_out, mo_out, vo_out):
        l = pl.program_id(0)
        chip = 2 * lax.axis_index("x") + lax.axis_index("y")
        cv = c_ref[...]
        silu_c = (cv * _sigmoid(cv)).astype(BF16).astype(F32)
        pieces = []
        for k in range(W_ADA_COLS // HEAD):
            dk = dm_ref[l, (W_ADA_COLS // HEAD) * chip + k].astype(BF16).astype(F32)
            pieces.append(_dot_exact(silu_c, dk, TN))
        g = jnp.concatenate(pieces, axis=1)
        ga_out[...] = g
        da_out[...], ma_out[...], va_out[...] = _adamw(wa_ref[...], g, ma_ref[...], va_ref[...])
        di_out[...], mi_out[...], vi_out[...] = _adamw(wi_ref[...], gi_ref[...], mi_ref[...], vi_ref[...])
        do_out[...], mo_out[...], vo_out[...] = _adamw(wo_ref[...], go_ref[...], mo_ref[...], vo_ref[...])

    def blk(r, cols):
        return pl.BlockSpec((None, r, cols), lambda l, i: (l, i, 0))

    b_ada, b_in, b_out = blk(rows, W_ADA_COLS), blk(rows, W_IN_COLS), blk(W_OUT_ROWS // ADAM_PARTS, D_MODEL)
    shapes = [jax.ShapeDtypeStruct(a[0].shape, F32) for a in (ada, w_in_set, w_out_set)]
    return pl.pallas_call(
        body, name="adam_sharded", grid=(DEPTH, ADAM_PARTS),
        in_specs=[pl.BlockSpec((N_DEV, rows), lambda l, i: (0, i)), _const_in((DEPTH, DM_ROWS, N_DEV, HEAD))]
        + [b_ada] * 3 + [b_in] * 4 + [b_out] * 4,
        out_specs=[b_ada] * 4 + [b_in] * 3 + [b_out] * 3,
        out_shape=[shapes[0]] * 4 + [shapes[1]] * 3 + [shapes[2]] * 3,
        compiler_params=pltpu.CompilerParams(dimension_semantics=("arbitrary", "arbitrary"), vmem_limit_bytes=VMEM_LIMIT),
    )(c_all, dmods, *ada, *w_in_set, *w_out_set)


def _adam_small(packs, dmods, weights, ms, vs):
    n = len(weights)

    def body(*refs):
        dm_refs = refs[1]
        b = 2
        w_refs, m_refs, v_refs = refs[b:b + n], refs[b + n:b + 2 * n], refs[b + 2 * n:b + 3 * n]
        outs = refs[b + 3 * n:b + 3 * n + 4 * n + 1]
        pack_refs = refs[-1]
        pack_refs[...] = refs[0][...].astype(F32)
        g_refs, d_refs, nm_refs, nv_refs = outs[0:n], outs[n:2 * n], outs[2 * n:3 * n], outs[3 * n:4 * n]
        squares = dm_refs[DEPTH - 1, DM_LOSS]
        total = squares[0:1, 0:1]
        for d in range(1, N_DEV):
            total = total + squares[d:d + 1, 0:1]
        outs[4 * n][...] = total * (0.5 / D_MODEL)

        def lanes(l, row0, count):
            return jnp.concatenate([pack_refs.at[l][row0 + k:row0 + k + 1, :] for k in range(count)], axis=1)

        def update(idx, at, g):
            g_refs[idx][at] = g
            d_refs[idx][at], nm_refs[idx][at], nv_refs[idx][at] = _adamw(w_refs[idx][at], g, m_refs[idx][at], v_refs[idx][at])

        for l in range(DEPTH):
            row = (slice(l, l + 1), slice(None))
            g_b = None
            for d in range(N_DEV):
                part = dm_refs.at[l][0:DM_LOSS, d, :]
                g_b = part if g_b is None else g_b + part
            update(0, row, jnp.concatenate([g_b[k:k + 1, :] for k in range(DM_LOSS)], axis=1))
            for g in range(N_HEAD):
                update(1, (l, g), pack_refs.at[l][PK_W_POOL + g * HEAD:PK_W_POOL + (g + 1) * HEAD, :])
                update(5, (l, g), pack_refs.at[l][PK_W_SGU + g * HEAD:PK_W_SGU + (g + 1) * HEAD, :])
            update(2, row, lanes(l, PK_POOL_SCALE, N_HEAD))
            update(3, (l,), pack_refs.at[l][PK_SGU_LN_G:PK_SGU_LN_G + N_HEAD, :])
            update(4, (l,), pack_refs.at[l][PK_SGU_LN_B:PK_SGU_LN_B + N_HEAD, :])
            update(6, (l,), pack_refs.at[l][PK_B_SGU:PK_B_SGU + N_HEAD, :])
            update(7, row, lanes(l, PK_LN_G, D_MODEL // HEAD))
            update(8, row, lanes(l, PK_LN_B, D_MODEL // HEAD))

    vmem = pl.BlockSpec(memory_space=pltpu.VMEM)
    shapes = [jax.ShapeDtypeStruct(w.shape, F32) for w in weights]
    return pl.pallas_call(
        body, name="adam_small", in_specs=[vmem] * (2 + 3 * n), out_specs=[vmem] * (4 * n + 1),
        out_shape=shapes * 4 + [jax.ShapeDtypeStruct((1, 1), F32)],
        scratch_shapes=[pltpu.VMEM(packs.shape, F32)],
        compiler_params=pltpu.CompilerParams(vmem_limit_bytes=VMEM_LIMIT),
    )(packs, dmods, *weights, *ms, *vs)


def kernel(x, c, w_ada, b_ada, w_in, w_pool, pool_scale, sgu_ln_g, sgu_ln_b, w_sgu, b_sgu, w_out, ln_g, ln_b, loss_target, m_w_ada, m_b_ada, m_w_in, m_w_pool, m_pool_scale, m_sgu_ln_g, m_sgu_ln_b, m_w_sgu, m_b_sgu, m_w_out, m_ln_g, m_ln_b, v_w_ada, v_b_ada, v_w_in, v_w_pool, v_pool_scale, v_sgu_ln_g, v_sgu_ln_b, v_w_sgu, v_b_sgu, v_w_out, v_ln_g, v_ln_b):
    small = (w_pool, pool_scale, sgu_ln_g, sgu_ln_b, w_sgu, b_sgu)
    *saved0, w_in0, w_out0, w_in1, w_out1, mod, c_all = _forward_first(x, c, w_ada, b_ada, w_in, w_out, small, ln_g, ln_b)
    *saved1, dout, sq = _forward_last(saved0[3], mod, w_in1, w_out1, small, ln_g, ln_b, loss_target)

    dx1, *shared = _backward_layer(1, dout, saved1, mod, w_in1, w_out1, small, ln_g, sq=sq)
    dx0, h, cat, dy, dproj, pack, dmod = _backward_layer(0, dx1, saved0, mod, w_in0, w_out0, small, ln_g, shared=shared)
    g_in, g_out, pack, dmods = _wgrad_reduce(h, dproj, cat, dy, pack, dmod)

    g_out = g_out.reshape(DEPTH, W_OUT_ROWS, D_MODEL)
    big = _adam_sharded(c_all, dmods, (w_ada, m_w_ada, v_w_ada), (w_in, g_in, m_w_in, v_w_in), (w_out, g_out, m_w_out, v_w_out))
    ada, win, wout = big[0:4], (g_in, *big[4:7]), (g_out, *big[7:10])
    small_w = (b_ada, w_pool, pool_scale, sgu_ln_g, sgu_ln_b, w_sgu, b_sgu, ln_g, ln_b)
    small_m = (m_b_ada, m_w_pool, m_pool_scale, m_sgu_ln_g, m_sgu_ln_b, m_w_sgu, m_b_sgu, m_ln_g, m_ln_b)
    small_v = (v_b_ada, v_w_pool, v_pool_scale, v_sgu_ln_g, v_sgu_ln_b, v_w_sgu, v_b_sgu, v_ln_g, v_ln_b)
    res = _adam_small(pack, dmods, small_w, small_m, small_v)
    n = len(small_w)
    loss = res[4 * n].reshape(())

    def ordered(k):
        s = res[k * n:(k + 1) * n]
        return (ada[k], s[0], win[k], s[1], s[2], s[3], s[4], s[5], s[6], wout[k], s[7], s[8])

    return (loss, dx0[None], *ordered(0), *ordered(1), *ordered(2), *ordered(3))
```

```python
import jax
import jax.numpy as jnp
from jax import lax
from jax.experimental import pallas as pl
from jax.experimental.pallas import tpu as pltpu

F32 = jnp.float32
BF16 = jnp.bfloat16
MESH = pl.DeviceIdType.MESH

N_DEV = 8
N_CHIP = 4
DEPTH = 2
SEQ = 2048
D_MODEL = 1024
D_POOL = 512
D_PROJ = 2560
HEAD = 128
N_HEAD = 4
ROWS = 256
N_TILE = SEQ // ROWS
HALO = 16
W_IN_COLS = D_PROJ // N_CHIP
W_OUT_ROWS = D_MODEL // N_CHIP
W_ADA_COLS = 3 * D_MODEL // N_CHIP
HALF_IN = D_MODEL // 2
HALF_OUT = W_OUT_ROWS // 2
DEEPNORM_ALPHA = (2.0 * DEPTH) ** 0.25
LN_EPS = 1e-5
INV_SQRT2 = 0.7071067811865476
INV_SQRT_2PI = 0.3989422804014327

ADAM_LR = 0.001
ADAM_B1 = 0.9
ADAM_B2 = 0.999
ADAM_EPS = 1e-08
ADAM_WD = 0.01
ADAM_STEP = 10
ADAM_PARTS = 2

PK_W_POOL = 0
PK_W_SGU = 512
PK_POOL_SCALE = 1024
PK_SGU_LN_G = 1032
PK_SGU_LN_B = 1040
PK_B_SGU = 1048
PK_LN_G = 1056
PK_LN_B = 1064
PK_ROWS = 1152
PK_HALF = PK_ROWS // 2
PK_PIECE = PK_HALF // N_CHIP
DM_LOSS = 3 * D_MODEL // HEAD
DM_ROWS = DM_LOSS + 1

VMEM_LIMIT = 56 * 1024 * 1024

GATHER_SECOND_ROUND_STEP = 0
GATHER_PASS_STEP = 2

NN = (((1,), (0,)), ((), ()))
NT = (((1,), (1,)), ((), ()))
TN = (((0,), (0,)), ((), ()))


def _dot(a, b, dims=NN):
    return lax.dot_general(a, b, dims, preferred_element_type=F32)


def _dot_exact(a, b, dims=NN):
    return lax.dot_general(a, b, dims, preferred_element_type=F32, precision=lax.Precision.HIGHEST)


def _layer_norm(v):
    mu = jnp.mean(v, axis=-1, keepdims=True)
    d = v - mu
    var = jnp.mean(d * d, axis=-1, keepdims=True)
    rstd = lax.rsqrt(var + LN_EPS)
    return d * rstd, rstd


def _layer_norm_bwd(dvhat, vhat, rstd):
    m1 = jnp.mean(dvhat, axis=-1, keepdims=True)
    m2 = jnp.mean(dvhat * vhat, axis=-1, keepdims=True)
    return rstd * (dvhat - m1 - vhat * m2)


def _sigmoid(v):
    return 1.0 / (1.0 + jnp.exp(-v))


def _gelu_parts(v):
    phi = 0.5 * (1.0 + lax.erf(v * INV_SQRT2))
    pdf = INV_SQRT_2PI * jnp.exp(-0.5 * v * v)
    return phi, pdf


def _sum_rows(v):
    return jnp.sum(v, axis=0, keepdims=True)


def _window_sums(ext, toward_later):
    n = ext.shape[0]

    def shifted(v, k):
        return pltpu.roll(v, (n - k) if toward_later else k, 0)

    s2 = ext + shifted(ext, 1)
    r4 = s2[:, HEAD:]
    s4 = r4 + shifted(r4, 2)
    r8 = s4[:, HEAD:]
    s8 = r8 + shifted(r8, 4)
    r16 = s8[:, HEAD:]
    s16 = r16 + shifted(r16, 8)
    return jnp.concatenate([s2[:, :HEAD], s4[:, :HEAD], s8[:, :HEAD], s16], axis=1)


def _window_counts(row0):
    t1 = row0 + 1 + lax.broadcasted_iota(jnp.int32, (ROWS, D_POOL), 0)
    lane = lax.broadcasted_iota(jnp.int32, (ROWS, D_POOL), 1)
    width = jnp.where(lane < HEAD, 2, jnp.where(lane < 2 * HEAD, 4, jnp.where(lane < 3 * HEAD, 8, 16)))
    return jnp.minimum(t1, width).astype(F32)


def _causal_mask():
    r = lax.broadcasted_iota(jnp.int32, (HEAD, HEAD), 0)
    s = lax.broadcasted_iota(jnp.int32, (HEAD, HEAD), 1)
    return r >= s


def _chunks_to_lanes(v):
    return jnp.concatenate([v[n * HEAD:(n + 1) * HEAD] for n in range(ROWS // HEAD)], axis=1)


def _lanes_to_chunks(v):
    return jnp.concatenate([v[:, n * HEAD:(n + 1) * HEAD] for n in range(ROWS // HEAD)], axis=0)


def _pack_stats(rstd_x, rstd_z, rstd_v):
    lane = lax.broadcasted_iota(jnp.int32, (ROWS, HEAD), 1)
    packed = rstd_x
    for k, r in enumerate([rstd_z] + list(rstd_v)):
        packed = jnp.where(lane < 16 * (k + 1), packed, r)
    return packed


def _unpack_stats(stats):
    cols = [stats[:, 16 * k:16 * k + 1] for k in range(2 + N_HEAD)]
    return cols[0], cols[1], cols[2:]


def _mixer(proj, halo, row0, wpool_ref, pscale, sgu_g_ref, sgu_b_ref, wsgu_ref, bsgu_ref, saved=None):
    xa = proj[:, 0:512]
    ga = proj[:, 512:1024]
    u = proj[:, 1024:1536]
    v = proj[:, 1536:2048]
    gb = proj[:, 2048:2560]
    ext = jnp.concatenate([halo, xa], axis=0)
    win = _window_sums(ext, toward_later=False)[HALO:]
    cnt = _window_counts(row0)
    pooled = (win / cnt - xa).astype(BF16)
    pw = jnp.concatenate(
        [_dot(pooled[:, g * HEAD:(g + 1) * HEAD], wpool_ref[g].astype(BF16)) for g in range(N_HEAD)], axis=1)
    sig_a = _sigmoid(ga) if saved is None else saved["sig_a"]
    ya = pw * pscale * (ga * sig_a)
    phi_u, pdf_u = _gelu_parts(u)
    phi_v, pdf_v = _gelu_parts(v)
    gu = u * phi_u
    gv = v * phi_v
    sig_b = _sigmoid(gb) if saved is None else saved["sig_b"]
    silu_b = gb * sig_b
    mask = _causal_mask()
    diag = lax.broadcasted_iota(jnp.int32, (HEAD, HEAD), 0) == lax.broadcasted_iota(jnp.int32, (HEAD, HEAD), 1)
    vhat, rstd_v, vln_l, mixed = [], [], [], []
    for h in range(N_HEAD):
        if saved is None:
            vh, rh = _layer_norm(gv[:, h * HEAD:(h + 1) * HEAD])
        else:
            vh, rh = saved["vhat"][h], saved["rstd_v"][h]
        ln = (vh * sgu_g_ref[h:h + 1, :] + sgu_b_ref[h:h + 1, :]).astype(BF16)
        ln_l = _chunks_to_lanes(ln)
        wm = jnp.where(mask, wsgu_ref[h], 0.0).astype(BF16)
        bias = jnp.sum(jnp.where(diag, jnp.broadcast_to(bsgu_ref[h:h + 1, :], (HEAD, HEAD)), 0.0), axis=1, keepdims=True)
        mx = _lanes_to_chunks(_dot(wm, ln_l) + bias)
        vhat.append(vh)
        rstd_v.append(rh)
        vln_l.append(ln_l)
        mixed.append(mx)
    mixed = jnp.concatenate(mixed, axis=1)
    yb = gu * mixed * silu_b
    return dict(xa=xa, ga=ga, u=u, gb=gb, cnt=cnt, pooled=pooled, pw=pw, sig_a=sig_a, ya=ya, phi_u=phi_u, pdf_u=pdf_u,
                phi_v=phi_v, pdf_v=pdf_v, gu=gu, sig_b=sig_b, silu_b=silu_b, vhat=vhat, rstd_v=rstd_v, vln_l=vln_l,
                mixed=mixed, yb=yb, mask=mask)


def _const(shape, *index):
    lead = tuple(index) + (0,) * (len(shape) - len(index))
    return pl.BlockSpec(shape, lambda *_: lead)


def _const_in(shape, *index):
    lead = tuple(index) + (0,) * (len(shape) - len(index))
    return pl.BlockSpec(shape, lambda *_: lead, pipeline_mode=pl.Buffered(1))


def _layer_weight_specs(l):
    return [
        _const_in((None, N_HEAD, HEAD, HEAD), l),
        _const_in((DEPTH, D_POOL)),
        _const_in((None, N_HEAD, HEAD), l),
        _const_in((None, N_HEAD, HEAD), l),
        _const_in((None, N_HEAD, HEAD, HEAD), l),
        _const_in((None, N_HEAD, HEAD), l),
    ]


def _forward_tile(l, i, x_ref, mod_ref, win_ref, wout_ref, small_refs, lng_ref, lnb_ref, carry_ref, saved_refs):
    wpool_ref, pscale_ref, sgu_g_ref, sgu_b_ref, wsgu_ref, bsgu_ref = small_refs
    proj_ref, y_ref, xn_ref, zn_ref, stats_ref, sig_ref, vhat_ref = saved_refs
    x = x_ref[...]
    if l > 0:
        x = x * lng_ref[l - 1:l, :] + lnb_ref[l - 1:l, :]
    shift, scale, gate = mod_ref[0:1, :], mod_ref[1:2, :], mod_ref[2:3, :]
    xn, rstd_x = _layer_norm(x)
    xn_ref[...] = xn.astype(xn_ref.dtype)
    h = xn * (1.0 + scale) + shift
    proj = _dot(h.astype(BF16), win_ref[...])
    proj_ref[...] = proj
    m = _mixer(proj, carry_ref[...], i * ROWS, wpool_ref, pscale_ref[l:l + 1, :], sgu_g_ref, sgu_b_ref, wsgu_ref, bsgu_ref)
    carry_ref[...] = m["xa"][ROWS - HALO:]
    sig_ref[...] = jnp.concatenate([m["sig_a"], m["sig_b"]], axis=1).astype(sig_ref.dtype)
    vhat_ref[...] = jnp.concatenate(m["vhat"], axis=1).astype(vhat_ref.dtype)
    cat = jnp.concatenate([m["ya"], m["yb"]], axis=1).astype(BF16)
    y = _dot(cat, wout_ref[...])
    y_ref[...] = y.astype(y_ref.dtype)
    zn, rstd_z = _layer_norm(DEEPNORM_ALPHA * x + gate * y)
    zn_ref[...] = zn
    stats_ref[...] = _pack_stats(rstd_x, rstd_z, m["rstd_v"])
    return zn


SAVED_COLS = (D_PROJ, D_MODEL, D_MODEL, D_MODEL, HEAD, D_MODEL, D_POOL)
SAVED_TYPES = (F32, BF16, BF16, F32, F32, BF16, BF16)


def _saved_outputs():
    return ([jax.ShapeDtypeStruct((SEQ, cols), t) for cols, t in zip(SAVED_COLS, SAVED_TYPES)],
            [pl.BlockSpec((ROWS, cols), lambda i: (i, 0)) for cols in SAVED_COLS])


def _forward_last(zn_prev, mod, w_in, w_out, small, ln_g, ln_b, target):
    l = DEPTH - 1
    n_saved = len(SAVED_COLS)

    def body(*refs):
        x_ref, mod_ref, win_ref, wout_ref = refs[:4]
        small_refs, lng_ref, lnb_ref, tgt_ref = refs[4:10], refs[10], refs[11], refs[12]
        saved_refs = refs[13:13 + n_saved]
        dout_ref, loss_ref, carry_ref = refs[13 + n_saved:]
        i = pl.program_id(0)

        @pl.when(i == 0)
        def _():
            carry_ref[...] = jnp.zeros_like(carry_ref)
            loss_ref[...] = jnp.zeros_like(loss_ref)

        zn = _forward_tile(l, i, x_ref, mod_ref, win_ref, wout_ref, small_refs, lng_ref, lnb_ref, carry_ref, saved_refs)
        err = zn * lng_ref[l:l + 1, :] + lnb_ref[l:l + 1, :] - tgt_ref[...]
        dout_ref[...] = err * (1.0 / D_MODEL)
        loss_ref[...] += jnp.sum(err * err)

    tile = pl.BlockSpec((ROWS, D_MODEL), lambda i: (i, 0))
    tile3 = pl.BlockSpec((None, ROWS, D_MODEL), lambda i: (0, i, 0))
    in_specs = [tile, _const_in((None, 8, D_MODEL), l), _const_in((D_MODEL, D_PROJ)), _const_in((D_MODEL, D_MODEL))]
    in_specs += _layer_weight_specs(l) + [_const_in((DEPTH, D_MODEL)), _const_in((DEPTH, D_MODEL)), tile3]
    out_shape, out_specs = _saved_outputs()
    out_shape += [jax.ShapeDtypeStruct((SEQ, D_MODEL), F32), jax.ShapeDtypeStruct((8, HEAD), F32)]
    out_specs += [tile, _const((8, HEAD))]
    return pl.pallas_call(
        body, name="fwd_last", grid=(N_TILE,), in_specs=in_specs, out_specs=out_specs, out_shape=out_shape,
        scratch_shapes=[pltpu.VMEM((HALO, D_POOL), F32)],
        compiler_params=pltpu.CompilerParams(dimension_semantics=("arbitrary",), vmem_limit_bytes=VMEM_LIMIT),
    )(zn_prev, mod, w_in, w_out, *small, ln_g, ln_b, target)


def _backward_layer(l, dout, saved, mod, w_in, w_out, small, ln_g, sq=None, shared=None):
    has_loss = sq is not None

    def body(*refs):
        (dout_ref, proj_ref, y_ref, xn_ref, zn_ref, stats_ref, sig_ref, vhat_ref, halo_ref, mod_ref, win_ref, wout_ref,
         wpool_ref, pscale_ref, sgu_g_ref, sgu_b_ref, wsgu_ref, bsgu_ref, lng_ref) = refs[:19]
        n_in = 20 if has_loss else 19 + 6
        dx_ref, h_ref, cat_ref, dy_ref, dproj_ref, pack_ref, dmod_ref, carry_ref = refs[n_in:n_in + 8]
        i = pl.program_id(0)
        tile = N_TILE - 1 - i

        @pl.when(i == 0)
        def _():
            carry_ref[...] = jnp.zeros_like(carry_ref)
            pack_ref[...] = jnp.zeros_like(pack_ref)
            dmod_ref[...] = jnp.zeros_like(dmod_ref)
            if has_loss:
                dmod_ref[3:4, 0:HEAD] = refs[19][0:1, :]

        xn = xn_ref[...].astype(F32)
        zn = zn_ref[...]
        y = y_ref[...].astype(F32)
        dout = dout_ref[...]
        rstd_x, rstd_z, rstd_v = _unpack_stats(stats_ref[...])
        kept = dict(sig_a=sig_ref[:, :D_POOL].astype(F32), sig_b=sig_ref[:, D_POOL:].astype(F32), rstd_v=rstd_v,
                    vhat=[vhat_ref[:, hd * HEAD:(hd + 1) * HEAD].astype(F32) for hd in range(N_HEAD)])
        pscale = pscale_ref[l:l + 1, :]
        shift, scale, gate = mod_ref[0:1, :], mod_ref[1:2, :], mod_ref[2:3, :]
        h = xn * (1.0 + scale) + shift
        h_ref[...] = h.astype(BF16)
        g_ln_g = _sum_rows(dout * zn)
        g_ln_b = _sum_rows(dout)
        dz = _layer_norm_bwd(dout * lng_ref[l:l + 1, :], zn, rstd_z)
        d_gate = _sum_rows(dz * y)
        dy = (gate * dz).astype(BF16)
        dy_ref[...] = dy

        halo = jnp.where(tile > 0, halo_ref[...], 0.0)
        m = _mixer(proj_ref[...], halo, tile * ROWS, wpool_ref, pscale, sgu_g_ref, sgu_b_ref, wsgu_ref, bsgu_ref,
                   saved=kept)
        cat_ref[...] = jnp.concatenate([m["ya"], m["yb"]], axis=1).astype(BF16)
        dcat = _dot(dy, wout_ref[...], NT)
        dya = dcat[:, :D_POOL]
        dyb = dcat[:, D_POOL:]

        ga, sig_a = m["ga"], m["sig_a"]
        dp = dya * (ga * sig_a)
        d_ga = dya * (m["pw"] * pscale) * (sig_a * (1.0 + ga * (1.0 - sig_a)))
        g_pscale = _sum_rows(dp * m["pw"])
        dpw = (dp * pscale).astype(BF16)
        dpooled = []
        for g in range(N_HEAD):
            cols = slice(g * HEAD, (g + 1) * HEAD)
            pack_ref[PK_W_POOL + g * HEAD:PK_W_POOL + (g + 1) * HEAD, :] += _dot(m["pooled"][:, cols], dpw[:, cols], TN)
            dpooled.append(_dot(dpw[:, cols], wpool_ref[g].astype(BF16), NT))
        dpooled = jnp.concatenate(dpooled, axis=1)
        q = dpooled / m["cnt"]
        ext = jnp.concatenate([q, carry_ref[...]], axis=0)
        d_xa = _window_sums(ext, toward_later=True)[:ROWS] - dpooled
        carry_ref[...] = q[:HALO]

        gu, mixed, silu_b, gb, sig_b = m["gu"], m["mixed"], m["silu_b"], m["gb"], m["sig_b"]
        d_mixed = dyb * gu * silu_b
        d_gu = dyb * mixed * silu_b
        d_gb = dyb * gu * mixed * (sig_b * (1.0 + gb * (1.0 - sig_b)))
        d_u = d_gu * (m["phi_u"] + m["u"] * m["pdf_u"])
        ones = jnp.ones((8, HEAD), F32)
        d_v = []
        for hd in range(N_HEAD):
            cols = slice(hd * HEAD, (hd + 1) * HEAD)
            dm = d_mixed[:, cols]
            dm_l = _chunks_to_lanes(dm.astype(BF16))
            g_w = _dot(dm_l, m["vln_l"][hd], NT)
            pack_ref[PK_W_SGU + hd * HEAD:PK_W_SGU + (hd + 1) * HEAD, :] += jnp.where(m["mask"], g_w, 0.0)
            dm_sum = dm[0:HEAD]
            for n in range(1, ROWS // HEAD):
                dm_sum = dm_sum + dm[n * HEAD:(n + 1) * HEAD]
            pack_ref[PK_B_SGU + hd:PK_B_SGU + hd + 1, :] += _dot_exact(ones, dm_sum, NT)[0:1]
            wm = jnp.where(m["mask"], wsgu_ref[hd], 0.0).astype(BF16)
            d_vln = _lanes_to_chunks(_dot(wm, dm_l, TN))
            vhat = m["vhat"][hd]
            pack_ref[PK_SGU_LN_G + hd:PK_SGU_LN_G + hd + 1, :] += _sum_rows(d_vln * vhat)
            pack_ref[PK_SGU_LN_B + hd:PK_SGU_LN_B + hd + 1, :] += _sum_rows(d_vln)
            d_v.append(_layer_norm_bwd(d_vln * sgu_g_ref[hd:hd + 1, :], vhat, m["rstd_v"][hd]))
        v = proj_ref[:, 1536:2048]
        d_v = jnp.concatenate(d_v, axis=1) * (m["phi_v"] + v * m["pdf_v"])

        dproj = jnp.concatenate([d_xa, d_ga, d_u, d_v, d_gb], axis=1).astype(BF16)
        dproj_ref[...] = dproj
        dh = _dot(dproj, win_ref[...], NT)
        d_scale = _sum_rows(dh * xn)
        d_shift = _sum_rows(dh)
        dx_ref[...] = DEEPNORM_ALPHA * dz + _layer_norm_bwd(dh * (1.0 + scale), xn, rstd_x)

        dmod_ref[0:1, :] += d_shift
        dmod_ref[1:2, :] += d_scale
        dmod_ref[2:3, :] += d_gate
        for g in range(N_HEAD):
            pack_ref[PK_POOL_SCALE + g:PK_POOL_SCALE + g + 1, :] += g_pscale[:, g * HEAD:(g + 1) * HEAD]
        for k in range(D_MODEL // HEAD):
            pack_ref[PK_LN_G + k:PK_LN_G + k + 1, :] += g_ln_g[:, k * HEAD:(k + 1) * HEAD]
            pack_ref[PK_LN_B + k:PK_LN_B + k + 1, :] += g_ln_b[:, k * HEAD:(k + 1) * HEAD]

    def rev(i):
        return (N_TILE - 1 - i, 0)

    tile = pl.BlockSpec((ROWS, D_MODEL), rev)
    halo = pl.BlockSpec((HALO, D_POOL), lambda i: (jnp.maximum((N_TILE - 1 - i) * (ROWS // HALO) - 1, 0), 0))
    in_specs = [tile] + [pl.BlockSpec((ROWS, a.shape[1]), rev) for a in saved] + [halo]
    in_specs += [_const_in((None, 8, D_MODEL), l), _const_in((D_MODEL, D_PROJ)), _const_in((D_MODEL, D_MODEL))]
    in_specs += _layer_weight_specs(l) + [_const_in((DEPTH, D_MODEL))]
    args = [dout, *saved, saved[0], mod, w_in, w_out, *small, ln_g]
    stacked = lambda cols: pl.BlockSpec((None, ROWS, cols), lambda i: (l, N_TILE - 1 - i, 0))
    out_shape = [jax.ShapeDtypeStruct((SEQ, D_MODEL), F32), jax.ShapeDtypeStruct((DEPTH, SEQ, D_MODEL), BF16),
                 jax.ShapeDtypeStruct((DEPTH, SEQ, D_MODEL), BF16), jax.ShapeDtypeStruct((DEPTH, SEQ, D_MODEL), BF16),
                 jax.ShapeDtypeStruct((DEPTH, SEQ, D_PROJ), BF16), jax.ShapeDtypeStruct((DEPTH, PK_ROWS, HEAD), F32),
                 jax.ShapeDtypeStruct((DEPTH, 8, D_MODEL), F32)]
    out_specs = [tile, stacked(D_MODEL), stacked(D_MODEL), stacked(D_MODEL), stacked(D_PROJ),
                 _const((None, PK_ROWS, HEAD), l), _const((None, 8, D_MODEL), l)]
    aliases = {}
    if has_loss:
        in_specs.append(_const_in((8, HEAD)))
        args.append(sq)
    else:
        aliases = {len(args) + k: 1 + k for k in range(len(shared))}
        in_specs += [pl.BlockSpec(memory_space=pl.ANY)] * len(shared)
        args += list(shared)
    return pl.pallas_call(
        body, name="bwd_last" if has_loss else "bwd_first", grid=(N_TILE,), in_specs=in_specs, out_specs=out_specs,
        out_shape=out_shape, scratch_shapes=[pltpu.VMEM((HALO, D_POOL), F32)], input_output_aliases=aliases,
        compiler_params=pltpu.CompilerParams(dimension_semantics=("arbitrary",), vmem_limit_bytes=VMEM_LIMIT),
    )(*args)


def _flip(v, f):
    return v + f - 2 * v * f


class _Place:
    def __init__(self):
        x, y, c = lax.axis_index("x"), lax.axis_index("y"), lax.axis_index("c")
        self.x, self.y, self.c = x, y, c
        self.chip = 2 * x + y
        self.dev = 4 * x + 2 * y + c
        self.sibling = (x, y, 1 - c)
        x1, y1 = _flip(x, 1 - c), _flip(y, c)
        x2, y2 = _flip(x, c), _flip(y, 1 - c)
        self.first = (x1, y1, c)
        self.second = (x2, y2, c)
        self.chip_first = 2 * x1 + y1
        self.chip_second = 2 * x2 + y2
        self.chip_far = 2 * (1 - x) + (1 - y)
        self.my_first_coord = jnp.where(c == 0, x, y)

    def first_coord(self, ch):
        return jnp.where(self.c == 0, ch // 2, ch % 2)

    def others(self):
        return [(_flip(self.x, (r >> 2) & 1), _flip(self.y, (r >> 1) & 1), _flip(self.c, r & 1)) for r in range(1, N_DEV)]

    def other_chips(self):
        return [(1 - self.x, self.y), (self.x, 1 - self.y), (1 - self.x, 1 - self.y)]


class _WeightGather:
    CHUNKS = 4
    N_SEMS = 12 * CHUNKS

    def __init__(self, place, win, wout, send, recv):
        self.p, self.win, self.wout, self.send, self.recv = place, win, wout, send, recv
        p = place
        self.plan = [(p.chip, p.first), (p.chip, p.second), (p.chip_first, p.second),
                     (p.chip_first, p.sibling), (p.chip_second, p.sibling), (p.chip_far, p.sibling)]

    def _copies(self, k, q):
        ch, target = self.plan[k]
        n_in, n_out = HALF_IN // self.CHUNKS, HALF_OUT // self.CHUNKS
        rows_in = pl.ds(pl.multiple_of(self.p.c * HALF_IN + q * n_in, n_in), n_in)
        cols_in = pl.ds(pl.multiple_of(ch * W_IN_COLS, 128), W_IN_COLS)
        rows_out = pl.ds(pl.multiple_of(ch * W_OUT_ROWS + self.p.c * HALF_OUT + q * n_out, n_out), n_out)
        r_in = self.win.at[rows_in, cols_in]
        r_out = self.wout.at[rows_out, :]
        s = 2 * (6 * q + k)
        return [pltpu.make_async_remote_copy(r_in, r_in, self.send.at[s], self.recv.at[s],
                                             device_id=target, device_id_type=MESH),
                pltpu.make_async_remote_copy(r_out, r_out, self.send.at[s + 1], self.recv.at[s + 1],
                                             device_id=target, device_id_type=MESH)]

    def _start(self, k, q):
        for cp in self._copies(k, q):
            cp.start()

    def _landed(self, k, q):
        for cp in self._copies(k, q):
            cp.wait_recv()

    def start_first_round(self):
        for q in range(self.CHUNKS):
            self._start(0, q)

    def start_second_round(self, q):
        self._landed(0, q)
        self._start(1, q)
        self._start(2, q)
        self._start(3, q)

    def pass_second_round(self, q):
        self._landed(1, q)
        self._start(4, q)
        self._landed(2, q)
        self._start(5, q)

    def finish(self):
        for q in range(self.CHUNKS):
            for k in (3, 4, 5):
                self._landed(k, q)
        for q in range(self.CHUNKS):
            for k in range(len(self.plan)):
                for cp in self._copies(k, q):
                    cp.wait_send()


def _forward_first(x, c_vec, w_ada, b_ada, w_in, w_out, small, ln_g, ln_b):
    n_saved = len(SAVED_COLS)

    def body(*refs):
        x_ref, c_ref, wada_hbm, bada_ref, win_hbm, wout_hbm = refs[:6]
        small_refs, lng_ref, lnb_ref = refs[6:12], refs[12], refs[13]
        saved_refs = refs[14:14 + n_saved]
        win0, wout0, win1, wout1, mod_out, c_out = refs[14 + n_saved:20 + n_saved]
        (carry_ref, wada_ref, win_ref, wout_ref, win_bf, wout_bf, mod_mine, mod_all, c_all, mod_ref, win_v, wout_v,
         g0_send, g0_recv, g1_send, g1_recv, c_send, c_recv, mod_send, mod_recv, local_sem) = refs[20 + n_saved:]
        i = pl.program_id(0)
        p = _Place()
        gather0 = _WeightGather(p, win0, wout0, g0_send, g0_recv)
        gather1 = _WeightGather(p, win1, wout1, g1_send, g1_recv)

        @pl.when(i == 0)
        def _():
            carry_ref[...] = jnp.zeros_like(carry_ref)
            loads = [pltpu.make_async_copy(win_hbm.at[0], win_ref.at[0], local_sem.at[4]),
                     pltpu.make_async_copy(wout_hbm.at[0], wout_ref.at[0], local_sem.at[5]),
                     pltpu.make_async_copy(win_hbm.at[1], win_ref.at[1], local_sem.at[6]),
                     pltpu.make_async_copy(wout_hbm.at[1], wout_ref.at[1], local_sem.at[7]),
                     pltpu.make_async_copy(wada_hbm, wada_ref, local_sem.at[8])]
            for cp in loads:
                cp.start()

            c_all[pl.ds(p.dev, 1), :] = c_ref[...]
            mine = c_all.at[pl.ds(p.dev, 1), :]
            c_copies = [pltpu.make_async_remote_copy(mine, mine, c_send.at[r], c_recv.at[r], device_id=d, device_id_type=MESH)
                        for r, d in enumerate(p.others())]
            for cp in c_copies:
                cp.start()

            cols = pl.ds(pl.multiple_of(p.chip * W_IN_COLS, 128), W_IN_COLS)
            rows = pl.ds(pl.multiple_of(p.chip * W_OUT_ROWS, W_OUT_ROWS), W_OUT_ROWS)
            own = [pltpu.make_async_copy(win_bf.at[0], win0.at[:, cols], local_sem.at[0]),
                   pltpu.make_async_copy(wout_bf.at[0], wout0.at[rows, :], local_sem.at[1]),
                   pltpu.make_async_copy(win_bf.at[1], win1.at[:, cols], local_sem.at[2]),
                   pltpu.make_async_copy(wout_bf.at[1], wout1.at[rows, :], local_sem.at[3])]
            for l in range(DEPTH):
                loads[2 * l].wait()
                win_bf[l] = win_ref[l].astype(BF16)
                own[2 * l].start()
                loads[2 * l + 1].wait()
                wout_bf[l] = wout_ref[l].astype(BF16)
                own[2 * l + 1].start()
                if l == 0:
                    own[0].wait()
                    own[1].wait()
                    gather0.start_first_round()
            for cp in c_copies:
                cp.wait()
            loads[4].wait()

            cv = c_all[...]
            c_out[...] = cv
            silu_c = (cv * _sigmoid(cv)).astype(BF16)
            for l in range(DEPTH):
                mod_mine[l] = _dot(silu_c, wada_ref[l].astype(BF16))
            mod_all[p.chip] = mod_mine[...]
            m_copies = [pltpu.make_async_remote_copy(mod_mine, mod_all.at[p.chip], mod_send.at[k], mod_recv.at[k],
                                                     device_id=(px, py, p.c), device_id_type=MESH)
                        for k, (px, py) in enumerate(p.other_chips())]
            for cp in m_copies:
                cp.start()
            for q in range(gather0.CHUNKS):
                gather0.start_second_round(q)
            own[2].wait()
            own[3].wait()
            gather1.start_first_round()
            for cp in m_copies:
                cp.wait()
            mod_ref[...] = jnp.zeros_like(mod_ref)
            for l in range(DEPTH):
                full = jnp.concatenate([mod_all[ch, l, pl.ds(p.dev, 1), :] for ch in range(N_CHIP)], axis=1) + bada_ref[l:l + 1, :]
                for k in range(3):
                    mod_ref[l, k:k + 1, :] = full[:, k * D_MODEL:(k + 1) * D_MODEL]
            mod_out[...] = mod_ref[...]
            for q in range(gather0.CHUNKS):
                gather0.pass_second_round(q)
            gather0.finish()
            fetch = [pltpu.make_async_copy(win0, win_v, local_sem.at[9]), pltpu.make_async_copy(wout0, wout_v, local_sem.at[10])]
            for cp in fetch:
                cp.start()
            for cp in fetch:
                cp.wait()

        for q in range(_WeightGather.CHUNKS):
            @pl.when(i == GATHER_SECOND_ROUND_STEP + q)
            def _(q=q):
                gather1.start_second_round(q)

            @pl.when(i == GATHER_PASS_STEP + q)
            def _(q=q):
                gather1.pass_second_round(q)

        _forward_tile(0, i, x_ref, mod_ref.at[0], win_v, wout_v, small_refs, lng_ref, lnb_ref, carry_ref, saved_refs)

        @pl.when(i == N_TILE - 1)
        def _():
            gather1.finish()

    hbm = pl.BlockSpec(memory_space=pl.ANY)
    tile3 = pl.BlockSpec((None, ROWS, D_MODEL), lambda i: (0, i, 0))
    in_specs = [tile3, _const_in((1, D_MODEL)), hbm, _const_in((DEPTH, 3 * D_MODEL)), hbm, hbm]
    in_specs += _layer_weight_specs(0) + [_const_in((DEPTH, D_MODEL)), _const_in((DEPTH, D_MODEL))]
    out_shape, out_specs = _saved_outputs()
    w_in_shape = jax.ShapeDtypeStruct((D_MODEL, D_PROJ), BF16)
    w_out_shape = jax.ShapeDtypeStruct((D_MODEL, D_MODEL), BF16)
    out_shape += [w_in_shape, w_out_shape, w_in_shape, w_out_shape,
                  jax.ShapeDtypeStruct((DEPTH, 8, D_MODEL), F32), jax.ShapeDtypeStruct((N_DEV, D_MODEL), F32)]
    out_specs += [hbm, hbm, hbm, hbm, _const((DEPTH, 8, D_MODEL)), _const((N_DEV, D_MODEL))]
    gather_sems = [pltpu.SemaphoreType.DMA((_WeightGather.N_SEMS,))] * 4
    scratch = [
        pltpu.VMEM((HALO, D_POOL), F32),
        pltpu.VMEM(w_ada.shape, F32), pltpu.VMEM(w_in.shape, F32), pltpu.VMEM(w_out.shape, F32),
        pltpu.VMEM((DEPTH, D_MODEL, W_IN_COLS), BF16), pltpu.VMEM((DEPTH, W_OUT_ROWS, D_MODEL), BF16),
        pltpu.VMEM((DEPTH, N_DEV, W_ADA_COLS), F32), pltpu.VMEM((N_CHIP, DEPTH, N_DEV, W_ADA_COLS), F32),
        pltpu.VMEM((N_DEV, D_MODEL), F32), pltpu.VMEM((DEPTH, 8, D_MODEL), F32),
        pltpu.VMEM((D_MODEL, D_PROJ), BF16), pltpu.VMEM((D_MODEL, D_MODEL), BF16),
    ] + gather_sems + [
        pltpu.SemaphoreType.DMA((7,)), pltpu.SemaphoreType.DMA((7,)),
        pltpu.SemaphoreType.DMA((3,)), pltpu.SemaphoreType.DMA((3,)),
        pltpu.SemaphoreType.DMA((11,)),
    ]
    return pl.pallas_call(
        body, name="fwd_first", grid=(N_TILE,), in_specs=in_specs, out_specs=out_specs, out_shape=out_shape,
        scratch_shapes=scratch,
        compiler_params=pltpu.CompilerParams(dimension_semantics=("arbitrary",), vmem_limit_bytes=VMEM_LIMIT),
    )(x, c_vec, w_ada, b_ada, w_in, w_out, *small, ln_g, ln_b)


IN_STEPS = W_IN_COLS // HEAD
OUT_STEPS = 4
OUT_COLS = D_MODEL // OUT_STEPS
OUT_FIRST = 2
ITEMS = ([("out", k) for k in range(OUT_FIRST)] + [("in", k) for k in range(IN_STEPS)]
         + [("out", k) for k in range(OUT_FIRST, OUT_STEPS)])
N_ITEMS = len(ITEMS)
N_STEPS = DEPTH * N_ITEMS
DELAY_SUM, DELAY_SECOND, DELAY_FINAL = 1, 3, 5
SMALL_SCATTER_STEP, SMALL_GATHER_STEP, SMALL_PASS_STEP, SMALL_FINISH_STEP = 1, 3, 5, 7


def _wgrad_reduce(h, dproj, cat, dy, pack, dmod):
    def body(*refs):
        h_ref, dp_refs, cat_ref, dy_ref, pack_ref, dmod_ref = refs[0], refs[1:5], refs[5], refs[6], refs[7], refs[8]
        fin_in, fin_out, pack_out, dmod_out = refs[9:13]
        scratch = refs[13:]
        (mine_in, send_in, sib_in, st_in, r1_in, r2_in, f_in,
         mine_out, send_out, sib_out, st_out, r1_out, r2_out, f_out,
         d2d_s, d2d_r, r1_s, r1_r, r2_s, r2_r, fin_l, fin_s, fin_r) = scratch[:23]
        p = _Place()
        c = p.c
        i = pl.program_id(0)
        my_rows = pl.ds(pl.multiple_of(c * HALF_IN, HALF_IN), HALF_IN)

        def layer_of(j):
            return DEPTH - 1 - j // N_ITEMS

        def bufs(j):
            kind, k = ITEMS[j % N_ITEMS]
            if kind == "in":
                return [r.at[k] for r in (mine_in, send_in, sib_in, st_in, r1_in, r2_in, f_in)]
            return [r.at[k] for r in (mine_out, send_out, sib_out, st_out, r1_out, r2_out, f_out)]

        def piece(j, ref, ch):
            if ITEMS[j % N_ITEMS][0] == "in":
                return ref.at[:, ch * HEAD:(ch + 1) * HEAD]
            return ref.at[ch]

        def slot(ch):
            return jnp.where(c == 0, ch % 2, ch // 2)

        def to_sibling(j):
            _, send, sib, _, _, _, _ = bufs(j)
            return pltpu.make_async_remote_copy(send, sib, d2d_s.at[j], d2d_r.at[j], device_id=p.sibling, device_id_type=MESH)

        def first_round(j, ch):
            _, _, _, st, r1, _, _ = bufs(j)
            k = slot(ch)
            return pltpu.make_async_remote_copy(st.at[k], r1.at[k], r1_s.at[2 * j + k], r1_r.at[2 * j + k],
                                                device_id=p.first, device_id_type=MESH)

        def second_round(j):
            _, _, _, st, _, r2, _ = bufs(j)
            return pltpu.make_async_remote_copy(st.at[2], r2, r2_s.at[j], r2_r.at[j], device_id=p.second, device_id_type=MESH)

        def finals(j):
            f = bufs(j)[6]
            kind, k = ITEMS[j % N_ITEMS]
            if kind == "in":
                dst = fin_in.at[layer_of(j), my_rows, k * HEAD:(k + 1) * HEAD]
            else:
                dst = fin_out.at[layer_of(j), c, :, k * OUT_COLS:(k + 1) * OUT_COLS]
            return [pltpu.make_async_copy(f, dst, fin_l.at[j]),
                    pltpu.make_async_remote_copy(f, dst, fin_s.at[j], fin_r.at[j], device_id=p.sibling, device_id_type=MESH)]

        def stage_sum(j):
            mine, _, sib, st, _, _, _ = bufs(j)
            to_sibling(j).wait_recv()
            mine[...] = mine[...] + sib[...]
            for ch in range(N_CHIP):
                @pl.when(p.first_coord(ch) != p.my_first_coord)
                def _(ch=ch):
                    st[slot(ch)] = piece(j, mine, ch)[...].astype(BF16)
                    first_round(j, ch).start()

        def stage_second(j):
            mine, _, _, st, r1, _, _ = bufs(j)
            for ch in range(N_CHIP):
                @pl.when(p.first_coord(ch) == p.my_first_coord)
                def _(ch=ch):
                    first_round(j, ch).wait_recv()
                    part = piece(j, mine, ch)
                    total = part[...] + r1[slot(ch)].astype(F32)
                    part[...] = total

                    @pl.when(ch != p.chip)
                    def _():
                        st[2] = total.astype(BF16)
                        second_round(j).start()

        def stage_final(j):
            mine, _, _, _, _, r2, f = bufs(j)
            second_round(j).wait_recv()
            for ch in range(N_CHIP):
                @pl.when(ch == p.chip)
                def _(ch=ch):
                    f[...] = piece(j, mine, ch)[...] + r2[...].astype(F32)
            for cp in finals(j):
                cp.start()

        def drain(j):
            to_sibling(j).wait_send()
            for ch in range(N_CHIP):
                @pl.when(p.first_coord(ch) != p.my_first_coord)
                def _(ch=ch):
                    first_round(j, ch).wait_send()

                @pl.when(jnp.logical_and(p.first_coord(ch) == p.my_first_coord, ch != p.chip))
                def _():
                    second_round(j).wait_send()
            for cp in finals(j):
                cp.wait()

        dev = p.dev
        devices = p.others()

        def half(core):
            return pl.ds(pl.multiple_of(core * PK_HALF, 16), PK_HALF)

        def finished(core, ch):
            return pl.ds(pl.multiple_of(core * PK_HALF + ch * PK_PIECE, 16), PK_PIECE)

        def small_exchange(l, first_step, bufs_l):
            (pk_mine, pk_sib, pk_st, pk_rs, pk_fin, pk_all, dm_st, dm_all, pk_sem, rs_s, rs_r, ag_s, ag_r, dm_s, dm_r) = bufs_l

            def pk_load():
                return pltpu.make_async_copy(pack_ref.at[l, half(c)], pk_mine, pk_sem.at[0])

            def pk_give():
                return pltpu.make_async_remote_copy(pack_ref.at[l, half(1 - c)], pk_sib, pk_sem.at[1], pk_sem.at[2],
                                                    device_id=p.sibling, device_id_type=MESH)

            def pk_scatter(ch):
                return pltpu.make_async_remote_copy(pk_st.at[ch * PK_PIECE:(ch + 1) * PK_PIECE], pk_rs.at[p.chip],
                                                    rs_s.at[ch], rs_r.at[p.chip], device_id=(ch // 2, ch % 2, c),
                                                    device_id_type=MESH)

            def pk_spread(ch):
                return pltpu.make_async_remote_copy(pk_fin, pk_all.at[finished(c, p.chip)], ag_s.at[ch], ag_r.at[p.chip],
                                                    device_id=(ch // 2, ch % 2, c), device_id_type=MESH)

            def pk_pass():
                return pltpu.make_async_remote_copy(pk_all.at[half(c)], pk_all.at[half(c)], pk_sem.at[3], pk_sem.at[4],
                                                    device_id=p.sibling, device_id_type=MESH)

            def dm_copy(r):
                return pltpu.make_async_remote_copy(dm_st, dm_all.at[:, pl.ds(dev, 1), :], dm_s.at[r], dm_r.at[r],
                                                    device_id=devices[r], device_id_type=MESH)

            def results():
                return [pltpu.make_async_copy(pk_all, pack_out.at[l], pk_sem.at[0]),
                        pltpu.make_async_copy(dm_all, dmod_out.at[l], pk_sem.at[5])]

            @pl.when(i == first_step)
            def _():
                pk_load().start()
                pk_give().start()
                for k in range(3):
                    for r in range(D_MODEL // HEAD):
                        dm_st[8 * k + r] = dmod_ref[l, k:k + 1, r * HEAD:(r + 1) * HEAD]
                dm_st[DM_LOSS] = dmod_ref[l, 3:4, 0:HEAD]
                dm_all[:, pl.ds(dev, 1), :] = dm_st[...]
                for r in range(N_DEV - 1):
                    dm_copy(r).start()

            @pl.when(i == first_step + SMALL_SCATTER_STEP)
            def _():
                pk_load().wait()
                pk_give().wait()
                total = pk_mine[...] + pk_sib[...]
                pk_mine[...] = total
                pk_st[...] = total.astype(BF16)
                for ch in range(N_CHIP):
                    @pl.when(ch != p.chip)
                    def _(ch=ch):
                        pk_scatter(ch).start()

            @pl.when(i == first_step + SMALL_GATHER_STEP)
            def _():
                for ch in range(N_CHIP):
                    @pl.when(ch != p.chip)
                    def _(ch=ch):
                        pltpu.make_async_remote_copy(pk_fin, pk_rs.at[ch], rs_s.at[ch], rs_r.at[ch],
                                                     device_id=p.sibling, device_id_type=MESH).wait_recv()
                for me in range(N_CHIP):
                    @pl.when(me == p.chip)
                    def _(me=me):
                        total = None
                        for ch in range(N_CHIP):
                            part = pk_mine[me * PK_PIECE:(me + 1) * PK_PIECE] if ch == me else pk_rs[ch].astype(F32)
                            total = part if total is None else total + part
                        pk_fin[...] = total.astype(BF16)
                        pk_all[finished(c, me)] = total.astype(BF16)
                for ch in range(N_CHIP):
                    @pl.when(ch != p.chip)
                    def _(ch=ch):
                        pk_spread(ch).start()

            @pl.when(i == first_step + SMALL_PASS_STEP)
            def _():
                for ch in range(N_CHIP):
                    @pl.when(ch != p.chip)
                    def _(ch=ch):
                        pltpu.make_async_remote_copy(pk_fin, pk_all.at[finished(c, ch)], ag_s.at[ch], ag_r.at[ch],
                                                     device_id=p.sibling, device_id_type=MESH).wait_recv()
                pk_pass().start()

            @pl.when(i == first_step + SMALL_FINISH_STEP)
            def _():
                pk_pass().wait()
                for ch in range(N_CHIP):
                    @pl.when(ch != p.chip)
                    def _(ch=ch):
                        pk_scatter(ch).wait_send()
                        pk_spread(ch).wait_send()
                for r in range(N_DEV - 1):
                    dm_copy(r).wait()
                for cp in results():
                    cp.start()
                for cp in results():
                    cp.wait()

        n_small = 15
        for l in range(DEPTH):
            small_exchange(l, (DEPTH - 1 - l) * N_ITEMS, scratch[23 + n_small * l:23 + n_small * (l + 1)])

        for step in range(N_ITEMS, N_STEPS):
            @pl.when(i == step)
            def _(step=step):
                drain(step - N_ITEMS)

        ii = jnp.where(i < N_ITEMS, i, i - N_ITEMS)
        in_step = jnp.logical_and(ii >= OUT_FIRST, ii < OUT_FIRST + IN_STEPS)

        @pl.when(in_step)
        def _():
            k = ii - OUT_FIRST
            rhs = jnp.concatenate([r[...] for r in dp_refs], axis=1)
            res = _dot(h_ref[...], rhs, TN)

            @pl.when(c == 0)
            def _():
                mine_in[k] = res[:HALF_IN]
                send_in[k] = res[HALF_IN:]

            @pl.when(c == 1)
            def _():
                mine_in[k] = res[HALF_IN:]
                send_in[k] = res[:HALF_IN]

        @pl.when(jnp.logical_not(in_step))
        def _():
            k = jnp.where(ii < OUT_FIRST, ii, ii - IN_STEPS)
            res = _dot(cat_ref[...], dy_ref[...], TN)

            @pl.when(c == 0)
            def _():
                for ch in range(N_CHIP):
                    mine_out[k, ch] = res[ch * W_OUT_ROWS:ch * W_OUT_ROWS + HALF_OUT]
                    send_out[k, ch] = res[ch * W_OUT_ROWS + HALF_OUT:(ch + 1) * W_OUT_ROWS]

            @pl.when(c == 1)
            def _():
                for ch in range(N_CHIP):
                    mine_out[k, ch] = res[ch * W_OUT_ROWS + HALF_OUT:(ch + 1) * W_OUT_ROWS]
                    send_out[k, ch] = res[ch * W_OUT_ROWS:ch * W_OUT_ROWS + HALF_OUT]

        stages = ((0, lambda j: to_sibling(j).start()), (DELAY_SUM, stage_sum), (DELAY_SECOND, stage_second),
                  (DELAY_FINAL, stage_final))
        for step in range(N_STEPS):
            @pl.when(i == step)
            def _(step=step):
                for delay, stage in stages:
                    if step - delay >= 0:
                        stage(step - delay)

        @pl.when(i == N_STEPS - 1)
        def _():
            for step in range(N_STEPS, N_STEPS + DELAY_FINAL):
                for delay, stage in stages:
                    if 0 <= step - delay < N_STEPS:
                        stage(step - delay)
            for j in range(N_STEPS - N_ITEMS, N_STEPS):
                drain(j)

    hbm = pl.BlockSpec(memory_space=pl.ANY)

    def layer(i):
        return jnp.where(i < N_ITEMS, DEPTH - 1, 0)

    def item(i):
        return jnp.where(i < N_ITEMS, i, i - N_ITEMS)

    def whole(i):
        return (layer(i), 0, 0)

    def dproj_piece(ch):
        return pl.BlockSpec((None, SEQ, HEAD),
                            lambda i: (layer(i), 0, ch * IN_STEPS + jnp.clip(item(i) - OUT_FIRST, 0, IN_STEPS - 1)))

    def dy_quarter(i):
        return (layer(i), 0, jnp.where(item(i) < OUT_FIRST, item(i), jnp.maximum(item(i) - IN_STEPS, OUT_FIRST)))

    operand = pl.BlockSpec((None, SEQ, D_MODEL), whole)
    in_specs = [operand] + [dproj_piece(ch) for ch in range(N_CHIP)]
    in_specs += [operand, pl.BlockSpec((None, SEQ, OUT_COLS), dy_quarter), hbm, _const_in((DEPTH, 8, D_MODEL))]
    args = [h, dproj, dproj, dproj, dproj, cat, dy, pack, dmod]
    out_shape = [jax.ShapeDtypeStruct((DEPTH, D_MODEL, W_IN_COLS), F32), jax.ShapeDtypeStruct((DEPTH, 2, HALF_OUT, D_MODEL), F32),
                 jax.ShapeDtypeStruct((DEPTH, PK_ROWS, HEAD), BF16), jax.ShapeDtypeStruct((DEPTH, DM_ROWS, N_DEV, HEAD), F32)]
    out_specs = [hbm, hbm, hbm, hbm]
    in_item = lambda *lead: pltpu.VMEM(lead + (HALF_IN, HEAD), BF16)
    out_item = lambda *lead: pltpu.VMEM(lead + (HALF_OUT, OUT_COLS), BF16)
    scratch = [
        pltpu.VMEM((IN_STEPS, HALF_IN, N_CHIP * HEAD), F32), pltpu.VMEM((IN_STEPS, HALF_IN, N_CHIP * HEAD), F32),
        pltpu.VMEM((IN_STEPS, HALF_IN, N_CHIP * HEAD), F32), in_item(IN_STEPS, 3), in_item(IN_STEPS, 2), in_item(IN_STEPS),
        pltpu.VMEM((IN_STEPS, HALF_IN, HEAD), F32),
        pltpu.VMEM((OUT_STEPS, N_CHIP, HALF_OUT, OUT_COLS), F32), pltpu.VMEM((OUT_STEPS, N_CHIP, HALF_OUT, OUT_COLS), F32),
        pltpu.VMEM((OUT_STEPS, N_CHIP, HALF_OUT, OUT_COLS), F32), out_item(OUT_STEPS, 3), out_item(OUT_STEPS, 2),
        out_item(OUT_STEPS), pltpu.VMEM((OUT_STEPS, HALF_OUT, OUT_COLS), F32),
        pltpu.SemaphoreType.DMA((N_STEPS,)), pltpu.SemaphoreType.DMA((N_STEPS,)),
        pltpu.SemaphoreType.DMA((2 * N_STEPS,)), pltpu.SemaphoreType.DMA((2 * N_STEPS,)),
        pltpu.SemaphoreType.DMA((N_STEPS,)), pltpu.SemaphoreType.DMA((N_STEPS,)),
        pltpu.SemaphoreType.DMA((N_STEPS,)), pltpu.SemaphoreType.DMA((N_STEPS,)), pltpu.SemaphoreType.DMA((N_STEPS,)),
    ]
    for _ in range(DEPTH):
        scratch += [
            pltpu.VMEM((PK_HALF, HEAD), F32), pltpu.VMEM((PK_HALF, HEAD), F32), pltpu.VMEM((PK_HALF, HEAD), BF16),
            pltpu.VMEM((N_CHIP, PK_PIECE, HEAD), BF16), pltpu.VMEM((PK_PIECE, HEAD), BF16), pltpu.VMEM((PK_ROWS, HEAD), BF16),
            pltpu.VMEM((DM_ROWS, 1, HEAD), F32), pltpu.VMEM((DM_ROWS, N_DEV, HEAD), F32),
            pltpu.SemaphoreType.DMA((6,)),
            pltpu.SemaphoreType.DMA((N_CHIP,)), pltpu.SemaphoreType.DMA((N_CHIP,)),
            pltpu.SemaphoreType.DMA((N_CHIP,)), pltpu.SemaphoreType.DMA((N_CHIP,)),
            pltpu.SemaphoreType.DMA((N_DEV - 1,)), pltpu.SemaphoreType.DMA((N_DEV - 1,)),
        ]
    return pl.pallas_call(
        body, name="wgrad", grid=(N_STEPS,), in_specs=in_specs, out_specs=out_specs, out_shape=out_shape,
        scratch_shapes=scratch,
        compiler_params=pltpu.CompilerParams(dimension_semantics=("arbitrary",), vmem_limit_bytes=VMEM_LIMIT),
    )(*args)


def _adamw(w, g, m, v):
    m = ADAM_B1 * m + (1.0 - ADAM_B1) * g
    v = ADAM_B2 * v + (1.0 - ADAM_B2) * (g * g)
    m_hat = m / (1.0 - ADAM_B1 ** ADAM_STEP)
    v_hat = v / (1.0 - ADAM_B2 ** ADAM_STEP)
    delta = -ADAM_LR * (m_hat / (jnp.sqrt(v_hat) + ADAM_EPS) + ADAM_WD * w)
    return delta, m, v


def _adam_sharded(c_all, dmods, ada, w_in_set, w_out_set):
    rows = D_MODEL // ADAM_PARTS

    def body(c_ref, dm_ref, wa_ref, ma_ref, va_ref, wi_ref, gi_ref, mi_ref, vi_ref, wo_ref, go_ref, mo_ref, vo_ref,
             ga_out, da_out, ma_out, va_out, di_out, mi_out, vi_out, do_out, mo_out, vo_out):
        l = pl.program_id(0)
        chip = 2 * lax.axis_index("x") + lax.axis_index("y")
        cv = c_ref[...]
        silu_c = (cv * _sigmoid(cv)).astype(BF16).astype(F32)
        pieces = []
        for k in range(W_ADA_COLS // HEAD):
            dk = dm_ref[l, (W_ADA_COLS // HEAD) * chip + k].astype(BF16).astype(F32)
            pieces.append(_dot_exact(silu_c, dk, TN))
        g = jnp.concatenate(pieces, axis=1)
        ga_out[...] = g
        da_out[...], ma_out[...], va_out[...] = _adamw(wa_ref[...], g, ma_ref[...], va_ref[...])
        di_out[...], mi_out[...], vi_out[...] = _adamw(wi_ref[...], gi_ref[...], mi_ref[...], vi_ref[...])
        do_out[...], mo_out[...], vo_out[...] = _adamw(wo_ref[...], go_ref[...], mo_ref[...], vo_ref[...])

    def blk(r, cols):
        return pl.BlockSpec((None, r, cols), lambda l, i: (l, i, 0))

    b_ada, b_in, b_out = blk(rows, W_ADA_COLS), blk(rows, W_IN_COLS), blk(W_OUT_ROWS // ADAM_PARTS, D_MODEL)
    shapes = [jax.ShapeDtypeStruct(a[0].shape, F32) for a in (ada, w_in_set, w_out_set)]
    return pl.pallas_call(
        body, name="adam_sharded", grid=(DEPTH, ADAM_PARTS),
        in_specs=[pl.BlockSpec((N_DEV, rows), lambda l, i: (0, i)), _const_in((DEPTH, DM_ROWS, N_DEV, HEAD))]
        + [b_ada] * 3 + [b_in] * 4 + [b_out] * 4,
        out_specs=[b_ada] * 4 + [b_in] * 3 + [b_out] * 3,
        out_shape=[shapes[0]] * 4 + [shapes[1]] * 3 + [shapes[2]] * 3,
        compiler_params=pltpu.CompilerParams(dimension_semantics=("arbitrary", "arbitrary"), vmem_limit_bytes=VMEM_LIMIT),
    )(c_all, dmods, *ada, *w_in_set, *w_out_set)


def _adam_small(packs, dmods, weights, ms, vs):
    n = len(weights)

    def body(*refs):
        dm_refs = refs[1]
        b = 2
        w_refs, m_refs, v_refs = refs[b:b + n], refs[b + n:b + 2 * n], refs[b + 2 * n:b + 3 * n]
        outs = refs[b + 3 * n:b + 3 * n + 4 * n + 1]
        pack_refs = refs[-1]
        pack_refs[...] = refs[0][...].astype(F32)
        g_refs, d_refs, nm_refs, nv_refs = outs[0:n], outs[n:2 * n], outs[2 * n:3 * n], outs[3 * n:4 * n]
        squares = dm_refs[DEPTH - 1, DM_LOSS]
        total = squares[0:1, 0:1]
        for d in range(1, N_DEV):
            total = total + squares[d:d + 1, 0:1]
        outs[4 * n][...] = total * (0.5 / D_MODEL)

        def lanes(l, row0, count):
            return jnp.concatenate([pack_refs.at[l][row0 + k:row0 + k + 1, :] for k in range(count)], axis=1)

        def update(idx, at, g):
            g_refs[idx][at] = g
            d_refs[idx][at], nm_refs[idx][at], nv_refs[idx][at] = _adamw(w_refs[idx][at], g, m_refs[idx][at], v_refs[idx][at])

        for l in range(DEPTH):
            row = (slice(l, l + 1), slice(None))
            g_b = None
            for d in range(N_DEV):
                part = dm_refs.at[l][0:DM_LOSS, d, :]
                g_b = part if g_b is None else g_b + part
            update(0, row, jnp.concatenate([g_b[k:k + 1, :] for k in range(DM_LOSS)], axis=1))
            for g in range(N_HEAD):
                update(1, (l, g), pack_refs.at[l][PK_W_POOL + g * HEAD:PK_W_POOL + (g + 1) * HEAD, :])
                update(5, (l, g), pack_refs.at[l][PK_W_SGU + g * HEAD:PK_W_SGU + (g + 1) * HEAD, :])
            update(2, row, lanes(l, PK_POOL_SCALE, N_HEAD))
            update(3, (l,), pack_refs.at[l][PK_SGU_LN_G:PK_SGU_LN_G + N_HEAD, :])
            update(4, (l,), pack_refs.at[l][PK_SGU_LN_B:PK_SGU_LN_B + N_HEAD, :])
            update(6, (l,), pack_refs.at[l][PK_B_SGU:PK_B_SGU + N_HEAD, :])
            update(7, row, lanes(l, PK_LN_G, D_MODEL // HEAD))
            update(8, row, lanes(l, PK_LN_B, D_MODEL // HEAD))

    vmem = pl.BlockSpec(memory_space=pltpu.VMEM)
    shapes = [jax.ShapeDtypeStruct(w.shape, F32) for w in weights]
    return pl.pallas_call(
        body, name="adam_small", in_specs=[vmem] * (2 + 3 * n), out_specs=[vmem] * (4 * n + 1),
        out_shape=shapes * 4 + [jax.ShapeDtypeStruct((1, 1), F32)],
        scratch_shapes=[pltpu.VMEM(packs.shape, F32)],
        compiler_params=pltpu.CompilerParams(vmem_limit_bytes=VMEM_LIMIT),
    )(packs, dmods, *weights, *ms, *vs)


def kernel(x, c, w_ada, b_ada, w_in, w_pool, pool_scale, sgu_ln_g, sgu_ln_b, w_sgu, b_sgu, w_out, ln_g, ln_b, loss_target, m_w_ada, m_b_ada, m_w_in, m_w_pool, m_pool_scale, m_sgu_ln_g, m_sgu_ln_b, m_w_sgu, m_b_sgu, m_w_out, m_ln_g, m_ln_b, v_w_ada, v_b_ada, v_w_in, v_w_pool, v_pool_scale, v_sgu_ln_g, v_sgu_ln_b, v_w_sgu, v_b_sgu, v_w_out, v_ln_g, v_ln_b):
    small = (w_pool, pool_scale, sgu_ln_g, sgu_ln_b, w_sgu, b_sgu)
    *saved0, w_in0, w_out0, w_in1, w_out1, mod, c_all = _forward_first(x, c, w_ada, b_ada, w_in, w_out, small, ln_g, ln_b)
    *saved1, dout, sq = _forward_last(saved0[3], mod, w_in1, w_out1, small, ln_g, ln_b, loss_target)

    dx1, *shared = _backward_layer(1, dout, saved1, mod, w_in1, w_out1, small, ln_g, sq=sq)
    dx0, h, cat, dy, dproj, pack, dmod = _backward_layer(0, dx1, saved0, mod, w_in0, w_out0, small, ln_g, shared=shared)
    g_in, g_out, pack, dmods = _wgrad_reduce(h, dproj, cat, dy, pack, dmod)

    g_out = g_out.reshape(DEPTH, W_OUT_ROWS, D_MODEL)
    big = _adam_sharded(c_all, dmods, (w_ada, m_w_ada, v_w_ada), (w_in, g_in, m_w_in, v_w_in), (w_out, g_out, m_w_out, v_w_out))
    ada, win, wout = big[0:4], (g_in, *big[4:7]), (g_out, *big[7:10])
    small_w = (b_ada, w_pool, pool_scale, sgu_ln_g, sgu_ln_b, w_sgu, b_sgu, ln_g, ln_b)
    small_m = (m_b_ada, m_w_pool, m_pool_scale, m_sgu_ln_g, m_sgu_ln_b, m_w_sgu, m_b_sgu, m_ln_g, m_ln_b)
    small_v = (v_b_ada, v_w_pool, v_pool_scale, v_sgu_ln_g, v_sgu_ln_b, v_w_sgu, v_b_sgu, v_ln_g, v_ln_b)
    res = _adam_small(pack, dmods, small_w, small_m, small_v)
    n = len(small_w)
    loss = res[4 * n].reshape(())

    def ordered(k):
        s = res[k * n:(k + 1) * n]
        return (ada[k], s[0], win[k], s[1], s[2], s[3], s[4], s[5], s[6], wout[k], s[7], s[8])

    return (loss, dx0[None], *ordered(0), *ordered(1), *ordered(2), *ordered(3))
```

```python
import jax
import jax.numpy as jnp
from jax import lax
from jax.experimental import pallas as pl
from jax.experimental.pallas import tpu as pltpu

F32 = jnp.float32
BF16 = jnp.bfloat16
MESH = pl.DeviceIdType.MESH

N_DEV = 8
N_CHIP = 4
DEPTH = 2
SEQ = 2048
D_MODEL = 1024
D_POOL = 512
D_PROJ = 2560
HEAD = 128
N_HEAD = 4
ROWS = 256
N_TILE = SEQ // ROWS
HALO = 16
W_IN_COLS = D_PROJ // N_CHIP
W_OUT_ROWS = D_MODEL // N_CHIP
W_ADA_COLS = 3 * D_MODEL // N_CHIP
HALF_IN = D_MODEL // 2
HALF_OUT = W_OUT_ROWS // 2
DEEPNORM_ALPHA = (2.0 * DEPTH) ** 0.25
LN_EPS = 1e-5
INV_SQRT2 = 0.7071067811865476
INV_SQRT_2PI = 0.3989422804014327

ADAM_LR = 0.001
ADAM_B1 = 0.9
ADAM_B2 = 0.999
ADAM_EPS = 1e-08
ADAM_WD = 0.01
ADAM_STEP = 10
ADAM_PARTS = 2

PK_W_POOL = 0
PK_W_SGU = 512
PK_POOL_SCALE = 1024
PK_SGU_LN_G = 1032
PK_SGU_LN_B = 1040
PK_B_SGU = 1048
PK_LN_G = 1056
PK_LN_B = 1064
PK_ROWS = 1152
PK_HALF = PK_ROWS // 2
PK_PIECE = PK_HALF // N_CHIP
DM_LOSS = 3 * D_MODEL // HEAD
DM_ROWS = DM_LOSS + 1

VMEM_LIMIT = 56 * 1024 * 1024

GATHER_SECOND_ROUND_STEP = 0
GATHER_PASS_STEP = 2

NN = (((1,), (0,)), ((), ()))
NT = (((1,), (1,)), ((), ()))
TN = (((0,), (0,)), ((), ()))


def _dot(a, b, dims=NN):
    return lax.dot_general(a, b, dims, preferred_element_type=F32)


def _dot_exact(a, b, dims=NN):
    return lax.dot_general(a, b, dims, preferred_element_type=F32, precision=lax.Precision.HIGHEST)


def _layer_norm(v):
    mu = jnp.mean(v, axis=-1, keepdims=True)
    d = v - mu
    var = jnp.mean(d * d, axis=-1, keepdims=True)
    rstd = lax.rsqrt(var + LN_EPS)
    return d * rstd, rstd


def _layer_norm_bwd(dvhat, vhat, rstd):
    m1 = jnp.mean(dvhat, axis=-1, keepdims=True)
    m2 = jnp.mean(dvhat * vhat, axis=-1, keepdims=True)
    return rstd * (dvhat - m1 - vhat * m2)


def _sigmoid(v):
    return 1.0 / (1.0 + jnp.exp(-v))


def _gelu_parts(v):
    phi = 0.5 * (1.0 + lax.erf(v * INV_SQRT2))
    pdf = INV_SQRT_2PI * jnp.exp(-0.5 * v * v)
    return phi, pdf


def _sum_rows(v):
    return jnp.sum(v, axis=0, keepdims=True)


def _window_sums(ext, toward_later):
    n = ext.shape[0]

    def shifted(v, k):
        return pltpu.roll(v, (n - k) if toward_later else k, 0)

    s2 = ext + shifted(ext, 1)
    r4 = s2[:, HEAD:]
    s4 = r4 + shifted(r4, 2)
    r8 = s4[:, HEAD:]
    s8 = r8 + shifted(r8, 4)
    r16 = s8[:, HEAD:]
    s16 = r16 + shifted(r16, 8)
    return jnp.concatenate([s2[:, :HEAD], s4[:, :HEAD], s8[:, :HEAD], s16], axis=1)


def _window_counts(row0):
    t1 = row0 + 1 + lax.broadcasted_iota(jnp.int32, (ROWS, D_POOL), 0)
    lane = lax.broadcasted_iota(jnp.int32, (ROWS, D_POOL), 1)
    width = jnp.where(lane < HEAD, 2, jnp.where(lane < 2 * HEAD, 4, jnp.where(lane < 3 * HEAD, 8, 16)))
    return jnp.minimum(t1, width).astype(F32)


def _causal_mask():
    r = lax.broadcasted_iota(jnp.int32, (HEAD, HEAD), 0)
    s = lax.broadcasted_iota(jnp.int32, (HEAD, HEAD), 1)
    return r >= s


def _chunks_to_lanes(v):
    return jnp.concatenate([v[n * HEAD:(n + 1) * HEAD] for n in range(ROWS // HEAD)], axis=1)


def _lanes_to_chunks(v):
    return jnp.concatenate([v[:, n * HEAD:(n + 1) * HEAD] for n in range(ROWS // HEAD)], axis=0)


def _pack_stats(rstd_x, rstd_z, rstd_v):
    lane = lax.broadcasted_iota(jnp.int32, (ROWS, HEAD), 1)
    packed = rstd_x
    for k, r in enumerate([rstd_z] + list(rstd_v)):
        packed = jnp.where(lane < 16 * (k + 1), packed, r)
    return packed


def _unpack_stats(stats):
    cols = [stats[:, 16 * k:16 * k + 1] for k in range(2 + N_HEAD)]
    return cols[0], cols[1], cols[2:]


def _mixer(proj, halo, row0, wpool_ref, pscale, sgu_g_ref, sgu_b_ref, wsgu_ref, bsgu_ref, saved=None):
    xa = proj[:, 0:512]
    ga = proj[:, 512:1024]
    u = proj[:, 1024:1536]
    v = proj[:, 1536:2048]
    gb = proj[:, 2048:2560]
    ext = jnp.concatenate([halo, xa], axis=0)
    win = _window_sums(ext, toward_later=False)[HALO:]
    cnt = _window_counts(row0)
    pooled = (win / cnt - xa).astype(BF16)
    pw = jnp.concatenate(
        [_dot(pooled[:, g * HEAD:(g + 1) * HEAD], wpool_ref[g].astype(BF16)) for g in range(N_HEAD)], axis=1)
    sig_a = _sigmoid(ga) if saved is None else saved["sig_a"]
    ya = pw * pscale * (ga * sig_a)
    phi_u, pdf_u = _gelu_parts(u)
    phi_v, pdf_v = _gelu_parts(v)
    gu = u * phi_u
    gv = v * phi_v
    sig_b = _sigmoid(gb) if saved is None else saved["sig_b"]
    silu_b = gb * sig_b
    mask = _causal_mask()
    diag = lax.broadcasted_iota(jnp.int32, (HEAD, HEAD), 0) == lax.broadcasted_iota(jnp.int32, (HEAD, HEAD), 1)
    vhat, rstd_v, vln_l, mixed = [], [], [], []
    for h in range(N_HEAD):
        if saved is None:
            vh, rh = _layer_norm(gv[:, h * HEAD:(h + 1) * HEAD])
        else:
            vh, rh = saved["vhat"][h], saved["rstd_v"][h]
        ln = (vh * sgu_g_ref[h:h + 1, :] + sgu_b_ref[h:h + 1, :]).astype(BF16)
        ln_l = _chunks_to_lanes(ln)
        wm = jnp.where(mask, wsgu_ref[h], 0.0).astype(BF16)
        bias = jnp.sum(jnp.where(diag, jnp.broadcast_to(bsgu_ref[h:h + 1, :], (HEAD, HEAD)), 0.0), axis=1, keepdims=True)
        mx = _lanes_to_chunks(_dot(wm, ln_l) + bias)
        vhat.append(vh)
        rstd_v.append(rh)
        vln_l.append(ln_l)
        mixed.append(mx)
    mixed = jnp.concatenate(mixed, axis=1)
    yb = gu * mixed * silu_b
    return dict(xa=xa, ga=ga, u=u, gb=gb, cnt=cnt, pooled=pooled, pw=pw, sig_a=sig_a, ya=ya, phi_u=phi_u, pdf_u=pdf_u,
                phi_v=phi_v, pdf_v=pdf_v, gu=gu, sig_b=sig_b, silu_b=silu_b, vhat=vhat, rstd_v=rstd_v, vln_l=vln_l,
                mixed=mixed, yb=yb, mask=mask)


def _const(shape, *index):
    lead = tuple(index) + (0,) * (len(shape) - len(index))
    return pl.BlockSpec(shape, lambda *_: lead)


def _const_in(shape, *index):
    lead = tuple(index) + (0,) * (len(shape) - len(index))
    return pl.BlockSpec(shape, lambda *_: lead, pipeline_mode=pl.Buffered(1))


def _layer_weight_specs(l):
    return [
        _const_in((None, N_HEAD, HEAD, HEAD), l),
        _const_in((DEPTH, D_POOL)),
        _const_in((None, N_HEAD, HEAD), l),
        _const_in((None, N_HEAD, HEAD), l),
        _const_in((None, N_HEAD, HEAD, HEAD), l),
        _const_in((None, N_HEAD, HEAD), l),
    ]


def _forward_tile(l, i, x_ref, mod_ref, win_ref, wout_ref, small_refs, lng_ref, lnb_ref, carry_ref, saved_refs):
    wpool_ref, pscale_ref, sgu_g_ref, sgu_b_ref, wsgu_ref, bsgu_ref = small_refs
    proj_ref, y_ref, xn_ref, zn_ref, stats_ref, sig_ref, vhat_ref = saved_refs
    x = x_ref[...]
    if l > 0:
        x = x * lng_ref[l - 1:l, :] + lnb_ref[l - 1:l, :]
    shift, scale, gate = mod_ref[0:1, :], mod_ref[1:2, :], mod_ref[2:3, :]
    xn, rstd_x = _layer_norm(x)
    xn_ref[...] = xn.astype(xn_ref.dtype)
    h = xn * (1.0 + scale) + shift
    proj = _dot(h.astype(BF16), win_ref[...])
    proj_ref[...] = proj
    m = _mixer(proj, carry_ref[...], i * ROWS, wpool_ref, pscale_ref[l:l + 1, :], sgu_g_ref, sgu_b_ref, wsgu_ref, bsgu_ref)
    carry_ref[...] = m["xa"][ROWS - HALO:]
    sig_ref[...] = jnp.concatenate([m["sig_a"], m["sig_b"]], axis=1).astype(sig_ref.dtype)
    vhat_ref[...] = jnp.concatenate(m["vhat"], axis=1).astype(vhat_ref.dtype)
    cat = jnp.concatenate([m["ya"], m["yb"]], axis=1).astype(BF16)
    y = _dot(cat, wout_ref[...])
    y_ref[...] = y.astype(y_ref.dtype)
    zn, rstd_z = _layer_norm(DEEPNORM_ALPHA * x + gate * y)
    zn_ref[...] = zn
    stats_ref[...] = _pack_stats(rstd_x, rstd_z, m["rstd_v"])
    return zn


SAVED_COLS = (D_PROJ, D_MODEL, D_MODEL, D_MODEL, HEAD, D_MODEL, D_POOL)
SAVED_TYPES = (F32, BF16, BF16, F32, F32, BF16, BF16)


def _saved_outputs():
    return ([jax.ShapeDtypeStruct((SEQ, cols), t) for cols, t in zip(SAVED_COLS, SAVED_TYPES)],
            [pl.BlockSpec((ROWS, cols), lambda i: (i, 0)) for cols in SAVED_COLS])


def _forward_last(zn_prev, mod, w_in, w_out, small, ln_g, ln_b, target):
    l = DEPTH - 1
    n_saved = len(SAVED_COLS)

    def body(*refs):
        x_ref, mod_ref, win_ref, wout_ref = refs[:4]
        small_refs, lng_ref, lnb_ref, tgt_ref = refs[4:10], refs[10], refs[11], refs[12]
        saved_refs = refs[13:13 + n_saved]
        dout_ref, loss_ref, carry_ref = refs[13 + n_saved:]
        i = pl.program_id(0)

        @pl.when(i == 0)
        def _():
            carry_ref[...] = jnp.zeros_like(carry_ref)
            loss_ref[...] = jnp.zeros_like(loss_ref)

        zn = _forward_tile(l, i, x_ref, mod_ref, win_ref, wout_ref, small_refs, lng_ref, lnb_ref, carry_ref, saved_refs)
        err = zn * lng_ref[l:l + 1, :] + lnb_ref[l:l + 1, :] - tgt_ref[...]
        dout_ref[...] = err * (1.0 / D_MODEL)
        loss_ref[...] += jnp.sum(err * err)

    tile = pl.BlockSpec((ROWS, D_MODEL), lambda i: (i, 0))
    tile3 = pl.BlockSpec((None, ROWS, D_MODEL), lambda i: (0, i, 0))
    in_specs = [tile, _const_in((None, 8, D_MODEL), l), _const_in((D_MODEL, D_PROJ)), _const_in((D_MODEL, D_MODEL))]
    in_specs += _layer_weight_specs(l) + [_const_in((DEPTH, D_MODEL)), _const_in((DEPTH, D_MODEL)), tile3]
    out_shape, out_specs = _saved_outputs()
    out_shape += [jax.ShapeDtypeStruct((SEQ, D_MODEL), F32), jax.ShapeDtypeStruct((8, HEAD), F32)]
    out_specs += [tile, _const((8, HEAD))]
    return pl.pallas_call(
        body, name="fwd_last", grid=(N_TILE,), in_specs=in_specs, out_specs=out_specs, out_shape=out_shape,
        scratch_shapes=[pltpu.VMEM((HALO, D_POOL), F32)],
        compiler_params=pltpu.CompilerParams(dimension_semantics=("arbitrary",), vmem_limit_bytes=VMEM_LIMIT),
    )(zn_prev, mod, w_in, w_out, *small, ln_g, ln_b, target)


def _backward_layer(l, dout, saved, mod, w_in, w_out, small, ln_g, sq=None, shared=None):
    has_loss = sq is not None

    def body(*refs):
        (dout_ref, proj_ref, y_ref, xn_ref, zn_ref, stats_ref, sig_ref, vhat_ref, halo_ref, mod_ref, win_hbm, wout_ref,
         wpool_ref, pscale_ref, sgu_g_ref, sgu_b_ref, wsgu_ref, bsgu_ref, lng_ref) = refs[:19]
        n_in = 20 if has_loss else 19 + 6
        dx_ref, h_ref, cat_ref, dy_ref, dproj_ref, pack_ref, dmod_ref, carry_ref, win_ref, win_sem = refs[n_in:n_in + 10]
        i = pl.program_id(0)
        tile = N_TILE - 1 - i
        fetch_win = pltpu.make_async_copy(win_hbm, win_ref, win_sem)

        @pl.when(i == 0)
        def _():
            fetch_win.start()
            carry_ref[...] = jnp.zeros_like(carry_ref)
            pack_ref[...] = jnp.zeros_like(pack_ref)
            dmod_ref[...] = jnp.zeros_like(dmod_ref)
            if has_loss:
                dmod_ref[3:4, 0:HEAD] = refs[19][0:1, :]

        xn = xn_ref[...].astype(F32)
        zn = zn_ref[...]
        y = y_ref[...].astype(F32)
        dout = dout_ref[...]
        rstd_x, rstd_z, rstd_v = _unpack_stats(stats_ref[...])
        kept = dict(sig_a=sig_ref[:, :D_POOL].astype(F32), sig_b=sig_ref[:, D_POOL:].astype(F32), rstd_v=rstd_v,
                    vhat=[vhat_ref[:, hd * HEAD:(hd + 1) * HEAD].astype(F32) for hd in range(N_HEAD)])
        pscale = pscale_ref[l:l + 1, :]
        shift, scale, gate = mod_ref[0:1, :], mod_ref[1:2, :], mod_ref[2:3, :]
        h = xn * (1.0 + scale) + shift
        h_ref[...] = h.astype(BF16)
        g_ln_g = _sum_rows(dout * zn)
        g_ln_b = _sum_rows(dout)
        dz = _layer_norm_bwd(dout * lng_ref[l:l + 1, :], zn, rstd_z)
        d_gate = _sum_rows(dz * y)
        dy = (gate * dz).astype(BF16)
        dy_ref[...] = dy

        halo = jnp.where(tile > 0, halo_ref[...], 0.0)
        m = _mixer(proj_ref[...], halo, tile * ROWS, wpool_ref, pscale, sgu_g_ref, sgu_b_ref, wsgu_ref, bsgu_ref,
                   saved=kept)
        cat_ref[...] = jnp.concatenate([m["ya"], m["yb"]], axis=1).astype(BF16)
        dcat = _dot(dy, wout_ref[...], NT)
        dya = dcat[:, :D_POOL]
        dyb = dcat[:, D_POOL:]

        ga, sig_a = m["ga"], m["sig_a"]
        dp = dya * (ga * sig_a)
        d_ga = dya * (m["pw"] * pscale) * (sig_a * (1.0 + ga * (1.0 - sig_a)))
        g_pscale = _sum_rows(dp * m["pw"])
        dpw = (dp * pscale).astype(BF16)
        dpooled = []
        for g in range(N_HEAD):
            cols = slice(g * HEAD, (g + 1) * HEAD)
            pack_ref[PK_W_POOL + g * HEAD:PK_W_POOL + (g + 1) * HEAD, :] += _dot(m["pooled"][:, cols], dpw[:, cols], TN)
            dpooled.append(_dot(dpw[:, cols], wpool_ref[g].astype(BF16), NT))
        dpooled = jnp.concatenate(dpooled, axis=1)
        q = dpooled / m["cnt"]
        ext = jnp.concatenate([q, carry_ref[...]], axis=0)
        d_xa = _window_sums(ext, toward_later=True)[:ROWS] - dpooled
        carry_ref[...] = q[:HALO]

        gu, mixed, silu_b, gb, sig_b = m["gu"], m["mixed"], m["silu_b"], m["gb"], m["sig_b"]
        d_mixed = dyb * gu * silu_b
        d_gu = dyb * mixed * silu_b
        d_gb = dyb * gu * mixed * (sig_b * (1.0 + gb * (1.0 - sig_b)))
        d_u = d_gu * (m["phi_u"] + m["u"] * m["pdf_u"])
        ones = jnp.ones((8, HEAD), F32)
        d_v = []
        for hd in range(N_HEAD):
            cols = slice(hd * HEAD, (hd + 1) * HEAD)
            dm = d_mixed[:, cols]
            dm_l = _chunks_to_lanes(dm.astype(BF16))
            g_w = _dot(dm_l, m["vln_l"][hd], NT)
            pack_ref[PK_W_SGU + hd * HEAD:PK_W_SGU + (hd + 1) * HEAD, :] += jnp.where(m["mask"], g_w, 0.0)
            dm_sum = dm[0:HEAD]
            for n in range(1, ROWS // HEAD):
                dm_sum = dm_sum + dm[n * HEAD:(n + 1) * HEAD]
            pack_ref[PK_B_SGU + hd:PK_B_SGU + hd + 1, :] += _dot_exact(ones, dm_sum, NT)[0:1]
            wm = jnp.where(m["mask"], wsgu_ref[hd], 0.0).astype(BF16)
            d_vln = _lanes_to_chunks(_dot(wm, dm_l, TN))
            vhat = m["vhat"][hd]
            pack_ref[PK_SGU_LN_G + hd:PK_SGU_LN_G + hd + 1, :] += _sum_rows(d_vln * vhat)
            pack_ref[PK_SGU_LN_B + hd:PK_SGU_LN_B + hd + 1, :] += _sum_rows(d_vln)
            d_v.append(_layer_norm_bwd(d_vln * sgu_g_ref[hd:hd + 1, :], vhat, m["rstd_v"][hd]))
        v = proj_ref[:, 1536:2048]
        d_v = jnp.concatenate(d_v, axis=1) * (m["phi_v"] + v * m["pdf_v"])

        dproj = jnp.concatenate([d_xa, d_ga, d_u, d_v, d_gb], axis=1).astype(BF16)
        dproj_ref[...] = dproj
        @pl.when(i == 0)
        def _():
            fetch_win.wait()

        dh = _dot(dproj, win_ref[...], NT)
        d_scale = _sum_rows(dh * xn)
        d_shift = _sum_rows(dh)
        dx_ref[...] = DEEPNORM_ALPHA * dz + _layer_norm_bwd(dh * (1.0 + scale), xn, rstd_x)

        dmod_ref[0:1, :] += d_shift
        dmod_ref[1:2, :] += d_scale
        dmod_ref[2:3, :] += d_gate
        for g in range(N_HEAD):
            pack_ref[PK_POOL_SCALE + g:PK_POOL_SCALE + g + 1, :] += g_pscale[:, g * HEAD:(g + 1) * HEAD]
        for k in range(D_MODEL // HEAD):
            pack_ref[PK_LN_G + k:PK_LN_G + k + 1, :] += g_ln_g[:, k * HEAD:(k + 1) * HEAD]
            pack_ref[PK_LN_B + k:PK_LN_B + k + 1, :] += g_ln_b[:, k * HEAD:(k + 1) * HEAD]

    def rev(i):
        return (N_TILE - 1 - i, 0)

    tile = pl.BlockSpec((ROWS, D_MODEL), rev)
    halo = pl.BlockSpec((HALO, D_POOL), lambda i: (jnp.maximum((N_TILE - 1 - i) * (ROWS // HALO) - 1, 0), 0))
    in_specs = [tile] + [pl.BlockSpec((ROWS, a.shape[1]), rev) for a in saved] + [halo]
    in_specs += [_const_in((None, 8, D_MODEL), l), pl.BlockSpec(memory_space=pl.ANY), _const_in((D_MODEL, D_MODEL))]
    in_specs += _layer_weight_specs(l) + [_const_in((DEPTH, D_MODEL))]
    args = [dout, *saved, saved[0], mod, w_in, w_out, *small, ln_g]
    stacked = lambda cols: pl.BlockSpec((None, ROWS, cols), lambda i: (l, N_TILE - 1 - i, 0))
    out_shape = [jax.ShapeDtypeStruct((SEQ, D_MODEL), F32), jax.ShapeDtypeStruct((DEPTH, SEQ, D_MODEL), BF16),
                 jax.ShapeDtypeStruct((DEPTH, SEQ, D_MODEL), BF16), jax.ShapeDtypeStruct((DEPTH, SEQ, D_MODEL), BF16),
                 jax.ShapeDtypeStruct((DEPTH, SEQ, D_PROJ), BF16), jax.ShapeDtypeStruct((DEPTH, PK_ROWS, HEAD), F32),
                 jax.ShapeDtypeStruct((DEPTH, 8, D_MODEL), F32)]
    out_specs = [tile, stacked(D_MODEL), stacked(D_MODEL), stacked(D_MODEL), stacked(D_PROJ),
                 _const((None, PK_ROWS, HEAD), l), _const((None, 8, D_MODEL), l)]
    aliases = {}
    if has_loss:
        in_specs.append(_const_in((8, HEAD)))
        args.append(sq)
    else:
        aliases = {len(args) + k: 1 + k for k in range(len(shared))}
        in_specs += [pl.BlockSpec(memory_space=pl.ANY)] * len(shared)
        args += list(shared)
    return pl.pallas_call(
        body, name="bwd_last" if has_loss else "bwd_first", grid=(N_TILE,), in_specs=in_specs, out_specs=out_specs,
        out_shape=out_shape, input_output_aliases=aliases,
        scratch_shapes=[pltpu.VMEM((HALO, D_POOL), F32), pltpu.VMEM((D_MODEL, D_PROJ), BF16), pltpu.SemaphoreType.DMA],
        compiler_params=pltpu.CompilerParams(dimension_semantics=("arbitrary",), vmem_limit_bytes=VMEM_LIMIT),
    )(*args)


def _flip(v, f):
    return v + f - 2 * v * f


class _Place:
    def __init__(self):
        x, y, c = lax.axis_index("x"), lax.axis_index("y"), lax.axis_index("c")
        self.x, self.y, self.c = x, y, c
        self.chip = 2 * x + y
        self.dev = 4 * x + 2 * y + c
        self.sibling = (x, y, 1 - c)
        x1, y1 = _flip(x, 1 - c), _flip(y, c)
        x2, y2 = _flip(x, c), _flip(y, 1 - c)
        self.first = (x1, y1, c)
        self.second = (x2, y2, c)
        self.chip_first = 2 * x1 + y1
        self.chip_second = 2 * x2 + y2
        self.chip_far = 2 * (1 - x) + (1 - y)
        self.my_first_coord = jnp.where(c == 0, x, y)

    def first_coord(self, ch):
        return jnp.where(self.c == 0, ch // 2, ch % 2)

    def others(self):
        return [(_flip(self.x, (r >> 2) & 1), _flip(self.y, (r >> 1) & 1), _flip(self.c, r & 1)) for r in range(1, N_DEV)]

    def other_chips(self):
        return [(1 - self.x, self.y), (self.x, 1 - self.y), (1 - self.x, 1 - self.y)]


class _WeightGather:
    CHUNKS = 4
    N_SEMS = 12 * CHUNKS

    def __init__(self, place, win, wout, send, recv):
        self.p, self.win, self.wout, self.send, self.recv = place, win, wout, send, recv
        p = place
        self.plan = [(p.chip, p.first), (p.chip, p.second), (p.chip_first, p.second),
                     (p.chip_first, p.sibling), (p.chip_second, p.sibling), (p.chip_far, p.sibling)]

    def _copies(self, k, q):
        ch, target = self.plan[k]
        n_in, n_out = HALF_IN // self.CHUNKS, HALF_OUT // self.CHUNKS
        rows_in = pl.ds(pl.multiple_of(self.p.c * HALF_IN + q * n_in, n_in), n_in)
        cols_in = pl.ds(pl.multiple_of(ch * W_IN_COLS, 128), W_IN_COLS)
        rows_out = pl.ds(pl.multiple_of(ch * W_OUT_ROWS + self.p.c * HALF_OUT + q * n_out, n_out), n_out)
        r_in = self.win.at[rows_in, cols_in]
        r_out = self.wout.at[rows_out, :]
        s = 2 * (6 * q + k)
        return [pltpu.make_async_remote_copy(r_in, r_in, self.send.at[s], self.recv.at[s],
                                             device_id=target, device_id_type=MESH),
                pltpu.make_async_remote_copy(r_out, r_out, self.send.at[s + 1], self.recv.at[s + 1],
                                             device_id=target, device_id_type=MESH)]

    def _start(self, k, q):
        for cp in self._copies(k, q):
            cp.start()

    def _landed(self, k, q):
        for cp in self._copies(k, q):
            cp.wait_recv()

    def start_first_round(self):
        for q in range(self.CHUNKS):
            self._start(0, q)

    def start_second_round(self, q):
        self._landed(0, q)
        self._start(1, q)
        self._start(2, q)
        self._start(3, q)

    def pass_second_round(self, q):
        self._landed(1, q)
        self._start(4, q)
        self._landed(2, q)
        self._start(5, q)

    def finish(self):
        for q in range(self.CHUNKS):
            for k in (3, 4, 5):
                self._landed(k, q)
        for q in range(self.CHUNKS):
            for k in range(len(self.plan)):
                for cp in self._copies(k, q):
                    cp.wait_send()


def _forward_first(x, c_vec, w_ada, b_ada, w_in, w_out, small, ln_g, ln_b):
    n_saved = len(SAVED_COLS)

    def body(*refs):
        x_ref, c_ref, wada_hbm, bada_ref, win_hbm, wout_hbm = refs[:6]
        small_refs, lng_ref, lnb_ref = refs[6:12], refs[12], refs[13]
        saved_refs = refs[14:14 + n_saved]
        win0, wout0, win1, wout1, mod_out, c_out = refs[14 + n_saved:20 + n_saved]
        (carry_ref, wada_ref, win_ref, wout_ref, win_bf, wout_bf, mod_mine, mod_all, c_all, mod_ref, win_v, wout_v,
         g0_send, g0_recv, g1_send, g1_recv, c_send, c_recv, mod_send, mod_recv, local_sem) = refs[20 + n_saved:]
        i = pl.program_id(0)
        p = _Place()
        gather0 = _WeightGather(p, win0, wout0, g0_send, g0_recv)
        gather1 = _WeightGather(p, win1, wout1, g1_send, g1_recv)

        @pl.when(i == 0)
        def _():
            carry_ref[...] = jnp.zeros_like(carry_ref)
            loads = [pltpu.make_async_copy(win_hbm.at[0], win_ref.at[0], local_sem.at[4]),
                     pltpu.make_async_copy(wout_hbm.at[0], wout_ref.at[0], local_sem.at[5]),
                     pltpu.make_async_copy(win_hbm.at[1], win_ref.at[1], local_sem.at[6]),
                     pltpu.make_async_copy(wout_hbm.at[1], wout_ref.at[1], local_sem.at[7]),
                     pltpu.make_async_copy(wada_hbm, wada_ref, local_sem.at[8])]
            for cp in loads:
                cp.start()

            c_all[pl.ds(p.dev, 1), :] = c_ref[...]
            mine = c_all.at[pl.ds(p.dev, 1), :]
            c_copies = [pltpu.make_async_remote_copy(mine, mine, c_send.at[r], c_recv.at[r], device_id=d, device_id_type=MESH)
                        for r, d in enumerate(p.others())]
            for cp in c_copies:
                cp.start()

            cols = pl.ds(pl.multiple_of(p.chip * W_IN_COLS, 128), W_IN_COLS)
            rows = pl.ds(pl.multiple_of(p.chip * W_OUT_ROWS, W_OUT_ROWS), W_OUT_ROWS)
            own = [pltpu.make_async_copy(win_bf.at[0], win0.at[:, cols], local_sem.at[0]),
                   pltpu.make_async_copy(wout_bf.at[0], wout0.at[rows, :], local_sem.at[1]),
                   pltpu.make_async_copy(win_bf.at[1], win1.at[:, cols], local_sem.at[2]),
                   pltpu.make_async_copy(wout_bf.at[1], wout1.at[rows, :], local_sem.at[3])]
            for l in range(DEPTH):
                loads[2 * l].wait()
                win_bf[l] = win_ref[l].astype(BF16)
                own[2 * l].start()
                loads[2 * l + 1].wait()
                wout_bf[l] = wout_ref[l].astype(BF16)
                own[2 * l + 1].start()
                if l == 0:
                    own[0].wait()
                    own[1].wait()
                    gather0.start_first_round()
            for cp in c_copies:
                cp.wait()
            loads[4].wait()

            cv = c_all[...]
            c_out[...] = cv
            silu_c = (cv * _sigmoid(cv)).astype(BF16)
            for l in range(DEPTH):
                mod_mine[l] = _dot(silu_c, wada_ref[l].astype(BF16))
            mod_all[p.chip] = mod_mine[...]
            m_copies = [pltpu.make_async_remote_copy(mod_mine, mod_all.at[p.chip], mod_send.at[k], mod_recv.at[k],
                                                     device_id=(px, py, p.c), device_id_type=MESH)
                        for k, (px, py) in enumerate(p.other_chips())]
            for cp in m_copies:
                cp.start()
            for q in range(gather0.CHUNKS):
                gather0.start_second_round(q)
            own[2].wait()
            own[3].wait()
            gather1.start_first_round()
            for cp in m_copies:
                cp.wait()
            mod_ref[...] = jnp.zeros_like(mod_ref)
            for l in range(DEPTH):
                full = jnp.concatenate([mod_all[ch, l, pl.ds(p.dev, 1), :] for ch in range(N_CHIP)], axis=1) + bada_ref[l:l + 1, :]
                for k in range(3):
                    mod_ref[l, k:k + 1, :] = full[:, k * D_MODEL:(k + 1) * D_MODEL]
            mod_out[...] = mod_ref[...]
            for q in range(gather0.CHUNKS):
                gather0.pass_second_round(q)
            gather0.finish()
            fetch = [pltpu.make_async_copy(win0, win_v, local_sem.at[9]), pltpu.make_async_copy(wout0, wout_v, local_sem.at[10])]
            for cp in fetch:
                cp.start()
            for cp in fetch:
                cp.wait()

        for q in range(_WeightGather.CHUNKS):
            @pl.when(i == GATHER_SECOND_ROUND_STEP + q)
            def _(q=q):
                gather1.start_second_round(q)

            @pl.when(i == GATHER_PASS_STEP + q)
            def _(q=q):
                gather1.pass_second_round(q)

        _forward_tile(0, i, x_ref, mod_ref.at[0], win_v, wout_v, small_refs, lng_ref, lnb_ref, carry_ref, saved_refs)

        @pl.when(i == N_TILE - 1)
        def _():
            gather1.finish()

    hbm = pl.BlockSpec(memory_space=pl.ANY)
    tile3 = pl.BlockSpec((None, ROWS, D_MODEL), lambda i: (0, i, 0))
    in_specs = [tile3, _const_in((1, D_MODEL)), hbm, _const_in((DEPTH, 3 * D_MODEL)), hbm, hbm]
    in_specs += _layer_weight_specs(0) + [_const_in((DEPTH, D_MODEL)), _const_in((DEPTH, D_MODEL))]
    out_shape, out_specs = _saved_outputs()
    w_in_shape = jax.ShapeDtypeStruct((D_MODEL, D_PROJ), BF16)
    w_out_shape = jax.ShapeDtypeStruct((D_MODEL, D_MODEL), BF16)
    out_shape += [w_in_shape, w_out_shape, w_in_shape, w_out_shape,
                  jax.ShapeDtypeStruct((DEPTH, 8, D_MODEL), F32), jax.ShapeDtypeStruct((N_DEV, D_MODEL), F32)]
    out_specs += [hbm, hbm, hbm, hbm, _const((DEPTH, 8, D_MODEL)), _const((N_DEV, D_MODEL))]
    gather_sems = [pltpu.SemaphoreType.DMA((_WeightGather.N_SEMS,))] * 4
    scratch = [
        pltpu.VMEM((HALO, D_POOL), F32),
        pltpu.VMEM(w_ada.shape, F32), pltpu.VMEM(w_in.shape, F32), pltpu.VMEM(w_out.shape, F32),
        pltpu.VMEM((DEPTH, D_MODEL, W_IN_COLS), BF16), pltpu.VMEM((DEPTH, W_OUT_ROWS, D_MODEL), BF16),
        pltpu.VMEM((DEPTH, N_DEV, W_ADA_COLS), F32), pltpu.VMEM((N_CHIP, DEPTH, N_DEV, W_ADA_COLS), F32),
        pltpu.VMEM((N_DEV, D_MODEL), F32), pltpu.VMEM((DEPTH, 8, D_MODEL), F32),
        pltpu.VMEM((D_MODEL, D_PROJ), BF16), pltpu.VMEM((D_MODEL, D_MODEL), BF16),
    ] + gather_sems + [
        pltpu.SemaphoreType.DMA((7,)), pltpu.SemaphoreType.DMA((7,)),
        pltpu.SemaphoreType.DMA((3,)), pltpu.SemaphoreType.DMA((3,)),
        pltpu.SemaphoreType.DMA((11,)),
    ]
    return pl.pallas_call(
        body, name="fwd_first", grid=(N_TILE,), in_specs=in_specs, out_specs=out_specs, out_shape=out_shape,
        scratch_shapes=scratch,
        compiler_params=pltpu.CompilerParams(dimension_semantics=("arbitrary",), vmem_limit_bytes=VMEM_LIMIT),
    )(x, c_vec, w_ada, b_ada, w_in, w_out, *small, ln_g, ln_b)


IN_STEPS = W_IN_COLS // HEAD
OUT_STEPS = 4
OUT_COLS = D_MODEL // OUT_STEPS
OUT_FIRST = 2
ITEMS = ([("out", k) for k in range(OUT_FIRST)] + [("in", k) for k in range(IN_STEPS)]
         + [("out", k) for k in range(OUT_FIRST, OUT_STEPS)])
N_ITEMS = len(ITEMS)
N_STEPS = DEPTH * N_ITEMS
DELAY_SUM, DELAY_SECOND, DELAY_FINAL = 1, 3, 5
SMALL_SCATTER_STEP, SMALL_GATHER_STEP, SMALL_PASS_STEP, SMALL_FINISH_STEP = 1, 3, 5, 7


def _wgrad_reduce(h, dproj, cat, dy, pack, dmod):
    def body(*refs):
        h_ref, dp_refs, cat_ref, dy_ref, pack_ref, dmod_ref = refs[0], refs[1:5], refs[5], refs[6], refs[7], refs[8]
        fin_in, fin_out, pack_out, dmod_out = refs[9:13]
        scratch = refs[13:]
        (mine_in, send_in, sib_in, st_in, r1_in, r2_in, f_in,
         mine_out, send_out, sib_out, st_out, r1_out, r2_out, f_out,
         d2d_s, d2d_r, r1_s, r1_r, r2_s, r2_r, fin_l, fin_s, fin_r) = scratch[:23]
        p = _Place()
        c = p.c
        i = pl.program_id(0)
        my_rows = pl.ds(pl.multiple_of(c * HALF_IN, HALF_IN), HALF_IN)

        def layer_of(j):
            return DEPTH - 1 - j // N_ITEMS

        def bufs(j):
            kind, k = ITEMS[j % N_ITEMS]
            if kind == "in":
                return [r.at[k] for r in (mine_in, send_in, sib_in, st_in, r1_in, r2_in, f_in)]
            return [r.at[k] for r in (mine_out, send_out, sib_out, st_out, r1_out, r2_out, f_out)]

        def piece(j, ref, ch):
            if ITEMS[j % N_ITEMS][0] == "in":
                return ref.at[:, ch * HEAD:(ch + 1) * HEAD]
            return ref.at[ch]

        def slot(ch):
            return jnp.where(c == 0, ch % 2, ch // 2)

        def to_sibling(j):
            _, send, sib, _, _, _, _ = bufs(j)
            return pltpu.make_async_remote_copy(send, sib, d2d_s.at[j], d2d_r.at[j], device_id=p.sibling, device_id_type=MESH)

        def first_round(j, ch):
            _, _, _, st, r1, _, _ = bufs(j)
            k = slot(ch)
            return pltpu.make_async_remote_copy(st.at[k], r1.at[k], r1_s.at[2 * j + k], r1_r.at[2 * j + k],
                                                device_id=p.first, device_id_type=MESH)

        def second_round(j):
            _, _, _, st, _, r2, _ = bufs(j)
            return pltpu.make_async_remote_copy(st.at[2], r2, r2_s.at[j], r2_r.at[j], device_id=p.second, device_id_type=MESH)

        def finals(j):
            f = bufs(j)[6]
            kind, k = ITEMS[j % N_ITEMS]
            if kind == "in":
                dst = fin_in.at[layer_of(j), my_rows, k * HEAD:(k + 1) * HEAD]
            else:
                dst = fin_out.at[layer_of(j), c, :, k * OUT_COLS:(k + 1) * OUT_COLS]
            return [pltpu.make_async_copy(f, dst, fin_l.at[j]),
                    pltpu.make_async_remote_copy(f, dst, fin_s.at[j], fin_r.at[j], device_id=p.sibling, device_id_type=MESH)]

        def stage_sum(j):
            mine, _, sib, st, _, _, _ = bufs(j)
            to_sibling(j).wait_recv()
            mine[...] = mine[...] + sib[...]
            for ch in range(N_CHIP):
                @pl.when(p.first_coord(ch) != p.my_first_coord)
                def _(ch=ch):
                    st[slot(ch)] = piece(j, mine, ch)[...].astype(BF16)
                    first_round(j, ch).start()

        def stage_second(j):
            mine, _, _, st, r1, _, _ = bufs(j)
            for ch in range(N_CHIP):
                @pl.when(p.first_coord(ch) == p.my_first_coord)
                def _(ch=ch):
                    first_round(j, ch).wait_recv()
                    part = piece(j, mine, ch)
                    total = part[...] + r1[slot(ch)].astype(F32)
                    part[...] = total

                    @pl.when(ch != p.chip)
                    def _():
                        st[2] = total.astype(BF16)
                        second_round(j).start()

        def stage_final(j):
            mine, _, _, _, _, r2, f = bufs(j)
            second_round(j).wait_recv()
            for ch in range(N_CHIP):
                @pl.when(ch == p.chip)
                def _(ch=ch):
                    f[...] = piece(j, mine, ch)[...] + r2[...].astype(F32)
            for cp in finals(j):
                cp.start()

        def drain(j):
            to_sibling(j).wait_send()
            for ch in range(N_CHIP):
                @pl.when(p.first_coord(ch) != p.my_first_coord)
                def _(ch=ch):
                    first_round(j, ch).wait_send()

                @pl.when(jnp.logical_and(p.first_coord(ch) == p.my_first_coord, ch != p.chip))
                def _():
                    second_round(j).wait_send()
            for cp in finals(j):
                cp.wait()

        dev = p.dev
        devices = p.others()

        def half(core):
            return pl.ds(pl.multiple_of(core * PK_HALF, 16), PK_HALF)

        def finished(core, ch):
            return pl.ds(pl.multiple_of(core * PK_HALF + ch * PK_PIECE, 16), PK_PIECE)

        def small_exchange(l, first_step, bufs_l):
            (pk_mine, pk_sib, pk_st, pk_rs, pk_fin, pk_all, dm_st, dm_all, pk_sem, rs_s, rs_r, ag_s, ag_r, dm_s, dm_r) = bufs_l

            def pk_load():
                return pltpu.make_async_copy(pack_ref.at[l, half(c)], pk_mine, pk_sem.at[0])

            def pk_give():
                return pltpu.make_async_remote_copy(pack_ref.at[l, half(1 - c)], pk_sib, pk_sem.at[1], pk_sem.at[2],
                                                    device_id=p.sibling, device_id_type=MESH)

            def pk_scatter(ch):
                return pltpu.make_async_remote_copy(pk_st.at[ch * PK_PIECE:(ch + 1) * PK_PIECE], pk_rs.at[p.chip],
                                                    rs_s.at[ch], rs_r.at[p.chip], device_id=(ch // 2, ch % 2, c),
                                                    device_id_type=MESH)

            def pk_spread(ch):
                return pltpu.make_async_remote_copy(pk_fin, pk_all.at[finished(c, p.chip)], ag_s.at[ch], ag_r.at[p.chip],
                                                    device_id=(ch // 2, ch % 2, c), device_id_type=MESH)

            def pk_pass():
                return pltpu.make_async_remote_copy(pk_all.at[half(c)], pk_all.at[half(c)], pk_sem.at[3], pk_sem.at[4],
                                                    device_id=p.sibling, device_id_type=MESH)

            def dm_copy(r):
                return pltpu.make_async_remote_copy(dm_st, dm_all.at[:, pl.ds(dev, 1), :], dm_s.at[r], dm_r.at[r],
                                                    device_id=devices[r], device_id_type=MESH)

            def results():
                return [pltpu.make_async_copy(pk_all, pack_out.at[l], pk_sem.at[0]),
                        pltpu.make_async_copy(dm_all, dmod_out.at[l], pk_sem.at[5])]

            @pl.when(i == first_step)
            def _():
                pk_load().start()
                pk_give().start()
                for k in range(3):
                    for r in range(D_MODEL // HEAD):
                        dm_st[8 * k + r] = dmod_ref[l, k:k + 1, r * HEAD:(r + 1) * HEAD]
                dm_st[DM_LOSS] = dmod_ref[l, 3:4, 0:HEAD]
                dm_all[:, pl.ds(dev, 1), :] = dm_st[...]
                for r in range(N_DEV - 1):
                    dm_copy(r).start()

            @pl.when(i == first_step + SMALL_SCATTER_STEP)
            def _():
                pk_load().wait()
                pk_give().wait()
                total = pk_mine[...] + pk_sib[...]
                pk_mine[...] = total
                pk_st[...] = total.astype(BF16)
                for ch in range(N_CHIP):
                    @pl.when(ch != p.chip)
                    def _(ch=ch):
                        pk_scatter(ch).start()

            @pl.when(i == first_step + SMALL_GATHER_STEP)
            def _():
                for ch in range(N_CHIP):
                    @pl.when(ch != p.chip)
                    def _(ch=ch):
                        pltpu.make_async_remote_copy(pk_fin, pk_rs.at[ch], rs_s.at[ch], rs_r.at[ch],
                                                     device_id=p.sibling, device_id_type=MESH).wait_recv()
                for me in range(N_CHIP):
                    @pl.when(me == p.chip)
                    def _(me=me):
                        total = None
                        for ch in range(N_CHIP):
                            part = pk_mine[me * PK_PIECE:(me + 1) * PK_PIECE] if ch == me else pk_rs[ch].astype(F32)
                            total = part if total is None else total + part
                        pk_fin[...] = total.astype(BF16)
                        pk_all[finished(c, me)] = total.astype(BF16)
                for ch in range(N_CHIP):
                    @pl.when(ch != p.chip)
                    def _(ch=ch):
                        pk_spread(ch).start()

            @pl.when(i == first_step + SMALL_PASS_STEP)
            def _():
                for ch in range(N_CHIP):
                    @pl.when(ch != p.chip)
                    def _(ch=ch):
                        pltpu.make_async_remote_copy(pk_fin, pk_all.at[finished(c, ch)], ag_s.at[ch], ag_r.at[ch],
                                                     device_id=p.sibling, device_id_type=MESH).wait_recv()
                pk_pass().start()

            @pl.when(i == first_step + SMALL_FINISH_STEP)
            def _():
                pk_pass().wait()
                for ch in range(N_CHIP):
                    @pl.when(ch != p.chip)
                    def _(ch=ch):
                        pk_scatter(ch).wait_send()
                        pk_spread(ch).wait_send()
                for r in range(N_DEV - 1):
                    dm_copy(r).wait()
                for cp in results():
                    cp.start()
                for cp in results():
                    cp.wait()

        n_small = 15
        for l in range(DEPTH):
            small_exchange(l, (DEPTH - 1 - l) * N_ITEMS, scratch[23 + n_small * l:23 + n_small * (l + 1)])

        for step in range(N_ITEMS, N_STEPS):
            @pl.when(i == step)
            def _(step=step):
                drain(step - N_ITEMS)

        ii = jnp.where(i < N_ITEMS, i, i - N_ITEMS)
        in_step = jnp.logical_and(ii >= OUT_FIRST, ii < OUT_FIRST + IN_STEPS)

        @pl.when(in_step)
        def _():
            k = ii - OUT_FIRST
            rhs = jnp.concatenate([r[...] for r in dp_refs], axis=1)
            res = _dot(h_ref[...], rhs, TN)

            @pl.when(c == 0)
            def _():
                mine_in[k] = res[:HALF_IN]
                send_in[k] = res[HALF_IN:]

            @pl.when(c == 1)
            def _():
                mine_in[k] = res[HALF_IN:]
                send_in[k] = res[:HALF_IN]

        @pl.when(jnp.logical_not(in_step))
        def _():
            k = jnp.where(ii < OUT_FIRST, ii, ii - IN_STEPS)
            res = _dot(cat_ref[...], dy_ref[...], TN)

            @pl.when(c == 0)
            def _():
                for ch in range(N_CHIP):
                    mine_out[k, ch] = res[ch * W_OUT_ROWS:ch * W_OUT_ROWS + HALF_OUT]
                    send_out[k, ch] = res[ch * W_OUT_ROWS + HALF_OUT:(ch + 1) * W_OUT_ROWS]

            @pl.when(c == 1)
            def _():
                for ch in range(N_CHIP):
                    mine_out[k, ch] = res[ch * W_OUT_ROWS + HALF_OUT:(ch + 1) * W_OUT_ROWS]
                    send_out[k, ch] = res[ch * W_OUT_ROWS:ch * W_OUT_ROWS + HALF_OUT]

        stages = ((0, lambda j: to_sibling(j).start()), (DELAY_SUM, stage_sum), (DELAY_SECOND, stage_second),
                  (DELAY_FINAL, stage_final))
        for step in range(N_STEPS):
            @pl.when(i == step)
            def _(step=step):
                for delay, stage in stages:
                    if step - delay >= 0:
                        stage(step - delay)

        @pl.when(i == N_STEPS - 1)
        def _():
            for step in range(N_STEPS, N_STEPS + DELAY_FINAL):
                for delay, stage in stages:
                    if 0 <= step - delay < N_STEPS:
                        stage(step - delay)
            for j in range(N_STEPS - N_ITEMS, N_STEPS):
                drain(j)

    hbm = pl.BlockSpec(memory_space=pl.ANY)

    def layer(i):
        return jnp.where(i < N_ITEMS, DEPTH - 1, 0)

    def item(i):
        return jnp.where(i < N_ITEMS, i, i - N_ITEMS)

    def whole(i):
        return (layer(i), 0, 0)

    def dproj_piece(ch):
        return pl.BlockSpec((None, SEQ, HEAD),
                            lambda i: (layer(i), 0, ch * IN_STEPS + jnp.clip(item(i) - OUT_FIRST, 0, IN_STEPS - 1)))

    def dy_quarter(i):
        return (layer(i), 0, jnp.where(item(i) < OUT_FIRST, item(i), jnp.maximum(item(i) - IN_STEPS, OUT_FIRST)))

    operand = pl.BlockSpec((None, SEQ, D_MODEL), whole)
    in_specs = [operand] + [dproj_piece(ch) for ch in range(N_CHIP)]
    in_specs += [operand, pl.BlockSpec((None, SEQ, OUT_COLS), dy_quarter), hbm, _const_in((DEPTH, 8, D_MODEL))]
    args = [h, dproj, dproj, dproj, dproj, cat, dy, pack, dmod]
    out_shape = [jax.ShapeDtypeStruct((DEPTH, D_MODEL, W_IN_COLS), F32), jax.ShapeDtypeStruct((DEPTH, 2, HALF_OUT, D_MODEL), F32),
                 jax.ShapeDtypeStruct((DEPTH, PK_ROWS, HEAD), BF16), jax.ShapeDtypeStruct((DEPTH, DM_ROWS, N_DEV, HEAD), F32)]
    out_specs = [hbm, hbm, hbm, hbm]
    in_item = lambda *lead: pltpu.VMEM(lead + (HALF_IN, HEAD), BF16)
    out_item = lambda *lead: pltpu.VMEM(lead + (HALF_OUT, OUT_COLS), BF16)
    scratch = [
        pltpu.VMEM((IN_STEPS, HALF_IN, N_CHIP * HEAD), F32), pltpu.VMEM((IN_STEPS, HALF_IN, N_CHIP * HEAD), F32),
        pltpu.VMEM((IN_STEPS, HALF_IN, N_CHIP * HEAD), F32), in_item(IN_STEPS, 3), in_item(IN_STEPS, 2), in_item(IN_STEPS),
        pltpu.VMEM((IN_STEPS, HALF_IN, HEAD), F32),
        pltpu.VMEM((OUT_STEPS, N_CHIP, HALF_OUT, OUT_COLS), F32), pltpu.VMEM((OUT_STEPS, N_CHIP, HALF_OUT, OUT_COLS), F32),
        pltpu.VMEM((OUT_STEPS, N_CHIP, HALF_OUT, OUT_COLS), F32), out_item(OUT_STEPS, 3), out_item(OUT_STEPS, 2),
        out_item(OUT_STEPS), pltpu.VMEM((OUT_STEPS, HALF_OUT, OUT_COLS), F32),
        pltpu.SemaphoreType.DMA((N_STEPS,)), pltpu.SemaphoreType.DMA((N_STEPS,)),
        pltpu.SemaphoreType.DMA((2 * N_STEPS,)), pltpu.SemaphoreType.DMA((2 * N_STEPS,)),
        pltpu.SemaphoreType.DMA((N_STEPS,)), pltpu.SemaphoreType.DMA((N_STEPS,)),
        pltpu.SemaphoreType.DMA((N_STEPS,)), pltpu.SemaphoreType.DMA((N_STEPS,)), pltpu.SemaphoreType.DMA((N_STEPS,)),
    ]
    for _ in range(DEPTH):
        scratch += [
            pltpu.VMEM((PK_HALF, HEAD), F32), pltpu.VMEM((PK_HALF, HEAD), F32), pltpu.VMEM((PK_HALF, HEAD), BF16),
            pltpu.VMEM((N_CHIP, PK_PIECE, HEAD), BF16), pltpu.VMEM((PK_PIECE, HEAD), BF16), pltpu.VMEM((PK_ROWS, HEAD), BF16),
            pltpu.VMEM((DM_ROWS, 1, HEAD), F32), pltpu.VMEM((DM_ROWS, N_DEV, HEAD), F32),
            pltpu.SemaphoreType.DMA((6,)),
            pltpu.SemaphoreType.DMA((N_CHIP,)), pltpu.SemaphoreType.DMA((N_CHIP,)),
            pltpu.SemaphoreType.DMA((N_CHIP,)), pltpu.SemaphoreType.DMA((N_CHIP,)),
            pltpu.SemaphoreType.DMA((N_DEV - 1,)), pltpu.SemaphoreType.DMA((N_DEV - 1,)),
        ]
    return pl.pallas_call(
        body, name="wgrad", grid=(N_STEPS,), in_specs=in_specs, out_specs=out_specs, out_shape=out_shape,
        scratch_shapes=scratch,
        compiler_params=pltpu.CompilerParams(dimension_semantics=("arbitrary",), vmem_limit_bytes=VMEM_LIMIT),
    )(*args)


def _adamw(w, g, m, v):
    m = ADAM_B1 * m + (1.0 - ADAM_B1) * g
    v = ADAM_B2 * v + (1.0 - ADAM_B2) * (g * g)
    m_hat = m / (1.0 - ADAM_B1 ** ADAM_STEP)
    v_hat = v / (1.0 - ADAM_B2 ** ADAM_STEP)
    delta = -ADAM_LR * (m_hat / (jnp.sqrt(v_hat) + ADAM_EPS) + ADAM_WD * w)
    return delta, m, v


def _adam_sharded(c_all, dmods, ada, w_in_set, w_out_set):
    rows = D_MODEL // ADAM_PARTS

    def body(c_ref, dm_ref, wa_ref, ma_ref, va_ref, wi_ref, gi_ref, mi_ref, vi_ref, wo_ref, go_ref, mo_ref, vo_ref,
             ga_out, da_out, ma_out, va_out, di_out, mi_out, vi_out, do_out, mo_out, vo_out):
        l = pl.program_id(0)
        chip = 2 * lax.axis_index("x") + lax.axis_index("y")
        cv = c_ref[...]
        silu_c = (cv * _sigmoid(cv)).astype(BF16).astype(F32)
        pieces = []
        for k in range(W_ADA_COLS // HEAD):
            dk = dm_ref[l, (W_ADA_COLS // HEAD) * chip + k].astype(BF16).astype(F32)
            pieces.append(_dot_exact(silu_c, dk, TN))
        g = jnp.concatenate(pieces, axis=1)
        ga_out[...] = g
        da_out[...], ma_out[...], va_out[...] = _adamw(wa_ref[...], g, ma_ref[...], va_ref[...])
        di_out[...], mi_out[...], vi_out[...] = _adamw(wi_ref[...], gi_ref[...], mi_ref[...], vi_ref[...])
        do_out[...], mo_out[...], vo_out[...] = _adamw(wo_ref[...], go_ref[...], mo_ref[...], vo_ref[...])

    def blk(r, cols):
        return pl.BlockSpec((None, r, cols), lambda l, i: (l, i, 0))

    b_ada, b_in, b_out = blk(rows, W_ADA_COLS), blk(rows, W_IN_COLS), blk(W_OUT_ROWS // ADAM_PARTS, D_MODEL)
    shapes = [jax.ShapeDtypeStruct(a[0].shape, F32) for a in (ada, w_in_set, w_out_set)]
    return pl.pallas_call(
        body, name="adam_sharded", grid=(DEPTH, ADAM_PARTS),
        in_specs=[pl.BlockSpec((N_DEV, rows), lambda l, i: (0, i)), _const_in((DEPTH, DM_ROWS, N_DEV, HEAD))]
        + [b_ada] * 3 + [b_in] * 4 + [b_out] * 4,
        out_specs=[b_ada] * 4 + [b_in] * 3 + [b_out] * 3,
        out_shape=[shapes[0]] * 4 + [shapes[1]] * 3 + [shapes[2]] * 3,
        compiler_params=pltpu.CompilerParams(dimension_semantics=("arbitrary", "arbitrary"), vmem_limit_bytes=VMEM_LIMIT),
    )(c_all, dmods, *ada, *w_in_set, *w_out_set)


def _adam_small(packs, dmods, weights, ms, vs):
    n = len(weights)

    def body(*refs):
        dm_refs = refs[1]
        b = 2
        w_refs, m_refs, v_refs = refs[b:b + n], refs[b + n:b + 2 * n], refs[b + 2 * n:b + 3 * n]
        outs = refs[b + 3 * n:b + 3 * n + 4 * n + 1]
        pack_refs = refs[-1]
        pack_refs[...] = refs[0][...].astype(F32)
        g_refs, d_refs, nm_refs, nv_refs = outs[0:n], outs[n:2 * n], outs[2 * n:3 * n], outs[3 * n:4 * n]
        squares = dm_refs[DEPTH - 1, DM_LOSS]
        total = squares[0:1, 0:1]
        for d in range(1, N_DEV):
            total = total + squares[d:d + 1, 0:1]
        outs[4 * n][...] = total * (0.5 / D_MODEL)

        def lanes(l, row0, count):
            return jnp.concatenate([pack_refs.at[l][row0 + k:row0 + k + 1, :] for k in range(count)], axis=1)

        def update(idx, at, g):
            g_refs[idx][at] = g
            d_refs[idx][at], nm_refs[idx][at], nv_refs[idx][at] = _adamw(w_refs[idx][at], g, m_refs[idx][at], v_refs[idx][at])

        for l in range(DEPTH):
            row = (slice(l, l + 1), slice(None))
            g_b = None
            for d in range(N_DEV):
                part = dm_refs.at[l][0:DM_LOSS, d, :]
                g_b = part if g_b is None else g_b + part
            update(0, row, jnp.concatenate([g_b[k:k + 1, :] for k in range(DM_LOSS)], axis=1))
            for g in range(N_HEAD):
                update(1, (l, g), pack_refs.at[l][PK_W_POOL + g * HEAD:PK_W_POOL + (g + 1) * HEAD, :])
                update(5, (l, g), pack_refs.at[l][PK_W_SGU + g * HEAD:PK_W_SGU + (g + 1) * HEAD, :])
            update(2, row, lanes(l, PK_POOL_SCALE, N_HEAD))
            update(3, (l,), pack_refs.at[l][PK_SGU_LN_G:PK_SGU_LN_G + N_HEAD, :])
            update(4, (l,), pack_refs.at[l][PK_SGU_LN_B:PK_SGU_LN_B + N_HEAD, :])
            update(6, (l,), pack_refs.at[l][PK_B_SGU:PK_B_SGU + N_HEAD, :])
            update(7, row, lanes(l, PK_LN_G, D_MODEL // HEAD))
            update(8, row, lanes(l, PK_LN_B, D_MODEL // HEAD))

    vmem = pl.BlockSpec(memory_space=pltpu.VMEM)
    shapes = [jax.ShapeDtypeStruct(w.shape, F32) for w in weights]
    return pl.pallas_call(
        body, name="adam_small", in_specs=[vmem] * (2 + 3 * n), out_specs=[vmem] * (4 * n + 1),
        out_shape=shapes * 4 + [jax.ShapeDtypeStruct((1, 1), F32)],
        scratch_shapes=[pltpu.VMEM(packs.shape, F32)],
        compiler_params=pltpu.CompilerParams(vmem_limit_bytes=VMEM_LIMIT),
    )(packs, dmods, *weights, *ms, *vs)


def kernel(x, c, w_ada, b_ada, w_in, w_pool, pool_scale, sgu_ln_g, sgu_ln_b, w_sgu, b_sgu, w_out, ln_g, ln_b, loss_target, m_w_ada, m_b_ada, m_w_in, m_w_pool, m_pool_scale, m_sgu_ln_g, m_sgu_ln_b, m_w_sgu, m_b_sgu, m_w_out, m_ln_g, m_ln_b, v_w_ada, v_b_ada, v_w_in, v_w_pool, v_pool_scale, v_sgu_ln_g, v_sgu_ln_b, v_w_sgu, v_b_sgu, v_w_out, v_ln_g, v_ln_b):
    small = (w_pool, pool_scale, sgu_ln_g, sgu_ln_b, w_sgu, b_sgu)
    *saved0, w_in0, w_out0, w_in1, w_out1, mod, c_all = _forward_first(x, c, w_ada, b_ada, w_in, w_out, small, ln_g, ln_b)
    *saved1, dout, sq = _forward_last(saved0[3], mod, w_in1, w_out1, small, ln_g, ln_b, loss_target)

    dx1, *shared = _backward_layer(1, dout, saved1, mod, w_in1, w_out1, small, ln_g, sq=sq)
    dx0, h, cat, dy, dproj, pack, dmod = _backward_layer(0, dx1, saved0, mod, w_in0, w_out0, small, ln_g, shared=shared)
    g_in, g_out, pack, dmods = _wgrad_reduce(h, dproj, cat, dy, pack, dmod)

    g_out = g_out.reshape(DEPTH, W_OUT_ROWS, D_MODEL)
    big = _adam_sharded(c_all, dmods, (w_ada, m_w_ada, v_w_ada), (w_in, g_in, m_w_in, v_w_in), (w_out, g_out, m_w_out, v_w_out))
    ada, win, wout = big[0:4], (g_in, *big[4:7]), (g_out, *big[7:10])
    small_w = (b_ada, w_pool, pool_scale, sgu_ln_g, sgu_ln_b, w_sgu, b_sgu, ln_g, ln_b)
    small_m = (m_b_ada, m_w_pool, m_pool_scale, m_sgu_ln_g, m_sgu_ln_b, m_w_sgu, m_b_sgu, m_ln_g, m_ln_b)
    small_v = (v_b_ada, v_w_pool, v_pool_scale, v_sgu_ln_g, v_sgu_ln_b, v_w_sgu, v_b_sgu, v_ln_g, v_ln_b)
    res = _adam_small(pack, dmods, small_w, small_m, small_v)
    n = len(small_w)
    loss = res[4 * n].reshape(())

    def ordered(k):
        s = res[k * n:(k + 1) * n]
        return (ada[k], s[0], win[k], s[1], s[2], s[3], s[4], s[5], s[6], wout[k], s[7], s[8])

    return (loss, dx0[None], *ordered(0), *ordered(1), *ordered(2), *ordered(3))
```

```python
import jax
import jax.numpy as jnp
from jax import lax
from jax.experimental import pallas as pl
from jax.experimental.pallas import tpu as pltpu

F32 = jnp.float32
BF16 = jnp.bfloat16
MESH = pl.DeviceIdType.MESH

N_DEV = 8
N_CHIP = 4
DEPTH = 2
SEQ = 2048
D_MODEL = 1024
D_POOL = 512
D_PROJ = 2560
HEAD = 128
N_HEAD = 4
ROWS = 256
N_TILE = SEQ // ROWS
HALO = 16
W_IN_COLS = D_PROJ // N_CHIP
W_OUT_ROWS = D_MODEL // N_CHIP
W_ADA_COLS = 3 * D_MODEL // N_CHIP
HALF_IN = D_MODEL // 2
HALF_OUT = W_OUT_ROWS // 2
DEEPNORM_ALPHA = (2.0 * DEPTH) ** 0.25
LN_EPS = 1e-5
INV_SQRT2 = 0.7071067811865476
INV_SQRT_2PI = 0.3989422804014327

ADAM_LR = 0.001
ADAM_B1 = 0.9
ADAM_B2 = 0.999
ADAM_EPS = 1e-08
ADAM_WD = 0.01
ADAM_STEP = 10
ADAM_PARTS = 2

PK_W_POOL = 0
PK_W_SGU = 512
PK_POOL_SCALE = 1024
PK_SGU_LN_G = 1032
PK_SGU_LN_B = 1040
PK_B_SGU = 1048
PK_LN_G = 1056
PK_LN_B = 1064
PK_ROWS = 1152
PK_HALF = PK_ROWS // 2
PK_PIECE = PK_HALF // N_CHIP
DM_LOSS = 3 * D_MODEL // HEAD
DM_ROWS = DM_LOSS + 1

VMEM_LIMIT = 56 * 1024 * 1024

GATHER_SECOND_ROUND_STEP = 0
GATHER_PASS_STEP = N_TILE - 2

NN = (((1,), (0,)), ((), ()))
NT = (((1,), (1,)), ((), ()))
TN = (((0,), (0,)), ((), ()))


def _dot(a, b, dims=NN):
    return lax.dot_general(a, b, dims, preferred_element_type=F32)


def _dot_exact(a, b, dims=NN):
    return lax.dot_general(a, b, dims, preferred_element_type=F32, precision=lax.Precision.HIGHEST)


def _layer_norm(v):
    mu = jnp.mean(v, axis=-1, keepdims=True)
    d = v - mu
    var = jnp.mean(d * d, axis=-1, keepdims=True)
    rstd = lax.rsqrt(var + LN_EPS)
    return d * rstd, rstd


def _layer_norm_bwd(dvhat, vhat, rstd):
    m1 = jnp.mean(dvhat, axis=-1, keepdims=True)
    m2 = jnp.mean(dvhat * vhat, axis=-1, keepdims=True)
    return rstd * (dvhat - m1 - vhat * m2)


def _sigmoid(v):
    return 1.0 / (1.0 + jnp.exp(-v))


def _gelu_parts(v):
    phi = 0.5 * (1.0 + lax.erf(v * INV_SQRT2))
    pdf = INV_SQRT_2PI * jnp.exp(-0.5 * v * v)
    return phi, pdf


def _sum_rows(v):
    return jnp.sum(v, axis=0, keepdims=True)


def _window_sums(ext, toward_later):
    n = ext.shape[0]

    def shifted(v, k):
        return pltpu.roll(v, (n - k) if toward_later else k, 0)

    s2 = ext + shifted(ext, 1)
    r4 = s2[:, HEAD:]
    s4 = r4 + shifted(r4, 2)
    r8 = s4[:, HEAD:]
    s8 = r8 + shifted(r8, 4)
    r16 = s8[:, HEAD:]
    s16 = r16 + shifted(r16, 8)
    return jnp.concatenate([s2[:, :HEAD], s4[:, :HEAD], s8[:, :HEAD], s16], axis=1)


def _window_counts(row0):
    t1 = row0 + 1 + lax.broadcasted_iota(jnp.int32, (ROWS, D_POOL), 0)
    lane = lax.broadcasted_iota(jnp.int32, (ROWS, D_POOL), 1)
    width = jnp.where(lane < HEAD, 2, jnp.where(lane < 2 * HEAD, 4, jnp.where(lane < 3 * HEAD, 8, 16)))
    return jnp.minimum(t1, width).astype(F32)


def _causal_mask():
    r = lax.broadcasted_iota(jnp.int32, (HEAD, HEAD), 0)
    s = lax.broadcasted_iota(jnp.int32, (HEAD, HEAD), 1)
    return r >= s


def _chunks_to_lanes(v):
    return jnp.concatenate([v[n * HEAD:(n + 1) * HEAD] for n in range(ROWS // HEAD)], axis=1)


def _lanes_to_chunks(v):
    return jnp.concatenate([v[:, n * HEAD:(n + 1) * HEAD] for n in range(ROWS // HEAD)], axis=0)


def _pack_stats(rstd_x, rstd_z, rstd_v):
    lane = lax.broadcasted_iota(jnp.int32, (ROWS, HEAD), 1)
    packed = rstd_x
    for k, r in enumerate([rstd_z] + list(rstd_v)):
        packed = jnp.where(lane < 16 * (k + 1), packed, r)
    return packed


def _unpack_stats(stats):
    cols = [stats[:, 16 * k:16 * k + 1] for k in range(2 + N_HEAD)]
    return cols[0], cols[1], cols[2:]


def _mixer(proj, halo, row0, wpool_ref, pscale, sgu_g_ref, sgu_b_ref, wsgu_ref, bsgu_ref, saved=None):
    xa = proj[:, 0:512]
    ga = proj[:, 512:1024]
    u = proj[:, 1024:1536]
    v = proj[:, 1536:2048]
    gb = proj[:, 2048:2560]
    ext = jnp.concatenate([halo, xa], axis=0)
    win = _window_sums(ext, toward_later=False)[HALO:]
    cnt = _window_counts(row0)
    pooled = (win / cnt - xa).astype(BF16)
    pw = jnp.concatenate(
        [_dot(pooled[:, g * HEAD:(g + 1) * HEAD], wpool_ref[g].astype(BF16)) for g in range(N_HEAD)], axis=1)
    sig_a = _sigmoid(ga) if saved is None else saved["sig_a"]
    ya = pw * pscale * (ga * sig_a)
    phi_u, pdf_u = _gelu_parts(u)
    phi_v, pdf_v = _gelu_parts(v)
    gu = u * phi_u
    gv = v * phi_v
    sig_b = _sigmoid(gb) if saved is None else saved["sig_b"]
    silu_b = gb * sig_b
    mask = _causal_mask()
    diag = lax.broadcasted_iota(jnp.int32, (HEAD, HEAD), 0) == lax.broadcasted_iota(jnp.int32, (HEAD, HEAD), 1)
    vhat, rstd_v, vln_l, mixed = [], [], [], []
    for h in range(N_HEAD):
        if saved is None:
            vh, rh = _layer_norm(gv[:, h * HEAD:(h + 1) * HEAD])
        else:
            vh, rh = saved["vhat"][h], saved["rstd_v"][h]
        ln = (vh * sgu_g_ref[h:h + 1, :] + sgu_b_ref[h:h + 1, :]).astype(BF16)
        ln_l = _chunks_to_lanes(ln)
        wm = jnp.where(mask, wsgu_ref[h], 0.0).astype(BF16)
        bias = jnp.sum(jnp.where(diag, jnp.broadcast_to(bsgu_ref[h:h + 1, :], (HEAD, HEAD)), 0.0), axis=1, keepdims=True)
        mx = _lanes_to_chunks(_dot(wm, ln_l) + bias)
        vhat.append(vh)
        rstd_v.append(rh)
        vln_l.append(ln_l)
        mixed.append(mx)
    mixed = jnp.concatenate(mixed, axis=1)
    yb = gu * mixed * silu_b
    return dict(xa=xa, ga=ga, u=u, gb=gb, cnt=cnt, pooled=pooled, pw=pw, sig_a=sig_a, ya=ya, phi_u=phi_u, pdf_u=pdf_u,
                phi_v=phi_v, pdf_v=pdf_v, gu=gu, sig_b=sig_b, silu_b=silu_b, vhat=vhat, rstd_v=rstd_v, vln_l=vln_l,
                mixed=mixed, yb=yb, mask=mask)


def _const(shape, *index):
    lead = tuple(index) + (0,) * (len(shape) - len(index))
    return pl.BlockSpec(shape, lambda *_: lead)


def _const_in(shape, *index):
    lead = tuple(index) + (0,) * (len(shape) - len(index))
    return pl.BlockSpec(shape, lambda *_: lead, pipeline_mode=pl.Buffered(1))


def _layer_weight_specs(l):
    return [
        _const_in((None, N_HEAD, HEAD, HEAD), l),
        _const_in((DEPTH, D_POOL)),
        _const_in((None, N_HEAD, HEAD), l),
        _const_in((None, N_HEAD, HEAD), l),
        _const_in((None, N_HEAD, HEAD, HEAD), l),
        _const_in((None, N_HEAD, HEAD), l),
    ]


def _forward_tile(l, i, x_ref, mod_ref, win_ref, wout_ref, small_refs, lng_ref, lnb_ref, carry_ref, saved_refs):
    wpool_ref, pscale_ref, sgu_g_ref, sgu_b_ref, wsgu_ref, bsgu_ref = small_refs
    proj_ref, y_ref, xn_ref, zn_ref, stats_ref, sig_ref, vhat_ref = saved_refs
    x = x_ref[...]
    if l > 0:
        x = x * lng_ref[l - 1:l, :] + lnb_ref[l - 1:l, :]
    shift, scale, gate = mod_ref[0:1, :], mod_ref[1:2, :], mod_ref[2:3, :]
    xn, rstd_x = _layer_norm(x)
    xn_ref[...] = xn.astype(xn_ref.dtype)
    h = xn * (1.0 + scale) + shift
    proj = _dot(h.astype(BF16), win_ref[...])
    proj_ref[...] = proj
    m = _mixer(proj, carry_ref[...], i * ROWS, wpool_ref, pscale_ref[l:l + 1, :], sgu_g_ref, sgu_b_ref, wsgu_ref, bsgu_ref)
    carry_ref[...] = m["xa"][ROWS - HALO:]
    sig_ref[...] = jnp.concatenate([m["sig_a"], m["sig_b"]], axis=1).astype(sig_ref.dtype)
    vhat_ref[...] = jnp.concatenate(m["vhat"], axis=1).astype(vhat_ref.dtype)
    cat = jnp.concatenate([m["ya"], m["yb"]], axis=1).astype(BF16)
    y = _dot(cat, wout_ref[...])
    y_ref[...] = y.astype(y_ref.dtype)
    zn, rstd_z = _layer_norm(DEEPNORM_ALPHA * x + gate * y)
    zn_ref[...] = zn
    stats_ref[...] = _pack_stats(rstd_x, rstd_z, m["rstd_v"])
    return zn


SAVED_COLS = (D_PROJ, D_MODEL, D_MODEL, D_MODEL, HEAD, D_MODEL, D_POOL)
SAVED_TYPES = (F32, BF16, BF16, F32, F32, BF16, BF16)


def _saved_outputs():
    return ([jax.ShapeDtypeStruct((SEQ, cols), t) for cols, t in zip(SAVED_COLS, SAVED_TYPES)],
            [pl.BlockSpec((ROWS, cols), lambda i: (i, 0)) for cols in SAVED_COLS])


def _forward_last(zn_prev, mod, w_in, w_out, small, ln_g, ln_b, target):
    l = DEPTH - 1
    n_saved = len(SAVED_COLS)

    def body(*refs):
        x_ref, mod_ref, win_ref, wout_ref = refs[:4]
        small_refs, lng_ref, lnb_ref, tgt_ref = refs[4:10], refs[10], refs[11], refs[12]
        saved_refs = refs[13:13 + n_saved]
        dout_ref, loss_ref, carry_ref = refs[13 + n_saved:]
        i = pl.program_id(0)

        @pl.when(i == 0)
        def _():
            carry_ref[...] = jnp.zeros_like(carry_ref)
            loss_ref[...] = jnp.zeros_like(loss_ref)

        zn = _forward_tile(l, i, x_ref, mod_ref, win_ref, wout_ref, small_refs, lng_ref, lnb_ref, carry_ref, saved_refs)
        err = zn * lng_ref[l:l + 1, :] + lnb_ref[l:l + 1, :] - tgt_ref[...]
        dout_ref[...] = err * (1.0 / D_MODEL)
        loss_ref[...] += jnp.sum(err * err)

    tile = pl.BlockSpec((ROWS, D_MODEL), lambda i: (i, 0))
    tile3 = pl.BlockSpec((None, ROWS, D_MODEL), lambda i: (0, i, 0))
    in_specs = [tile, _const_in((None, 8, D_MODEL), l), _const_in((D_MODEL, D_PROJ)), _const_in((D_MODEL, D_MODEL))]
    in_specs += _layer_weight_specs(l) + [_const_in((DEPTH, D_MODEL)), _const_in((DEPTH, D_MODEL)), tile3]
    out_shape, out_specs = _saved_outputs()
    out_shape += [jax.ShapeDtypeStruct((SEQ, D_MODEL), F32), jax.ShapeDtypeStruct((8, HEAD), F32)]
    out_specs += [tile, _const((8, HEAD))]
    return pl.pallas_call(
        body, name="fwd_last", grid=(N_TILE,), in_specs=in_specs, out_specs=out_specs, out_shape=out_shape,
        scratch_shapes=[pltpu.VMEM((HALO, D_POOL), F32)],
        compiler_params=pltpu.CompilerParams(dimension_semantics=("arbitrary",), vmem_limit_bytes=VMEM_LIMIT),
    )(zn_prev, mod, w_in, w_out, *small, ln_g, ln_b, target)


def _backward_layer(l, dout, saved, mod, w_in, w_out, small, ln_g, sq=None, shared=None):
    has_loss = sq is not None

    def body(*refs):
        (dout_ref, proj_ref, y_ref, xn_ref, zn_ref, stats_ref, sig_ref, vhat_ref, halo_ref, mod_ref, win_ref, wout_ref,
         wpool_ref, pscale_ref, sgu_g_ref, sgu_b_ref, wsgu_ref, bsgu_ref, lng_ref) = refs[:19]
        n_in = 20 if has_loss else 19 + 6
        dx_ref, h_ref, cat_ref, dy_ref, dproj_ref, pack_ref, dmod_ref, carry_ref = refs[n_in:n_in + 8]
        i = pl.program_id(0)
        tile = N_TILE - 1 - i

        @pl.when(i == 0)
        def _():
            carry_ref[...] = jnp.zeros_like(carry_ref)
            pack_ref[...] = jnp.zeros_like(pack_ref)
            dmod_ref[...] = jnp.zeros_like(dmod_ref)
            if has_loss:
                dmod_ref[3:4, 0:HEAD] = refs[19][0:1, :]

        xn = xn_ref[...].astype(F32)
        zn = zn_ref[...]
        y = y_ref[...].astype(F32)
        dout = dout_ref[...]
        rstd_x, rstd_z, rstd_v = _unpack_stats(stats_ref[...])
        kept = dict(sig_a=sig_ref[:, :D_POOL].astype(F32), sig_b=sig_ref[:, D_POOL:].astype(F32), rstd_v=rstd_v,
                    vhat=[vhat_ref[:, hd * HEAD:(hd + 1) * HEAD].astype(F32) for hd in range(N_HEAD)])
        pscale = pscale_ref[l:l + 1, :]
        shift, scale, gate = mod_ref[0:1, :], mod_ref[1:2, :], mod_ref[2:3, :]
        h = xn * (1.0 + scale) + shift
        h_ref[...] = h.astype(BF16)
        g_ln_g = _sum_rows(dout * zn)
        g_ln_b = _sum_rows(dout)
        dz = _layer_norm_bwd(dout * lng_ref[l:l + 1, :], zn, rstd_z)
        d_gate = _sum_rows(dz * y)
        dy = (gate * dz).astype(BF16)
        dy_ref[...] = dy

        halo = jnp.where(tile > 0, halo_ref[...], 0.0)
        m = _mixer(proj_ref[...], halo, tile * ROWS, wpool_ref, pscale, sgu_g_ref, sgu_b_ref, wsgu_ref, bsgu_ref,
                   saved=kept)
        cat_ref[...] = jnp.concatenate([m["ya"], m["yb"]], axis=1).astype(BF16)
        dcat = _dot(dy, wout_ref[...], NT)
        dya = dcat[:, :D_POOL]
        dyb = dcat[:, D_POOL:]

        ga, sig_a = m["ga"], m["sig_a"]
        dp = dya * (ga * sig_a)
        d_ga = dya * (m["pw"] * pscale) * (sig_a * (1.0 + ga * (1.0 - sig_a)))
        g_pscale = _sum_rows(dp * m["pw"])
        dpw = (dp * pscale).astype(BF16)
        dpooled = []
        for g in range(N_HEAD):
            cols = slice(g * HEAD, (g + 1) * HEAD)
            pack_ref[PK_W_POOL + g * HEAD:PK_W_POOL + (g + 1) * HEAD, :] += _dot(m["pooled"][:, cols], dpw[:, cols], TN)
            dpooled.append(_dot(dpw[:, cols], wpool_ref[g].astype(BF16), NT))
        dpooled = jnp.concatenate(dpooled, axis=1)
        q = dpooled / m["cnt"]
        ext = jnp.concatenate([q, carry_ref[...]], axis=0)
        d_xa = _window_sums(ext, toward_later=True)[:ROWS] - dpooled
        carry_ref[...] = q[:HALO]

        gu, mixed, silu_b, gb, sig_b = m["gu"], m["mixed"], m["silu_b"], m["gb"], m["sig_b"]
        d_mixed = dyb * gu * silu_b
        d_gu = dyb * mixed * silu_b
        d_gb = dyb * gu * mixed * (sig_b * (1.0 + gb * (1.0 - sig_b)))
        d_u = d_gu * (m["phi_u"] + m["u"] * m["pdf_u"])
        ones = jnp.ones((8, HEAD), F32)
        d_v = []
        for hd in range(N_HEAD):
            cols = slice(hd * HEAD, (hd + 1) * HEAD)
            dm = d_mixed[:, cols]
            dm_l = _chunks_to_lanes(dm.astype(BF16))
            g_w = _dot(dm_l, m["vln_l"][hd], NT)
            pack_ref[PK_W_SGU + hd * HEAD:PK_W_SGU + (hd + 1) * HEAD, :] += jnp.where(m["mask"], g_w, 0.0)
            dm_sum = dm[0:HEAD]
            for n in range(1, ROWS // HEAD):
                dm_sum = dm_sum + dm[n * HEAD:(n + 1) * HEAD]
            pack_ref[PK_B_SGU + hd:PK_B_SGU + hd + 1, :] += _dot_exact(ones, dm_sum, NT)[0:1]
            wm = jnp.where(m["mask"], wsgu_ref[hd], 0.0).astype(BF16)
            d_vln = _lanes_to_chunks(_dot(wm, dm_l, TN))
            vhat = m["vhat"][hd]
            pack_ref[PK_SGU_LN_G + hd:PK_SGU_LN_G + hd + 1, :] += _sum_rows(d_vln * vhat)
            pack_ref[PK_SGU_LN_B + hd:PK_SGU_LN_B + hd + 1, :] += _sum_rows(d_vln)
            d_v.append(_layer_norm_bwd(d_vln * sgu_g_ref[hd:hd + 1, :], vhat, m["rstd_v"][hd]))
        v = proj_ref[:, 1536:2048]
        d_v = jnp.concatenate(d_v, axis=1) * (m["phi_v"] + v * m["pdf_v"])

        dproj = jnp.concatenate([d_xa, d_ga, d_u, d_v, d_gb], axis=1).astype(BF16)
        dproj_ref[...] = dproj
        dh = _dot(dproj, win_ref[...], NT)
        d_scale = _sum_rows(dh * xn)
        d_shift = _sum_rows(dh)
        dx_ref[...] = DEEPNORM_ALPHA * dz + _layer_norm_bwd(dh * (1.0 + scale), xn, rstd_x)

        dmod_ref[0:1, :] += d_shift
        dmod_ref[1:2, :] += d_scale
        dmod_ref[2:3, :] += d_gate
        for g in range(N_HEAD):
            pack_ref[PK_POOL_SCALE + g:PK_POOL_SCALE + g + 1, :] += g_pscale[:, g * HEAD:(g + 1) * HEAD]
        for k in range(D_MODEL // HEAD):
            pack_ref[PK_LN_G + k:PK_LN_G + k + 1, :] += g_ln_g[:, k * HEAD:(k + 1) * HEAD]
            pack_ref[PK_LN_B + k:PK_LN_B + k + 1, :] += g_ln_b[:, k * HEAD:(k + 1) * HEAD]

    def rev(i):
        return (N_TILE - 1 - i, 0)

    tile = pl.BlockSpec((ROWS, D_MODEL), rev)
    halo = pl.BlockSpec((HALO, D_POOL), lambda i: (jnp.maximum((N_TILE - 1 - i) * (ROWS // HALO) - 1, 0), 0))
    in_specs = [tile] + [pl.BlockSpec((ROWS, a.shape[1]), rev) for a in saved] + [halo]
    in_specs += [_const_in((None, 8, D_MODEL), l), _const_in((D_MODEL, D_PROJ)), _const_in((D_MODEL, D_MODEL))]
    in_specs += _layer_weight_specs(l) + [_const_in((DEPTH, D_MODEL))]
    args = [dout, *saved, saved[0], mod, w_in, w_out, *small, ln_g]
    stacked = lambda cols: pl.BlockSpec((None, ROWS, cols), lambda i: (l, N_TILE - 1 - i, 0))
    out_shape = [jax.ShapeDtypeStruct((SEQ, D_MODEL), F32), jax.ShapeDtypeStruct((DEPTH, SEQ, D_MODEL), BF16),
                 jax.ShapeDtypeStruct((DEPTH, SEQ, D_MODEL), BF16), jax.ShapeDtypeStruct((DEPTH, SEQ, D_MODEL), BF16),
                 jax.ShapeDtypeStruct((DEPTH, SEQ, D_PROJ), BF16), jax.ShapeDtypeStruct((DEPTH, PK_ROWS, HEAD), F32),
                 jax.ShapeDtypeStruct((DEPTH, 8, D_MODEL), F32)]
    out_specs = [tile, stacked(D_MODEL), stacked(D_MODEL), stacked(D_MODEL), stacked(D_PROJ),
                 _const((None, PK_ROWS, HEAD), l), _const((None, 8, D_MODEL), l)]
    aliases = {}
    if has_loss:
        in_specs.append(_const_in((8, HEAD)))
        args.append(sq)
    else:
        aliases = {len(args) + k: 1 + k for k in range(len(shared))}
        in_specs += [pl.BlockSpec(memory_space=pl.ANY)] * len(shared)
        args += list(shared)
    return pl.pallas_call(
        body, name="bwd_last" if has_loss else "bwd_first", grid=(N_TILE,), in_specs=in_specs, out_specs=out_specs,
        out_shape=out_shape, scratch_shapes=[pltpu.VMEM((HALO, D_POOL), F32)], input_output_aliases=aliases,
        compiler_params=pltpu.CompilerParams(dimension_semantics=("arbitrary",), vmem_limit_bytes=VMEM_LIMIT),
    )(*args)


def _flip(v, f):
    return v + f - 2 * v * f


class _Place:
    def __init__(self):
        x, y, c = lax.axis_index("x"), lax.axis_index("y"), lax.axis_index("c")
        self.x, self.y, self.c = x, y, c
        self.chip = 2 * x + y
        self.dev = 4 * x + 2 * y + c
        self.sibling = (x, y, 1 - c)
        x1, y1 = _flip(x, 1 - c), _flip(y, c)
        x2, y2 = _flip(x, c), _flip(y, 1 - c)
        self.first = (x1, y1, c)
        self.second = (x2, y2, c)
        self.chip_first = 2 * x1 + y1
        self.chip_second = 2 * x2 + y2
        self.chip_far = 2 * (1 - x) + (1 - y)
        self.my_first_coord = jnp.where(c == 0, x, y)

    def first_coord(self, ch):
        return jnp.where(self.c == 0, ch // 2, ch % 2)

    def others(self):
        return [(_flip(self.x, (r >> 2) & 1), _flip(self.y, (r >> 1) & 1), _flip(self.c, r & 1)) for r in range(1, N_DEV)]

    def other_chips(self):
        return [(1 - self.x, self.y), (self.x, 1 - self.y), (1 - self.x, 1 - self.y)]


class _WeightGather:
    CHUNKS = 4
    N_SEMS = 12 * CHUNKS

    def __init__(self, place, win, wout, send, recv):
        self.p, self.win, self.wout, self.send, self.recv = place, win, wout, send, recv
        p = place
        self.plan = [(p.chip, p.first), (p.chip, p.second), (p.chip_first, p.second),
                     (p.chip_first, p.sibling), (p.chip_second, p.sibling), (p.chip_far, p.sibling)]

    def _copies(self, k, q):
        ch, target = self.plan[k]
        n_in, n_out = HALF_IN // self.CHUNKS, HALF_OUT // self.CHUNKS
        rows_in = pl.ds(pl.multiple_of(self.p.c * HALF_IN + q * n_in, n_in), n_in)
        cols_in = pl.ds(pl.multiple_of(ch * W_IN_COLS, 128), W_IN_COLS)
        rows_out = pl.ds(pl.multiple_of(ch * W_OUT_ROWS + self.p.c * HALF_OUT + q * n_out, n_out), n_out)
        r_in = self.win.at[rows_in, cols_in]
        r_out = self.wout.at[rows_out, :]
        s = 2 * (6 * q + k)
        return [pltpu.make_async_remote_copy(r_in, r_in, self.send.at[s], self.recv.at[s],
                                             device_id=target, device_id_type=MESH),
                pltpu.make_async_remote_copy(r_out, r_out, self.send.at[s + 1], self.recv.at[s + 1],
                                             device_id=target, device_id_type=MESH)]

    def _start(self, k, q):
        for cp in self._copies(k, q):
            cp.start()

    def _landed(self, k, q):
        for cp in self._copies(k, q):
            cp.wait_recv()

    def start_first_round(self):
        for q in range(self.CHUNKS):
            self._start(0, q)

    def start_second_round(self, q):
        self._landed(0, q)
        self._start(1, q)
        self._start(2, q)
        self._start(3, q)

    def pass_second_round(self, q):
        self._landed(1, q)
        self._start(4, q)
        self._landed(2, q)
        self._start(5, q)

    def finish(self):
        for q in range(self.CHUNKS):
            for k in (3, 4, 5):
                self._landed(k, q)
        for q in range(self.CHUNKS):
            for k in range(len(self.plan)):
                for cp in self._copies(k, q):
                    cp.wait_send()


def _forward_first(x, c_vec, w_ada, b_ada, w_in, w_out, small, ln_g, ln_b):
    n_saved = len(SAVED_COLS)

    def body(*refs):
        x_ref, c_ref, wada_hbm, bada_ref, win_hbm, wout_hbm = refs[:6]
        small_refs, lng_ref, lnb_ref = refs[6:12], refs[12], refs[13]
        saved_refs = refs[14:14 + n_saved]
        win0, wout0, win1, wout1, mod_out, c_out = refs[14 + n_saved:20 + n_saved]
        (carry_ref, wada_ref, win_ref, wout_ref, win_bf, wout_bf, mod_mine, mod_all, c_all, mod_ref, win_v, wout_v,
         g0_send, g0_recv, g1_send, g1_recv, c_send, c_recv, mod_send, mod_recv, local_sem) = refs[20 + n_saved:]
        i = pl.program_id(0)
        p = _Place()
        gather0 = _WeightGather(p, win0, wout0, g0_send, g0_recv)
        gather1 = _WeightGather(p, win1, wout1, g1_send, g1_recv)

        @pl.when(i == 0)
        def _():
            carry_ref[...] = jnp.zeros_like(carry_ref)
            loads = [pltpu.make_async_copy(win_hbm.at[0], win_ref.at[0], local_sem.at[4]),
                     pltpu.make_async_copy(wout_hbm.at[0], wout_ref.at[0], local_sem.at[5]),
                     pltpu.make_async_copy(win_hbm.at[1], win_ref.at[1], local_sem.at[6]),
                     pltpu.make_async_copy(wout_hbm.at[1], wout_ref.at[1], local_sem.at[7]),
                     pltpu.make_async_copy(wada_hbm, wada_ref, local_sem.at[8])]
            for cp in loads:
                cp.start()

            c_all[pl.ds(p.dev, 1), :] = c_ref[...]
            mine = c_all.at[pl.ds(p.dev, 1), :]
            c_copies = [pltpu.make_async_remote_copy(mine, mine, c_send.at[r], c_recv.at[r], device_id=d, device_id_type=MESH)
                        for r, d in enumerate(p.others())]
            for cp in c_copies:
                cp.start()

            cols = pl.ds(pl.multiple_of(p.chip * W_IN_COLS, 128), W_IN_COLS)
            rows = pl.ds(pl.multiple_of(p.chip * W_OUT_ROWS, W_OUT_ROWS), W_OUT_ROWS)
            own = [pltpu.make_async_copy(win_bf.at[0], win0.at[:, cols], local_sem.at[0]),
                   pltpu.make_async_copy(wout_bf.at[0], wout0.at[rows, :], local_sem.at[1]),
                   pltpu.make_async_copy(win_bf.at[1], win1.at[:, cols], local_sem.at[2]),
                   pltpu.make_async_copy(wout_bf.at[1], wout1.at[rows, :], local_sem.at[3])]
            for l in range(DEPTH):
                loads[2 * l].wait()
                win_bf[l] = win_ref[l].astype(BF16)
                own[2 * l].start()
                loads[2 * l + 1].wait()
                wout_bf[l] = wout_ref[l].astype(BF16)
                own[2 * l + 1].start()
                if l == 0:
                    own[0].wait()
                    own[1].wait()
                    gather0.start_first_round()
            for cp in c_copies:
                cp.wait()
            loads[4].wait()

            cv = c_all[...]
            c_out[...] = cv
            silu_c = (cv * _sigmoid(cv)).astype(BF16)
            for l in range(DEPTH):
                mod_mine[l] = _dot(silu_c, wada_ref[l].astype(BF16))
            mod_all[p.chip] = mod_mine[...]
            m_copies = [pltpu.make_async_remote_copy(mod_mine, mod_all.at[p.chip], mod_send.at[k], mod_recv.at[k],
                                                     device_id=(px, py, p.c), device_id_type=MESH)
                        for k, (px, py) in enumerate(p.other_chips())]
            for cp in m_copies:
                cp.start()
            for q in range(gather0.CHUNKS):
                gather0.start_second_round(q)
            own[2].wait()
            own[3].wait()
            gather1.start_first_round()
            for cp in m_copies:
                cp.wait()
            mod_ref[...] = jnp.zeros_like(mod_ref)
            for l in range(DEPTH):
                full = jnp.concatenate([mod_all[ch, l, pl.ds(p.dev, 1), :] for ch in range(N_CHIP)], axis=1) + bada_ref[l:l + 1, :]
                for k in range(3):
                    mod_ref[l, k:k + 1, :] = full[:, k * D_MODEL:(k + 1) * D_MODEL]
            mod_out[...] = mod_ref[...]
            for q in range(gather0.CHUNKS):
                gather0.pass_second_round(q)
            gather0.finish()
            fetch = [pltpu.make_async_copy(win0, win_v, local_sem.at[9]), pltpu.make_async_copy(wout0, wout_v, local_sem.at[10])]
            for cp in fetch:
                cp.start()
            for cp in fetch:
                cp.wait()

        for q in range(_WeightGather.CHUNKS):
            @pl.when(i == GATHER_SECOND_ROUND_STEP + q)
            def _(q=q):
                gather1.start_second_round(q)

            @pl.when(i == GATHER_PASS_STEP)
            def _(q=q):
                gather1.pass_second_round(q)

        _forward_tile(0, i, x_ref, mod_ref.at[0], win_v, wout_v, small_refs, lng_ref, lnb_ref, carry_ref, saved_refs)

        @pl.when(i == N_TILE - 1)
        def _():
            gather1.finish()

    hbm = pl.BlockSpec(memory_space=pl.ANY)
    tile3 = pl.BlockSpec((None, ROWS, D_MODEL), lambda i: (0, i, 0))
    in_specs = [tile3, _const_in((1, D_MODEL)), hbm, _const_in((DEPTH, 3 * D_MODEL)), hbm, hbm]
    in_specs += _layer_weight_specs(0) + [_const_in((DEPTH, D_MODEL)), _const_in((DEPTH, D_MODEL))]
    out_shape, out_specs = _saved_outputs()
    w_in_shape = jax.ShapeDtypeStruct((D_MODEL, D_PROJ), BF16)
    w_out_shape = jax.ShapeDtypeStruct((D_MODEL, D_MODEL), BF16)
    out_shape += [w_in_shape, w_out_shape, w_in_shape, w_out_shape,
                  jax.ShapeDtypeStruct((DEPTH, 8, D_MODEL), F32), jax.ShapeDtypeStruct((N_DEV, D_MODEL), F32)]
    out_specs += [hbm, hbm, hbm, hbm, _const((DEPTH, 8, D_MODEL)), _const((N_DEV, D_MODEL))]
    gather_sems = [pltpu.SemaphoreType.DMA((_WeightGather.N_SEMS,))] * 4
    scratch = [
        pltpu.VMEM((HALO, D_POOL), F32),
        pltpu.VMEM(w_ada.shape, F32), pltpu.VMEM(w_in.shape, F32), pltpu.VMEM(w_out.shape, F32),
        pltpu.VMEM((DEPTH, D_MODEL, W_IN_COLS), BF16), pltpu.VMEM((DEPTH, W_OUT_ROWS, D_MODEL), BF16),
        pltpu.VMEM((DEPTH, N_DEV, W_ADA_COLS), F32), pltpu.VMEM((N_CHIP, DEPTH, N_DEV, W_ADA_COLS), F32),
        pltpu.VMEM((N_DEV, D_MODEL), F32), pltpu.VMEM((DEPTH, 8, D_MODEL), F32),
        pltpu.VMEM((D_MODEL, D_PROJ), BF16), pltpu.VMEM((D_MODEL, D_MODEL), BF16),
    ] + gather_sems + [
        pltpu.SemaphoreType.DMA((7,)), pltpu.SemaphoreType.DMA((7,)),
        pltpu.SemaphoreType.DMA((3,)), pltpu.SemaphoreType.DMA((3,)),
        pltpu.SemaphoreType.DMA((11,)),
    ]
    return pl.pallas_call(
        body, name="fwd_first", grid=(N_TILE,), in_specs=in_specs, out_specs=out_specs, out_shape=out_shape,
        scratch_shapes=scratch,
        compiler_params=pltpu.CompilerParams(dimension_semantics=("arbitrary",), vmem_limit_bytes=VMEM_LIMIT),
    )(x, c_vec, w_ada, b_ada, w_in, w_out, *small, ln_g, ln_b)


IN_STEPS = W_IN_COLS // HEAD
OUT_STEPS = 4
OUT_COLS = D_MODEL // OUT_STEPS
OUT_FIRST = 2
ITEMS = ([("out", k) for k in range(OUT_FIRST)] + [("in", k) for k in range(IN_STEPS)]
         + [("out", k) for k in range(OUT_FIRST, OUT_STEPS)])
N_ITEMS = len(ITEMS)
N_STEPS = DEPTH * N_ITEMS
DELAY_SUM, DELAY_SECOND, DELAY_FINAL = 1, 3, 5
SMALL_SCATTER_STEP, SMALL_GATHER_STEP, SMALL_PASS_STEP, SMALL_FINISH_STEP = 1, 3, 5, 7


def _wgrad_reduce(h, dproj, cat, dy, pack, dmod):
    def body(*refs):
        h_ref, dp_refs, cat_ref, dy_ref, pack_ref, dmod_ref = refs[0], refs[1:5], refs[5], refs[6], refs[7], refs[8]
        fin_in, fin_out, pack_out, dmod_out = refs[9:13]
        scratch = refs[13:]
        (mine_in, send_in, sib_in, st_in, r1_in, r2_in, f_in,
         mine_out, send_out, sib_out, st_out, r1_out, r2_out, f_out,
         d2d_s, d2d_r, r1_s, r1_r, r2_s, r2_r, fin_l, fin_s, fin_r) = scratch[:23]
        p = _Place()
        c = p.c
        i = pl.program_id(0)
        my_rows = pl.ds(pl.multiple_of(c * HALF_IN, HALF_IN), HALF_IN)

        def layer_of(j):
            return DEPTH - 1 - j // N_ITEMS

        def bufs(j):
            kind, k = ITEMS[j % N_ITEMS]
            if kind == "in":
                return [r.at[k] for r in (mine_in, send_in, sib_in, st_in, r1_in, r2_in, f_in)]
            return [r.at[k] for r in (mine_out, send_out, sib_out, st_out, r1_out, r2_out, f_out)]

        def piece(j, ref, ch):
            if ITEMS[j % N_ITEMS][0] == "in":
                return ref.at[:, ch * HEAD:(ch + 1) * HEAD]
            return ref.at[ch]

        def slot(ch):
            return jnp.where(c == 0, ch % 2, ch // 2)

        def to_sibling(j):
            _, send, sib, _, _, _, _ = bufs(j)
            return pltpu.make_async_remote_copy(send, sib, d2d_s.at[j], d2d_r.at[j], device_id=p.sibling, device_id_type=MESH)

        def first_round(j, ch):
            _, _, _, st, r1, _, _ = bufs(j)
            k = slot(ch)
            return pltpu.make_async_remote_copy(st.at[k], r1.at[k], r1_s.at[2 * j + k], r1_r.at[2 * j + k],
                                                device_id=p.first, device_id_type=MESH)

        def second_round(j):
            _, _, _, st, _, r2, _ = bufs(j)
            return pltpu.make_async_remote_copy(st.at[2], r2, r2_s.at[j], r2_r.at[j], device_id=p.second, device_id_type=MESH)

        def finals(j):
            f = bufs(j)[6]
            kind, k = ITEMS[j % N_ITEMS]
            if kind == "in":
                dst = fin_in.at[layer_of(j), my_rows, k * HEAD:(k + 1) * HEAD]
            else:
                dst = fin_out.at[layer_of(j), c, :, k * OUT_COLS:(k + 1) * OUT_COLS]
            return [pltpu.make_async_copy(f, dst, fin_l.at[j]),
                    pltpu.make_async_remote_copy(f, dst, fin_s.at[j], fin_r.at[j], device_id=p.sibling, device_id_type=MESH)]

        def stage_sum(j):
            mine, _, sib, st, _, _, _ = bufs(j)
            to_sibling(j).wait_recv()
            mine[...] = mine[...] + sib[...]
            for ch in range(N_CHIP):
                @pl.when(p.first_coord(ch) != p.my_first_coord)
                def _(ch=ch):
                    st[slot(ch)] = piece(j, mine, ch)[...].astype(BF16)
                    first_round(j, ch).start()

        def stage_second(j):
            mine, _, _, st, r1, _, _ = bufs(j)
            for ch in range(N_CHIP):
                @pl.when(p.first_coord(ch) == p.my_first_coord)
                def _(ch=ch):
                    first_round(j, ch).wait_recv()
                    part = piece(j, mine, ch)
                    total = part[...] + r1[slot(ch)].astype(F32)
                    part[...] = total

                    @pl.when(ch != p.chip)
                    def _():
                        st[2] = total.astype(BF16)
                        second_round(j).start()

        def stage_final(j):
            mine, _, _, _, _, r2, f = bufs(j)
            second_round(j).wait_recv()
            for ch in range(N_CHIP):
                @pl.when(ch == p.chip)
                def _(ch=ch):
                    f[...] = piece(j, mine, ch)[...] + r2[...].astype(F32)
            for cp in finals(j):
                cp.start()

        def drain(j):
            to_sibling(j).wait_send()
            for ch in range(N_CHIP):
                @pl.when(p.first_coord(ch) != p.my_first_coord)
                def _(ch=ch):
                    first_round(j, ch).wait_send()

                @pl.when(jnp.logical_and(p.first_coord(ch) == p.my_first_coord, ch != p.chip))
                def _():
                    second_round(j).wait_send()
            for cp in finals(j):
                cp.wait()

        dev = p.dev
        devices = p.others()

        def half(core):
            return pl.ds(pl.multiple_of(core * PK_HALF, 16), PK_HALF)

        def finished(core, ch):
            return pl.ds(pl.multiple_of(core * PK_HALF + ch * PK_PIECE, 16), PK_PIECE)

        def small_exchange(l, first_step, bufs_l):
            (pk_mine, pk_sib, pk_st, pk_rs, pk_fin, pk_all, dm_st, dm_all, pk_sem, rs_s, rs_r, ag_s, ag_r, dm_s, dm_r) = bufs_l

            def pk_load():
                return pltpu.make_async_copy(pack_ref.at[l, half(c)], pk_mine, pk_sem.at[0])

            def pk_give():
                return pltpu.make_async_remote_copy(pack_ref.at[l, half(1 - c)], pk_sib, pk_sem.at[1], pk_sem.at[2],
                                                    device_id=p.sibling, device_id_type=MESH)

            def pk_scatter(ch):
                return pltpu.make_async_remote_copy(pk_st.at[ch * PK_PIECE:(ch + 1) * PK_PIECE], pk_rs.at[p.chip],
                                                    rs_s.at[ch], rs_r.at[p.chip], device_id=(ch // 2, ch % 2, c),
                                                    device_id_type=MESH)

            def pk_spread(ch):
                return pltpu.make_async_remote_copy(pk_fin, pk_all.at[finished(c, p.chip)], ag_s.at[ch], ag_r.at[p.chip],
                                                    device_id=(ch // 2, ch % 2, c), device_id_type=MESH)

            def pk_pass():
                return pltpu.make_async_remote_copy(pk_all.at[half(c)], pk_all.at[half(c)], pk_sem.at[3], pk_sem.at[4],
                                                    device_id=p.sibling, device_id_type=MESH)

            def dm_copy(r):
                return pltpu.make_async_remote_copy(dm_st, dm_all.at[:, pl.ds(dev, 1), :], dm_s.at[r], dm_r.at[r],
                                                    device_id=devices[r], device_id_type=MESH)

            def results():
                return [pltpu.make_async_copy(pk_all, pack_out.at[l], pk_sem.at[0]),
                        pltpu.make_async_copy(dm_all, dmod_out.at[l], pk_sem.at[5])]

            @pl.when(i == first_step)
            def _():
                pk_load().start()
                pk_give().start()
                for k in range(3):
                    for r in range(D_MODEL // HEAD):
                        dm_st[8 * k + r] = dmod_ref[l, k:k + 1, r * HEAD:(r + 1) * HEAD]
                dm_st[DM_LOSS] = dmod_ref[l, 3:4, 0:HEAD]
                dm_all[:, pl.ds(dev, 1), :] = dm_st[...]
                for r in range(N_DEV - 1):
                    dm_copy(r).start()

            @pl.when(i == first_step + SMALL_SCATTER_STEP)
            def _():
                pk_load().wait()
                pk_give().wait()
                total = pk_mine[...] + pk_sib[...]
                pk_mine[...] = total
                pk_st[...] = total.astype(BF16)
                for ch in range(N_CHIP):
                    @pl.when(ch != p.chip)
                    def _(ch=ch):
                        pk_scatter(ch).start()

            @pl.when(i == first_step + SMALL_GATHER_STEP)
            def _():
                for ch in range(N_CHIP):
                    @pl.when(ch != p.chip)
                    def _(ch=ch):
                        pltpu.make_async_remote_copy(pk_fin, pk_rs.at[ch], rs_s.at[ch], rs_r.at[ch],
                                                     device_id=p.sibling, device_id_type=MESH).wait_recv()
                for me in range(N_CHIP):
                    @pl.when(me == p.chip)
                    def _(me=me):
                        total = None
                        for ch in range(N_CHIP):
                            part = pk_mine[me * PK_PIECE:(me + 1) * PK_PIECE] if ch == me else pk_rs[ch].astype(F32)
                            total = part if total is None else total + part
                        pk_fin[...] = total.astype(BF16)
                        pk_all[finished(c, me)] = total.astype(BF16)
                for ch in range(N_CHIP):
                    @pl.when(ch != p.chip)
                    def _(ch=ch):
                        pk_spread(ch).start()

            @pl.when(i == first_step + SMALL_PASS_STEP)
            def _():
                for ch in range(N_CHIP):
                    @pl.when(ch != p.chip)
                    def _(ch=ch):
                        pltpu.make_async_remote_copy(pk_fin, pk_all.at[finished(c, ch)], ag_s.at[ch], ag_r.at[ch],
                                                     device_id=p.sibling, device_id_type=MESH).wait_recv()
                pk_pass().start()

            @pl.when(i == first_step + SMALL_FINISH_STEP)
            def _():
                pk_pass().wait()
                for ch in range(N_CHIP):
                    @pl.when(ch != p.chip)
                    def _(ch=ch):
                        pk_scatter(ch).wait_send()
                        pk_spread(ch).wait_send()
                for r in range(N_DEV - 1):
                    dm_copy(r).wait()
                for cp in results():
                    cp.start()
                for cp in results():
                    cp.wait()

        n_small = 15
        for l in range(DEPTH):
            small_exchange(l, (DEPTH - 1 - l) * N_ITEMS, scratch[23 + n_small * l:23 + n_small * (l + 1)])

        for step in range(N_ITEMS, N_STEPS):
            @pl.when(i == step)
            def _(step=step):
                drain(step - N_ITEMS)

        ii = jnp.where(i < N_ITEMS, i, i - N_ITEMS)
        in_step = jnp.logical_and(ii >= OUT_FIRST, ii < OUT_FIRST + IN_STEPS)

        @pl.when(in_step)
        def _():
            k = ii - OUT_FIRST
            rhs = jnp.concatenate([r[...] for r in dp_refs], axis=1)
            res = _dot(h_ref[...], rhs, TN)

            @pl.when(c == 0)
            def _():
                mine_in[k] = res[:HALF_IN]
                send_in[k] = res[HALF_IN:]

            @pl.when(c == 1)
            def _():
                mine_in[k] = res[HALF_IN:]
                send_in[k] = res[:HALF_IN]

        @pl.when(jnp.logical_not(in_step))
        def _():
            k = jnp.where(ii < OUT_FIRST, ii, ii - IN_STEPS)
            res = _dot(cat_ref[...], dy_ref[...], TN)

            @pl.when(c == 0)
            def _():
                for ch in range(N_CHIP):
                    mine_out[k, ch] = res[ch * W_OUT_ROWS:ch * W_OUT_ROWS + HALF_OUT]
                    send_out[k, ch] = res[ch * W_OUT_ROWS + HALF_OUT:(ch + 1) * W_OUT_ROWS]

            @pl.when(c == 1)
            def _():
                for ch in range(N_CHIP):
                    mine_out[k, ch] = res[ch * W_OUT_ROWS + HALF_OUT:(ch + 1) * W_OUT_ROWS]
                    send_out[k, ch] = res[ch * W_OUT_ROWS:ch * W_OUT_ROWS + HALF_OUT]

        stages = ((0, lambda j: to_sibling(j).start()), (DELAY_SUM, stage_sum), (DELAY_SECOND, stage_second),
                  (DELAY_FINAL, stage_final))
        for step in range(N_STEPS):
            @pl.when(i == step)
            def _(step=step):
                for delay, stage in stages:
                    if step - delay >= 0:
                        stage(step - delay)

        @pl.when(i == N_STEPS - 1)
        def _():
            for step in range(N_STEPS, N_STEPS + DELAY_FINAL):
                for delay, stage in stages:
                    if 0 <= step - delay < N_STEPS:
                        stage(step - delay)
            for j in range(N_STEPS - N_ITEMS, N_STEPS):
                drain(j)

    hbm = pl.BlockSpec(memory_space=pl.ANY)

    def layer(i):
        return jnp.where(i < N_ITEMS, DEPTH - 1, 0)

    def item(i):
        return jnp.where(i < N_ITEMS, i, i - N_ITEMS)

    def whole(i):
        return (layer(i), 0, 0)

    def dproj_piece(ch):
        return pl.BlockSpec((None, SEQ, HEAD),
                            lambda i: (layer(i), 0, ch * IN_STEPS + jnp.clip(item(i) - OUT_FIRST, 0, IN_STEPS - 1)))

    def dy_quarter(i):
        return (layer(i), 0, jnp.where(item(i) < OUT_FIRST, item(i), jnp.maximum(item(i) - IN_STEPS, OUT_FIRST)))

    operand = pl.BlockSpec((None, SEQ, D_MODEL), whole)
    in_specs = [operand] + [dproj_piece(ch) for ch in range(N_CHIP)]
    in_specs += [operand, pl.BlockSpec((None, SEQ, OUT_COLS), dy_quarter), hbm, _const_in((DEPTH, 8, D_MODEL))]
    args = [h, dproj, dproj, dproj, dproj, cat, dy, pack, dmod]
    out_shape = [jax.ShapeDtypeStruct((DEPTH, D_MODEL, W_IN_COLS), F32), jax.ShapeDtypeStruct((DEPTH, 2, HALF_OUT, D_MODEL), F32),
                 jax.ShapeDtypeStruct((DEPTH, PK_ROWS, HEAD), BF16), jax.ShapeDtypeStruct((DEPTH, DM_ROWS, N_DEV, HEAD), F32)]
    out_specs = [hbm, hbm, hbm, hbm]
    in_item = lambda *lead: pltpu.VMEM(lead + (HALF_IN, HEAD), BF16)
    out_item = lambda *lead: pltpu.VMEM(lead + (HALF_OUT, OUT_COLS), BF16)
    scratch = [
        pltpu.VMEM((IN_STEPS, HALF_IN, N_CHIP * HEAD), F32), pltpu.VMEM((IN_STEPS, HALF_IN, N_CHIP * HEAD), F32),
        pltpu.VMEM((IN_STEPS, HALF_IN, N_CHIP * HEAD), F32), in_item(IN_STEPS, 3), in_item(IN_STEPS, 2), in_item(IN_STEPS),
        pltpu.VMEM((IN_STEPS, HALF_IN, HEAD), F32),
        pltpu.VMEM((OUT_STEPS, N_CHIP, HALF_OUT, OUT_COLS), F32), pltpu.VMEM((OUT_STEPS, N_CHIP, HALF_OUT, OUT_COLS), F32),
        pltpu.VMEM((OUT_STEPS, N_CHIP, HALF_OUT, OUT_COLS), F32), out_item(OUT_STEPS, 3), out_item(OUT_STEPS, 2),
        out_item(OUT_STEPS), pltpu.VMEM((OUT_STEPS, HALF_OUT, OUT_COLS), F32),
        pltpu.SemaphoreType.DMA((N_STEPS,)), pltpu.SemaphoreType.DMA((N_STEPS,)),
        pltpu.SemaphoreType.DMA((2 * N_STEPS,)), pltpu.SemaphoreType.DMA((2 * N_STEPS,)),
        pltpu.SemaphoreType.DMA((N_STEPS,)), pltpu.SemaphoreType.DMA((N_STEPS,)),
        pltpu.SemaphoreType.DMA((N_STEPS,)), pltpu.SemaphoreType.DMA((N_STEPS,)), pltpu.SemaphoreType.DMA((N_STEPS,)),
    ]
    for _ in range(DEPTH):
        scratch += [
            pltpu.VMEM((PK_HALF, HEAD), F32), pltpu.VMEM((PK_HALF, HEAD), F32), pltpu.VMEM((PK_HALF, HEAD), BF16),
            pltpu.VMEM((N_CHIP, PK_PIECE, HEAD), BF16), pltpu.VMEM((PK_PIECE, HEAD), BF16), pltpu.VMEM((PK_ROWS, HEAD), BF16),
            pltpu.VMEM((DM_ROWS, 1, HEAD), F32), pltpu.VMEM((DM_ROWS, N_DEV, HEAD), F32),
            pltpu.SemaphoreType.DMA((6,)),
            pltpu.SemaphoreType.DMA((N_CHIP,)), pltpu.SemaphoreType.DMA((N_CHIP,)),
            pltpu.SemaphoreType.DMA((N_CHIP,)), pltpu.SemaphoreType.DMA((N_CHIP,)),
            pltpu.SemaphoreType.DMA((N_DEV - 1,)), pltpu.SemaphoreType.DMA((N_DEV - 1,)),
        ]
    return pl.pallas_call(
        body, name="wgrad", grid=(N_STEPS,), in_specs=in_specs, out_specs=out_specs, out_shape=out_shape,
        scratch_shapes=scratch,
        compiler_params=pltpu.CompilerParams(dimension_semantics=("arbitrary",), vmem_limit_bytes=VMEM_LIMIT),
    )(*args)


def _adamw(w, g, m, v):
    m = ADAM_B1 * m + (1.0 - ADAM_B1) * g
    v = ADAM_B2 * v + (1.0 - ADAM_B2) * (g * g)
    m_hat = m / (1.0 - ADAM_B1 ** ADAM_STEP)
    v_hat = v / (1.0 - ADAM_B2 ** ADAM_STEP)
    delta = -ADAM_LR * (m_hat / (jnp.sqrt(v_hat) + ADAM_EPS) + ADAM_WD * w)
    return delta, m, v


def _adam_sharded(c_all, dmods, ada, w_in_set, w_out_set):
    rows = D_MODEL // ADAM_PARTS

    def body(c_ref, dm_ref, wa_ref, ma_ref, va_ref, wi_ref, gi_ref, mi_ref, vi_ref, wo_ref, go_ref, mo_ref, vo_ref,
             ga_out, da_out, ma_out, va_out, di_out, mi_out, vi_out, do_out, mo_out, vo_out):
        l = pl.program_id(0)
        chip = 2 * lax.axis_index("x") + lax.axis_index("y")
        cv = c_ref[...]
        silu_c = (cv * _sigmoid(cv)).astype(BF16).astype(F32)
        pieces = []
        for k in range(W_ADA_COLS // HEAD):
            dk = dm_ref[l, (W_ADA_COLS // HEAD) * chip + k].astype(BF16).astype(F32)
            pieces.append(_dot_exact(silu_c, dk, TN))
        g = jnp.concatenate(pieces, axis=1)
        ga_out[...] = g
        da_out[...], ma_out[...], va_out[...] = _adamw(wa_ref[...], g, ma_ref[...], va_ref[...])
        di_out[...], mi_out[...], vi_out[...] = _adamw(wi_ref[...], gi_ref[...], mi_ref[...], vi_ref[...])
        do_out[...], mo_out[...], vo_out[...] = _adamw(wo_ref[...], go_ref[...], mo_ref[...], vo_ref[...])

    def blk(r, cols):
        return pl.BlockSpec((None, r, cols), lambda l, i: (l, i, 0))

    b_ada, b_in, b_out = blk(rows, W_ADA_COLS), blk(rows, W_IN_COLS), blk(W_OUT_ROWS // ADAM_PARTS, D_MODEL)
    shapes = [jax.ShapeDtypeStruct(a[0].shape, F32) for a in (ada, w_in_set, w_out_set)]
    return pl.pallas_call(
        body, name="adam_sharded", grid=(DEPTH, ADAM_PARTS),
        in_specs=[pl.BlockSpec((N_DEV, rows), lambda l, i: (0, i)), _const_in((DEPTH, DM_ROWS, N_DEV, HEAD))]
        + [b_ada] * 3 + [b_in] * 4 + [b_out] * 4,
        out_specs=[b_ada] * 4 + [b_in] * 3 + [b_out] * 3,
        out_shape=[shapes[0]] * 4 + [shapes[1]] * 3 + [shapes[2]] * 3,
        compiler_params=pltpu.CompilerParams(dimension_semantics=("arbitrary", "arbitrary"), vmem_limit_bytes=VMEM_LIMIT),
    )(c_all, dmods, *ada, *w_in_set, *w_out_set)


def _adam_small(packs, dmods, weights, ms, vs):
    n = len(weights)

    def body(*refs):
        dm_refs = refs[1]
        b = 2
        w_refs, m_refs, v_refs = refs[b:b + n], refs[b + n:b + 2 * n], refs[b + 2 * n:b + 3 * n]
        outs = refs[b + 3 * n:b + 3 * n + 4 * n + 1]
        pack_refs = refs[-1]
        pack_refs[...] = refs[0][...].astype(F32)
        g_refs, d_refs, nm_refs, nv_refs = outs[0:n], outs[n:2 * n], outs[2 * n:3 * n], outs[3 * n:4 * n]
        squares = dm_refs[DEPTH - 1, DM_LOSS]
        total = squares[0:1, 0:1]
        for d in range(1, N_DEV):
            total = total + squares[d:d + 1, 0:1]
        outs[4 * n][...] = total * (0.5 / D_MODEL)

        def lanes(l, row0, count):
            return jnp.concatenate([pack_refs.at[l][row0 + k:row0 + k + 1, :] for k in range(count)], axis=1)

        def update(idx, at, g):
            g_refs[idx][at] = g
            d_refs[idx][at], nm_refs[idx][at], nv_refs[idx][at] = _adamw(w_refs[idx][at], g, m_refs[idx][at], v_refs[idx][at])

        for l in range(DEPTH):
            row = (slice(l, l + 1), slice(None))
            g_b = None
            for d in range(N_DEV):
                part = dm_refs.at[l][0:DM_LOSS, d, :]
                g_b = part if g_b is None else g_b + part
            update(0, row, jnp.concatenate([g_b[k:k + 1, :] for k in range(DM_LOSS)], axis=1))
            for g in range(N_HEAD):
                update(1, (l, g), pack_refs.at[l][PK_W_POOL + g * HEAD:PK_W_POOL + (g + 1) * HEAD, :])
                update(5, (l, g), pack_refs.at[l][PK_W_SGU + g * HEAD:PK_W_SGU + (g + 1) * HEAD, :])
            update(2, row, lanes(l, PK_POOL_SCALE, N_HEAD))
            update(3, (l,), pack_refs.at[l][PK_SGU_LN_G:PK_SGU_LN_G + N_HEAD, :])
            update(4, (l,), pack_refs.at[l][PK_SGU_LN_B:PK_SGU_LN_B + N_HEAD, :])
            update(6, (l,), pack_refs.at[l][PK_B_SGU:PK_B_SGU + N_HEAD, :])
            update(7, row, lanes(l, PK_LN_G, D_MODEL // HEAD))
            update(8, row, lanes(l, PK_LN_B, D_MODEL // HEAD))

    vmem = pl.BlockSpec(memory_space=pltpu.VMEM)
    shapes = [jax.ShapeDtypeStruct(w.shape, F32) for w in weights]
    return pl.pallas_call(
        body, name="adam_small", in_specs=[vmem] * (2 + 3 * n), out_specs=[vmem] * (4 * n + 1),
        out_shape=shapes * 4 + [jax.ShapeDtypeStruct((1, 1), F32)],
        scratch_shapes=[pltpu.VMEM(packs.shape, F32)],
        compiler_params=pltpu.CompilerParams(vmem_limit_bytes=VMEM_LIMIT),
    )(packs, dmods, *weights, *ms, *vs)


def kernel(x, c, w_ada, b_ada, w_in, w_pool, pool_scale, sgu_ln_g, sgu_ln_b, w_sgu, b_sgu, w_out, ln_g, ln_b, loss_target, m_w_ada, m_b_ada, m_w_in, m_w_pool, m_pool_scale, m_sgu_ln_g, m_sgu_ln_b, m_w_sgu, m_b_sgu, m_w_out, m_ln_g, m_ln_b, v_w_ada, v_b_ada, v_w_in, v_w_pool, v_pool_scale, v_sgu_ln_g, v_sgu_ln_b, v_w_sgu, v_b_sgu, v_w_out, v_ln_g, v_ln_b):
    small = (w_pool, pool_scale, sgu_ln_g, sgu_ln_b, w_sgu, b_sgu)
    *saved0, w_in0, w_out0, w_in1, w_out1, mod, c_all = _forward_first(x, c, w_ada, b_ada, w_in, w_out, small, ln_g, ln_b)
    *saved1, dout, sq = _forward_last(saved0[3], mod, w_in1, w_out1, small, ln_g, ln_b, loss_target)

    dx1, *shared = _backward_layer(1, dout, saved1, mod, w_in1, w_out1, small, ln_g, sq=sq)
    dx0, h, cat, dy, dproj, pack, dmod = _backward_layer(0, dx1, saved0, mod, w_in0, w_out0, small, ln_g, shared=shared)
    g_in, g_out, pack, dmods = _wgrad_reduce(h, dproj, cat, dy, pack, dmod)

    g_out = g_out.reshape(DEPTH, W_OUT_ROWS, D_MODEL)
    big = _adam_sharded(c_all, dmods, (w_ada, m_w_ada, v_w_ada), (w_in, g_in, m_w_in, v_w_in), (w_out, g_out, m_w_out, v_w_out))
    ada, win, wout = big[0:4], (g_in, *big[4:7]), (g_out, *big[7:10])
    small_w = (b_ada, w_pool, pool_scale, sgu_ln_g, sgu_ln_b, w_sgu, b_sgu, ln_g, ln_b)
    small_m = (m_b_ada, m_w_pool, m_pool_scale, m_sgu_ln_g, m_sgu_ln_b, m_w_sgu, m_b_sgu, m_ln_g, m_ln_b)
    small_v = (v_b_ada, v_w_pool, v_pool_scale, v_sgu_ln_g, v_sgu_ln_b, v_w_sgu, v_b_sgu, v_ln_g, v_ln_b)
    res = _adam_small(pack, dmods, small_w, small_m, small_v)
    n = len(small_w)
    loss = res[4 * n].reshape(())

    def ordered(k):
        s = res[k * n:(k + 1) * n]
        return (ada[k], s[0], win[k], s[1], s[2], s[3], s[4], s[5], s[6], wout[k], s[7], s[8])

    return (loss, dx0[None], *ordered(0), *ordered(1), *ordered(2), *ordered(3))
```

```python
import jax
import jax.numpy as jnp
from jax import lax
from jax.experimental import pallas as pl
from jax.experimental.pallas import tpu as pltpu

F32 = jnp.float32
BF16 = jnp.bfloat16
MESH = pl.DeviceIdType.MESH

N_DEV = 8
N_CHIP = 4
DEPTH = 2
SEQ = 2048
D_MODEL = 1024
D_POOL = 512
D_PROJ = 2560
HEAD = 128
N_HEAD = 4
ROWS = 256
N_TILE = SEQ // ROWS
HALO = 16
W_IN_COLS = D_PROJ // N_CHIP
W_OUT_ROWS = D_MODEL // N_CHIP
W_ADA_COLS = 3 * D_MODEL // N_CHIP
HALF_IN = D_MODEL // 2
HALF_OUT = W_OUT_ROWS // 2
DEEPNORM_ALPHA = (2.0 * DEPTH) ** 0.25
LN_EPS = 1e-5
INV_SQRT2 = 0.7071067811865476
INV_SQRT_2PI = 0.3989422804014327

ADAM_LR = 0.001
ADAM_B1 = 0.9
ADAM_B2 = 0.999
ADAM_EPS = 1e-08
ADAM_WD = 0.01
ADAM_STEP = 10
ADAM_PARTS = 2

PK_W_POOL = 0
PK_W_SGU = 512
PK_POOL_SCALE = 1024
PK_SGU_LN_G = 1032
PK_SGU_LN_B = 1040
PK_B_SGU = 1048
PK_LN_G = 1056
PK_LN_B = 1064
PK_ROWS = 1152
PK_HALF = PK_ROWS // 2
PK_PIECE = PK_HALF // N_CHIP
DM_LOSS = 3 * D_MODEL // HEAD
DM_ROWS = DM_LOSS + 1

VMEM_LIMIT = 56 * 1024 * 1024

GATHER_SECOND_ROUND_STEP = 0
GATHER_PASS_STEP = N_TILE - 2

NN = (((1,), (0,)), ((), ()))
NT = (((1,), (1,)), ((), ()))
TN = (((0,), (0,)), ((), ()))


def _dot(a, b, dims=NN):
    return lax.dot_general(a, b, dims, preferred_element_type=F32)


def _dot_exact(a, b, dims=NN):
    return lax.dot_general(a, b, dims, preferred_element_type=F32, precision=lax.Precision.HIGHEST)


def _layer_norm(v):
    mu = jnp.mean(v, axis=-1, keepdims=True)
    d = v - mu
    var = jnp.mean(d * d, axis=-1, keepdims=True)
    rstd = lax.rsqrt(var + LN_EPS)
    return d * rstd, rstd


def _layer_norm_bwd(dvhat, vhat, rstd):
    m1 = jnp.mean(dvhat, axis=-1, keepdims=True)
    m2 = jnp.mean(dvhat * vhat, axis=-1, keepdims=True)
    return rstd * (dvhat - m1 - vhat * m2)


def _sigmoid(v):
    return 1.0 / (1.0 + jnp.exp(-v))


def _gelu_parts(v):
    phi = 0.5 * (1.0 + lax.erf(v * INV_SQRT2))
    pdf = INV_SQRT_2PI * jnp.exp(-0.5 * v * v)
    return phi, pdf


def _sum_rows(v):
    return jnp.sum(v, axis=0, keepdims=True)


def _window_sums(ext, toward_later):
    n = ext.shape[0]

    def shifted(v, k):
        return pltpu.roll(v, (n - k) if toward_later else k, 0)

    s2 = ext + shifted(ext, 1)
    r4 = s2[:, HEAD:]
    s4 = r4 + shifted(r4, 2)
    r8 = s4[:, HEAD:]
    s8 = r8 + shifted(r8, 4)
    r16 = s8[:, HEAD:]
    s16 = r16 + shifted(r16, 8)
    return jnp.concatenate([s2[:, :HEAD], s4[:, :HEAD], s8[:, :HEAD], s16], axis=1)


def _window_counts(row0):
    t1 = row0 + 1 + lax.broadcasted_iota(jnp.int32, (ROWS, D_POOL), 0)
    lane = lax.broadcasted_iota(jnp.int32, (ROWS, D_POOL), 1)
    width = jnp.where(lane < HEAD, 2, jnp.where(lane < 2 * HEAD, 4, jnp.where(lane < 3 * HEAD, 8, 16)))
    return jnp.minimum(t1, width).astype(F32)


def _causal_mask():
    r = lax.broadcasted_iota(jnp.int32, (HEAD, HEAD), 0)
    s = lax.broadcasted_iota(jnp.int32, (HEAD, HEAD), 1)
    return r >= s


def _chunks_to_lanes(v):
    return jnp.concatenate([v[n * HEAD:(n + 1) * HEAD] for n in range(ROWS // HEAD)], axis=1)


def _lanes_to_chunks(v):
    return jnp.concatenate([v[:, n * HEAD:(n + 1) * HEAD] for n in range(ROWS // HEAD)], axis=0)


def _pack_stats(rstd_x, rstd_z, rstd_v):
    lane = lax.broadcasted_iota(jnp.int32, (ROWS, HEAD), 1)
    packed = rstd_x
    for k, r in enumerate([rstd_z] + list(rstd_v)):
        packed = jnp.where(lane < 16 * (k + 1), packed, r)
    return packed


def _unpack_stats(stats):
    cols = [stats[:, 16 * k:16 * k + 1] for k in range(2 + N_HEAD)]
    return cols[0], cols[1], cols[2:]


def _mixer(proj, halo, row0, wpool_ref, pscale, sgu_g_ref, sgu_b_ref, wsgu_ref, bsgu_ref, saved=None):
    xa = proj[:, 0:512]
    ga = proj[:, 512:1024]
    u = proj[:, 1024:1536]
    v = proj[:, 1536:2048]
    gb = proj[:, 2048:2560]
    ext = jnp.concatenate([halo, xa], axis=0)
    win = _window_sums(ext, toward_later=False)[HALO:]
    cnt = _window_counts(row0)
    pooled = (win / cnt - xa).astype(BF16)
    pw = jnp.concatenate(
        [_dot(pooled[:, g * HEAD:(g + 1) * HEAD], wpool_ref[g].astype(BF16)) for g in range(N_HEAD)], axis=1)
    sig_a = _sigmoid(ga) if saved is None else saved["sig_a"]
    ya = pw * pscale * (ga * sig_a)
    phi_u, pdf_u = _gelu_parts(u)
    phi_v, pdf_v = _gelu_parts(v)
    gu = u * phi_u
    gv = v * phi_v
    sig_b = _sigmoid(gb) if saved is None else saved["sig_b"]
    silu_b = gb * sig_b
    mask = _causal_mask()
    diag = lax.broadcasted_iota(jnp.int32, (HEAD, HEAD), 0) == lax.broadcasted_iota(jnp.int32, (HEAD, HEAD), 1)
    vhat, rstd_v, vln_l, mixed = [], [], [], []
    for h in range(N_HEAD):
        if saved is None:
            vh, rh = _layer_norm(gv[:, h * HEAD:(h + 1) * HEAD])
        else:
            vh, rh = saved["vhat"][h], saved["rstd_v"][h]
        ln = (vh * sgu_g_ref[h:h + 1, :] + sgu_b_ref[h:h + 1, :]).astype(BF16)
        ln_l = _chunks_to_lanes(ln)
        wm = jnp.where(mask, wsgu_ref[h], 0.0).astype(BF16)
        bias = jnp.sum(jnp.where(diag, jnp.broadcast_to(bsgu_ref[h:h + 1, :], (HEAD, HEAD)), 0.0), axis=1, keepdims=True)
        mx = _lanes_to_chunks(_dot(wm, ln_l) + bias)
        vhat.append(vh)
        rstd_v.append(rh)
        vln_l.append(ln_l)
        mixed.append(mx)
    mixed = jnp.concatenate(mixed, axis=1)
    yb = gu * mixed * silu_b
    return dict(xa=xa, ga=ga, u=u, v=v, gb=gb, cnt=cnt, pooled=pooled, pw=pw, sig_a=sig_a, ya=ya, phi_u=phi_u, pdf_u=pdf_u,
                phi_v=phi_v, pdf_v=pdf_v, gu=gu, sig_b=sig_b, silu_b=silu_b, vhat=vhat, rstd_v=rstd_v, vln_l=vln_l,
                mixed=mixed, yb=yb, mask=mask)


def _const(shape, *index):
    lead = tuple(index) + (0,) * (len(shape) - len(index))
    return pl.BlockSpec(shape, lambda *_: lead)


def _const_in(shape, *index):
    lead = tuple(index) + (0,) * (len(shape) - len(index))
    return pl.BlockSpec(shape, lambda *_: lead, pipeline_mode=pl.Buffered(1))


def _layer_weight_specs(l):
    return [
        _const_in((None, N_HEAD, HEAD, HEAD), l),
        _const_in((DEPTH, D_POOL)),
        _const_in((None, N_HEAD, HEAD), l),
        _const_in((None, N_HEAD, HEAD), l),
        _const_in((None, N_HEAD, HEAD, HEAD), l),
        _const_in((None, N_HEAD, HEAD), l),
    ]


def _forward_tile(l, i, x_ref, mod_ref, win_ref, wout_ref, small_refs, lng_ref, lnb_ref, carry_ref, saved_refs):
    wpool_ref, pscale_ref, sgu_g_ref, sgu_b_ref, wsgu_ref, bsgu_ref = small_refs
    proj_ref, y_ref, xn_ref, zn_ref, stats_ref, sig_ref, vhat_ref = saved_refs
    x = x_ref[...]
    if l > 0:
        x = x * lng_ref[l - 1:l, :] + lnb_ref[l - 1:l, :]
    shift, scale, gate = mod_ref[0:1, :], mod_ref[1:2, :], mod_ref[2:3, :]
    xn, rstd_x = _layer_norm(x)
    xn_ref[...] = xn.astype(xn_ref.dtype)
    h = xn * (1.0 + scale) + shift
    proj = _dot(h.astype(BF16), win_ref[...])
    proj_ref[...] = proj.astype(proj_ref.dtype)
    m = _mixer(proj, carry_ref[...], i * ROWS, wpool_ref, pscale_ref[l:l + 1, :], sgu_g_ref, sgu_b_ref, wsgu_ref, bsgu_ref)
    carry_ref[...] = m["xa"][ROWS - HALO:]
    sig_ref[...] = jnp.concatenate([m["sig_a"], m["sig_b"]], axis=1).astype(sig_ref.dtype)
    vhat_ref[...] = jnp.concatenate(m["vhat"], axis=1).astype(vhat_ref.dtype)
    cat = jnp.concatenate([m["ya"], m["yb"]], axis=1).astype(BF16)
    y = _dot(cat, wout_ref[...])
    y_ref[...] = y.astype(y_ref.dtype)
    zn, rstd_z = _layer_norm(DEEPNORM_ALPHA * x + gate * y)
    zn_ref[...] = zn
    stats_ref[...] = _pack_stats(rstd_x, rstd_z, m["rstd_v"])
    return zn


SAVED_COLS = (D_PROJ, D_MODEL, D_MODEL, D_MODEL, HEAD, D_MODEL, D_POOL)
SAVED_TYPES = (BF16, BF16, BF16, F32, F32, BF16, BF16)


def _saved_outputs():
    return ([jax.ShapeDtypeStruct((SEQ, cols), t) for cols, t in zip(SAVED_COLS, SAVED_TYPES)],
            [pl.BlockSpec((ROWS, cols), lambda i: (i, 0)) for cols in SAVED_COLS])


def _forward_last(zn_prev, mod, w_in, w_out, small, ln_g, ln_b, target):
    l = DEPTH - 1
    n_saved = len(SAVED_COLS)

    def body(*refs):
        x_ref, mod_ref, win_ref, wout_ref = refs[:4]
        small_refs, lng_ref, lnb_ref, tgt_ref = refs[4:10], refs[10], refs[11], refs[12]
        saved_refs = refs[13:13 + n_saved]
        dout_ref, loss_ref, carry_ref = refs[13 + n_saved:]
        i = pl.program_id(0)

        @pl.when(i == 0)
        def _():
            carry_ref[...] = jnp.zeros_like(carry_ref)
            loss_ref[...] = jnp.zeros_like(loss_ref)

        zn = _forward_tile(l, i, x_ref, mod_ref, win_ref, wout_ref, small_refs, lng_ref, lnb_ref, carry_ref, saved_refs)
        err = zn * lng_ref[l:l + 1, :] + lnb_ref[l:l + 1, :] - tgt_ref[...]
        dout_ref[...] = err * (1.0 / D_MODEL)
        loss_ref[...] += jnp.sum(err * err)

    tile = pl.BlockSpec((ROWS, D_MODEL), lambda i: (i, 0))
    tile3 = pl.BlockSpec((None, ROWS, D_MODEL), lambda i: (0, i, 0))
    in_specs = [tile, _const_in((None, 8, D_MODEL), l), _const_in((D_MODEL, D_PROJ)), _const_in((D_MODEL, D_MODEL))]
    in_specs += _layer_weight_specs(l) + [_const_in((DEPTH, D_MODEL)), _const_in((DEPTH, D_MODEL)), tile3]
    out_shape, out_specs = _saved_outputs()
    out_shape += [jax.ShapeDtypeStruct((SEQ, D_MODEL), F32), jax.ShapeDtypeStruct((8, HEAD), F32)]
    out_specs += [tile, _const((8, HEAD))]
    return pl.pallas_call(
        body, name="fwd_last", grid=(N_TILE,), in_specs=in_specs, out_specs=out_specs, out_shape=out_shape,
        scratch_shapes=[pltpu.VMEM((HALO, D_POOL), F32)],
        compiler_params=pltpu.CompilerParams(dimension_semantics=("arbitrary",), vmem_limit_bytes=VMEM_LIMIT),
    )(zn_prev, mod, w_in, w_out, *small, ln_g, ln_b, target)


def _backward_layer(l, dout, saved, mod, w_in, w_out, small, ln_g, sq=None, shared=None):
    has_loss = sq is not None

    def body(*refs):
        (dout_ref, proj_ref, y_ref, xn_ref, zn_ref, stats_ref, sig_ref, vhat_ref, halo_ref, mod_ref, win_ref, wout_ref,
         wpool_ref, pscale_ref, sgu_g_ref, sgu_b_ref, wsgu_ref, bsgu_ref, lng_ref) = refs[:19]
        n_in = 20 if has_loss else 19 + 6
        dx_ref, h_ref, cat_ref, dy_ref, dproj_ref, pack_ref, dmod_ref, carry_ref = refs[n_in:n_in + 8]
        i = pl.program_id(0)
        tile = N_TILE - 1 - i

        @pl.when(i == 0)
        def _():
            carry_ref[...] = jnp.zeros_like(carry_ref)
            pack_ref[...] = jnp.zeros_like(pack_ref)
            dmod_ref[...] = jnp.zeros_like(dmod_ref)
            if has_loss:
                dmod_ref[3:4, 0:HEAD] = refs[19][0:1, :]

        xn = xn_ref[...].astype(F32)
        zn = zn_ref[...]
        y = y_ref[...].astype(F32)
        dout = dout_ref[...]
        rstd_x, rstd_z, rstd_v = _unpack_stats(stats_ref[...])
        kept = dict(sig_a=sig_ref[:, :D_POOL].astype(F32), sig_b=sig_ref[:, D_POOL:].astype(F32), rstd_v=rstd_v,
                    vhat=[vhat_ref[:, hd * HEAD:(hd + 1) * HEAD].astype(F32) for hd in range(N_HEAD)])
        pscale = pscale_ref[l:l + 1, :]
        shift, scale, gate = mod_ref[0:1, :], mod_ref[1:2, :], mod_ref[2:3, :]
        h = xn * (1.0 + scale) + shift
        h_ref[...] = h.astype(BF16)
        g_ln_g = _sum_rows(dout * zn)
        g_ln_b = _sum_rows(dout)
        dz = _layer_norm_bwd(dout * lng_ref[l:l + 1, :], zn, rstd_z)
        d_gate = _sum_rows(dz * y)
        dy = (gate * dz).astype(BF16)
        dy_ref[...] = dy

        halo = jnp.where(tile > 0, halo_ref[...].astype(F32), 0.0)
        m = _mixer(proj_ref[...].astype(F32), halo, tile * ROWS, wpool_ref, pscale, sgu_g_ref, sgu_b_ref, wsgu_ref, bsgu_ref,
                   saved=kept)
        cat_ref[...] = jnp.concatenate([m["ya"], m["yb"]], axis=1).astype(BF16)
        dcat = _dot(dy, wout_ref[...], NT)
        dya = dcat[:, :D_POOL]
        dyb = dcat[:, D_POOL:]

        ga, sig_a = m["ga"], m["sig_a"]
        dp = dya * (ga * sig_a)
        d_ga = dya * (m["pw"] * pscale) * (sig_a * (1.0 + ga * (1.0 - sig_a)))
        g_pscale = _sum_rows(dp * m["pw"])
        dpw = (dp * pscale).astype(BF16)
        dpooled = []
        for g in range(N_HEAD):
            cols = slice(g * HEAD, (g + 1) * HEAD)
            pack_ref[PK_W_POOL + g * HEAD:PK_W_POOL + (g + 1) * HEAD, :] += _dot(m["pooled"][:, cols], dpw[:, cols], TN)
            dpooled.append(_dot(dpw[:, cols], wpool_ref[g].astype(BF16), NT))
        dpooled = jnp.concatenate(dpooled, axis=1)
        q = dpooled / m["cnt"]
        ext = jnp.concatenate([q, carry_ref[...]], axis=0)
        d_xa = _window_sums(ext, toward_later=True)[:ROWS] - dpooled
        carry_ref[...] = q[:HALO]

        gu, mixed, silu_b, gb, sig_b = m["gu"], m["mixed"], m["silu_b"], m["gb"], m["sig_b"]
        d_mixed = dyb * gu * silu_b
        d_gu = dyb * mixed * silu_b
        d_gb = dyb * gu * mixed * (sig_b * (1.0 + gb * (1.0 - sig_b)))
        d_u = d_gu * (m["phi_u"] + m["u"] * m["pdf_u"])
        ones = jnp.ones((8, HEAD), F32)
        d_v = []
        for hd in range(N_HEAD):
            cols = slice(hd * HEAD, (hd + 1) * HEAD)
            dm = d_mixed[:, cols]
            dm_l = _chunks_to_lanes(dm.astype(BF16))
            g_w = _dot(dm_l, m["vln_l"][hd], NT)
            pack_ref[PK_W_SGU + hd * HEAD:PK_W_SGU + (hd + 1) * HEAD, :] += jnp.where(m["mask"], g_w, 0.0)
            dm_sum = dm[0:HEAD]
            for n in range(1, ROWS // HEAD):
                dm_sum = dm_sum + dm[n * HEAD:(n + 1) * HEAD]
            pack_ref[PK_B_SGU + hd:PK_B_SGU + hd + 1, :] += _dot_exact(ones, dm_sum, NT)[0:1]
            wm = jnp.where(m["mask"], wsgu_ref[hd], 0.0).astype(BF16)
            d_vln = _lanes_to_chunks(_dot(wm, dm_l, TN))
            vhat = m["vhat"][hd]
            pack_ref[PK_SGU_LN_G + hd:PK_SGU_LN_G + hd + 1, :] += _sum_rows(d_vln * vhat)
            pack_ref[PK_SGU_LN_B + hd:PK_SGU_LN_B + hd + 1, :] += _sum_rows(d_vln)
            d_v.append(_layer_norm_bwd(d_vln * sgu_g_ref[hd:hd + 1, :], vhat, m["rstd_v"][hd]))
        v = m["v"]
        d_v = jnp.concatenate(d_v, axis=1) * (m["phi_v"] + v * m["pdf_v"])

        dproj = jnp.concatenate([d_xa, d_ga, d_u, d_v, d_gb], axis=1).astype(BF16)
        dproj_ref[...] = dproj
        dh = _dot(dproj, win_ref[...], NT)
        d_scale = _sum_rows(dh * xn)
        d_shift = _sum_rows(dh)
        dx_ref[...] = DEEPNORM_ALPHA * dz + _layer_norm_bwd(dh * (1.0 + scale), xn, rstd_x)

        dmod_ref[0:1, :] += d_shift
        dmod_ref[1:2, :] += d_scale
        dmod_ref[2:3, :] += d_gate
        for g in range(N_HEAD):
            pack_ref[PK_POOL_SCALE + g:PK_POOL_SCALE + g + 1, :] += g_pscale[:, g * HEAD:(g + 1) * HEAD]
        for k in range(D_MODEL // HEAD):
            pack_ref[PK_LN_G + k:PK_LN_G + k + 1, :] += g_ln_g[:, k * HEAD:(k + 1) * HEAD]
            pack_ref[PK_LN_B + k:PK_LN_B + k + 1, :] += g_ln_b[:, k * HEAD:(k + 1) * HEAD]

    def rev(i):
        return (N_TILE - 1 - i, 0)

    tile = pl.BlockSpec((ROWS, D_MODEL), rev)
    halo = pl.BlockSpec((HALO, D_POOL), lambda i: (jnp.maximum((N_TILE - 1 - i) * (ROWS // HALO) - 1, 0), 0))
    in_specs = [tile] + [pl.BlockSpec((ROWS, a.shape[1]), rev) for a in saved] + [halo]
    in_specs += [_const_in((None, 8, D_MODEL), l), _const_in((D_MODEL, D_PROJ)), _const_in((D_MODEL, D_MODEL))]
    in_specs += _layer_weight_specs(l) + [_const_in((DEPTH, D_MODEL))]
    args = [dout, *saved, saved[0], mod, w_in, w_out, *small, ln_g]
    stacked = lambda cols: pl.BlockSpec((None, ROWS, cols), lambda i: (l, N_TILE - 1 - i, 0))
    out_shape = [jax.ShapeDtypeStruct((SEQ, D_MODEL), F32), jax.ShapeDtypeStruct((DEPTH, SEQ, D_MODEL), BF16),
                 jax.ShapeDtypeStruct((DEPTH, SEQ, D_MODEL), BF16), jax.ShapeDtypeStruct((DEPTH, SEQ, D_MODEL), BF16),
                 jax.ShapeDtypeStruct((DEPTH, SEQ, D_PROJ), BF16), jax.ShapeDtypeStruct((DEPTH, PK_ROWS, HEAD), F32),
                 jax.ShapeDtypeStruct((DEPTH, 8, D_MODEL), F32)]
    out_specs = [tile, stacked(D_MODEL), stacked(D_MODEL), stacked(D_MODEL), stacked(D_PROJ),
                 _const((None, PK_ROWS, HEAD), l), _const((None, 8, D_MODEL), l)]
    aliases = {}
    if has_loss:
        in_specs.append(_const_in((8, HEAD)))
        args.append(sq)
    else:
        aliases = {len(args) + k: 1 + k for k in range(len(shared))}
        in_specs += [pl.BlockSpec(memory_space=pl.ANY)] * len(shared)
        args += list(shared)
    return pl.pallas_call(
        body, name="bwd_last" if has_loss else "bwd_first", grid=(N_TILE,), in_specs=in_specs, out_specs=out_specs,
        out_shape=out_shape, scratch_shapes=[pltpu.VMEM((HALO, D_POOL), F32)], input_output_aliases=aliases,
        compiler_params=pltpu.CompilerParams(dimension_semantics=("arbitrary",), vmem_limit_bytes=VMEM_LIMIT),
    )(*args)


def _flip(v, f):
    return v + f - 2 * v * f


class _Place:
    def __init__(self):
        x, y, c = lax.axis_index("x"), lax.axis_index("y"), lax.axis_index("c")
        self.x, self.y, self.c = x, y, c
        self.chip = 2 * x + y
        self.dev = 4 * x + 2 * y + c
        self.sibling = (x, y, 1 - c)
        x1, y1 = _flip(x, 1 - c), _flip(y, c)
        x2, y2 = _flip(x, c), _flip(y, 1 - c)
        self.first = (x1, y1, c)
        self.second = (x2, y2, c)
        self.chip_first = 2 * x1 + y1
        self.chip_second = 2 * x2 + y2
        self.chip_far = 2 * (1 - x) + (1 - y)
        self.my_first_coord = jnp.where(c == 0, x, y)

    def first_coord(self, ch):
        return jnp.where(self.c == 0, ch // 2, ch % 2)

    def others(self):
        return [(_flip(self.x, (r >> 2) & 1), _flip(self.y, (r >> 1) & 1), _flip(self.c, r & 1)) for r in range(1, N_DEV)]

    def other_chips(self):
        return [(1 - self.x, self.y), (self.x, 1 - self.y), (1 - self.x, 1 - self.y)]


class _WeightGather:
    CHUNKS = 4
    N_SEMS = 12 * CHUNKS

    def __init__(self, place, win, wout, send, recv):
        self.p, self.win, self.wout, self.send, self.recv = place, win, wout, send, recv
        p = place
        self.plan = [(p.chip, p.first), (p.chip, p.second), (p.chip_first, p.second),
                     (p.chip_first, p.sibling), (p.chip_second, p.sibling), (p.chip_far, p.sibling)]

    def _copies(self, k, q):
        ch, target = self.plan[k]
        n_in, n_out = HALF_IN // self.CHUNKS, HALF_OUT // self.CHUNKS
        rows_in = pl.ds(pl.multiple_of(self.p.c * HALF_IN + q * n_in, n_in), n_in)
        cols_in = pl.ds(pl.multiple_of(ch * W_IN_COLS, 128), W_IN_COLS)
        rows_out = pl.ds(pl.multiple_of(ch * W_OUT_ROWS + self.p.c * HALF_OUT + q * n_out, n_out), n_out)
        r_in = self.win.at[rows_in, cols_in]
        r_out = self.wout.at[rows_out, :]
        s = 2 * (6 * q + k)
        return [pltpu.make_async_remote_copy(r_in, r_in, self.send.at[s], self.recv.at[s],
                                             device_id=target, device_id_type=MESH),
                pltpu.make_async_remote_copy(r_out, r_out, self.send.at[s + 1], self.recv.at[s + 1],
                                             device_id=target, device_id_type=MESH)]

    def _start(self, k, q):
        for cp in self._copies(k, q):
            cp.start()

    def _landed(self, k, q):
        for cp in self._copies(k, q):
            cp.wait_recv()

    def start_first_round(self):
        for q in range(self.CHUNKS):
            self._start(0, q)

    def start_second_round(self, q):
        self._landed(0, q)
        self._start(1, q)
        self._start(2, q)
        self._start(3, q)

    def pass_second_round(self, q):
        self._landed(1, q)
        self._start(4, q)
        self._landed(2, q)
        self._start(5, q)

    def finish(self):
        for q in range(self.CHUNKS):
            for k in (3, 4, 5):
                self._landed(k, q)
        for q in range(self.CHUNKS):
            for k in range(len(self.plan)):
                for cp in self._copies(k, q):
                    cp.wait_send()


def _forward_first(x, c_vec, w_ada, b_ada, w_in, w_out, small, ln_g, ln_b):
    n_saved = len(SAVED_COLS)

    def body(*refs):
        x_ref, c_ref, wada_hbm, bada_ref, win_hbm, wout_hbm = refs[:6]
        small_refs, lng_ref, lnb_ref = refs[6:12], refs[12], refs[13]
        saved_refs = refs[14:14 + n_saved]
        win0, wout0, win1, wout1, mod_out, c_out = refs[14 + n_saved:20 + n_saved]
        (carry_ref, wada_ref, win_ref, wout_ref, win_bf, wout_bf, mod_mine, mod_all, c_all, mod_ref, win_v, wout_v,
         g0_send, g0_recv, g1_send, g1_recv, c_send, c_recv, mod_send, mod_recv, local_sem) = refs[20 + n_saved:]
        i = pl.program_id(0)
        p = _Place()
        gather0 = _WeightGather(p, win0, wout0, g0_send, g0_recv)
        gather1 = _WeightGather(p, win1, wout1, g1_send, g1_recv)

        @pl.when(i == 0)
        def _():
            carry_ref[...] = jnp.zeros_like(carry_ref)
            loads = [pltpu.make_async_copy(win_hbm.at[0], win_ref.at[0], local_sem.at[4]),
                     pltpu.make_async_copy(wout_hbm.at[0], wout_ref.at[0], local_sem.at[5]),
                     pltpu.make_async_copy(win_hbm.at[1], win_ref.at[1], local_sem.at[6]),
                     pltpu.make_async_copy(wout_hbm.at[1], wout_ref.at[1], local_sem.at[7]),
                     pltpu.make_async_copy(wada_hbm, wada_ref, local_sem.at[8])]
            for cp in loads:
                cp.start()

            c_all[pl.ds(p.dev, 1), :] = c_ref[...]
            mine = c_all.at[pl.ds(p.dev, 1), :]
            c_copies = [pltpu.make_async_remote_copy(mine, mine, c_send.at[r], c_recv.at[r], device_id=d, device_id_type=MESH)
                        for r, d in enumerate(p.others())]
            for cp in c_copies:
                cp.start()

            cols = pl.ds(pl.multiple_of(p.chip * W_IN_COLS, 128), W_IN_COLS)
            rows = pl.ds(pl.multiple_of(p.chip * W_OUT_ROWS, W_OUT_ROWS), W_OUT_ROWS)
            own = [pltpu.make_async_copy(win_bf.at[0], win0.at[:, cols], local_sem.at[0]),
                   pltpu.make_async_copy(wout_bf.at[0], wout0.at[rows, :], local_sem.at[1]),
                   pltpu.make_async_copy(win_bf.at[1], win1.at[:, cols], local_sem.at[2]),
                   pltpu.make_async_copy(wout_bf.at[1], wout1.at[rows, :], local_sem.at[3])]
            for l in range(DEPTH):
                loads[2 * l].wait()
                win_bf[l] = win_ref[l].astype(BF16)
                own[2 * l].start()
                loads[2 * l + 1].wait()
                wout_bf[l] = wout_ref[l].astype(BF16)
                own[2 * l + 1].start()
                if l == 0:
                    own[0].wait()
                    own[1].wait()
                    gather0.start_first_round()
            for cp in c_copies:
                cp.wait()
            loads[4].wait()

            cv = c_all[...]
            c_out[...] = cv
            silu_c = (cv * _sigmoid(cv)).astype(BF16)
            for l in range(DEPTH):
                mod_mine[l] = _dot(silu_c, wada_ref[l].astype(BF16))
            mod_all[p.chip] = mod_mine[...]
            m_copies = [pltpu.make_async_remote_copy(mod_mine, mod_all.at[p.chip], mod_send.at[k], mod_recv.at[k],
                                                     device_id=(px, py, p.c), device_id_type=MESH)
                        for k, (px, py) in enumerate(p.other_chips())]
            for cp in m_copies:
                cp.start()
            for q in range(gather0.CHUNKS):
                gather0.start_second_round(q)
            own[2].wait()
            own[3].wait()
            gather1.start_first_round()
            for cp in m_copies:
                cp.wait()
            mod_ref[...] = jnp.zeros_like(mod_ref)
            for l in range(DEPTH):
                full = jnp.concatenate([mod_all[ch, l, pl.ds(p.dev, 1), :] for ch in range(N_CHIP)], axis=1) + bada_ref[l:l + 1, :]
                for k in range(3):
                    mod_ref[l, k:k + 1, :] = full[:, k * D_MODEL:(k + 1) * D_MODEL]
            mod_out[...] = mod_ref[...]
            for q in range(gather0.CHUNKS):
                gather0.pass_second_round(q)
            gather0.finish()
            fetch = [pltpu.make_async_copy(win0, win_v, local_sem.at[9]), pltpu.make_async_copy(wout0, wout_v, local_sem.at[10])]
            for cp in fetch:
                cp.start()
            for cp in fetch:
                cp.wait()

        for q in range(_WeightGather.CHUNKS):
            @pl.when(i == GATHER_SECOND_ROUND_STEP + q)
            def _(q=q):
                gather1.start_second_round(q)

            @pl.when(i == GATHER_PASS_STEP)
            def _(q=q):
                gather1.pass_second_round(q)

        _forward_tile(0, i, x_ref, mod_ref.at[0], win_v, wout_v, small_refs, lng_ref, lnb_ref, carry_ref, saved_refs)

        @pl.when(i == N_TILE - 1)
        def _():
            gather1.finish()

    hbm = pl.BlockSpec(memory_space=pl.ANY)
    tile3 = pl.BlockSpec((None, ROWS, D_MODEL), lambda i: (0, i, 0))
    in_specs = [tile3, _const_in((1, D_MODEL)), hbm, _const_in((DEPTH, 3 * D_MODEL)), hbm, hbm]
    in_specs += _layer_weight_specs(0) + [_const_in((DEPTH, D_MODEL)), _const_in((DEPTH, D_MODEL))]
    out_shape, out_specs = _saved_outputs()
    w_in_shape = jax.ShapeDtypeStruct((D_MODEL, D_PROJ), BF16)
    w_out_shape = jax.ShapeDtypeStruct((D_MODEL, D_MODEL), BF16)
    out_shape += [w_in_shape, w_out_shape, w_in_shape, w_out_shape,
                  jax.ShapeDtypeStruct((DEPTH, 8, D_MODEL), F32), jax.ShapeDtypeStruct((N_DEV, D_MODEL), F32)]
    out_specs += [hbm, hbm, hbm, hbm, _const((DEPTH, 8, D_MODEL)), _const((N_DEV, D_MODEL))]
    gather_sems = [pltpu.SemaphoreType.DMA((_WeightGather.N_SEMS,))] * 4
    scratch = [
        pltpu.VMEM((HALO, D_POOL), F32),
        pltpu.VMEM(w_ada.shape, F32), pltpu.VMEM(w_in.shape, F32), pltpu.VMEM(w_out.shape, F32),
        pltpu.VMEM((DEPTH, D_MODEL, W_IN_COLS), BF16), pltpu.VMEM((DEPTH, W_OUT_ROWS, D_MODEL), BF16),
        pltpu.VMEM((DEPTH, N_DEV, W_ADA_COLS), F32), pltpu.VMEM((N_CHIP, DEPTH, N_DEV, W_ADA_COLS), F32),
        pltpu.VMEM((N_DEV, D_MODEL), F32), pltpu.VMEM((DEPTH, 8, D_MODEL), F32),
        pltpu.VMEM((D_MODEL, D_PROJ), BF16), pltpu.VMEM((D_MODEL, D_MODEL), BF16),
    ] + gather_sems + [
        pltpu.SemaphoreType.DMA((7,)), pltpu.SemaphoreType.DMA((7,)),
        pltpu.SemaphoreType.DMA((3,)), pltpu.SemaphoreType.DMA((3,)),
        pltpu.SemaphoreType.DMA((11,)),
    ]
    return pl.pallas_call(
        body, name="fwd_first", grid=(N_TILE,), in_specs=in_specs, out_specs=out_specs, out_shape=out_shape,
        scratch_shapes=scratch,
        compiler_params=pltpu.CompilerParams(dimension_semantics=("arbitrary",), vmem_limit_bytes=VMEM_LIMIT),
    )(x, c_vec, w_ada, b_ada, w_in, w_out, *small, ln_g, ln_b)


IN_STEPS = W_IN_COLS // HEAD
OUT_STEPS = 4
OUT_COLS = D_MODEL // OUT_STEPS
OUT_FIRST = 2
ITEMS = ([("out", k) for k in range(OUT_FIRST)] + [("in", k) for k in range(IN_STEPS)]
         + [("out", k) for k in range(OUT_FIRST, OUT_STEPS)])
N_ITEMS = len(ITEMS)
N_STEPS = DEPTH * N_ITEMS
DELAY_SUM, DELAY_SECOND, DELAY_FINAL = 1, 3, 5
SMALL_SCATTER_STEP, SMALL_GATHER_STEP, SMALL_PASS_STEP, SMALL_FINISH_STEP = 1, 3, 5, 7


def _wgrad_reduce(h, dproj, cat, dy, pack, dmod):
    def body(*refs):
        h_ref, dp_refs, cat_ref, dy_ref, pack_ref, dmod_ref = refs[0], refs[1:5], refs[5], refs[6], refs[7], refs[8]
        fin_in, fin_out, pack_out, dmod_out = refs[9:13]
        scratch = refs[13:]
        (mine_in, send_in, sib_in, st_in, r1_in, r2_in, f_in,
         mine_out, send_out, sib_out, st_out, r1_out, r2_out, f_out,
         d2d_s, d2d_r, r1_s, r1_r, r2_s, r2_r, fin_l, fin_s, fin_r) = scratch[:23]
        p = _Place()
        c = p.c
        i = pl.program_id(0)
        my_rows = pl.ds(pl.multiple_of(c * HALF_IN, HALF_IN), HALF_IN)

        def layer_of(j):
            return DEPTH - 1 - j // N_ITEMS

        def bufs(j):
            kind, k = ITEMS[j % N_ITEMS]
            if kind == "in":
                return [r.at[k] for r in (mine_in, send_in, sib_in, st_in, r1_in, r2_in, f_in)]
            return [r.at[k] for r in (mine_out, send_out, sib_out, st_out, r1_out, r2_out, f_out)]

        def piece(j, ref, ch):
            if ITEMS[j % N_ITEMS][0] == "in":
                return ref.at[:, ch * HEAD:(ch + 1) * HEAD]
            return ref.at[ch]

        def slot(ch):
            return jnp.where(c == 0, ch % 2, ch // 2)

        def to_sibling(j):
            _, send, sib, _, _, _, _ = bufs(j)
            return pltpu.make_async_remote_copy(send, sib, d2d_s.at[j], d2d_r.at[j], device_id=p.sibling, device_id_type=MESH)

        def first_round(j, ch):
            _, _, _, st, r1, _, _ = bufs(j)
            k = slot(ch)
            return pltpu.make_async_remote_copy(st.at[k], r1.at[k], r1_s.at[2 * j + k], r1_r.at[2 * j + k],
                                                device_id=p.first, device_id_type=MESH)

        def second_round(j):
            _, _, _, st, _, r2, _ = bufs(j)
            return pltpu.make_async_remote_copy(st.at[2], r2, r2_s.at[j], r2_r.at[j], device_id=p.second, device_id_type=MESH)

        def finals(j):
            f = bufs(j)[6]
            kind, k = ITEMS[j % N_ITEMS]
            if kind == "in":
                dst = fin_in.at[layer_of(j), my_rows, k * HEAD:(k + 1) * HEAD]
            else:
                dst = fin_out.at[layer_of(j), c, :, k * OUT_COLS:(k + 1) * OUT_COLS]
            return [pltpu.make_async_copy(f, dst, fin_l.at[j]),
                    pltpu.make_async_remote_copy(f, dst, fin_s.at[j], fin_r.at[j], device_id=p.sibling, device_id_type=MESH)]

        def stage_sum(j):
            mine, _, sib, st, _, _, _ = bufs(j)
            to_sibling(j).wait_recv()
            mine[...] = mine[...] + sib[...]
            for ch in range(N_CHIP):
                @pl.when(p.first_coord(ch) != p.my_first_coord)
                def _(ch=ch):
                    st[slot(ch)] = piece(j, mine, ch)[...].astype(BF16)
                    first_round(j, ch).start()

        def stage_second(j):
            mine, _, _, st, r1, _, _ = bufs(j)
            for ch in range(N_CHIP):
                @pl.when(p.first_coord(ch) == p.my_first_coord)
                def _(ch=ch):
                    first_round(j, ch).wait_recv()
                    part = piece(j, mine, ch)
                    total = part[...] + r1[slot(ch)].astype(F32)
                    part[...] = total

                    @pl.when(ch != p.chip)
                    def _():
                        st[2] = total.astype(BF16)
                        second_round(j).start()

        def stage_final(j):
            mine, _, _, _, _, r2, f = bufs(j)
            second_round(j).wait_recv()
            for ch in range(N_CHIP):
                @pl.when(ch == p.chip)
                def _(ch=ch):
                    f[...] = piece(j, mine, ch)[...] + r2[...].astype(F32)
            for cp in finals(j):
                cp.start()

        def drain(j):
            to_sibling(j).wait_send()
            for ch in range(N_CHIP):
                @pl.when(p.first_coord(ch) != p.my_first_coord)
                def _(ch=ch):
                    first_round(j, ch).wait_send()

                @pl.when(jnp.logical_and(p.first_coord(ch) == p.my_first_coord, ch != p.chip))
                def _():
                    second_round(j).wait_send()
            for cp in finals(j):
                cp.wait()

        dev = p.dev
        devices = p.others()

        def half(core):
            return pl.ds(pl.multiple_of(core * PK_HALF, 16), PK_HALF)

        def finished(core, ch):
            return pl.ds(pl.multiple_of(core * PK_HALF + ch * PK_PIECE, 16), PK_PIECE)

        def small_exchange(l, first_step, bufs_l):
            (pk_mine, pk_sib, pk_st, pk_rs, pk_fin, pk_all, dm_st, dm_all, pk_sem, rs_s, rs_r, ag_s, ag_r, dm_s, dm_r) = bufs_l

            def pk_load():
                return pltpu.make_async_copy(pack_ref.at[l, half(c)], pk_mine, pk_sem.at[0])

            def pk_give():
                return pltpu.make_async_remote_copy(pack_ref.at[l, half(1 - c)], pk_sib, pk_sem.at[1], pk_sem.at[2],
                                                    device_id=p.sibling, device_id_type=MESH)

            def pk_scatter(ch):
                return pltpu.make_async_remote_copy(pk_st.at[ch * PK_PIECE:(ch + 1) * PK_PIECE], pk_rs.at[p.chip],
                                                    rs_s.at[ch], rs_r.at[p.chip], device_id=(ch // 2, ch % 2, c),
                                                    device_id_type=MESH)

            def pk_spread(ch):
                return pltpu.make_async_remote_copy(pk_fin, pk_all.at[finished(c, p.chip)], ag_s.at[ch], ag_r.at[p.chip],
                                                    device_id=(ch // 2, ch % 2, c), device_id_type=MESH)

            def pk_pass():
                return pltpu.make_async_remote_copy(pk_all.at[half(c)], pk_all.at[half(c)], pk_sem.at[3], pk_sem.at[4],
                                                    device_id=p.sibling, device_id_type=MESH)

            def dm_copy(r):
                return pltpu.make_async_remote_copy(dm_st, dm_all.at[:, pl.ds(dev, 1), :], dm_s.at[r], dm_r.at[r],
                                                    device_id=devices[r], device_id_type=MESH)

            def results():
                return [pltpu.make_async_copy(pk_all, pack_out.at[l], pk_sem.at[0]),
                        pltpu.make_async_copy(dm_all, dmod_out.at[l], pk_sem.at[5])]

            @pl.when(i == first_step)
            def _():
                pk_load().start()
                pk_give().start()
                for k in range(3):
                    for r in range(D_MODEL // HEAD):
                        dm_st[8 * k + r] = dmod_ref[l, k:k + 1, r * HEAD:(r + 1) * HEAD]
                dm_st[DM_LOSS] = dmod_ref[l, 3:4, 0:HEAD]
                dm_all[:, pl.ds(dev, 1), :] = dm_st[...]
                for r in range(N_DEV - 1):
                    dm_copy(r).start()

            @pl.when(i == first_step + SMALL_SCATTER_STEP)
            def _():
                pk_load().wait()
                pk_give().wait()
                total = pk_mine[...] + pk_sib[...]
                pk_mine[...] = total
                pk_st[...] = total.astype(BF16)
                for ch in range(N_CHIP):
                    @pl.when(ch != p.chip)
                    def _(ch=ch):
                        pk_scatter(ch).start()

            @pl.when(i == first_step + SMALL_GATHER_STEP)
            def _():
                for ch in range(N_CHIP):
                    @pl.when(ch != p.chip)
                    def _(ch=ch):
                        pltpu.make_async_remote_copy(pk_fin, pk_rs.at[ch], rs_s.at[ch], rs_r.at[ch],
                                                     device_id=p.sibling, device_id_type=MESH).wait_recv()
                for me in range(N_CHIP):
                    @pl.when(me == p.chip)
                    def _(me=me):
                        total = None
                        for ch in range(N_CHIP):
                            part = pk_mine[me * PK_PIECE:(me + 1) * PK_PIECE] if ch == me else pk_rs[ch].astype(F32)
                            total = part if total is None else total + part
                        pk_fin[...] = total.astype(BF16)
                        pk_all[finished(c, me)] = total.astype(BF16)
                for ch in range(N_CHIP):
                    @pl.when(ch != p.chip)
                    def _(ch=ch):
                        pk_spread(ch).start()

            @pl.when(i == first_step + SMALL_PASS_STEP)
            def _():
                for ch in range(N_CHIP):
                    @pl.when(ch != p.chip)
                    def _(ch=ch):
                        pltpu.make_async_remote_copy(pk_fin, pk_all.at[finished(c, ch)], ag_s.at[ch], ag_r.at[ch],
                                                     device_id=p.sibling, device_id_type=MESH).wait_recv()
                pk_pass().start()

            @pl.when(i == first_step + SMALL_FINISH_STEP)
            def _():
                pk_pass().wait()
                for ch in range(N_CHIP):
                    @pl.when(ch != p.chip)
                    def _(ch=ch):
                        pk_scatter(ch).wait_send()
                        pk_spread(ch).wait_send()
                for r in range(N_DEV - 1):
                    dm_copy(r).wait()
                for cp in results():
                    cp.start()
                for cp in results():
                    cp.wait()

        n_small = 15
        for l in range(DEPTH):
            small_exchange(l, (DEPTH - 1 - l) * N_ITEMS, scratch[23 + n_small * l:23 + n_small * (l + 1)])

        for step in range(N_ITEMS, N_STEPS):
            @pl.when(i == step)
            def _(step=step):
                drain(step - N_ITEMS)

        ii = jnp.where(i < N_ITEMS, i, i - N_ITEMS)
        in_step = jnp.logical_and(ii >= OUT_FIRST, ii < OUT_FIRST + IN_STEPS)

        @pl.when(in_step)
        def _():
            k = ii - OUT_FIRST
            rhs = jnp.concatenate([r[...] for r in dp_refs], axis=1)
            res = _dot(h_ref[...], rhs, TN)

            @pl.when(c == 0)
            def _():
                mine_in[k] = res[:HALF_IN]
                send_in[k] = res[HALF_IN:]

            @pl.when(c == 1)
            def _():
                mine_in[k] = res[HALF_IN:]
                send_in[k] = res[:HALF_IN]

        @pl.when(jnp.logical_not(in_step))
        def _():
            k = jnp.where(ii < OUT_FIRST, ii, ii - IN_STEPS)
            res = _dot(cat_ref[...], dy_ref[...], TN)

            @pl.when(c == 0)
            def _():
                for ch in range(N_CHIP):
                    mine_out[k, ch] = res[ch * W_OUT_ROWS:ch * W_OUT_ROWS + HALF_OUT]
                    send_out[k, ch] = res[ch * W_OUT_ROWS + HALF_OUT:(ch + 1) * W_OUT_ROWS]

            @pl.when(c == 1)
            def _():
                for ch in range(N_CHIP):
                    mine_out[k, ch] = res[ch * W_OUT_ROWS + HALF_OUT:(ch + 1) * W_OUT_ROWS]
                    send_out[k, ch] = res[ch * W_OUT_ROWS:ch * W_OUT_ROWS + HALF_OUT]

        stages = ((0, lambda j: to_sibling(j).start()), (DELAY_SUM, stage_sum), (DELAY_SECOND, stage_second),
                  (DELAY_FINAL, stage_final))
        for step in range(N_STEPS):
            @pl.when(i == step)
            def _(step=step):
                for delay, stage in stages:
                    if step - delay >= 0:
                        stage(step - delay)

        @pl.when(i == N_STEPS - 1)
        def _():
            for step in range(N_STEPS, N_STEPS + DELAY_FINAL):
                for delay, stage in stages:
                    if 0 <= step - delay < N_STEPS:
                        stage(step - delay)
            for j in range(N_STEPS - N_ITEMS, N_STEPS):
                drain(j)

    hbm = pl.BlockSpec(memory_space=pl.ANY)

    def layer(i):
        return jnp.where(i < N_ITEMS, DEPTH - 1, 0)

    def item(i):
        return jnp.where(i < N_ITEMS, i, i - N_ITEMS)

    def whole(i):
        return (layer(i), 0, 0)

    def dproj_piece(ch):
        return pl.BlockSpec((None, SEQ, HEAD),
                            lambda i: (layer(i), 0, ch * IN_STEPS + jnp.clip(item(i) - OUT_FIRST, 0, IN_STEPS - 1)))

    def dy_quarter(i):
        return (layer(i), 0, jnp.where(item(i) < OUT_FIRST, item(i), jnp.maximum(item(i) - IN_STEPS, OUT_FIRST)))

    operand = pl.BlockSpec((None, SEQ, D_MODEL), whole)
    in_specs = [operand] + [dproj_piece(ch) for ch in range(N_CHIP)]
    in_specs += [operand, pl.BlockSpec((None, SEQ, OUT_COLS), dy_quarter), hbm, _const_in((DEPTH, 8, D_MODEL))]
    args = [h, dproj, dproj, dproj, dproj, cat, dy, pack, dmod]
    out_shape = [jax.ShapeDtypeStruct((DEPTH, D_MODEL, W_IN_COLS), F32), jax.ShapeDtypeStruct((DEPTH, 2, HALF_OUT, D_MODEL), F32),
                 jax.ShapeDtypeStruct((DEPTH, PK_ROWS, HEAD), BF16), jax.ShapeDtypeStruct((DEPTH, DM_ROWS, N_DEV, HEAD), F32)]
    out_specs = [hbm, hbm, hbm, hbm]
    in_item = lambda *lead: pltpu.VMEM(lead + (HALF_IN, HEAD), BF16)
    out_item = lambda *lead: pltpu.VMEM(lead + (HALF_OUT, OUT_COLS), BF16)
    scratch = [
        pltpu.VMEM((IN_STEPS, HALF_IN, N_CHIP * HEAD), F32), pltpu.VMEM((IN_STEPS, HALF_IN, N_CHIP * HEAD), F32),
        pltpu.VMEM((IN_STEPS, HALF_IN, N_CHIP * HEAD), F32), in_item(IN_STEPS, 3), in_item(IN_STEPS, 2), in_item(IN_STEPS),
        pltpu.VMEM((IN_STEPS, HALF_IN, HEAD), F32),
        pltpu.VMEM((OUT_STEPS, N_CHIP, HALF_OUT, OUT_COLS), F32), pltpu.VMEM((OUT_STEPS, N_CHIP, HALF_OUT, OUT_COLS), F32),
        pltpu.VMEM((OUT_STEPS, N_CHIP, HALF_OUT, OUT_COLS), F32), out_item(OUT_STEPS, 3), out_item(OUT_STEPS, 2),
        out_item(OUT_STEPS), pltpu.VMEM((OUT_STEPS, HALF_OUT, OUT_COLS), F32),
        pltpu.SemaphoreType.DMA((N_STEPS,)), pltpu.SemaphoreType.DMA((N_STEPS,)),
        pltpu.SemaphoreType.DMA((2 * N_STEPS,)), pltpu.SemaphoreType.DMA((2 * N_STEPS,)),
        pltpu.SemaphoreType.DMA((N_STEPS,)), pltpu.SemaphoreType.DMA((N_STEPS,)),
        pltpu.SemaphoreType.DMA((N_STEPS,)), pltpu.SemaphoreType.DMA((N_STEPS,)), pltpu.SemaphoreType.DMA((N_STEPS,)),
    ]
    for _ in range(DEPTH):
        scratch += [
            pltpu.VMEM((PK_HALF, HEAD), F32), pltpu.VMEM((PK_HALF, HEAD), F32), pltpu.VMEM((PK_HALF, HEAD), BF16),
            pltpu.VMEM((N_CHIP, PK_PIECE, HEAD), BF16), pltpu.VMEM((PK_PIECE, HEAD), BF16), pltpu.VMEM((PK_ROWS, HEAD), BF16),
            pltpu.VMEM((DM_ROWS, 1, HEAD), F32), pltpu.VMEM((DM_ROWS, N_DEV, HEAD), F32),
            pltpu.SemaphoreType.DMA((6,)),
            pltpu.SemaphoreType.DMA((N_CHIP,)), pltpu.SemaphoreType.DMA((N_CHIP,)),
            pltpu.SemaphoreType.DMA((N_CHIP,)), pltpu.SemaphoreType.DMA((N_CHIP,)),
            pltpu.SemaphoreType.DMA((N_DEV - 1,)), pltpu.SemaphoreType.DMA((N_DEV - 1,)),
        ]
    return pl.pallas_call(
        body, name="wgrad", grid=(N_STEPS,), in_specs=in_specs, out_specs=out_specs, out_shape=out_shape,
        scratch_shapes=scratch,
        compiler_params=pltpu.CompilerParams(dimension_semantics=("arbitrary",), vmem_limit_bytes=VMEM_LIMIT),
    )(*args)


def _adamw(w, g, m, v):
    m = ADAM_B1 * m + (1.0 - ADAM_B1) * g
    v = ADAM_B2 * v + (1.0 - ADAM_B2) * (g * g)
    m_hat = m / (1.0 - ADAM_B1 ** ADAM_STEP)
    v_hat = v / (1.0 - ADAM_B2 ** ADAM_STEP)
    delta = -ADAM_LR * (m_hat / (jnp.sqrt(v_hat) + ADAM_EPS) + ADAM_WD * w)
    return delta, m, v


def _adam_sharded(c_all, dmods, ada, w_in_set, w_out_set):
    rows = D_MODEL // ADAM_PARTS

    def body(c_ref, dm_ref, wa_ref, ma_ref, va_ref, wi_ref, gi_ref, mi_ref, vi_ref, wo_ref, go_ref, mo_ref, vo_ref,
             ga_out, da_out, ma_out, va_out, di_out, mi_out, vi_out, do_out, mo_out, vo_out):
        l = pl.program_id(0)
        chip = 2 * lax.axis_index("x") + lax.axis_index("y")
        cv = c_ref[...]
        silu_c = (cv * _sigmoid(cv)).astype(BF16).astype(F32)
        pieces = []
        for k in range(W_ADA_COLS // HEAD):
            dk = dm_ref[l, (W_ADA_COLS // HEAD) * chip + k].astype(BF16).astype(F32)
            pieces.append(_dot_exact(silu_c, dk, TN))
        g = jnp.concatenate(pieces, axis=1)
        ga_out[...] = g
        da_out[...], ma_out[...], va_out[...] = _adamw(wa_ref[...], g, ma_ref[...], va_ref[...])
        di_out[...], mi_out[...], vi_out[...] = _adamw(wi_ref[...], gi_ref[...], mi_ref[...], vi_ref[...])
        do_out[...], mo_out[...], vo_out[...] = _adamw(wo_ref[...], go_ref[...], mo_ref[...], vo_ref[...])

    def blk(r, cols):
        return pl.BlockSpec((None, r, cols), lambda l, i: (l, i, 0))

    b_ada, b_in, b_out = blk(rows, W_ADA_COLS), blk(rows, W_IN_COLS), blk(W_OUT_ROWS // ADAM_PARTS, D_MODEL)
    shapes = [jax.ShapeDtypeStruct(a[0].shape, F32) for a in (ada, w_in_set, w_out_set)]
    return pl.pallas_call(
        body, name="adam_sharded", grid=(DEPTH, ADAM_PARTS),
        in_specs=[pl.BlockSpec((N_DEV, rows), lambda l, i: (0, i)), _const_in((DEPTH, DM_ROWS, N_DEV, HEAD))]
        + [b_ada] * 3 + [b_in] * 4 + [b_out] * 4,
        out_specs=[b_ada] * 4 + [b_in] * 3 + [b_out] * 3,
        out_shape=[shapes[0]] * 4 + [shapes[1]] * 3 + [shapes[2]] * 3,
        compiler_params=pltpu.CompilerParams(dimension_semantics=("arbitrary", "arbitrary"), vmem_limit_bytes=VMEM_LIMIT),
    )(c_all, dmods, *ada, *w_in_set, *w_out_set)


def _adam_small(packs, dmods, weights, ms, vs):
    n = len(weights)

    def body(*refs):
        dm_refs = refs[1]
        b = 2
        w_refs, m_refs, v_refs = refs[b:b + n], refs[b + n:b + 2 * n], refs[b + 2 * n:b + 3 * n]
        outs = refs[b + 3 * n:b + 3 * n + 4 * n + 1]
        pack_refs = refs[-1]
        pack_refs[...] = refs[0][...].astype(F32)
        g_refs, d_refs, nm_refs, nv_refs = outs[0:n], outs[n:2 * n], outs[2 * n:3 * n], outs[3 * n:4 * n]
        squares = dm_refs[DEPTH - 1, DM_LOSS]
        total = squares[0:1, 0:1]
        for d in range(1, N_DEV):
            total = total + squares[d:d + 1, 0:1]
        outs[4 * n][...] = total * (0.5 / D_MODEL)

        def lanes(l, row0, count):
            return jnp.concatenate([pack_refs.at[l][row0 + k:row0 + k + 1, :] for k in range(count)], axis=1)

        def update(idx, at, g):
            g_refs[idx][at] = g
            d_refs[idx][at], nm_refs[idx][at], nv_refs[idx][at] = _adamw(w_refs[idx][at], g, m_refs[idx][at], v_refs[idx][at])

        for l in range(DEPTH):
            row = (slice(l, l + 1), slice(None))
            g_b = None
            for d in range(N_DEV):
                part = dm_refs.at[l][0:DM_LOSS, d, :]
                g_b = part if g_b is None else g_b + part
            update(0, row, jnp.concatenate([g_b[k:k + 1, :] for k in range(DM_LOSS)], axis=1))
            for g in range(N_HEAD):
                update(1, (l, g), pack_refs.at[l][PK_W_POOL + g * HEAD:PK_W_POOL + (g + 1) * HEAD, :])
                update(5, (l, g), pack_refs.at[l][PK_W_SGU + g * HEAD:PK_W_SGU + (g + 1) * HEAD, :])
            update(2, row, lanes(l, PK_POOL_SCALE, N_HEAD))
            update(3, (l,), pack_refs.at[l][PK_SGU_LN_G:PK_SGU_LN_G + N_HEAD, :])
            update(4, (l,), pack_refs.at[l][PK_SGU_LN_B:PK_SGU_LN_B + N_HEAD, :])
            update(6, (l,), pack_refs.at[l][PK_B_SGU:PK_B_SGU + N_HEAD, :])
            update(7, row, lanes(l, PK_LN_G, D_MODEL // HEAD))
            update(8, row, lanes(l, PK_LN_B, D_MODEL // HEAD))

    vmem = pl.BlockSpec(memory_space=pltpu.VMEM)
    shapes = [jax.ShapeDtypeStruct(w.shape, F32) for w in weights]
    return pl.pallas_call(
        body, name="adam_small", in_specs=[vmem] * (2 + 3 * n), out_specs=[vmem] * (4 * n + 1),
        out_shape=shapes * 4 + [jax.ShapeDtypeStruct((1, 1), F32)],
        scratch_shapes=[pltpu.VMEM(packs.shape, F32)],
        compiler_params=pltpu.CompilerParams(vmem_limit_bytes=VMEM_LIMIT),
    )(packs, dmods, *weights, *ms, *vs)


def kernel(x, c, w_ada, b_ada, w_in, w_pool, pool_scale, sgu_ln_g, sgu_ln_b, w_sgu, b_sgu, w_out, ln_g, ln_b, loss_target, m_w_ada, m_b_ada, m_w_in, m_w_pool, m_pool_scale, m_sgu_ln_g, m_sgu_ln_b, m_w_sgu, m_b_sgu, m_w_out, m_ln_g, m_ln_b, v_w_ada, v_b_ada, v_w_in, v_w_pool, v_pool_scale, v_sgu_ln_g, v_sgu_ln_b, v_w_sgu, v_b_sgu, v_w_out, v_ln_g, v_ln_b):
    small = (w_pool, pool_scale, sgu_ln_g, sgu_ln_b, w_sgu, b_sgu)
    *saved0, w_in0, w_out0, w_in1, w_out1, mod, c_all = _forward_first(x, c, w_ada, b_ada, w_in, w_out, small, ln_g, ln_b)
    *saved1, dout, sq = _forward_last(saved0[3], mod, w_in1, w_out1, small, ln_g, ln_b, loss_target)

    dx1, *shared = _backward_layer(1, dout, saved1, mod, w_in1, w_out1, small, ln_g, sq=sq)
    dx0, h, cat, dy, dproj, pack, dmod = _backward_layer(0, dx1, saved0, mod, w_in0, w_out0, small, ln_g, shared=shared)
    g_in, g_out, pack, dmods = _wgrad_reduce(h, dproj, cat, dy, pack, dmod)

    g_out = g_out.reshape(DEPTH, W_OUT_ROWS, D_MODEL)
    big = _adam_sharded(c_all, dmods, (w_ada, m_w_ada, v_w_ada), (w_in, g_in, m_w_in, v_w_in), (w_out, g_out, m_w_out, v_w_out))
    ada, win, wout = big[0:4], (g_in, *big[4:7]), (g_out, *big[7:10])
    small_w = (b_ada, w_pool, pool_scale, sgu_ln_g, sgu_ln_b, w_sgu, b_sgu, ln_g, ln_b)
    small_m = (m_b_ada, m_w_pool, m_pool_scale, m_sgu_ln_g, m_sgu_ln_b, m_w_sgu, m_b_sgu, m_ln_g, m_ln_b)
    small_v = (v_b_ada, v_w_pool, v_pool_scale, v_sgu_ln_g, v_sgu_ln_b, v_w_sgu, v_b_sgu, v_ln_g, v_ln_b)
    res = _adam_small(pack, dmods, small_w, small_m, small_v)
    n = len(small_w)
    loss = res[4 * n].reshape(())

    def ordered(k):
        s = res[k * n:(k + 1) * n]
        return (ada[k], s[0], win[k], s[1], s[2], s[3], s[4], s[5], s[6], wout[k], s[7], s[8])

    return (loss, dx0[None], *ordered(0), *ordered(1), *ordered(2), *ordered(3))
```

```python
import jax
import jax.numpy as jnp
from jax import lax
from jax.experimental import pallas as pl
from jax.experimental.pallas import tpu as pltpu

F32 = jnp.float32
BF16 = jnp.bfloat16
MESH = pl.DeviceIdType.MESH

N_DEV = 8
N_CHIP = 4
DEPTH = 2
SEQ = 2048
D_MODEL = 1024
D_POOL = 512
D_PROJ = 2560
HEAD = 128
N_HEAD = 4
ROWS = 256
N_TILE = SEQ // ROWS
HALO = 16
W_IN_COLS = D_PROJ // N_CHIP
W_OUT_ROWS = D_MODEL // N_CHIP
W_ADA_COLS = 3 * D_MODEL // N_CHIP
HALF_IN = D_MODEL // 2
HALF_OUT = W_OUT_ROWS // 2
DEEPNORM_ALPHA = (2.0 * DEPTH) ** 0.25
LN_EPS = 1e-5
INV_SQRT2 = 0.7071067811865476
INV_SQRT_2PI = 0.3989422804014327

ADAM_LR = 0.001
ADAM_B1 = 0.9
ADAM_B2 = 0.999
ADAM_EPS = 1e-08
ADAM_WD = 0.01
ADAM_STEP = 10
ADAM_PARTS = 2

PK_W_POOL = 0
PK_W_SGU = 512
PK_POOL_SCALE = 1024
PK_SGU_LN_G = 1032
PK_SGU_LN_B = 1040
PK_B_SGU = 1048
PK_LN_G = 1056
PK_LN_B = 1064
PK_ROWS = 1152
PK_HALF = PK_ROWS // 2
PK_PIECE = PK_HALF // N_CHIP
DM_LOSS = 3 * D_MODEL // HEAD
DM_ROWS = DM_LOSS + 1

VMEM_LIMIT = 56 * 1024 * 1024

GATHER_SECOND_ROUND_STEP = 0
GATHER_PASS_STEP = N_TILE - 2

NN = (((1,), (0,)), ((), ()))
NT = (((1,), (1,)), ((), ()))
TN = (((0,), (0,)), ((), ()))


def _dot(a, b, dims=NN):
    return lax.dot_general(a, b, dims, preferred_element_type=F32)


def _dot_exact(a, b, dims=NN):
    return lax.dot_general(a, b, dims, preferred_element_type=F32, precision=lax.Precision.HIGHEST)


def _layer_norm(v):
    mu = jnp.mean(v, axis=-1, keepdims=True)
    d = v - mu
    var = jnp.mean(d * d, axis=-1, keepdims=True)
    rstd = lax.rsqrt(var + LN_EPS)
    return d * rstd, rstd


def _layer_norm_bwd(dvhat, vhat, rstd):
    m1 = jnp.mean(dvhat, axis=-1, keepdims=True)
    m2 = jnp.mean(dvhat * vhat, axis=-1, keepdims=True)
    return rstd * (dvhat - m1 - vhat * m2)


def _sigmoid(v):
    return 1.0 / (1.0 + jnp.exp(-v))


def _gelu_parts(v):
    phi = 0.5 * (1.0 + lax.erf(v * INV_SQRT2))
    pdf = INV_SQRT_2PI * jnp.exp(-0.5 * v * v)
    return phi, pdf


def _sum_rows(v):
    return jnp.sum(v, axis=0, keepdims=True)


def _window_sums(ext, toward_later):
    n = ext.shape[0]

    def shifted(v, k):
        return pltpu.roll(v, (n - k) if toward_later else k, 0)

    s2 = ext + shifted(ext, 1)
    r4 = s2[:, HEAD:]
    s4 = r4 + shifted(r4, 2)
    r8 = s4[:, HEAD:]
    s8 = r8 + shifted(r8, 4)
    r16 = s8[:, HEAD:]
    s16 = r16 + shifted(r16, 8)
    return jnp.concatenate([s2[:, :HEAD], s4[:, :HEAD], s8[:, :HEAD], s16], axis=1)


def _window_counts(row0):
    t1 = row0 + 1 + lax.broadcasted_iota(jnp.int32, (ROWS, D_POOL), 0)
    lane = lax.broadcasted_iota(jnp.int32, (ROWS, D_POOL), 1)
    width = jnp.where(lane < HEAD, 2, jnp.where(lane < 2 * HEAD, 4, jnp.where(lane < 3 * HEAD, 8, 16)))
    return jnp.minimum(t1, width).astype(F32)


def _causal_mask():
    r = lax.broadcasted_iota(jnp.int32, (HEAD, HEAD), 0)
    s = lax.broadcasted_iota(jnp.int32, (HEAD, HEAD), 1)
    return r >= s


def _chunks_to_lanes(v):
    return jnp.concatenate([v[n * HEAD:(n + 1) * HEAD] for n in range(ROWS // HEAD)], axis=1)


def _lanes_to_chunks(v):
    return jnp.concatenate([v[:, n * HEAD:(n + 1) * HEAD] for n in range(ROWS // HEAD)], axis=0)


STAT_LANES = 16


def _pack_stats(rstd_x, rstd_z, rstd_v):
    lane = lax.broadcasted_iota(jnp.int32, (ROWS, HEAD), 1)
    packed = rstd_x
    for k, r in enumerate([rstd_z] + list(rstd_v)):
        packed = jnp.where(lane < STAT_LANES * (k + 1), packed, r)
    return packed


def _unpack_stats(stats):
    cols = [stats[:, STAT_LANES * k:STAT_LANES * k + 1] for k in range(2 + N_HEAD)]
    return cols[0], cols[1], cols[2:]


def _mixer(proj, halo, row0, wpool_ref, pscale, sgu_g_ref, sgu_b_ref, wsgu_ref, bsgu_ref, saved=None):
    xa = proj[:, 0:512]
    ga = proj[:, 512:1024]
    u = proj[:, 1024:1536]
    v = proj[:, 1536:2048]
    gb = proj[:, 2048:2560]
    ext = jnp.concatenate([halo, xa], axis=0)
    win = _window_sums(ext, toward_later=False)[HALO:]
    cnt = _window_counts(row0)
    pooled = (win / cnt - xa).astype(BF16)
    pw = jnp.concatenate(
        [_dot(pooled[:, g * HEAD:(g + 1) * HEAD], wpool_ref[g].astype(BF16)) for g in range(N_HEAD)], axis=1)
    sig_a = _sigmoid(ga) if saved is None else saved["sig_a"]
    ya = pw * pscale * (ga * sig_a)
    phi_u, pdf_u = _gelu_parts(u)
    phi_v, pdf_v = _gelu_parts(v)
    gu = u * phi_u
    gv = v * phi_v
    sig_b = _sigmoid(gb) if saved is None else saved["sig_b"]
    silu_b = gb * sig_b
    mask = _causal_mask()
    diag = lax.broadcasted_iota(jnp.int32, (HEAD, HEAD), 0) == lax.broadcasted_iota(jnp.int32, (HEAD, HEAD), 1)
    vhat, rstd_v, vln_l, mixed = [], [], [], []
    for h in range(N_HEAD):
        if saved is None:
            vh, rh = _layer_norm(gv[:, h * HEAD:(h + 1) * HEAD])
        else:
            vh, rh = saved["vhat"][h], saved["rstd_v"][h]
        ln = (vh * sgu_g_ref[h:h + 1, :] + sgu_b_ref[h:h + 1, :]).astype(BF16)
        ln_l = _chunks_to_lanes(ln)
        wm = jnp.where(mask, wsgu_ref[h], 0.0).astype(BF16)
        bias = jnp.sum(jnp.where(diag, jnp.broadcast_to(bsgu_ref[h:h + 1, :], (HEAD, HEAD)), 0.0), axis=1, keepdims=True)
        mx = _lanes_to_chunks(_dot(wm, ln_l) + bias)
        vhat.append(vh)
        rstd_v.append(rh)
        vln_l.append(ln_l)
        mixed.append(mx)
    mixed = jnp.concatenate(mixed, axis=1)
    yb = gu * mixed * silu_b
    return dict(xa=xa, ga=ga, u=u, v=v, gb=gb, cnt=cnt, pooled=pooled, pw=pw, sig_a=sig_a, ya=ya, phi_u=phi_u, pdf_u=pdf_u,
                phi_v=phi_v, pdf_v=pdf_v, gu=gu, sig_b=sig_b, silu_b=silu_b, vhat=vhat, rstd_v=rstd_v, vln_l=vln_l,
                mixed=mixed, yb=yb, mask=mask)


def _const(shape, *index):
    lead = tuple(index) + (0,) * (len(shape) - len(index))
    return pl.BlockSpec(shape, lambda *_: lead)


def _const_in(shape, *index):
    lead = tuple(index) + (0,) * (len(shape) - len(index))
    return pl.BlockSpec(shape, lambda *_: lead, pipeline_mode=pl.Buffered(1))


def _layer_weight_specs(l):
    return [
        _const_in((None, N_HEAD, HEAD, HEAD), l),
        _const_in((DEPTH, D_POOL)),
        _const_in((None, N_HEAD, HEAD), l),
        _const_in((None, N_HEAD, HEAD), l),
        _const_in((None, N_HEAD, HEAD, HEAD), l),
        _const_in((None, N_HEAD, HEAD), l),
    ]


def _forward_front(l, i, x_ref, mod_ref, win_ref, small_refs, lng_ref, lnb_ref, carry_ref, saved_refs, held_refs):
    wpool_ref, pscale_ref, sgu_g_ref, sgu_b_ref, wsgu_ref, bsgu_ref = small_refs
    proj_ref, _, xn_ref, _, _, sig_ref, vhat_ref = saved_refs
    x_held, cat_held, stats_held = held_refs
    x = x_ref[...]
    if l > 0:
        x = x * lng_ref[l - 1:l, :] + lnb_ref[l - 1:l, :]
    shift, scale = mod_ref[0:1, :], mod_ref[1:2, :]
    xn, rstd_x = _layer_norm(x)
    xn_ref[...] = xn.astype(xn_ref.dtype)
    h = xn * (1.0 + scale) + shift
    proj = _dot(h.astype(BF16), win_ref[...])
    proj_ref[...] = proj.astype(proj_ref.dtype)
    m = _mixer(proj, carry_ref[...], i * ROWS, wpool_ref, pscale_ref[l:l + 1, :], sgu_g_ref, sgu_b_ref, wsgu_ref, bsgu_ref)
    carry_ref[...] = m["xa"][ROWS - HALO:]
    sig_ref[...] = jnp.concatenate([m["sig_a"], m["sig_b"]], axis=1).astype(sig_ref.dtype)
    vhat_ref[...] = jnp.concatenate(m["vhat"], axis=1).astype(vhat_ref.dtype)
    x_held[...] = x
    cat_held[...] = jnp.concatenate([m["ya"], m["yb"]], axis=1).astype(BF16)
    stats_held[...] = _pack_stats(rstd_x, rstd_x, m["rstd_v"])


def _forward_back(mod_ref, wout_ref, saved_refs, held_refs):
    _, y_ref, _, zn_ref, stats_ref, _, _ = saved_refs
    x_held, cat_held, stats_held = held_refs
    gate = mod_ref[2:3, :]
    y = _dot(cat_held[...], wout_ref[...])
    y_ref[...] = y.astype(y_ref.dtype)
    zn, rstd_z = _layer_norm(DEEPNORM_ALPHA * x_held[...] + gate * y)
    zn_ref[...] = zn
    lane = lax.broadcasted_iota(jnp.int32, (ROWS, HEAD), 1)
    stats_ref[...] = jnp.where((lane >= STAT_LANES) & (lane < 2 * STAT_LANES), rstd_z, stats_held[...])
    return zn


def _forward_steps(i, front, back):
    @pl.when(i == 0)
    def _():
        front()

    @pl.when((i > 0) & (i < N_TILE))
    def _():
        back()
        front()

    @pl.when(i == N_TILE)
    def _():
        back()


def _held_scratch():
    return [pltpu.VMEM((ROWS, D_MODEL), F32), pltpu.VMEM((ROWS, D_MODEL), BF16), pltpu.VMEM((ROWS, HEAD), F32)]


SAVED_COLS = (D_PROJ, D_MODEL, D_MODEL, D_MODEL, HEAD, D_MODEL, D_POOL)
SAVED_TYPES = (BF16, BF16, BF16, F32, F32, BF16, BF16)
SAVED_LATE = (False, True, False, True, True, False, False)


def _front_tile(i):
    return (jnp.minimum(i, N_TILE - 1), 0)


def _back_tile(i):
    return (jnp.maximum(i - 1, 0), 0)


def _saved_outputs():
    return ([jax.ShapeDtypeStruct((SEQ, cols), t) for cols, t in zip(SAVED_COLS, SAVED_TYPES)],
            [pl.BlockSpec((ROWS, cols), _back_tile if late else _front_tile) for cols, late in zip(SAVED_COLS, SAVED_LATE)])


def _forward_last(zn_prev, mod, w_in, w_out, small, ln_g, ln_b, target):
    l = DEPTH - 1
    n_saved = len(SAVED_COLS)

    def body(*refs):
        x_ref, mod_ref, win_ref, wout_ref = refs[:4]
        small_refs, lng_ref, lnb_ref, tgt_ref = refs[4:10], refs[10], refs[11], refs[12]
        saved_refs = refs[13:13 + n_saved]
        dout_ref, loss_ref, carry_ref = refs[13 + n_saved:16 + n_saved]
        held_refs = refs[16 + n_saved:]
        i = pl.program_id(0)

        @pl.when(i == 0)
        def _():
            carry_ref[...] = jnp.zeros_like(carry_ref)
            loss_ref[...] = jnp.zeros_like(loss_ref)

        def front():
            _forward_front(l, i, x_ref, mod_ref, win_ref, small_refs, lng_ref, lnb_ref, carry_ref, saved_refs, held_refs)

        def back():
            zn = _forward_back(mod_ref, wout_ref, saved_refs, held_refs)
            err = zn * lng_ref[l:l + 1, :] + lnb_ref[l:l + 1, :] - tgt_ref[...]
            dout_ref[...] = err * (1.0 / D_MODEL)
            loss_ref[...] += jnp.sum(err * err)

        _forward_steps(i, front, back)

    tile = pl.BlockSpec((ROWS, D_MODEL), _front_tile)
    late = pl.BlockSpec((ROWS, D_MODEL), _back_tile)
    tile3 = pl.BlockSpec((None, ROWS, D_MODEL), lambda i: (0, *_back_tile(i)))
    in_specs = [tile, _const_in((None, 8, D_MODEL), l), _const_in((D_MODEL, D_PROJ)), _const_in((D_MODEL, D_MODEL))]
    in_specs += _layer_weight_specs(l) + [_const_in((DEPTH, D_MODEL)), _const_in((DEPTH, D_MODEL)), tile3]
    out_shape, out_specs = _saved_outputs()
    out_shape += [jax.ShapeDtypeStruct((SEQ, D_MODEL), F32), jax.ShapeDtypeStruct((8, HEAD), F32)]
    out_specs += [late, _const((8, HEAD))]
    return pl.pallas_call(
        body, name="fwd_last", grid=(N_TILE + 1,), in_specs=in_specs, out_specs=out_specs, out_shape=out_shape,
        scratch_shapes=[pltpu.VMEM((HALO, D_POOL), F32)] + _held_scratch(),
        compiler_params=pltpu.CompilerParams(dimension_semantics=("arbitrary",), vmem_limit_bytes=VMEM_LIMIT),
    )(zn_prev, mod, w_in, w_out, *small, ln_g, ln_b, target)


def _backward_layer(l, dout, saved, mod, w_in, w_out, small, ln_g, sq=None, shared=None):
    has_loss = sq is not None

    def body(*refs):
        (dout_ref, proj_ref, y_ref, xn_ref, zn_ref, stats_ref, sig_ref, vhat_ref, halo_ref, mod_ref, win_ref, wout_ref,
         wpool_ref, pscale_ref, sgu_g_ref, sgu_b_ref, wsgu_ref, bsgu_ref, lng_ref) = refs[:19]
        n_in = 20 if has_loss else 19 + 6
        dx_ref, h_ref, cat_ref, dy_ref, dproj_ref, pack_ref, dmod_ref, carry_ref = refs[n_in:n_in + 8]
        i = pl.program_id(0)
        tile = N_TILE - 1 - i

        @pl.when(i == 0)
        def _():
            carry_ref[...] = jnp.zeros_like(carry_ref)
            pack_ref[...] = jnp.zeros_like(pack_ref)
            dmod_ref[...] = jnp.zeros_like(dmod_ref)
            if has_loss:
                dmod_ref[3:4, 0:HEAD] = refs[19][0:1, :]

        xn = xn_ref[...].astype(F32)
        zn = zn_ref[...]
        y = y_ref[...].astype(F32)
        dout = dout_ref[...]
        rstd_x, rstd_z, rstd_v = _unpack_stats(stats_ref[...])
        kept = dict(sig_a=sig_ref[:, :D_POOL].astype(F32), sig_b=sig_ref[:, D_POOL:].astype(F32), rstd_v=rstd_v,
                    vhat=[vhat_ref[:, hd * HEAD:(hd + 1) * HEAD].astype(F32) for hd in range(N_HEAD)])
        pscale = pscale_ref[l:l + 1, :]
        shift, scale, gate = mod_ref[0:1, :], mod_ref[1:2, :], mod_ref[2:3, :]
        h = xn * (1.0 + scale) + shift
        h_ref[...] = h.astype(BF16)
        g_ln_g = _sum_rows(dout * zn)
        g_ln_b = _sum_rows(dout)
        dz = _layer_norm_bwd(dout * lng_ref[l:l + 1, :], zn, rstd_z)
        d_gate = _sum_rows(dz * y)
        dy = (gate * dz).astype(BF16)
        dy_ref[...] = dy

        halo = jnp.where(tile > 0, halo_ref[...].astype(F32), 0.0)
        m = _mixer(proj_ref[...].astype(F32), halo, tile * ROWS, wpool_ref, pscale, sgu_g_ref, sgu_b_ref, wsgu_ref, bsgu_ref,
                   saved=kept)
        cat_ref[...] = jnp.concatenate([m["ya"], m["yb"]], axis=1).astype(BF16)
        dcat = _dot(dy, wout_ref[...], NT)
        dya = dcat[:, :D_POOL]
        dyb = dcat[:, D_POOL:]

        ga, sig_a = m["ga"], m["sig_a"]
        dp = dya * (ga * sig_a)
        d_ga = dya * (m["pw"] * pscale) * (sig_a * (1.0 + ga * (1.0 - sig_a)))
        g_pscale = _sum_rows(dp * m["pw"])
        dpw = (dp * pscale).astype(BF16)
        dpooled = []
        for g in range(N_HEAD):
            cols = slice(g * HEAD, (g + 1) * HEAD)
            pack_ref[PK_W_POOL + g * HEAD:PK_W_POOL + (g + 1) * HEAD, :] += _dot(m["pooled"][:, cols], dpw[:, cols], TN)
            dpooled.append(_dot(dpw[:, cols], wpool_ref[g].astype(BF16), NT))
        dpooled = jnp.concatenate(dpooled, axis=1)
        q = dpooled / m["cnt"]
        ext = jnp.concatenate([q, carry_ref[...]], axis=0)
        d_xa = _window_sums(ext, toward_later=True)[:ROWS] - dpooled
        carry_ref[...] = q[:HALO]

        gu, mixed, silu_b, gb, sig_b = m["gu"], m["mixed"], m["silu_b"], m["gb"], m["sig_b"]
        d_mixed = dyb * gu * silu_b
        d_gu = dyb * mixed * silu_b
        d_gb = dyb * gu * mixed * (sig_b * (1.0 + gb * (1.0 - sig_b)))
        d_u = d_gu * (m["phi_u"] + m["u"] * m["pdf_u"])
        ones = jnp.ones((8, HEAD), F32)
        d_v = []
        for hd in range(N_HEAD):
            cols = slice(hd * HEAD, (hd + 1) * HEAD)
            dm = d_mixed[:, cols]
            dm_l = _chunks_to_lanes(dm.astype(BF16))
            g_w = _dot(dm_l, m["vln_l"][hd], NT)
            pack_ref[PK_W_SGU + hd * HEAD:PK_W_SGU + (hd + 1) * HEAD, :] += jnp.where(m["mask"], g_w, 0.0)
            dm_sum = dm[0:HEAD]
            for n in range(1, ROWS // HEAD):
                dm_sum = dm_sum + dm[n * HEAD:(n + 1) * HEAD]
            pack_ref[PK_B_SGU + hd:PK_B_SGU + hd + 1, :] += _dot_exact(ones, dm_sum, NT)[0:1]
            wm = jnp.where(m["mask"], wsgu_ref[hd], 0.0).astype(BF16)
            d_vln = _lanes_to_chunks(_dot(wm, dm_l, TN))
            vhat = m["vhat"][hd]
            pack_ref[PK_SGU_LN_G + hd:PK_SGU_LN_G + hd + 1, :] += _sum_rows(d_vln * vhat)
            pack_ref[PK_SGU_LN_B + hd:PK_SGU_LN_B + hd + 1, :] += _sum_rows(d_vln)
            d_v.append(_layer_norm_bwd(d_vln * sgu_g_ref[hd:hd + 1, :], vhat, m["rstd_v"][hd]))
        v = m["v"]
        d_v = jnp.concatenate(d_v, axis=1) * (m["phi_v"] + v * m["pdf_v"])

        dproj = jnp.concatenate([d_xa, d_ga, d_u, d_v, d_gb], axis=1).astype(BF16)
        dproj_ref[...] = dproj
        dh = _dot(dproj, win_ref[...], NT)
        d_scale = _sum_rows(dh * xn)
        d_shift = _sum_rows(dh)
        dx_ref[...] = DEEPNORM_ALPHA * dz + _layer_norm_bwd(dh * (1.0 + scale), xn, rstd_x)

        dmod_ref[0:1, :] += d_shift
        dmod_ref[1:2, :] += d_scale
        dmod_ref[2:3, :] += d_gate
        for g in range(N_HEAD):
            pack_ref[PK_POOL_SCALE + g:PK_POOL_SCALE + g + 1, :] += g_pscale[:, g * HEAD:(g + 1) * HEAD]
        for k in range(D_MODEL // HEAD):
            pack_ref[PK_LN_G + k:PK_LN_G + k + 1, :] += g_ln_g[:, k * HEAD:(k + 1) * HEAD]
            pack_ref[PK_LN_B + k:PK_LN_B + k + 1, :] += g_ln_b[:, k * HEAD:(k + 1) * HEAD]

    def rev(i):
        return (N_TILE - 1 - i, 0)

    tile = pl.BlockSpec((ROWS, D_MODEL), rev)
    halo = pl.BlockSpec((HALO, D_POOL), lambda i: (jnp.maximum((N_TILE - 1 - i) * (ROWS // HALO) - 1, 0), 0))
    in_specs = [tile] + [pl.BlockSpec((ROWS, a.shape[1]), rev) for a in saved] + [halo]
    in_specs += [_const_in((None, 8, D_MODEL), l), _const_in((D_MODEL, D_PROJ)), _const_in((D_MODEL, D_MODEL))]
    in_specs += _layer_weight_specs(l) + [_const_in((DEPTH, D_MODEL))]
    args = [dout, *saved, saved[0], mod, w_in, w_out, *small, ln_g]
    stacked = lambda cols: pl.BlockSpec((None, ROWS, cols), lambda i: (l, N_TILE - 1 - i, 0))
    out_shape = [jax.ShapeDtypeStruct((SEQ, D_MODEL), F32), jax.ShapeDtypeStruct((DEPTH, SEQ, D_MODEL), BF16),
                 jax.ShapeDtypeStruct((DEPTH, SEQ, D_MODEL), BF16), jax.ShapeDtypeStruct((DEPTH, SEQ, D_MODEL), BF16),
                 jax.ShapeDtypeStruct((DEPTH, SEQ, D_PROJ), BF16), jax.ShapeDtypeStruct((DEPTH, PK_ROWS, HEAD), F32),
                 jax.ShapeDtypeStruct((DEPTH, 8, D_MODEL), F32)]
    out_specs = [tile, stacked(D_MODEL), stacked(D_MODEL), stacked(D_MODEL), stacked(D_PROJ),
                 _const((None, PK_ROWS, HEAD), l), _const((None, 8, D_MODEL), l)]
    aliases = {}
    if has_loss:
        in_specs.append(_const_in((8, HEAD)))
        args.append(sq)
    else:
        aliases = {len(args) + k: 1 + k for k in range(len(shared))}
        in_specs += [pl.BlockSpec(memory_space=pl.ANY)] * len(shared)
        args += list(shared)
    return pl.pallas_call(
        body, name="bwd_last" if has_loss else "bwd_first", grid=(N_TILE,), in_specs=in_specs, out_specs=out_specs,
        out_shape=out_shape, scratch_shapes=[pltpu.VMEM((HALO, D_POOL), F32)], input_output_aliases=aliases,
        compiler_params=pltpu.CompilerParams(dimension_semantics=("arbitrary",), vmem_limit_bytes=VMEM_LIMIT),
    )(*args)


def _flip(v, f):
    return v + f - 2 * v * f


class _Place:
    def __init__(self):
        x, y, c = lax.axis_index("x"), lax.axis_index("y"), lax.axis_index("c")
        self.x, self.y, self.c = x, y, c
        self.chip = 2 * x + y
        self.dev = 4 * x + 2 * y + c
        self.sibling = (x, y, 1 - c)
        x1, y1 = _flip(x, 1 - c), _flip(y, c)
        x2, y2 = _flip(x, c), _flip(y, 1 - c)
        self.first = (x1, y1, c)
        self.second = (x2, y2, c)
        self.chip_first = 2 * x1 + y1
        self.chip_second = 2 * x2 + y2
        self.chip_far = 2 * (1 - x) + (1 - y)
        self.my_first_coord = jnp.where(c == 0, x, y)

    def first_coord(self, ch):
        return jnp.where(self.c == 0, ch // 2, ch % 2)

    def others(self):
        return [(_flip(self.x, (r >> 2) & 1), _flip(self.y, (r >> 1) & 1), _flip(self.c, r & 1)) for r in range(1, N_DEV)]

    def other_chips(self):
        return [(1 - self.x, self.y), (self.x, 1 - self.y), (1 - self.x, 1 - self.y)]


class _WeightGather:
    CHUNKS = 4
    N_SEMS = 12 * CHUNKS

    def __init__(self, place, win, wout, send, recv):
        self.p, self.win, self.wout, self.send, self.recv = place, win, wout, send, recv
        p = place
        self.plan = [(p.chip, p.first), (p.chip, p.second), (p.chip_first, p.second),
                     (p.chip_first, p.sibling), (p.chip_second, p.sibling), (p.chip_far, p.sibling)]

    def _copies(self, k, q):
        ch, target = self.plan[k]
        n_in, n_out = HALF_IN // self.CHUNKS, HALF_OUT // self.CHUNKS
        rows_in = pl.ds(pl.multiple_of(self.p.c * HALF_IN + q * n_in, n_in), n_in)
        cols_in = pl.ds(pl.multiple_of(ch * W_IN_COLS, 128), W_IN_COLS)
        rows_out = pl.ds(pl.multiple_of(ch * W_OUT_ROWS + self.p.c * HALF_OUT + q * n_out, n_out), n_out)
        r_in = self.win.at[rows_in, cols_in]
        r_out = self.wout.at[rows_out, :]
        s = 2 * (6 * q + k)
        return [pltpu.make_async_remote_copy(r_in, r_in, self.send.at[s], self.recv.at[s],
                                             device_id=target, device_id_type=MESH),
                pltpu.make_async_remote_copy(r_out, r_out, self.send.at[s + 1], self.recv.at[s + 1],
                                             device_id=target, device_id_type=MESH)]

    def _start(self, k, q):
        for cp in self._copies(k, q):
            cp.start()

    def _landed(self, k, q):
        for cp in self._copies(k, q):
            cp.wait_recv()

    def start_first_round(self):
        for q in range(self.CHUNKS):
            self._start(0, q)

    def start_second_round(self, q):
        self._landed(0, q)
        self._start(1, q)
        self._start(2, q)
        self._start(3, q)

    def pass_second_round(self, q):
        self._landed(1, q)
        self._start(4, q)
        self._landed(2, q)
        self._start(5, q)

    def finish(self):
        for q in range(self.CHUNKS):
            for k in (3, 4, 5):
                self._landed(k, q)
        for q in range(self.CHUNKS):
            for k in range(len(self.plan)):
                for cp in self._copies(k, q):
                    cp.wait_send()


def _forward_first(x, c_vec, w_ada, b_ada, w_in, w_out, small, ln_g, ln_b):
    n_saved = len(SAVED_COLS)

    def body(*refs):
        x_ref, c_ref, wada_hbm, bada_ref, win_hbm, wout_hbm = refs[:6]
        small_refs, lng_ref, lnb_ref = refs[6:12], refs[12], refs[13]
        saved_refs = refs[14:14 + n_saved]
        win0, wout0, win1, wout1, mod_out, c_out = refs[14 + n_saved:20 + n_saved]
        (carry_ref, wada_ref, win_ref, wout_ref, win_bf, wout_bf, mod_mine, mod_all, c_all, mod_ref, win_v, wout_v,
         g0_send, g0_recv, g1_send, g1_recv, c_send, c_recv, mod_send, mod_recv, local_sem) = refs[20 + n_saved:41 + n_saved]
        held_refs = refs[41 + n_saved:]
        i = pl.program_id(0)
        p = _Place()
        gather0 = _WeightGather(p, win0, wout0, g0_send, g0_recv)
        gather1 = _WeightGather(p, win1, wout1, g1_send, g1_recv)

        @pl.when(i == 0)
        def _():
            carry_ref[...] = jnp.zeros_like(carry_ref)
            loads = [pltpu.make_async_copy(win_hbm.at[0], win_ref.at[0], local_sem.at[4]),
                     pltpu.make_async_copy(wout_hbm.at[0], wout_ref.at[0], local_sem.at[5]),
                     pltpu.make_async_copy(win_hbm.at[1], win_ref.at[1], local_sem.at[6]),
                     pltpu.make_async_copy(wout_hbm.at[1], wout_ref.at[1], local_sem.at[7]),
                     pltpu.make_async_copy(wada_hbm, wada_ref, local_sem.at[8])]
            for cp in loads:
                cp.start()

            c_all[pl.ds(p.dev, 1), :] = c_ref[...]
            mine = c_all.at[pl.ds(p.dev, 1), :]
            c_copies = [pltpu.make_async_remote_copy(mine, mine, c_send.at[r], c_recv.at[r], device_id=d, device_id_type=MESH)
                        for r, d in enumerate(p.others())]
            for cp in c_copies:
                cp.start()

            cols = pl.ds(pl.multiple_of(p.chip * W_IN_COLS, 128), W_IN_COLS)
            rows = pl.ds(pl.multiple_of(p.chip * W_OUT_ROWS, W_OUT_ROWS), W_OUT_ROWS)
            own = [pltpu.make_async_copy(win_bf.at[0], win0.at[:, cols], local_sem.at[0]),
                   pltpu.make_async_copy(wout_bf.at[0], wout0.at[rows, :], local_sem.at[1]),
                   pltpu.make_async_copy(win_bf.at[1], win1.at[:, cols], local_sem.at[2]),
                   pltpu.make_async_copy(wout_bf.at[1], wout1.at[rows, :], local_sem.at[3])]
            for l in range(DEPTH):
                loads[2 * l].wait()
                win_bf[l] = win_ref[l].astype(BF16)
                own[2 * l].start()
                loads[2 * l + 1].wait()
                wout_bf[l] = wout_ref[l].astype(BF16)
                own[2 * l + 1].start()
                if l == 0:
                    own[0].wait()
                    own[1].wait()
                    gather0.start_first_round()
            for cp in c_copies:
                cp.wait()
            loads[4].wait()

            cv = c_all[...]
            c_out[...] = cv
            silu_c = (cv * _sigmoid(cv)).astype(BF16)
            for l in range(DEPTH):
                mod_mine[l] = _dot(silu_c, wada_ref[l].astype(BF16))
            mod_all[p.chip] = mod_mine[...]
            m_copies = [pltpu.make_async_remote_copy(mod_mine, mod_all.at[p.chip], mod_send.at[k], mod_recv.at[k],
                                                     device_id=(px, py, p.c), device_id_type=MESH)
                        for k, (px, py) in enumerate(p.other_chips())]
            for cp in m_copies:
                cp.start()
            for q in range(gather0.CHUNKS):
                gather0.start_second_round(q)
            own[2].wait()
            own[3].wait()
            gather1.start_first_round()
            for cp in m_copies:
                cp.wait()
            mod_ref[...] = jnp.zeros_like(mod_ref)
            for l in range(DEPTH):
                full = jnp.concatenate([mod_all[ch, l, pl.ds(p.dev, 1), :] for ch in range(N_CHIP)], axis=1) + bada_ref[l:l + 1, :]
                for k in range(3):
                    mod_ref[l, k:k + 1, :] = full[:, k * D_MODEL:(k + 1) * D_MODEL]
            mod_out[...] = mod_ref[...]
            for q in range(gather0.CHUNKS):
                gather0.pass_second_round(q)
            gather0.finish()
            fetch = [pltpu.make_async_copy(win0, win_v, local_sem.at[9]), pltpu.make_async_copy(wout0, wout_v, local_sem.at[10])]
            for cp in fetch:
                cp.start()
            for cp in fetch:
                cp.wait()

        for q in range(_WeightGather.CHUNKS):
            @pl.when(i == GATHER_SECOND_ROUND_STEP + q)
            def _(q=q):
                gather1.start_second_round(q)

            @pl.when(i == GATHER_PASS_STEP)
            def _(q=q):
                gather1.pass_second_round(q)

        def front():
            _forward_front(0, i, x_ref, mod_ref.at[0], win_v, small_refs, lng_ref, lnb_ref, carry_ref, saved_refs, held_refs)

        def back():
            _forward_back(mod_ref.at[0], wout_v, saved_refs, held_refs)

        _forward_steps(i, front, back)

        @pl.when(i == N_TILE)
        def _():
            gather1.finish()

    hbm = pl.BlockSpec(memory_space=pl.ANY)
    tile3 = pl.BlockSpec((None, ROWS, D_MODEL), lambda i: (0, *_front_tile(i)))
    in_specs = [tile3, _const_in((1, D_MODEL)), hbm, _const_in((DEPTH, 3 * D_MODEL)), hbm, hbm]
    in_specs += _layer_weight_specs(0) + [_const_in((DEPTH, D_MODEL)), _const_in((DEPTH, D_MODEL))]
    out_shape, out_specs = _saved_outputs()
    w_in_shape = jax.ShapeDtypeStruct((D_MODEL, D_PROJ), BF16)
    w_out_shape = jax.ShapeDtypeStruct((D_MODEL, D_MODEL), BF16)
    out_shape += [w_in_shape, w_out_shape, w_in_shape, w_out_shape,
                  jax.ShapeDtypeStruct((DEPTH, 8, D_MODEL), F32), jax.ShapeDtypeStruct((N_DEV, D_MODEL), F32)]
    out_specs += [hbm, hbm, hbm, hbm, _const((DEPTH, 8, D_MODEL)), _const((N_DEV, D_MODEL))]
    gather_sems = [pltpu.SemaphoreType.DMA((_WeightGather.N_SEMS,))] * 4
    scratch = [
        pltpu.VMEM((HALO, D_POOL), F32),
        pltpu.VMEM(w_ada.shape, F32), pltpu.VMEM(w_in.shape, F32), pltpu.VMEM(w_out.shape, F32),
        pltpu.VMEM((DEPTH, D_MODEL, W_IN_COLS), BF16), pltpu.VMEM((DEPTH, W_OUT_ROWS, D_MODEL), BF16),
        pltpu.VMEM((DEPTH, N_DEV, W_ADA_COLS), F32), pltpu.VMEM((N_CHIP, DEPTH, N_DEV, W_ADA_COLS), F32),
        pltpu.VMEM((N_DEV, D_MODEL), F32), pltpu.VMEM((DEPTH, 8, D_MODEL), F32),
        pltpu.VMEM((D_MODEL, D_PROJ), BF16), pltpu.VMEM((D_MODEL, D_MODEL), BF16),
    ] + gather_sems + [
        pltpu.SemaphoreType.DMA((7,)), pltpu.SemaphoreType.DMA((7,)),
        pltpu.SemaphoreType.DMA((3,)), pltpu.SemaphoreType.DMA((3,)),
        pltpu.SemaphoreType.DMA((11,)),
    ] + _held_scratch()
    return pl.pallas_call(
        body, name="fwd_first", grid=(N_TILE + 1,), in_specs=in_specs, out_specs=out_specs, out_shape=out_shape,
        scratch_shapes=scratch,
        compiler_params=pltpu.CompilerParams(dimension_semantics=("arbitrary",), vmem_limit_bytes=VMEM_LIMIT),
    )(x, c_vec, w_ada, b_ada, w_in, w_out, *small, ln_g, ln_b)


IN_STEPS = W_IN_COLS // HEAD
OUT_STEPS = 4
OUT_COLS = D_MODEL // OUT_STEPS
OUT_FIRST = 2
ITEMS = ([("out", k) for k in range(OUT_FIRST)] + [("in", k) for k in range(IN_STEPS)]
         + [("out", k) for k in range(OUT_FIRST, OUT_STEPS)])
N_ITEMS = len(ITEMS)
N_STEPS = DEPTH * N_ITEMS
DELAY_SUM, DELAY_SECOND, DELAY_FINAL = 1, 3, 5
SMALL_SCATTER_STEP, SMALL_GATHER_STEP, SMALL_PASS_STEP, SMALL_FINISH_STEP = 1, 3, 5, 7


def _wgrad_reduce(h, dproj, cat, dy, pack, dmod):
    def body(*refs):
        h_ref, dp_refs, cat_ref, dy_ref, pack_ref, dmod_ref = refs[0], refs[1:5], refs[5], refs[6], refs[7], refs[8]
        fin_in, fin_out, pack_out, dmod_out = refs[9:13]
        scratch = refs[13:]
        (mine_in, send_in, sib_in, st_in, r1_in, r2_in, f_in,
         mine_out, send_out, sib_out, st_out, r1_out, r2_out, f_out,
         d2d_s, d2d_r, r1_s, r1_r, r2_s, r2_r, fin_l, fin_s, fin_r) = scratch[:23]
        p = _Place()
        c = p.c
        i = pl.program_id(0)
        my_rows = pl.ds(pl.multiple_of(c * HALF_IN, HALF_IN), HALF_IN)

        def layer_of(j):
            return DEPTH - 1 - j // N_ITEMS

        def bufs(j):
            kind, k = ITEMS[j % N_ITEMS]
            if kind == "in":
                return [r.at[k] for r in (mine_in, send_in, sib_in, st_in, r1_in, r2_in, f_in)]
            return [r.at[k] for r in (mine_out, send_out, sib_out, st_out, r1_out, r2_out, f_out)]

        def piece(j, ref, ch):
            if ITEMS[j % N_ITEMS][0] == "in":
                return ref.at[:, ch * HEAD:(ch + 1) * HEAD]
            return ref.at[ch]

        def slot(ch):
            return jnp.where(c == 0, ch % 2, ch // 2)

        def to_sibling(j):
            _, send, sib, _, _, _, _ = bufs(j)
            return pltpu.make_async_remote_copy(send, sib, d2d_s.at[j], d2d_r.at[j], device_id=p.sibling, device_id_type=MESH)

        def first_round(j, ch):
            _, _, _, st, r1, _, _ = bufs(j)
            k = slot(ch)
            return pltpu.make_async_remote_copy(st.at[k], r1.at[k], r1_s.at[2 * j + k], r1_r.at[2 * j + k],
                                                device_id=p.first, device_id_type=MESH)

        def second_round(j):
            _, _, _, st, _, r2, _ = bufs(j)
            return pltpu.make_async_remote_copy(st.at[2], r2, r2_s.at[j], r2_r.at[j], device_id=p.second, device_id_type=MESH)

        def finals(j):
            f = bufs(j)[6]
            kind, k = ITEMS[j % N_ITEMS]
            if kind == "in":
                dst = fin_in.at[layer_of(j), my_rows, k * HEAD:(k + 1) * HEAD]
            else:
                dst = fin_out.at[layer_of(j), c, :, k * OUT_COLS:(k + 1) * OUT_COLS]
            return [pltpu.make_async_copy(f, dst, fin_l.at[j]),
                    pltpu.make_async_remote_copy(f, dst, fin_s.at[j], fin_r.at[j], device_id=p.sibling, device_id_type=MESH)]

        def stage_sum(j):
            mine, _, sib, st, _, _, _ = bufs(j)
            to_sibling(j).wait_recv()
            mine[...] = mine[...] + sib[...]
            for ch in range(N_CHIP):
                @pl.when(p.first_coord(ch) != p.my_first_coord)
                def _(ch=ch):
                    st[slot(ch)] = piece(j, mine, ch)[...].astype(BF16)
                    first_round(j, ch).start()

        def stage_second(j):
            mine, _, _, st, r1, _, _ = bufs(j)
            for ch in range(N_CHIP):
                @pl.when(p.first_coord(ch) == p.my_first_coord)
                def _(ch=ch):
                    first_round(j, ch).wait_recv()
                    part = piece(j, mine, ch)
                    total = part[...] + r1[slot(ch)].astype(F32)
                    part[...] = total

                    @pl.when(ch != p.chip)
                    def _():
                        st[2] = total.astype(BF16)
                        second_round(j).start()

        def stage_final(j):
            mine, _, _, _, _, r2, f = bufs(j)
            second_round(j).wait_recv()
            for ch in range(N_CHIP):
                @pl.when(ch == p.chip)
                def _(ch=ch):
                    f[...] = piece(j, mine, ch)[...] + r2[...].astype(F32)
            for cp in finals(j):
                cp.start()

        def drain(j):
            to_sibling(j).wait_send()
            for ch in range(N_CHIP):
                @pl.when(p.first_coord(ch) != p.my_first_coord)
                def _(ch=ch):
                    first_round(j, ch).wait_send()

                @pl.when(jnp.logical_and(p.first_coord(ch) == p.my_first_coord, ch != p.chip))
                def _():
                    second_round(j).wait_send()
            for cp in finals(j):
                cp.wait()

        dev = p.dev
        devices = p.others()

        def half(core):
            return pl.ds(pl.multiple_of(core * PK_HALF, 16), PK_HALF)

        def finished(core, ch):
            return pl.ds(pl.multiple_of(core * PK_HALF + ch * PK_PIECE, 16), PK_PIECE)

        def small_exchange(l, first_step, bufs_l):
            (pk_mine, pk_sib, pk_st, pk_rs, pk_fin, pk_all, dm_st, dm_all, pk_sem, rs_s, rs_r, ag_s, ag_r, dm_s, dm_r) = bufs_l

            def pk_load():
                return pltpu.make_async_copy(pack_ref.at[l, half(c)], pk_mine, pk_sem.at[0])

            def pk_give():
                return pltpu.make_async_remote_copy(pack_ref.at[l, half(1 - c)], pk_sib, pk_sem.at[1], pk_sem.at[2],
                                                    device_id=p.sibling, device_id_type=MESH)

            def pk_scatter(ch):
                return pltpu.make_async_remote_copy(pk_st.at[ch * PK_PIECE:(ch + 1) * PK_PIECE], pk_rs.at[p.chip],
                                                    rs_s.at[ch], rs_r.at[p.chip], device_id=(ch // 2, ch % 2, c),
                                                    device_id_type=MESH)

            def pk_spread(ch):
                return pltpu.make_async_remote_copy(pk_fin, pk_all.at[finished(c, p.chip)], ag_s.at[ch], ag_r.at[p.chip],
                                                    device_id=(ch // 2, ch % 2, c), device_id_type=MESH)

            def pk_pass():
                return pltpu.make_async_remote_copy(pk_all.at[half(c)], pk_all.at[half(c)], pk_sem.at[3], pk_sem.at[4],
                                                    device_id=p.sibling, device_id_type=MESH)

            def dm_copy(r):
                return pltpu.make_async_remote_copy(dm_st, dm_all.at[:, pl.ds(dev, 1), :], dm_s.at[r], dm_r.at[r],
                                                    device_id=devices[r], device_id_type=MESH)

            def results():
                return [pltpu.make_async_copy(pk_all, pack_out.at[l], pk_sem.at[0]),
                        pltpu.make_async_copy(dm_all, dmod_out.at[l], pk_sem.at[5])]

            @pl.when(i == first_step)
            def _():
                pk_load().start()
                pk_give().start()
                for k in range(3):
                    for r in range(D_MODEL // HEAD):
                        dm_st[8 * k + r] = dmod_ref[l, k:k + 1, r * HEAD:(r + 1) * HEAD]
                dm_st[DM_LOSS] = dmod_ref[l, 3:4, 0:HEAD]
                dm_all[:, pl.ds(dev, 1), :] = dm_st[...]
                for r in range(N_DEV - 1):
                    dm_copy(r).start()

            @pl.when(i == first_step + SMALL_SCATTER_STEP)
            def _():
                pk_load().wait()
                pk_give().wait()
                total = pk_mine[...] + pk_sib[...]
                pk_mine[...] = total
                pk_st[...] = total.astype(BF16)
                for ch in range(N_CHIP):
                    @pl.when(ch != p.chip)
                    def _(ch=ch):
                        pk_scatter(ch).start()

            @pl.when(i == first_step + SMALL_GATHER_STEP)
            def _():
                for ch in range(N_CHIP):
                    @pl.when(ch != p.chip)
                    def _(ch=ch):
                        pltpu.make_async_remote_copy(pk_fin, pk_rs.at[ch], rs_s.at[ch], rs_r.at[ch],
                                                     device_id=p.sibling, device_id_type=MESH).wait_recv()
                for me in range(N_CHIP):
                    @pl.when(me == p.chip)
                    def _(me=me):
                        total = None
                        for ch in range(N_CHIP):
                            part = pk_mine[me * PK_PIECE:(me + 1) * PK_PIECE] if ch == me else pk_rs[ch].astype(F32)
                            total = part if total is None else total + part
                        pk_fin[...] = total.astype(BF16)
                        pk_all[finished(c, me)] = total.astype(BF16)
                for ch in range(N_CHIP):
                    @pl.when(ch != p.chip)
                    def _(ch=ch):
                        pk_spread(ch).start()

            @pl.when(i == first_step + SMALL_PASS_STEP)
            def _():
                for ch in range(N_CHIP):
                    @pl.when(ch != p.chip)
                    def _(ch=ch):
                        pltpu.make_async_remote_copy(pk_fin, pk_all.at[finished(c, ch)], ag_s.at[ch], ag_r.at[ch],
                                                     device_id=p.sibling, device_id_type=MESH).wait_recv()
                pk_pass().start()

            @pl.when(i == first_step + SMALL_FINISH_STEP)
            def _():
                pk_pass().wait()
                for ch in range(N_CHIP):
                    @pl.when(ch != p.chip)
                    def _(ch=ch):
                        pk_scatter(ch).wait_send()
                        pk_spread(ch).wait_send()
                for r in range(N_DEV - 1):
                    dm_copy(r).wait()
                for cp in results():
                    cp.start()
                for cp in results():
                    cp.wait()

        n_small = 15
        for l in range(DEPTH):
            small_exchange(l, (DEPTH - 1 - l) * N_ITEMS, scratch[23 + n_small * l:23 + n_small * (l + 1)])

        for step in range(N_ITEMS, N_STEPS):
            @pl.when(i == step)
            def _(step=step):
                drain(step - N_ITEMS)

        ii = jnp.where(i < N_ITEMS, i, i - N_ITEMS)
        in_step = jnp.logical_and(ii >= OUT_FIRST, ii < OUT_FIRST + IN_STEPS)

        @pl.when(in_step)
        def _():
            k = ii - OUT_FIRST
            rhs = jnp.concatenate([r[...] for r in dp_refs], axis=1)
            res = _dot(h_ref[...], rhs, TN)

            @pl.when(c == 0)
            def _():
                mine_in[k] = res[:HALF_IN]
                send_in[k] = res[HALF_IN:]

            @pl.when(c == 1)
            def _():
                mine_in[k] = res[HALF_IN:]
                send_in[k] = res[:HALF_IN]

        @pl.when(jnp.logical_not(in_step))
        def _():
            k = jnp.where(ii < OUT_FIRST, ii, ii - IN_STEPS)
            res = _dot(cat_ref[...], dy_ref[...], TN)

            @pl.when(c == 0)
            def _():
                for ch in range(N_CHIP):
                    mine_out[k, ch] = res[ch * W_OUT_ROWS:ch * W_OUT_ROWS + HALF_OUT]
                    send_out[k, ch] = res[ch * W_OUT_ROWS + HALF_OUT:(ch + 1) * W_OUT_ROWS]

            @pl.when(c == 1)
            def _():
                for ch in range(N_CHIP):
                    mine_out[k, ch] = res[ch * W_OUT_ROWS + HALF_OUT:(ch + 1) * W_OUT_ROWS]
                    send_out[k, ch] = res[ch * W_OUT_ROWS:ch * W_OUT_ROWS + HALF_OUT]

        stages = ((0, lambda j: to_sibling(j).start()), (DELAY_SUM, stage_sum), (DELAY_SECOND, stage_second),
                  (DELAY_FINAL, stage_final))
        for step in range(N_STEPS):
            @pl.when(i == step)
            def _(step=step):
                for delay, stage in stages:
                    if step - delay >= 0:
                        stage(step - delay)

        @pl.when(i == N_STEPS - 1)
        def _():
            for step in range(N_STEPS, N_STEPS + DELAY_FINAL):
                for delay, stage in stages:
                    if 0 <= step - delay < N_STEPS:
                        stage(step - delay)
            for j in range(N_STEPS - N_ITEMS, N_STEPS):
                drain(j)

    hbm = pl.BlockSpec(memory_space=pl.ANY)

    def layer(i):
        return jnp.where(i < N_ITEMS, DEPTH - 1, 0)

    def item(i):
        return jnp.where(i < N_ITEMS, i, i - N_ITEMS)

    def whole(i):
        return (layer(i), 0, 0)

    def dproj_piece(ch):
        return pl.BlockSpec((None, SEQ, HEAD),
                            lambda i: (layer(i), 0, ch * IN_STEPS + jnp.clip(item(i) - OUT_FIRST, 0, IN_STEPS - 1)))

    def dy_quarter(i):
        return (layer(i), 0, jnp.where(item(i) < OUT_FIRST, item(i), jnp.maximum(item(i) - IN_STEPS, OUT_FIRST)))

    operand = pl.BlockSpec((None, SEQ, D_MODEL), whole)
    in_specs = [operand] + [dproj_piece(ch) for ch in range(N_CHIP)]
    in_specs += [operand, pl.BlockSpec((None, SEQ, OUT_COLS), dy_quarter), hbm, _const_in((DEPTH, 8, D_MODEL))]
    args = [h, dproj, dproj, dproj, dproj, cat, dy, pack, dmod]
    out_shape = [jax.ShapeDtypeStruct((DEPTH, D_MODEL, W_IN_COLS), F32), jax.ShapeDtypeStruct((DEPTH, 2, HALF_OUT, D_MODEL), F32),
                 jax.ShapeDtypeStruct((DEPTH, PK_ROWS, HEAD), BF16), jax.ShapeDtypeStruct((DEPTH, DM_ROWS, N_DEV, HEAD), F32)]
    out_specs = [hbm, hbm, hbm, hbm]
    in_item = lambda *lead: pltpu.VMEM(lead + (HALF_IN, HEAD), BF16)
    out_item = lambda *lead: pltpu.VMEM(lead + (HALF_OUT, OUT_COLS), BF16)
    scratch = [
        pltpu.VMEM((IN_STEPS, HALF_IN, N_CHIP * HEAD), F32), pltpu.VMEM((IN_STEPS, HALF_IN, N_CHIP * HEAD), F32),
        pltpu.VMEM((IN_STEPS, HALF_IN, N_CHIP * HEAD), F32), in_item(IN_STEPS, 3), in_item(IN_STEPS, 2), in_item(IN_STEPS),
        pltpu.VMEM((IN_STEPS, HALF_IN, HEAD), F32),
        pltpu.VMEM((OUT_STEPS, N_CHIP, HALF_OUT, OUT_COLS), F32), pltpu.VMEM((OUT_STEPS, N_CHIP, HALF_OUT, OUT_COLS), F32),
        pltpu.VMEM((OUT_STEPS, N_CHIP, HALF_OUT, OUT_COLS), F32), out_item(OUT_STEPS, 3), out_item(OUT_STEPS, 2),
        out_item(OUT_STEPS), pltpu.VMEM((OUT_STEPS, HALF_OUT, OUT_COLS), F32),
        pltpu.SemaphoreType.DMA((N_STEPS,)), pltpu.SemaphoreType.DMA((N_STEPS,)),
        pltpu.SemaphoreType.DMA((2 * N_STEPS,)), pltpu.SemaphoreType.DMA((2 * N_STEPS,)),
        pltpu.SemaphoreType.DMA((N_STEPS,)), pltpu.SemaphoreType.DMA((N_STEPS,)),
        pltpu.SemaphoreType.DMA((N_STEPS,)), pltpu.SemaphoreType.DMA((N_STEPS,)), pltpu.SemaphoreType.DMA((N_STEPS,)),
    ]
    for _ in range(DEPTH):
        scratch += [
            pltpu.VMEM((PK_HALF, HEAD), F32), pltpu.VMEM((PK_HALF, HEAD), F32), pltpu.VMEM((PK_HALF, HEAD), BF16),
            pltpu.VMEM((N_CHIP, PK_PIECE, HEAD), BF16), pltpu.VMEM((PK_PIECE, HEAD), BF16), pltpu.VMEM((PK_ROWS, HEAD), BF16),
            pltpu.VMEM((DM_ROWS, 1, HEAD), F32), pltpu.VMEM((DM_ROWS, N_DEV, HEAD), F32),
            pltpu.SemaphoreType.DMA((6,)),
            pltpu.SemaphoreType.DMA((N_CHIP,)), pltpu.SemaphoreType.DMA((N_CHIP,)),
            pltpu.SemaphoreType.DMA((N_CHIP,)), pltpu.SemaphoreType.DMA((N_CHIP,)),
            pltpu.SemaphoreType.DMA((N_DEV - 1,)), pltpu.SemaphoreType.DMA((N_DEV - 1,)),
        ]
    return pl.pallas_call(
        body, name="wgrad", grid=(N_STEPS,), in_specs=in_specs, out_specs=out_specs, out_shape=out_shape,
        scratch_shapes=scratch,
        compiler_params=pltpu.CompilerParams(dimension_semantics=("arbitrary",), vmem_limit_bytes=VMEM_LIMIT),
    )(*args)


def _adamw(w, g, m, v):
    m = ADAM_B1 * m + (1.0 - ADAM_B1) * g
    v = ADAM_B2 * v + (1.0 - ADAM_B2) * (g * g)
    m_hat = m / (1.0 - ADAM_B1 ** ADAM_STEP)
    v_hat = v / (1.0 - ADAM_B2 ** ADAM_STEP)
    delta = -ADAM_LR * (m_hat / (jnp.sqrt(v_hat) + ADAM_EPS) + ADAM_WD * w)
    return delta, m, v


def _adam_sharded(c_all, dmods, ada, w_in_set, w_out_set):
    rows = D_MODEL // ADAM_PARTS

    def body(c_ref, dm_ref, wa_ref, ma_ref, va_ref, wi_ref, gi_ref, mi_ref, vi_ref, wo_ref, go_ref, mo_ref, vo_ref,
             ga_out, da_out, ma_out, va_out, di_out, mi_out, vi_out, do_out, mo_out, vo_out):
        l = pl.program_id(0)
        chip = 2 * lax.axis_index("x") + lax.axis_index("y")
        cv = c_ref[...]
        silu_c = (cv * _sigmoid(cv)).astype(BF16).astype(F32)
        pieces = []
        for k in range(W_ADA_COLS // HEAD):
            dk = dm_ref[l, (W_ADA_COLS // HEAD) * chip + k].astype(BF16).astype(F32)
            pieces.append(_dot_exact(silu_c, dk, TN))
        g = jnp.concatenate(pieces, axis=1)
        ga_out[...] = g
        da_out[...], ma_out[...], va_out[...] = _adamw(wa_ref[...], g, ma_ref[...], va_ref[...])
        di_out[...], mi_out[...], vi_out[...] = _adamw(wi_ref[...], gi_ref[...], mi_ref[...], vi_ref[...])
        do_out[...], mo_out[...], vo_out[...] = _adamw(wo_ref[...], go_ref[...], mo_ref[...], vo_ref[...])

    def blk(r, cols):
        return pl.BlockSpec((None, r, cols), lambda l, i: (l, i, 0))

    b_ada, b_in, b_out = blk(rows, W_ADA_COLS), blk(rows, W_IN_COLS), blk(W_OUT_ROWS // ADAM_PARTS, D_MODEL)
    shapes = [jax.ShapeDtypeStruct(a[0].shape, F32) for a in (ada, w_in_set, w_out_set)]
    return pl.pallas_call(
        body, name="adam_sharded", grid=(DEPTH, ADAM_PARTS),
        in_specs=[pl.BlockSpec((N_DEV, rows), lambda l, i: (0, i)), _const_in((DEPTH, DM_ROWS, N_DEV, HEAD))]
        + [b_ada] * 3 + [b_in] * 4 + [b_out] * 4,
        out_specs=[b_ada] * 4 + [b_in] * 3 + [b_out] * 3,
        out_shape=[shapes[0]] * 4 + [shapes[1]] * 3 + [shapes[2]] * 3,
        compiler_params=pltpu.CompilerParams(dimension_semantics=("arbitrary", "arbitrary"), vmem_limit_bytes=VMEM_LIMIT),
    )(c_all, dmods, *ada, *w_in_set, *w_out_set)


def _adam_small(packs, dmods, weights, ms, vs):
    n = len(weights)

    def body(*refs):
        dm_refs = refs[1]
        b = 2
        w_refs, m_refs, v_refs = refs[b:b + n], refs[b + n:b + 2 * n], refs[b + 2 * n:b + 3 * n]
        outs = refs[b + 3 * n:b + 3 * n + 4 * n + 1]
        pack_refs = refs[-1]
        pack_refs[...] = refs[0][...].astype(F32)
        g_refs, d_refs, nm_refs, nv_refs = outs[0:n], outs[n:2 * n], outs[2 * n:3 * n], outs[3 * n:4 * n]
        squares = dm_refs[DEPTH - 1, DM_LOSS]
        total = squares[0:1, 0:1]
        for d in range(1, N_DEV):
            total = total + squares[d:d + 1, 0:1]
        outs[4 * n][...] = total * (0.5 / D_MODEL)

        def lanes(l, row0, count):
            return jnp.concatenate([pack_refs.at[l][row0 + k:row0 + k + 1, :] for k in range(count)], axis=1)

        def update(idx, at, g):
            g_refs[idx][at] = g
            d_refs[idx][at], nm_refs[idx][at], nv_refs[idx][at] = _adamw(w_refs[idx][at], g, m_refs[idx][at], v_refs[idx][at])

        for l in range(DEPTH):
            row = (slice(l, l + 1), slice(None))
            g_b = None
            for d in range(N_DEV):
                part = dm_refs.at[l][0:DM_LOSS, d, :]
                g_b = part if g_b is None else g_b + part
            update(0, row, jnp.concatenate([g_b[k:k + 1, :] for k in range(DM_LOSS)], axis=1))
            for g in range(N_HEAD):
                update(1, (l, g), pack_refs.at[l][PK_W_POOL + g * HEAD:PK_W_POOL + (g + 1) * HEAD, :])
                update(5, (l, g), pack_refs.at[l][PK_W_SGU + g * HEAD:PK_W_SGU + (g + 1) * HEAD, :])
            update(2, row, lanes(l, PK_POOL_SCALE, N_HEAD))
            update(3, (l,), pack_refs.at[l][PK_SGU_LN_G:PK_SGU_LN_G + N_HEAD, :])
            update(4, (l,), pack_refs.at[l][PK_SGU_LN_B:PK_SGU_LN_B + N_HEAD, :])
            update(6, (l,), pack_refs.at[l][PK_B_SGU:PK_B_SGU + N_HEAD, :])
            update(7, row, lanes(l, PK_LN_G, D_MODEL // HEAD))
            update(8, row, lanes(l, PK_LN_B, D_MODEL // HEAD))

    vmem = pl.BlockSpec(memory_space=pltpu.VMEM)
    shapes = [jax.ShapeDtypeStruct(w.shape, F32) for w in weights]
    return pl.pallas_call(
        body, name="adam_small", in_specs=[vmem] * (2 + 3 * n), out_specs=[vmem] * (4 * n + 1),
        out_shape=shapes * 4 + [jax.ShapeDtypeStruct((1, 1), F32)],
        scratch_shapes=[pltpu.VMEM(packs.shape, F32)],
        compiler_params=pltpu.CompilerParams(vmem_limit_bytes=VMEM_LIMIT),
    )(packs, dmods, *weights, *ms, *vs)


def kernel(x, c, w_ada, b_ada, w_in, w_pool, pool_scale, sgu_ln_g, sgu_ln_b, w_sgu, b_sgu, w_out, ln_g, ln_b, loss_target, m_w_ada, m_b_ada, m_w_in, m_w_pool, m_pool_scale, m_sgu_ln_g, m_sgu_ln_b, m_w_sgu, m_b_sgu, m_w_out, m_ln_g, m_ln_b, v_w_ada, v_b_ada, v_w_in, v_w_pool, v_pool_scale, v_sgu_ln_g, v_sgu_ln_b, v_w_sgu, v_b_sgu, v_w_out, v_ln_g, v_ln_b):
    small = (w_pool, pool_scale, sgu_ln_g, sgu_ln_b, w_sgu, b_sgu)
    *saved0, w_in0, w_out0, w_in1, w_out1, mod, c_all = _forward_first(x, c, w_ada, b_ada, w_in, w_out, small, ln_g, ln_b)
    *saved1, dout, sq = _forward_last(saved0[3], mod, w_in1, w_out1, small, ln_g, ln_b, loss_target)

    dx1, *shared = _backward_layer(1, dout, saved1, mod, w_in1, w_out1, small, ln_g, sq=sq)
    dx0, h, cat, dy, dproj, pack, dmod = _backward_layer(0, dx1, saved0, mod, w_in0, w_out0, small, ln_g, shared=shared)
    g_in, g_out, pack, dmods = _wgrad_reduce(h, dproj, cat, dy, pack, dmod)

    g_out = g_out.reshape(DEPTH, W_OUT_ROWS, D_MODEL)
    big = _adam_sharded(c_all, dmods, (w_ada, m_w_ada, v_w_ada), (w_in, g_in, m_w_in, v_w_in), (w_out, g_out, m_w_out, v_w_out))
    ada, win, wout = big[0:4], (g_in, *big[4:7]), (g_out, *big[7:10])
    small_w = (b_ada, w_pool, pool_scale, sgu_ln_g, sgu_ln_b, w_sgu, b_sgu, ln_g, ln_b)
    small_m = (m_b_ada, m_w_pool, m_pool_scale, m_sgu_ln_g, m_sgu_ln_b, m_w_sgu, m_b_sgu, m_ln_g, m_ln_b)
    small_v = (v_b_ada, v_w_pool, v_pool_scale, v_sgu_ln_g, v_sgu_ln_b, v_w_sgu, v_b_sgu, v_ln_g, v_ln_b)
    res = _adam_small(pack, dmods, small_w, small_m, small_v)
    n = len(small_w)
    loss = res[4 * n].reshape(())

    def ordered(k):
        s = res[k * n:(k + 1) * n]
        return (ada[k], s[0], win[k], s[1], s[2], s[3], s[4], s[5], s[6], wout[k], s[7], s[8])

    return (loss, dx0[None], *ordered(0), *ordered(1), *ordered(2), *ordered(3))
```

```python
import jax
import jax.numpy as jnp
from jax import lax
from jax.experimental import pallas as pl
from jax.experimental.pallas import tpu as pltpu

F32 = jnp.float32
BF16 = jnp.bfloat16
MESH = pl.DeviceIdType.MESH

N_DEV = 8
N_CHIP = 4
DEPTH = 2
SEQ = 2048
D_MODEL = 1024
D_POOL = 512
D_PROJ = 2560
HEAD = 128
N_HEAD = 4
ROWS = 256
N_TILE = SEQ // ROWS
HALO = 16
W_IN_COLS = D_PROJ // N_CHIP
W_OUT_ROWS = D_MODEL // N_CHIP
W_ADA_COLS = 3 * D_MODEL // N_CHIP
HALF_IN = D_MODEL // 2
HALF_OUT = W_OUT_ROWS // 2
DEEPNORM_ALPHA = (2.0 * DEPTH) ** 0.25
LN_EPS = 1e-5
INV_SQRT2 = 0.7071067811865476
INV_SQRT_2PI = 0.3989422804014327

ADAM_LR = 0.001
ADAM_B1 = 0.9
ADAM_B2 = 0.999
ADAM_EPS = 1e-08
ADAM_WD = 0.01
ADAM_STEP = 10
ADAM_PARTS = 2

PK_W_POOL = 0
PK_W_SGU = 512
PK_POOL_SCALE = 1024
PK_SGU_LN_G = 1032
PK_SGU_LN_B = 1040
PK_B_SGU = 1048
PK_LN_G = 1056
PK_LN_B = 1064
PK_ROWS = 1152
PK_HALF = PK_ROWS // 2
PK_PIECE = PK_HALF // N_CHIP
DM_LOSS = 3 * D_MODEL // HEAD
DM_ROWS = DM_LOSS + 1

VMEM_LIMIT = 56 * 1024 * 1024

GATHER_SECOND_ROUND_STEP = 0
GATHER_PASS_STEP = N_TILE - 2

NN = (((1,), (0,)), ((), ()))
NT = (((1,), (1,)), ((), ()))
TN = (((0,), (0,)), ((), ()))


def _dot(a, b, dims=NN):
    return lax.dot_general(a, b, dims, preferred_element_type=F32)


def _dot_exact(a, b, dims=NN):
    return lax.dot_general(a, b, dims, preferred_element_type=F32, precision=lax.Precision.HIGHEST)


def _layer_norm(v):
    mu = jnp.mean(v, axis=-1, keepdims=True)
    d = v - mu
    var = jnp.mean(d * d, axis=-1, keepdims=True)
    rstd = lax.rsqrt(var + LN_EPS)
    return d * rstd, rstd


def _layer_norm_bwd(dvhat, vhat, rstd):
    m1 = jnp.mean(dvhat, axis=-1, keepdims=True)
    m2 = jnp.mean(dvhat * vhat, axis=-1, keepdims=True)
    return rstd * (dvhat - m1 - vhat * m2)


def _sigmoid(v):
    return 1.0 / (1.0 + jnp.exp(-v))


def _gelu_parts(v):
    phi = 0.5 * (1.0 + lax.erf(v * INV_SQRT2))
    pdf = INV_SQRT_2PI * jnp.exp(-0.5 * v * v)
    return phi, pdf


def _sum_rows(v):
    return jnp.sum(v, axis=0, keepdims=True)


def _window_sums(ext, toward_later):
    n = ext.shape[0]

    def shifted(v, k):
        return pltpu.roll(v, (n - k) if toward_later else k, 0)

    s2 = ext + shifted(ext, 1)
    r4 = s2[:, HEAD:]
    s4 = r4 + shifted(r4, 2)
    r8 = s4[:, HEAD:]
    s8 = r8 + shifted(r8, 4)
    r16 = s8[:, HEAD:]
    s16 = r16 + shifted(r16, 8)
    return jnp.concatenate([s2[:, :HEAD], s4[:, :HEAD], s8[:, :HEAD], s16], axis=1)


def _window_counts(row0):
    t1 = row0 + 1 + lax.broadcasted_iota(jnp.int32, (ROWS, D_POOL), 0)
    lane = lax.broadcasted_iota(jnp.int32, (ROWS, D_POOL), 1)
    width = jnp.where(lane < HEAD, 2, jnp.where(lane < 2 * HEAD, 4, jnp.where(lane < 3 * HEAD, 8, 16)))
    return jnp.minimum(t1, width).astype(F32)


def _causal_mask():
    r = lax.broadcasted_iota(jnp.int32, (HEAD, HEAD), 0)
    s = lax.broadcasted_iota(jnp.int32, (HEAD, HEAD), 1)
    return r >= s


def _chunks_to_lanes(v):
    return jnp.concatenate([v[n * HEAD:(n + 1) * HEAD] for n in range(ROWS // HEAD)], axis=1)


def _lanes_to_chunks(v):
    return jnp.concatenate([v[:, n * HEAD:(n + 1) * HEAD] for n in range(ROWS // HEAD)], axis=0)


def _pack_stats(rstd_x, rstd_z, rstd_v):
    lane = lax.broadcasted_iota(jnp.int32, (ROWS, HEAD), 1)
    packed = rstd_x
    for k, r in enumerate([rstd_z] + list(rstd_v)):
        packed = jnp.where(lane < 16 * (k + 1), packed, r)
    return packed


def _unpack_stats(stats):
    cols = [stats[:, 16 * k:16 * k + 1] for k in range(2 + N_HEAD)]
    return cols[0], cols[1], cols[2:]


def _mixer(proj, halo, row0, wpool_ref, pscale, sgu_g_ref, sgu_b_ref, wsgu_ref, bsgu_ref, saved=None):
    xa = proj[:, 0:512]
    ga = proj[:, 512:1024]
    u = proj[:, 1024:1536]
    v = proj[:, 1536:2048]
    gb = proj[:, 2048:2560]
    ext = jnp.concatenate([halo, xa], axis=0)
    win = _window_sums(ext, toward_later=False)[HALO:]
    cnt = _window_counts(row0)
    pooled = (win / cnt - xa).astype(BF16)
    pw = jnp.concatenate(
        [_dot(pooled[:, g * HEAD:(g + 1) * HEAD], wpool_ref[g].astype(BF16)) for g in range(N_HEAD)], axis=1)
    sig_a = _sigmoid(ga) if saved is None else saved["sig_a"]
    ya = pw * pscale * (ga * sig_a)
    phi_u, pdf_u = _gelu_parts(u)
    phi_v, pdf_v = _gelu_parts(v)
    gu = u * phi_u
    gv = v * phi_v
    sig_b = _sigmoid(gb) if saved is None else saved["sig_b"]
    silu_b = gb * sig_b
    mask = _causal_mask()
    diag = lax.broadcasted_iota(jnp.int32, (HEAD, HEAD), 0) == lax.broadcasted_iota(jnp.int32, (HEAD, HEAD), 1)
    vhat, rstd_v, vln_l, mixed = [], [], [], []
    for h in range(N_HEAD):
        if saved is None:
            vh, rh = _layer_norm(gv[:, h * HEAD:(h + 1) * HEAD])
        else:
            vh, rh = saved["vhat"][h], saved["rstd_v"][h]
        ln = (vh * sgu_g_ref[h:h + 1, :] + sgu_b_ref[h:h + 1, :]).astype(BF16)
        ln_l = _chunks_to_lanes(ln)
        wm = jnp.where(mask, wsgu_ref[h], 0.0).astype(BF16)
        bias = jnp.sum(jnp.where(diag, jnp.broadcast_to(bsgu_ref[h:h + 1, :], (HEAD, HEAD)), 0.0), axis=1, keepdims=True)
        mx = _lanes_to_chunks(_dot(wm, ln_l) + bias)
        vhat.append(vh)
        rstd_v.append(rh)
        vln_l.append(ln_l)
        mixed.append(mx)
    mixed = jnp.concatenate(mixed, axis=1)
    yb = gu * mixed * silu_b
    return dict(xa=xa, ga=ga, u=u, v=v, gb=gb, cnt=cnt, pooled=pooled, pw=pw, sig_a=sig_a, ya=ya, phi_u=phi_u, pdf_u=pdf_u,
                phi_v=phi_v, pdf_v=pdf_v, gu=gu, sig_b=sig_b, silu_b=silu_b, vhat=vhat, rstd_v=rstd_v, vln_l=vln_l,
                mixed=mixed, yb=yb, mask=mask)


def _const(shape, *index):
    lead = tuple(index) + (0,) * (len(shape) - len(index))
    return pl.BlockSpec(shape, lambda *_: lead)


def _const_in(shape, *index):
    lead = tuple(index) + (0,) * (len(shape) - len(index))
    return pl.BlockSpec(shape, lambda *_: lead, pipeline_mode=pl.Buffered(1))


def _layer_weight_specs(l):
    return [
        _const_in((None, N_HEAD, HEAD, HEAD), l),
        _const_in((DEPTH, D_POOL)),
        _const_in((None, N_HEAD, HEAD), l),
        _const_in((None, N_HEAD, HEAD), l),
        _const_in((None, N_HEAD, HEAD, HEAD), l),
        _const_in((None, N_HEAD, HEAD), l),
    ]


def _forward_tile(l, i, x_ref, mod_ref, win_ref, wout_ref, small_refs, lng_ref, lnb_ref, carry_ref, saved_refs):
    wpool_ref, pscale_ref, sgu_g_ref, sgu_b_ref, wsgu_ref, bsgu_ref = small_refs
    proj_ref, y_ref, xn_ref, zn_ref, stats_ref, sig_ref, vhat_ref = saved_refs
    x = x_ref[...]
    if l > 0:
        x = x * lng_ref[l - 1:l, :] + lnb_ref[l - 1:l, :]
    shift, scale, gate = mod_ref[0:1, :], mod_ref[1:2, :], mod_ref[2:3, :]
    xn, rstd_x = _layer_norm(x)
    xn_ref[...] = xn.astype(xn_ref.dtype)
    h = xn * (1.0 + scale) + shift
    proj = _dot(h.astype(BF16), win_ref[...])
    proj_ref[...] = proj.astype(proj_ref.dtype)
    m = _mixer(proj, carry_ref[...], i * ROWS, wpool_ref, pscale_ref[l:l + 1, :], sgu_g_ref, sgu_b_ref, wsgu_ref, bsgu_ref)
    carry_ref[...] = m["xa"][ROWS - HALO:]
    sig_ref[...] = jnp.concatenate([m["sig_a"], m["sig_b"]], axis=1).astype(sig_ref.dtype)
    vhat_ref[...] = jnp.concatenate(m["vhat"], axis=1).astype(vhat_ref.dtype)
    cat = jnp.concatenate([m["ya"], m["yb"]], axis=1).astype(BF16)
    y = _dot(cat, wout_ref[...])
    y_ref[...] = y.astype(y_ref.dtype)
    zn, rstd_z = _layer_norm(DEEPNORM_ALPHA * x + gate * y)
    zn_ref[...] = zn
    stats_ref[...] = _pack_stats(rstd_x, rstd_z, m["rstd_v"])
    return zn


SAVED_COLS = (D_PROJ, D_MODEL, D_MODEL, D_MODEL, HEAD, D_MODEL, D_POOL)
SAVED_TYPES = (BF16, BF16, BF16, F32, F32, BF16, BF16)


def _saved_outputs():
    return ([jax.ShapeDtypeStruct((SEQ, cols), t) for cols, t in zip(SAVED_COLS, SAVED_TYPES)],
            [pl.BlockSpec((ROWS, cols), lambda i: (i, 0)) for cols in SAVED_COLS])


def _forward_last(zn_prev, mod, w_in, w_out, small, ln_g, ln_b, target):
    l = DEPTH - 1
    n_saved = len(SAVED_COLS)

    def body(*refs):
        x_ref, mod_ref, win_ref, wout_ref = refs[:4]
        small_refs, lng_ref, lnb_ref, tgt_ref = refs[4:10], refs[10], refs[11], refs[12]
        saved_refs = refs[13:13 + n_saved]
        dout_ref, loss_ref, carry_ref = refs[13 + n_saved:]
        i = pl.program_id(0)

        @pl.when(i == 0)
        def _():
            carry_ref[...] = jnp.zeros_like(carry_ref)
            loss_ref[...] = jnp.zeros_like(loss_ref)

        zn = _forward_tile(l, i, x_ref, mod_ref, win_ref, wout_ref, small_refs, lng_ref, lnb_ref, carry_ref, saved_refs)
        err = zn * lng_ref[l:l + 1, :] + lnb_ref[l:l + 1, :] - tgt_ref[...]
        dout_ref[...] = err * (1.0 / D_MODEL)
        loss_ref[...] += jnp.sum(err * err)

    tile = pl.BlockSpec((ROWS, D_MODEL), lambda i: (i, 0))
    tile3 = pl.BlockSpec((None, ROWS, D_MODEL), lambda i: (0, i, 0))
    in_specs = [tile, _const_in((None, 8, D_MODEL), l), _const_in((D_MODEL, D_PROJ)), _const_in((D_MODEL, D_MODEL))]
    in_specs += _layer_weight_specs(l) + [_const_in((DEPTH, D_MODEL)), _const_in((DEPTH, D_MODEL)), tile3]
    out_shape, out_specs = _saved_outputs()
    out_shape += [jax.ShapeDtypeStruct((SEQ, D_MODEL), F32), jax.ShapeDtypeStruct((8, HEAD), F32)]
    out_specs += [tile, _const((8, HEAD))]
    return pl.pallas_call(
        body, name="fwd_last", grid=(N_TILE,), in_specs=in_specs, out_specs=out_specs, out_shape=out_shape,
        scratch_shapes=[pltpu.VMEM((HALO, D_POOL), F32)],
        compiler_params=pltpu.CompilerParams(dimension_semantics=("arbitrary",), vmem_limit_bytes=VMEM_LIMIT),
    )(zn_prev, mod, w_in, w_out, *small, ln_g, ln_b, target)


def _backward_layer(l, dout, saved, mod, w_in, w_out, small, ln_g, sq=None, shared=None):
    has_loss = sq is not None

    def body(*refs):
        (dout_ref, proj_ref, y_ref, xn_ref, zn_ref, stats_ref, sig_ref, vhat_ref, halo_ref, mod_ref, win_ref, wout_ref,
         wpool_ref, pscale_ref, sgu_g_ref, sgu_b_ref, wsgu_ref, bsgu_ref, lng_ref) = refs[:19]
        n_in = 20 if has_loss else 19 + 6
        dx_ref, h_ref, cat_ref, dy_ref, dproj_ref, pack_ref, dmod_ref, carry_ref = refs[n_in:n_in + 8]
        i = pl.program_id(0)
        tile = N_TILE - 1 - i

        @pl.when(i == 0)
        def _():
            carry_ref[...] = jnp.zeros_like(carry_ref)
            pack_ref[...] = jnp.zeros_like(pack_ref)
            dmod_ref[...] = jnp.zeros_like(dmod_ref)
            if has_loss:
                dmod_ref[3:4, 0:HEAD] = refs[19][0:1, :]

        xn = xn_ref[...].astype(F32)
        zn = zn_ref[...]
        y = y_ref[...].astype(F32)
        dout = dout_ref[...]
        rstd_x, rstd_z, rstd_v = _unpack_stats(stats_ref[...])
        kept = dict(sig_a=sig_ref[:, :D_POOL].astype(F32), sig_b=sig_ref[:, D_POOL:].astype(F32), rstd_v=rstd_v,
                    vhat=[vhat_ref[:, hd * HEAD:(hd + 1) * HEAD].astype(F32) for hd in range(N_HEAD)])
        pscale = pscale_ref[l:l + 1, :]
        shift, scale, gate = mod_ref[0:1, :], mod_ref[1:2, :], mod_ref[2:3, :]
        h = xn * (1.0 + scale) + shift
        h_ref[...] = h.astype(BF16)
        g_ln_g = _sum_rows(dout * zn)
        g_ln_b = _sum_rows(dout)
        dz = _layer_norm_bwd(dout * lng_ref[l:l + 1, :], zn, rstd_z)
        d_gate = _sum_rows(dz * y)
        dy = (gate * dz).astype(BF16)
        dy_ref[...] = dy

        halo = jnp.where(tile > 0, halo_ref[...].astype(F32), 0.0)
        m = _mixer(proj_ref[...].astype(F32), halo, tile * ROWS, wpool_ref, pscale, sgu_g_ref, sgu_b_ref, wsgu_ref, bsgu_ref,
                   saved=kept)
        cat_ref[...] = jnp.concatenate([m["ya"], m["yb"]], axis=1).astype(BF16)
        dcat = _dot(dy, wout_ref[...], NT)
        dya = dcat[:, :D_POOL]
        dyb = dcat[:, D_POOL:]

        ga, sig_a = m["ga"], m["sig_a"]
        dp = dya * (ga * sig_a)
        d_ga = dya * (m["pw"] * pscale) * (sig_a * (1.0 + ga * (1.0 - sig_a)))
        g_pscale = _sum_rows(dp * m["pw"])
        dpw = (dp * pscale).astype(BF16)
        dpooled = []
        for g in range(N_HEAD):
            cols = slice(g * HEAD, (g + 1) * HEAD)
            pack_ref[PK_W_POOL + g * HEAD:PK_W_POOL + (g + 1) * HEAD, :] += _dot(m["pooled"][:, cols], dpw[:, cols], TN)
            dpooled.append(_dot(dpw[:, cols], wpool_ref[g].astype(BF16), NT))
        dpooled = jnp.concatenate(dpooled, axis=1)
        q = dpooled / m["cnt"]
        ext = jnp.concatenate([q, carry_ref[...]], axis=0)
        d_xa = _window_sums(ext, toward_later=True)[:ROWS] - dpooled
        carry_ref[...] = q[:HALO]

        gu, mixed, silu_b, gb, sig_b = m["gu"], m["mixed"], m["silu_b"], m["gb"], m["sig_b"]
        d_mixed = dyb * gu * silu_b
        d_gu = dyb * mixed * silu_b
        d_gb = dyb * gu * mixed * (sig_b * (1.0 + gb * (1.0 - sig_b)))
        d_u = d_gu * (m["phi_u"] + m["u"] * m["pdf_u"])
        ones = jnp.ones((8, HEAD), F32)
        d_v = []
        for hd in range(N_HEAD):
            cols = slice(hd * HEAD, (hd + 1) * HEAD)
            dm = d_mixed[:, cols]
            dm_l = _chunks_to_lanes(dm.astype(BF16))
            g_w = _dot(dm_l, m["vln_l"][hd], NT)
            pack_ref[PK_W_SGU + hd * HEAD:PK_W_SGU + (hd + 1) * HEAD, :] += jnp.where(m["mask"], g_w, 0.0)
            dm_sum = dm[0:HEAD]
            for n in range(1, ROWS // HEAD):
                dm_sum = dm_sum + dm[n * HEAD:(n + 1) * HEAD]
            pack_ref[PK_B_SGU + hd:PK_B_SGU + hd + 1, :] += _dot_exact(ones, dm_sum, NT)[0:1]
            wm = jnp.where(m["mask"], wsgu_ref[hd], 0.0).astype(BF16)
            d_vln = _lanes_to_chunks(_dot(wm, dm_l, TN))
            vhat = m["vhat"][hd]
            pack_ref[PK_SGU_LN_G + hd:PK_SGU_LN_G + hd + 1, :] += _sum_rows(d_vln * vhat)
            pack_ref[PK_SGU_LN_B + hd:PK_SGU_LN_B + hd + 1, :] += _sum_rows(d_vln)
            d_v.append(_layer_norm_bwd(d_vln * sgu_g_ref[hd:hd + 1, :], vhat, m["rstd_v"][hd]))
        v = m["v"]
        d_v = jnp.concatenate(d_v, axis=1) * (m["phi_v"] + v * m["pdf_v"])

        dproj = jnp.concatenate([d_xa, d_ga, d_u, d_v, d_gb], axis=1).astype(BF16)
        dproj_ref[...] = dproj
        dh = _dot(dproj, win_ref[...], NT)
        d_scale = _sum_rows(dh * xn)
        d_shift = _sum_rows(dh)
        dx_ref[...] = DEEPNORM_ALPHA * dz + _layer_norm_bwd(dh * (1.0 + scale), xn, rstd_x)

        dmod_ref[0:1, :] += d_shift
        dmod_ref[1:2, :] += d_scale
        dmod_ref[2:3, :] += d_gate
        for g in range(N_HEAD):
            pack_ref[PK_POOL_SCALE + g:PK_POOL_SCALE + g + 1, :] += g_pscale[:, g * HEAD:(g + 1) * HEAD]
        for k in range(D_MODEL // HEAD):
            pack_ref[PK_LN_G + k:PK_LN_G + k + 1, :] += g_ln_g[:, k * HEAD:(k + 1) * HEAD]
            pack_ref[PK_LN_B + k:PK_LN_B + k + 1, :] += g_ln_b[:, k * HEAD:(k + 1) * HEAD]

    def rev(i):
        return (N_TILE - 1 - i, 0)

    tile = pl.BlockSpec((ROWS, D_MODEL), rev)
    halo = pl.BlockSpec((HALO, D_POOL), lambda i: (jnp.maximum((N_TILE - 1 - i) * (ROWS // HALO) - 1, 0), 0))
    in_specs = [tile] + [pl.BlockSpec((ROWS, a.shape[1]), rev) for a in saved] + [halo]
    in_specs += [_const_in((None, 8, D_MODEL), l), _const_in((D_MODEL, D_PROJ)), _const_in((D_MODEL, D_MODEL))]
    in_specs += _layer_weight_specs(l) + [_const_in((DEPTH, D_MODEL))]
    args = [dout, *saved, saved[0], mod, w_in, w_out, *small, ln_g]
    stacked = lambda cols: pl.BlockSpec((None, ROWS, cols), lambda i: (l, N_TILE - 1 - i, 0))
    out_shape = [jax.ShapeDtypeStruct((SEQ, D_MODEL), F32), jax.ShapeDtypeStruct((DEPTH, SEQ, D_MODEL), BF16),
                 jax.ShapeDtypeStruct((DEPTH, SEQ, D_MODEL), BF16), jax.ShapeDtypeStruct((DEPTH, SEQ, D_MODEL), BF16),
                 jax.ShapeDtypeStruct((DEPTH, SEQ, D_PROJ), BF16), jax.ShapeDtypeStruct((DEPTH, PK_ROWS, HEAD), F32),
                 jax.ShapeDtypeStruct((DEPTH, 8, D_MODEL), F32)]
    out_specs = [tile, stacked(D_MODEL), stacked(D_MODEL), stacked(D_MODEL), stacked(D_PROJ),
                 _const((None, PK_ROWS, HEAD), l), _const((None, 8, D_MODEL), l)]
    aliases = {}
    if has_loss:
        in_specs.append(_const_in((8, HEAD)))
        args.append(sq)
    else:
        aliases = {len(args) + k: 1 + k for k in range(len(shared))}
        in_specs += [pl.BlockSpec(memory_space=pl.ANY)] * len(shared)
        args += list(shared)
    return pl.pallas_call(
        body, name="bwd_last" if has_loss else "bwd_first", grid=(N_TILE,), in_specs=in_specs, out_specs=out_specs,
        out_shape=out_shape, scratch_shapes=[pltpu.VMEM((HALO, D_POOL), F32)], input_output_aliases=aliases,
        compiler_params=pltpu.CompilerParams(dimension_semantics=("arbitrary",), vmem_limit_bytes=VMEM_LIMIT),
    )(*args)


def _flip(v, f):
    return v + f - 2 * v * f


class _Place:
    def __init__(self):
        x, y, c = lax.axis_index("x"), lax.axis_index("y"), lax.axis_index("c")
        self.x, self.y, self.c = x, y, c
        self.chip = 2 * x + y
        self.dev = 4 * x + 2 * y + c
        self.sibling = (x, y, 1 - c)
        x1, y1 = _flip(x, 1 - c), _flip(y, c)
        x2, y2 = _flip(x, c), _flip(y, 1 - c)
        self.first = (x1, y1, c)
        self.second = (x2, y2, c)
        self.chip_first = 2 * x1 + y1
        self.chip_second = 2 * x2 + y2
        self.chip_far = 2 * (1 - x) + (1 - y)
        self.my_first_coord = jnp.where(c == 0, x, y)

    def first_coord(self, ch):
        return jnp.where(self.c == 0, ch // 2, ch % 2)

    def others(self):
        return [(_flip(self.x, (r >> 2) & 1), _flip(self.y, (r >> 1) & 1), _flip(self.c, r & 1)) for r in range(1, N_DEV)]

    def other_chips(self):
        return [(1 - self.x, self.y), (self.x, 1 - self.y), (1 - self.x, 1 - self.y)]


class _WeightGather:
    CHUNKS = 4
    N_SEMS = 12 * CHUNKS

    def __init__(self, place, win, wout, send, recv):
        self.p, self.win, self.wout, self.send, self.recv = place, win, wout, send, recv
        p = place
        self.plan = [(p.chip, p.first), (p.chip, p.second), (p.chip_first, p.second),
                     (p.chip_first, p.sibling), (p.chip_second, p.sibling), (p.chip_far, p.sibling)]

    def _copies(self, k, q):
        ch, target = self.plan[k]
        n_in, n_out = HALF_IN // self.CHUNKS, HALF_OUT // self.CHUNKS
        rows_in = pl.ds(pl.multiple_of(self.p.c * HALF_IN + q * n_in, n_in), n_in)
        cols_in = pl.ds(pl.multiple_of(ch * W_IN_COLS, 128), W_IN_COLS)
        rows_out = pl.ds(pl.multiple_of(ch * W_OUT_ROWS + self.p.c * HALF_OUT + q * n_out, n_out), n_out)
        r_in = self.win.at[rows_in, cols_in]
        r_out = self.wout.at[rows_out, :]
        s = 2 * (6 * q + k)
        return [pltpu.make_async_remote_copy(r_in, r_in, self.send.at[s], self.recv.at[s],
                                             device_id=target, device_id_type=MESH),
                pltpu.make_async_remote_copy(r_out, r_out, self.send.at[s + 1], self.recv.at[s + 1],
                                             device_id=target, device_id_type=MESH)]

    def _start(self, k, q):
        for cp in self._copies(k, q):
            cp.start()

    def _landed(self, k, q):
        for cp in self._copies(k, q):
            cp.wait_recv()

    def start_first_round(self):
        for q in range(self.CHUNKS):
            self._start(0, q)

    def start_second_round(self, q):
        self._landed(0, q)
        self._start(1, q)
        self._start(2, q)
        self._start(3, q)

    def pass_second_round(self, q):
        self._landed(1, q)
        self._start(4, q)
        self._landed(2, q)
        self._start(5, q)

    def finish(self):
        for q in range(self.CHUNKS):
            for k in (3, 4, 5):
                self._landed(k, q)
        for q in range(self.CHUNKS):
            for k in range(len(self.plan)):
                for cp in self._copies(k, q):
                    cp.wait_send()


def _forward_first(x, c_vec, w_ada, b_ada, w_in, w_out, small, ln_g, ln_b):
    n_saved = len(SAVED_COLS)

    def body(*refs):
        x_ref, c_ref, wada_hbm, bada_ref, win_hbm, wout_hbm = refs[:6]
        small_refs, lng_ref, lnb_ref = refs[6:12], refs[12], refs[13]
        saved_refs = refs[14:14 + n_saved]
        win0, wout0, win1, wout1, mod_out, c_out = refs[14 + n_saved:20 + n_saved]
        (carry_ref, wada_ref, win_ref, wout_ref, win_bf, wout_bf, mod_mine, mod_all, c_all, mod_ref, win_v, wout_v,
         g0_send, g0_recv, g1_send, g1_recv, c_send, c_recv, mod_send, mod_recv, local_sem) = refs[20 + n_saved:]
        i = pl.program_id(0)
        p = _Place()
        gather0 = _WeightGather(p, win_v, wout_v, g0_send, g0_recv)
        keep = [pltpu.make_async_copy(win_v, win0, local_sem.at[9]), pltpu.make_async_copy(wout_v, wout0, local_sem.at[10])]
        gather1 = _WeightGather(p, win1, wout1, g1_send, g1_recv)

        @pl.when(i == 0)
        def _():
            carry_ref[...] = jnp.zeros_like(carry_ref)
            loads = [pltpu.make_async_copy(win_hbm.at[0], win_ref.at[0], local_sem.at[4]),
                     pltpu.make_async_copy(wout_hbm.at[0], wout_ref.at[0], local_sem.at[5]),
                     pltpu.make_async_copy(win_hbm.at[1], win_ref.at[1], local_sem.at[6]),
                     pltpu.make_async_copy(wout_hbm.at[1], wout_ref.at[1], local_sem.at[7]),
                     pltpu.make_async_copy(wada_hbm, wada_ref, local_sem.at[8])]
            for cp in loads:
                cp.start()

            c_all[pl.ds(p.dev, 1), :] = c_ref[...]
            mine = c_all.at[pl.ds(p.dev, 1), :]
            c_copies = [pltpu.make_async_remote_copy(mine, mine, c_send.at[r], c_recv.at[r], device_id=d, device_id_type=MESH)
                        for r, d in enumerate(p.others())]
            for cp in c_copies:
                cp.start()

            cols = pl.ds(pl.multiple_of(p.chip * W_IN_COLS, 128), W_IN_COLS)
            rows = pl.ds(pl.multiple_of(p.chip * W_OUT_ROWS, W_OUT_ROWS), W_OUT_ROWS)
            own = [pltpu.make_async_copy(win_bf.at[0], win_v.at[:, cols], local_sem.at[0]),
                   pltpu.make_async_copy(wout_bf.at[0], wout_v.at[rows, :], local_sem.at[1]),
                   pltpu.make_async_copy(win_bf.at[1], win1.at[:, cols], local_sem.at[2]),
                   pltpu.make_async_copy(wout_bf.at[1], wout1.at[rows, :], local_sem.at[3])]
            for l in range(DEPTH):
                loads[2 * l].wait()
                win_bf[l] = win_ref[l].astype(BF16)
                own[2 * l].start()
                loads[2 * l + 1].wait()
                wout_bf[l] = wout_ref[l].astype(BF16)
                own[2 * l + 1].start()
                if l == 0:
                    own[0].wait()
                    own[1].wait()
                    gather0.start_first_round()
            for cp in c_copies:
                cp.wait()
            loads[4].wait()

            cv = c_all[...]
            c_out[...] = cv
            silu_c = (cv * _sigmoid(cv)).astype(BF16)
            for l in range(DEPTH):
                mod_mine[l] = _dot(silu_c, wada_ref[l].astype(BF16))
            mod_all[p.chip] = mod_mine[...]
            m_copies = [pltpu.make_async_remote_copy(mod_mine, mod_all.at[p.chip], mod_send.at[k], mod_recv.at[k],
                                                     device_id=(px, py, p.c), device_id_type=MESH)
                        for k, (px, py) in enumerate(p.other_chips())]
            for cp in m_copies:
                cp.start()
            for q in range(gather0.CHUNKS):
                gather0.start_second_round(q)
            own[2].wait()
            own[3].wait()
            gather1.start_first_round()
            for cp in m_copies:
                cp.wait()
            mod_ref[...] = jnp.zeros_like(mod_ref)
            for l in range(DEPTH):
                full = jnp.concatenate([mod_all[ch, l, pl.ds(p.dev, 1), :] for ch in range(N_CHIP)], axis=1) + bada_ref[l:l + 1, :]
                for k in range(3):
                    mod_ref[l, k:k + 1, :] = full[:, k * D_MODEL:(k + 1) * D_MODEL]
            mod_out[...] = mod_ref[...]
            for q in range(gather0.CHUNKS):
                gather0.pass_second_round(q)
            gather0.finish()
            for cp in keep:
                cp.start()

        for q in range(_WeightGather.CHUNKS):
            @pl.when(i == GATHER_SECOND_ROUND_STEP + q)
            def _(q=q):
                gather1.start_second_round(q)

            @pl.when(i == GATHER_PASS_STEP)
            def _(q=q):
                gather1.pass_second_round(q)

        _forward_tile(0, i, x_ref, mod_ref.at[0], win_v, wout_v, small_refs, lng_ref, lnb_ref, carry_ref, saved_refs)

        @pl.when(i == N_TILE - 1)
        def _():
            gather1.finish()
            for cp in keep:
                cp.wait()

    hbm = pl.BlockSpec(memory_space=pl.ANY)
    tile3 = pl.BlockSpec((None, ROWS, D_MODEL), lambda i: (0, i, 0))
    in_specs = [tile3, _const_in((1, D_MODEL)), hbm, _const_in((DEPTH, 3 * D_MODEL)), hbm, hbm]
    in_specs += _layer_weight_specs(0) + [_const_in((DEPTH, D_MODEL)), _const_in((DEPTH, D_MODEL))]
    out_shape, out_specs = _saved_outputs()
    w_in_shape = jax.ShapeDtypeStruct((D_MODEL, D_PROJ), BF16)
    w_out_shape = jax.ShapeDtypeStruct((D_MODEL, D_MODEL), BF16)
    out_shape += [w_in_shape, w_out_shape, w_in_shape, w_out_shape,
                  jax.ShapeDtypeStruct((DEPTH, 8, D_MODEL), F32), jax.ShapeDtypeStruct((N_DEV, D_MODEL), F32)]
    out_specs += [hbm, hbm, hbm, hbm, _const((DEPTH, 8, D_MODEL)), _const((N_DEV, D_MODEL))]
    gather_sems = [pltpu.SemaphoreType.DMA((_WeightGather.N_SEMS,))] * 4
    scratch = [
        pltpu.VMEM((HALO, D_POOL), F32),
        pltpu.VMEM(w_ada.shape, F32), pltpu.VMEM(w_in.shape, F32), pltpu.VMEM(w_out.shape, F32),
        pltpu.VMEM((DEPTH, D_MODEL, W_IN_COLS), BF16), pltpu.VMEM((DEPTH, W_OUT_ROWS, D_MODEL), BF16),
        pltpu.VMEM((DEPTH, N_DEV, W_ADA_COLS), F32), pltpu.VMEM((N_CHIP, DEPTH, N_DEV, W_ADA_COLS), F32),
        pltpu.VMEM((N_DEV, D_MODEL), F32), pltpu.VMEM((DEPTH, 8, D_MODEL), F32),
        pltpu.VMEM((D_MODEL, D_PROJ), BF16), pltpu.VMEM((D_MODEL, D_MODEL), BF16),
    ] + gather_sems + [
        pltpu.SemaphoreType.DMA((7,)), pltpu.SemaphoreType.DMA((7,)),
        pltpu.SemaphoreType.DMA((3,)), pltpu.SemaphoreType.DMA((3,)),
        pltpu.SemaphoreType.DMA((11,)),
    ]
    return pl.pallas_call(
        body, name="fwd_first", grid=(N_TILE,), in_specs=in_specs, out_specs=out_specs, out_shape=out_shape,
        scratch_shapes=scratch,
        compiler_params=pltpu.CompilerParams(dimension_semantics=("arbitrary",), vmem_limit_bytes=VMEM_LIMIT),
    )(x, c_vec, w_ada, b_ada, w_in, w_out, *small, ln_g, ln_b)


IN_STEPS = W_IN_COLS // HEAD
OUT_STEPS = 4
OUT_COLS = D_MODEL // OUT_STEPS
OUT_FIRST = 2
ITEMS = ([("out", k) for k in range(OUT_FIRST)] + [("in", k) for k in range(IN_STEPS)]
         + [("out", k) for k in range(OUT_FIRST, OUT_STEPS)])
N_ITEMS = len(ITEMS)
N_STEPS = DEPTH * N_ITEMS
DELAY_SUM, DELAY_SECOND, DELAY_FINAL = 1, 3, 5
SMALL_SCATTER_STEP, SMALL_GATHER_STEP, SMALL_PASS_STEP, SMALL_FINISH_STEP = 1, 3, 5, 7


def _wgrad_reduce(h, dproj, cat, dy, pack, dmod):
    def body(*refs):
        h_ref, dp_refs, cat_ref, dy_ref, pack_ref, dmod_ref = refs[0], refs[1:5], refs[5], refs[6], refs[7], refs[8]
        fin_in, fin_out, pack_out, dmod_out = refs[9:13]
        scratch = refs[13:]
        (mine_in, send_in, sib_in, st_in, r1_in, r2_in, f_in,
         mine_out, send_out, sib_out, st_out, r1_out, r2_out, f_out,
         d2d_s, d2d_r, r1_s, r1_r, r2_s, r2_r, fin_l, fin_s, fin_r) = scratch[:23]
        p = _Place()
        c = p.c
        i = pl.program_id(0)
        my_rows = pl.ds(pl.multiple_of(c * HALF_IN, HALF_IN), HALF_IN)

        def layer_of(j):
            return DEPTH - 1 - j // N_ITEMS

        def bufs(j):
            kind, k = ITEMS[j % N_ITEMS]
            if kind == "in":
                return [r.at[k] for r in (mine_in, send_in, sib_in, st_in, r1_in, r2_in, f_in)]
            return [r.at[k] for r in (mine_out, send_out, sib_out, st_out, r1_out, r2_out, f_out)]

        def piece(j, ref, ch):
            if ITEMS[j % N_ITEMS][0] == "in":
                return ref.at[:, ch * HEAD:(ch + 1) * HEAD]
            return ref.at[ch]

        def slot(ch):
            return jnp.where(c == 0, ch % 2, ch // 2)

        def to_sibling(j):
            _, send, sib, _, _, _, _ = bufs(j)
            return pltpu.make_async_remote_copy(send, sib, d2d_s.at[j], d2d_r.at[j], device_id=p.sibling, device_id_type=MESH)

        def first_round(j, ch):
            _, _, _, st, r1, _, _ = bufs(j)
            k = slot(ch)
            return pltpu.make_async_remote_copy(st.at[k], r1.at[k], r1_s.at[2 * j + k], r1_r.at[2 * j + k],
                                                device_id=p.first, device_id_type=MESH)

        def second_round(j):
            _, _, _, st, _, r2, _ = bufs(j)
            return pltpu.make_async_remote_copy(st.at[2], r2, r2_s.at[j], r2_r.at[j], device_id=p.second, device_id_type=MESH)

        def finals(j):
            f = bufs(j)[6]
            kind, k = ITEMS[j % N_ITEMS]
            if kind == "in":
                dst = fin_in.at[layer_of(j), my_rows, k * HEAD:(k + 1) * HEAD]
            else:
                dst = fin_out.at[layer_of(j), c, :, k * OUT_COLS:(k + 1) * OUT_COLS]
            return [pltpu.make_async_copy(f, dst, fin_l.at[j]),
                    pltpu.make_async_remote_copy(f, dst, fin_s.at[j], fin_r.at[j], device_id=p.sibling, device_id_type=MESH)]

        def stage_sum(j):
            mine, _, sib, st, _, _, _ = bufs(j)
            to_sibling(j).wait_recv()
            mine[...] = mine[...] + sib[...]
            for ch in range(N_CHIP):
                @pl.when(p.first_coord(ch) != p.my_first_coord)
                def _(ch=ch):
                    st[slot(ch)] = piece(j, mine, ch)[...].astype(BF16)
                    first_round(j, ch).start()

        def stage_second(j):
            mine, _, _, st, r1, _, _ = bufs(j)
            for ch in range(N_CHIP):
                @pl.when(p.first_coord(ch) == p.my_first_coord)
                def _(ch=ch):
                    first_round(j, ch).wait_recv()
                    part = piece(j, mine, ch)
                    total = part[...] + r1[slot(ch)].astype(F32)
                    part[...] = total

                    @pl.when(ch != p.chip)
                    def _():
                        st[2] = total.astype(BF16)
                        second_round(j).start()

        def stage_final(j):
            mine, _, _, _, _, r2, f = bufs(j)
            second_round(j).wait_recv()
            for ch in range(N_CHIP):
                @pl.when(ch == p.chip)
                def _(ch=ch):
                    f[...] = piece(j, mine, ch)[...] + r2[...].astype(F32)
            for cp in finals(j):
                cp.start()

        def drain(j):
            to_sibling(j).wait_send()
            for ch in range(N_CHIP):
                @pl.when(p.first_coord(ch) != p.my_first_coord)
                def _(ch=ch):
                    first_round(j, ch).wait_send()

                @pl.when(jnp.logical_and(p.first_coord(ch) == p.my_first_coord, ch != p.chip))
                def _():
                    second_round(j).wait_send()
            for cp in finals(j):
                cp.wait()

        dev = p.dev
        devices = p.others()

        def half(core):
            return pl.ds(pl.multiple_of(core * PK_HALF, 16), PK_HALF)

        def finished(core, ch):
            return pl.ds(pl.multiple_of(core * PK_HALF + ch * PK_PIECE, 16), PK_PIECE)

        def small_exchange(l, first_step, bufs_l):
            (pk_mine, pk_sib, pk_st, pk_rs, pk_fin, pk_all, dm_st, dm_all, pk_sem, rs_s, rs_r, ag_s, ag_r, dm_s, dm_r) = bufs_l

            def pk_load():
                return pltpu.make_async_copy(pack_ref.at[l, half(c)], pk_mine, pk_sem.at[0])

            def pk_give():
                return pltpu.make_async_remote_copy(pack_ref.at[l, half(1 - c)], pk_sib, pk_sem.at[1], pk_sem.at[2],
                                                    device_id=p.sibling, device_id_type=MESH)

            def pk_scatter(ch):
                return pltpu.make_async_remote_copy(pk_st.at[ch * PK_PIECE:(ch + 1) * PK_PIECE], pk_rs.at[p.chip],
                                                    rs_s.at[ch], rs_r.at[p.chip], device_id=(ch // 2, ch % 2, c),
                                                    device_id_type=MESH)

            def pk_spread(ch):
                return pltpu.make_async_remote_copy(pk_fin, pk_all.at[finished(c, p.chip)], ag_s.at[ch], ag_r.at[p.chip],
                                                    device_id=(ch // 2, ch % 2, c), device_id_type=MESH)

            def pk_pass():
                return pltpu.make_async_remote_copy(pk_all.at[half(c)], pk_all.at[half(c)], pk_sem.at[3], pk_sem.at[4],
                                                    device_id=p.sibling, device_id_type=MESH)

            def dm_copy(r):
                return pltpu.make_async_remote_copy(dm_st, dm_all.at[:, pl.ds(dev, 1), :], dm_s.at[r], dm_r.at[r],
                                                    device_id=devices[r], device_id_type=MESH)

            def results():
                return [pltpu.make_async_copy(pk_all, pack_out.at[l], pk_sem.at[0]),
                        pltpu.make_async_copy(dm_all, dmod_out.at[l], pk_sem.at[5])]

            @pl.when(i == first_step)
            def _():
                pk_load().start()
                pk_give().start()
                for k in range(3):
                    for r in range(D_MODEL // HEAD):
                        dm_st[8 * k + r] = dmod_ref[l, k:k + 1, r * HEAD:(r + 1) * HEAD]
                dm_st[DM_LOSS] = dmod_ref[l, 3:4, 0:HEAD]
                dm_all[:, pl.ds(dev, 1), :] = dm_st[...]
                for r in range(N_DEV - 1):
                    dm_copy(r).start()

            @pl.when(i == first_step + SMALL_SCATTER_STEP)
            def _():
                pk_load().wait()
                pk_give().wait()
                total = pk_mine[...] + pk_sib[...]
                pk_mine[...] = total
                pk_st[...] = total.astype(BF16)
                for ch in range(N_CHIP):
                    @pl.when(ch != p.chip)
                    def _(ch=ch):
                        pk_scatter(ch).start()

            @pl.when(i == first_step + SMALL_GATHER_STEP)
            def _():
                for ch in range(N_CHIP):
                    @pl.when(ch != p.chip)
                    def _(ch=ch):
                        pltpu.make_async_remote_copy(pk_fin, pk_rs.at[ch], rs_s.at[ch], rs_r.at[ch],
                                                     device_id=p.sibling, device_id_type=MESH).wait_recv()
                for me in range(N_CHIP):
                    @pl.when(me == p.chip)
                    def _(me=me):
                        total = None
                        for ch in range(N_CHIP):
                            part = pk_mine[me * PK_PIECE:(me + 1) * PK_PIECE] if ch == me else pk_rs[ch].astype(F32)
                            total = part if total is None else total + part
                        pk_fin[...] = total.astype(BF16)
                        pk_all[finished(c, me)] = total.astype(BF16)
                for ch in range(N_CHIP):
                    @pl.when(ch != p.chip)
                    def _(ch=ch):
                        pk_spread(ch).start()

            @pl.when(i == first_step + SMALL_PASS_STEP)
            def _():
                for ch in range(N_CHIP):
                    @pl.when(ch != p.chip)
                    def _(ch=ch):
                        pltpu.make_async_remote_copy(pk_fin, pk_all.at[finished(c, ch)], ag_s.at[ch], ag_r.at[ch],
                                                     device_id=p.sibling, device_id_type=MESH).wait_recv()
                pk_pass().start()

            @pl.when(i == first_step + SMALL_FINISH_STEP)
            def _():
                pk_pass().wait()
                for ch in range(N_CHIP):
                    @pl.when(ch != p.chip)
                    def _(ch=ch):
                        pk_scatter(ch).wait_send()
                        pk_spread(ch).wait_send()
                for r in range(N_DEV - 1):
                    dm_copy(r).wait()
                for cp in results():
                    cp.start()
                for cp in results():
                    cp.wait()

        n_small = 15
        for l in range(DEPTH):
            small_exchange(l, (DEPTH - 1 - l) * N_ITEMS, scratch[23 + n_small * l:23 + n_small * (l + 1)])

        for step in range(N_ITEMS, N_STEPS):
            @pl.when(i == step)
            def _(step=step):
                drain(step - N_ITEMS)

        ii = jnp.where(i < N_ITEMS, i, i - N_ITEMS)
        in_step = jnp.logical_and(ii >= OUT_FIRST, ii < OUT_FIRST + IN_STEPS)

        @pl.when(in_step)
        def _():
            k = ii - OUT_FIRST
            rhs = jnp.concatenate([r[...] for r in dp_refs], axis=1)
            res = _dot(h_ref[...], rhs, TN)

            @pl.when(c == 0)
            def _():
                mine_in[k] = res[:HALF_IN]
                send_in[k] = res[HALF_IN:]

            @pl.when(c == 1)
            def _():
                mine_in[k] = res[HALF_IN:]
                send_in[k] = res[:HALF_IN]

        @pl.when(jnp.logical_not(in_step))
        def _():
            k = jnp.where(ii < OUT_FIRST, ii, ii - IN_STEPS)
            res = _dot(cat_ref[...], dy_ref[...], TN)

            @pl.when(c == 0)
            def _():
                for ch in range(N_CHIP):
                    mine_out[k, ch] = res[ch * W_OUT_ROWS:ch * W_OUT_ROWS + HALF_OUT]
                    send_out[k, ch] = res[ch * W_OUT_ROWS + HALF_OUT:(ch + 1) * W_OUT_ROWS]

            @pl.when(c == 1)
            def _():
                for ch in range(N_CHIP):
                    mine_out[k, ch] = res[ch * W_OUT_ROWS + HALF_OUT:(ch + 1) * W_OUT_ROWS]
                    send_out[k, ch] = res[ch * W_OUT_ROWS:ch * W_OUT_ROWS + HALF_OUT]

        stages = ((0, lambda j: to_sibling(j).start()), (DELAY_SUM, stage_sum), (DELAY_SECOND, stage_second),
                  (DELAY_FINAL, stage_final))
        for step in range(N_STEPS):
            @pl.when(i == step)
            def _(step=step):
                for delay, stage in stages:
                    if step - delay >= 0:
                        stage(step - delay)

        @pl.when(i == N_STEPS - 1)
        def _():
            for step in range(N_STEPS, N_STEPS + DELAY_FINAL):
                for delay, stage in stages:
                    if 0 <= step - delay < N_STEPS:
                        stage(step - delay)
            for j in range(N_STEPS - N_ITEMS, N_STEPS):
                drain(j)

    hbm = pl.BlockSpec(memory_space=pl.ANY)

    def layer(i):
        return jnp.where(i < N_ITEMS, DEPTH - 1, 0)

    def item(i):
        return jnp.where(i < N_ITEMS, i, i - N_ITEMS)

    def whole(i):
        return (layer(i), 0, 0)

    def dproj_piece(ch):
        return pl.BlockSpec((None, SEQ, HEAD),
                            lambda i: (layer(i), 0, ch * IN_STEPS + jnp.clip(item(i) - OUT_FIRST, 0, IN_STEPS - 1)))

    def dy_quarter(i):
        return (layer(i), 0, jnp.where(item(i) < OUT_FIRST, item(i), jnp.maximum(item(i) - IN_STEPS, OUT_FIRST)))

    operand = pl.BlockSpec((None, SEQ, D_MODEL), whole)
    in_specs = [operand] + [dproj_piece(ch) for ch in range(N_CHIP)]
    in_specs += [operand, pl.BlockSpec((None, SEQ, OUT_COLS), dy_quarter), hbm, _const_in((DEPTH, 8, D_MODEL))]
    args = [h, dproj, dproj, dproj, dproj, cat, dy, pack, dmod]
    out_shape = [jax.ShapeDtypeStruct((DEPTH, D_MODEL, W_IN_COLS), F32), jax.ShapeDtypeStruct((DEPTH, 2, HALF_OUT, D_MODEL), F32),
                 jax.ShapeDtypeStruct((DEPTH, PK_ROWS, HEAD), BF16), jax.ShapeDtypeStruct((DEPTH, DM_ROWS, N_DEV, HEAD), F32)]
    out_specs = [hbm, hbm, hbm, hbm]
    in_item = lambda *lead: pltpu.VMEM(lead + (HALF_IN, HEAD), BF16)
    out_item = lambda *lead: pltpu.VMEM(lead + (HALF_OUT, OUT_COLS), BF16)
    scratch = [
        pltpu.VMEM((IN_STEPS, HALF_IN, N_CHIP * HEAD), F32), pltpu.VMEM((IN_STEPS, HALF_IN, N_CHIP * HEAD), F32),
        pltpu.VMEM((IN_STEPS, HALF_IN, N_CHIP * HEAD), F32), in_item(IN_STEPS, 3), in_item(IN_STEPS, 2), in_item(IN_STEPS),
        pltpu.VMEM((IN_STEPS, HALF_IN, HEAD), F32),
        pltpu.VMEM((OUT_STEPS, N_CHIP, HALF_OUT, OUT_COLS), F32), pltpu.VMEM((OUT_STEPS, N_CHIP, HALF_OUT, OUT_COLS), F32),
        pltpu.VMEM((OUT_STEPS, N_CHIP, HALF_OUT, OUT_COLS), F32), out_item(OUT_STEPS, 3), out_item(OUT_STEPS, 2),
        out_item(OUT_STEPS), pltpu.VMEM((OUT_STEPS, HALF_OUT, OUT_COLS), F32),
        pltpu.SemaphoreType.DMA((N_STEPS,)), pltpu.SemaphoreType.DMA((N_STEPS,)),
        pltpu.SemaphoreType.DMA((2 * N_STEPS,)), pltpu.SemaphoreType.DMA((2 * N_STEPS,)),
        pltpu.SemaphoreType.DMA((N_STEPS,)), pltpu.SemaphoreType.DMA((N_STEPS,)),
        pltpu.SemaphoreType.DMA((N_STEPS,)), pltpu.SemaphoreType.DMA((N_STEPS,)), pltpu.SemaphoreType.DMA((N_STEPS,)),
    ]
    for _ in range(DEPTH):
        scratch += [
            pltpu.VMEM((PK_HALF, HEAD), F32), pltpu.VMEM((PK_HALF, HEAD), F32), pltpu.VMEM((PK_HALF, HEAD), BF16),
            pltpu.VMEM((N_CHIP, PK_PIECE, HEAD), BF16), pltpu.VMEM((PK_PIECE, HEAD), BF16), pltpu.VMEM((PK_ROWS, HEAD), BF16),
            pltpu.VMEM((DM_ROWS, 1, HEAD), F32), pltpu.VMEM((DM_ROWS, N_DEV, HEAD), F32),
            pltpu.SemaphoreType.DMA((6,)),
            pltpu.SemaphoreType.DMA((N_CHIP,)), pltpu.SemaphoreType.DMA((N_CHIP,)),
            pltpu.SemaphoreType.DMA((N_CHIP,)), pltpu.SemaphoreType.DMA((N_CHIP,)),
            pltpu.SemaphoreType.DMA((N_DEV - 1,)), pltpu.SemaphoreType.DMA((N_DEV - 1,)),
        ]
    return pl.pallas_call(
        body, name="wgrad", grid=(N_STEPS,), in_specs=in_specs, out_specs=out_specs, out_shape=out_shape,
        scratch_shapes=scratch,
        compiler_params=pltpu.CompilerParams(dimension_semantics=("arbitrary",), vmem_limit_bytes=VMEM_LIMIT),
    )(*args)


def _adamw(w, g, m, v):
    m = ADAM_B1 * m + (1.0 - ADAM_B1) * g
    v = ADAM_B2 * v + (1.0 - ADAM_B2) * (g * g)
    m_hat = m / (1.0 - ADAM_B1 ** ADAM_STEP)
    v_hat = v / (1.0 - ADAM_B2 ** ADAM_STEP)
    delta = -ADAM_LR * (m_hat / (jnp.sqrt(v_hat) + ADAM_EPS) + ADAM_WD * w)
    return delta, m, v


def _adam_sharded(c_all, dmods, ada, w_in_set, w_out_set):
    rows = D_MODEL // ADAM_PARTS

    def body(c_ref, dm_ref, wa_ref, ma_ref, va_ref, wi_ref, gi_ref, mi_ref, vi_ref, wo_ref, go_ref, mo_ref, vo_ref,
             ga_out, da_out, ma_out, va_out, di_out, mi_out, vi_out, do_out, mo_out, vo_out):
        l = pl.program_id(0)
        chip = 2 * lax.axis_index("x") + lax.axis_index("y")
        cv = c_ref[...]
        silu_c = (cv * _sigmoid(cv)).astype(BF16).astype(F32)
        pieces = []
        for k in range(W_ADA_COLS // HEAD):
            dk = dm_ref[l, (W_ADA_COLS // HEAD) * chip + k].astype(BF16).astype(F32)
            pieces.append(_dot_exact(silu_c, dk, TN))
        g = jnp.concatenate(pieces, axis=1)
        ga_out[...] = g
        da_out[...], ma_out[...], va_out[...] = _adamw(wa_ref[...], g, ma_ref[...], va_ref[...])
        di_out[...], mi_out[...], vi_out[...] = _adamw(wi_ref[...], gi_ref[...], mi_ref[...], vi_ref[...])
        do_out[...], mo_out[...], vo_out[...] = _adamw(wo_ref[...], go_ref[...], mo_ref[...], vo_ref[...])

    def blk(r, cols):
        return pl.BlockSpec((None, r, cols), lambda l, i: (l, i, 0))

    b_ada, b_in, b_out = blk(rows, W_ADA_COLS), blk(rows, W_IN_COLS), blk(W_OUT_ROWS // ADAM_PARTS, D_MODEL)
    shapes = [jax.ShapeDtypeStruct(a[0].shape, F32) for a in (ada, w_in_set, w_out_set)]
    return pl.pallas_call(
        body, name="adam_sharded", grid=(DEPTH, ADAM_PARTS),
        in_specs=[pl.BlockSpec((N_DEV, rows), lambda l, i: (0, i)), _const_in((DEPTH, DM_ROWS, N_DEV, HEAD))]
        + [b_ada] * 3 + [b_in] * 4 + [b_out] * 4,
        out_specs=[b_ada] * 4 + [b_in] * 3 + [b_out] * 3,
        out_shape=[shapes[0]] * 4 + [shapes[1]] * 3 + [shapes[2]] * 3,
        compiler_params=pltpu.CompilerParams(dimension_semantics=("arbitrary", "arbitrary"), vmem_limit_bytes=VMEM_LIMIT),
    )(c_all, dmods, *ada, *w_in_set, *w_out_set)


def _adam_small(packs, dmods, weights, ms, vs):
    n = len(weights)

    def body(*refs):
        dm_refs = refs[1]
        b = 2
        w_refs, m_refs, v_refs = refs[b:b + n], refs[b + n:b + 2 * n], refs[b + 2 * n:b + 3 * n]
        outs = refs[b + 3 * n:b + 3 * n + 4 * n + 1]
        pack_refs = refs[-1]
        pack_refs[...] = refs[0][...].astype(F32)
        g_refs, d_refs, nm_refs, nv_refs = outs[0:n], outs[n:2 * n], outs[2 * n:3 * n], outs[3 * n:4 * n]
        squares = dm_refs[DEPTH - 1, DM_LOSS]
        total = squares[0:1, 0:1]
        for d in range(1, N_DEV):
            total = total + squares[d:d + 1, 0:1]
        outs[4 * n][...] = total * (0.5 / D_MODEL)

        def lanes(l, row0, count):
            return jnp.concatenate([pack_refs.at[l][row0 + k:row0 + k + 1, :] for k in range(count)], axis=1)

        def update(idx, at, g):
            g_refs[idx][at] = g
            d_refs[idx][at], nm_refs[idx][at], nv_refs[idx][at] = _adamw(w_refs[idx][at], g, m_refs[idx][at], v_refs[idx][at])

        for l in range(DEPTH):
            row = (slice(l, l + 1), slice(None))
            g_b = None
            for d in range(N_DEV):
                part = dm_refs.at[l][0:DM_LOSS, d, :]
                g_b = part if g_b is None else g_b + part
            update(0, row, jnp.concatenate([g_b[k:k + 1, :] for k in range(DM_LOSS)], axis=1))
            for g in range(N_HEAD):
                update(1, (l, g), pack_refs.at[l][PK_W_POOL + g * HEAD:PK_W_POOL + (g + 1) * HEAD, :])
                update(5, (l, g), pack_refs.at[l][PK_W_SGU + g * HEAD:PK_W_SGU + (g + 1) * HEAD, :])
            update(2, row, lanes(l, PK_POOL_SCALE, N_HEAD))
            update(3, (l,), pack_refs.at[l][PK_SGU_LN_G:PK_SGU_LN_G + N_HEAD, :])
            update(4, (l,), pack_refs.at[l][PK_SGU_LN_B:PK_SGU_LN_B + N_HEAD, :])
            update(6, (l,), pack_refs.at[l][PK_B_SGU:PK_B_SGU + N_HEAD, :])
            update(7, row, lanes(l, PK_LN_G, D_MODEL // HEAD))
            update(8, row, lanes(l, PK_LN_B, D_MODEL // HEAD))

    vmem = pl.BlockSpec(memory_space=pltpu.VMEM)
    shapes = [jax.ShapeDtypeStruct(w.shape, F32) for w in weights]
    return pl.pallas_call(
        body, name="adam_small", in_specs=[vmem] * (2 + 3 * n), out_specs=[vmem] * (4 * n + 1),
        out_shape=shapes * 4 + [jax.ShapeDtypeStruct((1, 1), F32)],
        scratch_shapes=[pltpu.VMEM(packs.shape, F32)],
        compiler_params=pltpu.CompilerParams(vmem_limit_bytes=VMEM_LIMIT),
    )(packs, dmods, *weights, *ms, *vs)


def kernel(x, c, w_ada, b_ada, w_in, w_pool, pool_scale, sgu_ln_g, sgu_ln_b, w_sgu, b_sgu, w_out, ln_g, ln_b, loss_target, m_w_ada, m_b_ada, m_w_in, m_w_pool, m_pool_scale, m_sgu_ln_g, m_sgu_ln_b, m_w_sgu, m_b_sgu, m_w_out, m_ln_g, m_ln_b, v_w_ada, v_b_ada, v_w_in, v_w_pool, v_pool_scale, v_sgu_ln_g, v_sgu_ln_b, v_w_sgu, v_b_sgu, v_w_out, v_ln_g, v_ln_b):
    small = (w_pool, pool_scale, sgu_ln_g, sgu_ln_b, w_sgu, b_sgu)
    *saved0, w_in0, w_out0, w_in1, w_out1, mod, c_all = _forward_first(x, c, w_ada, b_ada, w_in, w_out, small, ln_g, ln_b)
    *saved1, dout, sq = _forward_last(saved0[3], mod, w_in1, w_out1, small, ln_g, ln_b, loss_target)

    dx1, *shared = _backward_layer(1, dout, saved1, mod, w_in1, w_out1, small, ln_g, sq=sq)
    dx0, h, cat, dy, dproj, pack, dmod = _backward_layer(0, dx1, saved0, mod, w_in0, w_out0, small, ln_g, shared=shared)
    g_in, g_out, pack, dmods = _wgrad_reduce(h, dproj, cat, dy, pack, dmod)

    g_out = g_out.reshape(DEPTH, W_OUT_ROWS, D_MODEL)
    big = _adam_sharded(c_all, dmods, (w_ada, m_w_ada, v_w_ada), (w_in, g_in, m_w_in, v_w_in), (w_out, g_out, m_w_out, v_w_out))
    ada, win, wout = big[0:4], (g_in, *big[4:7]), (g_out, *big[7:10])
    small_w = (b_ada, w_pool, pool_scale, sgu_ln_g, sgu_ln_b, w_sgu, b_sgu, ln_g, ln_b)
    small_m = (m_b_ada, m_w_pool, m_pool_scale, m_sgu_ln_g, m_sgu_ln_b, m_w_sgu, m_b_sgu, m_ln_g, m_ln_b)
    small_v = (v_b_ada, v_w_pool, v_pool_scale, v_sgu_ln_g, v_sgu_ln_b, v_w_sgu, v_b_sgu, v_ln_g, v_ln_b)
    res = _adam_small(pack, dmods, small_w, small_m, small_v)
    n = len(small_w)
    loss = res[4 * n].reshape(())

    def ordered(k):
        s = res[k * n:(k + 1) * n]
        return (ada[k], s[0], win[k], s[1], s[2], s[3], s[4], s[5], s[6], wout[k], s[7], s[8])

    return (loss, dx0[None], *ordered(0), *ordered(1), *ordered(2), *ordered(3))
```

```python
import jax
import jax.numpy as jnp
from jax import lax
from jax.experimental import pallas as pl
from jax.experimental.pallas import tpu as pltpu

F32 = jnp.float32
BF16 = jnp.bfloat16
MESH = pl.DeviceIdType.MESH

N_DEV = 8
N_CHIP = 4
DEPTH = 2
SEQ = 2048
D_MODEL = 1024
D_POOL = 512
D_PROJ = 2560
HEAD = 128
N_HEAD = 4
ROWS = 256
N_TILE = SEQ // ROWS
HALO = 16
W_IN_COLS = D_PROJ // N_CHIP
W_OUT_ROWS = D_MODEL // N_CHIP
W_ADA_COLS = 3 * D_MODEL // N_CHIP
HALF_IN = D_MODEL // 2
HALF_OUT = W_OUT_ROWS // 2
DEEPNORM_ALPHA = (2.0 * DEPTH) ** 0.25
LN_EPS = 1e-5
INV_SQRT2 = 0.7071067811865476
INV_SQRT_2PI = 0.3989422804014327

ADAM_LR = 0.001
ADAM_B1 = 0.9
ADAM_B2 = 0.999
ADAM_EPS = 1e-08
ADAM_WD = 0.01
ADAM_STEP = 10
ADAM_PARTS = 2

PK_W_POOL = 0
PK_W_SGU = 512
PK_POOL_SCALE = 1024
PK_SGU_LN_G = 1032
PK_SGU_LN_B = 1040
PK_B_SGU = 1048
PK_LN_G = 1056
PK_LN_B = 1064
PK_ROWS = 1152
PK_HALF = PK_ROWS // 2
PK_PIECE = PK_HALF // N_CHIP
DM_LOSS = 3 * D_MODEL // HEAD
DM_ROWS = DM_LOSS + 1

VMEM_LIMIT = 56 * 1024 * 1024

GATHER_SECOND_ROUND_STEP = 0
GATHER_PASS_STEP = N_TILE - 3

NN = (((1,), (0,)), ((), ()))
NT = (((1,), (1,)), ((), ()))
TN = (((0,), (0,)), ((), ()))


def _dot(a, b, dims=NN):
    return lax.dot_general(a, b, dims, preferred_element_type=F32)


def _dot_exact(a, b, dims=NN):
    return lax.dot_general(a, b, dims, preferred_element_type=F32, precision=lax.Precision.HIGHEST)


def _layer_norm(v):
    mu = jnp.mean(v, axis=-1, keepdims=True)
    d = v - mu
    var = jnp.mean(d * d, axis=-1, keepdims=True)
    rstd = lax.rsqrt(var + LN_EPS)
    return d * rstd, rstd


def _layer_norm_bwd(dvhat, vhat, rstd):
    m1 = jnp.mean(dvhat, axis=-1, keepdims=True)
    m2 = jnp.mean(dvhat * vhat, axis=-1, keepdims=True)
    return rstd * (dvhat - m1 - vhat * m2)


def _sigmoid(v):
    return 1.0 / (1.0 + jnp.exp(-v))


def _gelu_parts(v):
    phi = 0.5 * (1.0 + lax.erf(v * INV_SQRT2))
    pdf = INV_SQRT_2PI * jnp.exp(-0.5 * v * v)
    return phi, pdf


def _sum_rows(v):
    return jnp.sum(v, axis=0, keepdims=True)


def _window_sums(ext, toward_later):
    n = ext.shape[0]

    def shifted(v, k):
        return pltpu.roll(v, (n - k) if toward_later else k, 0)

    s2 = ext + shifted(ext, 1)
    r4 = s2[:, HEAD:]
    s4 = r4 + shifted(r4, 2)
    r8 = s4[:, HEAD:]
    s8 = r8 + shifted(r8, 4)
    r16 = s8[:, HEAD:]
    s16 = r16 + shifted(r16, 8)
    return jnp.concatenate([s2[:, :HEAD], s4[:, :HEAD], s8[:, :HEAD], s16], axis=1)


def _window_counts(row0):
    t1 = row0 + 1 + lax.broadcasted_iota(jnp.int32, (ROWS, D_POOL), 0)
    lane = lax.broadcasted_iota(jnp.int32, (ROWS, D_POOL), 1)
    width = jnp.where(lane < HEAD, 2, jnp.where(lane < 2 * HEAD, 4, jnp.where(lane < 3 * HEAD, 8, 16)))
    return jnp.minimum(t1, width).astype(F32)


def _causal_mask():
    r = lax.broadcasted_iota(jnp.int32, (HEAD, HEAD), 0)
    s = lax.broadcasted_iota(jnp.int32, (HEAD, HEAD), 1)
    return r >= s


def _chunks_to_lanes(v):
    return jnp.concatenate([v[n * HEAD:(n + 1) * HEAD] for n in range(ROWS // HEAD)], axis=1)


def _lanes_to_chunks(v):
    return jnp.concatenate([v[:, n * HEAD:(n + 1) * HEAD] for n in range(ROWS // HEAD)], axis=0)


def _pack_stats(rstd_x, rstd_z, rstd_v):
    lane = lax.broadcasted_iota(jnp.int32, (ROWS, HEAD), 1)
    packed = rstd_x
    for k, r in enumerate([rstd_z] + list(rstd_v)):
        packed = jnp.where(lane < 16 * (k + 1), packed, r)
    return packed


def _unpack_stats(stats):
    cols = [stats[:, 16 * k:16 * k + 1] for k in range(2 + N_HEAD)]
    return cols[0], cols[1], cols[2:]


def _mixer(proj, halo, row0, wpool_ref, pscale, sgu_g_ref, sgu_b_ref, wsgu_ref, bsgu_ref, saved=None):
    xa = proj[:, 0:512]
    ga = proj[:, 512:1024]
    u = proj[:, 1024:1536]
    v = proj[:, 1536:2048]
    gb = proj[:, 2048:2560]
    ext = jnp.concatenate([halo, xa], axis=0)
    win = _window_sums(ext, toward_later=False)[HALO:]
    cnt = _window_counts(row0)
    pooled = (win / cnt - xa).astype(BF16)
    pw = jnp.concatenate(
        [_dot(pooled[:, g * HEAD:(g + 1) * HEAD], wpool_ref[g].astype(BF16)) for g in range(N_HEAD)], axis=1)
    sig_a = _sigmoid(ga) if saved is None else saved["sig_a"]
    ya = pw * pscale * (ga * sig_a)
    phi_u, pdf_u = _gelu_parts(u)
    phi_v, pdf_v = _gelu_parts(v)
    gu = u * phi_u
    gv = v * phi_v
    sig_b = _sigmoid(gb) if saved is None else saved["sig_b"]
    silu_b = gb * sig_b
    mask = _causal_mask()
    diag = lax.broadcasted_iota(jnp.int32, (HEAD, HEAD), 0) == lax.broadcasted_iota(jnp.int32, (HEAD, HEAD), 1)
    vhat, rstd_v, vln_l, mixed = [], [], [], []
    for h in range(N_HEAD):
        if saved is None:
            vh, rh = _layer_norm(gv[:, h * HEAD:(h + 1) * HEAD])
        else:
            vh, rh = saved["vhat"][h], saved["rstd_v"][h]
        ln = (vh * sgu_g_ref[h:h + 1, :] + sgu_b_ref[h:h + 1, :]).astype(BF16)
        ln_l = _chunks_to_lanes(ln)
        wm = jnp.where(mask, wsgu_ref[h], 0.0).astype(BF16)
        bias = jnp.sum(jnp.where(diag, jnp.broadcast_to(bsgu_ref[h:h + 1, :], (HEAD, HEAD)), 0.0), axis=1, keepdims=True)
        mx = _lanes_to_chunks(_dot(wm, ln_l) + bias)
        vhat.append(vh)
        rstd_v.append(rh)
        vln_l.append(ln_l)
        mixed.append(mx)
    mixed = jnp.concatenate(mixed, axis=1)
    yb = gu * mixed * silu_b
    return dict(xa=xa, ga=ga, u=u, v=v, gb=gb, cnt=cnt, pooled=pooled, pw=pw, sig_a=sig_a, ya=ya, phi_u=phi_u, pdf_u=pdf_u,
                phi_v=phi_v, pdf_v=pdf_v, gu=gu, sig_b=sig_b, silu_b=silu_b, vhat=vhat, rstd_v=rstd_v, vln_l=vln_l,
                mixed=mixed, yb=yb, mask=mask)


def _const(shape, *index):
    lead = tuple(index) + (0,) * (len(shape) - len(index))
    return pl.BlockSpec(shape, lambda *_: lead)


def _const_in(shape, *index):
    lead = tuple(index) + (0,) * (len(shape) - len(index))
    return pl.BlockSpec(shape, lambda *_: lead, pipeline_mode=pl.Buffered(1))


def _layer_weight_specs(l):
    return [
        _const_in((None, N_HEAD, HEAD, HEAD), l),
        _const_in((DEPTH, D_POOL)),
        _const_in((None, N_HEAD, HEAD), l),
        _const_in((None, N_HEAD, HEAD), l),
        _const_in((None, N_HEAD, HEAD, HEAD), l),
        _const_in((None, N_HEAD, HEAD), l),
    ]


def _forward_tile(l, i, x_ref, mod_ref, win_ref, wout_ref, small_refs, lng_ref, lnb_ref, carry_ref, saved_refs):
    wpool_ref, pscale_ref, sgu_g_ref, sgu_b_ref, wsgu_ref, bsgu_ref = small_refs
    proj_ref, y_ref, xn_ref, zn_ref, stats_ref, sig_ref, vhat_ref = saved_refs
    x = x_ref[...]
    if l > 0:
        x = x * lng_ref[l - 1:l, :] + lnb_ref[l - 1:l, :]
    shift, scale, gate = mod_ref[0:1, :], mod_ref[1:2, :], mod_ref[2:3, :]
    xn, rstd_x = _layer_norm(x)
    xn_ref[...] = xn.astype(xn_ref.dtype)
    h = xn * (1.0 + scale) + shift
    proj = _dot(h.astype(BF16), win_ref[...])
    proj_ref[...] = proj.astype(proj_ref.dtype)
    m = _mixer(proj, carry_ref[...], i * ROWS, wpool_ref, pscale_ref[l:l + 1, :], sgu_g_ref, sgu_b_ref, wsgu_ref, bsgu_ref)
    carry_ref[...] = m["xa"][ROWS - HALO:]
    sig_ref[...] = jnp.concatenate([m["sig_a"], m["sig_b"]], axis=1).astype(sig_ref.dtype)
    vhat_ref[...] = jnp.concatenate(m["vhat"], axis=1).astype(vhat_ref.dtype)
    cat = jnp.concatenate([m["ya"], m["yb"]], axis=1).astype(BF16)
    y = _dot(cat, wout_ref[...])
    y_ref[...] = y.astype(y_ref.dtype)
    zn, rstd_z = _layer_norm(DEEPNORM_ALPHA * x + gate * y)
    zn_ref[...] = zn
    stats_ref[...] = _pack_stats(rstd_x, rstd_z, m["rstd_v"])
    return zn


SAVED_COLS = (D_PROJ, D_MODEL, D_MODEL, D_MODEL, HEAD, D_MODEL, D_POOL)
SAVED_TYPES = (BF16, BF16, BF16, F32, F32, BF16, BF16)


def _saved_outputs():
    return ([jax.ShapeDtypeStruct((SEQ, cols), t) for cols, t in zip(SAVED_COLS, SAVED_TYPES)],
            [pl.BlockSpec((ROWS, cols), lambda i: (i, 0)) for cols in SAVED_COLS])


def _forward_last(zn_prev, mod, w_in, w_out, small, ln_g, ln_b, target):
    l = DEPTH - 1
    n_saved = len(SAVED_COLS)

    def body(*refs):
        x_ref, mod_ref, win_ref, wout_ref = refs[:4]
        small_refs, lng_ref, lnb_ref, tgt_ref = refs[4:10], refs[10], refs[11], refs[12]
        saved_refs = refs[13:13 + n_saved]
        dout_ref, loss_ref, carry_ref = refs[13 + n_saved:]
        i = pl.program_id(0)

        @pl.when(i == 0)
        def _():
            carry_ref[...] = jnp.zeros_like(carry_ref)
            loss_ref[...] = jnp.zeros_like(loss_ref)

        zn = _forward_tile(l, i, x_ref, mod_ref, win_ref, wout_ref, small_refs, lng_ref, lnb_ref, carry_ref, saved_refs)
        err = zn * lng_ref[l:l + 1, :] + lnb_ref[l:l + 1, :] - tgt_ref[...]
        dout_ref[...] = err * (1.0 / D_MODEL)
        loss_ref[...] += jnp.sum(err * err)

    tile = pl.BlockSpec((ROWS, D_MODEL), lambda i: (i, 0))
    tile3 = pl.BlockSpec((None, ROWS, D_MODEL), lambda i: (0, i, 0))
    in_specs = [tile, _const_in((None, 8, D_MODEL), l), _const_in((D_MODEL, D_PROJ)), _const_in((D_MODEL, D_MODEL))]
    in_specs += _layer_weight_specs(l) + [_const_in((DEPTH, D_MODEL)), _const_in((DEPTH, D_MODEL)), tile3]
    out_shape, out_specs = _saved_outputs()
    out_shape += [jax.ShapeDtypeStruct((SEQ, D_MODEL), F32), jax.ShapeDtypeStruct((8, HEAD), F32)]
    out_specs += [tile, _const((8, HEAD))]
    return pl.pallas_call(
        body, name="fwd_last", grid=(N_TILE,), in_specs=in_specs, out_specs=out_specs, out_shape=out_shape,
        scratch_shapes=[pltpu.VMEM((HALO, D_POOL), F32)],
        compiler_params=pltpu.CompilerParams(dimension_semantics=("arbitrary",), vmem_limit_bytes=VMEM_LIMIT),
    )(zn_prev, mod, w_in, w_out, *small, ln_g, ln_b, target)


def _backward_layer(l, dout, saved, mod, w_in, w_out, small, ln_g, sq=None, shared=None):
    has_loss = sq is not None

    def body(*refs):
        (dout_ref, proj_ref, y_ref, xn_ref, zn_ref, stats_ref, sig_ref, vhat_ref, halo_ref, mod_ref, win_ref, wout_ref,
         wpool_ref, pscale_ref, sgu_g_ref, sgu_b_ref, wsgu_ref, bsgu_ref, lng_ref) = refs[:19]
        n_in = 20 if has_loss else 19 + 6
        dx_ref, h_ref, cat_ref, dy_ref, dproj_ref, pack_ref, dmod_ref, carry_ref = refs[n_in:n_in + 8]
        i = pl.program_id(0)
        tile = N_TILE - 1 - i

        @pl.when(i == 0)
        def _():
            carry_ref[...] = jnp.zeros_like(carry_ref)
            pack_ref[...] = jnp.zeros_like(pack_ref)
            dmod_ref[...] = jnp.zeros_like(dmod_ref)
            if has_loss:
                dmod_ref[3:4, 0:HEAD] = refs[19][0:1, :]

        xn = xn_ref[...].astype(F32)
        zn = zn_ref[...]
        y = y_ref[...].astype(F32)
        dout = dout_ref[...]
        rstd_x, rstd_z, rstd_v = _unpack_stats(stats_ref[...])
        kept = dict(sig_a=sig_ref[:, :D_POOL].astype(F32), sig_b=sig_ref[:, D_POOL:].astype(F32), rstd_v=rstd_v,
                    vhat=[vhat_ref[:, hd * HEAD:(hd + 1) * HEAD].astype(F32) for hd in range(N_HEAD)])
        pscale = pscale_ref[l:l + 1, :]
        shift, scale, gate = mod_ref[0:1, :], mod_ref[1:2, :], mod_ref[2:3, :]
        h = xn * (1.0 + scale) + shift
        h_ref[...] = h.astype(BF16)
        g_ln_g = _sum_rows(dout * zn)
        g_ln_b = _sum_rows(dout)
        dz = _layer_norm_bwd(dout * lng_ref[l:l + 1, :], zn, rstd_z)
        d_gate = _sum_rows(dz * y)
        dy = (gate * dz).astype(BF16)
        dy_ref[...] = dy

        halo = jnp.where(tile > 0, halo_ref[...].astype(F32), 0.0)
        m = _mixer(proj_ref[...].astype(F32), halo, tile * ROWS, wpool_ref, pscale, sgu_g_ref, sgu_b_ref, wsgu_ref, bsgu_ref,
                   saved=kept)
        cat_ref[...] = jnp.concatenate([m["ya"], m["yb"]], axis=1).astype(BF16)
        dcat = _dot(dy, wout_ref[...], NT)
        dya = dcat[:, :D_POOL]
        dyb = dcat[:, D_POOL:]

        ga, sig_a = m["ga"], m["sig_a"]
        dp = dya * (ga * sig_a)
        d_ga = dya * (m["pw"] * pscale) * (sig_a * (1.0 + ga * (1.0 - sig_a)))
        g_pscale = _sum_rows(dp * m["pw"])
        dpw = (dp * pscale).astype(BF16)
        dpooled = []
        for g in range(N_HEAD):
            cols = slice(g * HEAD, (g + 1) * HEAD)
            pack_ref[PK_W_POOL + g * HEAD:PK_W_POOL + (g + 1) * HEAD, :] += _dot(m["pooled"][:, cols], dpw[:, cols], TN)
            dpooled.append(_dot(dpw[:, cols], wpool_ref[g].astype(BF16), NT))
        dpooled = jnp.concatenate(dpooled, axis=1)
        q = dpooled / m["cnt"]
        ext = jnp.concatenate([q, carry_ref[...]], axis=0)
        d_xa = _window_sums(ext, toward_later=True)[:ROWS] - dpooled
        carry_ref[...] = q[:HALO]

        gu, mixed, silu_b, gb, sig_b = m["gu"], m["mixed"], m["silu_b"], m["gb"], m["sig_b"]
        d_mixed = dyb * gu * silu_b
        d_gu = dyb * mixed * silu_b
        d_gb = dyb * gu * mixed * (sig_b * (1.0 + gb * (1.0 - sig_b)))
        d_u = d_gu * (m["phi_u"] + m["u"] * m["pdf_u"])
        ones = jnp.ones((8, HEAD), F32)
        d_v = []
        for hd in range(N_HEAD):
            cols = slice(hd * HEAD, (hd + 1) * HEAD)
            dm = d_mixed[:, cols]
            dm_l = _chunks_to_lanes(dm.astype(BF16))
            g_w = _dot(dm_l, m["vln_l"][hd], NT)
            pack_ref[PK_W_SGU + hd * HEAD:PK_W_SGU + (hd + 1) * HEAD, :] += jnp.where(m["mask"], g_w, 0.0)
            dm_sum = dm[0:HEAD]
            for n in range(1, ROWS // HEAD):
                dm_sum = dm_sum + dm[n * HEAD:(n + 1) * HEAD]
            pack_ref[PK_B_SGU + hd:PK_B_SGU + hd + 1, :] += _dot_exact(ones, dm_sum, NT)[0:1]
            wm = jnp.where(m["mask"], wsgu_ref[hd], 0.0).astype(BF16)
            d_vln = _lanes_to_chunks(_dot(wm, dm_l, TN))
            vhat = m["vhat"][hd]
            pack_ref[PK_SGU_LN_G + hd:PK_SGU_LN_G + hd + 1, :] += _sum_rows(d_vln * vhat)
            pack_ref[PK_SGU_LN_B + hd:PK_SGU_LN_B + hd + 1, :] += _sum_rows(d_vln)
            d_v.append(_layer_norm_bwd(d_vln * sgu_g_ref[hd:hd + 1, :], vhat, m["rstd_v"][hd]))
        v = m["v"]
        d_v = jnp.concatenate(d_v, axis=1) * (m["phi_v"] + v * m["pdf_v"])

        dproj = jnp.concatenate([d_xa, d_ga, d_u, d_v, d_gb], axis=1).astype(BF16)
        dproj_ref[...] = dproj
        dh = _dot(dproj, win_ref[...], NT)
        d_scale = _sum_rows(dh * xn)
        d_shift = _sum_rows(dh)
        dx_ref[...] = DEEPNORM_ALPHA * dz + _layer_norm_bwd(dh * (1.0 + scale), xn, rstd_x)

        dmod_ref[0:1, :] += d_shift
        dmod_ref[1:2, :] += d_scale
        dmod_ref[2:3, :] += d_gate
        for g in range(N_HEAD):
            pack_ref[PK_POOL_SCALE + g:PK_POOL_SCALE + g + 1, :] += g_pscale[:, g * HEAD:(g + 1) * HEAD]
        for k in range(D_MODEL // HEAD):
            pack_ref[PK_LN_G + k:PK_LN_G + k + 1, :] += g_ln_g[:, k * HEAD:(k + 1) * HEAD]
            pack_ref[PK_LN_B + k:PK_LN_B + k + 1, :] += g_ln_b[:, k * HEAD:(k + 1) * HEAD]

    def rev(i):
        return (N_TILE - 1 - i, 0)

    tile = pl.BlockSpec((ROWS, D_MODEL), rev)
    halo = pl.BlockSpec((HALO, D_POOL), lambda i: (jnp.maximum((N_TILE - 1 - i) * (ROWS // HALO) - 1, 0), 0))
    in_specs = [tile] + [pl.BlockSpec((ROWS, a.shape[1]), rev) for a in saved] + [halo]
    in_specs += [_const_in((None, 8, D_MODEL), l), _const_in((D_MODEL, D_PROJ)), _const_in((D_MODEL, D_MODEL))]
    in_specs += _layer_weight_specs(l) + [_const_in((DEPTH, D_MODEL))]
    args = [dout, *saved, saved[0], mod, w_in, w_out, *small, ln_g]
    stacked = lambda cols: pl.BlockSpec((None, ROWS, cols), lambda i: (l, N_TILE - 1 - i, 0))
    out_shape = [jax.ShapeDtypeStruct((SEQ, D_MODEL), F32), jax.ShapeDtypeStruct((DEPTH, SEQ, D_MODEL), BF16),
                 jax.ShapeDtypeStruct((DEPTH, SEQ, D_MODEL), BF16), jax.ShapeDtypeStruct((DEPTH, SEQ, D_MODEL), BF16),
                 jax.ShapeDtypeStruct((DEPTH, SEQ, D_PROJ), BF16), jax.ShapeDtypeStruct((DEPTH, PK_ROWS, HEAD), F32),
                 jax.ShapeDtypeStruct((DEPTH, 8, D_MODEL), F32)]
    out_specs = [tile, stacked(D_MODEL), stacked(D_MODEL), stacked(D_MODEL), stacked(D_PROJ),
                 _const((None, PK_ROWS, HEAD), l), _const((None, 8, D_MODEL), l)]
    aliases = {}
    if has_loss:
        in_specs.append(_const_in((8, HEAD)))
        args.append(sq)
    else:
        aliases = {len(args) + k: 1 + k for k in range(len(shared))}
        in_specs += [pl.BlockSpec(memory_space=pl.ANY)] * len(shared)
        args += list(shared)
    return pl.pallas_call(
        body, name="bwd_last" if has_loss else "bwd_first", grid=(N_TILE,), in_specs=in_specs, out_specs=out_specs,
        out_shape=out_shape, scratch_shapes=[pltpu.VMEM((HALO, D_POOL), F32)], input_output_aliases=aliases,
        compiler_params=pltpu.CompilerParams(dimension_semantics=("arbitrary",), vmem_limit_bytes=VMEM_LIMIT),
    )(*args)


def _flip(v, f):
    return v + f - 2 * v * f


class _Place:
    def __init__(self):
        x, y, c = lax.axis_index("x"), lax.axis_index("y"), lax.axis_index("c")
        self.x, self.y, self.c = x, y, c
        self.chip = 2 * x + y
        self.dev = 4 * x + 2 * y + c
        self.sibling = (x, y, 1 - c)
        x1, y1 = _flip(x, 1 - c), _flip(y, c)
        x2, y2 = _flip(x, c), _flip(y, 1 - c)
        self.first = (x1, y1, c)
        self.second = (x2, y2, c)
        self.chip_first = 2 * x1 + y1
        self.chip_second = 2 * x2 + y2
        self.chip_far = 2 * (1 - x) + (1 - y)
        self.my_first_coord = jnp.where(c == 0, x, y)

    def first_coord(self, ch):
        return jnp.where(self.c == 0, ch // 2, ch % 2)

    def others(self):
        return [(_flip(self.x, (r >> 2) & 1), _flip(self.y, (r >> 1) & 1), _flip(self.c, r & 1)) for r in range(1, N_DEV)]

    def other_chips(self):
        return [(1 - self.x, self.y), (self.x, 1 - self.y), (1 - self.x, 1 - self.y)]


class _WeightGather:
    CHUNKS = 2
    N_SEMS = 12 * CHUNKS

    def __init__(self, place, win, wout, send, recv):
        self.p, self.win, self.wout, self.send, self.recv = place, win, wout, send, recv
        p = place
        self.plan = [(p.chip, p.first), (p.chip, p.second), (p.chip_first, p.second),
                     (p.chip_first, p.sibling), (p.chip_second, p.sibling), (p.chip_far, p.sibling)]

    def _copies(self, k, q):
        ch, target = self.plan[k]
        n_in, n_out = HALF_IN // self.CHUNKS, HALF_OUT // self.CHUNKS
        rows_in = pl.ds(pl.multiple_of(self.p.c * HALF_IN + q * n_in, n_in), n_in)
        cols_in = pl.ds(pl.multiple_of(ch * W_IN_COLS, 128), W_IN_COLS)
        rows_out = pl.ds(pl.multiple_of(ch * W_OUT_ROWS + self.p.c * HALF_OUT + q * n_out, n_out), n_out)
        r_in = self.win.at[rows_in, cols_in]
        r_out = self.wout.at[rows_out, :]
        s = 2 * (6 * q + k)
        return [pltpu.make_async_remote_copy(r_in, r_in, self.send.at[s], self.recv.at[s],
                                             device_id=target, device_id_type=MESH),
                pltpu.make_async_remote_copy(r_out, r_out, self.send.at[s + 1], self.recv.at[s + 1],
                                             device_id=target, device_id_type=MESH)]

    def _start(self, k, q):
        for cp in self._copies(k, q):
            cp.start()

    def _landed(self, k, q):
        for cp in self._copies(k, q):
            cp.wait_recv()

    def start_first_round(self):
        for q in range(self.CHUNKS):
            self._start(0, q)
        for q in range(self.CHUNKS):
            self._start(1, q)

    def start_second_round(self, q):
        self._landed(0, q)
        self._start(2, q)
        self._start(3, q)

    def pass_second_round(self, q):
        self._landed(1, q)
        self._start(4, q)
        self._landed(2, q)
        self._start(5, q)

    def finish(self):
        for q in range(self.CHUNKS):
            for k in (3, 4, 5):
                self._landed(k, q)
        for q in range(self.CHUNKS):
            for k in range(len(self.plan)):
                for cp in self._copies(k, q):
                    cp.wait_send()


def _forward_first(x, c_vec, w_ada, b_ada, w_in, w_out, small, ln_g, ln_b):
    n_saved = len(SAVED_COLS)

    def body(*refs):
        x_ref, c_ref, wada_hbm, bada_ref, win_hbm, wout_hbm = refs[:6]
        small_refs, lng_ref, lnb_ref = refs[6:12], refs[12], refs[13]
        saved_refs = refs[14:14 + n_saved]
        win0, wout0, win1, wout1, mod_out, c_out = refs[14 + n_saved:20 + n_saved]
        (carry_ref, wada_ref, win_ref, wout_ref, win_bf, wout_bf, mod_mine, mod_all, c_all, mod_ref, win_v, wout_v,
         g0_send, g0_recv, g1_send, g1_recv, c_send, c_recv, mod_send, mod_recv, local_sem) = refs[20 + n_saved:]
        i = pl.program_id(0)
        p = _Place()
        gather0 = _WeightGather(p, win_v, wout_v, g0_send, g0_recv)
        keep = [pltpu.make_async_copy(win_v, win0, local_sem.at[9]), pltpu.make_async_copy(wout_v, wout0, local_sem.at[10])]
        gather1 = _WeightGather(p, win1, wout1, g1_send, g1_recv)

        @pl.when(i == 0)
        def _():
            carry_ref[...] = jnp.zeros_like(carry_ref)
            loads = [pltpu.make_async_copy(win_hbm.at[0], win_ref.at[0], local_sem.at[4]),
                     pltpu.make_async_copy(wout_hbm.at[0], wout_ref.at[0], local_sem.at[5]),
                     pltpu.make_async_copy(win_hbm.at[1], win_ref.at[1], local_sem.at[6]),
                     pltpu.make_async_copy(wout_hbm.at[1], wout_ref.at[1], local_sem.at[7]),
                     pltpu.make_async_copy(wada_hbm, wada_ref, local_sem.at[8])]
            for cp in loads:
                cp.start()

            c_all[pl.ds(p.dev, 1), :] = c_ref[...]
            mine = c_all.at[pl.ds(p.dev, 1), :]
            c_copies = [pltpu.make_async_remote_copy(mine, mine, c_send.at[r], c_recv.at[r], device_id=d, device_id_type=MESH)
                        for r, d in enumerate(p.others())]
            for cp in c_copies:
                cp.start()

            cols = pl.ds(pl.multiple_of(p.chip * W_IN_COLS, 128), W_IN_COLS)
            rows = pl.ds(pl.multiple_of(p.chip * W_OUT_ROWS, W_OUT_ROWS), W_OUT_ROWS)
            own = [pltpu.make_async_copy(win_bf.at[0], win_v.at[:, cols], local_sem.at[0]),
                   pltpu.make_async_copy(wout_bf.at[0], wout_v.at[rows, :], local_sem.at[1]),
                   pltpu.make_async_copy(win_bf.at[1], win1.at[:, cols], local_sem.at[2]),
                   pltpu.make_async_copy(wout_bf.at[1], wout1.at[rows, :], local_sem.at[3])]
            for l in range(DEPTH):
                loads[2 * l].wait()
                win_bf[l] = win_ref[l].astype(BF16)
                own[2 * l].start()
                loads[2 * l + 1].wait()
                wout_bf[l] = wout_ref[l].astype(BF16)
                own[2 * l + 1].start()
                if l == 0:
                    own[0].wait()
                    own[1].wait()
                    gather0.start_first_round()
            for cp in c_copies:
                cp.wait()
            loads[4].wait()

            cv = c_all[...]
            c_out[...] = cv
            silu_c = (cv * _sigmoid(cv)).astype(BF16)
            for l in range(DEPTH):
                mod_mine[l] = _dot(silu_c, wada_ref[l].astype(BF16))
            mod_all[p.chip] = mod_mine[...]
            m_copies = [pltpu.make_async_remote_copy(mod_mine, mod_all.at[p.chip], mod_send.at[k], mod_recv.at[k],
                                                     device_id=(px, py, p.c), device_id_type=MESH)
                        for k, (px, py) in enumerate(p.other_chips())]
            for cp in m_copies:
                cp.start()
            for q in range(gather0.CHUNKS):
                gather0.start_second_round(q)
            own[2].wait()
            own[3].wait()
            gather1.start_first_round()
            for cp in m_copies:
                cp.wait()
            mod_ref[...] = jnp.zeros_like(mod_ref)
            for l in range(DEPTH):
                full = jnp.concatenate([mod_all[ch, l, pl.ds(p.dev, 1), :] for ch in range(N_CHIP)], axis=1) + bada_ref[l:l + 1, :]
                for k in range(3):
                    mod_ref[l, k:k + 1, :] = full[:, k * D_MODEL:(k + 1) * D_MODEL]
            mod_out[...] = mod_ref[...]
            for q in range(gather0.CHUNKS):
                gather0.pass_second_round(q)
            gather0.finish()
            for cp in keep:
                cp.start()

        for q in range(_WeightGather.CHUNKS):
            @pl.when(i == GATHER_SECOND_ROUND_STEP + q)
            def _(q=q):
                gather1.start_second_round(q)

            @pl.when(i == GATHER_PASS_STEP)
            def _(q=q):
                gather1.pass_second_round(q)

        _forward_tile(0, i, x_ref, mod_ref.at[0], win_v, wout_v, small_refs, lng_ref, lnb_ref, carry_ref, saved_refs)

        @pl.when(i == N_TILE - 1)
        def _():
            gather1.finish()
            for cp in keep:
                cp.wait()

    hbm = pl.BlockSpec(memory_space=pl.ANY)
    tile3 = pl.BlockSpec((None, ROWS, D_MODEL), lambda i: (0, i, 0))
    in_specs = [tile3, _const_in((1, D_MODEL)), hbm, _const_in((DEPTH, 3 * D_MODEL)), hbm, hbm]
    in_specs += _layer_weight_specs(0) + [_const_in((DEPTH, D_MODEL)), _const_in((DEPTH, D_MODEL))]
    out_shape, out_specs = _saved_outputs()
    w_in_shape = jax.ShapeDtypeStruct((D_MODEL, D_PROJ), BF16)
    w_out_shape = jax.ShapeDtypeStruct((D_MODEL, D_MODEL), BF16)
    out_shape += [w_in_shape, w_out_shape, w_in_shape, w_out_shape,
                  jax.ShapeDtypeStruct((DEPTH, 8, D_MODEL), F32), jax.ShapeDtypeStruct((N_DEV, D_MODEL), F32)]
    out_specs += [hbm, hbm, hbm, hbm, _const((DEPTH, 8, D_MODEL)), _const((N_DEV, D_MODEL))]
    gather_sems = [pltpu.SemaphoreType.DMA((_WeightGather.N_SEMS,))] * 4
    scratch = [
        pltpu.VMEM((HALO, D_POOL), F32),
        pltpu.VMEM(w_ada.shape, F32), pltpu.VMEM(w_in.shape, F32), pltpu.VMEM(w_out.shape, F32),
        pltpu.VMEM((DEPTH, D_MODEL, W_IN_COLS), BF16), pltpu.VMEM((DEPTH, W_OUT_ROWS, D_MODEL), BF16),
        pltpu.VMEM((DEPTH, N_DEV, W_ADA_COLS), F32), pltpu.VMEM((N_CHIP, DEPTH, N_DEV, W_ADA_COLS), F32),
        pltpu.VMEM((N_DEV, D_MODEL), F32), pltpu.VMEM((DEPTH, 8, D_MODEL), F32),
        pltpu.VMEM((D_MODEL, D_PROJ), BF16), pltpu.VMEM((D_MODEL, D_MODEL), BF16),
    ] + gather_sems + [
        pltpu.SemaphoreType.DMA((7,)), pltpu.SemaphoreType.DMA((7,)),
        pltpu.SemaphoreType.DMA((3,)), pltpu.SemaphoreType.DMA((3,)),
        pltpu.SemaphoreType.DMA((11,)),
    ]
    return pl.pallas_call(
        body, name="fwd_first", grid=(N_TILE,), in_specs=in_specs, out_specs=out_specs, out_shape=out_shape,
        scratch_shapes=scratch,
        compiler_params=pltpu.CompilerParams(dimension_semantics=("arbitrary",), vmem_limit_bytes=VMEM_LIMIT),
    )(x, c_vec, w_ada, b_ada, w_in, w_out, *small, ln_g, ln_b)


IN_STEPS = W_IN_COLS // HEAD
OUT_STEPS = 4
OUT_COLS = D_MODEL // OUT_STEPS
OUT_FIRST = 2
ITEMS = ([("out", k) for k in range(OUT_FIRST)] + [("in", k) for k in range(IN_STEPS)]
         + [("out", k) for k in range(OUT_FIRST, OUT_STEPS)])
N_ITEMS = len(ITEMS)
N_STEPS = DEPTH * N_ITEMS
DELAY_SUM, DELAY_SECOND, DELAY_FINAL = 1, 3, 5
SMALL_SCATTER_STEP, SMALL_GATHER_STEP, SMALL_PASS_STEP, SMALL_FINISH_STEP = 1, 3, 5, 7


def _wgrad_reduce(h, dproj, cat, dy, pack, dmod):
    def body(*refs):
        h_ref, dp_refs, cat_ref, dy_ref, pack_ref, dmod_ref = refs[0], refs[1:5], refs[5], refs[6], refs[7], refs[8]
        fin_in, fin_out, pack_out, dmod_out = refs[9:13]
        scratch = refs[13:]
        (mine_in, send_in, sib_in, st_in, r1_in, r2_in, f_in,
         mine_out, send_out, sib_out, st_out, r1_out, r2_out, f_out,
         d2d_s, d2d_r, r1_s, r1_r, r2_s, r2_r, fin_l, fin_s, fin_r) = scratch[:23]
        p = _Place()
        c = p.c
        i = pl.program_id(0)
        my_rows = pl.ds(pl.multiple_of(c * HALF_IN, HALF_IN), HALF_IN)

        def layer_of(j):
            return DEPTH - 1 - j // N_ITEMS

        def bufs(j):
            kind, k = ITEMS[j % N_ITEMS]
            if kind == "in":
                return [r.at[k] for r in (mine_in, send_in, sib_in, st_in, r1_in, r2_in, f_in)]
            return [r.at[k] for r in (mine_out, send_out, sib_out, st_out, r1_out, r2_out, f_out)]

        def piece(j, ref, ch):
            if ITEMS[j % N_ITEMS][0] == "in":
                return ref.at[:, ch * HEAD:(ch + 1) * HEAD]
            return ref.at[ch]

        def slot(ch):
            return jnp.where(c == 0, ch % 2, ch // 2)

        def to_sibling(j):
            _, send, sib, _, _, _, _ = bufs(j)
            return pltpu.make_async_remote_copy(send, sib, d2d_s.at[j], d2d_r.at[j], device_id=p.sibling, device_id_type=MESH)

        def first_round(j, ch):
            _, _, _, st, r1, _, _ = bufs(j)
            k = slot(ch)
            return pltpu.make_async_remote_copy(st.at[k], r1.at[k], r1_s.at[2 * j + k], r1_r.at[2 * j + k],
                                                device_id=p.first, device_id_type=MESH)

        def second_round(j):
            _, _, _, st, _, r2, _ = bufs(j)
            return pltpu.make_async_remote_copy(st.at[2], r2, r2_s.at[j], r2_r.at[j], device_id=p.second, device_id_type=MESH)

        def finals(j):
            f = bufs(j)[6]
            kind, k = ITEMS[j % N_ITEMS]
            if kind == "in":
                dst = fin_in.at[layer_of(j), my_rows, k * HEAD:(k + 1) * HEAD]
            else:
                dst = fin_out.at[layer_of(j), c, :, k * OUT_COLS:(k + 1) * OUT_COLS]
            return [pltpu.make_async_copy(f, dst, fin_l.at[j]),
                    pltpu.make_async_remote_copy(f, dst, fin_s.at[j], fin_r.at[j], device_id=p.sibling, device_id_type=MESH)]

        def stage_sum(j):
            mine, _, sib, st, _, _, _ = bufs(j)
            to_sibling(j).wait_recv()
            mine[...] = mine[...] + sib[...]
            for ch in range(N_CHIP):
                @pl.when(p.first_coord(ch) != p.my_first_coord)
                def _(ch=ch):
                    st[slot(ch)] = piece(j, mine, ch)[...].astype(BF16)
                    first_round(j, ch).start()

        def stage_second(j):
            mine, _, _, st, r1, _, _ = bufs(j)
            for ch in range(N_CHIP):
                @pl.when(p.first_coord(ch) == p.my_first_coord)
                def _(ch=ch):
                    first_round(j, ch).wait_recv()
                    part = piece(j, mine, ch)
                    total = part[...] + r1[slot(ch)].astype(F32)
                    part[...] = total

                    @pl.when(ch != p.chip)
                    def _():
                        st[2] = total.astype(BF16)
                        second_round(j).start()

        def stage_final(j):
            mine, _, _, _, _, r2, f = bufs(j)
            second_round(j).wait_recv()
            for ch in range(N_CHIP):
                @pl.when(ch == p.chip)
                def _(ch=ch):
                    f[...] = piece(j, mine, ch)[...] + r2[...].astype(F32)
            for cp in finals(j):
                cp.start()

        def drain(j):
            to_sibling(j).wait_send()
            for ch in range(N_CHIP):
                @pl.when(p.first_coord(ch) != p.my_first_coord)
                def _(ch=ch):
                    first_round(j, ch).wait_send()

                @pl.when(jnp.logical_and(p.first_coord(ch) == p.my_first_coord, ch != p.chip))
                def _():
                    second_round(j).wait_send()
            for cp in finals(j):
                cp.wait()

        dev = p.dev
        devices = p.others()

        def half(core):
            return pl.ds(pl.multiple_of(core * PK_HALF, 16), PK_HALF)

        def finished(core, ch):
            return pl.ds(pl.multiple_of(core * PK_HALF + ch * PK_PIECE, 16), PK_PIECE)

        def small_exchange(l, first_step, bufs_l):
            (pk_mine, pk_sib, pk_st, pk_rs, pk_fin, pk_all, dm_st, dm_all, pk_sem, rs_s, rs_r, ag_s, ag_r, dm_s, dm_r) = bufs_l

            def pk_load():
                return pltpu.make_async_copy(pack_ref.at[l, half(c)], pk_mine, pk_sem.at[0])

            def pk_give():
                return pltpu.make_async_remote_copy(pack_ref.at[l, half(1 - c)], pk_sib, pk_sem.at[1], pk_sem.at[2],
                                                    device_id=p.sibling, device_id_type=MESH)

            def pk_scatter(ch):
                return pltpu.make_async_remote_copy(pk_st.at[ch * PK_PIECE:(ch + 1) * PK_PIECE], pk_rs.at[p.chip],
                                                    rs_s.at[ch], rs_r.at[p.chip], device_id=(ch // 2, ch % 2, c),
                                                    device_id_type=MESH)

            def pk_spread(ch):
                return pltpu.make_async_remote_copy(pk_fin, pk_all.at[finished(c, p.chip)], ag_s.at[ch], ag_r.at[p.chip],
                                                    device_id=(ch // 2, ch % 2, c), device_id_type=MESH)

            def pk_pass():
                return pltpu.make_async_remote_copy(pk_all.at[half(c)], pk_all.at[half(c)], pk_sem.at[3], pk_sem.at[4],
                                                    device_id=p.sibling, device_id_type=MESH)

            def dm_copy(r):
                return pltpu.make_async_remote_copy(dm_st, dm_all.at[:, pl.ds(dev, 1), :], dm_s.at[r], dm_r.at[r],
                                                    device_id=devices[r], device_id_type=MESH)

            def results():
                return [pltpu.make_async_copy(pk_all, pack_out.at[l], pk_sem.at[0]),
                        pltpu.make_async_copy(dm_all, dmod_out.at[l], pk_sem.at[5])]

            @pl.when(i == first_step)
            def _():
                pk_load().start()
                pk_give().start()
                for k in range(3):
                    for r in range(D_MODEL // HEAD):
                        dm_st[8 * k + r] = dmod_ref[l, k:k + 1, r * HEAD:(r + 1) * HEAD]
                dm_st[DM_LOSS] = dmod_ref[l, 3:4, 0:HEAD]
                dm_all[:, pl.ds(dev, 1), :] = dm_st[...]
                for r in range(N_DEV - 1):
                    dm_copy(r).start()

            @pl.when(i == first_step + SMALL_SCATTER_STEP)
            def _():
                pk_load().wait()
                pk_give().wait()
                total = pk_mine[...] + pk_sib[...]
                pk_mine[...] = total
                pk_st[...] = total.astype(BF16)
                for ch in range(N_CHIP):
                    @pl.when(ch != p.chip)
                    def _(ch=ch):
                        pk_scatter(ch).start()

            @pl.when(i == first_step + SMALL_GATHER_STEP)
            def _():
                for ch in range(N_CHIP):
                    @pl.when(ch != p.chip)
                    def _(ch=ch):
                        pltpu.make_async_remote_copy(pk_fin, pk_rs.at[ch], rs_s.at[ch], rs_r.at[ch],
                                                     device_id=p.sibling, device_id_type=MESH).wait_recv()
                for me in range(N_CHIP):
                    @pl.when(me == p.chip)
                    def _(me=me):
                        total = None
                        for ch in range(N_CHIP):
                            part = pk_mine[me * PK_PIECE:(me + 1) * PK_PIECE] if ch == me else pk_rs[ch].astype(F32)
                            total = part if total is None else total + part
                        pk_fin[...] = total.astype(BF16)
                        pk_all[finished(c, me)] = total.astype(BF16)
                for ch in range(N_CHIP):
                    @pl.when(ch != p.chip)
                    def _(ch=ch):
                        pk_spread(ch).start()

            @pl.when(i == first_step + SMALL_PASS_STEP)
            def _():
                for ch in range(N_CHIP):
                    @pl.when(ch != p.chip)
                    def _(ch=ch):
                        pltpu.make_async_remote_copy(pk_fin, pk_all.at[finished(c, ch)], ag_s.at[ch], ag_r.at[ch],
                                                     device_id=p.sibling, device_id_type=MESH).wait_recv()
                pk_pass().start()

            @pl.when(i == first_step + SMALL_FINISH_STEP)
            def _():
                pk_pass().wait()
                for ch in range(N_CHIP):
                    @pl.when(ch != p.chip)
                    def _(ch=ch):
                        pk_scatter(ch).wait_send()
                        pk_spread(ch).wait_send()
                for r in range(N_DEV - 1):
                    dm_copy(r).wait()
                for cp in results():
                    cp.start()
                for cp in results():
                    cp.wait()

        n_small = 15
        for l in range(DEPTH):
            small_exchange(l, (DEPTH - 1 - l) * N_ITEMS, scratch[23 + n_small * l:23 + n_small * (l + 1)])

        for step in range(N_ITEMS, N_STEPS):
            @pl.when(i == step)
            def _(step=step):
                drain(step - N_ITEMS)

        ii = jnp.where(i < N_ITEMS, i, i - N_ITEMS)
        in_step = jnp.logical_and(ii >= OUT_FIRST, ii < OUT_FIRST + IN_STEPS)

        @pl.when(in_step)
        def _():
            k = ii - OUT_FIRST
            rhs = jnp.concatenate([r[...] for r in dp_refs], axis=1)
            res = _dot(h_ref[...], rhs, TN)

            @pl.when(c == 0)
            def _():
                mine_in[k] = res[:HALF_IN]
                send_in[k] = res[HALF_IN:]

            @pl.when(c == 1)
            def _():
                mine_in[k] = res[HALF_IN:]
                send_in[k] = res[:HALF_IN]

        @pl.when(jnp.logical_not(in_step))
        def _():
            k = jnp.where(ii < OUT_FIRST, ii, ii - IN_STEPS)
            res = _dot(cat_ref[...], dy_ref[...], TN)

            @pl.when(c == 0)
            def _():
                for ch in range(N_CHIP):
                    mine_out[k, ch] = res[ch * W_OUT_ROWS:ch * W_OUT_ROWS + HALF_OUT]
                    send_out[k, ch] = res[ch * W_OUT_ROWS + HALF_OUT:(ch + 1) * W_OUT_ROWS]

            @pl.when(c == 1)
            def _():
                for ch in range(N_CHIP):
                    mine_out[k, ch] = res[ch * W_OUT_ROWS + HALF_OUT:(ch + 1) * W_OUT_ROWS]
                    send_out[k, ch] = res[ch * W_OUT_ROWS:ch * W_OUT_ROWS + HALF_OUT]

        stages = ((0, lambda j: to_sibling(j).start()), (DELAY_SUM, stage_sum), (DELAY_SECOND, stage_second),
                  (DELAY_FINAL, stage_final))
        for step in range(N_STEPS):
            @pl.when(i == step)
            def _(step=step):
                for delay, stage in stages:
                    if step - delay >= 0:
                        stage(step - delay)

        @pl.when(i == N_STEPS - 1)
        def _():
            for step in range(N_STEPS, N_STEPS + DELAY_FINAL):
                for delay, stage in stages:
                    if 0 <= step - delay < N_STEPS:
                        stage(step - delay)
            for j in range(N_STEPS - N_ITEMS, N_STEPS):
                drain(j)

    hbm = pl.BlockSpec(memory_space=pl.ANY)

    def layer(i):
        return jnp.where(i < N_ITEMS, DEPTH - 1, 0)

    def item(i):
        return jnp.where(i < N_ITEMS, i, i - N_ITEMS)

    def whole(i):
        return (layer(i), 0, 0)

    def dproj_piece(ch):
        return pl.BlockSpec((None, SEQ, HEAD),
                            lambda i: (layer(i), 0, ch * IN_STEPS + jnp.clip(item(i) - OUT_FIRST, 0, IN_STEPS - 1)))

    def dy_quarter(i):
        return (layer(i), 0, jnp.where(item(i) < OUT_FIRST, item(i), jnp.maximum(item(i) - IN_STEPS, OUT_FIRST)))

    operand = pl.BlockSpec((None, SEQ, D_MODEL), whole)
    in_specs = [operand] + [dproj_piece(ch) for ch in range(N_CHIP)]
    in_specs += [operand, pl.BlockSpec((None, SEQ, OUT_COLS), dy_quarter), hbm, _const_in((DEPTH, 8, D_MODEL))]
    args = [h, dproj, dproj, dproj, dproj, cat, dy, pack, dmod]
    out_shape = [jax.ShapeDtypeStruct((DEPTH, D_MODEL, W_IN_COLS), F32), jax.ShapeDtypeStruct((DEPTH, 2, HALF_OUT, D_MODEL), F32),
                 jax.ShapeDtypeStruct((DEPTH, PK_ROWS, HEAD), BF16), jax.ShapeDtypeStruct((DEPTH, DM_ROWS, N_DEV, HEAD), F32)]
    out_specs = [hbm, hbm, hbm, hbm]
    in_item = lambda *lead: pltpu.VMEM(lead + (HALF_IN, HEAD), BF16)
    out_item = lambda *lead: pltpu.VMEM(lead + (HALF_OUT, OUT_COLS), BF16)
    scratch = [
        pltpu.VMEM((IN_STEPS, HALF_IN, N_CHIP * HEAD), F32), pltpu.VMEM((IN_STEPS, HALF_IN, N_CHIP * HEAD), F32),
        pltpu.VMEM((IN_STEPS, HALF_IN, N_CHIP * HEAD), F32), in_item(IN_STEPS, 3), in_item(IN_STEPS, 2), in_item(IN_STEPS),
        pltpu.VMEM((IN_STEPS, HALF_IN, HEAD), F32),
        pltpu.VMEM((OUT_STEPS, N_CHIP, HALF_OUT, OUT_COLS), F32), pltpu.VMEM((OUT_STEPS, N_CHIP, HALF_OUT, OUT_COLS), F32),
        pltpu.VMEM((OUT_STEPS, N_CHIP, HALF_OUT, OUT_COLS), F32), out_item(OUT_STEPS, 3), out_item(OUT_STEPS, 2),
        out_item(OUT_STEPS), pltpu.VMEM((OUT_STEPS, HALF_OUT, OUT_COLS), F32),
        pltpu.SemaphoreType.DMA((N_STEPS,)), pltpu.SemaphoreType.DMA((N_STEPS,)),
        pltpu.SemaphoreType.DMA((2 * N_STEPS,)), pltpu.SemaphoreType.DMA((2 * N_STEPS,)),
        pltpu.SemaphoreType.DMA((N_STEPS,)), pltpu.SemaphoreType.DMA((N_STEPS,)),
        pltpu.SemaphoreType.DMA((N_STEPS,)), pltpu.SemaphoreType.DMA((N_STEPS,)), pltpu.SemaphoreType.DMA((N_STEPS,)),
    ]
    for _ in range(DEPTH):
        scratch += [
            pltpu.VMEM((PK_HALF, HEAD), F32), pltpu.VMEM((PK_HALF, HEAD), F32), pltpu.VMEM((PK_HALF, HEAD), BF16),
            pltpu.VMEM((N_CHIP, PK_PIECE, HEAD), BF16), pltpu.VMEM((PK_PIECE, HEAD), BF16), pltpu.VMEM((PK_ROWS, HEAD), BF16),
            pltpu.VMEM((DM_ROWS, 1, HEAD), F32), pltpu.VMEM((DM_ROWS, N_DEV, HEAD), F32),
            pltpu.SemaphoreType.DMA((6,)),
            pltpu.SemaphoreType.DMA((N_CHIP,)), pltpu.SemaphoreType.DMA((N_CHIP,)),
            pltpu.SemaphoreType.DMA((N_CHIP,)), pltpu.SemaphoreType.DMA((N_CHIP,)),
            pltpu.SemaphoreType.DMA((N_DEV - 1,)), pltpu.SemaphoreType.DMA((N_DEV - 1,)),
        ]
    return pl.pallas_call(
        body, name="wgrad", grid=(N_STEPS,), in_specs=in_specs, out_specs=out_specs, out_shape=out_shape,
        scratch_shapes=scratch,
        compiler_params=pltpu.CompilerParams(dimension_semantics=("arbitrary",), vmem_limit_bytes=VMEM_LIMIT),
    )(*args)


def _adamw(w, g, m, v):
    m = ADAM_B1 * m + (1.0 - ADAM_B1) * g
    v = ADAM_B2 * v + (1.0 - ADAM_B2) * (g * g)
    m_hat = m / (1.0 - ADAM_B1 ** ADAM_STEP)
    v_hat = v / (1.0 - ADAM_B2 ** ADAM_STEP)
    delta = -ADAM_LR * (m_hat / (jnp.sqrt(v_hat) + ADAM_EPS) + ADAM_WD * w)
    return delta, m, v


def _adam_sharded(c_all, dmods, ada, w_in_set, w_out_set):
    rows = D_MODEL // ADAM_PARTS

    def body(c_ref, dm_ref, wa_ref, ma_ref, va_ref, wi_ref, gi_ref, mi_ref, vi_ref, wo_ref, go_ref, mo_ref, vo_ref,
             ga_out, da_out, ma_out, va_out, di_out, mi_out, vi_out, do_out, mo_out, vo_out):
        l = pl.program_id(0)
        chip = 2 * lax.axis_index("x") + lax.axis_index("y")
        cv = c_ref[...]
        silu_c = (cv * _sigmoid(cv)).astype(BF16).astype(F32)
        pieces = []
        for k in range(W_ADA_COLS // HEAD):
            dk = dm_ref[l, (W_ADA_COLS // HEAD) * chip + k].astype(BF16).astype(F32)
            pieces.append(_dot_exact(silu_c, dk, TN))
        g = jnp.concatenate(pieces, axis=1)
        ga_out[...] = g
        da_out[...], ma_out[...], va_out[...] = _adamw(wa_ref[...], g, ma_ref[...], va_ref[...])
        di_out[...], mi_out[...], vi_out[...] = _adamw(wi_ref[...], gi_ref[...], mi_ref[...], vi_ref[...])
        do_out[...], mo_out[...], vo_out[...] = _adamw(wo_ref[...], go_ref[...], mo_ref[...], vo_ref[...])

    def blk(r, cols):
        return pl.BlockSpec((None, r, cols), lambda l, i: (l, i, 0))

    b_ada, b_in, b_out = blk(rows, W_ADA_COLS), blk(rows, W_IN_COLS), blk(W_OUT_ROWS // ADAM_PARTS, D_MODEL)
    shapes = [jax.ShapeDtypeStruct(a[0].shape, F32) for a in (ada, w_in_set, w_out_set)]
    return pl.pallas_call(
        body, name="adam_sharded", grid=(DEPTH, ADAM_PARTS),
        in_specs=[pl.BlockSpec((N_DEV, rows), lambda l, i: (0, i)), _const_in((DEPTH, DM_ROWS, N_DEV, HEAD))]
        + [b_ada] * 3 + [b_in] * 4 + [b_out] * 4,
        out_specs=[b_ada] * 4 + [b_in] * 3 + [b_out] * 3,
        out_shape=[shapes[0]] * 4 + [shapes[1]] * 3 + [shapes[2]] * 3,
        compiler_params=pltpu.CompilerParams(dimension_semantics=("arbitrary", "arbitrary"), vmem_limit_bytes=VMEM_LIMIT),
    )(c_all, dmods, *ada, *w_in_set, *w_out_set)


def _adam_small(packs, dmods, weights, ms, vs):
    n = len(weights)

    def body(*refs):
        dm_refs = refs[1]
        b = 2
        w_refs, m_refs, v_refs = refs[b:b + n], refs[b + n:b + 2 * n], refs[b + 2 * n:b + 3 * n]
        outs = refs[b + 3 * n:b + 3 * n + 4 * n + 1]
        pack_refs = refs[-1]
        pack_refs[...] = refs[0][...].astype(F32)
        g_refs, d_refs, nm_refs, nv_refs = outs[0:n], outs[n:2 * n], outs[2 * n:3 * n], outs[3 * n:4 * n]
        squares = dm_refs[DEPTH - 1, DM_LOSS]
        total = squares[0:1, 0:1]
        for d in range(1, N_DEV):
            total = total + squares[d:d + 1, 0:1]
        outs[4 * n][...] = total * (0.5 / D_MODEL)

        def lanes(l, row0, count):
            return jnp.concatenate([pack_refs.at[l][row0 + k:row0 + k + 1, :] for k in range(count)], axis=1)

        def update(idx, at, g):
            g_refs[idx][at] = g
            d_refs[idx][at], nm_refs[idx][at], nv_refs[idx][at] = _adamw(w_refs[idx][at], g, m_refs[idx][at], v_refs[idx][at])

        for l in range(DEPTH):
            row = (slice(l, l + 1), slice(None))
            g_b = None
            for d in range(N_DEV):
                part = dm_refs.at[l][0:DM_LOSS, d, :]
                g_b = part if g_b is None else g_b + part
            update(0, row, jnp.concatenate([g_b[k:k + 1, :] for k in range(DM_LOSS)], axis=1))
            for g in range(N_HEAD):
                update(1, (l, g), pack_refs.at[l][PK_W_POOL + g * HEAD:PK_W_POOL + (g + 1) * HEAD, :])
                update(5, (l, g), pack_refs.at[l][PK_W_SGU + g * HEAD:PK_W_SGU + (g + 1) * HEAD, :])
            update(2, row, lanes(l, PK_POOL_SCALE, N_HEAD))
            update(3, (l,), pack_refs.at[l][PK_SGU_LN_G:PK_SGU_LN_G + N_HEAD, :])
            update(4, (l,), pack_refs.at[l][PK_SGU_LN_B:PK_SGU_LN_B + N_HEAD, :])
            update(6, (l,), pack_refs.at[l][PK_B_SGU:PK_B_SGU + N_HEAD, :])
            update(7, row, lanes(l, PK_LN_G, D_MODEL // HEAD))
            update(8, row, lanes(l, PK_LN_B, D_MODEL // HEAD))

    vmem = pl.BlockSpec(memory_space=pltpu.VMEM)
    shapes = [jax.ShapeDtypeStruct(w.shape, F32) for w in weights]
    return pl.pallas_call(
        body, name="adam_small", in_specs=[vmem] * (2 + 3 * n), out_specs=[vmem] * (4 * n + 1),
        out_shape=shapes * 4 + [jax.ShapeDtypeStruct((1, 1), F32)],
        scratch_shapes=[pltpu.VMEM(packs.shape, F32)],
        compiler_params=pltpu.CompilerParams(vmem_limit_bytes=VMEM_LIMIT),
    )(packs, dmods, *weights, *ms, *vs)


def kernel(x, c, w_ada, b_ada, w_in, w_pool, pool_scale, sgu_ln_g, sgu_ln_b, w_sgu, b_sgu, w_out, ln_g, ln_b, loss_target, m_w_ada, m_b_ada, m_w_in, m_w_pool, m_pool_scale, m_sgu_ln_g, m_sgu_ln_b, m_w_sgu, m_b_sgu, m_w_out, m_ln_g, m_ln_b, v_w_ada, v_b_ada, v_w_in, v_w_pool, v_pool_scale, v_sgu_ln_g, v_sgu_ln_b, v_w_sgu, v_b_sgu, v_w_out, v_ln_g, v_ln_b):
    small = (w_pool, pool_scale, sgu_ln_g, sgu_ln_b, w_sgu, b_sgu)
    *saved0, w_in0, w_out0, w_in1, w_out1, mod, c_all = _forward_first(x, c, w_ada, b_ada, w_in, w_out, small, ln_g, ln_b)
    *saved1, dout, sq = _forward_last(saved0[3], mod, w_in1, w_out1, small, ln_g, ln_b, loss_target)

    dx1, *shared = _backward_layer(1, dout, saved1, mod, w_in1, w_out1, small, ln_g, sq=sq)
    dx0, h, cat, dy, dproj, pack, dmod = _backward_layer(0, dx1, saved0, mod, w_in0, w_out0, small, ln_g, shared=shared)
    g_in, g_out, pack, dmods = _wgrad_reduce(h, dproj, cat, dy, pack, dmod)

    g_out = g_out.reshape(DEPTH, W_OUT_ROWS, D_MODEL)
    big = _adam_sharded(c_all, dmods, (w_ada, m_w_ada, v_w_ada), (w_in, g_in, m_w_in, v_w_in), (w_out, g_out, m_w_out, v_w_out))
    ada, win, wout = big[0:4], (g_in, *big[4:7]), (g_out, *big[7:10])
    small_w = (b_ada, w_pool, pool_scale, sgu_ln_g, sgu_ln_b, w_sgu, b_sgu, ln_g, ln_b)
    small_m = (m_b_ada, m_w_pool, m_pool_scale, m_sgu_ln_g, m_sgu_ln_b, m_w_sgu, m_b_sgu, m_ln_g, m_ln_b)
    small_v = (v_b_ada, v_w_pool, v_pool_scale, v_sgu_ln_g, v_sgu_ln_b, v_w_sgu, v_b_sgu, v_ln_g, v_ln_b)
    res = _adam_small(pack, dmods, small_w, small_m, small_v)
    n = len(small_w)
    loss = res[4 * n].reshape(())

    def ordered(k):
        s = res[k * n:(k + 1) * n]
        return (ada[k], s[0], win[k], s[1], s[2], s[3], s[4], s[5], s[6], wout[k], s[7], s[8])

    return (loss, dx0[None], *ordered(0), *ordered(1), *ordered(2), *ordered(3))
```

```python
import jax
import jax.numpy as jnp
from jax import lax
from jax.experimental import pallas as pl
from jax.experimental.pallas import tpu as pltpu

F32 = jnp.float32
BF16 = jnp.bfloat16
MESH = pl.DeviceIdType.MESH

N_DEV = 8
N_CHIP = 4
DEPTH = 2
SEQ = 2048
D_MODEL = 1024
D_POOL = 512
D_PROJ = 2560
HEAD = 128
N_HEAD = 4
ROWS = 256
N_TILE = SEQ // ROWS
HALO = 16
W_IN_COLS = D_PROJ // N_CHIP
W_OUT_ROWS = D_MODEL // N_CHIP
W_ADA_COLS = 3 * D_MODEL // N_CHIP
HALF_IN = D_MODEL // 2
HALF_OUT = W_OUT_ROWS // 2
DEEPNORM_ALPHA = (2.0 * DEPTH) ** 0.25
LN_EPS = 1e-5
INV_SQRT2 = 0.7071067811865476
INV_SQRT_2PI = 0.3989422804014327

ADAM_LR = 0.001
ADAM_B1 = 0.9
ADAM_B2 = 0.999
ADAM_EPS = 1e-08
ADAM_WD = 0.01
ADAM_STEP = 10
ADAM_PARTS = 2

PK_W_POOL = 0
PK_W_SGU = 512
PK_POOL_SCALE = 1024
PK_SGU_LN_G = 1032
PK_SGU_LN_B = 1040
PK_B_SGU = 1048
PK_LN_G = 1056
PK_LN_B = 1064
PK_ROWS = 1152
PK_HALF = PK_ROWS // 2
PK_PIECE = PK_HALF // N_CHIP
DM_LOSS = 3 * D_MODEL // HEAD
DM_ROWS = DM_LOSS + 1

VMEM_LIMIT = 56 * 1024 * 1024

GATHER_SECOND_ROUND_STEP = 0
GATHER_PASS_STEP = N_TILE - 3

NN = (((1,), (0,)), ((), ()))
NT = (((1,), (1,)), ((), ()))
TN = (((0,), (0,)), ((), ()))


def _dot(a, b, dims=NN):
    return lax.dot_general(a, b, dims, preferred_element_type=F32)


def _dot_exact(a, b, dims=NN):
    return lax.dot_general(a, b, dims, preferred_element_type=F32, precision=lax.Precision.HIGHEST)


def _layer_norm(v):
    mu = jnp.mean(v, axis=-1, keepdims=True)
    d = v - mu
    var = jnp.mean(d * d, axis=-1, keepdims=True)
    rstd = lax.rsqrt(var + LN_EPS)
    return d * rstd, rstd


def _layer_norm_bwd(dvhat, vhat, rstd):
    m1 = jnp.mean(dvhat, axis=-1, keepdims=True)
    m2 = jnp.mean(dvhat * vhat, axis=-1, keepdims=True)
    return rstd * (dvhat - m1 - vhat * m2)


def _sigmoid(v):
    return 1.0 / (1.0 + jnp.exp(-v))


def _gelu_parts(v):
    phi = 0.5 * (1.0 + lax.erf(v * INV_SQRT2))
    pdf = INV_SQRT_2PI * jnp.exp(-0.5 * v * v)
    return phi, pdf


def _sum_rows(v):
    return jnp.sum(v, axis=0, keepdims=True)


def _window_sums(ext, toward_later):
    n = ext.shape[0]

    def shifted(v, k):
        return pltpu.roll(v, (n - k) if toward_later else k, 0)

    s2 = ext + shifted(ext, 1)
    r4 = s2[:, HEAD:]
    s4 = r4 + shifted(r4, 2)
    r8 = s4[:, HEAD:]
    s8 = r8 + shifted(r8, 4)
    r16 = s8[:, HEAD:]
    s16 = r16 + shifted(r16, 8)
    return jnp.concatenate([s2[:, :HEAD], s4[:, :HEAD], s8[:, :HEAD], s16], axis=1)


def _window_counts(row0):
    t1 = row0 + 1 + lax.broadcasted_iota(jnp.int32, (ROWS, D_POOL), 0)
    lane = lax.broadcasted_iota(jnp.int32, (ROWS, D_POOL), 1)
    width = jnp.where(lane < HEAD, 2, jnp.where(lane < 2 * HEAD, 4, jnp.where(lane < 3 * HEAD, 8, 16)))
    return jnp.minimum(t1, width).astype(F32)


def _causal_mask():
    r = lax.broadcasted_iota(jnp.int32, (HEAD, HEAD), 0)
    s = lax.broadcasted_iota(jnp.int32, (HEAD, HEAD), 1)
    return r >= s


def _chunks_to_lanes(v):
    return jnp.concatenate([v[n * HEAD:(n + 1) * HEAD] for n in range(ROWS // HEAD)], axis=1)


def _lanes_to_chunks(v):
    return jnp.concatenate([v[:, n * HEAD:(n + 1) * HEAD] for n in range(ROWS // HEAD)], axis=0)


def _pack_stats(rstd_x, rstd_z, rstd_v):
    lane = lax.broadcasted_iota(jnp.int32, (ROWS, HEAD), 1)
    packed = rstd_x
    for k, r in enumerate([rstd_z] + list(rstd_v)):
        packed = jnp.where(lane < 16 * (k + 1), packed, r)
    return packed


def _unpack_stats(stats):
    cols = [stats[:, 16 * k:16 * k + 1] for k in range(2 + N_HEAD)]
    return cols[0], cols[1], cols[2:]


def _mixer(proj, halo, row0, wpool_ref, pscale, sgu_g_ref, sgu_b_ref, wsgu_ref, bsgu_ref, saved=None):
    xa = proj[:, 0:512]
    ga = proj[:, 512:1024]
    u = proj[:, 1024:1536]
    v = proj[:, 1536:2048]
    gb = proj[:, 2048:2560]
    ext = jnp.concatenate([halo, xa], axis=0)
    win = _window_sums(ext, toward_later=False)[HALO:]
    cnt = _window_counts(row0)
    pooled = (win / cnt - xa).astype(BF16)
    pw = jnp.concatenate(
        [_dot(pooled[:, g * HEAD:(g + 1) * HEAD], wpool_ref[g].astype(BF16)) for g in range(N_HEAD)], axis=1)
    sig_a = _sigmoid(ga) if saved is None else saved["sig_a"]
    ya = pw * pscale * (ga * sig_a)
    phi_u, pdf_u = _gelu_parts(u)
    phi_v, pdf_v = _gelu_parts(v)
    gu = u * phi_u
    gv = v * phi_v
    sig_b = _sigmoid(gb) if saved is None else saved["sig_b"]
    silu_b = gb * sig_b
    mask = _causal_mask()
    diag = lax.broadcasted_iota(jnp.int32, (HEAD, HEAD), 0) == lax.broadcasted_iota(jnp.int32, (HEAD, HEAD), 1)
    vhat, rstd_v, vln_l, mixed = [], [], [], []
    for h in range(N_HEAD):
        if saved is None:
            vh, rh = _layer_norm(gv[:, h * HEAD:(h + 1) * HEAD])
        else:
            vh, rh = saved["vhat"][h], saved["rstd_v"][h]
        ln = (vh * sgu_g_ref[h:h + 1, :] + sgu_b_ref[h:h + 1, :]).astype(BF16)
        ln_l = _chunks_to_lanes(ln)
        wm = jnp.where(mask, wsgu_ref[h], 0.0).astype(BF16)
        bias = jnp.sum(jnp.where(diag, jnp.broadcast_to(bsgu_ref[h:h + 1, :], (HEAD, HEAD)), 0.0), axis=1, keepdims=True)
        mx = _lanes_to_chunks(_dot(wm, ln_l) + bias)
        vhat.append(vh)
        rstd_v.append(rh)
        vln_l.append(ln_l)
        mixed.append(mx)
    mixed = jnp.concatenate(mixed, axis=1)
    yb = gu * mixed * silu_b
    return dict(xa=xa, ga=ga, u=u, v=v, gb=gb, cnt=cnt, pooled=pooled, pw=pw, sig_a=sig_a, ya=ya, phi_u=phi_u, pdf_u=pdf_u,
                phi_v=phi_v, pdf_v=pdf_v, gu=gu, sig_b=sig_b, silu_b=silu_b, vhat=vhat, rstd_v=rstd_v, vln_l=vln_l,
                mixed=mixed, yb=yb, mask=mask)


def _const(shape, *index):
    lead = tuple(index) + (0,) * (len(shape) - len(index))
    return pl.BlockSpec(shape, lambda *_: lead)


def _const_in(shape, *index):
    lead = tuple(index) + (0,) * (len(shape) - len(index))
    return pl.BlockSpec(shape, lambda *_: lead, pipeline_mode=pl.Buffered(1))


def _layer_weight_specs(l):
    return [
        _const_in((None, N_HEAD, HEAD, HEAD), l),
        _const_in((DEPTH, D_POOL)),
        _const_in((None, N_HEAD, HEAD), l),
        _const_in((None, N_HEAD, HEAD), l),
        _const_in((None, N_HEAD, HEAD, HEAD), l),
        _const_in((None, N_HEAD, HEAD), l),
    ]


def _forward_tile(l, i, x_ref, mod_ref, win_ref, wout_ref, small_refs, lng_ref, lnb_ref, carry_ref, saved_refs):
    wpool_ref, pscale_ref, sgu_g_ref, sgu_b_ref, wsgu_ref, bsgu_ref = small_refs
    proj_ref, y_ref, xn_ref, zn_ref, stats_ref, sig_ref, vhat_ref = saved_refs
    x = x_ref[...]
    if l > 0:
        x = x * lng_ref[l - 1:l, :] + lnb_ref[l - 1:l, :]
    shift, scale, gate = mod_ref[0:1, :], mod_ref[1:2, :], mod_ref[2:3, :]
    xn, rstd_x = _layer_norm(x)
    xn_ref[...] = xn.astype(xn_ref.dtype)
    h = xn * (1.0 + scale) + shift
    proj = _dot(h.astype(BF16), win_ref[...])
    proj_ref[...] = proj.astype(proj_ref.dtype)
    m = _mixer(proj, carry_ref[...], i * ROWS, wpool_ref, pscale_ref[l:l + 1, :], sgu_g_ref, sgu_b_ref, wsgu_ref, bsgu_ref)
    carry_ref[...] = m["xa"][ROWS - HALO:]
    sig_ref[...] = jnp.concatenate([m["sig_a"], m["sig_b"]], axis=1).astype(sig_ref.dtype)
    vhat_ref[...] = jnp.concatenate(m["vhat"], axis=1).astype(vhat_ref.dtype)
    cat = jnp.concatenate([m["ya"], m["yb"]], axis=1).astype(BF16)
    y = _dot(cat, wout_ref[...])
    y_ref[...] = y.astype(y_ref.dtype)
    zn, rstd_z = _layer_norm(DEEPNORM_ALPHA * x + gate * y)
    zn_ref[...] = zn
    stats_ref[...] = _pack_stats(rstd_x, rstd_z, m["rstd_v"])
    return zn


SAVED_COLS = (D_PROJ, D_MODEL, D_MODEL, D_MODEL, HEAD, D_MODEL, D_POOL)
SAVED_TYPES = (BF16, BF16, BF16, F32, F32, BF16, BF16)


def _saved_outputs():
    return ([jax.ShapeDtypeStruct((SEQ, cols), t) for cols, t in zip(SAVED_COLS, SAVED_TYPES)],
            [pl.BlockSpec((ROWS, cols), lambda i: (i, 0)) for cols in SAVED_COLS])


def _forward_last(zn_prev, mod, w_in, w_out, small, ln_g, ln_b, target):
    l = DEPTH - 1
    n_saved = len(SAVED_COLS)

    def body(*refs):
        x_ref, mod_ref, win_ref, wout_ref = refs[:4]
        small_refs, lng_ref, lnb_ref, tgt_ref = refs[4:10], refs[10], refs[11], refs[12]
        saved_refs = refs[13:13 + n_saved]
        dout_ref, loss_ref, carry_ref = refs[13 + n_saved:]
        i = pl.program_id(0)

        @pl.when(i == 0)
        def _():
            carry_ref[...] = jnp.zeros_like(carry_ref)
            loss_ref[...] = jnp.zeros_like(loss_ref)

        zn = _forward_tile(l, i, x_ref, mod_ref, win_ref, wout_ref, small_refs, lng_ref, lnb_ref, carry_ref, saved_refs)
        err = zn * lng_ref[l:l + 1, :] + lnb_ref[l:l + 1, :] - tgt_ref[...]
        dout_ref[...] = err * (1.0 / D_MODEL)
        loss_ref[...] += jnp.sum(err * err)

    tile = pl.BlockSpec((ROWS, D_MODEL), lambda i: (i, 0))
    tile3 = pl.BlockSpec((None, ROWS, D_MODEL), lambda i: (0, i, 0))
    in_specs = [tile, _const_in((None, 8, D_MODEL), l), _const_in((D_MODEL, D_PROJ)), _const_in((D_MODEL, D_MODEL))]
    in_specs += _layer_weight_specs(l) + [_const_in((DEPTH, D_MODEL)), _const_in((DEPTH, D_MODEL)), tile3]
    out_shape, out_specs = _saved_outputs()
    out_shape += [jax.ShapeDtypeStruct((SEQ, D_MODEL), F32), jax.ShapeDtypeStruct((8, HEAD), F32)]
    out_specs += [tile, _const((8, HEAD))]
    return pl.pallas_call(
        body, name="fwd_last", grid=(N_TILE,), in_specs=in_specs, out_specs=out_specs, out_shape=out_shape,
        scratch_shapes=[pltpu.VMEM((HALO, D_POOL), F32)],
        compiler_params=pltpu.CompilerParams(dimension_semantics=("arbitrary",), vmem_limit_bytes=VMEM_LIMIT),
    )(zn_prev, mod, w_in, w_out, *small, ln_g, ln_b, target)


def _backward_layer(l, dout, saved, mod, w_in, w_out, small, ln_g, sq=None, shared=None):
    has_loss = sq is not None

    def body(*refs):
        (dout_ref, proj_ref, y_ref, xn_ref, zn_ref, stats_ref, sig_ref, vhat_ref, halo_ref, mod_ref, win_ref, wout_ref,
         wpool_ref, pscale_ref, sgu_g_ref, sgu_b_ref, wsgu_ref, bsgu_ref, lng_ref) = refs[:19]
        n_in = 20 if has_loss else 19 + 6
        dx_ref, h_ref, cat_ref, dy_ref, dproj_ref, pack_ref, dmod_ref, carry_ref = refs[n_in:n_in + 8]
        i = pl.program_id(0)
        tile = N_TILE - 1 - i

        @pl.when(i == 0)
        def _():
            carry_ref[...] = jnp.zeros_like(carry_ref)
            pack_ref[...] = jnp.zeros_like(pack_ref)
            dmod_ref[...] = jnp.zeros_like(dmod_ref)
            if has_loss:
                dmod_ref[3:4, 0:HEAD] = refs[19][0:1, :]

        xn = xn_ref[...].astype(F32)
        zn = zn_ref[...]
        y = y_ref[...].astype(F32)
        dout = dout_ref[...]
        rstd_x, rstd_z, rstd_v = _unpack_stats(stats_ref[...])
        kept = dict(sig_a=sig_ref[:, :D_POOL].astype(F32), sig_b=sig_ref[:, D_POOL:].astype(F32), rstd_v=rstd_v,
                    vhat=[vhat_ref[:, hd * HEAD:(hd + 1) * HEAD].astype(F32) for hd in range(N_HEAD)])
        pscale = pscale_ref[l:l + 1, :]
        shift, scale, gate = mod_ref[0:1, :], mod_ref[1:2, :], mod_ref[2:3, :]
        h = xn * (1.0 + scale) + shift
        h_ref[...] = h.astype(BF16)
        g_ln_g = _sum_rows(dout * zn)
        g_ln_b = _sum_rows(dout)
        dz = _layer_norm_bwd(dout * lng_ref[l:l + 1, :], zn, rstd_z)
        d_gate = _sum_rows(dz * y)
        dy = (gate * dz).astype(BF16)
        dy_ref[...] = dy

        halo = jnp.where(tile > 0, halo_ref[...].astype(F32), 0.0)
        m = _mixer(proj_ref[...].astype(F32), halo, tile * ROWS, wpool_ref, pscale, sgu_g_ref, sgu_b_ref, wsgu_ref, bsgu_ref,
                   saved=kept)
        cat_ref[...] = jnp.concatenate([m["ya"], m["yb"]], axis=1).astype(BF16)
        dcat = _dot(dy, wout_ref[...], NT)
        dya = dcat[:, :D_POOL]
        dyb = dcat[:, D_POOL:]

        ga, sig_a = m["ga"], m["sig_a"]
        dp = dya * (ga * sig_a)
        d_ga = dya * (m["pw"] * pscale) * (sig_a * (1.0 + ga * (1.0 - sig_a)))
        g_pscale = _sum_rows(dp * m["pw"])
        dpw = (dp * pscale).astype(BF16)
        dpooled = []
        for g in range(N_HEAD):
            cols = slice(g * HEAD, (g + 1) * HEAD)
            pack_ref[PK_W_POOL + g * HEAD:PK_W_POOL + (g + 1) * HEAD, :] += _dot(m["pooled"][:, cols], dpw[:, cols], TN)
            dpooled.append(_dot(dpw[:, cols], wpool_ref[g].astype(BF16), NT))
        dpooled = jnp.concatenate(dpooled, axis=1)
        q = dpooled / m["cnt"]
        ext = jnp.concatenate([q, carry_ref[...]], axis=0)
        d_xa = _window_sums(ext, toward_later=True)[:ROWS] - dpooled
        carry_ref[...] = q[:HALO]

        gu, mixed, silu_b, gb, sig_b = m["gu"], m["mixed"], m["silu_b"], m["gb"], m["sig_b"]
        d_mixed = dyb * gu * silu_b
        d_gu = dyb * mixed * silu_b
        d_gb = dyb * gu * mixed * (sig_b * (1.0 + gb * (1.0 - sig_b)))
        d_u = d_gu * (m["phi_u"] + m["u"] * m["pdf_u"])
        ones = jnp.ones((8, HEAD), F32)
        d_v = []
        for hd in range(N_HEAD):
            cols = slice(hd * HEAD, (hd + 1) * HEAD)
            dm = d_mixed[:, cols]
            dm_l = _chunks_to_lanes(dm.astype(BF16))
            g_w = _dot(dm_l, m["vln_l"][hd], NT)
            pack_ref[PK_W_SGU + hd * HEAD:PK_W_SGU + (hd + 1) * HEAD, :] += jnp.where(m["mask"], g_w, 0.0)
            dm_sum = dm[0:HEAD]
            for n in range(1, ROWS // HEAD):
                dm_sum = dm_sum + dm[n * HEAD:(n + 1) * HEAD]
            pack_ref[PK_B_SGU + hd:PK_B_SGU + hd + 1, :] += _dot_exact(ones, dm_sum, NT)[0:1]
            wm = jnp.where(m["mask"], wsgu_ref[hd], 0.0).astype(BF16)
            d_vln = _lanes_to_chunks(_dot(wm, dm_l, TN))
            vhat = m["vhat"][hd]
            pack_ref[PK_SGU_LN_G + hd:PK_SGU_LN_G + hd + 1, :] += _sum_rows(d_vln * vhat)
            pack_ref[PK_SGU_LN_B + hd:PK_SGU_LN_B + hd + 1, :] += _sum_rows(d_vln)
            d_v.append(_layer_norm_bwd(d_vln * sgu_g_ref[hd:hd + 1, :], vhat, m["rstd_v"][hd]))
        v = m["v"]
        d_v = jnp.concatenate(d_v, axis=1) * (m["phi_v"] + v * m["pdf_v"])

        dproj = jnp.concatenate([d_xa, d_ga, d_u, d_v, d_gb], axis=1).astype(BF16)
        dproj_ref[...] = dproj
        dh = _dot(dproj, win_ref[...], NT)
        d_scale = _sum_rows(dh * xn)
        d_shift = _sum_rows(dh)
        dx_ref[...] = DEEPNORM_ALPHA * dz + _layer_norm_bwd(dh * (1.0 + scale), xn, rstd_x)

        dmod_ref[0:1, :] += d_shift
        dmod_ref[1:2, :] += d_scale
        dmod_ref[2:3, :] += d_gate
        for g in range(N_HEAD):
            pack_ref[PK_POOL_SCALE + g:PK_POOL_SCALE + g + 1, :] += g_pscale[:, g * HEAD:(g + 1) * HEAD]
        for k in range(D_MODEL // HEAD):
            pack_ref[PK_LN_G + k:PK_LN_G + k + 1, :] += g_ln_g[:, k * HEAD:(k + 1) * HEAD]
            pack_ref[PK_LN_B + k:PK_LN_B + k + 1, :] += g_ln_b[:, k * HEAD:(k + 1) * HEAD]

    def rev(i):
        return (N_TILE - 1 - i, 0)

    tile = pl.BlockSpec((ROWS, D_MODEL), rev)
    halo = pl.BlockSpec((HALO, D_POOL), lambda i: (jnp.maximum((N_TILE - 1 - i) * (ROWS // HALO) - 1, 0), 0))
    in_specs = [tile] + [pl.BlockSpec((ROWS, a.shape[1]), rev) for a in saved] + [halo]
    in_specs += [_const_in((None, 8, D_MODEL), l), _const_in((D_MODEL, D_PROJ)), _const_in((D_MODEL, D_MODEL))]
    in_specs += _layer_weight_specs(l) + [_const_in((DEPTH, D_MODEL))]
    args = [dout, *saved, saved[0], mod, w_in, w_out, *small, ln_g]
    stacked = lambda cols: pl.BlockSpec((None, ROWS, cols), lambda i: (l, N_TILE - 1 - i, 0))
    out_shape = [jax.ShapeDtypeStruct((SEQ, D_MODEL), F32), jax.ShapeDtypeStruct((DEPTH, SEQ, D_MODEL), BF16),
                 jax.ShapeDtypeStruct((DEPTH, SEQ, D_MODEL), BF16), jax.ShapeDtypeStruct((DEPTH, SEQ, D_MODEL), BF16),
                 jax.ShapeDtypeStruct((DEPTH, SEQ, D_PROJ), BF16), jax.ShapeDtypeStruct((DEPTH, PK_ROWS, HEAD), F32),
                 jax.ShapeDtypeStruct((DEPTH, 8, D_MODEL), F32)]
    out_specs = [tile, stacked(D_MODEL), stacked(D_MODEL), stacked(D_MODEL), stacked(D_PROJ),
                 _const((None, PK_ROWS, HEAD), l), _const((None, 8, D_MODEL), l)]
    aliases = {}
    if has_loss:
        in_specs.append(_const_in((8, HEAD)))
        args.append(sq)
    else:
        aliases = {len(args) + k: 1 + k for k in range(len(shared))}
        in_specs += [pl.BlockSpec(memory_space=pl.ANY)] * len(shared)
        args += list(shared)
    return pl.pallas_call(
        body, name="bwd_last" if has_loss else "bwd_first", grid=(N_TILE,), in_specs=in_specs, out_specs=out_specs,
        out_shape=out_shape, scratch_shapes=[pltpu.VMEM((HALO, D_POOL), F32)], input_output_aliases=aliases,
        compiler_params=pltpu.CompilerParams(dimension_semantics=("arbitrary",), vmem_limit_bytes=VMEM_LIMIT),
    )(*args)


def _flip(v, f):
    return v + f - 2 * v * f


class _Place:
    def __init__(self):
        x, y, c = lax.axis_index("x"), lax.axis_index("y"), lax.axis_index("c")
        self.x, self.y, self.c = x, y, c
        self.chip = 2 * x + y
        self.dev = 4 * x + 2 * y + c
        self.sibling = (x, y, 1 - c)
        x1, y1 = _flip(x, 1 - c), _flip(y, c)
        x2, y2 = _flip(x, c), _flip(y, 1 - c)
        self.first = (x1, y1, c)
        self.second = (x2, y2, c)
        self.chip_first = 2 * x1 + y1
        self.chip_second = 2 * x2 + y2
        self.chip_far = 2 * (1 - x) + (1 - y)
        self.my_first_coord = jnp.where(c == 0, x, y)

    def first_coord(self, ch):
        return jnp.where(self.c == 0, ch // 2, ch % 2)

    def others(self):
        return [(_flip(self.x, (r >> 2) & 1), _flip(self.y, (r >> 1) & 1), _flip(self.c, r & 1)) for r in range(1, N_DEV)]

    def other_chips(self):
        return [(1 - self.x, self.y), (self.x, 1 - self.y), (1 - self.x, 1 - self.y)]


class _WeightGather:
    CHUNKS = 2
    N_SEMS = 12 * CHUNKS

    def __init__(self, place, win, wout, send, recv):
        self.p, self.win, self.wout, self.send, self.recv = place, win, wout, send, recv
        p = place
        self.plan = [(p.chip, p.first), (p.chip, p.second), (p.chip_first, p.second),
                     (p.chip_first, p.sibling), (p.chip_second, p.sibling), (p.chip_far, p.sibling)]

    def _copies(self, k, q):
        ch, target = self.plan[k]
        n_in, n_out = HALF_IN // self.CHUNKS, HALF_OUT // self.CHUNKS
        rows_in = pl.ds(pl.multiple_of(self.p.c * HALF_IN + q * n_in, n_in), n_in)
        cols_in = pl.ds(pl.multiple_of(ch * W_IN_COLS, 128), W_IN_COLS)
        rows_out = pl.ds(pl.multiple_of(ch * W_OUT_ROWS + self.p.c * HALF_OUT + q * n_out, n_out), n_out)
        r_in = self.win.at[rows_in, cols_in]
        r_out = self.wout.at[rows_out, :]
        s = 2 * (6 * q + k)
        return [pltpu.make_async_remote_copy(r_in, r_in, self.send.at[s], self.recv.at[s],
                                             device_id=target, device_id_type=MESH),
                pltpu.make_async_remote_copy(r_out, r_out, self.send.at[s + 1], self.recv.at[s + 1],
                                             device_id=target, device_id_type=MESH)]

    def _start(self, k, q):
        for cp in self._copies(k, q):
            cp.start()

    def _landed(self, k, q):
        for cp in self._copies(k, q):
            cp.wait_recv()

    def start_first_round(self):
        for q in range(self.CHUNKS):
            self._start(0, q)
        for q in range(self.CHUNKS):
            self._start(1, q)

    def start_second_round(self, q):
        self._landed(0, q)
        self._start(2, q)
        self._start(3, q)

    def pass_second_round(self, q):
        self._landed(1, q)
        self._start(4, q)
        self._landed(2, q)
        self._start(5, q)

    def finish(self):
        for q in range(self.CHUNKS):
            for k in (3, 4, 5):
                self._landed(k, q)
        for q in range(self.CHUNKS):
            for k in range(len(self.plan)):
                for cp in self._copies(k, q):
                    cp.wait_send()


def _forward_first(x, c_vec, w_ada, b_ada, w_in, w_out, small, ln_g, ln_b):
    n_saved = len(SAVED_COLS)

    def body(*refs):
        x_ref, c_ref, wada_hbm, bada_ref, win_hbm, wout_hbm = refs[:6]
        small_refs, lng_ref, lnb_ref = refs[6:12], refs[12], refs[13]
        saved_refs = refs[14:14 + n_saved]
        win0, wout0, win1, wout1, mod_out, c_out = refs[14 + n_saved:20 + n_saved]
        (carry_ref, wada_ref, win_ref, wout_ref, win_bf, wout_bf, mod_mine, mod_all, c_all, mod_ref, win_v, wout_v,
         g0_send, g0_recv, g1_send, g1_recv, c_send, c_recv, mod_send, mod_recv, local_sem) = refs[20 + n_saved:]
        i = pl.program_id(0)
        p = _Place()
        gather0 = _WeightGather(p, win_v, wout_v, g0_send, g0_recv)
        keep = [pltpu.make_async_copy(win_v, win0, local_sem.at[9]), pltpu.make_async_copy(wout_v, wout0, local_sem.at[10])]
        gather1 = _WeightGather(p, win1, wout1, g1_send, g1_recv)

        @pl.when(i == 0)
        def _():
            carry_ref[...] = jnp.zeros_like(carry_ref)
            loads = [pltpu.make_async_copy(win_hbm.at[0], win_ref.at[0], local_sem.at[4]),
                     pltpu.make_async_copy(wout_hbm.at[0], wout_ref.at[0], local_sem.at[5]),
                     pltpu.make_async_copy(win_hbm.at[1], win_ref.at[1], local_sem.at[6]),
                     pltpu.make_async_copy(wout_hbm.at[1], wout_ref.at[1], local_sem.at[7]),
                     pltpu.make_async_copy(wada_hbm, wada_ref, local_sem.at[8])]
            for cp in loads:
                cp.start()

            c_all[pl.ds(p.dev, 1), :] = c_ref[...]
            mine = c_all.at[pl.ds(p.dev, 1), :]
            c_copies = [pltpu.make_async_remote_copy(mine, mine, c_send.at[r], c_recv.at[r], device_id=d, device_id_type=MESH)
                        for r, d in enumerate(p.others())]
            for cp in c_copies:
                cp.start()

            cols = pl.ds(pl.multiple_of(p.chip * W_IN_COLS, 128), W_IN_COLS)
            rows = pl.ds(pl.multiple_of(p.chip * W_OUT_ROWS, W_OUT_ROWS), W_OUT_ROWS)
            own = [pltpu.make_async_copy(win_bf.at[0], win_v.at[:, cols], local_sem.at[0]),
                   pltpu.make_async_copy(wout_bf.at[0], wout_v.at[rows, :], local_sem.at[1]),
                   pltpu.make_async_copy(win_bf.at[1], win1.at[:, cols], local_sem.at[2]),
                   pltpu.make_async_copy(wout_bf.at[1], wout1.at[rows, :], local_sem.at[3])]
            for l in range(DEPTH):
                loads[2 * l].wait()
                win_bf[l] = win_ref[l].astype(BF16)
                own[2 * l].start()
                loads[2 * l + 1].wait()
                wout_bf[l] = wout_ref[l].astype(BF16)
                own[2 * l + 1].start()
                if l == 0:
                    own[0].wait()
                    own[1].wait()
                    gather0.start_first_round()
            for cp in c_copies:
                cp.wait()
            loads[4].wait()

            cv = c_all[...]
            c_out[...] = cv
            silu_c = (cv * _sigmoid(cv)).astype(BF16)
            for l in range(DEPTH):
                mod_mine[l] = _dot(silu_c, wada_ref[l].astype(BF16))
            mod_all[p.chip] = mod_mine[...]
            m_copies = [pltpu.make_async_remote_copy(mod_mine, mod_all.at[p.chip], mod_send.at[k], mod_recv.at[k],
                                                     device_id=(px, py, p.c), device_id_type=MESH)
                        for k, (px, py) in enumerate(p.other_chips())]
            for cp in m_copies:
                cp.start()
            for q in range(gather0.CHUNKS):
                gather0.start_second_round(q)
            own[2].wait()
            own[3].wait()
            gather1.start_first_round()
            for cp in m_copies:
                cp.wait()
            mod_ref[...] = jnp.zeros_like(mod_ref)
            for l in range(DEPTH):
                full = jnp.concatenate([mod_all[ch, l, pl.ds(p.dev, 1), :] for ch in range(N_CHIP)], axis=1) + bada_ref[l:l + 1, :]
                for k in range(3):
                    mod_ref[l, k:k + 1, :] = full[:, k * D_MODEL:(k + 1) * D_MODEL]
            mod_out[...] = mod_ref[...]
            for q in range(gather0.CHUNKS):
                gather0.pass_second_round(q)
            gather0.finish()
            for cp in keep:
                cp.start()

        for q in range(_WeightGather.CHUNKS):
            @pl.when(i == GATHER_SECOND_ROUND_STEP + q)
            def _(q=q):
                gather1.start_second_round(q)

            @pl.when(i == GATHER_PASS_STEP)
            def _(q=q):
                gather1.pass_second_round(q)

        _forward_tile(0, i, x_ref, mod_ref.at[0], win_v, wout_v, small_refs, lng_ref, lnb_ref, carry_ref, saved_refs)

        @pl.when(i == N_TILE - 1)
        def _():
            gather1.finish()
            for cp in keep:
                cp.wait()

    hbm = pl.BlockSpec(memory_space=pl.ANY)
    tile3 = pl.BlockSpec((None, ROWS, D_MODEL), lambda i: (0, i, 0))
    in_specs = [tile3, _const_in((1, D_MODEL)), hbm, _const_in((DEPTH, 3 * D_MODEL)), hbm, hbm]
    in_specs += _layer_weight_specs(0) + [_const_in((DEPTH, D_MODEL)), _const_in((DEPTH, D_MODEL))]
    out_shape, out_specs = _saved_outputs()
    w_in_shape = jax.ShapeDtypeStruct((D_MODEL, D_PROJ), BF16)
    w_out_shape = jax.ShapeDtypeStruct((D_MODEL, D_MODEL), BF16)
    out_shape += [w_in_shape, w_out_shape, w_in_shape, w_out_shape,
                  jax.ShapeDtypeStruct((DEPTH, 8, D_MODEL), F32), jax.ShapeDtypeStruct((N_DEV, D_MODEL), F32)]
    out_specs += [hbm, hbm, hbm, hbm, _const((DEPTH, 8, D_MODEL)), _const((N_DEV, D_MODEL))]
    gather_sems = [pltpu.SemaphoreType.DMA((_WeightGather.N_SEMS,))] * 4
    scratch = [
        pltpu.VMEM((HALO, D_POOL), F32),
        pltpu.VMEM(w_ada.shape, F32), pltpu.VMEM(w_in.shape, F32), pltpu.VMEM(w_out.shape, F32),
        pltpu.VMEM((DEPTH, D_MODEL, W_IN_COLS), BF16), pltpu.VMEM((DEPTH, W_OUT_ROWS, D_MODEL), BF16),
        pltpu.VMEM((DEPTH, N_DEV, W_ADA_COLS), F32), pltpu.VMEM((N_CHIP, DEPTH, N_DEV, W_ADA_COLS), F32),
        pltpu.VMEM((N_DEV, D_MODEL), F32), pltpu.VMEM((DEPTH, 8, D_MODEL), F32),
        pltpu.VMEM((D_MODEL, D_PROJ), BF16), pltpu.VMEM((D_MODEL, D_MODEL), BF16),
    ] + gather_sems + [
        pltpu.SemaphoreType.DMA((7,)), pltpu.SemaphoreType.DMA((7,)),
        pltpu.SemaphoreType.DMA((3,)), pltpu.SemaphoreType.DMA((3,)),
        pltpu.SemaphoreType.DMA((11,)),
    ]
    return pl.pallas_call(
        body, name="fwd_first", grid=(N_TILE,), in_specs=in_specs, out_specs=out_specs, out_shape=out_shape,
        scratch_shapes=scratch,
        compiler_params=pltpu.CompilerParams(dimension_semantics=("arbitrary",), vmem_limit_bytes=VMEM_LIMIT),
    )(x, c_vec, w_ada, b_ada, w_in, w_out, *small, ln_g, ln_b)


IN_STEPS = W_IN_COLS // HEAD
OUT_STEPS = 4
OUT_COLS = D_MODEL // OUT_STEPS
OUT_FIRST = 2
ITEMS = ([("out", k) for k in range(OUT_FIRST)] + [("in", k) for k in range(IN_STEPS)]
         + [("out", k) for k in range(OUT_FIRST, OUT_STEPS)])
N_ITEMS = len(ITEMS)
N_STEPS = DEPTH * N_ITEMS
DELAY_SUM, DELAY_SECOND, DELAY_FINAL = 1, 3, 5
SMALL_SCATTER_STEP, SMALL_GATHER_STEP, SMALL_PASS_STEP, SMALL_FINISH_STEP = 1, 3, 5, 7


def _wgrad_reduce(h, dproj, cat, dy, pack, dmod):
    def body(*refs):
        h_ref, dp_refs, cat_ref, dy_ref, pack_ref, dmod_ref = refs[0], refs[1:5], refs[5], refs[6], refs[7], refs[8]
        fin_in, fin_out, pack_out, dmod_out = refs[9:13]
        scratch = refs[13:]
        (mine_in, send_in, sib_in, st_in, r1_in, r2_in, f_in,
         mine_out, send_out, sib_out, st_out, r1_out, r2_out, f_out,
         d2d_s, d2d_r, r1_s, r1_r, r2_s, r2_r, fin_l, fin_s, fin_r) = scratch[:23]
        p = _Place()
        c = p.c
        i = pl.program_id(0)
        my_rows = pl.ds(pl.multiple_of(c * HALF_IN, HALF_IN), HALF_IN)

        def layer_of(j):
            return DEPTH - 1 - j // N_ITEMS

        def bufs(j):
            kind, k = ITEMS[j % N_ITEMS]
            if kind == "in":
                return [r.at[k] for r in (mine_in, send_in, sib_in, st_in, r1_in, r2_in, f_in)]
            return [r.at[k] for r in (mine_out, send_out, sib_out, st_out, r1_out, r2_out, f_out)]

        def piece(j, ref, ch):
            if ITEMS[j % N_ITEMS][0] == "in":
                return ref.at[:, ch * HEAD:(ch + 1) * HEAD]
            return ref.at[ch]

        def slot(ch):
            return jnp.where(c == 0, ch % 2, ch // 2)

        def to_sibling(j):
            _, send, sib, _, _, _, _ = bufs(j)
            return pltpu.make_async_remote_copy(send, sib, d2d_s.at[j], d2d_r.at[j], device_id=p.sibling, device_id_type=MESH)

        def first_round(j, ch):
            _, _, _, st, r1, _, _ = bufs(j)
            k = slot(ch)
            return pltpu.make_async_remote_copy(st.at[k], r1.at[k], r1_s.at[2 * j + k], r1_r.at[2 * j + k],
                                                device_id=p.first, device_id_type=MESH)

        def second_round(j):
            _, _, _, st, _, r2, _ = bufs(j)
            return pltpu.make_async_remote_copy(st.at[2], r2, r2_s.at[j], r2_r.at[j], device_id=p.second, device_id_type=MESH)

        def finals(j):
            f = bufs(j)[6]
            kind, k = ITEMS[j % N_ITEMS]
            if kind == "in":
                dst = fin_in.at[layer_of(j), my_rows, k * HEAD:(k + 1) * HEAD]
            else:
                dst = fin_out.at[layer_of(j), c, :, k * OUT_COLS:(k + 1) * OUT_COLS]
            return [pltpu.make_async_copy(f, dst, fin_l.at[j]),
                    pltpu.make_async_remote_copy(f, dst, fin_s.at[j], fin_r.at[j], device_id=p.sibling, device_id_type=MESH)]

        def stage_sum(j):
            mine, _, sib, st, _, _, _ = bufs(j)
            to_sibling(j).wait_recv()
            mine[...] = mine[...] + sib[...]
            for ch in range(N_CHIP):
                @pl.when(p.first_coord(ch) != p.my_first_coord)
                def _(ch=ch):
                    st[slot(ch)] = piece(j, mine, ch)[...].astype(BF16)
                    first_round(j, ch).start()

        def stage_second(j):
            mine, _, _, st, r1, _, _ = bufs(j)
            for ch in range(N_CHIP):
                @pl.when(p.first_coord(ch) == p.my_first_coord)
                def _(ch=ch):
                    first_round(j, ch).wait_recv()
                    part = piece(j, mine, ch)
                    total = part[...] + r1[slot(ch)].astype(F32)
                    part[...] = total

                    @pl.when(ch != p.chip)
                    def _():
                        st[2] = total.astype(BF16)
                        second_round(j).start()

        def stage_final(j):
            mine, _, _, _, _, r2, f = bufs(j)
            second_round(j).wait_recv()
            for ch in range(N_CHIP):
                @pl.when(ch == p.chip)
                def _(ch=ch):
                    f[...] = piece(j, mine, ch)[...] + r2[...].astype(F32)
            for cp in finals(j):
                cp.start()

        def drain(j):
            to_sibling(j).wait_send()
            for ch in range(N_CHIP):
                @pl.when(p.first_coord(ch) != p.my_first_coord)
                def _(ch=ch):
                    first_round(j, ch).wait_send()

                @pl.when(jnp.logical_and(p.first_coord(ch) == p.my_first_coord, ch != p.chip))
                def _():
                    second_round(j).wait_send()
            for cp in finals(j):
                cp.wait()

        dev = p.dev
        devices = p.others()

        def half(core):
            return pl.ds(pl.multiple_of(core * PK_HALF, 16), PK_HALF)

        def finished(core, ch):
            return pl.ds(pl.multiple_of(core * PK_HALF + ch * PK_PIECE, 16), PK_PIECE)

        def small_exchange(l, first_step, bufs_l):
            (pk_mine, pk_sib, pk_st, pk_rs, pk_fin, pk_all, dm_st, dm_all, pk_sem, rs_s, rs_r, ag_s, ag_r, dm_s, dm_r) = bufs_l

            def pk_load():
                return pltpu.make_async_copy(pack_ref.at[l, half(c)], pk_mine, pk_sem.at[0])

            def pk_give():
                return pltpu.make_async_remote_copy(pack_ref.at[l, half(1 - c)], pk_sib, pk_sem.at[1], pk_sem.at[2],
                                                    device_id=p.sibling, device_id_type=MESH)

            def pk_scatter(ch):
                return pltpu.make_async_remote_copy(pk_st.at[ch * PK_PIECE:(ch + 1) * PK_PIECE], pk_rs.at[p.chip],
                                                    rs_s.at[ch], rs_r.at[p.chip], device_id=(ch // 2, ch % 2, c),
                                                    device_id_type=MESH)

            def pk_spread(ch):
                return pltpu.make_async_remote_copy(pk_fin, pk_all.at[finished(c, p.chip)], ag_s.at[ch], ag_r.at[p.chip],
                                                    device_id=(ch // 2, ch % 2, c), device_id_type=MESH)

            def pk_pass():
                return pltpu.make_async_remote_copy(pk_all.at[half(c)], pk_all.at[half(c)], pk_sem.at[3], pk_sem.at[4],
                                                    device_id=p.sibling, device_id_type=MESH)

            def dm_copy(r):
                return pltpu.make_async_remote_copy(dm_st, dm_all.at[:, pl.ds(dev, 1), :], dm_s.at[r], dm_r.at[r],
                                                    device_id=devices[r], device_id_type=MESH)

            def results():
                return [pltpu.make_async_copy(pk_all, pack_out.at[l], pk_sem.at[0]),
                        pltpu.make_async_copy(dm_all, dmod_out.at[l], pk_sem.at[5])]

            @pl.when(i == first_step)
            def _():
                pk_load().start()
                pk_give().start()
                for k in range(3):
                    for r in range(D_MODEL // HEAD):
                        dm_st[8 * k + r] = dmod_ref[l, k:k + 1, r * HEAD:(r + 1) * HEAD]
                dm_st[DM_LOSS] = dmod_ref[l, 3:4, 0:HEAD]
                dm_all[:, pl.ds(dev, 1), :] = dm_st[...]
                for r in range(N_DEV - 1):
                    dm_copy(r).start()

            @pl.when(i == first_step + SMALL_SCATTER_STEP)
            def _():
                pk_load().wait()
                pk_give().wait()
                total = pk_mine[...] + pk_sib[...]
                pk_mine[...] = total
                pk_st[...] = total.astype(BF16)
                for ch in range(N_CHIP):
                    @pl.when(ch != p.chip)
                    def _(ch=ch):
                        pk_scatter(ch).start()

            @pl.when(i == first_step + SMALL_GATHER_STEP)
            def _():
                for ch in range(N_CHIP):
                    @pl.when(ch != p.chip)
                    def _(ch=ch):
                        pltpu.make_async_remote_copy(pk_fin, pk_rs.at[ch], rs_s.at[ch], rs_r.at[ch],
                                                     device_id=p.sibling, device_id_type=MESH).wait_recv()
                for me in range(N_CHIP):
                    @pl.when(me == p.chip)
                    def _(me=me):
                        total = None
                        for ch in range(N_CHIP):
                            part = pk_mine[me * PK_PIECE:(me + 1) * PK_PIECE] if ch == me else pk_rs[ch].astype(F32)
                            total = part if total is None else total + part
                        pk_fin[...] = total.astype(BF16)
                        pk_all[finished(c, me)] = total.astype(BF16)
                for ch in range(N_CHIP):
                    @pl.when(ch != p.chip)
                    def _(ch=ch):
                        pk_spread(ch).start()

            @pl.when(i == first_step + SMALL_PASS_STEP)
            def _():
                for ch in range(N_CHIP):
                    @pl.when(ch != p.chip)
                    def _(ch=ch):
                        pltpu.make_async_remote_copy(pk_fin, pk_all.at[finished(c, ch)], ag_s.at[ch], ag_r.at[ch],
                                                     device_id=p.sibling, device_id_type=MESH).wait_recv()
                pk_pass().start()

            @pl.when(i == first_step + SMALL_FINISH_STEP)
            def _():
                pk_pass().wait()
                for ch in range(N_CHIP):
                    @pl.when(ch != p.chip)
                    def _(ch=ch):
                        pk_scatter(ch).wait_send()
                        pk_spread(ch).wait_send()
                for r in range(N_DEV - 1):
                    dm_copy(r).wait()
                for cp in results():
                    cp.start()
                for cp in results():
                    cp.wait()

        n_small = 15
        for l in range(DEPTH):
            small_exchange(l, (DEPTH - 1 - l) * N_ITEMS, scratch[23 + n_small * l:23 + n_small * (l + 1)])

        for step in range(N_ITEMS, N_STEPS):
            @pl.when(i == step)
            def _(step=step):
                drain(step - N_ITEMS)

        ii = jnp.where(i < N_ITEMS, i, i - N_ITEMS)
        in_step = jnp.logical_and(ii >= OUT_FIRST, ii < OUT_FIRST + IN_STEPS)

        @pl.when(in_step)
        def _():
            k = ii - OUT_FIRST
            rhs = jnp.concatenate([r[...] for r in dp_refs], axis=1)
            res = _dot(h_ref[...], rhs, TN)

            @pl.when(c == 0)
            def _():
                mine_in[k] = res[:HALF_IN]
                send_in[k] = res[HALF_IN:]

            @pl.when(c == 1)
            def _():
                mine_in[k] = res[HALF_IN:]
                send_in[k] = res[:HALF_IN]

        @pl.when(jnp.logical_not(in_step))
        def _():
            k = jnp.where(ii < OUT_FIRST, ii, ii - IN_STEPS)
            res = _dot(cat_ref[...], dy_ref[...], TN)

            @pl.when(c == 0)
            def _():
                for ch in range(N_CHIP):
                    mine_out[k, ch] = res[ch * W_OUT_ROWS:ch * W_OUT_ROWS + HALF_OUT]
                    send_out[k, ch] = res[ch * W_OUT_ROWS + HALF_OUT:(ch + 1) * W_OUT_ROWS]

            @pl.when(c == 1)
            def _():
                for ch in range(N_CHIP):
                    mine_out[k, ch] = res[ch * W_OUT_ROWS + HALF_OUT:(ch + 1) * W_OUT_ROWS]
                    send_out[k, ch] = res[ch * W_OUT_ROWS:ch * W_OUT_ROWS + HALF_OUT]

        stages = ((0, lambda j: to_sibling(j).start()), (DELAY_SUM, stage_sum), (DELAY_SECOND, stage_second),
                  (DELAY_FINAL, stage_final))
        for step in range(N_STEPS):
            @pl.when(i == step)
            def _(step=step):
                for delay, stage in stages:
                    if step - delay >= 0:
                        stage(step - delay)

        @pl.when(i == N_STEPS - 1)
        def _():
            for step in range(N_STEPS, N_STEPS + DELAY_FINAL):
                for delay, stage in stages:
                    if 0 <= step - delay < N_STEPS:
                        stage(step - delay)
            for j in range(N_STEPS - N_ITEMS, N_STEPS):
                drain(j)

    hbm = pl.BlockSpec(memory_space=pl.ANY)

    def layer(i):
        return jnp.where(i < N_ITEMS, DEPTH - 1, 0)

    def item(i):
        return jnp.where(i < N_ITEMS, i, i - N_ITEMS)

    def whole(i):
        return (layer(i), 0, 0)

    def dproj_piece(ch):
        return pl.BlockSpec((None, SEQ, HEAD),
                            lambda i: (layer(i), 0, ch * IN_STEPS + jnp.clip(item(i) - OUT_FIRST, 0, IN_STEPS - 1)))

    def dy_quarter(i):
        return (layer(i), 0, jnp.where(item(i) < OUT_FIRST, item(i), jnp.maximum(item(i) - IN_STEPS, OUT_FIRST)))

    operand = pl.BlockSpec((None, SEQ, D_MODEL), whole)
    in_specs = [operand] + [dproj_piece(ch) for ch in range(N_CHIP)]
    in_specs += [operand, pl.BlockSpec((None, SEQ, OUT_COLS), dy_quarter), hbm, _const_in((DEPTH, 8, D_MODEL))]
    args = [h, dproj, dproj, dproj, dproj, cat, dy, pack, dmod]
    out_shape = [jax.ShapeDtypeStruct((DEPTH, D_MODEL, W_IN_COLS), F32), jax.ShapeDtypeStruct((DEPTH, 2, HALF_OUT, D_MODEL), F32),
                 jax.ShapeDtypeStruct((DEPTH, PK_ROWS, HEAD), BF16), jax.ShapeDtypeStruct((DEPTH, DM_ROWS, N_DEV, HEAD), F32)]
    out_specs = [hbm, hbm, hbm, hbm]
    in_item = lambda *lead: pltpu.VMEM(lead + (HALF_IN, HEAD), BF16)
    out_item = lambda *lead: pltpu.VMEM(lead + (HALF_OUT, OUT_COLS), BF16)
    scratch = [
        pltpu.VMEM((IN_STEPS, HALF_IN, N_CHIP * HEAD), F32), pltpu.VMEM((IN_STEPS, HALF_IN, N_CHIP * HEAD), F32),
        pltpu.VMEM((IN_STEPS, HALF_IN, N_CHIP * HEAD), F32), in_item(IN_STEPS, 3), in_item(IN_STEPS, 2), in_item(IN_STEPS),
        pltpu.VMEM((IN_STEPS, HALF_IN, HEAD), F32),
        pltpu.VMEM((OUT_STEPS, N_CHIP, HALF_OUT, OUT_COLS), F32), pltpu.VMEM((OUT_STEPS, N_CHIP, HALF_OUT, OUT_COLS), F32),
        pltpu.VMEM((OUT_STEPS, N_CHIP, HALF_OUT, OUT_COLS), F32), out_item(OUT_STEPS, 3), out_item(OUT_STEPS, 2),
        out_item(OUT_STEPS), pltpu.VMEM((OUT_STEPS, HALF_OUT, OUT_COLS), F32),
        pltpu.SemaphoreType.DMA((N_STEPS,)), pltpu.SemaphoreType.DMA((N_STEPS,)),
        pltpu.SemaphoreType.DMA((2 * N_STEPS,)), pltpu.SemaphoreType.DMA((2 * N_STEPS,)),
        pltpu.SemaphoreType.DMA((N_STEPS,)), pltpu.SemaphoreType.DMA((N_STEPS,)),
        pltpu.SemaphoreType.DMA((N_STEPS,)), pltpu.SemaphoreType.DMA((N_STEPS,)), pltpu.SemaphoreType.DMA((N_STEPS,)),
    ]
    for _ in range(DEPTH):
        scratch += [
            pltpu.VMEM((PK_HALF, HEAD), F32), pltpu.VMEM((PK_HALF, HEAD), F32), pltpu.VMEM((PK_HALF, HEAD), BF16),
            pltpu.VMEM((N_CHIP, PK_PIECE, HEAD), BF16), pltpu.VMEM((PK_PIECE, HEAD), BF16), pltpu.VMEM((PK_ROWS, HEAD), BF16),
            pltpu.VMEM((DM_ROWS, 1, HEAD), F32), pltpu.VMEM((DM_ROWS, N_DEV, HEAD), F32),
            pltpu.SemaphoreType.DMA((6,)),
            pltpu.SemaphoreType.DMA((N_CHIP,)), pltpu.SemaphoreType.DMA((N_CHIP,)),
            pltpu.SemaphoreType.DMA((N_CHIP,)), pltpu.SemaphoreType.DMA((N_CHIP,)),
            pltpu.SemaphoreType.DMA((N_DEV - 1,)), pltpu.SemaphoreType.DMA((N_DEV - 1,)),
        ]
    return pl.pallas_call(
        body, name="wgrad", grid=(N_STEPS,), in_specs=in_specs, out_specs=out_specs, out_shape=out_shape,
        scratch_shapes=scratch,
        compiler_params=pltpu.CompilerParams(dimension_semantics=("arbitrary",), vmem_limit_bytes=VMEM_LIMIT),
    )(*args)


def _adamw(w, g, m, v):
    m = ADAM_B1 * m + (1.0 - ADAM_B1) * g
    v = ADAM_B2 * v + (1.0 - ADAM_B2) * (g * g)
    m_hat = m / (1.0 - ADAM_B1 ** ADAM_STEP)
    v_hat = v / (1.0 - ADAM_B2 ** ADAM_STEP)
    delta = -ADAM_LR * (m_hat / (jnp.sqrt(v_hat) + ADAM_EPS) + ADAM_WD * w)
    return delta, m, v


def _adam_sharded(c_all, dmods, ada, w_in_set, w_out_set):
    rows = D_MODEL // ADAM_PARTS

    def body(c_ref, dm_ref, wa_ref, ma_ref, va_ref, wi_ref, gi_ref, mi_ref, vi_ref, wo_ref, go_ref, mo_ref, vo_ref,
             ga_out, da_out, ma_out, va_out, gi_out, di_out, mi_out, vi_out, go_out, do_out, mo_out, vo_out):
        l = pl.program_id(0)
        chip = 2 * lax.axis_index("x") + lax.axis_index("y")
        cv = c_ref[...]
        silu_c = (cv * _sigmoid(cv)).astype(BF16).astype(F32)
        pieces = []
        for k in range(W_ADA_COLS // HEAD):
            dk = dm_ref[l, (W_ADA_COLS // HEAD) * chip + k].astype(BF16).astype(F32)
            pieces.append(_dot_exact(silu_c, dk, TN))
        g = jnp.concatenate(pieces, axis=1)
        ga_out[...] = g
        da_out[...], ma_out[...], va_out[...] = _adamw(wa_ref[...], g, ma_ref[...], va_ref[...])
        gi_out[...] = gi_ref[...]
        go_out[...] = go_ref[...]
        di_out[...], mi_out[...], vi_out[...] = _adamw(wi_ref[...], gi_ref[...], mi_ref[...], vi_ref[...])
        do_out[...], mo_out[...], vo_out[...] = _adamw(wo_ref[...], go_ref[...], mo_ref[...], vo_ref[...])

    def blk(r, cols):
        return pl.BlockSpec((None, r, cols), lambda l, i: (l, i, 0))

    b_ada, b_in, b_out = blk(rows, W_ADA_COLS), blk(rows, W_IN_COLS), blk(W_OUT_ROWS // ADAM_PARTS, D_MODEL)
    shapes = [jax.ShapeDtypeStruct(a[0].shape, F32) for a in (ada, w_in_set, w_out_set)]
    return pl.pallas_call(
        body, name="adam_sharded", grid=(DEPTH, ADAM_PARTS),
        in_specs=[pl.BlockSpec((N_DEV, rows), lambda l, i: (0, i)), _const_in((DEPTH, DM_ROWS, N_DEV, HEAD))]
        + [b_ada] * 3 + [b_in] * 4 + [b_out] * 4,
        out_specs=[b_ada] * 4 + [b_in] * 4 + [b_out] * 4,
        out_shape=[shapes[0]] * 4 + [shapes[1]] * 4 + [shapes[2]] * 4,
        compiler_params=pltpu.CompilerParams(dimension_semantics=("arbitrary", "arbitrary"), vmem_limit_bytes=VMEM_LIMIT),
    )(c_all, dmods, *ada, *w_in_set, *w_out_set)


def _adam_small(packs, dmods, weights, ms, vs):
    n = len(weights)

    def body(*refs):
        dm_refs = refs[1]
        b = 2
        w_refs, m_refs, v_refs = refs[b:b + n], refs[b + n:b + 2 * n], refs[b + 2 * n:b + 3 * n]
        outs = refs[b + 3 * n:b + 3 * n + 4 * n + 1]
        pack_refs = refs[-1]
        pack_refs[...] = refs[0][...].astype(F32)
        g_refs, d_refs, nm_refs, nv_refs = outs[0:n], outs[n:2 * n], outs[2 * n:3 * n], outs[3 * n:4 * n]
        squares = dm_refs[DEPTH - 1, DM_LOSS]
        total = squares[0:1, 0:1]
        for d in range(1, N_DEV):
            total = total + squares[d:d + 1, 0:1]
        outs[4 * n][...] = total * (0.5 / D_MODEL)

        def lanes(l, row0, count):
            return jnp.concatenate([pack_refs.at[l][row0 + k:row0 + k + 1, :] for k in range(count)], axis=1)

        def update(idx, at, g):
            g_refs[idx][at] = g
            d_refs[idx][at], nm_refs[idx][at], nv_refs[idx][at] = _adamw(w_refs[idx][at], g, m_refs[idx][at], v_refs[idx][at])

        for l in range(DEPTH):
            row = (slice(l, l + 1), slice(None))
            g_b = None
            for d in range(N_DEV):
                part = dm_refs.at[l][0:DM_LOSS, d, :]
                g_b = part if g_b is None else g_b + part
            update(0, row, jnp.concatenate([g_b[k:k + 1, :] for k in range(DM_LOSS)], axis=1))
            for g in range(N_HEAD):
                update(1, (l, g), pack_refs.at[l][PK_W_POOL + g * HEAD:PK_W_POOL + (g + 1) * HEAD, :])
                update(5, (l, g), pack_refs.at[l][PK_W_SGU + g * HEAD:PK_W_SGU + (g + 1) * HEAD, :])
            update(2, row, lanes(l, PK_POOL_SCALE, N_HEAD))
            update(3, (l,), pack_refs.at[l][PK_SGU_LN_G:PK_SGU_LN_G + N_HEAD, :])
            update(4, (l,), pack_refs.at[l][PK_SGU_LN_B:PK_SGU_LN_B + N_HEAD, :])
            update(6, (l,), pack_refs.at[l][PK_B_SGU:PK_B_SGU + N_HEAD, :])
            update(7, row, lanes(l, PK_LN_G, D_MODEL // HEAD))
            update(8, row, lanes(l, PK_LN_B, D_MODEL // HEAD))

    vmem = pl.BlockSpec(memory_space=pltpu.VMEM)
    shapes = [jax.ShapeDtypeStruct(w.shape, F32) for w in weights]
    return pl.pallas_call(
        body, name="adam_small", in_specs=[vmem] * (2 + 3 * n), out_specs=[vmem] * (4 * n + 1),
        out_shape=shapes * 4 + [jax.ShapeDtypeStruct((1, 1), F32)],
        scratch_shapes=[pltpu.VMEM(packs.shape, F32)],
        compiler_params=pltpu.CompilerParams(vmem_limit_bytes=VMEM_LIMIT),
    )(packs, dmods, *weights, *ms, *vs)


def kernel(x, c, w_ada, b_ada, w_in, w_pool, pool_scale, sgu_ln_g, sgu_ln_b, w_sgu, b_sgu, w_out, ln_g, ln_b, loss_target, m_w_ada, m_b_ada, m_w_in, m_w_pool, m_pool_scale, m_sgu_ln_g, m_sgu_ln_b, m_w_sgu, m_b_sgu, m_w_out, m_ln_g, m_ln_b, v_w_ada, v_b_ada, v_w_in, v_w_pool, v_pool_scale, v_sgu_ln_g, v_sgu_ln_b, v_w_sgu, v_b_sgu, v_w_out, v_ln_g, v_ln_b):
    small = (w_pool, pool_scale, sgu_ln_g, sgu_ln_b, w_sgu, b_sgu)
    *saved0, w_in0, w_out0, w_in1, w_out1, mod, c_all = _forward_first(x, c, w_ada, b_ada, w_in, w_out, small, ln_g, ln_b)
    *saved1, dout, sq = _forward_last(saved0[3], mod, w_in1, w_out1, small, ln_g, ln_b, loss_target)

    dx1, *shared = _backward_layer(1, dout, saved1, mod, w_in1, w_out1, small, ln_g, sq=sq)
    dx0, h, cat, dy, dproj, pack, dmod = _backward_layer(0, dx1, saved0, mod, w_in0, w_out0, small, ln_g, shared=shared)
    g_in, g_out, pack, dmods = _wgrad_reduce(h, dproj, cat, dy, pack, dmod)

    g_out = g_out.reshape(DEPTH, W_OUT_ROWS, D_MODEL)
    big = _adam_sharded(c_all, dmods, (w_ada, m_w_ada, v_w_ada), (w_in, g_in, m_w_in, v_w_in), (w_out, g_out, m_w_out, v_w_out))
    ada, win, wout = big[0:4], big[4:8], big[8:12]
    small_w = (b_ada, w_pool, pool_scale, sgu_ln_g, sgu_ln_b, w_sgu, b_sgu, ln_g, ln_b)
    small_m = (m_b_ada, m_w_pool, m_pool_scale, m_sgu_ln_g, m_sgu_ln_b, m_w_sgu, m_b_sgu, m_ln_g, m_ln_b)
    small_v = (v_b_ada, v_w_pool, v_pool_scale, v_sgu_ln_g, v_sgu_ln_b, v_w_sgu, v_b_sgu, v_ln_g, v_ln_b)
    res = _adam_small(pack, dmods, small_w, small_m, small_v)
    n = len(small_w)
    loss = res[4 * n].reshape(())

    def ordered(k):
        s = res[k * n:(k + 1) * n]
        return (ada[k], s[0], win[k], s[1], s[2], s[3], s[4], s[5], s[6], wout[k], s[7], s[8])

    return (loss, dx0[None], *ordered(0), *ordered(1), *ordered(2), *ordered(3))
```

```python
import jax
import jax.numpy as jnp
from jax import lax
from jax.experimental import pallas as pl
from jax.experimental.pallas import tpu as pltpu

F32 = jnp.float32
BF16 = jnp.bfloat16
MESH = pl.DeviceIdType.MESH

N_DEV = 8
N_CHIP = 4
DEPTH = 2
SEQ = 2048
D_MODEL = 1024
D_POOL = 512
D_PROJ = 2560
HEAD = 128
N_HEAD = 4
ROWS = 256
N_TILE = SEQ // ROWS
HALO = 16
W_IN_COLS = D_PROJ // N_CHIP
W_OUT_ROWS = D_MODEL // N_CHIP
W_ADA_COLS = 3 * D_MODEL // N_CHIP
HALF_IN = D_MODEL // 2
HALF_OUT = W_OUT_ROWS // 2
DEEPNORM_ALPHA = (2.0 * DEPTH) ** 0.25
LN_EPS = 1e-5
INV_SQRT2 = 0.7071067811865476
INV_SQRT_2PI = 0.3989422804014327

ADAM_LR = 0.001
ADAM_B1 = 0.9
ADAM_B2 = 0.999
ADAM_EPS = 1e-08
ADAM_WD = 0.01
ADAM_STEP = 10
ADAM_PARTS = 2

PK_W_POOL = 0
PK_W_SGU = 512
PK_POOL_SCALE = 1024
PK_SGU_LN_G = 1032
PK_SGU_LN_B = 1040
PK_B_SGU = 1048
PK_LN_G = 1056
PK_LN_B = 1064
PK_ROWS = 1152
PK_HALF = PK_ROWS // 2
PK_PIECE = PK_HALF // N_CHIP
DM_LOSS = 3 * D_MODEL // HEAD
DM_ROWS = DM_LOSS + 1

VMEM_LIMIT = 56 * 1024 * 1024

GATHER_SECOND_ROUND_STEP = 0
GATHER_PASS_STEP = N_TILE - 3

LOCAL_DMA_PRIORITY = 1

NN = (((1,), (0,)), ((), ()))
NT = (((1,), (1,)), ((), ()))
TN = (((0,), (0,)), ((), ()))


def _dot(a, b, dims=NN):
    return lax.dot_general(a, b, dims, preferred_element_type=F32)


def _dot_exact(a, b, dims=NN):
    return lax.dot_general(a, b, dims, preferred_element_type=F32, precision=lax.Precision.HIGHEST)


def _layer_norm(v):
    mu = jnp.mean(v, axis=-1, keepdims=True)
    d = v - mu
    var = jnp.mean(d * d, axis=-1, keepdims=True)
    rstd = lax.rsqrt(var + LN_EPS)
    return d * rstd, rstd


def _layer_norm_bwd(dvhat, vhat, rstd):
    m1 = jnp.mean(dvhat, axis=-1, keepdims=True)
    m2 = jnp.mean(dvhat * vhat, axis=-1, keepdims=True)
    return rstd * (dvhat - m1 - vhat * m2)


def _sigmoid(v):
    return 1.0 / (1.0 + jnp.exp(-v))


def _gelu_parts(v):
    phi = 0.5 * (1.0 + lax.erf(v * INV_SQRT2))
    pdf = INV_SQRT_2PI * jnp.exp(-0.5 * v * v)
    return phi, pdf


def _sum_rows(v):
    return jnp.sum(v, axis=0, keepdims=True)


def _window_sums(ext, toward_later):
    n = ext.shape[0]

    def shifted(v, k):
        return pltpu.roll(v, (n - k) if toward_later else k, 0)

    s2 = ext + shifted(ext, 1)
    r4 = s2[:, HEAD:]
    s4 = r4 + shifted(r4, 2)
    r8 = s4[:, HEAD:]
    s8 = r8 + shifted(r8, 4)
    r16 = s8[:, HEAD:]
    s16 = r16 + shifted(r16, 8)
    return jnp.concatenate([s2[:, :HEAD], s4[:, :HEAD], s8[:, :HEAD], s16], axis=1)


def _window_counts(row0):
    t1 = row0 + 1 + lax.broadcasted_iota(jnp.int32, (ROWS, D_POOL), 0)
    lane = lax.broadcasted_iota(jnp.int32, (ROWS, D_POOL), 1)
    width = jnp.where(lane < HEAD, 2, jnp.where(lane < 2 * HEAD, 4, jnp.where(lane < 3 * HEAD, 8, 16)))
    return jnp.minimum(t1, width).astype(F32)


def _causal_mask():
    r = lax.broadcasted_iota(jnp.int32, (HEAD, HEAD), 0)
    s = lax.broadcasted_iota(jnp.int32, (HEAD, HEAD), 1)
    return r >= s


def _chunks_to_lanes(v):
    return jnp.concatenate([v[n * HEAD:(n + 1) * HEAD] for n in range(ROWS // HEAD)], axis=1)


def _lanes_to_chunks(v):
    return jnp.concatenate([v[:, n * HEAD:(n + 1) * HEAD] for n in range(ROWS // HEAD)], axis=0)


def _pack_stats(rstd_x, rstd_z, rstd_v):
    lane = lax.broadcasted_iota(jnp.int32, (ROWS, HEAD), 1)
    packed = rstd_x
    for k, r in enumerate([rstd_z] + list(rstd_v)):
        packed = jnp.where(lane < 16 * (k + 1), packed, r)
    return packed


def _unpack_stats(stats):
    cols = [stats[:, 16 * k:16 * k + 1] for k in range(2 + N_HEAD)]
    return cols[0], cols[1], cols[2:]


def _mixer(proj, halo, row0, wpool_ref, pscale, sgu_g_ref, sgu_b_ref, wsgu_ref, bsgu_ref, saved=None):
    xa = proj[:, 0:512]
    ga = proj[:, 512:1024]
    u = proj[:, 1024:1536]
    v = proj[:, 1536:2048]
    gb = proj[:, 2048:2560]
    ext = jnp.concatenate([halo, xa], axis=0)
    win = _window_sums(ext, toward_later=False)[HALO:]
    cnt = _window_counts(row0)
    pooled = (win / cnt - xa).astype(BF16)
    pw = jnp.concatenate(
        [_dot(pooled[:, g * HEAD:(g + 1) * HEAD], wpool_ref[g].astype(BF16)) for g in range(N_HEAD)], axis=1)
    sig_a = _sigmoid(ga) if saved is None else saved["sig_a"]
    ya = pw * pscale * (ga * sig_a)
    phi_u, pdf_u = _gelu_parts(u)
    phi_v, pdf_v = _gelu_parts(v)
    gu = u * phi_u
    gv = v * phi_v
    sig_b = _sigmoid(gb) if saved is None else saved["sig_b"]
    silu_b = gb * sig_b
    mask = _causal_mask()
    diag = lax.broadcasted_iota(jnp.int32, (HEAD, HEAD), 0) == lax.broadcasted_iota(jnp.int32, (HEAD, HEAD), 1)
    vhat, rstd_v, vln_l, mixed = [], [], [], []
    for h in range(N_HEAD):
        if saved is None:
            vh, rh = _layer_norm(gv[:, h * HEAD:(h + 1) * HEAD])
        else:
            vh, rh = saved["vhat"][h], saved["rstd_v"][h]
        ln = (vh * sgu_g_ref[h:h + 1, :] + sgu_b_ref[h:h + 1, :]).astype(BF16)
        ln_l = _chunks_to_lanes(ln)
        wm = jnp.where(mask, wsgu_ref[h], 0.0).astype(BF16)
        bias = jnp.sum(jnp.where(diag, jnp.broadcast_to(bsgu_ref[h:h + 1, :], (HEAD, HEAD)), 0.0), axis=1, keepdims=True)
        mx = _lanes_to_chunks(_dot(wm, ln_l) + bias)
        vhat.append(vh)
        rstd_v.append(rh)
        vln_l.append(ln_l)
        mixed.append(mx)
    mixed = jnp.concatenate(mixed, axis=1)
    yb = gu * mixed * silu_b
    return dict(xa=xa, ga=ga, u=u, v=v, gb=gb, cnt=cnt, pooled=pooled, pw=pw, sig_a=sig_a, ya=ya, phi_u=phi_u, pdf_u=pdf_u,
                phi_v=phi_v, pdf_v=pdf_v, gu=gu, sig_b=sig_b, silu_b=silu_b, vhat=vhat, rstd_v=rstd_v, vln_l=vln_l,
                mixed=mixed, yb=yb, mask=mask)


def _const(shape, *index):
    lead = tuple(index) + (0,) * (len(shape) - len(index))
    return pl.BlockSpec(shape, lambda *_: lead)


def _const_in(shape, *index):
    lead = tuple(index) + (0,) * (len(shape) - len(index))
    return pl.BlockSpec(shape, lambda *_: lead, pipeline_mode=pl.Buffered(1))


def _layer_weight_specs(l):
    return [
        _const_in((None, N_HEAD, HEAD, HEAD), l),
        _const_in((DEPTH, D_POOL)),
        _const_in((None, N_HEAD, HEAD), l),
        _const_in((None, N_HEAD, HEAD), l),
        _const_in((None, N_HEAD, HEAD, HEAD), l),
        _const_in((None, N_HEAD, HEAD), l),
    ]


def _forward_tile(l, i, x_ref, mod_ref, win_ref, wout_ref, small_refs, lng_ref, lnb_ref, carry_ref, saved_refs):
    wpool_ref, pscale_ref, sgu_g_ref, sgu_b_ref, wsgu_ref, bsgu_ref = small_refs
    proj_ref, y_ref, xn_ref, zn_ref, stats_ref, sig_ref, vhat_ref = saved_refs
    x = x_ref[...]
    if l > 0:
        x = x * lng_ref[l - 1:l, :] + lnb_ref[l - 1:l, :]
    shift, scale, gate = mod_ref[0:1, :], mod_ref[1:2, :], mod_ref[2:3, :]
    xn, rstd_x = _layer_norm(x)
    xn_ref[...] = xn.astype(xn_ref.dtype)
    h = xn * (1.0 + scale) + shift
    proj = _dot(h.astype(BF16), win_ref[...])
    proj_ref[...] = proj.astype(proj_ref.dtype)
    m = _mixer(proj, carry_ref[...], i * ROWS, wpool_ref, pscale_ref[l:l + 1, :], sgu_g_ref, sgu_b_ref, wsgu_ref, bsgu_ref)
    carry_ref[...] = m["xa"][ROWS - HALO:]
    sig_ref[...] = jnp.concatenate([m["sig_a"], m["sig_b"]], axis=1).astype(sig_ref.dtype)
    vhat_ref[...] = jnp.concatenate(m["vhat"], axis=1).astype(vhat_ref.dtype)
    cat = jnp.concatenate([m["ya"], m["yb"]], axis=1).astype(BF16)
    y = _dot(cat, wout_ref[...])
    y_ref[...] = y.astype(y_ref.dtype)
    zn, rstd_z = _layer_norm(DEEPNORM_ALPHA * x + gate * y)
    zn_ref[...] = zn
    stats_ref[...] = _pack_stats(rstd_x, rstd_z, m["rstd_v"])
    return zn


SAVED_COLS = (D_PROJ, D_MODEL, D_MODEL, D_MODEL, HEAD, D_MODEL, D_POOL)
SAVED_TYPES = (BF16, BF16, BF16, F32, F32, BF16, BF16)


def _saved_outputs():
    return ([jax.ShapeDtypeStruct((SEQ, cols), t) for cols, t in zip(SAVED_COLS, SAVED_TYPES)],
            [pl.BlockSpec((ROWS, cols), lambda i: (i, 0)) for cols in SAVED_COLS])


def _forward_last(zn_prev, mod, w_in, w_out, small, ln_g, ln_b, target):
    l = DEPTH - 1
    n_saved = len(SAVED_COLS)

    def body(*refs):
        x_ref, mod_ref, win_ref, wout_ref = refs[:4]
        small_refs, lng_ref, lnb_ref, tgt_ref = refs[4:10], refs[10], refs[11], refs[12]
        saved_refs = refs[13:13 + n_saved]
        dout_ref, loss_ref, carry_ref = refs[13 + n_saved:]
        i = pl.program_id(0)

        @pl.when(i == 0)
        def _():
            carry_ref[...] = jnp.zeros_like(carry_ref)
            loss_ref[...] = jnp.zeros_like(loss_ref)

        zn = _forward_tile(l, i, x_ref, mod_ref, win_ref, wout_ref, small_refs, lng_ref, lnb_ref, carry_ref, saved_refs)
        err = zn * lng_ref[l:l + 1, :] + lnb_ref[l:l + 1, :] - tgt_ref[...]
        dout_ref[...] = err * (1.0 / D_MODEL)
        loss_ref[...] += jnp.sum(err * err)

    tile = pl.BlockSpec((ROWS, D_MODEL), lambda i: (i, 0))
    tile3 = pl.BlockSpec((None, ROWS, D_MODEL), lambda i: (0, i, 0))
    in_specs = [tile, _const_in((None, 8, D_MODEL), l), _const_in((D_MODEL, D_PROJ)), _const_in((D_MODEL, D_MODEL))]
    in_specs += _layer_weight_specs(l) + [_const_in((DEPTH, D_MODEL)), _const_in((DEPTH, D_MODEL)), tile3]
    out_shape, out_specs = _saved_outputs()
    out_shape += [jax.ShapeDtypeStruct((SEQ, D_MODEL), F32), jax.ShapeDtypeStruct((8, HEAD), F32)]
    out_specs += [tile, _const((8, HEAD))]
    return pl.pallas_call(
        body, name="fwd_last", grid=(N_TILE,), in_specs=in_specs, out_specs=out_specs, out_shape=out_shape,
        scratch_shapes=[pltpu.VMEM((HALO, D_POOL), F32)],
        compiler_params=pltpu.CompilerParams(dimension_semantics=("arbitrary",), vmem_limit_bytes=VMEM_LIMIT),
    )(zn_prev, mod, w_in, w_out, *small, ln_g, ln_b, target)


def _backward_layer(l, dout, saved, mod, w_in, w_out, small, ln_g, sq=None, shared=None):
    has_loss = sq is not None

    def body(*refs):
        (dout_ref, proj_ref, y_ref, xn_ref, zn_ref, stats_ref, sig_ref, vhat_ref, halo_ref, mod_ref, win_ref, wout_ref,
         wpool_ref, pscale_ref, sgu_g_ref, sgu_b_ref, wsgu_ref, bsgu_ref, lng_ref) = refs[:19]
        n_in = 20 if has_loss else 19 + 6
        dx_ref, h_ref, cat_ref, dy_ref, dproj_ref, pack_ref, dmod_ref, carry_ref = refs[n_in:n_in + 8]
        i = pl.program_id(0)
        tile = N_TILE - 1 - i

        @pl.when(i == 0)
        def _():
            carry_ref[...] = jnp.zeros_like(carry_ref)
            pack_ref[...] = jnp.zeros_like(pack_ref)
            dmod_ref[...] = jnp.zeros_like(dmod_ref)
            if has_loss:
                dmod_ref[3:4, 0:HEAD] = refs[19][0:1, :]

        xn = xn_ref[...].astype(F32)
        zn = zn_ref[...]
        y = y_ref[...].astype(F32)
        dout = dout_ref[...]
        rstd_x, rstd_z, rstd_v = _unpack_stats(stats_ref[...])
        kept = dict(sig_a=sig_ref[:, :D_POOL].astype(F32), sig_b=sig_ref[:, D_POOL:].astype(F32), rstd_v=rstd_v,
                    vhat=[vhat_ref[:, hd * HEAD:(hd + 1) * HEAD].astype(F32) for hd in range(N_HEAD)])
        pscale = pscale_ref[l:l + 1, :]
        shift, scale, gate = mod_ref[0:1, :], mod_ref[1:2, :], mod_ref[2:3, :]
        h = xn * (1.0 + scale) + shift
        h_ref[...] = h.astype(BF16)
        g_ln_g = _sum_rows(dout * zn)
        g_ln_b = _sum_rows(dout)
        dz = _layer_norm_bwd(dout * lng_ref[l:l + 1, :], zn, rstd_z)
        d_gate = _sum_rows(dz * y)
        dy = (gate * dz).astype(BF16)
        dy_ref[...] = dy

        halo = jnp.where(tile > 0, halo_ref[...].astype(F32), 0.0)
        m = _mixer(proj_ref[...].astype(F32), halo, tile * ROWS, wpool_ref, pscale, sgu_g_ref, sgu_b_ref, wsgu_ref, bsgu_ref,
                   saved=kept)
        cat_ref[...] = jnp.concatenate([m["ya"], m["yb"]], axis=1).astype(BF16)
        dcat = _dot(dy, wout_ref[...], NT)
        dya = dcat[:, :D_POOL]
        dyb = dcat[:, D_POOL:]

        ga, sig_a = m["ga"], m["sig_a"]
        dp = dya * (ga * sig_a)
        d_ga = dya * (m["pw"] * pscale) * (sig_a * (1.0 + ga * (1.0 - sig_a)))
        g_pscale = _sum_rows(dp * m["pw"])
        dpw = (dp * pscale).astype(BF16)
        dpooled = []
        for g in range(N_HEAD):
            cols = slice(g * HEAD, (g + 1) * HEAD)
            pack_ref[PK_W_POOL + g * HEAD:PK_W_POOL + (g + 1) * HEAD, :] += _dot(m["pooled"][:, cols], dpw[:, cols], TN)
            dpooled.append(_dot(dpw[:, cols], wpool_ref[g].astype(BF16), NT))
        dpooled = jnp.concatenate(dpooled, axis=1)
        q = dpooled / m["cnt"]
        ext = jnp.concatenate([q, carry_ref[...]], axis=0)
        d_xa = _window_sums(ext, toward_later=True)[:ROWS] - dpooled
        carry_ref[...] = q[:HALO]

        gu, mixed, silu_b, gb, sig_b = m["gu"], m["mixed"], m["silu_b"], m["gb"], m["sig_b"]
        d_mixed = dyb * gu * silu_b
        d_gu = dyb * mixed * silu_b
        d_gb = dyb * gu * mixed * (sig_b * (1.0 + gb * (1.0 - sig_b)))
        d_u = d_gu * (m["phi_u"] + m["u"] * m["pdf_u"])
        ones = jnp.ones((8, HEAD), F32)
        d_v = []
        for hd in range(N_HEAD):
            cols = slice(hd * HEAD, (hd + 1) * HEAD)
            dm = d_mixed[:, cols]
            dm_l = _chunks_to_lanes(dm.astype(BF16))
            g_w = _dot(dm_l, m["vln_l"][hd], NT)
            pack_ref[PK_W_SGU + hd * HEAD:PK_W_SGU + (hd + 1) * HEAD, :] += jnp.where(m["mask"], g_w, 0.0)
            dm_sum = dm[0:HEAD]
            for n in range(1, ROWS // HEAD):
                dm_sum = dm_sum + dm[n * HEAD:(n + 1) * HEAD]
            pack_ref[PK_B_SGU + hd:PK_B_SGU + hd + 1, :] += _dot_exact(ones, dm_sum, NT)[0:1]
            wm = jnp.where(m["mask"], wsgu_ref[hd], 0.0).astype(BF16)
            d_vln = _lanes_to_chunks(_dot(wm, dm_l, TN))
            vhat = m["vhat"][hd]
            pack_ref[PK_SGU_LN_G + hd:PK_SGU_LN_G + hd + 1, :] += _sum_rows(d_vln * vhat)
            pack_ref[PK_SGU_LN_B + hd:PK_SGU_LN_B + hd + 1, :] += _sum_rows(d_vln)
            d_v.append(_layer_norm_bwd(d_vln * sgu_g_ref[hd:hd + 1, :], vhat, m["rstd_v"][hd]))
        v = m["v"]
        d_v = jnp.concatenate(d_v, axis=1) * (m["phi_v"] + v * m["pdf_v"])

        dproj = jnp.concatenate([d_xa, d_ga, d_u, d_v, d_gb], axis=1).astype(BF16)
        dproj_ref[...] = dproj
        dh = _dot(dproj, win_ref[...], NT)
        d_scale = _sum_rows(dh * xn)
        d_shift = _sum_rows(dh)
        dx_ref[...] = DEEPNORM_ALPHA * dz + _layer_norm_bwd(dh * (1.0 + scale), xn, rstd_x)

        dmod_ref[0:1, :] += d_shift
        dmod_ref[1:2, :] += d_scale
        dmod_ref[2:3, :] += d_gate
        for g in range(N_HEAD):
            pack_ref[PK_POOL_SCALE + g:PK_POOL_SCALE + g + 1, :] += g_pscale[:, g * HEAD:(g + 1) * HEAD]
        for k in range(D_MODEL // HEAD):
            pack_ref[PK_LN_G + k:PK_LN_G + k + 1, :] += g_ln_g[:, k * HEAD:(k + 1) * HEAD]
            pack_ref[PK_LN_B + k:PK_LN_B + k + 1, :] += g_ln_b[:, k * HEAD:(k + 1) * HEAD]

    def rev(i):
        return (N_TILE - 1 - i, 0)

    tile = pl.BlockSpec((ROWS, D_MODEL), rev)
    halo = pl.BlockSpec((HALO, D_POOL), lambda i: (jnp.maximum((N_TILE - 1 - i) * (ROWS // HALO) - 1, 0), 0))
    in_specs = [tile] + [pl.BlockSpec((ROWS, a.shape[1]), rev) for a in saved] + [halo]
    in_specs += [_const_in((None, 8, D_MODEL), l), _const_in((D_MODEL, D_PROJ)), _const_in((D_MODEL, D_MODEL))]
    in_specs += _layer_weight_specs(l) + [_const_in((DEPTH, D_MODEL))]
    args = [dout, *saved, saved[0], mod, w_in, w_out, *small, ln_g]
    stacked = lambda cols: pl.BlockSpec((None, ROWS, cols), lambda i: (l, N_TILE - 1 - i, 0))
    out_shape = [jax.ShapeDtypeStruct((SEQ, D_MODEL), F32), jax.ShapeDtypeStruct((DEPTH, SEQ, D_MODEL), BF16),
                 jax.ShapeDtypeStruct((DEPTH, SEQ, D_MODEL), BF16), jax.ShapeDtypeStruct((DEPTH, SEQ, D_MODEL), BF16),
                 jax.ShapeDtypeStruct((DEPTH, SEQ, D_PROJ), BF16), jax.ShapeDtypeStruct((DEPTH, PK_ROWS, HEAD), F32),
                 jax.ShapeDtypeStruct((DEPTH, 8, D_MODEL), F32)]
    out_specs = [tile, stacked(D_MODEL), stacked(D_MODEL), stacked(D_MODEL), stacked(D_PROJ),
                 _const((None, PK_ROWS, HEAD), l), _const((None, 8, D_MODEL), l)]
    aliases = {}
    if has_loss:
        in_specs.append(_const_in((8, HEAD)))
        args.append(sq)
    else:
        aliases = {len(args) + k: 1 + k for k in range(len(shared))}
        in_specs += [pl.BlockSpec(memory_space=pl.ANY)] * len(shared)
        args += list(shared)
    return pl.pallas_call(
        body, name="bwd_last" if has_loss else "bwd_first", grid=(N_TILE,), in_specs=in_specs, out_specs=out_specs,
        out_shape=out_shape, scratch_shapes=[pltpu.VMEM((HALO, D_POOL), F32)], input_output_aliases=aliases,
        compiler_params=pltpu.CompilerParams(dimension_semantics=("arbitrary",), vmem_limit_bytes=VMEM_LIMIT),
    )(*args)


def _flip(v, f):
    return v + f - 2 * v * f


class _Place:
    def __init__(self):
        x, y, c = lax.axis_index("x"), lax.axis_index("y"), lax.axis_index("c")
        self.x, self.y, self.c = x, y, c
        self.chip = 2 * x + y
        self.dev = 4 * x + 2 * y + c
        self.sibling = (x, y, 1 - c)
        x1, y1 = _flip(x, 1 - c), _flip(y, c)
        x2, y2 = _flip(x, c), _flip(y, 1 - c)
        self.first = (x1, y1, c)
        self.second = (x2, y2, c)
        self.chip_first = 2 * x1 + y1
        self.chip_second = 2 * x2 + y2
        self.chip_far = 2 * (1 - x) + (1 - y)
        self.my_first_coord = jnp.where(c == 0, x, y)

    def first_coord(self, ch):
        return jnp.where(self.c == 0, ch // 2, ch % 2)

    def others(self):
        return [(_flip(self.x, (r >> 2) & 1), _flip(self.y, (r >> 1) & 1), _flip(self.c, r & 1)) for r in range(1, N_DEV)]

    def other_chips(self):
        return [(1 - self.x, self.y), (self.x, 1 - self.y), (1 - self.x, 1 - self.y)]


class _WeightGather:
    CHUNKS = 2
    N_SEMS = 12 * CHUNKS

    def __init__(self, place, win, wout, send, recv):
        self.p, self.win, self.wout, self.send, self.recv = place, win, wout, send, recv
        p = place
        self.plan = [(p.chip, p.first), (p.chip, p.second), (p.chip_first, p.second),
                     (p.chip_first, p.sibling), (p.chip_second, p.sibling), (p.chip_far, p.sibling)]

    def _copies(self, k, q):
        ch, target = self.plan[k]
        n_in, n_out = HALF_IN // self.CHUNKS, HALF_OUT // self.CHUNKS
        rows_in = pl.ds(pl.multiple_of(self.p.c * HALF_IN + q * n_in, n_in), n_in)
        cols_in = pl.ds(pl.multiple_of(ch * W_IN_COLS, 128), W_IN_COLS)
        rows_out = pl.ds(pl.multiple_of(ch * W_OUT_ROWS + self.p.c * HALF_OUT + q * n_out, n_out), n_out)
        r_in = self.win.at[rows_in, cols_in]
        r_out = self.wout.at[rows_out, :]
        s = 2 * (6 * q + k)
        return [pltpu.make_async_remote_copy(r_in, r_in, self.send.at[s], self.recv.at[s],
                                             device_id=target, device_id_type=MESH),
                pltpu.make_async_remote_copy(r_out, r_out, self.send.at[s + 1], self.recv.at[s + 1],
                                             device_id=target, device_id_type=MESH)]

    def _start(self, k, q):
        for cp in self._copies(k, q):
            cp.start()

    def _landed(self, k, q):
        for cp in self._copies(k, q):
            cp.wait_recv()

    def start_first_round(self):
        for q in range(self.CHUNKS):
            self._start(0, q)
        for q in range(self.CHUNKS):
            self._start(1, q)

    def start_second_round(self, q):
        self._landed(0, q)
        self._start(2, q)
        self._start(3, q)

    def pass_second_round(self, q):
        self._landed(1, q)
        self._start(4, q)
        self._landed(2, q)
        self._start(5, q)

    def finish(self):
        for q in range(self.CHUNKS):
            for k in (3, 4, 5):
                self._landed(k, q)
        for q in range(self.CHUNKS):
            for k in range(len(self.plan)):
                for cp in self._copies(k, q):
                    cp.wait_send()


def _forward_first(x, c_vec, w_ada, b_ada, w_in, w_out, small, ln_g, ln_b):
    n_saved = len(SAVED_COLS)

    def body(*refs):
        x_ref, c_ref, wada_hbm, bada_ref, win_hbm, wout_hbm = refs[:6]
        small_refs, lng_ref, lnb_ref = refs[6:12], refs[12], refs[13]
        saved_refs = refs[14:14 + n_saved]
        win0, wout0, win1, wout1, mod_out, c_out = refs[14 + n_saved:20 + n_saved]
        (carry_ref, wada_ref, win_ref, wout_ref, win_bf, wout_bf, mod_mine, mod_all, c_all, mod_ref, win_v, wout_v,
         g0_send, g0_recv, g1_send, g1_recv, c_send, c_recv, mod_send, mod_recv, local_sem) = refs[20 + n_saved:]
        i = pl.program_id(0)
        p = _Place()
        gather0 = _WeightGather(p, win_v, wout_v, g0_send, g0_recv)
        keep = [pltpu.make_async_copy(win_v, win0, local_sem.at[9]), pltpu.make_async_copy(wout_v, wout0, local_sem.at[10])]
        gather1 = _WeightGather(p, win1, wout1, g1_send, g1_recv)

        @pl.when(i == 0)
        def _():
            carry_ref[...] = jnp.zeros_like(carry_ref)
            loads = [pltpu.make_async_copy(win_hbm.at[0], win_ref.at[0], local_sem.at[4]),
                     pltpu.make_async_copy(wout_hbm.at[0], wout_ref.at[0], local_sem.at[5]),
                     pltpu.make_async_copy(win_hbm.at[1], win_ref.at[1], local_sem.at[6]),
                     pltpu.make_async_copy(wout_hbm.at[1], wout_ref.at[1], local_sem.at[7]),
                     pltpu.make_async_copy(wada_hbm, wada_ref, local_sem.at[8])]
            for n, cp in enumerate(loads):
                cp.start(priority=LOCAL_DMA_PRIORITY if n >= 2 else 0)

            c_all[pl.ds(p.dev, 1), :] = c_ref[...]
            mine = c_all.at[pl.ds(p.dev, 1), :]
            c_copies = [pltpu.make_async_remote_copy(mine, mine, c_send.at[r], c_recv.at[r], device_id=d, device_id_type=MESH)
                        for r, d in enumerate(p.others())]
            for cp in c_copies:
                cp.start()

            cols = pl.ds(pl.multiple_of(p.chip * W_IN_COLS, 128), W_IN_COLS)
            rows = pl.ds(pl.multiple_of(p.chip * W_OUT_ROWS, W_OUT_ROWS), W_OUT_ROWS)
            own = [pltpu.make_async_copy(win_bf.at[0], win_v.at[:, cols], local_sem.at[0]),
                   pltpu.make_async_copy(wout_bf.at[0], wout_v.at[rows, :], local_sem.at[1]),
                   pltpu.make_async_copy(win_bf.at[1], win1.at[:, cols], local_sem.at[2]),
                   pltpu.make_async_copy(wout_bf.at[1], wout1.at[rows, :], local_sem.at[3])]
            for l in range(DEPTH):
                loads[2 * l].wait()
                win_bf[l] = win_ref[l].astype(BF16)
                own[2 * l].start(priority=LOCAL_DMA_PRIORITY * l)
                loads[2 * l + 1].wait()
                wout_bf[l] = wout_ref[l].astype(BF16)
                own[2 * l + 1].start(priority=LOCAL_DMA_PRIORITY * l)
                if l == 0:
                    own[0].wait()
                    own[1].wait()
                    gather0.start_first_round()
            for cp in c_copies:
                cp.wait()
            loads[4].wait()

            cv = c_all[...]
            c_out[...] = cv
            silu_c = (cv * _sigmoid(cv)).astype(BF16)
            for l in range(DEPTH):
                mod_mine[l] = _dot(silu_c, wada_ref[l].astype(BF16))
            mod_all[p.chip] = mod_mine[...]
            m_copies = [pltpu.make_async_remote_copy(mod_mine, mod_all.at[p.chip], mod_send.at[k], mod_recv.at[k],
                                                     device_id=(px, py, p.c), device_id_type=MESH)
                        for k, (px, py) in enumerate(p.other_chips())]
            for cp in m_copies:
                cp.start()
            for q in range(gather0.CHUNKS):
                gather0.start_second_round(q)
            own[2].wait()
            own[3].wait()
            gather1.start_first_round()
            for cp in m_copies:
                cp.wait()
            mod_ref[...] = jnp.zeros_like(mod_ref)
            for l in range(DEPTH):
                full = jnp.concatenate([mod_all[ch, l, pl.ds(p.dev, 1), :] for ch in range(N_CHIP)], axis=1) + bada_ref[l:l + 1, :]
                for k in range(3):
                    mod_ref[l, k:k + 1, :] = full[:, k * D_MODEL:(k + 1) * D_MODEL]
            mod_out[...] = mod_ref[...]
            for q in range(gather0.CHUNKS):
                gather0.pass_second_round(q)
            gather0.finish()
            for cp in keep:
                cp.start(priority=LOCAL_DMA_PRIORITY)

        for q in range(_WeightGather.CHUNKS):
            @pl.when(i == GATHER_SECOND_ROUND_STEP + q)
            def _(q=q):
                gather1.start_second_round(q)

            @pl.when(i == GATHER_PASS_STEP)
            def _(q=q):
                gather1.pass_second_round(q)

        _forward_tile(0, i, x_ref, mod_ref.at[0], win_v, wout_v, small_refs, lng_ref, lnb_ref, carry_ref, saved_refs)

        @pl.when(i == N_TILE - 1)
        def _():
            gather1.finish()
            for cp in keep:
                cp.wait()

    hbm = pl.BlockSpec(memory_space=pl.ANY)
    tile3 = pl.BlockSpec((None, ROWS, D_MODEL), lambda i: (0, i, 0))
    in_specs = [tile3, _const_in((1, D_MODEL)), hbm, _const_in((DEPTH, 3 * D_MODEL)), hbm, hbm]
    in_specs += _layer_weight_specs(0) + [_const_in((DEPTH, D_MODEL)), _const_in((DEPTH, D_MODEL))]
    out_shape, out_specs = _saved_outputs()
    w_in_shape = jax.ShapeDtypeStruct((D_MODEL, D_PROJ), BF16)
    w_out_shape = jax.ShapeDtypeStruct((D_MODEL, D_MODEL), BF16)
    out_shape += [w_in_shape, w_out_shape, w_in_shape, w_out_shape,
                  jax.ShapeDtypeStruct((DEPTH, 8, D_MODEL), F32), jax.ShapeDtypeStruct((N_DEV, D_MODEL), F32)]
    out_specs += [hbm, hbm, hbm, hbm, _const((DEPTH, 8, D_MODEL)), _const((N_DEV, D_MODEL))]
    gather_sems = [pltpu.SemaphoreType.DMA((_WeightGather.N_SEMS,))] * 4
    scratch = [
        pltpu.VMEM((HALO, D_POOL), F32),
        pltpu.VMEM(w_ada.shape, F32), pltpu.VMEM(w_in.shape, F32), pltpu.VMEM(w_out.shape, F32),
        pltpu.VMEM((DEPTH, D_MODEL, W_IN_COLS), BF16), pltpu.VMEM((DEPTH, W_OUT_ROWS, D_MODEL), BF16),
        pltpu.VMEM((DEPTH, N_DEV, W_ADA_COLS), F32), pltpu.VMEM((N_CHIP, DEPTH, N_DEV, W_ADA_COLS), F32),
        pltpu.VMEM((N_DEV, D_MODEL), F32), pltpu.VMEM((DEPTH, 8, D_MODEL), F32),
        pltpu.VMEM((D_MODEL, D_PROJ), BF16), pltpu.VMEM((D_MODEL, D_MODEL), BF16),
    ] + gather_sems + [
        pltpu.SemaphoreType.DMA((7,)), pltpu.SemaphoreType.DMA((7,)),
        pltpu.SemaphoreType.DMA((3,)), pltpu.SemaphoreType.DMA((3,)),
        pltpu.SemaphoreType.DMA((11,)),
    ]
    return pl.pallas_call(
        body, name="fwd_first", grid=(N_TILE,), in_specs=in_specs, out_specs=out_specs, out_shape=out_shape,
        scratch_shapes=scratch,
        compiler_params=pltpu.CompilerParams(dimension_semantics=("arbitrary",), vmem_limit_bytes=VMEM_LIMIT),
    )(x, c_vec, w_ada, b_ada, w_in, w_out, *small, ln_g, ln_b)


IN_STEPS = W_IN_COLS // HEAD
OUT_STEPS = 4
OUT_COLS = D_MODEL // OUT_STEPS
OUT_FIRST = 2
ITEMS = ([("out", k) for k in range(OUT_FIRST)] + [("in", k) for k in range(IN_STEPS)]
         + [("out", k) for k in range(OUT_FIRST, OUT_STEPS)])
N_ITEMS = len(ITEMS)
N_STEPS = DEPTH * N_ITEMS
DELAY_SUM, DELAY_SECOND, DELAY_FINAL = 1, 3, 5
SMALL_SCATTER_STEP, SMALL_GATHER_STEP, SMALL_PASS_STEP, SMALL_FINISH_STEP = 1, 3, 5, 7


def _wgrad_reduce(h, dproj, cat, dy, pack, dmod):
    def body(*refs):
        h_ref, dp_refs, cat_ref, dy_ref, pack_ref, dmod_ref = refs[0], refs[1:5], refs[5], refs[6], refs[7], refs[8]
        fin_in, fin_out, pack_out, dmod_out = refs[9:13]
        scratch = refs[13:]
        (mine_in, send_in, sib_in, st_in, r1_in, r2_in, f_in,
         mine_out, send_out, sib_out, st_out, r1_out, r2_out, f_out,
         d2d_s, d2d_r, r1_s, r1_r, r2_s, r2_r, fin_l, fin_s, fin_r) = scratch[:23]
        p = _Place()
        c = p.c
        i = pl.program_id(0)
        my_rows = pl.ds(pl.multiple_of(c * HALF_IN, HALF_IN), HALF_IN)

        def layer_of(j):
            return DEPTH - 1 - j // N_ITEMS

        def bufs(j):
            kind, k = ITEMS[j % N_ITEMS]
            if kind == "in":
                return [r.at[k] for r in (mine_in, send_in, sib_in, st_in, r1_in, r2_in, f_in)]
            return [r.at[k] for r in (mine_out, send_out, sib_out, st_out, r1_out, r2_out, f_out)]

        def piece(j, ref, ch):
            if ITEMS[j % N_ITEMS][0] == "in":
                return ref.at[:, ch * HEAD:(ch + 1) * HEAD]
            return ref.at[ch]

        def slot(ch):
            return jnp.where(c == 0, ch % 2, ch // 2)

        def to_sibling(j):
            _, send, sib, _, _, _, _ = bufs(j)
            return pltpu.make_async_remote_copy(send, sib, d2d_s.at[j], d2d_r.at[j], device_id=p.sibling, device_id_type=MESH)

        def first_round(j, ch):
            _, _, _, st, r1, _, _ = bufs(j)
            k = slot(ch)
            return pltpu.make_async_remote_copy(st.at[k], r1.at[k], r1_s.at[2 * j + k], r1_r.at[2 * j + k],
                                                device_id=p.first, device_id_type=MESH)

        def second_round(j):
            _, _, _, st, _, r2, _ = bufs(j)
            return pltpu.make_async_remote_copy(st.at[2], r2, r2_s.at[j], r2_r.at[j], device_id=p.second, device_id_type=MESH)

        def finals(j):
            f = bufs(j)[6]
            kind, k = ITEMS[j % N_ITEMS]
            if kind == "in":
                dst = fin_in.at[layer_of(j), my_rows, k * HEAD:(k + 1) * HEAD]
            else:
                dst = fin_out.at[layer_of(j), c, :, k * OUT_COLS:(k + 1) * OUT_COLS]
            return [pltpu.make_async_copy(f, dst, fin_l.at[j]),
                    pltpu.make_async_remote_copy(f, dst, fin_s.at[j], fin_r.at[j], device_id=p.sibling, device_id_type=MESH)]

        def stage_sum(j):
            mine, _, sib, st, _, _, _ = bufs(j)
            to_sibling(j).wait_recv()
            mine[...] = mine[...] + sib[...]
            for ch in range(N_CHIP):
                @pl.when(p.first_coord(ch) != p.my_first_coord)
                def _(ch=ch):
                    st[slot(ch)] = piece(j, mine, ch)[...].astype(BF16)
                    first_round(j, ch).start()

        def stage_second(j):
            mine, _, _, st, r1, _, _ = bufs(j)
            for ch in range(N_CHIP):
                @pl.when(p.first_coord(ch) == p.my_first_coord)
                def _(ch=ch):
                    first_round(j, ch).wait_recv()
                    part = piece(j, mine, ch)
                    total = part[...] + r1[slot(ch)].astype(F32)
                    part[...] = total

                    @pl.when(ch != p.chip)
                    def _():
                        st[2] = total.astype(BF16)
                        second_round(j).start()

        def stage_final(j):
            mine, _, _, _, _, r2, f = bufs(j)
            second_round(j).wait_recv()
            for ch in range(N_CHIP):
                @pl.when(ch == p.chip)
                def _(ch=ch):
                    f[...] = piece(j, mine, ch)[...] + r2[...].astype(F32)
            local, remote = finals(j)
            local.start(priority=LOCAL_DMA_PRIORITY)
            remote.start()

        def drain(j):
            to_sibling(j).wait_send()
            for ch in range(N_CHIP):
                @pl.when(p.first_coord(ch) != p.my_first_coord)
                def _(ch=ch):
                    first_round(j, ch).wait_send()

                @pl.when(jnp.logical_and(p.first_coord(ch) == p.my_first_coord, ch != p.chip))
                def _():
                    second_round(j).wait_send()
            for cp in finals(j):
                cp.wait()

        dev = p.dev
        devices = p.others()

        def half(core):
            return pl.ds(pl.multiple_of(core * PK_HALF, 16), PK_HALF)

        def finished(core, ch):
            return pl.ds(pl.multiple_of(core * PK_HALF + ch * PK_PIECE, 16), PK_PIECE)

        def small_exchange(l, first_step, bufs_l):
            (pk_mine, pk_sib, pk_st, pk_rs, pk_fin, pk_all, dm_st, dm_all, pk_sem, rs_s, rs_r, ag_s, ag_r, dm_s, dm_r) = bufs_l

            def pk_load():
                return pltpu.make_async_copy(pack_ref.at[l, half(c)], pk_mine, pk_sem.at[0])

            def pk_give():
                return pltpu.make_async_remote_copy(pack_ref.at[l, half(1 - c)], pk_sib, pk_sem.at[1], pk_sem.at[2],
                                                    device_id=p.sibling, device_id_type=MESH)

            def pk_scatter(ch):
                return pltpu.make_async_remote_copy(pk_st.at[ch * PK_PIECE:(ch + 1) * PK_PIECE], pk_rs.at[p.chip],
                                                    rs_s.at[ch], rs_r.at[p.chip], device_id=(ch // 2, ch % 2, c),
                                                    device_id_type=MESH)

            def pk_spread(ch):
                return pltpu.make_async_remote_copy(pk_fin, pk_all.at[finished(c, p.chip)], ag_s.at[ch], ag_r.at[p.chip],
                                                    device_id=(ch // 2, ch % 2, c), device_id_type=MESH)

            def pk_pass():
                return pltpu.make_async_remote_copy(pk_all.at[half(c)], pk_all.at[half(c)], pk_sem.at[3], pk_sem.at[4],
                                                    device_id=p.sibling, device_id_type=MESH)

            def dm_copy(r):
                return pltpu.make_async_remote_copy(dm_st, dm_all.at[:, pl.ds(dev, 1), :], dm_s.at[r], dm_r.at[r],
                                                    device_id=devices[r], device_id_type=MESH)

            def results():
                return [pltpu.make_async_copy(pk_all, pack_out.at[l], pk_sem.at[0]),
                        pltpu.make_async_copy(dm_all, dmod_out.at[l], pk_sem.at[5])]

            @pl.when(i == first_step)
            def _():
                pk_load().start()
                pk_give().start()
                for k in range(3):
                    for r in range(D_MODEL // HEAD):
                        dm_st[8 * k + r] = dmod_ref[l, k:k + 1, r * HEAD:(r + 1) * HEAD]
                dm_st[DM_LOSS] = dmod_ref[l, 3:4, 0:HEAD]
                dm_all[:, pl.ds(dev, 1), :] = dm_st[...]
                for r in range(N_DEV - 1):
                    dm_copy(r).start()

            @pl.when(i == first_step + SMALL_SCATTER_STEP)
            def _():
                pk_load().wait()
                pk_give().wait()
                total = pk_mine[...] + pk_sib[...]
                pk_mine[...] = total
                pk_st[...] = total.astype(BF16)
                for ch in range(N_CHIP):
                    @pl.when(ch != p.chip)
                    def _(ch=ch):
                        pk_scatter(ch).start()

            @pl.when(i == first_step + SMALL_GATHER_STEP)
            def _():
                for ch in range(N_CHIP):
                    @pl.when(ch != p.chip)
                    def _(ch=ch):
                        pltpu.make_async_remote_copy(pk_fin, pk_rs.at[ch], rs_s.at[ch], rs_r.at[ch],
                                                     device_id=p.sibling, device_id_type=MESH).wait_recv()
                for me in range(N_CHIP):
                    @pl.when(me == p.chip)
                    def _(me=me):
                        total = None
                        for ch in range(N_CHIP):
                            part = pk_mine[me * PK_PIECE:(me + 1) * PK_PIECE] if ch == me else pk_rs[ch].astype(F32)
                            total = part if total is None else total + part
                        pk_fin[...] = total.astype(BF16)
                        pk_all[finished(c, me)] = total.astype(BF16)
                for ch in range(N_CHIP):
                    @pl.when(ch != p.chip)
                    def _(ch=ch):
                        pk_spread(ch).start()

            @pl.when(i == first_step + SMALL_PASS_STEP)
            def _():
                for ch in range(N_CHIP):
                    @pl.when(ch != p.chip)
                    def _(ch=ch):
                        pltpu.make_async_remote_copy(pk_fin, pk_all.at[finished(c, ch)], ag_s.at[ch], ag_r.at[ch],
                                                     device_id=p.sibling, device_id_type=MESH).wait_recv()
                pk_pass().start()

            @pl.when(i == first_step + SMALL_FINISH_STEP)
            def _():
                pk_pass().wait()
                for ch in range(N_CHIP):
                    @pl.when(ch != p.chip)
                    def _(ch=ch):
                        pk_scatter(ch).wait_send()
                        pk_spread(ch).wait_send()
                for r in range(N_DEV - 1):
                    dm_copy(r).wait()
                for cp in results():
                    cp.start(priority=LOCAL_DMA_PRIORITY)
                for cp in results():
                    cp.wait()

        n_small = 15
        for l in range(DEPTH):
            small_exchange(l, (DEPTH - 1 - l) * N_ITEMS, scratch[23 + n_small * l:23 + n_small * (l + 1)])

        for step in range(N_ITEMS, N_STEPS):
            @pl.when(i == step)
            def _(step=step):
                drain(step - N_ITEMS)

        ii = jnp.where(i < N_ITEMS, i, i - N_ITEMS)
        in_step = jnp.logical_and(ii >= OUT_FIRST, ii < OUT_FIRST + IN_STEPS)

        @pl.when(in_step)
        def _():
            k = ii - OUT_FIRST
            rhs = jnp.concatenate([r[...] for r in dp_refs], axis=1)
            res = _dot(h_ref[...], rhs, TN)

            @pl.when(c == 0)
            def _():
                mine_in[k] = res[:HALF_IN]
                send_in[k] = res[HALF_IN:]

            @pl.when(c == 1)
            def _():
                mine_in[k] = res[HALF_IN:]
                send_in[k] = res[:HALF_IN]

        @pl.when(jnp.logical_not(in_step))
        def _():
            k = jnp.where(ii < OUT_FIRST, ii, ii - IN_STEPS)
            res = _dot(cat_ref[...], dy_ref[...], TN)

            @pl.when(c == 0)
            def _():
                for ch in range(N_CHIP):
                    mine_out[k, ch] = res[ch * W_OUT_ROWS:ch * W_OUT_ROWS + HALF_OUT]
                    send_out[k, ch] = res[ch * W_OUT_ROWS + HALF_OUT:(ch + 1) * W_OUT_ROWS]

            @pl.when(c == 1)
            def _():
                for ch in range(N_CHIP):
                    mine_out[k, ch] = res[ch * W_OUT_ROWS + HALF_OUT:(ch + 1) * W_OUT_ROWS]
                    send_out[k, ch] = res[ch * W_OUT_ROWS:ch * W_OUT_ROWS + HALF_OUT]

        stages = ((0, lambda j: to_sibling(j).start()), (DELAY_SUM, stage_sum), (DELAY_SECOND, stage_second),
                  (DELAY_FINAL, stage_final))
        for step in range(N_STEPS):
            @pl.when(i == step)
            def _(step=step):
                for delay, stage in stages:
                    if step - delay >= 0:
                        stage(step - delay)

        @pl.when(i == N_STEPS - 1)
        def _():
            for step in range(N_STEPS, N_STEPS + DELAY_FINAL):
                for delay, stage in stages:
                    if 0 <= step - delay < N_STEPS:
                        stage(step - delay)
            for j in range(N_STEPS - N_ITEMS, N_STEPS):
                drain(j)

    hbm = pl.BlockSpec(memory_space=pl.ANY)

    def layer(i):
        return jnp.where(i < N_ITEMS, DEPTH - 1, 0)

    def item(i):
        return jnp.where(i < N_ITEMS, i, i - N_ITEMS)

    def whole(i):
        return (layer(i), 0, 0)

    def dproj_piece(ch):
        return pl.BlockSpec((None, SEQ, HEAD),
                            lambda i: (layer(i), 0, ch * IN_STEPS + jnp.clip(item(i) - OUT_FIRST, 0, IN_STEPS - 1)))

    def dy_quarter(i):
        return (layer(i), 0, jnp.where(item(i) < OUT_FIRST, item(i), jnp.maximum(item(i) - IN_STEPS, OUT_FIRST)))

    operand = pl.BlockSpec((None, SEQ, D_MODEL), whole)
    in_specs = [operand] + [dproj_piece(ch) for ch in range(N_CHIP)]
    in_specs += [operand, pl.BlockSpec((None, SEQ, OUT_COLS), dy_quarter), hbm, _const_in((DEPTH, 8, D_MODEL))]
    args = [h, dproj, dproj, dproj, dproj, cat, dy, pack, dmod]
    out_shape = [jax.ShapeDtypeStruct((DEPTH, D_MODEL, W_IN_COLS), F32), jax.ShapeDtypeStruct((DEPTH, 2, HALF_OUT, D_MODEL), F32),
                 jax.ShapeDtypeStruct((DEPTH, PK_ROWS, HEAD), BF16), jax.ShapeDtypeStruct((DEPTH, DM_ROWS, N_DEV, HEAD), F32)]
    out_specs = [hbm, hbm, hbm, hbm]
    in_item = lambda *lead: pltpu.VMEM(lead + (HALF_IN, HEAD), BF16)
    out_item = lambda *lead: pltpu.VMEM(lead + (HALF_OUT, OUT_COLS), BF16)
    scratch = [
        pltpu.VMEM((IN_STEPS, HALF_IN, N_CHIP * HEAD), F32), pltpu.VMEM((IN_STEPS, HALF_IN, N_CHIP * HEAD), F32),
        pltpu.VMEM((IN_STEPS, HALF_IN, N_CHIP * HEAD), F32), in_item(IN_STEPS, 3), in_item(IN_STEPS, 2), in_item(IN_STEPS),
        pltpu.VMEM((IN_STEPS, HALF_IN, HEAD), F32),
        pltpu.VMEM((OUT_STEPS, N_CHIP, HALF_OUT, OUT_COLS), F32), pltpu.VMEM((OUT_STEPS, N_CHIP, HALF_OUT, OUT_COLS), F32),
        pltpu.VMEM((OUT_STEPS, N_CHIP, HALF_OUT, OUT_COLS), F32), out_item(OUT_STEPS, 3), out_item(OUT_STEPS, 2),
        out_item(OUT_STEPS), pltpu.VMEM((OUT_STEPS, HALF_OUT, OUT_COLS), F32),
        pltpu.SemaphoreType.DMA((N_STEPS,)), pltpu.SemaphoreType.DMA((N_STEPS,)),
        pltpu.SemaphoreType.DMA((2 * N_STEPS,)), pltpu.SemaphoreType.DMA((2 * N_STEPS,)),
        pltpu.SemaphoreType.DMA((N_STEPS,)), pltpu.SemaphoreType.DMA((N_STEPS,)),
        pltpu.SemaphoreType.DMA((N_STEPS,)), pltpu.SemaphoreType.DMA((N_STEPS,)), pltpu.SemaphoreType.DMA((N_STEPS,)),
    ]
    for _ in range(DEPTH):
        scratch += [
            pltpu.VMEM((PK_HALF, HEAD), F32), pltpu.VMEM((PK_HALF, HEAD), F32), pltpu.VMEM((PK_HALF, HEAD), BF16),
            pltpu.VMEM((N_CHIP, PK_PIECE, HEAD), BF16), pltpu.VMEM((PK_PIECE, HEAD), BF16), pltpu.VMEM((PK_ROWS, HEAD), BF16),
            pltpu.VMEM((DM_ROWS, 1, HEAD), F32), pltpu.VMEM((DM_ROWS, N_DEV, HEAD), F32),
            pltpu.SemaphoreType.DMA((6,)),
            pltpu.SemaphoreType.DMA((N_CHIP,)), pltpu.SemaphoreType.DMA((N_CHIP,)),
            pltpu.SemaphoreType.DMA((N_CHIP,)), pltpu.SemaphoreType.DMA((N_CHIP,)),
            pltpu.SemaphoreType.DMA((N_DEV - 1,)), pltpu.SemaphoreType.DMA((N_DEV - 1,)),
        ]
    return pl.pallas_call(
        body, name="wgrad", grid=(N_STEPS,), in_specs=in_specs, out_specs=out_specs, out_shape=out_shape,
        scratch_shapes=scratch,
        compiler_params=pltpu.CompilerParams(dimension_semantics=("arbitrary",), vmem_limit_bytes=VMEM_LIMIT),
    )(*args)


def _adamw(w, g, m, v):
    m = ADAM_B1 * m + (1.0 - ADAM_B1) * g
    v = ADAM_B2 * v + (1.0 - ADAM_B2) * (g * g)
    m_hat = m / (1.0 - ADAM_B1 ** ADAM_STEP)
    v_hat = v / (1.0 - ADAM_B2 ** ADAM_STEP)
    delta = -ADAM_LR * (m_hat / (jnp.sqrt(v_hat) + ADAM_EPS) + ADAM_WD * w)
    return delta, m, v


def _adam_sharded(c_all, dmods, ada, w_in_set, w_out_set):
    rows = D_MODEL // ADAM_PARTS

    def body(c_ref, dm_ref, wa_ref, ma_ref, va_ref, wi_ref, gi_ref, mi_ref, vi_ref, wo_ref, go_ref, mo_ref, vo_ref,
             ga_out, da_out, ma_out, va_out, gi_out, di_out, mi_out, vi_out, go_out, do_out, mo_out, vo_out):
        l = pl.program_id(0)
        chip = 2 * lax.axis_index("x") + lax.axis_index("y")
        cv = c_ref[...]
        silu_c = (cv * _sigmoid(cv)).astype(BF16).astype(F32)
        pieces = []
        for k in range(W_ADA_COLS // HEAD):
            dk = dm_ref[l, (W_ADA_COLS // HEAD) * chip + k].astype(BF16).astype(F32)
            pieces.append(_dot_exact(silu_c, dk, TN))
        g = jnp.concatenate(pieces, axis=1)
        ga_out[...] = g
        da_out[...], ma_out[...], va_out[...] = _adamw(wa_ref[...], g, ma_ref[...], va_ref[...])
        gi_out[...] = gi_ref[...]
        go_out[...] = go_ref[...]
        di_out[...], mi_out[...], vi_out[...] = _adamw(wi_ref[...], gi_ref[...], mi_ref[...], vi_ref[...])
        do_out[...], mo_out[...], vo_out[...] = _adamw(wo_ref[...], go_ref[...], mo_ref[...], vo_ref[...])

    def blk(r, cols):
        return pl.BlockSpec((None, r, cols), lambda l, i: (l, i, 0))

    b_ada, b_in, b_out = blk(rows, W_ADA_COLS), blk(rows, W_IN_COLS), blk(W_OUT_ROWS // ADAM_PARTS, D_MODEL)
    shapes = [jax.ShapeDtypeStruct(a[0].shape, F32) for a in (ada, w_in_set, w_out_set)]
    return pl.pallas_call(
        body, name="adam_sharded", grid=(DEPTH, ADAM_PARTS),
        in_specs=[pl.BlockSpec((N_DEV, rows), lambda l, i: (0, i)), _const_in((DEPTH, DM_ROWS, N_DEV, HEAD))]
        + [b_ada] * 3 + [b_in] * 4 + [b_out] * 4,
        out_specs=[b_ada] * 4 + [b_in] * 4 + [b_out] * 4,
        out_shape=[shapes[0]] * 4 + [shapes[1]] * 4 + [shapes[2]] * 4,
        compiler_params=pltpu.CompilerParams(dimension_semantics=("arbitrary", "arbitrary"), vmem_limit_bytes=VMEM_LIMIT),
    )(c_all, dmods, *ada, *w_in_set, *w_out_set)


def _adam_small(packs, dmods, weights, ms, vs):
    n = len(weights)

    def body(*refs):
        dm_refs = refs[1]
        b = 2
        w_refs, m_refs, v_refs = refs[b:b + n], refs[b + n:b + 2 * n], refs[b + 2 * n:b + 3 * n]
        outs = refs[b + 3 * n:b + 3 * n + 4 * n + 1]
        pack_refs = refs[-1]
        pack_refs[...] = refs[0][...].astype(F32)
        g_refs, d_refs, nm_refs, nv_refs = outs[0:n], outs[n:2 * n], outs[2 * n:3 * n], outs[3 * n:4 * n]
        squares = dm_refs[DEPTH - 1, DM_LOSS]
        total = squares[0:1, 0:1]
        for d in range(1, N_DEV):
            total = total + squares[d:d + 1, 0:1]
        outs[4 * n][...] = total * (0.5 / D_MODEL)

        def lanes(l, row0, count):
            return jnp.concatenate([pack_refs.at[l][row0 + k:row0 + k + 1, :] for k in range(count)], axis=1)

        def update(idx, at, g):
            g_refs[idx][at] = g
            d_refs[idx][at], nm_refs[idx][at], nv_refs[idx][at] = _adamw(w_refs[idx][at], g, m_refs[idx][at], v_refs[idx][at])

        for l in range(DEPTH):
            row = (slice(l, l + 1), slice(None))
            g_b = None
            for d in range(N_DEV):
                part = dm_refs.at[l][0:DM_LOSS, d, :]
                g_b = part if g_b is None else g_b + part
            update(0, row, jnp.concatenate([g_b[k:k + 1, :] for k in range(DM_LOSS)], axis=1))
            for g in range(N_HEAD):
                update(1, (l, g), pack_refs.at[l][PK_W_POOL + g * HEAD:PK_W_POOL + (g + 1) * HEAD, :])
                update(5, (l, g), pack_refs.at[l][PK_W_SGU + g * HEAD:PK_W_SGU + (g + 1) * HEAD, :])
            update(2, row, lanes(l, PK_POOL_SCALE, N_HEAD))
            update(3, (l,), pack_refs.at[l][PK_SGU_LN_G:PK_SGU_LN_G + N_HEAD, :])
            update(4, (l,), pack_refs.at[l][PK_SGU_LN_B:PK_SGU_LN_B + N_HEAD, :])
            update(6, (l,), pack_refs.at[l][PK_B_SGU:PK_B_SGU + N_HEAD, :])
            update(7, row, lanes(l, PK_LN_G, D_MODEL // HEAD))
            update(8, row, lanes(l, PK_LN_B, D_MODEL // HEAD))

    vmem = pl.BlockSpec(memory_space=pltpu.VMEM)
    shapes = [jax.ShapeDtypeStruct(w.shape, F32) for w in weights]
    return pl.pallas_call(
        body, name="adam_small", in_specs=[vmem] * (2 + 3 * n), out_specs=[vmem] * (4 * n + 1),
        out_shape=shapes * 4 + [jax.ShapeDtypeStruct((1, 1), F32)],
        scratch_shapes=[pltpu.VMEM(packs.shape, F32)],
        compiler_params=pltpu.CompilerParams(vmem_limit_bytes=VMEM_LIMIT),
    )(packs, dmods, *weights, *ms, *vs)


def kernel(x, c, w_ada, b_ada, w_in, w_pool, pool_scale, sgu_ln_g, sgu_ln_b, w_sgu, b_sgu, w_out, ln_g, ln_b, loss_target, m_w_ada, m_b_ada, m_w_in, m_w_pool, m_pool_scale, m_sgu_ln_g, m_sgu_ln_b, m_w_sgu, m_b_sgu, m_w_out, m_ln_g, m_ln_b, v_w_ada, v_b_ada, v_w_in, v_w_pool, v_pool_scale, v_sgu_ln_g, v_sgu_ln_b, v_w_sgu, v_b_sgu, v_w_out, v_ln_g, v_ln_b):
    small = (w_pool, pool_scale, sgu_ln_g, sgu_ln_b, w_sgu, b_sgu)
    *saved0, w_in0, w_out0, w_in1, w_out1, mod, c_all = _forward_first(x, c, w_ada, b_ada, w_in, w_out, small, ln_g, ln_b)
    *saved1, dout, sq = _forward_last(saved0[3], mod, w_in1, w_out1, small, ln_g, ln_b, loss_target)

    dx1, *shared = _backward_layer(1, dout, saved1, mod, w_in1, w_out1, small, ln_g, sq=sq)
    dx0, h, cat, dy, dproj, pack, dmod = _backward_layer(0, dx1, saved0, mod, w_in0, w_out0, small, ln_g, shared=shared)
    g_in, g_out, pack, dmods = _wgrad_reduce(h, dproj, cat, dy, pack, dmod)

    g_out = g_out.reshape(DEPTH, W_OUT_ROWS, D_MODEL)
    big = _adam_sharded(c_all, dmods, (w_ada, m_w_ada, v_w_ada), (w_in, g_in, m_w_in, v_w_in), (w_out, g_out, m_w_out, v_w_out))
    ada, win, wout = big[0:4], big[4:8], big[8:12]
    small_w = (b_ada, w_pool, pool_scale, sgu_ln_g, sgu_ln_b, w_sgu, b_sgu, ln_g, ln_b)
    small_m = (m_b_ada, m_w_pool, m_pool_scale, m_sgu_ln_g, m_sgu_ln_b, m_w_sgu, m_b_sgu, m_ln_g, m_ln_b)
    small_v = (v_b_ada, v_w_pool, v_pool_scale, v_sgu_ln_g, v_sgu_ln_b, v_w_sgu, v_b_sgu, v_ln_g, v_ln_b)
    res = _adam_small(pack, dmods, small_w, small_m, small_v)
    n = len(small_w)
    loss = res[4 * n].reshape(())

    def ordered(k):
        s = res[k * n:(k + 1) * n]
        return (ada[k], s[0], win[k], s[1], s[2], s[3], s[4], s[5], s[6], wout[k], s[7], s[8])

    return (loss, dx0[None], *ordered(0), *ordered(1), *ordered(2), *ordered(3))
```

```python
import jax
import jax.numpy as jnp
from jax import lax
from jax.experimental import pallas as pl
from jax.experimental.pallas import tpu as pltpu

F32 = jnp.float32
BF16 = jnp.bfloat16
MESH = pl.DeviceIdType.MESH

N_DEV = 8
N_CHIP = 4
DEPTH = 2
SEQ = 2048
D_MODEL = 1024
D_POOL = 512
D_PROJ = 2560
HEAD = 128
N_HEAD = 4
ROWS = 256
N_TILE = SEQ // ROWS
HALO = 16
W_IN_COLS = D_PROJ // N_CHIP
W_OUT_ROWS = D_MODEL // N_CHIP
W_ADA_COLS = 3 * D_MODEL // N_CHIP
HALF_IN = D_MODEL // 2
HALF_OUT = W_OUT_ROWS // 2
DEEPNORM_ALPHA = (2.0 * DEPTH) ** 0.25
LN_EPS = 1e-5
INV_SQRT2 = 0.7071067811865476
INV_SQRT_2PI = 0.3989422804014327

ADAM_LR = 0.001
ADAM_B1 = 0.9
ADAM_B2 = 0.999
ADAM_EPS = 1e-08
ADAM_WD = 0.01
ADAM_STEP = 10
ADAM_PARTS = 2

PK_W_POOL = 0
PK_W_SGU = 512
PK_POOL_SCALE = 1024
PK_SGU_LN_G = 1032
PK_SGU_LN_B = 1040
PK_B_SGU = 1048
PK_LN_G = 1056
PK_LN_B = 1064
PK_ROWS = 1152
PK_HALF = PK_ROWS // 2
PK_PIECE = PK_HALF // N_CHIP
DM_LOSS = 3 * D_MODEL // HEAD
DM_ROWS = DM_LOSS + 1

VMEM_LIMIT = 56 * 1024 * 1024

GATHER_SECOND_ROUND_STEP = 0
GATHER_PASS_STEP = N_TILE - 3

NN = (((1,), (0,)), ((), ()))
NT = (((1,), (1,)), ((), ()))
TN = (((0,), (0,)), ((), ()))


def _dot(a, b, dims=NN):
    return lax.dot_general(a, b, dims, preferred_element_type=F32)


def _dot_exact(a, b, dims=NN):
    return lax.dot_general(a, b, dims, preferred_element_type=F32, precision=lax.Precision.HIGHEST)


def _layer_norm(v):
    mu = jnp.mean(v, axis=-1, keepdims=True)
    d = v - mu
    var = jnp.mean(d * d, axis=-1, keepdims=True)
    rstd = lax.rsqrt(var + LN_EPS)
    return d * rstd, rstd


def _layer_norm_bwd(dvhat, vhat, rstd):
    m1 = jnp.mean(dvhat, axis=-1, keepdims=True)
    m2 = jnp.mean(dvhat * vhat, axis=-1, keepdims=True)
    return rstd * (dvhat - m1 - vhat * m2)


def _sigmoid(v):
    return 1.0 / (1.0 + jnp.exp(-v))


def _gelu_parts(v):
    phi = 0.5 * (1.0 + lax.erf(v * INV_SQRT2))
    pdf = INV_SQRT_2PI * jnp.exp(-0.5 * v * v)
    return phi, pdf


def _sum_rows(v):
    return jnp.sum(v, axis=0, keepdims=True)


def _window_sums(ext, toward_later):
    n = ext.shape[0]

    def shifted(v, k):
        return pltpu.roll(v, (n - k) if toward_later else k, 0)

    s2 = ext + shifted(ext, 1)
    r4 = s2[:, HEAD:]
    s4 = r4 + shifted(r4, 2)
    r8 = s4[:, HEAD:]
    s8 = r8 + shifted(r8, 4)
    r16 = s8[:, HEAD:]
    s16 = r16 + shifted(r16, 8)
    return jnp.concatenate([s2[:, :HEAD], s4[:, :HEAD], s8[:, :HEAD], s16], axis=1)


def _window_counts(row0):
    t1 = row0 + 1 + lax.broadcasted_iota(jnp.int32, (ROWS, D_POOL), 0)
    lane = lax.broadcasted_iota(jnp.int32, (ROWS, D_POOL), 1)
    width = jnp.where(lane < HEAD, 2, jnp.where(lane < 2 * HEAD, 4, jnp.where(lane < 3 * HEAD, 8, 16)))
    return jnp.minimum(t1, width).astype(F32)


def _causal_mask():
    r = lax.broadcasted_iota(jnp.int32, (HEAD, HEAD), 0)
    s = lax.broadcasted_iota(jnp.int32, (HEAD, HEAD), 1)
    return r >= s


def _chunks_to_lanes(v):
    return jnp.concatenate([v[n * HEAD:(n + 1) * HEAD] for n in range(ROWS // HEAD)], axis=1)


def _lanes_to_chunks(v):
    return jnp.concatenate([v[:, n * HEAD:(n + 1) * HEAD] for n in range(ROWS // HEAD)], axis=0)


def _pack_stats(rstd_x, rstd_z, rstd_v):
    lane = lax.broadcasted_iota(jnp.int32, (ROWS, HEAD), 1)
    packed = rstd_x
    for k, r in enumerate([rstd_z] + list(rstd_v)):
        packed = jnp.where(lane < 16 * (k + 1), packed, r)
    return packed


def _unpack_stats(stats):
    cols = [stats[:, 16 * k:16 * k + 1] for k in range(2 + N_HEAD)]
    return cols[0], cols[1], cols[2:]


def _mixer(proj, halo, row0, wpool_ref, pscale, sgu_g_ref, sgu_b_ref, wsgu_ref, bsgu_ref, saved=None):
    xa = proj[:, 0:512]
    ga = proj[:, 512:1024]
    u = proj[:, 1024:1536]
    v = proj[:, 1536:2048]
    gb = proj[:, 2048:2560]
    ext = jnp.concatenate([halo, xa], axis=0)
    win = _window_sums(ext, toward_later=False)[HALO:]
    cnt = _window_counts(row0)
    pooled = (win / cnt - xa).astype(BF16)
    pw = jnp.concatenate(
        [_dot(pooled[:, g * HEAD:(g + 1) * HEAD], wpool_ref[g].astype(BF16)) for g in range(N_HEAD)], axis=1)
    sig_a = _sigmoid(ga) if saved is None else saved["sig_a"]
    ya = pw * pscale * (ga * sig_a)
    phi_u, pdf_u = _gelu_parts(u)
    phi_v, pdf_v = _gelu_parts(v)
    gu = u * phi_u
    gv = v * phi_v
    sig_b = _sigmoid(gb) if saved is None else saved["sig_b"]
    silu_b = gb * sig_b
    mask = _causal_mask()
    diag = lax.broadcasted_iota(jnp.int32, (HEAD, HEAD), 0) == lax.broadcasted_iota(jnp.int32, (HEAD, HEAD), 1)
    vhat, rstd_v, vln_l, mixed = [], [], [], []
    for h in range(N_HEAD):
        if saved is None:
            vh, rh = _layer_norm(gv[:, h * HEAD:(h + 1) * HEAD])
        else:
            vh, rh = saved["vhat"][h], saved["rstd_v"][h]
        ln = (vh * sgu_g_ref[h:h + 1, :] + sgu_b_ref[h:h + 1, :]).astype(BF16)
        ln_l = _chunks_to_lanes(ln)
        wm = jnp.where(mask, wsgu_ref[h], 0.0).astype(BF16)
        bias = jnp.sum(jnp.where(diag, jnp.broadcast_to(bsgu_ref[h:h + 1, :], (HEAD, HEAD)), 0.0), axis=1, keepdims=True)
        mx = _lanes_to_chunks(_dot(wm, ln_l) + bias)
        vhat.append(vh)
        rstd_v.append(rh)
        vln_l.append(ln_l)
        mixed.append(mx)
    mixed = jnp.concatenate(mixed, axis=1)
    yb = gu * mixed * silu_b
    return dict(xa=xa, ga=ga, u=u, v=v, gb=gb, cnt=cnt, pooled=pooled, pw=pw, sig_a=sig_a, ya=ya, phi_u=phi_u, pdf_u=pdf_u,
                phi_v=phi_v, pdf_v=pdf_v, gu=gu, sig_b=sig_b, silu_b=silu_b, vhat=vhat, rstd_v=rstd_v, vln_l=vln_l,
                mixed=mixed, yb=yb, mask=mask)


def _const(shape, *index):
    lead = tuple(index) + (0,) * (len(shape) - len(index))
    return pl.BlockSpec(shape, lambda *_: lead)


def _const_in(shape, *index):
    lead = tuple(index) + (0,) * (len(shape) - len(index))
    return pl.BlockSpec(shape, lambda *_: lead, pipeline_mode=pl.Buffered(1))


def _layer_weight_specs(l):
    return [
        _const_in((None, N_HEAD, HEAD, HEAD), l),
        _const_in((DEPTH, D_POOL)),
        _const_in((None, N_HEAD, HEAD), l),
        _const_in((None, N_HEAD, HEAD), l),
        _const_in((None, N_HEAD, HEAD, HEAD), l),
        _const_in((None, N_HEAD, HEAD), l),
    ]


def _forward_tile(l, i, x_ref, mod_ref, win_ref, wout_ref, small_refs, lng_ref, lnb_ref, carry_ref, saved_refs):
    wpool_ref, pscale_ref, sgu_g_ref, sgu_b_ref, wsgu_ref, bsgu_ref = small_refs
    proj_ref, y_ref, xn_ref, zn_ref, stats_ref, sig_ref, vhat_ref = saved_refs
    x = x_ref[...]
    if l > 0:
        x = x * lng_ref[l - 1:l, :] + lnb_ref[l - 1:l, :]
    shift, scale, gate = mod_ref[0:1, :], mod_ref[1:2, :], mod_ref[2:3, :]
    xn, rstd_x = _layer_norm(x)
    xn_ref[...] = xn.astype(xn_ref.dtype)
    h = xn * (1.0 + scale) + shift
    proj = _dot(h.astype(BF16), win_ref[...])
    proj_ref[...] = proj.astype(proj_ref.dtype)
    m = _mixer(proj, carry_ref[...], i * ROWS, wpool_ref, pscale_ref[l:l + 1, :], sgu_g_ref, sgu_b_ref, wsgu_ref, bsgu_ref)
    carry_ref[...] = m["xa"][ROWS - HALO:]
    sig_ref[...] = jnp.concatenate([m["sig_a"], m["sig_b"]], axis=1).astype(sig_ref.dtype)
    vhat_ref[...] = jnp.concatenate(m["vhat"], axis=1).astype(vhat_ref.dtype)
    cat = jnp.concatenate([m["ya"], m["yb"]], axis=1).astype(BF16)
    y = _dot(cat, wout_ref[...])
    y_ref[...] = y.astype(y_ref.dtype)
    zn, rstd_z = _layer_norm(DEEPNORM_ALPHA * x + gate * y)
    zn_ref[...] = zn
    stats_ref[...] = _pack_stats(rstd_x, rstd_z, m["rstd_v"])
    return zn


SAVED_COLS = (D_PROJ, D_MODEL, D_MODEL, D_MODEL, HEAD, D_MODEL, D_POOL)
SAVED_TYPES = (BF16, BF16, BF16, F32, F32, BF16, BF16)


def _saved_outputs():
    return ([jax.ShapeDtypeStruct((SEQ, cols), t) for cols, t in zip(SAVED_COLS, SAVED_TYPES)],
            [pl.BlockSpec((ROWS, cols), lambda i: (i, 0)) for cols in SAVED_COLS])


def _forward_last(zn_prev, mod, w_in, w_out, small, ln_g, ln_b, target):
    l = DEPTH - 1
    n_saved = len(SAVED_COLS)

    def body(*refs):
        x_ref, mod_ref, win_ref, wout_ref = refs[:4]
        small_refs, lng_ref, lnb_ref, tgt_ref = refs[4:10], refs[10], refs[11], refs[12]
        saved_refs = refs[13:13 + n_saved]
        loss_ref, carry_ref = refs[13 + n_saved:]
        i = pl.program_id(0)

        @pl.when(i == 0)
        def _():
            carry_ref[...] = jnp.zeros_like(carry_ref)
            loss_ref[...] = jnp.zeros_like(loss_ref)

        zn = _forward_tile(l, i, x_ref, mod_ref, win_ref, wout_ref, small_refs, lng_ref, lnb_ref, carry_ref, saved_refs)
        err = zn * lng_ref[l:l + 1, :] + lnb_ref[l:l + 1, :] - tgt_ref[...]
        loss_ref[...] += jnp.sum(err * err)

    tile = pl.BlockSpec((ROWS, D_MODEL), lambda i: (i, 0))
    tile3 = pl.BlockSpec((None, ROWS, D_MODEL), lambda i: (0, i, 0))
    in_specs = [tile, _const_in((None, 8, D_MODEL), l), _const_in((D_MODEL, D_PROJ)), _const_in((D_MODEL, D_MODEL))]
    in_specs += _layer_weight_specs(l) + [_const_in((DEPTH, D_MODEL)), _const_in((DEPTH, D_MODEL)), tile3]
    out_shape, out_specs = _saved_outputs()
    out_shape += [jax.ShapeDtypeStruct((8, HEAD), F32)]
    out_specs += [_const((8, HEAD))]
    return pl.pallas_call(
        body, name="fwd_last", grid=(N_TILE,), in_specs=in_specs, out_specs=out_specs, out_shape=out_shape,
        scratch_shapes=[pltpu.VMEM((HALO, D_POOL), F32)],
        compiler_params=pltpu.CompilerParams(dimension_semantics=("arbitrary",), vmem_limit_bytes=VMEM_LIMIT),
    )(zn_prev, mod, w_in, w_out, *small, ln_g, ln_b, target)


def _backward_layer(l, dout, saved, mod, w_in, w_out, small, ln_g, sq=None, ln_b=None, shared=None):
    has_loss = sq is not None

    def body(*refs):
        (dout_ref, proj_ref, y_ref, xn_ref, zn_ref, stats_ref, sig_ref, vhat_ref, halo_ref, mod_ref, win_ref, wout_ref,
         wpool_ref, pscale_ref, sgu_g_ref, sgu_b_ref, wsgu_ref, bsgu_ref, lng_ref) = refs[:19]
        n_in = 21 if has_loss else 19 + 6
        dx_ref, h_ref, cat_ref, dy_ref, dproj_ref, pack_ref, dmod_ref, carry_ref = refs[n_in:n_in + 8]
        i = pl.program_id(0)
        tile = N_TILE - 1 - i

        @pl.when(i == 0)
        def _():
            carry_ref[...] = jnp.zeros_like(carry_ref)
            pack_ref[...] = jnp.zeros_like(pack_ref)
            dmod_ref[...] = jnp.zeros_like(dmod_ref)
            if has_loss:
                dmod_ref[3:4, 0:HEAD] = refs[19][0:1, :]

        xn = xn_ref[...].astype(F32)
        zn = zn_ref[...]
        y = y_ref[...].astype(F32)
        if has_loss:
            dout = (zn * lng_ref[l:l + 1, :] + refs[20][l:l + 1, :] - dout_ref[...]) * (1.0 / D_MODEL)
        else:
            dout = dout_ref[...]
        rstd_x, rstd_z, rstd_v = _unpack_stats(stats_ref[...])
        kept = dict(sig_a=sig_ref[:, :D_POOL].astype(F32), sig_b=sig_ref[:, D_POOL:].astype(F32), rstd_v=rstd_v,
                    vhat=[vhat_ref[:, hd * HEAD:(hd + 1) * HEAD].astype(F32) for hd in range(N_HEAD)])
        pscale = pscale_ref[l:l + 1, :]
        shift, scale, gate = mod_ref[0:1, :], mod_ref[1:2, :], mod_ref[2:3, :]
        h = xn * (1.0 + scale) + shift
        h_ref[...] = h.astype(BF16)
        g_ln_g = _sum_rows(dout * zn)
        g_ln_b = _sum_rows(dout)
        dz = _layer_norm_bwd(dout * lng_ref[l:l + 1, :], zn, rstd_z)
        d_gate = _sum_rows(dz * y)
        dy = (gate * dz).astype(BF16)
        dy_ref[...] = dy

        halo = jnp.where(tile > 0, halo_ref[...].astype(F32), 0.0)
        m = _mixer(proj_ref[...].astype(F32), halo, tile * ROWS, wpool_ref, pscale, sgu_g_ref, sgu_b_ref, wsgu_ref, bsgu_ref,
                   saved=kept)
        cat_ref[...] = jnp.concatenate([m["ya"], m["yb"]], axis=1).astype(BF16)
        dcat = _dot(dy, wout_ref[...], NT)
        dya = dcat[:, :D_POOL]
        dyb = dcat[:, D_POOL:]

        ga, sig_a = m["ga"], m["sig_a"]
        dp = dya * (ga * sig_a)
        d_ga = dya * (m["pw"] * pscale) * (sig_a * (1.0 + ga * (1.0 - sig_a)))
        g_pscale = _sum_rows(dp * m["pw"])
        dpw = (dp * pscale).astype(BF16)
        dpooled = []
        for g in range(N_HEAD):
            cols = slice(g * HEAD, (g + 1) * HEAD)
            pack_ref[PK_W_POOL + g * HEAD:PK_W_POOL + (g + 1) * HEAD, :] += _dot(m["pooled"][:, cols], dpw[:, cols], TN)
            dpooled.append(_dot(dpw[:, cols], wpool_ref[g].astype(BF16), NT))
        dpooled = jnp.concatenate(dpooled, axis=1)
        q = dpooled / m["cnt"]
        ext = jnp.concatenate([q, carry_ref[...]], axis=0)
        d_xa = _window_sums(ext, toward_later=True)[:ROWS] - dpooled
        carry_ref[...] = q[:HALO]

        gu, mixed, silu_b, gb, sig_b = m["gu"], m["mixed"], m["silu_b"], m["gb"], m["sig_b"]
        d_mixed = dyb * gu * silu_b
        d_gu = dyb * mixed * silu_b
        d_gb = dyb * gu * mixed * (sig_b * (1.0 + gb * (1.0 - sig_b)))
        d_u = d_gu * (m["phi_u"] + m["u"] * m["pdf_u"])
        ones = jnp.ones((8, HEAD), F32)
        d_v = []
        for hd in range(N_HEAD):
            cols = slice(hd * HEAD, (hd + 1) * HEAD)
            dm = d_mixed[:, cols]
            dm_l = _chunks_to_lanes(dm.astype(BF16))
            g_w = _dot(dm_l, m["vln_l"][hd], NT)
            pack_ref[PK_W_SGU + hd * HEAD:PK_W_SGU + (hd + 1) * HEAD, :] += jnp.where(m["mask"], g_w, 0.0)
            dm_sum = dm[0:HEAD]
            for n in range(1, ROWS // HEAD):
                dm_sum = dm_sum + dm[n * HEAD:(n + 1) * HEAD]
            pack_ref[PK_B_SGU + hd:PK_B_SGU + hd + 1, :] += _dot_exact(ones, dm_sum, NT)[0:1]
            wm = jnp.where(m["mask"], wsgu_ref[hd], 0.0).astype(BF16)
            d_vln = _lanes_to_chunks(_dot(wm, dm_l, TN))
            vhat = m["vhat"][hd]
            pack_ref[PK_SGU_LN_G + hd:PK_SGU_LN_G + hd + 1, :] += _sum_rows(d_vln * vhat)
            pack_ref[PK_SGU_LN_B + hd:PK_SGU_LN_B + hd + 1, :] += _sum_rows(d_vln)
            d_v.append(_layer_norm_bwd(d_vln * sgu_g_ref[hd:hd + 1, :], vhat, m["rstd_v"][hd]))
        v = m["v"]
        d_v = jnp.concatenate(d_v, axis=1) * (m["phi_v"] + v * m["pdf_v"])

        dproj = jnp.concatenate([d_xa, d_ga, d_u, d_v, d_gb], axis=1).astype(BF16)
        dproj_ref[...] = dproj
        dh = _dot(dproj, win_ref[...], NT)
        d_scale = _sum_rows(dh * xn)
        d_shift = _sum_rows(dh)
        dx_ref[...] = DEEPNORM_ALPHA * dz + _layer_norm_bwd(dh * (1.0 + scale), xn, rstd_x)

        dmod_ref[0:1, :] += d_shift
        dmod_ref[1:2, :] += d_scale
        dmod_ref[2:3, :] += d_gate
        for g in range(N_HEAD):
            pack_ref[PK_POOL_SCALE + g:PK_POOL_SCALE + g + 1, :] += g_pscale[:, g * HEAD:(g + 1) * HEAD]
        for k in range(D_MODEL // HEAD):
            pack_ref[PK_LN_G + k:PK_LN_G + k + 1, :] += g_ln_g[:, k * HEAD:(k + 1) * HEAD]
            pack_ref[PK_LN_B + k:PK_LN_B + k + 1, :] += g_ln_b[:, k * HEAD:(k + 1) * HEAD]

    def rev(i):
        return (N_TILE - 1 - i, 0)

    tile = pl.BlockSpec((ROWS, D_MODEL), rev)
    halo = pl.BlockSpec((HALO, D_POOL), lambda i: (jnp.maximum((N_TILE - 1 - i) * (ROWS // HALO) - 1, 0), 0))
    in_specs = [tile] + [pl.BlockSpec((ROWS, a.shape[1]), rev) for a in saved] + [halo]
    in_specs += [_const_in((None, 8, D_MODEL), l), _const_in((D_MODEL, D_PROJ)), _const_in((D_MODEL, D_MODEL))]
    in_specs += _layer_weight_specs(l) + [_const_in((DEPTH, D_MODEL))]
    args = [dout, *saved, saved[0], mod, w_in, w_out, *small, ln_g]
    stacked = lambda cols: pl.BlockSpec((None, ROWS, cols), lambda i: (l, N_TILE - 1 - i, 0))
    out_shape = [jax.ShapeDtypeStruct((SEQ, D_MODEL), F32), jax.ShapeDtypeStruct((DEPTH, SEQ, D_MODEL), BF16),
                 jax.ShapeDtypeStruct((DEPTH, SEQ, D_MODEL), BF16), jax.ShapeDtypeStruct((DEPTH, SEQ, D_MODEL), BF16),
                 jax.ShapeDtypeStruct((DEPTH, SEQ, D_PROJ), BF16), jax.ShapeDtypeStruct((DEPTH, PK_ROWS, HEAD), F32),
                 jax.ShapeDtypeStruct((DEPTH, 8, D_MODEL), F32)]
    out_specs = [tile, stacked(D_MODEL), stacked(D_MODEL), stacked(D_MODEL), stacked(D_PROJ),
                 _const((None, PK_ROWS, HEAD), l), _const((None, 8, D_MODEL), l)]
    aliases = {}
    if has_loss:
        in_specs[0] = pl.BlockSpec((None, ROWS, D_MODEL), lambda i: (0, N_TILE - 1 - i, 0))
        in_specs += [_const_in((8, HEAD)), _const_in((DEPTH, D_MODEL))]
        args += [sq, ln_b]
    else:
        aliases = {len(args) + k: 1 + k for k in range(len(shared))}
        in_specs += [pl.BlockSpec(memory_space=pl.ANY)] * len(shared)
        args += list(shared)
    return pl.pallas_call(
        body, name="bwd_last" if has_loss else "bwd_first", grid=(N_TILE,), in_specs=in_specs, out_specs=out_specs,
        out_shape=out_shape, scratch_shapes=[pltpu.VMEM((HALO, D_POOL), F32)], input_output_aliases=aliases,
        compiler_params=pltpu.CompilerParams(dimension_semantics=("arbitrary",), vmem_limit_bytes=VMEM_LIMIT),
    )(*args)


def _flip(v, f):
    return v + f - 2 * v * f


class _Place:
    def __init__(self):
        x, y, c = lax.axis_index("x"), lax.axis_index("y"), lax.axis_index("c")
        self.x, self.y, self.c = x, y, c
        self.chip = 2 * x + y
        self.dev = 4 * x + 2 * y + c
        self.sibling = (x, y, 1 - c)
        x1, y1 = _flip(x, 1 - c), _flip(y, c)
        x2, y2 = _flip(x, c), _flip(y, 1 - c)
        self.first = (x1, y1, c)
        self.second = (x2, y2, c)
        self.chip_first = 2 * x1 + y1
        self.chip_second = 2 * x2 + y2
        self.chip_far = 2 * (1 - x) + (1 - y)
        self.my_first_coord = jnp.where(c == 0, x, y)

    def first_coord(self, ch):
        return jnp.where(self.c == 0, ch // 2, ch % 2)

    def others(self):
        return [(_flip(self.x, (r >> 2) & 1), _flip(self.y, (r >> 1) & 1), _flip(self.c, r & 1)) for r in range(1, N_DEV)]

    def other_chips(self):
        return [(1 - self.x, self.y), (self.x, 1 - self.y), (1 - self.x, 1 - self.y)]


class _WeightGather:
    CHUNKS = 2
    N_SEMS = 12 * CHUNKS

    def __init__(self, place, win, wout, send, recv):
        self.p, self.win, self.wout, self.send, self.recv = place, win, wout, send, recv
        p = place
        self.plan = [(p.chip, p.first), (p.chip, p.second), (p.chip_first, p.second),
                     (p.chip_first, p.sibling), (p.chip_second, p.sibling), (p.chip_far, p.sibling)]

    def _copies(self, k, q):
        ch, target = self.plan[k]
        n_in, n_out = HALF_IN // self.CHUNKS, HALF_OUT // self.CHUNKS
        rows_in = pl.ds(pl.multiple_of(self.p.c * HALF_IN + q * n_in, n_in), n_in)
        cols_in = pl.ds(pl.multiple_of(ch * W_IN_COLS, 128), W_IN_COLS)
        rows_out = pl.ds(pl.multiple_of(ch * W_OUT_ROWS + self.p.c * HALF_OUT + q * n_out, n_out), n_out)
        r_in = self.win.at[rows_in, cols_in]
        r_out = self.wout.at[rows_out, :]
        s = 2 * (6 * q + k)
        return [pltpu.make_async_remote_copy(r_in, r_in, self.send.at[s], self.recv.at[s],
                                             device_id=target, device_id_type=MESH),
                pltpu.make_async_remote_copy(r_out, r_out, self.send.at[s + 1], self.recv.at[s + 1],
                                             device_id=target, device_id_type=MESH)]

    def _start(self, k, q):
        for cp in self._copies(k, q):
            cp.start()

    def _landed(self, k, q):
        for cp in self._copies(k, q):
            cp.wait_recv()

    def start_first_round(self):
        for q in range(self.CHUNKS):
            self._start(0, q)
        for q in range(self.CHUNKS):
            self._start(1, q)

    def start_second_round(self, q):
        self._landed(0, q)
        self._start(2, q)
        self._start(3, q)

    def pass_second_round(self, q):
        self._landed(1, q)
        self._start(4, q)
        self._landed(2, q)
        self._start(5, q)

    def finish(self):
        for q in range(self.CHUNKS):
            for k in (3, 4, 5):
                self._landed(k, q)
        for q in range(self.CHUNKS):
            for k in range(len(self.plan)):
                for cp in self._copies(k, q):
                    cp.wait_send()


def _forward_first(x, c_vec, w_ada, b_ada, w_in, w_out, small, ln_g, ln_b):
    n_saved = len(SAVED_COLS)

    def body(*refs):
        x_ref, c_ref, wada_hbm, bada_ref, win_hbm, wout_hbm = refs[:6]
        small_refs, lng_ref, lnb_ref = refs[6:12], refs[12], refs[13]
        saved_refs = refs[14:14 + n_saved]
        win0, wout0, win1, wout1, mod_out, c_out = refs[14 + n_saved:20 + n_saved]
        (carry_ref, wada_ref, win_ref, wout_ref, win_bf, wout_bf, mod_mine, mod_all, c_all, mod_ref, win_v, wout_v,
         g0_send, g0_recv, g1_send, g1_recv, c_send, c_recv, mod_send, mod_recv, local_sem) = refs[20 + n_saved:]
        i = pl.program_id(0)
        p = _Place()
        gather0 = _WeightGather(p, win_v, wout_v, g0_send, g0_recv)
        keep = [pltpu.make_async_copy(win_v, win0, local_sem.at[9]), pltpu.make_async_copy(wout_v, wout0, local_sem.at[10])]
        gather1 = _WeightGather(p, win1, wout1, g1_send, g1_recv)

        @pl.when(i == 0)
        def _():
            carry_ref[...] = jnp.zeros_like(carry_ref)
            loads = [pltpu.make_async_copy(win_hbm.at[0], win_ref.at[0], local_sem.at[4]),
                     pltpu.make_async_copy(wout_hbm.at[0], wout_ref.at[0], local_sem.at[5]),
                     pltpu.make_async_copy(win_hbm.at[1], win_ref.at[1], local_sem.at[6]),
                     pltpu.make_async_copy(wout_hbm.at[1], wout_ref.at[1], local_sem.at[7]),
                     pltpu.make_async_copy(wada_hbm, wada_ref, local_sem.at[8])]
            for cp in loads:
                cp.start()

            c_all[pl.ds(p.dev, 1), :] = c_ref[...]
            mine = c_all.at[pl.ds(p.dev, 1), :]
            c_copies = [pltpu.make_async_remote_copy(mine, mine, c_send.at[r], c_recv.at[r], device_id=d, device_id_type=MESH)
                        for r, d in enumerate(p.others())]
            for cp in c_copies:
                cp.start()

            cols = pl.ds(pl.multiple_of(p.chip * W_IN_COLS, 128), W_IN_COLS)
            rows = pl.ds(pl.multiple_of(p.chip * W_OUT_ROWS, W_OUT_ROWS), W_OUT_ROWS)
            own = [pltpu.make_async_copy(win_bf.at[0], win_v.at[:, cols], local_sem.at[0]),
                   pltpu.make_async_copy(wout_bf.at[0], wout_v.at[rows, :], local_sem.at[1]),
                   pltpu.make_async_copy(win_bf.at[1], win1.at[:, cols], local_sem.at[2]),
                   pltpu.make_async_copy(wout_bf.at[1], wout1.at[rows, :], local_sem.at[3])]
            for l in range(DEPTH):
                loads[2 * l].wait()
                win_bf[l] = win_ref[l].astype(BF16)
                own[2 * l].start()
                loads[2 * l + 1].wait()
                wout_bf[l] = wout_ref[l].astype(BF16)
                own[2 * l + 1].start()
                if l == 0:
                    own[0].wait()
                    own[1].wait()
                    gather0.start_first_round()
            for cp in c_copies:
                cp.wait()
            loads[4].wait()

            cv = c_all[...]
            c_out[...] = cv
            silu_c = (cv * _sigmoid(cv)).astype(BF16)
            for l in range(DEPTH):
                mod_mine[l] = _dot(silu_c, wada_ref[l].astype(BF16))
            mod_all[p.chip] = mod_mine[...]
            m_copies = [pltpu.make_async_remote_copy(mod_mine, mod_all.at[p.chip], mod_send.at[k], mod_recv.at[k],
                                                     device_id=(px, py, p.c), device_id_type=MESH)
                        for k, (px, py) in enumerate(p.other_chips())]
            for cp in m_copies:
                cp.start()
            for q in range(gather0.CHUNKS):
                gather0.start_second_round(q)
            own[2].wait()
            own[3].wait()
            gather1.start_first_round()
            for cp in m_copies:
                cp.wait()
            mod_ref[...] = jnp.zeros_like(mod_ref)
            for l in range(DEPTH):
                full = jnp.concatenate([mod_all[ch, l, pl.ds(p.dev, 1), :] for ch in range(N_CHIP)], axis=1) + bada_ref[l:l + 1, :]
                for k in range(3):
                    mod_ref[l, k:k + 1, :] = full[:, k * D_MODEL:(k + 1) * D_MODEL]
            mod_out[...] = mod_ref[...]
            for q in range(gather0.CHUNKS):
                gather0.pass_second_round(q)
            gather0.finish()
            for cp in keep:
                cp.start()

        for q in range(_WeightGather.CHUNKS):
            @pl.when(i == GATHER_SECOND_ROUND_STEP + q)
            def _(q=q):
                gather1.start_second_round(q)

            @pl.when(i == GATHER_PASS_STEP)
            def _(q=q):
                gather1.pass_second_round(q)

        _forward_tile(0, i, x_ref, mod_ref.at[0], win_v, wout_v, small_refs, lng_ref, lnb_ref, carry_ref, saved_refs)

        @pl.when(i == N_TILE - 1)
        def _():
            gather1.finish()
            for cp in keep:
                cp.wait()

    hbm = pl.BlockSpec(memory_space=pl.ANY)
    tile3 = pl.BlockSpec((None, ROWS, D_MODEL), lambda i: (0, i, 0))
    in_specs = [tile3, _const_in((1, D_MODEL)), hbm, _const_in((DEPTH, 3 * D_MODEL)), hbm, hbm]
    in_specs += _layer_weight_specs(0) + [_const_in((DEPTH, D_MODEL)), _const_in((DEPTH, D_MODEL))]
    out_shape, out_specs = _saved_outputs()
    w_in_shape = jax.ShapeDtypeStruct((D_MODEL, D_PROJ), BF16)
    w_out_shape = jax.ShapeDtypeStruct((D_MODEL, D_MODEL), BF16)
    out_shape += [w_in_shape, w_out_shape, w_in_shape, w_out_shape,
                  jax.ShapeDtypeStruct((DEPTH, 8, D_MODEL), F32), jax.ShapeDtypeStruct((N_DEV, D_MODEL), F32)]
    out_specs += [hbm, hbm, hbm, hbm, _const((DEPTH, 8, D_MODEL)), _const((N_DEV, D_MODEL))]
    gather_sems = [pltpu.SemaphoreType.DMA((_WeightGather.N_SEMS,))] * 4
    scratch = [
        pltpu.VMEM((HALO, D_POOL), F32),
        pltpu.VMEM(w_ada.shape, F32), pltpu.VMEM(w_in.shape, F32), pltpu.VMEM(w_out.shape, F32),
        pltpu.VMEM((DEPTH, D_MODEL, W_IN_COLS), BF16), pltpu.VMEM((DEPTH, W_OUT_ROWS, D_MODEL), BF16),
        pltpu.VMEM((DEPTH, N_DEV, W_ADA_COLS), F32), pltpu.VMEM((N_CHIP, DEPTH, N_DEV, W_ADA_COLS), F32),
        pltpu.VMEM((N_DEV, D_MODEL), F32), pltpu.VMEM((DEPTH, 8, D_MODEL), F32),
        pltpu.VMEM((D_MODEL, D_PROJ), BF16), pltpu.VMEM((D_MODEL, D_MODEL), BF16),
    ] + gather_sems + [
        pltpu.SemaphoreType.DMA((7,)), pltpu.SemaphoreType.DMA((7,)),
        pltpu.SemaphoreType.DMA((3,)), pltpu.SemaphoreType.DMA((3,)),
        pltpu.SemaphoreType.DMA((11,)),
    ]
    return pl.pallas_call(
        body, name="fwd_first", grid=(N_TILE,), in_specs=in_specs, out_specs=out_specs, out_shape=out_shape,
        scratch_shapes=scratch,
        compiler_params=pltpu.CompilerParams(dimension_semantics=("arbitrary",), vmem_limit_bytes=VMEM_LIMIT),
    )(x, c_vec, w_ada, b_ada, w_in, w_out, *small, ln_g, ln_b)


IN_STEPS = W_IN_COLS // HEAD
OUT_STEPS = 4
OUT_COLS = D_MODEL // OUT_STEPS
OUT_FIRST = 2
ITEMS = ([("out", k) for k in range(OUT_FIRST)] + [("in", k) for k in range(IN_STEPS)]
         + [("out", k) for k in range(OUT_FIRST, OUT_STEPS)])
N_ITEMS = len(ITEMS)
N_STEPS = DEPTH * N_ITEMS
DELAY_SUM, DELAY_SECOND, DELAY_FINAL = 1, 3, 5
SMALL_SCATTER_STEP, SMALL_GATHER_STEP, SMALL_PASS_STEP, SMALL_FINISH_STEP = 1, 3, 5, 7


def _wgrad_reduce(h, dproj, cat, dy, pack, dmod):
    def body(*refs):
        h_ref, dp_refs, cat_ref, dy_ref, pack_ref, dmod_ref = refs[0], refs[1:5], refs[5], refs[6], refs[7], refs[8]
        fin_in, fin_out, pack_out, dmod_out = refs[9:13]
        scratch = refs[13:]
        (mine_in, send_in, sib_in, st_in, r1_in, r2_in, f_in,
         mine_out, send_out, sib_out, st_out, r1_out, r2_out, f_out,
         d2d_s, d2d_r, r1_s, r1_r, r2_s, r2_r, fin_l, fin_s, fin_r) = scratch[:23]
        p = _Place()
        c = p.c
        i = pl.program_id(0)
        my_rows = pl.ds(pl.multiple_of(c * HALF_IN, HALF_IN), HALF_IN)

        def layer_of(j):
            return DEPTH - 1 - j // N_ITEMS

        def bufs(j):
            kind, k = ITEMS[j % N_ITEMS]
            if kind == "in":
                return [r.at[k] for r in (mine_in, send_in, sib_in, st_in, r1_in, r2_in, f_in)]
            return [r.at[k] for r in (mine_out, send_out, sib_out, st_out, r1_out, r2_out, f_out)]

        def piece(j, ref, ch):
            if ITEMS[j % N_ITEMS][0] == "in":
                return ref.at[:, ch * HEAD:(ch + 1) * HEAD]
            return ref.at[ch]

        def slot(ch):
            return jnp.where(c == 0, ch % 2, ch // 2)

        def to_sibling(j):
            _, send, sib, _, _, _, _ = bufs(j)
            return pltpu.make_async_remote_copy(send, sib, d2d_s.at[j], d2d_r.at[j], device_id=p.sibling, device_id_type=MESH)

        def first_round(j, ch):
            _, _, _, st, r1, _, _ = bufs(j)
            k = slot(ch)
            return pltpu.make_async_remote_copy(st.at[k], r1.at[k], r1_s.at[2 * j + k], r1_r.at[2 * j + k],
                                                device_id=p.first, device_id_type=MESH)

        def second_round(j):
            _, _, _, st, _, r2, _ = bufs(j)
            return pltpu.make_async_remote_copy(st.at[2], r2, r2_s.at[j], r2_r.at[j], device_id=p.second, device_id_type=MESH)

        def finals(j):
            f = bufs(j)[6]
            kind, k = ITEMS[j % N_ITEMS]
            if kind == "in":
                dst = fin_in.at[layer_of(j), my_rows, k * HEAD:(k + 1) * HEAD]
            else:
                dst = fin_out.at[layer_of(j), c, :, k * OUT_COLS:(k + 1) * OUT_COLS]
            return [pltpu.make_async_copy(f, dst, fin_l.at[j]),
                    pltpu.make_async_remote_copy(f, dst, fin_s.at[j], fin_r.at[j], device_id=p.sibling, device_id_type=MESH)]

        def stage_sum(j):
            mine, _, sib, st, _, _, _ = bufs(j)
            to_sibling(j).wait_recv()
            mine[...] = mine[...] + sib[...]
            for ch in range(N_CHIP):
                @pl.when(p.first_coord(ch) != p.my_first_coord)
                def _(ch=ch):
                    st[slot(ch)] = piece(j, mine, ch)[...].astype(BF16)
                    first_round(j, ch).start()

        def stage_second(j):
            mine, _, _, st, r1, _, _ = bufs(j)
            for ch in range(N_CHIP):
                @pl.when(p.first_coord(ch) == p.my_first_coord)
                def _(ch=ch):
                    first_round(j, ch).wait_recv()
                    part = piece(j, mine, ch)
                    total = part[...] + r1[slot(ch)].astype(F32)
                    part[...] = total

                    @pl.when(ch != p.chip)
                    def _():
                        st[2] = total.astype(BF16)
                        second_round(j).start()

        def stage_final(j):
            mine, _, _, _, _, r2, f = bufs(j)
            second_round(j).wait_recv()
            for ch in range(N_CHIP):
                @pl.when(ch == p.chip)
                def _(ch=ch):
                    f[...] = piece(j, mine, ch)[...] + r2[...].astype(F32)
            for cp in finals(j):
                cp.start()

        def drain(j):
            to_sibling(j).wait_send()
            for ch in range(N_CHIP):
                @pl.when(p.first_coord(ch) != p.my_first_coord)
                def _(ch=ch):
                    first_round(j, ch).wait_send()

                @pl.when(jnp.logical_and(p.first_coord(ch) == p.my_first_coord, ch != p.chip))
                def _():
                    second_round(j).wait_send()
            for cp in finals(j):
                cp.wait()

        dev = p.dev
        devices = p.others()

        def half(core):
            return pl.ds(pl.multiple_of(core * PK_HALF, 16), PK_HALF)

        def finished(core, ch):
            return pl.ds(pl.multiple_of(core * PK_HALF + ch * PK_PIECE, 16), PK_PIECE)

        def small_exchange(l, first_step, bufs_l):
            (pk_mine, pk_sib, pk_st, pk_rs, pk_fin, pk_all, dm_st, dm_all, pk_sem, rs_s, rs_r, ag_s, ag_r, dm_s, dm_r) = bufs_l

            def pk_load():
                return pltpu.make_async_copy(pack_ref.at[l, half(c)], pk_mine, pk_sem.at[0])

            def pk_give():
                return pltpu.make_async_remote_copy(pack_ref.at[l, half(1 - c)], pk_sib, pk_sem.at[1], pk_sem.at[2],
                                                    device_id=p.sibling, device_id_type=MESH)

            def pk_scatter(ch):
                return pltpu.make_async_remote_copy(pk_st.at[ch * PK_PIECE:(ch + 1) * PK_PIECE], pk_rs.at[p.chip],
                                                    rs_s.at[ch], rs_r.at[p.chip], device_id=(ch // 2, ch % 2, c),
                                                    device_id_type=MESH)

            def pk_spread(ch):
                return pltpu.make_async_remote_copy(pk_fin, pk_all.at[finished(c, p.chip)], ag_s.at[ch], ag_r.at[p.chip],
                                                    device_id=(ch // 2, ch % 2, c), device_id_type=MESH)

            def pk_pass():
                return pltpu.make_async_remote_copy(pk_all.at[half(c)], pk_all.at[half(c)], pk_sem.at[3], pk_sem.at[4],
                                                    device_id=p.sibling, device_id_type=MESH)

            def dm_copy(r):
                return pltpu.make_async_remote_copy(dm_st, dm_all.at[:, pl.ds(dev, 1), :], dm_s.at[r], dm_r.at[r],
                                                    device_id=devices[r], device_id_type=MESH)

            def results():
                return [pltpu.make_async_copy(pk_all, pack_out.at[l], pk_sem.at[0]),
                        pltpu.make_async_copy(dm_all, dmod_out.at[l], pk_sem.at[5])]

            @pl.when(i == first_step)
            def _():
                pk_load().start()
                pk_give().start()
                for k in range(3):
                    for r in range(D_MODEL // HEAD):
                        dm_st[8 * k + r] = dmod_ref[l, k:k + 1, r * HEAD:(r + 1) * HEAD]
                dm_st[DM_LOSS] = dmod_ref[l, 3:4, 0:HEAD]
                dm_all[:, pl.ds(dev, 1), :] = dm_st[...]
                for r in range(N_DEV - 1):
                    dm_copy(r).start()

            @pl.when(i == first_step + SMALL_SCATTER_STEP)
            def _():
                pk_load().wait()
                pk_give().wait()
                total = pk_mine[...] + pk_sib[...]
                pk_mine[...] = total
                pk_st[...] = total.astype(BF16)
                for ch in range(N_CHIP):
                    @pl.when(ch != p.chip)
                    def _(ch=ch):
                        pk_scatter(ch).start()

            @pl.when(i == first_step + SMALL_GATHER_STEP)
            def _():
                for ch in range(N_CHIP):
                    @pl.when(ch != p.chip)
                    def _(ch=ch):
                        pltpu.make_async_remote_copy(pk_fin, pk_rs.at[ch], rs_s.at[ch], rs_r.at[ch],
                                                     device_id=p.sibling, device_id_type=MESH).wait_recv()
                for me in range(N_CHIP):
                    @pl.when(me == p.chip)
                    def _(me=me):
                        total = None
                        for ch in range(N_CHIP):
                            part = pk_mine[me * PK_PIECE:(me + 1) * PK_PIECE] if ch == me else pk_rs[ch].astype(F32)
                            total = part if total is None else total + part
                        pk_fin[...] = total.astype(BF16)
                        pk_all[finished(c, me)] = total.astype(BF16)
                for ch in range(N_CHIP):
                    @pl.when(ch != p.chip)
                    def _(ch=ch):
                        pk_spread(ch).start()

            @pl.when(i == first_step + SMALL_PASS_STEP)
            def _():
                for ch in range(N_CHIP):
                    @pl.when(ch != p.chip)
                    def _(ch=ch):
                        pltpu.make_async_remote_copy(pk_fin, pk_all.at[finished(c, ch)], ag_s.at[ch], ag_r.at[ch],
                                                     device_id=p.sibling, device_id_type=MESH).wait_recv()
                pk_pass().start()

            @pl.when(i == first_step + SMALL_FINISH_STEP)
            def _():
                pk_pass().wait()
                for ch in range(N_CHIP):
                    @pl.when(ch != p.chip)
                    def _(ch=ch):
                        pk_scatter(ch).wait_send()
                        pk_spread(ch).wait_send()
                for r in range(N_DEV - 1):
                    dm_copy(r).wait()
                for cp in results():
                    cp.start()
                for cp in results():
                    cp.wait()

        n_small = 15
        for l in range(DEPTH):
            small_exchange(l, (DEPTH - 1 - l) * N_ITEMS, scratch[23 + n_small * l:23 + n_small * (l + 1)])

        for step in range(N_ITEMS, N_STEPS):
            @pl.when(i == step)
            def _(step=step):
                drain(step - N_ITEMS)

        ii = jnp.where(i < N_ITEMS, i, i - N_ITEMS)
        in_step = jnp.logical_and(ii >= OUT_FIRST, ii < OUT_FIRST + IN_STEPS)

        @pl.when(in_step)
        def _():
            k = ii - OUT_FIRST
            rhs = jnp.concatenate([r[...] for r in dp_refs], axis=1)
            res = _dot(h_ref[...], rhs, TN)

            @pl.when(c == 0)
            def _():
                mine_in[k] = res[:HALF_IN]
                send_in[k] = res[HALF_IN:]

            @pl.when(c == 1)
            def _():
                mine_in[k] = res[HALF_IN:]
                send_in[k] = res[:HALF_IN]

        @pl.when(jnp.logical_not(in_step))
        def _():
            k = jnp.where(ii < OUT_FIRST, ii, ii - IN_STEPS)
            res = _dot(cat_ref[...], dy_ref[...], TN)

            @pl.when(c == 0)
            def _():
                for ch in range(N_CHIP):
                    mine_out[k, ch] = res[ch * W_OUT_ROWS:ch * W_OUT_ROWS + HALF_OUT]
                    send_out[k, ch] = res[ch * W_OUT_ROWS + HALF_OUT:(ch + 1) * W_OUT_ROWS]

            @pl.when(c == 1)
            def _():
                for ch in range(N_CHIP):
                    mine_out[k, ch] = res[ch * W_OUT_ROWS + HALF_OUT:(ch + 1) * W_OUT_ROWS]
                    send_out[k, ch] = res[ch * W_OUT_ROWS:ch * W_OUT_ROWS + HALF_OUT]

        stages = ((0, lambda j: to_sibling(j).start()), (DELAY_SUM, stage_sum), (DELAY_SECOND, stage_second),
                  (DELAY_FINAL, stage_final))
        for step in range(N_STEPS):
            @pl.when(i == step)
            def _(step=step):
                for delay, stage in stages:
                    if step - delay >= 0:
                        stage(step - delay)

        @pl.when(i == N_STEPS - 1)
        def _():
            for step in range(N_STEPS, N_STEPS + DELAY_FINAL):
                for delay, stage in stages:
                    if 0 <= step - delay < N_STEPS:
                        stage(step - delay)
            for j in range(N_STEPS - N_ITEMS, N_STEPS):
                drain(j)

    hbm = pl.BlockSpec(memory_space=pl.ANY)

    def layer(i):
        return jnp.where(i < N_ITEMS, DEPTH - 1, 0)

    def item(i):
        return jnp.where(i < N_ITEMS, i, i - N_ITEMS)

    def whole(i):
        return (layer(i), 0, 0)

    def dproj_piece(ch):
        return pl.BlockSpec((None, SEQ, HEAD),
                            lambda i: (layer(i), 0, ch * IN_STEPS + jnp.clip(item(i) - OUT_FIRST, 0, IN_STEPS - 1)))

    def dy_quarter(i):
        return (layer(i), 0, jnp.where(item(i) < OUT_FIRST, item(i), jnp.maximum(item(i) - IN_STEPS, OUT_FIRST)))

    operand = pl.BlockSpec((None, SEQ, D_MODEL), whole)
    in_specs = [operand] + [dproj_piece(ch) for ch in range(N_CHIP)]
    in_specs += [operand, pl.BlockSpec((None, SEQ, OUT_COLS), dy_quarter), hbm, _const_in((DEPTH, 8, D_MODEL))]
    args = [h, dproj, dproj, dproj, dproj, cat, dy, pack, dmod]
    out_shape = [jax.ShapeDtypeStruct((DEPTH, D_MODEL, W_IN_COLS), F32), jax.ShapeDtypeStruct((DEPTH, 2, HALF_OUT, D_MODEL), F32),
                 jax.ShapeDtypeStruct((DEPTH, PK_ROWS, HEAD), BF16), jax.ShapeDtypeStruct((DEPTH, DM_ROWS, N_DEV, HEAD), F32)]
    out_specs = [hbm, hbm, hbm, hbm]
    in_item = lambda *lead: pltpu.VMEM(lead + (HALF_IN, HEAD), BF16)
    out_item = lambda *lead: pltpu.VMEM(lead + (HALF_OUT, OUT_COLS), BF16)
    scratch = [
        pltpu.VMEM((IN_STEPS, HALF_IN, N_CHIP * HEAD), F32), pltpu.VMEM((IN_STEPS, HALF_IN, N_CHIP * HEAD), F32),
        pltpu.VMEM((IN_STEPS, HALF_IN, N_CHIP * HEAD), F32), in_item(IN_STEPS, 3), in_item(IN_STEPS, 2), in_item(IN_STEPS),
        pltpu.VMEM((IN_STEPS, HALF_IN, HEAD), F32),
        pltpu.VMEM((OUT_STEPS, N_CHIP, HALF_OUT, OUT_COLS), F32), pltpu.VMEM((OUT_STEPS, N_CHIP, HALF_OUT, OUT_COLS), F32),
        pltpu.VMEM((OUT_STEPS, N_CHIP, HALF_OUT, OUT_COLS), F32), out_item(OUT_STEPS, 3), out_item(OUT_STEPS, 2),
        out_item(OUT_STEPS), pltpu.VMEM((OUT_STEPS, HALF_OUT, OUT_COLS), F32),
        pltpu.SemaphoreType.DMA((N_STEPS,)), pltpu.SemaphoreType.DMA((N_STEPS,)),
        pltpu.SemaphoreType.DMA((2 * N_STEPS,)), pltpu.SemaphoreType.DMA((2 * N_STEPS,)),
        pltpu.SemaphoreType.DMA((N_STEPS,)), pltpu.SemaphoreType.DMA((N_STEPS,)),
        pltpu.SemaphoreType.DMA((N_STEPS,)), pltpu.SemaphoreType.DMA((N_STEPS,)), pltpu.SemaphoreType.DMA((N_STEPS,)),
    ]
    for _ in range(DEPTH):
        scratch += [
            pltpu.VMEM((PK_HALF, HEAD), F32), pltpu.VMEM((PK_HALF, HEAD), F32), pltpu.VMEM((PK_HALF, HEAD), BF16),
            pltpu.VMEM((N_CHIP, PK_PIECE, HEAD), BF16), pltpu.VMEM((PK_PIECE, HEAD), BF16), pltpu.VMEM((PK_ROWS, HEAD), BF16),
            pltpu.VMEM((DM_ROWS, 1, HEAD), F32), pltpu.VMEM((DM_ROWS, N_DEV, HEAD), F32),
            pltpu.SemaphoreType.DMA((6,)),
            pltpu.SemaphoreType.DMA((N_CHIP,)), pltpu.SemaphoreType.DMA((N_CHIP,)),
            pltpu.SemaphoreType.DMA((N_CHIP,)), pltpu.SemaphoreType.DMA((N_CHIP,)),
            pltpu.SemaphoreType.DMA((N_DEV - 1,)), pltpu.SemaphoreType.DMA((N_DEV - 1,)),
        ]
    return pl.pallas_call(
        body, name="wgrad", grid=(N_STEPS,), in_specs=in_specs, out_specs=out_specs, out_shape=out_shape,
        scratch_shapes=scratch,
        compiler_params=pltpu.CompilerParams(dimension_semantics=("arbitrary",), vmem_limit_bytes=VMEM_LIMIT),
    )(*args)


def _adamw(w, g, m, v):
    m = ADAM_B1 * m + (1.0 - ADAM_B1) * g
    v = ADAM_B2 * v + (1.0 - ADAM_B2) * (g * g)
    m_hat = m / (1.0 - ADAM_B1 ** ADAM_STEP)
    v_hat = v / (1.0 - ADAM_B2 ** ADAM_STEP)
    delta = -ADAM_LR * (m_hat / (jnp.sqrt(v_hat) + ADAM_EPS) + ADAM_WD * w)
    return delta, m, v


def _adam_sharded(c_all, dmods, ada, w_in_set, w_out_set):
    rows = D_MODEL // ADAM_PARTS

    def body(c_ref, dm_ref, wa_ref, ma_ref, va_ref, wi_ref, gi_ref, mi_ref, vi_ref, wo_ref, go_ref, mo_ref, vo_ref,
             ga_out, da_out, ma_out, va_out, gi_out, di_out, mi_out, vi_out, go_out, do_out, mo_out, vo_out):
        l = pl.program_id(0)
        chip = 2 * lax.axis_index("x") + lax.axis_index("y")
        cv = c_ref[...]
        silu_c = (cv * _sigmoid(cv)).astype(BF16).astype(F32)
        pieces = []
        for k in range(W_ADA_COLS // HEAD):
            dk = dm_ref[l, (W_ADA_COLS // HEAD) * chip + k].astype(BF16).astype(F32)
            pieces.append(_dot_exact(silu_c, dk, TN))
        g = jnp.concatenate(pieces, axis=1)
        ga_out[...] = g
        da_out[...], ma_out[...], va_out[...] = _adamw(wa_ref[...], g, ma_ref[...], va_ref[...])
        gi_out[...] = gi_ref[...]
        go_out[...] = go_ref[...]
        di_out[...], mi_out[...], vi_out[...] = _adamw(wi_ref[...], gi_ref[...], mi_ref[...], vi_ref[...])
        do_out[...], mo_out[...], vo_out[...] = _adamw(wo_ref[...], go_ref[...], mo_ref[...], vo_ref[...])

    def blk(r, cols):
        return pl.BlockSpec((None, r, cols), lambda l, i: (l, i, 0))

    b_ada, b_in, b_out = blk(rows, W_ADA_COLS), blk(rows, W_IN_COLS), blk(W_OUT_ROWS // ADAM_PARTS, D_MODEL)
    shapes = [jax.ShapeDtypeStruct(a[0].shape, F32) for a in (ada, w_in_set, w_out_set)]
    return pl.pallas_call(
        body, name="adam_sharded", grid=(DEPTH, ADAM_PARTS),
        in_specs=[pl.BlockSpec((N_DEV, rows), lambda l, i: (0, i)), _const_in((DEPTH, DM_ROWS, N_DEV, HEAD))]
        + [b_ada] * 3 + [b_in] * 4 + [b_out] * 4,
        out_specs=[b_ada] * 4 + [b_in] * 4 + [b_out] * 4,
        out_shape=[shapes[0]] * 4 + [shapes[1]] * 4 + [shapes[2]] * 4,
        compiler_params=pltpu.CompilerParams(dimension_semantics=("arbitrary", "arbitrary"), vmem_limit_bytes=VMEM_LIMIT),
    )(c_all, dmods, *ada, *w_in_set, *w_out_set)


def _adam_small(packs, dmods, weights, ms, vs):
    n = len(weights)

    def body(*refs):
        dm_refs = refs[1]
        b = 2
        w_refs, m_refs, v_refs = refs[b:b + n], refs[b + n:b + 2 * n], refs[b + 2 * n:b + 3 * n]
        outs = refs[b + 3 * n:b + 3 * n + 4 * n + 1]
        pack_refs = refs[-1]
        pack_refs[...] = refs[0][...].astype(F32)
        g_refs, d_refs, nm_refs, nv_refs = outs[0:n], outs[n:2 * n], outs[2 * n:3 * n], outs[3 * n:4 * n]
        squares = dm_refs[DEPTH - 1, DM_LOSS]
        total = squares[0:1, 0:1]
        for d in range(1, N_DEV):
            total = total + squares[d:d + 1, 0:1]
        outs[4 * n][...] = total * (0.5 / D_MODEL)

        def lanes(l, row0, count):
            return jnp.concatenate([pack_refs.at[l][row0 + k:row0 + k + 1, :] for k in range(count)], axis=1)

        def update(idx, at, g):
            g_refs[idx][at] = g
            d_refs[idx][at], nm_refs[idx][at], nv_refs[idx][at] = _adamw(w_refs[idx][at], g, m_refs[idx][at], v_refs[idx][at])

        for l in range(DEPTH):
            row = (slice(l, l + 1), slice(None))
            g_b = None
            for d in range(N_DEV):
                part = dm_refs.at[l][0:DM_LOSS, d, :]
                g_b = part if g_b is None else g_b + part
            update(0, row, jnp.concatenate([g_b[k:k + 1, :] for k in range(DM_LOSS)], axis=1))
            for g in range(N_HEAD):
                update(1, (l, g), pack_refs.at[l][PK_W_POOL + g * HEAD:PK_W_POOL + (g + 1) * HEAD, :])
                update(5, (l, g), pack_refs.at[l][PK_W_SGU + g * HEAD:PK_W_SGU + (g + 1) * HEAD, :])
            update(2, row, lanes(l, PK_POOL_SCALE, N_HEAD))
            update(3, (l,), pack_refs.at[l][PK_SGU_LN_G:PK_SGU_LN_G + N_HEAD, :])
            update(4, (l,), pack_refs.at[l][PK_SGU_LN_B:PK_SGU_LN_B + N_HEAD, :])
            update(6, (l,), pack_refs.at[l][PK_B_SGU:PK_B_SGU + N_HEAD, :])
            update(7, row, lanes(l, PK_LN_G, D_MODEL // HEAD))
            update(8, row, lanes(l, PK_LN_B, D_MODEL // HEAD))

    vmem = pl.BlockSpec(memory_space=pltpu.VMEM)
    shapes = [jax.ShapeDtypeStruct(w.shape, F32) for w in weights]
    return pl.pallas_call(
        body, name="adam_small", in_specs=[vmem] * (2 + 3 * n), out_specs=[vmem] * (4 * n + 1),
        out_shape=shapes * 4 + [jax.ShapeDtypeStruct((1, 1), F32)],
        scratch_shapes=[pltpu.VMEM(packs.shape, F32)],
        compiler_params=pltpu.CompilerParams(vmem_limit_bytes=VMEM_LIMIT),
    )(packs, dmods, *weights, *ms, *vs)


def kernel(x, c, w_ada, b_ada, w_in, w_pool, pool_scale, sgu_ln_g, sgu_ln_b, w_sgu, b_sgu, w_out, ln_g, ln_b, loss_target, m_w_ada, m_b_ada, m_w_in, m_w_pool, m_pool_scale, m_sgu_ln_g, m_sgu_ln_b, m_w_sgu, m_b_sgu, m_w_out, m_ln_g, m_ln_b, v_w_ada, v_b_ada, v_w_in, v_w_pool, v_pool_scale, v_sgu_ln_g, v_sgu_ln_b, v_w_sgu, v_b_sgu, v_w_out, v_ln_g, v_ln_b):
    small = (w_pool, pool_scale, sgu_ln_g, sgu_ln_b, w_sgu, b_sgu)
    *saved0, w_in0, w_out0, w_in1, w_out1, mod, c_all = _forward_first(x, c, w_ada, b_ada, w_in, w_out, small, ln_g, ln_b)
    *saved1, sq = _forward_last(saved0[3], mod, w_in1, w_out1, small, ln_g, ln_b, loss_target)

    dx1, *shared = _backward_layer(1, loss_target, saved1, mod, w_in1, w_out1, small, ln_g, sq=sq, ln_b=ln_b)
    dx0, h, cat, dy, dproj, pack, dmod = _backward_layer(0, dx1, saved0, mod, w_in0, w_out0, small, ln_g, shared=shared)
    g_in, g_out, pack, dmods = _wgrad_reduce(h, dproj, cat, dy, pack, dmod)

    g_out = g_out.reshape(DEPTH, W_OUT_ROWS, D_MODEL)
    big = _adam_sharded(c_all, dmods, (w_ada, m_w_ada, v_w_ada), (w_in, g_in, m_w_in, v_w_in), (w_out, g_out, m_w_out, v_w_out))
    ada, win, wout = big[0:4], big[4:8], big[8:12]
    small_w = (b_ada, w_pool, pool_scale, sgu_ln_g, sgu_ln_b, w_sgu, b_sgu, ln_g, ln_b)
    small_m = (m_b_ada, m_w_pool, m_pool_scale, m_sgu_ln_g, m_sgu_ln_b, m_w_sgu, m_b_sgu, m_ln_g, m_ln_b)
    small_v = (v_b_ada, v_w_pool, v_pool_scale, v_sgu_ln_g, v_sgu_ln_b, v_w_sgu, v_b_sgu, v_ln_g, v_ln_b)
    res = _adam_small(pack, dmods, small_w, small_m, small_v)
    n = len(small_w)
    loss = res[4 * n].reshape(())

    def ordered(k):
        s = res[k * n:(k + 1) * n]
        return (ada[k], s[0], win[k], s[1], s[2], s[3], s[4], s[5], s[6], wout[k], s[7], s[8])

    return (loss, dx0[None], *ordered(0), *ordered(1), *ordered(2), *ordered(3))
```

```python
import jax
import jax.numpy as jnp
from jax import lax
from jax.experimental import pallas as pl
from jax.experimental.pallas import tpu as pltpu

F32 = jnp.float32
BF16 = jnp.bfloat16
MESH = pl.DeviceIdType.MESH

N_DEV = 8
N_CHIP = 4
DEPTH = 2
SEQ = 2048
D_MODEL = 1024
D_POOL = 512
D_PROJ = 2560
HEAD = 128
N_HEAD = 4
ROWS = 256
N_TILE = SEQ // ROWS
HALO = 16
W_IN_COLS = D_PROJ // N_CHIP
W_OUT_ROWS = D_MODEL // N_CHIP
W_ADA_COLS = 3 * D_MODEL // N_CHIP
HALF_IN = D_MODEL // 2
HALF_OUT = W_OUT_ROWS // 2
DEEPNORM_ALPHA = (2.0 * DEPTH) ** 0.25
LN_EPS = 1e-5
INV_SQRT2 = 0.7071067811865476
INV_SQRT_2PI = 0.3989422804014327

ADAM_LR = 0.001
ADAM_B1 = 0.9
ADAM_B2 = 0.999
ADAM_EPS = 1e-08
ADAM_WD = 0.01
ADAM_STEP = 10
ADAM_PARTS = 2

PK_W_POOL = 0
PK_W_SGU = 512
PK_POOL_SCALE = 1024
PK_SGU_LN_G = 1032
PK_SGU_LN_B = 1040
PK_B_SGU = 1048
PK_LN_G = 1056
PK_LN_B = 1064
PK_ROWS = 1152
PK_HALF = PK_ROWS // 2
PK_PIECE = PK_HALF // N_CHIP
DM_LOSS = 3 * D_MODEL // HEAD
DM_ROWS = DM_LOSS + 1

VMEM_LIMIT = 56 * 1024 * 1024

GATHER_SECOND_ROUND_STEP = 0
GATHER_PASS_STEP = N_TILE - 3

NN = (((1,), (0,)), ((), ()))
NT = (((1,), (1,)), ((), ()))
TN = (((0,), (0,)), ((), ()))


def _dot(a, b, dims=NN):
    return lax.dot_general(a, b, dims, preferred_element_type=F32)


def _dot_exact(a, b, dims=NN):
    return lax.dot_general(a, b, dims, preferred_element_type=F32, precision=lax.Precision.HIGHEST)


def _layer_norm(v):
    mu = jnp.mean(v, axis=-1, keepdims=True)
    d = v - mu
    var = jnp.mean(d * d, axis=-1, keepdims=True)
    rstd = lax.rsqrt(var + LN_EPS)
    return d * rstd, rstd


def _layer_norm_bwd(dvhat, vhat, rstd):
    m1 = jnp.mean(dvhat, axis=-1, keepdims=True)
    m2 = jnp.mean(dvhat * vhat, axis=-1, keepdims=True)
    return rstd * (dvhat - m1 - vhat * m2)


def _sigmoid(v):
    return 1.0 / (1.0 + jnp.exp(-v))


def _gelu_parts(v):
    phi = 0.5 * (1.0 + lax.erf(v * INV_SQRT2))
    pdf = INV_SQRT_2PI * jnp.exp(-0.5 * v * v)
    return phi, pdf


def _sum_rows(v):
    return jnp.sum(v, axis=0, keepdims=True)


def _window_sums(ext, toward_later):
    n = ext.shape[0]

    def shifted(v, k):
        return pltpu.roll(v, (n - k) if toward_later else k, 0)

    s2 = ext + shifted(ext, 1)
    r4 = s2[:, HEAD:]
    s4 = r4 + shifted(r4, 2)
    r8 = s4[:, HEAD:]
    s8 = r8 + shifted(r8, 4)
    r16 = s8[:, HEAD:]
    s16 = r16 + shifted(r16, 8)
    return jnp.concatenate([s2[:, :HEAD], s4[:, :HEAD], s8[:, :HEAD], s16], axis=1)


def _window_counts(row0):
    t1 = row0 + 1 + lax.broadcasted_iota(jnp.int32, (ROWS, D_POOL), 0)
    lane = lax.broadcasted_iota(jnp.int32, (ROWS, D_POOL), 1)
    width = jnp.where(lane < HEAD, 2, jnp.where(lane < 2 * HEAD, 4, jnp.where(lane < 3 * HEAD, 8, 16)))
    return jnp.minimum(t1, width).astype(F32)


def _causal_mask():
    r = lax.broadcasted_iota(jnp.int32, (HEAD, HEAD), 0)
    s = lax.broadcasted_iota(jnp.int32, (HEAD, HEAD), 1)
    return r >= s


def _chunks_to_lanes(v):
    return jnp.concatenate([v[n * HEAD:(n + 1) * HEAD] for n in range(ROWS // HEAD)], axis=1)


def _lanes_to_chunks(v):
    return jnp.concatenate([v[:, n * HEAD:(n + 1) * HEAD] for n in range(ROWS // HEAD)], axis=0)


def _pack_stats(rstd_x, rstd_z, rstd_v):
    lane = lax.broadcasted_iota(jnp.int32, (ROWS, HEAD), 1)
    packed = rstd_x
    for k, r in enumerate([rstd_z] + list(rstd_v)):
        packed = jnp.where(lane < 16 * (k + 1), packed, r)
    return packed


def _unpack_stats(stats):
    cols = [stats[:, 16 * k:16 * k + 1] for k in range(2 + N_HEAD)]
    return cols[0], cols[1], cols[2:]


def _mixer(proj, halo, row0, wpool_ref, pscale, sgu_g_ref, sgu_b_ref, wsgu_ref, bsgu_ref, saved=None):
    xa = proj[:, 0:512]
    ga = proj[:, 512:1024]
    u = proj[:, 1024:1536]
    v = proj[:, 1536:2048]
    gb = proj[:, 2048:2560]
    ext = jnp.concatenate([halo, xa], axis=0)
    win = _window_sums(ext, toward_later=False)[HALO:]
    cnt = _window_counts(row0)
    pooled = (win / cnt - xa).astype(BF16)
    pw = jnp.concatenate(
        [_dot(pooled[:, g * HEAD:(g + 1) * HEAD], wpool_ref[g].astype(BF16)) for g in range(N_HEAD)], axis=1)
    sig_a = _sigmoid(ga) if saved is None else saved["sig_a"]
    ya = pw * pscale * (ga * sig_a)
    phi_u, pdf_u = _gelu_parts(u)
    phi_v, pdf_v = _gelu_parts(v)
    gu = u * phi_u
    gv = v * phi_v
    sig_b = _sigmoid(gb) if saved is None else saved["sig_b"]
    silu_b = gb * sig_b
    mask = _causal_mask()
    diag = lax.broadcasted_iota(jnp.int32, (HEAD, HEAD), 0) == lax.broadcasted_iota(jnp.int32, (HEAD, HEAD), 1)
    vhat, rstd_v, vln_l, mixed = [], [], [], []
    for h in range(N_HEAD):
        if saved is None:
            vh, rh = _layer_norm(gv[:, h * HEAD:(h + 1) * HEAD])
        else:
            vh, rh = saved["vhat"][h], saved["rstd_v"][h]
        ln = (vh * sgu_g_ref[h:h + 1, :] + sgu_b_ref[h:h + 1, :]).astype(BF16)
        ln_l = _chunks_to_lanes(ln)
        wm = jnp.where(mask, wsgu_ref[h], 0.0).astype(BF16)
        bias = jnp.sum(jnp.where(diag, jnp.broadcast_to(bsgu_ref[h:h + 1, :], (HEAD, HEAD)), 0.0), axis=1, keepdims=True)
        mx = _lanes_to_chunks(_dot(wm, ln_l) + bias)
        vhat.append(vh)
        rstd_v.append(rh)
        vln_l.append(ln_l)
        mixed.append(mx)
    mixed = jnp.concatenate(mixed, axis=1)
    yb = gu * mixed * silu_b
    return dict(xa=xa, ga=ga, u=u, v=v, gb=gb, cnt=cnt, pooled=pooled, pw=pw, sig_a=sig_a, ya=ya, phi_u=phi_u, pdf_u=pdf_u,
                phi_v=phi_v, pdf_v=pdf_v, gu=gu, sig_b=sig_b, silu_b=silu_b, vhat=vhat, rstd_v=rstd_v, vln_l=vln_l,
                mixed=mixed, yb=yb, mask=mask)


def _const(shape, *index):
    lead = tuple(index) + (0,) * (len(shape) - len(index))
    return pl.BlockSpec(shape, lambda *_: lead)


def _const_in(shape, *index):
    lead = tuple(index) + (0,) * (len(shape) - len(index))
    return pl.BlockSpec(shape, lambda *_: lead, pipeline_mode=pl.Buffered(1))


def _layer_weight_specs(l):
    return [
        _const_in((None, N_HEAD, HEAD, HEAD), l),
        _const_in((DEPTH, D_POOL)),
        _const_in((None, N_HEAD, HEAD), l),
        _const_in((None, N_HEAD, HEAD), l),
        _const_in((None, N_HEAD, HEAD, HEAD), l),
        _const_in((None, N_HEAD, HEAD), l),
    ]


def _forward_tile(l, i, x_ref, mod_ref, win_ref, wout_ref, small_refs, lng_ref, lnb_ref, carry_ref, saved_refs):
    wpool_ref, pscale_ref, sgu_g_ref, sgu_b_ref, wsgu_ref, bsgu_ref = small_refs
    proj_ref, y_ref, xn_ref, zn_ref, stats_ref, sig_ref, vhat_ref = saved_refs
    x = x_ref[...]
    if l > 0:
        x = x * lng_ref[l - 1:l, :] + lnb_ref[l - 1:l, :]
    shift, scale, gate = mod_ref[0:1, :], mod_ref[1:2, :], mod_ref[2:3, :]
    xn, rstd_x = _layer_norm(x)
    xn_ref[...] = xn.astype(xn_ref.dtype)
    h = xn * (1.0 + scale) + shift
    proj = _dot(h.astype(BF16), win_ref[...])
    proj_ref[...] = proj.astype(proj_ref.dtype)
    m = _mixer(proj, carry_ref[...], i * ROWS, wpool_ref, pscale_ref[l:l + 1, :], sgu_g_ref, sgu_b_ref, wsgu_ref, bsgu_ref)
    carry_ref[...] = m["xa"][ROWS - HALO:]
    sig_ref[...] = jnp.concatenate([m["sig_a"], m["sig_b"]], axis=1).astype(sig_ref.dtype)
    vhat_ref[...] = jnp.concatenate(m["vhat"], axis=1).astype(vhat_ref.dtype)
    cat = jnp.concatenate([m["ya"], m["yb"]], axis=1).astype(BF16)
    y = _dot(cat, wout_ref[...])
    y_ref[...] = y.astype(y_ref.dtype)
    zn, rstd_z = _layer_norm(DEEPNORM_ALPHA * x + gate * y)
    zn_ref[...] = zn.astype(zn_ref.dtype)
    stats_ref[...] = _pack_stats(rstd_x, rstd_z, m["rstd_v"])
    return zn


SAVED_COLS = (D_PROJ, D_MODEL, D_MODEL, D_MODEL, HEAD, D_MODEL, D_POOL)
SAVED_TYPES = (BF16, BF16, BF16, F32, F32, BF16, BF16)


def _saved_outputs(zn_type=F32):
    types = SAVED_TYPES[:3] + (zn_type,) + SAVED_TYPES[4:]
    return ([jax.ShapeDtypeStruct((SEQ, cols), t) for cols, t in zip(SAVED_COLS, types)],
            [pl.BlockSpec((ROWS, cols), lambda i: (i, 0)) for cols in SAVED_COLS])


def _forward_last(zn_prev, mod, w_in, w_out, small, ln_g, ln_b, target):
    l = DEPTH - 1
    n_saved = len(SAVED_COLS)

    def body(*refs):
        x_ref, mod_ref, win_ref, wout_ref = refs[:4]
        small_refs, lng_ref, lnb_ref, tgt_ref = refs[4:10], refs[10], refs[11], refs[12]
        saved_refs = refs[13:13 + n_saved]
        dout_ref, loss_ref, carry_ref = refs[13 + n_saved:]
        i = pl.program_id(0)

        @pl.when(i == 0)
        def _():
            carry_ref[...] = jnp.zeros_like(carry_ref)
            loss_ref[...] = jnp.zeros_like(loss_ref)

        zn = _forward_tile(l, i, x_ref, mod_ref, win_ref, wout_ref, small_refs, lng_ref, lnb_ref, carry_ref, saved_refs)
        err = zn * lng_ref[l:l + 1, :] + lnb_ref[l:l + 1, :] - tgt_ref[...]
        dout_ref[...] = err * (1.0 / D_MODEL)
        loss_ref[...] += jnp.sum(err * err)

    tile = pl.BlockSpec((ROWS, D_MODEL), lambda i: (i, 0))
    tile3 = pl.BlockSpec((None, ROWS, D_MODEL), lambda i: (0, i, 0))
    in_specs = [tile, _const_in((None, 8, D_MODEL), l), _const_in((D_MODEL, D_PROJ)), _const_in((D_MODEL, D_MODEL))]
    in_specs += _layer_weight_specs(l) + [_const_in((DEPTH, D_MODEL)), _const_in((DEPTH, D_MODEL)), tile3]
    out_shape, out_specs = _saved_outputs(zn_type=BF16)
    out_shape += [jax.ShapeDtypeStruct((SEQ, D_MODEL), F32), jax.ShapeDtypeStruct((8, HEAD), F32)]
    out_specs += [tile, _const((8, HEAD))]
    return pl.pallas_call(
        body, name="fwd_last", grid=(N_TILE,), in_specs=in_specs, out_specs=out_specs, out_shape=out_shape,
        scratch_shapes=[pltpu.VMEM((HALO, D_POOL), F32)],
        compiler_params=pltpu.CompilerParams(dimension_semantics=("arbitrary",), vmem_limit_bytes=VMEM_LIMIT),
    )(zn_prev, mod, w_in, w_out, *small, ln_g, ln_b, target)


def _backward_layer(l, dout, saved, mod, w_in, w_out, small, ln_g, sq=None, shared=None):
    has_loss = sq is not None

    def body(*refs):
        (dout_ref, proj_ref, y_ref, xn_ref, zn_ref, stats_ref, sig_ref, vhat_ref, halo_ref, mod_ref, win_ref, wout_ref,
         wpool_ref, pscale_ref, sgu_g_ref, sgu_b_ref, wsgu_ref, bsgu_ref, lng_ref) = refs[:19]
        n_in = 20 if has_loss else 19 + 6
        dx_ref, h_ref, cat_ref, dy_ref, dproj_ref, pack_ref, dmod_ref, carry_ref = refs[n_in:n_in + 8]
        i = pl.program_id(0)
        tile = N_TILE - 1 - i

        @pl.when(i == 0)
        def _():
            carry_ref[...] = jnp.zeros_like(carry_ref)
            pack_ref[...] = jnp.zeros_like(pack_ref)
            dmod_ref[...] = jnp.zeros_like(dmod_ref)
            if has_loss:
                dmod_ref[3:4, 0:HEAD] = refs[19][0:1, :]

        xn = xn_ref[...].astype(F32)
        zn = zn_ref[...].astype(F32)
        y = y_ref[...].astype(F32)
        dout = dout_ref[...]
        rstd_x, rstd_z, rstd_v = _unpack_stats(stats_ref[...])
        kept = dict(sig_a=sig_ref[:, :D_POOL].astype(F32), sig_b=sig_ref[:, D_POOL:].astype(F32), rstd_v=rstd_v,
                    vhat=[vhat_ref[:, hd * HEAD:(hd + 1) * HEAD].astype(F32) for hd in range(N_HEAD)])
        pscale = pscale_ref[l:l + 1, :]
        shift, scale, gate = mod_ref[0:1, :], mod_ref[1:2, :], mod_ref[2:3, :]
        h = xn * (1.0 + scale) + shift
        h_ref[...] = h.astype(BF16)
        g_ln_g = _sum_rows(dout * zn)
        g_ln_b = _sum_rows(dout)
        dz = _layer_norm_bwd(dout * lng_ref[l:l + 1, :], zn, rstd_z)
        d_gate = _sum_rows(dz * y)
        dy = (gate * dz).astype(BF16)
        dy_ref[...] = dy

        halo = jnp.where(tile > 0, halo_ref[...].astype(F32), 0.0)
        m = _mixer(proj_ref[...].astype(F32), halo, tile * ROWS, wpool_ref, pscale, sgu_g_ref, sgu_b_ref, wsgu_ref, bsgu_ref,
                   saved=kept)
        cat_ref[...] = jnp.concatenate([m["ya"], m["yb"]], axis=1).astype(BF16)
        dcat = _dot(dy, wout_ref[...], NT)
        dya = dcat[:, :D_POOL]
        dyb = dcat[:, D_POOL:]

        ga, sig_a = m["ga"], m["sig_a"]
        dp = dya * (ga * sig_a)
        d_ga = dya * (m["pw"] * pscale) * (sig_a * (1.0 + ga * (1.0 - sig_a)))
        g_pscale = _sum_rows(dp * m["pw"])
        dpw = (dp * pscale).astype(BF16)
        dpooled = []
        for g in range(N_HEAD):
            cols = slice(g * HEAD, (g + 1) * HEAD)
            pack_ref[PK_W_POOL + g * HEAD:PK_W_POOL + (g + 1) * HEAD, :] += _dot(m["pooled"][:, cols], dpw[:, cols], TN)
            dpooled.append(_dot(dpw[:, cols], wpool_ref[g].astype(BF16), NT))
        dpooled = jnp.concatenate(dpooled, axis=1)
        q = dpooled / m["cnt"]
        ext = jnp.concatenate([q, carry_ref[...]], axis=0)
        d_xa = _window_sums(ext, toward_later=True)[:ROWS] - dpooled
        carry_ref[...] = q[:HALO]

        gu, mixed, silu_b, gb, sig_b = m["gu"], m["mixed"], m["silu_b"], m["gb"], m["sig_b"]
        d_mixed = dyb * gu * silu_b
        d_gu = dyb * mixed * silu_b
        d_gb = dyb * gu * mixed * (sig_b * (1.0 + gb * (1.0 - sig_b)))
        d_u = d_gu * (m["phi_u"] + m["u"] * m["pdf_u"])
        ones = jnp.ones((8, HEAD), F32)
        d_v = []
        for hd in range(N_HEAD):
            cols = slice(hd * HEAD, (hd + 1) * HEAD)
            dm = d_mixed[:, cols]
            dm_l = _chunks_to_lanes(dm.astype(BF16))
            g_w = _dot(dm_l, m["vln_l"][hd], NT)
            pack_ref[PK_W_SGU + hd * HEAD:PK_W_SGU + (hd + 1) * HEAD, :] += jnp.where(m["mask"], g_w, 0.0)
            dm_sum = dm[0:HEAD]
            for n in range(1, ROWS // HEAD):
                dm_sum = dm_sum + dm[n * HEAD:(n + 1) * HEAD]
            pack_ref[PK_B_SGU + hd:PK_B_SGU + hd + 1, :] += _dot_exact(ones, dm_sum, NT)[0:1]
            wm = jnp.where(m["mask"], wsgu_ref[hd], 0.0).astype(BF16)
            d_vln = _lanes_to_chunks(_dot(wm, dm_l, TN))
            vhat = m["vhat"][hd]
            pack_ref[PK_SGU_LN_G + hd:PK_SGU_LN_G + hd + 1, :] += _sum_rows(d_vln * vhat)
            pack_ref[PK_SGU_LN_B + hd:PK_SGU_LN_B + hd + 1, :] += _sum_rows(d_vln)
            d_v.append(_layer_norm_bwd(d_vln * sgu_g_ref[hd:hd + 1, :], vhat, m["rstd_v"][hd]))
        v = m["v"]
        d_v = jnp.concatenate(d_v, axis=1) * (m["phi_v"] + v * m["pdf_v"])

        dproj = jnp.concatenate([d_xa, d_ga, d_u, d_v, d_gb], axis=1).astype(BF16)
        dproj_ref[...] = dproj
        dh = _dot(dproj, win_ref[...], NT)
        d_scale = _sum_rows(dh * xn)
        d_shift = _sum_rows(dh)
        dx_ref[...] = DEEPNORM_ALPHA * dz + _layer_norm_bwd(dh * (1.0 + scale), xn, rstd_x)

        dmod_ref[0:1, :] += d_shift
        dmod_ref[1:2, :] += d_scale
        dmod_ref[2:3, :] += d_gate
        for g in range(N_HEAD):
            pack_ref[PK_POOL_SCALE + g:PK_POOL_SCALE + g + 1, :] += g_pscale[:, g * HEAD:(g + 1) * HEAD]
        for k in range(D_MODEL // HEAD):
            pack_ref[PK_LN_G + k:PK_LN_G + k + 1, :] += g_ln_g[:, k * HEAD:(k + 1) * HEAD]
            pack_ref[PK_LN_B + k:PK_LN_B + k + 1, :] += g_ln_b[:, k * HEAD:(k + 1) * HEAD]

    def rev(i):
        return (N_TILE - 1 - i, 0)

    tile = pl.BlockSpec((ROWS, D_MODEL), rev)
    halo = pl.BlockSpec((HALO, D_POOL), lambda i: (jnp.maximum((N_TILE - 1 - i) * (ROWS // HALO) - 1, 0), 0))
    in_specs = [tile] + [pl.BlockSpec((ROWS, a.shape[1]), rev) for a in saved] + [halo]
    in_specs += [_const_in((None, 8, D_MODEL), l), _const_in((D_MODEL, D_PROJ)), _const_in((D_MODEL, D_MODEL))]
    in_specs += _layer_weight_specs(l) + [_const_in((DEPTH, D_MODEL))]
    args = [dout, *saved, saved[0], mod, w_in, w_out, *small, ln_g]
    stacked = lambda cols: pl.BlockSpec((None, ROWS, cols), lambda i: (l, N_TILE - 1 - i, 0))
    out_shape = [jax.ShapeDtypeStruct((SEQ, D_MODEL), F32), jax.ShapeDtypeStruct((DEPTH, SEQ, D_MODEL), BF16),
                 jax.ShapeDtypeStruct((DEPTH, SEQ, D_MODEL), BF16), jax.ShapeDtypeStruct((DEPTH, SEQ, D_MODEL), BF16),
                 jax.ShapeDtypeStruct((DEPTH, SEQ, D_PROJ), BF16), jax.ShapeDtypeStruct((DEPTH, PK_ROWS, HEAD), F32),
                 jax.ShapeDtypeStruct((DEPTH, 8, D_MODEL), F32)]
    out_specs = [tile, stacked(D_MODEL), stacked(D_MODEL), stacked(D_MODEL), stacked(D_PROJ),
                 _const((None, PK_ROWS, HEAD), l), _const((None, 8, D_MODEL), l)]
    aliases = {}
    if has_loss:
        in_specs.append(_const_in((8, HEAD)))
        args.append(sq)
    else:
        aliases = {len(args) + k: 1 + k for k in range(len(shared))}
        in_specs += [pl.BlockSpec(memory_space=pl.ANY)] * len(shared)
        args += list(shared)
    return pl.pallas_call(
        body, name="bwd_last" if has_loss else "bwd_first", grid=(N_TILE,), in_specs=in_specs, out_specs=out_specs,
        out_shape=out_shape, scratch_shapes=[pltpu.VMEM((HALO, D_POOL), F32)], input_output_aliases=aliases,
        compiler_params=pltpu.CompilerParams(dimension_semantics=("arbitrary",), vmem_limit_bytes=VMEM_LIMIT),
    )(*args)


def _flip(v, f):
    return v + f - 2 * v * f


class _Place:
    def __init__(self):
        x, y, c = lax.axis_index("x"), lax.axis_index("y"), lax.axis_index("c")
        self.x, self.y, self.c = x, y, c
        self.chip = 2 * x + y
        self.dev = 4 * x + 2 * y + c
        self.sibling = (x, y, 1 - c)
        x1, y1 = _flip(x, 1 - c), _flip(y, c)
        x2, y2 = _flip(x, c), _flip(y, 1 - c)
        self.first = (x1, y1, c)
        self.second = (x2, y2, c)
        self.chip_first = 2 * x1 + y1
        self.chip_second = 2 * x2 + y2
        self.chip_far = 2 * (1 - x) + (1 - y)
        self.my_first_coord = jnp.where(c == 0, x, y)

    def first_coord(self, ch):
        return jnp.where(self.c == 0, ch // 2, ch % 2)

    def others(self):
        return [(_flip(self.x, (r >> 2) & 1), _flip(self.y, (r >> 1) & 1), _flip(self.c, r & 1)) for r in range(1, N_DEV)]

    def other_chips(self):
        return [(1 - self.x, self.y), (self.x, 1 - self.y), (1 - self.x, 1 - self.y)]


class _WeightGather:
    CHUNKS = 2
    N_SEMS = 12 * CHUNKS

    def __init__(self, place, win, wout, send, recv):
        self.p, self.win, self.wout, self.send, self.recv = place, win, wout, send, recv
        p = place
        self.plan = [(p.chip, p.first), (p.chip, p.second), (p.chip_first, p.second),
                     (p.chip_first, p.sibling), (p.chip_second, p.sibling), (p.chip_far, p.sibling)]

    def _copies(self, k, q):
        ch, target = self.plan[k]
        n_in, n_out = HALF_IN // self.CHUNKS, HALF_OUT // self.CHUNKS
        rows_in = pl.ds(pl.multiple_of(self.p.c * HALF_IN + q * n_in, n_in), n_in)
        cols_in = pl.ds(pl.multiple_of(ch * W_IN_COLS, 128), W_IN_COLS)
        rows_out = pl.ds(pl.multiple_of(ch * W_OUT_ROWS + self.p.c * HALF_OUT + q * n_out, n_out), n_out)
        r_in = self.win.at[rows_in, cols_in]
        r_out = self.wout.at[rows_out, :]
        s = 2 * (6 * q + k)
        return [pltpu.make_async_remote_copy(r_in, r_in, self.send.at[s], self.recv.at[s],
                                             device_id=target, device_id_type=MESH),
                pltpu.make_async_remote_copy(r_out, r_out, self.send.at[s + 1], self.recv.at[s + 1],
                                             device_id=target, device_id_type=MESH)]

    def _start(self, k, q):
        for cp in self._copies(k, q):
            cp.start()

    def _landed(self, k, q):
        for cp in self._copies(k, q):
            cp.wait_recv()

    def start_first_round(self):
        for q in range(self.CHUNKS):
            self._start(0, q)
        for q in range(self.CHUNKS):
            self._start(1, q)

    def start_second_round(self, q):
        self._landed(0, q)
        self._start(2, q)
        self._start(3, q)

    def pass_second_round(self, q):
        self._landed(1, q)
        self._start(4, q)
        self._landed(2, q)
        self._start(5, q)

    def finish(self):
        for q in range(self.CHUNKS):
            for k in (3, 4, 5):
                self._landed(k, q)
        for q in range(self.CHUNKS):
            for k in range(len(self.plan)):
                for cp in self._copies(k, q):
                    cp.wait_send()


def _forward_first(x, c_vec, w_ada, b_ada, w_in, w_out, small, ln_g, ln_b):
    n_saved = len(SAVED_COLS)

    def body(*refs):
        x_ref, c_ref, wada_hbm, bada_ref, win_hbm, wout_hbm = refs[:6]
        small_refs, lng_ref, lnb_ref = refs[6:12], refs[12], refs[13]
        saved_refs = refs[14:14 + n_saved]
        win0, wout0, win1, wout1, mod_out, c_out = refs[14 + n_saved:20 + n_saved]
        (carry_ref, wada_ref, win_ref, wout_ref, win_bf, wout_bf, mod_mine, mod_all, c_all, mod_ref, win_v, wout_v,
         g0_send, g0_recv, g1_send, g1_recv, c_send, c_recv, mod_send, mod_recv, local_sem) = refs[20 + n_saved:]
        i = pl.program_id(0)
        p = _Place()
        gather0 = _WeightGather(p, win_v, wout_v, g0_send, g0_recv)
        keep = [pltpu.make_async_copy(win_v, win0, local_sem.at[9]), pltpu.make_async_copy(wout_v, wout0, local_sem.at[10])]
        gather1 = _WeightGather(p, win1, wout1, g1_send, g1_recv)

        @pl.when(i == 0)
        def _():
            carry_ref[...] = jnp.zeros_like(carry_ref)
            loads = [pltpu.make_async_copy(win_hbm.at[0], win_ref.at[0], local_sem.at[4]),
                     pltpu.make_async_copy(wout_hbm.at[0], wout_ref.at[0], local_sem.at[5]),
                     pltpu.make_async_copy(win_hbm.at[1], win_ref.at[1], local_sem.at[6]),
                     pltpu.make_async_copy(wout_hbm.at[1], wout_ref.at[1], local_sem.at[7]),
                     pltpu.make_async_copy(wada_hbm, wada_ref, local_sem.at[8])]
            for cp in loads:
                cp.start()

            c_all[pl.ds(p.dev, 1), :] = c_ref[...]
            mine = c_all.at[pl.ds(p.dev, 1), :]
            c_copies = [pltpu.make_async_remote_copy(mine, mine, c_send.at[r], c_recv.at[r], device_id=d, device_id_type=MESH)
                        for r, d in enumerate(p.others())]
            for cp in c_copies:
                cp.start()

            cols = pl.ds(pl.multiple_of(p.chip * W_IN_COLS, 128), W_IN_COLS)
            rows = pl.ds(pl.multiple_of(p.chip * W_OUT_ROWS, W_OUT_ROWS), W_OUT_ROWS)
            own = [pltpu.make_async_copy(win_bf.at[0], win_v.at[:, cols], local_sem.at[0]),
                   pltpu.make_async_copy(wout_bf.at[0], wout_v.at[rows, :], local_sem.at[1]),
                   pltpu.make_async_copy(win_bf.at[1], win1.at[:, cols], local_sem.at[2]),
                   pltpu.make_async_copy(wout_bf.at[1], wout1.at[rows, :], local_sem.at[3])]
            for l in range(DEPTH):
                loads[2 * l].wait()
                win_bf[l] = win_ref[l].astype(BF16)
                own[2 * l].start()
                loads[2 * l + 1].wait()
                wout_bf[l] = wout_ref[l].astype(BF16)
                own[2 * l + 1].start()
                if l == 0:
                    own[0].wait()
                    own[1].wait()
                    gather0.start_first_round()
            for cp in c_copies:
                cp.wait()
            loads[4].wait()

            cv = c_all[...]
            c_out[...] = cv
            silu_c = (cv * _sigmoid(cv)).astype(BF16)
            for l in range(DEPTH):
                mod_mine[l] = _dot(silu_c, wada_ref[l].astype(BF16))
            mod_all[p.chip] = mod_mine[...]
            m_copies = [pltpu.make_async_remote_copy(mod_mine, mod_all.at[p.chip], mod_send.at[k], mod_recv.at[k],
                                                     device_id=(px, py, p.c), device_id_type=MESH)
                        for k, (px, py) in enumerate(p.other_chips())]
            for cp in m_copies:
                cp.start()
            for q in range(gather0.CHUNKS):
                gather0.start_second_round(q)
            own[2].wait()
            own[3].wait()
            gather1.start_first_round()
            for cp in m_copies:
                cp.wait()
            mod_ref[...] = jnp.zeros_like(mod_ref)
            for l in range(DEPTH):
                full = jnp.concatenate([mod_all[ch, l, pl.ds(p.dev, 1), :] for ch in range(N_CHIP)], axis=1) + bada_ref[l:l + 1, :]
                for k in range(3):
                    mod_ref[l, k:k + 1, :] = full[:, k * D_MODEL:(k + 1) * D_MODEL]
            mod_out[...] = mod_ref[...]
            for q in range(gather0.CHUNKS):
                gather0.pass_second_round(q)
            gather0.finish()
            for cp in keep:
                cp.start()

        for q in range(_WeightGather.CHUNKS):
            @pl.when(i == GATHER_SECOND_ROUND_STEP + q)
            def _(q=q):
                gather1.start_second_round(q)

            @pl.when(i == GATHER_PASS_STEP)
            def _(q=q):
                gather1.pass_second_round(q)

        _forward_tile(0, i, x_ref, mod_ref.at[0], win_v, wout_v, small_refs, lng_ref, lnb_ref, carry_ref, saved_refs)

        @pl.when(i == N_TILE - 1)
        def _():
            gather1.finish()
            for cp in keep:
                cp.wait()

    hbm = pl.BlockSpec(memory_space=pl.ANY)
    tile3 = pl.BlockSpec((None, ROWS, D_MODEL), lambda i: (0, i, 0))
    in_specs = [tile3, _const_in((1, D_MODEL)), hbm, _const_in((DEPTH, 3 * D_MODEL)), hbm, hbm]
    in_specs += _layer_weight_specs(0) + [_const_in((DEPTH, D_MODEL)), _const_in((DEPTH, D_MODEL))]
    out_shape, out_specs = _saved_outputs()
    w_in_shape = jax.ShapeDtypeStruct((D_MODEL, D_PROJ), BF16)
    w_out_shape = jax.ShapeDtypeStruct((D_MODEL, D_MODEL), BF16)
    out_shape += [w_in_shape, w_out_shape, w_in_shape, w_out_shape,
                  jax.ShapeDtypeStruct((DEPTH, 8, D_MODEL), F32), jax.ShapeDtypeStruct((N_DEV, D_MODEL), F32)]
    out_specs += [hbm, hbm, hbm, hbm, _const((DEPTH, 8, D_MODEL)), _const((N_DEV, D_MODEL))]
    gather_sems = [pltpu.SemaphoreType.DMA((_WeightGather.N_SEMS,))] * 4
    scratch = [
        pltpu.VMEM((HALO, D_POOL), F32),
        pltpu.VMEM(w_ada.shape, F32), pltpu.VMEM(w_in.shape, F32), pltpu.VMEM(w_out.shape, F32),
        pltpu.VMEM((DEPTH, D_MODEL, W_IN_COLS), BF16), pltpu.VMEM((DEPTH, W_OUT_ROWS, D_MODEL), BF16),
        pltpu.VMEM((DEPTH, N_DEV, W_ADA_COLS), F32), pltpu.VMEM((N_CHIP, DEPTH, N_DEV, W_ADA_COLS), F32),
        pltpu.VMEM((N_DEV, D_MODEL), F32), pltpu.VMEM((DEPTH, 8, D_MODEL), F32),
        pltpu.VMEM((D_MODEL, D_PROJ), BF16), pltpu.VMEM((D_MODEL, D_MODEL), BF16),
    ] + gather_sems + [
        pltpu.SemaphoreType.DMA((7,)), pltpu.SemaphoreType.DMA((7,)),
        pltpu.SemaphoreType.DMA((3,)), pltpu.SemaphoreType.DMA((3,)),
        pltpu.SemaphoreType.DMA((11,)),
    ]
    return pl.pallas_call(
        body, name="fwd_first", grid=(N_TILE,), in_specs=in_specs, out_specs=out_specs, out_shape=out_shape,
        scratch_shapes=scratch,
        compiler_params=pltpu.CompilerParams(dimension_semantics=("arbitrary",), vmem_limit_bytes=VMEM_LIMIT),
    )(x, c_vec, w_ada, b_ada, w_in, w_out, *small, ln_g, ln_b)


IN_STEPS = W_IN_COLS // HEAD
OUT_STEPS = 4
OUT_COLS = D_MODEL // OUT_STEPS
OUT_FIRST = 2
ITEMS = ([("out", k) for k in range(OUT_FIRST)] + [("in", k) for k in range(IN_STEPS)]
         + [("out", k) for k in range(OUT_FIRST, OUT_STEPS)])
N_ITEMS = len(ITEMS)
N_STEPS = DEPTH * N_ITEMS
DELAY_SUM, DELAY_SECOND, DELAY_FINAL = 1, 3, 5
SMALL_SCATTER_STEP, SMALL_GATHER_STEP, SMALL_PASS_STEP, SMALL_FINISH_STEP = 1, 3, 5, 7


def _wgrad_reduce(h, dproj, cat, dy, pack, dmod):
    def body(*refs):
        h_ref, dp_refs, cat_ref, dy_ref, pack_ref, dmod_ref = refs[0], refs[1:5], refs[5], refs[6], refs[7], refs[8]
        fin_in, fin_out, pack_out, dmod_out = refs[9:13]
        scratch = refs[13:]
        (mine_in, send_in, sib_in, st_in, r1_in, r2_in, f_in,
         mine_out, send_out, sib_out, st_out, r1_out, r2_out, f_out,
         d2d_s, d2d_r, r1_s, r1_r, r2_s, r2_r, fin_l, fin_s, fin_r) = scratch[:23]
        p = _Place()
        c = p.c
        i = pl.program_id(0)
        my_rows = pl.ds(pl.multiple_of(c * HALF_IN, HALF_IN), HALF_IN)

        def layer_of(j):
            return DEPTH - 1 - j // N_ITEMS

        def bufs(j):
            kind, k = ITEMS[j % N_ITEMS]
            if kind == "in":
                return [r.at[k] for r in (mine_in, send_in, sib_in, st_in, r1_in, r2_in, f_in)]
            return [r.at[k] for r in (mine_out, send_out, sib_out, st_out, r1_out, r2_out, f_out)]

        def piece(j, ref, ch):
            if ITEMS[j % N_ITEMS][0] == "in":
                return ref.at[:, ch * HEAD:(ch + 1) * HEAD]
            return ref.at[ch]

        def slot(ch):
            return jnp.where(c == 0, ch % 2, ch // 2)

        def to_sibling(j):
            _, send, sib, _, _, _, _ = bufs(j)
            return pltpu.make_async_remote_copy(send, sib, d2d_s.at[j], d2d_r.at[j], device_id=p.sibling, device_id_type=MESH)

        def first_round(j, ch):
            _, _, _, st, r1, _, _ = bufs(j)
            k = slot(ch)
            return pltpu.make_async_remote_copy(st.at[k], r1.at[k], r1_s.at[2 * j + k], r1_r.at[2 * j + k],
                                                device_id=p.first, device_id_type=MESH)

        def second_round(j):
            _, _, _, st, _, r2, _ = bufs(j)
            return pltpu.make_async_remote_copy(st.at[2], r2, r2_s.at[j], r2_r.at[j], device_id=p.second, device_id_type=MESH)

        def finals(j):
            f = bufs(j)[6]
            kind, k = ITEMS[j % N_ITEMS]
            if kind == "in":
                dst = fin_in.at[layer_of(j), my_rows, k * HEAD:(k + 1) * HEAD]
            else:
                dst = fin_out.at[layer_of(j), c, :, k * OUT_COLS:(k + 1) * OUT_COLS]
            return [pltpu.make_async_copy(f, dst, fin_l.at[j]),
                    pltpu.make_async_remote_copy(f, dst, fin_s.at[j], fin_r.at[j], device_id=p.sibling, device_id_type=MESH)]

        def stage_sum(j):
            mine, _, sib, st, _, _, _ = bufs(j)
            to_sibling(j).wait_recv()
            mine[...] = mine[...] + sib[...]
            for ch in range(N_CHIP):
                @pl.when(p.first_coord(ch) != p.my_first_coord)
                def _(ch=ch):
                    st[slot(ch)] = piece(j, mine, ch)[...].astype(BF16)
                    first_round(j, ch).start()

        def stage_second(j):
            mine, _, _, st, r1, _, _ = bufs(j)
            for ch in range(N_CHIP):
                @pl.when(p.first_coord(ch) == p.my_first_coord)
                def _(ch=ch):
                    first_round(j, ch).wait_recv()
                    part = piece(j, mine, ch)
                    total = part[...] + r1[slot(ch)].astype(F32)
                    part[...] = total

                    @pl.when(ch != p.chip)
                    def _():
                        st[2] = total.astype(BF16)
                        second_round(j).start()

        def stage_final(j):
            mine, _, _, _, _, r2, f = bufs(j)
            second_round(j).wait_recv()
            for ch in range(N_CHIP):
                @pl.when(ch == p.chip)
                def _(ch=ch):
                    f[...] = piece(j, mine, ch)[...] + r2[...].astype(F32)
            for cp in finals(j):
                cp.start()

        def drain(j):
            to_sibling(j).wait_send()
            for ch in range(N_CHIP):
                @pl.when(p.first_coord(ch) != p.my_first_coord)
                def _(ch=ch):
                    first_round(j, ch).wait_send()

                @pl.when(jnp.logical_and(p.first_coord(ch) == p.my_first_coord, ch != p.chip))
                def _():
                    second_round(j).wait_send()
            for cp in finals(j):
                cp.wait()

        dev = p.dev
        devices = p.others()

        def half(core):
            return pl.ds(pl.multiple_of(core * PK_HALF, 16), PK_HALF)

        def finished(core, ch):
            return pl.ds(pl.multiple_of(core * PK_HALF + ch * PK_PIECE, 16), PK_PIECE)

        def small_exchange(l, first_step, bufs_l):
            (pk_mine, pk_sib, pk_st, pk_rs, pk_fin, pk_all, dm_st, dm_all, pk_sem, rs_s, rs_r, ag_s, ag_r, dm_s, dm_r) = bufs_l

            def pk_load():
                return pltpu.make_async_copy(pack_ref.at[l, half(c)], pk_mine, pk_sem.at[0])

            def pk_give():
                return pltpu.make_async_remote_copy(pack_ref.at[l, half(1 - c)], pk_sib, pk_sem.at[1], pk_sem.at[2],
                                                    device_id=p.sibling, device_id_type=MESH)

            def pk_scatter(ch):
                return pltpu.make_async_remote_copy(pk_st.at[ch * PK_PIECE:(ch + 1) * PK_PIECE], pk_rs.at[p.chip],
                                                    rs_s.at[ch], rs_r.at[p.chip], device_id=(ch // 2, ch % 2, c),
                                                    device_id_type=MESH)

            def pk_spread(ch):
                return pltpu.make_async_remote_copy(pk_fin, pk_all.at[finished(c, p.chip)], ag_s.at[ch], ag_r.at[p.chip],
                                                    device_id=(ch // 2, ch % 2, c), device_id_type=MESH)

            def pk_pass():
                return pltpu.make_async_remote_copy(pk_all.at[half(c)], pk_all.at[half(c)], pk_sem.at[3], pk_sem.at[4],
                                                    device_id=p.sibling, device_id_type=MESH)

            def dm_copy(r):
                return pltpu.make_async_remote_copy(dm_st, dm_all.at[:, pl.ds(dev, 1), :], dm_s.at[r], dm_r.at[r],
                                                    device_id=devices[r], device_id_type=MESH)

            def results():
                return [pltpu.make_async_copy(pk_all, pack_out.at[l], pk_sem.at[0]),
                        pltpu.make_async_copy(dm_all, dmod_out.at[l], pk_sem.at[5])]

            @pl.when(i == first_step)
            def _():
                pk_load().start()
                pk_give().start()
                for k in range(3):
                    for r in range(D_MODEL // HEAD):
                        dm_st[8 * k + r] = dmod_ref[l, k:k + 1, r * HEAD:(r + 1) * HEAD]
                dm_st[DM_LOSS] = dmod_ref[l, 3:4, 0:HEAD]
                dm_all[:, pl.ds(dev, 1), :] = dm_st[...]
                for r in range(N_DEV - 1):
                    dm_copy(r).start()

            @pl.when(i == first_step + SMALL_SCATTER_STEP)
            def _():
                pk_load().wait()
                pk_give().wait()
                total = pk_mine[...] + pk_sib[...]
                pk_mine[...] = total
                pk_st[...] = total.astype(BF16)
                for ch in range(N_CHIP):
                    @pl.when(ch != p.chip)
                    def _(ch=ch):
                        pk_scatter(ch).start()

            @pl.when(i == first_step + SMALL_GATHER_STEP)
            def _():
                for ch in range(N_CHIP):
                    @pl.when(ch != p.chip)
                    def _(ch=ch):
                        pltpu.make_async_remote_copy(pk_fin, pk_rs.at[ch], rs_s.at[ch], rs_r.at[ch],
                                                     device_id=p.sibling, device_id_type=MESH).wait_recv()
                for me in range(N_CHIP):
                    @pl.when(me == p.chip)
                    def _(me=me):
                        total = None
                        for ch in range(N_CHIP):
                            part = pk_mine[me * PK_PIECE:(me + 1) * PK_PIECE] if ch == me else pk_rs[ch].astype(F32)
                            total = part if total is None else total + part
                        pk_fin[...] = total.astype(BF16)
                        pk_all[finished(c, me)] = total.astype(BF16)
                for ch in range(N_CHIP):
                    @pl.when(ch != p.chip)
                    def _(ch=ch):
                        pk_spread(ch).start()

            @pl.when(i == first_step + SMALL_PASS_STEP)
            def _():
                for ch in range(N_CHIP):
                    @pl.when(ch != p.chip)
                    def _(ch=ch):
                        pltpu.make_async_remote_copy(pk_fin, pk_all.at[finished(c, ch)], ag_s.at[ch], ag_r.at[ch],
                                                     device_id=p.sibling, device_id_type=MESH).wait_recv()
                pk_pass().start()

            @pl.when(i == first_step + SMALL_FINISH_STEP)
            def _():
                pk_pass().wait()
                for ch in range(N_CHIP):
                    @pl.when(ch != p.chip)
                    def _(ch=ch):
                        pk_scatter(ch).wait_send()
                        pk_spread(ch).wait_send()
                for r in range(N_DEV - 1):
                    dm_copy(r).wait()
                for cp in results():
                    cp.start()
                for cp in results():
                    cp.wait()

        n_small = 15
        for l in range(DEPTH):
            small_exchange(l, (DEPTH - 1 - l) * N_ITEMS, scratch[23 + n_small * l:23 + n_small * (l + 1)])

        for step in range(N_ITEMS, N_STEPS):
            @pl.when(i == step)
            def _(step=step):
                drain(step - N_ITEMS)

        ii = jnp.where(i < N_ITEMS, i, i - N_ITEMS)
        in_step = jnp.logical_and(ii >= OUT_FIRST, ii < OUT_FIRST + IN_STEPS)

        @pl.when(in_step)
        def _():
            k = ii - OUT_FIRST
            rhs = jnp.concatenate([r[...] for r in dp_refs], axis=1)
            res = _dot(h_ref[...], rhs, TN)

            @pl.when(c == 0)
            def _():
                mine_in[k] = res[:HALF_IN]
                send_in[k] = res[HALF_IN:]

            @pl.when(c == 1)
            def _():
                mine_in[k] = res[HALF_IN:]
                send_in[k] = res[:HALF_IN]

        @pl.when(jnp.logical_not(in_step))
        def _():
            k = jnp.where(ii < OUT_FIRST, ii, ii - IN_STEPS)
            res = _dot(cat_ref[...], dy_ref[...], TN)

            @pl.when(c == 0)
            def _():
                for ch in range(N_CHIP):
                    mine_out[k, ch] = res[ch * W_OUT_ROWS:ch * W_OUT_ROWS + HALF_OUT]
                    send_out[k, ch] = res[ch * W_OUT_ROWS + HALF_OUT:(ch + 1) * W_OUT_ROWS]

            @pl.when(c == 1)
            def _():
                for ch in range(N_CHIP):
                    mine_out[k, ch] = res[ch * W_OUT_ROWS + HALF_OUT:(ch + 1) * W_OUT_ROWS]
                    send_out[k, ch] = res[ch * W_OUT_ROWS:ch * W_OUT_ROWS + HALF_OUT]

        stages = ((0, lambda j: to_sibling(j).start()), (DELAY_SUM, stage_sum), (DELAY_SECOND, stage_second),
                  (DELAY_FINAL, stage_final))
        for step in range(N_STEPS):
            @pl.when(i == step)
            def _(step=step):
                for delay, stage in stages:
                    if step - delay >= 0:
                        stage(step - delay)

        @pl.when(i == N_STEPS - 1)
        def _():
            for step in range(N_STEPS, N_STEPS + DELAY_FINAL):
                for delay, stage in stages:
                    if 0 <= step - delay < N_STEPS:
                        stage(step - delay)
            for j in range(N_STEPS - N_ITEMS, N_STEPS):
                drain(j)

    hbm = pl.BlockSpec(memory_space=pl.ANY)

    def layer(i):
        return jnp.where(i < N_ITEMS, DEPTH - 1, 0)

    def item(i):
        return jnp.where(i < N_ITEMS, i, i - N_ITEMS)

    def whole(i):
        return (layer(i), 0, 0)

    def dproj_piece(ch):
        return pl.BlockSpec((None, SEQ, HEAD),
                            lambda i: (layer(i), 0, ch * IN_STEPS + jnp.clip(item(i) - OUT_FIRST, 0, IN_STEPS - 1)))

    def dy_quarter(i):
        return (layer(i), 0, jnp.where(item(i) < OUT_FIRST, item(i), jnp.maximum(item(i) - IN_STEPS, OUT_FIRST)))

    operand = pl.BlockSpec((None, SEQ, D_MODEL), whole)
    in_specs = [operand] + [dproj_piece(ch) for ch in range(N_CHIP)]
    in_specs += [operand, pl.BlockSpec((None, SEQ, OUT_COLS), dy_quarter), hbm, _const_in((DEPTH, 8, D_MODEL))]
    args = [h, dproj, dproj, dproj, dproj, cat, dy, pack, dmod]
    out_shape = [jax.ShapeDtypeStruct((DEPTH, D_MODEL, W_IN_COLS), F32), jax.ShapeDtypeStruct((DEPTH, 2, HALF_OUT, D_MODEL), F32),
                 jax.ShapeDtypeStruct((DEPTH, PK_ROWS, HEAD), BF16), jax.ShapeDtypeStruct((DEPTH, DM_ROWS, N_DEV, HEAD), F32)]
    out_specs = [hbm, hbm, hbm, hbm]
    in_item = lambda *lead: pltpu.VMEM(lead + (HALF_IN, HEAD), BF16)
    out_item = lambda *lead: pltpu.VMEM(lead + (HALF_OUT, OUT_COLS), BF16)
    scratch = [
        pltpu.VMEM((IN_STEPS, HALF_IN, N_CHIP * HEAD), F32), pltpu.VMEM((IN_STEPS, HALF_IN, N_CHIP * HEAD), F32),
        pltpu.VMEM((IN_STEPS, HALF_IN, N_CHIP * HEAD), F32), in_item(IN_STEPS, 3), in_item(IN_STEPS, 2), in_item(IN_STEPS),
        pltpu.VMEM((IN_STEPS, HALF_IN, HEAD), F32),
        pltpu.VMEM((OUT_STEPS, N_CHIP, HALF_OUT, OUT_COLS), F32), pltpu.VMEM((OUT_STEPS, N_CHIP, HALF_OUT, OUT_COLS), F32),
        pltpu.VMEM((OUT_STEPS, N_CHIP, HALF_OUT, OUT_COLS), F32), out_item(OUT_STEPS, 3), out_item(OUT_STEPS, 2),
        out_item(OUT_STEPS), pltpu.VMEM((OUT_STEPS, HALF_OUT, OUT_COLS), F32),
        pltpu.SemaphoreType.DMA((N_STEPS,)), pltpu.SemaphoreType.DMA((N_STEPS,)),
        pltpu.SemaphoreType.DMA((2 * N_STEPS,)), pltpu.SemaphoreType.DMA((2 * N_STEPS,)),
        pltpu.SemaphoreType.DMA((N_STEPS,)), pltpu.SemaphoreType.DMA((N_STEPS,)),
        pltpu.SemaphoreType.DMA((N_STEPS,)), pltpu.SemaphoreType.DMA((N_STEPS,)), pltpu.SemaphoreType.DMA((N_STEPS,)),
    ]
    for _ in range(DEPTH):
        scratch += [
            pltpu.VMEM((PK_HALF, HEAD), F32), pltpu.VMEM((PK_HALF, HEAD), F32), pltpu.VMEM((PK_HALF, HEAD), BF16),
            pltpu.VMEM((N_CHIP, PK_PIECE, HEAD), BF16), pltpu.VMEM((PK_PIECE, HEAD), BF16), pltpu.VMEM((PK_ROWS, HEAD), BF16),
            pltpu.VMEM((DM_ROWS, 1, HEAD), F32), pltpu.VMEM((DM_ROWS, N_DEV, HEAD), F32),
            pltpu.SemaphoreType.DMA((6,)),
            pltpu.SemaphoreType.DMA((N_CHIP,)), pltpu.SemaphoreType.DMA((N_CHIP,)),
            pltpu.SemaphoreType.DMA((N_CHIP,)), pltpu.SemaphoreType.DMA((N_CHIP,)),
            pltpu.SemaphoreType.DMA((N_DEV - 1,)), pltpu.SemaphoreType.DMA((N_DEV - 1,)),
        ]
    return pl.pallas_call(
        body, name="wgrad", grid=(N_STEPS,), in_specs=in_specs, out_specs=out_specs, out_shape=out_shape,
        scratch_shapes=scratch,
        compiler_params=pltpu.CompilerParams(dimension_semantics=("arbitrary",), vmem_limit_bytes=VMEM_LIMIT),
    )(*args)


def _adamw(w, g, m, v):
    m = ADAM_B1 * m + (1.0 - ADAM_B1) * g
    v = ADAM_B2 * v + (1.0 - ADAM_B2) * (g * g)
    m_hat = m / (1.0 - ADAM_B1 ** ADAM_STEP)
    v_hat = v / (1.0 - ADAM_B2 ** ADAM_STEP)
    delta = -ADAM_LR * (m_hat / (jnp.sqrt(v_hat) + ADAM_EPS) + ADAM_WD * w)
    return delta, m, v


def _adam_sharded(c_all, dmods, ada, w_in_set, w_out_set):
    rows = D_MODEL // ADAM_PARTS

    def body(c_ref, dm_ref, wa_ref, ma_ref, va_ref, wi_ref, gi_ref, mi_ref, vi_ref, wo_ref, go_ref, mo_ref, vo_ref,
             ga_out, da_out, ma_out, va_out, gi_out, di_out, mi_out, vi_out, go_out, do_out, mo_out, vo_out):
        l = pl.program_id(0)
        chip = 2 * lax.axis_index("x") + lax.axis_index("y")
        cv = c_ref[...]
        silu_c = (cv * _sigmoid(cv)).astype(BF16).astype(F32)
        pieces = []
        for k in range(W_ADA_COLS // HEAD):
            dk = dm_ref[l, (W_ADA_COLS // HEAD) * chip + k].astype(BF16).astype(F32)
            pieces.append(_dot_exact(silu_c, dk, TN))
        g = jnp.concatenate(pieces, axis=1)
        ga_out[...] = g
        da_out[...], ma_out[...], va_out[...] = _adamw(wa_ref[...], g, ma_ref[...], va_ref[...])
        gi_out[...] = gi_ref[...]
        go_out[...] = go_ref[...]
        di_out[...], mi_out[...], vi_out[...] = _adamw(wi_ref[...], gi_ref[...], mi_ref[...], vi_ref[...])
        do_out[...], mo_out[...], vo_out[...] = _adamw(wo_ref[...], go_ref[...], mo_ref[...], vo_ref[...])

    def blk(r, cols):
        return pl.BlockSpec((None, r, cols), lambda l, i: (l, i, 0))

    b_ada, b_in, b_out = blk(rows, W_ADA_COLS), blk(rows, W_IN_COLS), blk(W_OUT_ROWS // ADAM_PARTS, D_MODEL)
    shapes = [jax.ShapeDtypeStruct(a[0].shape, F32) for a in (ada, w_in_set, w_out_set)]
    return pl.pallas_call(
        body, name="adam_sharded", grid=(DEPTH, ADAM_PARTS),
        in_specs=[pl.BlockSpec((N_DEV, rows), lambda l, i: (0, i)), _const_in((DEPTH, DM_ROWS, N_DEV, HEAD))]
        + [b_ada] * 3 + [b_in] * 4 + [b_out] * 4,
        out_specs=[b_ada] * 4 + [b_in] * 4 + [b_out] * 4,
        out_shape=[shapes[0]] * 4 + [shapes[1]] * 4 + [shapes[2]] * 4,
        compiler_params=pltpu.CompilerParams(dimension_semantics=("arbitrary", "arbitrary"), vmem_limit_bytes=VMEM_LIMIT),
    )(c_all, dmods, *ada, *w_in_set, *w_out_set)


def _adam_small(packs, dmods, weights, ms, vs):
    n = len(weights)

    def body(*refs):
        dm_refs = refs[1]
        b = 2
        w_refs, m_refs, v_refs = refs[b:b + n], refs[b + n:b + 2 * n], refs[b + 2 * n:b + 3 * n]
        outs = refs[b + 3 * n:b + 3 * n + 4 * n + 1]
        pack_refs = refs[-1]
        pack_refs[...] = refs[0][...].astype(F32)
        g_refs, d_refs, nm_refs, nv_refs = outs[0:n], outs[n:2 * n], outs[2 * n:3 * n], outs[3 * n:4 * n]
        squares = dm_refs[DEPTH - 1, DM_LOSS]
        total = squares[0:1, 0:1]
        for d in range(1, N_DEV):
            total = total + squares[d:d + 1, 0:1]
        outs[4 * n][...] = total * (0.5 / D_MODEL)

        def lanes(l, row0, count):
            return jnp.concatenate([pack_refs.at[l][row0 + k:row0 + k + 1, :] for k in range(count)], axis=1)

        def update(idx, at, g):
            g_refs[idx][at] = g
            d_refs[idx][at], nm_refs[idx][at], nv_refs[idx][at] = _adamw(w_refs[idx][at], g, m_refs[idx][at], v_refs[idx][at])

        for l in range(DEPTH):
            row = (slice(l, l + 1), slice(None))
            g_b = None
            for d in range(N_DEV):
                part = dm_refs.at[l][0:DM_LOSS, d, :]
                g_b = part if g_b is None else g_b + part
            update(0, row, jnp.concatenate([g_b[k:k + 1, :] for k in range(DM_LOSS)], axis=1))
            for g in range(N_HEAD):
                update(1, (l, g), pack_refs.at[l][PK_W_POOL + g * HEAD:PK_W_POOL + (g + 1) * HEAD, :])
                update(5, (l, g), pack_refs.at[l][PK_W_SGU + g * HEAD:PK_W_SGU + (g + 1) * HEAD, :])
            update(2, row, lanes(l, PK_POOL_SCALE, N_HEAD))
            update(3, (l,), pack_refs.at[l][PK_SGU_LN_G:PK_SGU_LN_G + N_HEAD, :])
            update(4, (l,), pack_refs.at[l][PK_SGU_LN_B:PK_SGU_LN_B + N_HEAD, :])
            update(6, (l,), pack_refs.at[l][PK_B_SGU:PK_B_SGU + N_HEAD, :])
            update(7, row, lanes(l, PK_LN_G, D_MODEL // HEAD))
            update(8, row, lanes(l, PK_LN_B, D_MODEL // HEAD))

    vmem = pl.BlockSpec(memory_space=pltpu.VMEM)
    shapes = [jax.ShapeDtypeStruct(w.shape, F32) for w in weights]
    return pl.pallas_call(
        body, name="adam_small", in_specs=[vmem] * (2 + 3 * n), out_specs=[vmem] * (4 * n + 1),
        out_shape=shapes * 4 + [jax.ShapeDtypeStruct((1, 1), F32)],
        scratch_shapes=[pltpu.VMEM(packs.shape, F32)],
        compiler_params=pltpu.CompilerParams(vmem_limit_bytes=VMEM_LIMIT),
    )(packs, dmods, *weights, *ms, *vs)


def kernel(x, c, w_ada, b_ada, w_in, w_pool, pool_scale, sgu_ln_g, sgu_ln_b, w_sgu, b_sgu, w_out, ln_g, ln_b, loss_target, m_w_ada, m_b_ada, m_w_in, m_w_pool, m_pool_scale, m_sgu_ln_g, m_sgu_ln_b, m_w_sgu, m_b_sgu, m_w_out, m_ln_g, m_ln_b, v_w_ada, v_b_ada, v_w_in, v_w_pool, v_pool_scale, v_sgu_ln_g, v_sgu_ln_b, v_w_sgu, v_b_sgu, v_w_out, v_ln_g, v_ln_b):
    small = (w_pool, pool_scale, sgu_ln_g, sgu_ln_b, w_sgu, b_sgu)
    *saved0, w_in0, w_out0, w_in1, w_out1, mod, c_all = _forward_first(x, c, w_ada, b_ada, w_in, w_out, small, ln_g, ln_b)
    *saved1, dout, sq = _forward_last(saved0[3], mod, w_in1, w_out1, small, ln_g, ln_b, loss_target)

    dx1, *shared = _backward_layer(1, dout, saved1, mod, w_in1, w_out1, small, ln_g, sq=sq)
    dx0, h, cat, dy, dproj, pack, dmod = _backward_layer(0, dx1, saved0, mod, w_in0, w_out0, small, ln_g, shared=shared)
    g_in, g_out, pack, dmods = _wgrad_reduce(h, dproj, cat, dy, pack, dmod)

    g_out = g_out.reshape(DEPTH, W_OUT_ROWS, D_MODEL)
    big = _adam_sharded(c_all, dmods, (w_ada, m_w_ada, v_w_ada), (w_in, g_in, m_w_in, v_w_in), (w_out, g_out, m_w_out, v_w_out))
    ada, win, wout = big[0:4], big[4:8], big[8:12]
    small_w = (b_ada, w_pool, pool_scale, sgu_ln_g, sgu_ln_b, w_sgu, b_sgu, ln_g, ln_b)
    small_m = (m_b_ada, m_w_pool, m_pool_scale, m_sgu_ln_g, m_sgu_ln_b, m_w_sgu, m_b_sgu, m_ln_g, m_ln_b)
    small_v = (v_b_ada, v_w_pool, v_pool_scale, v_sgu_ln_g, v_sgu_ln_b, v_w_sgu, v_b_sgu, v_ln_g, v_ln_b)
    res = _adam_small(pack, dmods, small_w, small_m, small_v)
    n = len(small_w)
    loss = res[4 * n].reshape(())

    def ordered(k):
        s = res[k * n:(k + 1) * n]
        return (ada[k], s[0], win[k], s[1], s[2], s[3], s[4], s[5], s[6], wout[k], s[7], s[8])

    return (loss, dx0[None], *ordered(0), *ordered(1), *ordered(2), *ordered(3))
```
